```python
import jax, jax.numpy as jnp
from jax import lax
import numpy as np

D_MODEL = 1024
BATCH = 4
SEQ = 4096
DEPTH = 1
DEC_BATCH = 128
DEC_SEQ = 8
PAST_LEN = 8192
PAGE_SIZE = 128

N_META = 16
HEAD_DIM = 64
N_HEADS = D_MODEL // 128
N_KV = N_HEADS // 4
Q_PER_KV = N_HEADS // N_KV
WINDOW = 128
ATT_BLOCK = 128
GLA_HEADS = 4
GLA_DK = D_MODEL // 2
GLA_DV = D_MODEL
GLA_HK = GLA_DK // GLA_HEADS
GLA_HV = GLA_DV // GLA_HEADS
GLA_RANK = 16
GLA_TAU = 16.0
GLA_CHUNK = 64
N_GROUPS = 4
EXP_PER_GROUP = 8
N_EXPERTS = N_GROUPS * EXP_PER_GROUP
TOP_K = 2
D_EXPERT = D_MODEL // 4
DN_ALPHA = (2.0 * DEPTH) ** 0.25
DN_BETA = (8.0 * DEPTH) ** -0.25
EPS = 1e-5
IN_WIDTHS = (N_HEADS * HEAD_DIM, N_KV * HEAD_DIM, N_KV * HEAD_DIM, GLA_DK, GLA_DK, GLA_DV, GLA_RANK, GLA_DV, 2 * D_MODEL)

kernel_name = "hybrid_swa_gla_hmoe_decoder_step"


def layer_norm(x, g, b):
    xf = x.astype(jnp.float32)
    mu = jnp.mean(xf, -1, keepdims=True)
    var = jnp.mean(jnp.square(xf - mu), -1, keepdims=True)
    return ((xf - mu) * lax.rsqrt(var + EPS) * g + b).astype(x.dtype)


def alibi_slopes():
    return jnp.exp2(-8.0 * jnp.arange(1, N_HEADS + 1, dtype=jnp.float32) / N_HEADS)


def split_points():
    pts, acc = [], 0
    for w in IN_WIDTHS[:-1]:
        acc += w
        pts.append(acc)
    return pts


def front_pad(a, n):
    return jnp.pad(a, ((0, 0), (n, 0)) + ((0, 0),) * (a.ndim - 2))


def mixer_inputs(u, w_in_l, w_alpha2_l, b_alpha_l):
    z = u @ w_in_l
    q, k, v, gq, gk, gv, ga, gr, gate_pre = jnp.split(z, split_points(), axis=-1)
    lead = u.shape[:-1]
    q = q.reshape(lead + (N_KV, Q_PER_KV, HEAD_DIM))
    k = k.reshape(lead + (N_KV, HEAD_DIM))
    v = v.reshape(lead + (N_KV, HEAD_DIM))
    gq = gq.reshape(lead + (GLA_HEADS, GLA_HK)) * GLA_HK ** -0.5
    gk = gk.reshape(lead + (GLA_HEADS, GLA_HK))
    gv = gv.reshape(lead + (GLA_HEADS, GLA_HV))
    log_a = jax.nn.log_sigmoid((ga @ w_alpha2_l + b_alpha_l).astype(jnp.float32)) / GLA_TAU
    log_a = log_a.reshape(lead + (GLA_HEADS, GLA_HK))
    return q, k, v, gq, gk, gv, log_a, gr, gate_pre


def sink_attention(q, k, v, q_pos, k_pos, sink):
    s = jnp.einsum('...qkgd,...skd->...kgqs', q, k).astype(jnp.float32) * HEAD_DIM ** -0.5
    dist = q_pos[..., :, None] - k_pos[..., None, :]
    mask = (dist >= 0) & (dist < WINDOW) & (k_pos[..., None, :] >= 0)
    slopes = alibi_slopes().reshape(N_KV, Q_PER_KV)[:, :, None, None]
    s = s - slopes * dist.astype(jnp.float32)[..., None, None, :, :]
    s = jnp.where(mask[..., None, None, :, :], s, -jnp.inf)
    sk = jnp.broadcast_to(sink.astype(jnp.float32).reshape(N_KV, Q_PER_KV, 1, 1), s.shape[:-1] + (1,))
    p = jax.nn.softmax(jnp.concatenate([s, sk], -1), axis=-1)[..., :-1]
    return jnp.einsum('...kgqs,...skd->...qkgd', p.astype(v.dtype), v)


def swa_prompt(q, k, v, sink):
    B, L = q.shape[:2]
    pad = (-N_META) % ATT_BLOCK
    Lp = L + pad
    nb = Lp // ATT_BLOCK
    qb = front_pad(q, pad).reshape(B, nb, ATT_BLOCK, N_KV, Q_PER_KV, HEAD_DIM)
    kb = front_pad(k, pad).reshape(B, nb, ATT_BLOCK, N_KV, HEAD_DIM)
    vb = front_pad(v, pad).reshape(B, nb, ATT_BLOCK, N_KV, HEAD_DIM)
    prev = lambda a: jnp.concatenate([jnp.zeros_like(a[:, :1]), a[:, :-1]], 1)
    k_band = jnp.concatenate([prev(kb), kb], 2)
    v_band = jnp.concatenate([prev(vb), vb], 2)
    pos = (jnp.arange(Lp) - pad).reshape(nb, ATT_BLOCK)
    k_pos = jnp.concatenate([pos - ATT_BLOCK, pos], 1)
    o = sink_attention(qb, k_band, v_band, pos, k_pos, sink)
    return o.reshape(B, Lp, N_HEADS * HEAD_DIM)[:, pad:]


def swa_sample(q, k, v, buf_k, buf_v, sink):
    Bd, T = q.shape[:2]
    nbuf = buf_k.shape[1]
    k_all = jnp.concatenate([buf_k.astype(k.dtype), k], 1)
    v_all = jnp.concatenate([buf_v.astype(v.dtype), v], 1)
    k_pos = PAST_LEN - nbuf + jnp.arange(nbuf + T)
    q_pos = PAST_LEN + jnp.arange(T)
    o = sink_attention(q, k_all, v_all, q_pos, k_pos, sink)
    return o.reshape(Bd, T, N_HEADS * HEAD_DIM), k_all[:, -nbuf:], v_all[:, -nbuf:]


def gla_scan(q, k, v, log_a, s0, chunk):
    B, T, H, _ = q.shape
    n = T // chunk
    to_chunks = lambda a: a.astype(jnp.float32).reshape(B, n, chunk, H, a.shape[-1]).transpose(1, 0, 3, 2, 4)
    xs = tuple(to_chunks(a) for a in (q, k, v, log_a))
    causal = jnp.tril(jnp.ones((chunk, chunk), bool))[..., None]

    def step(S, inp):
        qi, ki, vi, ai = inp
        b = jnp.cumsum(ai, axis=-2)
        o_inter = jnp.einsum('bhck,bhkv->bhcv', qi * jnp.exp(b), S)
        decay = jnp.exp(jnp.where(causal, b[..., :, None, :] - b[..., None, :, :], -jnp.inf))
        att = jnp.einsum('bhtk,bhtsk,bhsk->bhts', qi, decay, ki)
        o = o_inter + jnp.einsum('bhts,bhsv->bhtv', att, vi)
        b_last = b[..., -1:, :]
        S_new = S * jnp.exp(b_last[..., 0, :])[..., None] + jnp.einsum('bhck,bhcv->bhkv', ki * jnp.exp(b_last - b), vi)
        return S_new, o

    S_fin, ys = lax.scan(step, s0, xs)
    out = ys.transpose(1, 0, 3, 2, 4).reshape(B, T, H, v.shape[-1])
    return out, S_fin


def gla_output(o, gr, norm_g):
    of = o.astype(jnp.float32)
    of = of * lax.rsqrt(jnp.mean(of * of, -1, keepdims=True) + EPS)
    of = of.reshape(o.shape[:-2] + (GLA_DV,)) * norm_g
    return of.astype(gr.dtype) * jax.nn.silu(gr)


def merge_branches(y_attn, y_gla, gate_pre, b_gate_l, w_attn_br_l, w_gla_br_l, w_out_l):
    g_attn, g_gla = jnp.split(jax.nn.sigmoid(gate_pre + b_gate_l), 2, axis=-1)
    h = g_attn * (y_attn @ w_attn_br_l) + g_gla * (y_gla @ w_gla_br_l)
    return h @ w_out_l


def hier_moe(u, w_rg, b_rg, w_re, b_re, w_g, w_u, w_d):
    shp = u.shape
    t = u.reshape(-1, D_MODEL)
    g_logits = (t @ w_rg + b_rg).astype(jnp.float32)
    g_prob = jax.nn.softmax(g_logits, axis=-1)
    g_idx = jnp.argmax(g_logits, axis=-1)
    e_logits = (t @ w_re + b_re).astype(jnp.float32).reshape(-1, N_GROUPS, EXP_PER_GROUP)
    e_in = jnp.einsum('ng,nge->ne', jax.nn.one_hot(g_idx, N_GROUPS, dtype=jnp.float32), e_logits)
    top_v, top_i = lax.top_k(e_in, TOP_K)
    w = jax.nn.softmax(top_v, axis=-1) * jnp.max(g_prob, axis=-1, keepdims=True)
    e_glob = g_idx[:, None] * EXP_PER_GROUP + top_i
    combine = jnp.sum(jax.nn.one_hot(e_glob, N_EXPERTS, dtype=jnp.float32) * w[..., None], axis=1)
    y = jnp.zeros(t.shape, jnp.float32)
    for e in range(N_EXPERTS):
        he = jax.nn.silu(t @ w_g[e]) * (t @ w_u[e])
        y = y + combine[:, e:e + 1] * (he @ w_d[e])
    return y.astype(u.dtype).reshape(shp)


def setup_inputs(seed: int = 0) -> dict:
    key = jax.random.key(seed)
    ks = iter(jax.random.split(key, 40))
    f32 = jnp.float32

    def nrm(shape, scale):
        return jax.random.normal(next(ks), shape, f32) * scale

    n_buf = min(WINDOW, PAST_LEN)
    d_in = sum(IN_WIDTHS)
    return {
        "x_prompt": nrm((BATCH, SEQ, D_MODEL), 1.0),
        "x_sample": nrm((DEC_BATCH, DEC_SEQ, D_MODEL), 1.0),
        "state_swa_k": nrm((DEPTH, DEC_BATCH, n_buf, N_KV, HEAD_DIM), 1.0),
        "state_swa_v": nrm((DEPTH, DEC_BATCH, n_buf, N_KV, HEAD_DIM), 1.0),
        "state_gla": nrm((DEPTH, DEC_BATCH, GLA_HEADS, GLA_HK, GLA_HV), 0.3),
        "meta_tokens": nrm((N_META, D_MODEL), 1.0),
        "ln_emb_g": 1.0 + nrm((D_MODEL,), 0.02),
        "ln_emb_b": nrm((D_MODEL,), 0.02),
        "w_in": nrm((DEPTH, D_MODEL, d_in), D_MODEL ** -0.5),
        "b_gate": nrm((DEPTH, 2 * D_MODEL), 0.02),
        "attn_sink": nrm((DEPTH, N_HEADS), 0.5),
        "w_alpha2": nrm((DEPTH, GLA_RANK, GLA_DK), GLA_RANK ** -0.5),
        "b_alpha": nrm((DEPTH, GLA_DK), 0.02),
        "gla_norm_g": 1.0 + nrm((DEPTH, GLA_DV), 0.02),
        "w_attn_br": nrm((DEPTH, N_HEADS * HEAD_DIM, D_MODEL), (N_HEADS * HEAD_DIM) ** -0.5),
        "w_gla_br": nrm((DEPTH, GLA_DV, D_MODEL), GLA_DV ** -0.5),
        "w_out": nrm((DEPTH, D_MODEL, D_MODEL), D_MODEL ** -0.5 * DN_BETA),
        "ln1_g": 1.0 + nrm((DEPTH, D_MODEL), 0.02),
        "ln1_b": nrm((DEPTH, D_MODEL), 0.02),
        "w_router_group": nrm((DEPTH, D_MODEL, N_GROUPS), D_MODEL ** -0.5),
        "b_router_group": nrm((DEPTH, N_GROUPS), 0.01),
        "w_router_expert": nrm((DEPTH, D_MODEL, N_EXPERTS), D_MODEL ** -0.5),
        "b_router_expert": nrm((DEPTH, N_EXPERTS), 0.01),
        "w_exp_gate": nrm((DEPTH, N_EXPERTS, D_MODEL, D_EXPERT), D_MODEL ** -0.5),
        "w_exp_up": nrm((DEPTH, N_EXPERTS, D_MODEL, D_EXPERT), D_MODEL ** -0.5),
        "w_exp_down": nrm((DEPTH, N_EXPERTS, D_EXPERT, D_MODEL), D_EXPERT ** -0.5 * DN_BETA),
        "ln2_g": 1.0 + nrm((DEPTH, D_MODEL), 0.02),
        "ln2_b": nrm((DEPTH, D_MODEL), 0.02),
    }


def reference(x_prompt, x_sample, state_swa_k, state_swa_v, state_gla,
              meta_tokens, ln_emb_g, ln_emb_b, w_in, b_gate, attn_sink,
              w_alpha2, b_alpha, gla_norm_g, w_attn_br, w_gla_br, w_out,
              ln1_g, ln1_b, w_router_group, b_router_group, w_router_expert,
              b_router_expert, w_exp_gate, w_exp_up, w_exp_down, ln2_g, ln2_b):
    B = x_prompt.shape[0]
    T_s = x_sample.shape[1]
    meta = jnp.broadcast_to(meta_tokens[None].astype(x_prompt.dtype), (B, N_META, D_MODEL))
    hp = layer_norm(jnp.concatenate([meta, x_prompt], axis=1), ln_emb_g, ln_emb_b)
    hs = layer_norm(x_sample, ln_emb_g, ln_emb_b)
    pad_g = (-N_META) % GLA_CHUNK
    kp_l, vp_l, sp_l, ks_l, vs_l, ss_l = [], [], [], [], [], []
    for l in range(DEPTH):
        q, k, v, gq, gk, gv, log_a, gr, gate_pre = mixer_inputs(hp, w_in[l], w_alpha2[l], b_alpha[l])
        y_attn = swa_prompt(q, k, v, attn_sink[l])
        s0 = jnp.zeros((B, GLA_HEADS, GLA_HK, GLA_HV), jnp.float32)
        og, s_p = gla_scan(front_pad(gq, pad_g), front_pad(gk, pad_g), front_pad(gv, pad_g),
                           front_pad(log_a, pad_g), s0, GLA_CHUNK)
        y_gla = gla_output(og[:, pad_g:], gr, gla_norm_g[l])
        mix = merge_branches(y_attn, y_gla, gate_pre, b_gate[l], w_attn_br[l], w_gla_br[l], w_out[l])
        hp = layer_norm(DN_ALPHA * hp + mix, ln1_g[l], ln1_b[l])
        if l == DEPTH - 1:
            hp = hp[:, N_META:]
        ff = hier_moe(hp, w_router_group[l], b_router_group[l], w_router_expert[l], b_router_expert[l],
                      w_exp_gate[l], w_exp_up[l], w_exp_down[l])
        hp = layer_norm(DN_ALPHA * hp + ff, ln2_g[l], ln2_b[l])
        kp_l.append(k[:, -WINDOW:].astype(state_swa_k.dtype))
        vp_l.append(v[:, -WINDOW:].astype(state_swa_v.dtype))
        sp_l.append(s_p.astype(state_gla.dtype))
        q, k, v, gq, gk, gv, log_a, gr, gate_pre = mixer_inputs(hs, w_in[l], w_alpha2[l], b_alpha[l])
        y_attn, nk, nv = swa_sample(q, k, v, state_swa_k[l], state_swa_v[l], attn_sink[l])
        og, s_s = gla_scan(gq, gk, gv, log_a, state_gla[l].astype(jnp.float32), T_s)
        y_gla = gla_output(og, gr, gla_norm_g[l])
        mix = merge_branches(y_attn, y_gla, gate_pre, b_gate[l], w_attn_br[l], w_gla_br[l], w_out[l])
        hs = layer_norm(DN_ALPHA * hs + mix, ln1_g[l], ln1_b[l])
        ff = hier_moe(hs, w_router_group[l], b_router_group[l], w_router_expert[l], b_router_expert[l],
                      w_exp_gate[l], w_exp_up[l], w_exp_down[l])
        hs = layer_norm(DN_ALPHA * hs + ff, ln2_g[l], ln2_b[l])
        ks_l.append(nk.astype(state_swa_k.dtype))
        vs_l.append(nv.astype(state_swa_v.dtype))
        ss_l.append(s_s.astype(state_gla.dtype))
    y_prompt = hp
    y_sample = hs
    new_swa_k_prompt = jnp.stack(kp_l, 0)
    new_swa_v_prompt = jnp.stack(vp_l, 0)
    new_gla_prompt = jnp.stack(sp_l, 0)
    new_swa_k_sample = jnp.stack(ks_l, 0)
    new_swa_v_sample = jnp.stack(vs_l, 0)
    new_gla_sample = jnp.stack(ss_l, 0)
    return (y_prompt, y_sample, new_swa_k_prompt, new_swa_v_prompt, new_gla_prompt, new_swa_k_sample, new_swa_v_sample, new_gla_sample)
```

```python
import functools

import numpy as np
import jax
import jax.numpy as jnp
from jax import lax
from jax.experimental import pallas as pl
from jax.experimental.pallas import tpu as pltpu

F32 = jnp.float32
BF16 = jnp.bfloat16

D_MODEL = 1024
N_META = 16
HEAD_DIM = 64
N_HEADS = 8
N_KV = 2
Q_PER_KV = 4
WINDOW = 128
ATT_BLOCK = 128
GLA_HEADS = 4
GLA_HK = 128
GLA_HV = 256
GLA_DK = GLA_HEADS * GLA_HK
GLA_DV = GLA_HEADS * GLA_HV
GLA_RANK = 16
GLA_TAU = 16.0
GLA_CHUNK = 64
N_GROUPS = 4
EXP_PER_GROUP = 8
N_EXPERTS = 32
D_EXPERT = 256
DN_ALPHA = 2.0 ** 0.25
EPS = 1e-5
NEG = -1e30

FRONT_PAD = (-N_META) % ATT_BLOCK
SKIP_ROWS = FRONT_PAD + N_META

Q_W, KV_W = N_HEADS * HEAD_DIM, N_KV * HEAD_DIM
SEG = {}
_o = 0
for _n, _w in (("q", Q_W), ("k", KV_W), ("v", KV_W), ("gq", GLA_DK), ("gk", GLA_DK), ("gv", GLA_DV),
               ("gr", GLA_DV), ("gate", 2 * D_MODEL), ("ga", 128)):
    SEG[_n] = (_o, _o + _w)
    _o += _w
W_IN_COLS = _o

ROW_TILE = 256
FFN_TILE = 256
CMB_TILE = 128
SAMPLE_SEQS = 8
VMEM_LIMIT = 56 * 1024 * 1024


def _ln(x, g, b):
    mu = jnp.mean(x, -1, keepdims=True)
    xc = x - mu
    var = jnp.mean(xc * xc, -1, keepdims=True)
    return xc * lax.rsqrt(var + EPS) * g + b


def _sigmoid(x):
    return 1.0 / (1.0 + jnp.exp(-x))


def _dot(a, b):
    return jnp.dot(a, b, preferred_element_type=F32)


def _dot_nt(a, b):
    return lax.dot_general(a, b, (((1,), (1,)), ((), ())), preferred_element_type=F32)


def _inproj_kernel(x_ref, keep_ref, g_ref, b_ref, w_ref, wa2_ref, ba_ref,
                   q_ref, k_ref, v_ref, gq_ref, gk_ref, gv_ref, la_ref, gr_ref, gate_ref):
    hb = _ln(x_ref[...], g_ref[...], b_ref[...]).astype(BF16)
    keep = keep_ref[...]

    def seg(name):
        a, b = SEG[name]
        return _dot(hb, w_ref[:, a:b])

    q_ref[...] = seg("q")
    k_ref[...] = seg("k")
    v_ref[...] = seg("v")
    gq_ref[...] = seg("gq") * (GLA_HK ** -0.5)
    gk_ref[...] = seg("gk") * keep
    gv_ref[...] = seg("gv") * keep
    gr_ref[...] = seg("gr")
    gate_ref[...] = seg("gate")
    z = _dot(seg("ga").astype(BF16), wa2_ref[...]) + ba_ref[...]
    la = (jnp.minimum(z, 0.0) - jnp.log(1.0 + jnp.exp(-jnp.abs(z)))) * (1.0 / GLA_TAU)
    la_ref[...] = la * keep


def _inproj(x_all, keep, ln_g, ln_b, w_bf, wa2_bf, b_alpha):
    n = x_all.shape[0]
    tm = ROW_TILE
    widths = [Q_W, KV_W, KV_W, GLA_DK, GLA_DK, GLA_DV, GLA_DK, GLA_DV, 2 * D_MODEL]
    row = lambda w: pl.BlockSpec((tm, w), lambda i: (i, 0))
    full = lambda a: pl.BlockSpec(a.shape, lambda i: (0,) * a.ndim)
    return pl.pallas_call(
        _inproj_kernel,
        grid=(n // tm,),
        in_specs=[row(D_MODEL), row(1), full(ln_g), full(ln_b), full(w_bf), full(wa2_bf), full(b_alpha)],
        out_specs=[row(w) for w in widths],
        out_shape=[jax.ShapeDtypeStruct((n, w), F32) for w in widths],
        compiler_params=pltpu.CompilerParams(dimension_semantics=("parallel",), vmem_limit_bytes=VMEM_LIMIT),
        name="inproj",
    )(x_all, keep, ln_g, ln_b, w_bf, wa2_bf, b_alpha)


def _softmax_pv(s, sink, vv):
    m = jnp.maximum(jnp.max(s, -1, keepdims=True), sink)
    p = jnp.exp(s - m)
    l = jnp.sum(p, -1, keepdims=True) + jnp.exp(sink - m)
    return _dot(p.astype(BF16), vv) / l


def _swa_prompt_kernel(sink_ref, q_ref, kp_ref, kc_ref, vp_ref, vc_ref, o_ref):
    j = pl.program_id(1)
    q = q_ref[...]
    kb = jnp.concatenate([kp_ref[...], kc_ref[...]], 0)
    vb = jnp.concatenate([vp_ref[...], vc_ref[...]], 0)
    r = lax.broadcasted_iota(jnp.int32, (ATT_BLOCK, 2 * ATT_BLOCK), 0)
    c = lax.broadcasted_iota(jnp.int32, (ATT_BLOCK, 2 * ATT_BLOCK), 1)
    dist = r - c + ATT_BLOCK
    k_pos = (j - 1) * ATT_BLOCK + c - FRONT_PAD
    mask = (dist >= 0) & (dist < WINDOW) & (k_pos >= 0)
    distf = dist.astype(F32)
    for h in range(N_HEADS):
        kv = h // Q_PER_KV
        qh = q[:, h * HEAD_DIM:(h + 1) * HEAD_DIM].astype(BF16)
        kk = kb[:, kv * HEAD_DIM:(kv + 1) * HEAD_DIM].astype(BF16)
        vv = vb[:, kv * HEAD_DIM:(kv + 1) * HEAD_DIM].astype(BF16)
        s = _dot_nt(qh, kk) * (HEAD_DIM ** -0.5) - (2.0 ** -(h + 1)) * distf
        s = jnp.where(mask, s, NEG)
        o_ref[:, h * HEAD_DIM:(h + 1) * HEAD_DIM] = _softmax_pv(s, sink_ref[h], vv)


def _swa_prompt(sink, q, k, v, batch, lp):
    nb = lp // ATT_BLOCK
    n = batch * lp
    cur = lambda w: pl.BlockSpec((ATT_BLOCK, w), lambda b, j: (b * nb + j, 0))
    prev = lambda w: pl.BlockSpec((ATT_BLOCK, w), lambda b, j: (b * nb + jnp.maximum(j - 1, 0), 0))
    return pl.pallas_call(
        _swa_prompt_kernel,
        grid=(batch, nb),
        in_specs=[pl.BlockSpec(memory_space=pltpu.SMEM), cur(Q_W), prev(KV_W), cur(KV_W), prev(KV_W), cur(KV_W)],
        out_specs=cur(Q_W),
        out_shape=jax.ShapeDtypeStruct((n, Q_W), F32),
        compiler_params=pltpu.CompilerParams(dimension_semantics=("parallel", "parallel")),
        name="swa_prompt",
    )(sink, q, k, k, v, v)


def _swa_sample_kernel(sink_ref, q_ref, k_ref, v_ref, bk_ref, bv_ref, o_ref, nk_ref, nv_ref, *, t_s):
    nbuf = WINDOW
    span = 2 * WINDOW
    rows = Q_PER_KV * t_s
    r = lax.broadcasted_iota(jnp.int32, (rows, span), 0)
    c = lax.broadcasted_iota(jnp.int32, (rows, span), 1)
    t = r % t_s
    dist = t + nbuf - c
    mask = (dist >= 0) & (dist < WINDOW) & (c < nbuf + t_s)
    distf = dist.astype(F32)
    g_col = lax.broadcasted_iota(jnp.int32, (rows, 1), 0) // t_s
    fill = jnp.zeros((span - nbuf - t_s, KV_W), F32)

    def one_seq(s, carry):
        rs = pl.ds(pl.multiple_of(s * t_s, t_s), t_s)
        q = q_ref[rs, :]
        k_new = k_ref[rs, :]
        v_new = v_ref[rs, :]
        bk = bk_ref[s]
        bv = bv_ref[s]
        k_all = jnp.concatenate([bk, k_new, fill], 0)
        v_all = jnp.concatenate([bv, v_new, fill], 0)
        for kv in range(N_KV):
            qg = jnp.concatenate(
                [q[:, (kv * Q_PER_KV + g) * HEAD_DIM:(kv * Q_PER_KV + g + 1) * HEAD_DIM] for g in range(Q_PER_KV)], 0)
            kk = k_all[:, kv * HEAD_DIM:(kv + 1) * HEAD_DIM].astype(BF16)
            vv = v_all[:, kv * HEAD_DIM:(kv + 1) * HEAD_DIM].astype(BF16)
            slope = jnp.zeros((rows, 1), F32)
            sink = jnp.zeros((rows, 1), F32)
            for g in range(Q_PER_KV):
                h = kv * Q_PER_KV + g
                slope = jnp.where(g_col == g, 2.0 ** -(h + 1), slope)
                sink = jnp.where(g_col == g, sink_ref[h], sink)
            sc = _dot_nt(qg.astype(BF16), kk) * (HEAD_DIM ** -0.5) - slope * distf
            sc = jnp.where(mask, sc, NEG)
            o = _softmax_pv(sc, sink, vv)
            for g in range(Q_PER_KV):
                h = kv * Q_PER_KV + g
                o_ref[rs, h * HEAD_DIM:(h + 1) * HEAD_DIM] = o[g * t_s:(g + 1) * t_s]
        nk_ref[s, 0:nbuf - t_s, :] = bk[t_s:, :]
        nk_ref[s, nbuf - t_s:nbuf, :] = k_new
        nv_ref[s, 0:nbuf - t_s, :] = bv[t_s:, :]
        nv_ref[s, nbuf - t_s:nbuf, :] = v_new
        return carry

    lax.fori_loop(0, SAMPLE_SEQS, one_seq, 0)


def _swa_sample(sink, q, k, v, buf_k, buf_v, row0, t_s):
    n_seq = buf_k.shape[0]
    sb = SAMPLE_SEQS
    rb = sb * t_s
    b0 = row0 // rb
    rows = lambda w: pl.BlockSpec((rb, w), lambda i: (b0 + i, 0))
    bufs = pl.BlockSpec((sb, WINDOW, KV_W), lambda i: (i, 0, 0))
    return pl.pallas_call(
        functools.partial(_swa_sample_kernel, t_s=t_s),
        grid=(n_seq // sb,),
        in_specs=[pl.BlockSpec(memory_space=pltpu.SMEM), rows(Q_W), rows(KV_W), rows(KV_W), bufs, bufs],
        out_specs=[pl.BlockSpec((rb, Q_W), lambda i: (i, 0)), bufs, bufs],
        out_shape=[jax.ShapeDtypeStruct((n_seq * t_s, Q_W), F32),
                   jax.ShapeDtypeStruct(buf_k.shape, F32), jax.ShapeDtypeStruct(buf_v.shape, F32)],
        compiler_params=pltpu.CompilerParams(dimension_semantics=("parallel",)),
        name="swa_sample",
    )(sink, q, k, v, buf_k, buf_v)


def _gla_tables(chunk):
    t = np.arange(chunk)[:, None]
    u = np.arange(chunk)[None, :]
    mats = [u <= t, u > t]
    masks = []
    w = chunk // 2
    while w >= 1:
        ref = (t // (2 * w)) * 2 * w + w - 1
        right = (t // w) % 2 == 1
        mats.append(np.where(right, (u > ref) & (u <= t), (u > t) & (u <= ref)))
        masks.append((t // (2 * w) == u // (2 * w)) & right & ((u // w) % 2 == 0))
        w //= 2
    return (np.concatenate(mats, 0).astype(np.float32), np.stack(masks, 0).astype(np.float32))


def _split3(x):
    hi = x.astype(BF16)
    r1 = x - hi.astype(F32)
    mid = r1.astype(BF16)
    lo = (r1 - mid.astype(F32)).astype(BF16)
    return hi, mid, lo


def _gla_prompt_kernel(g_ref, m_ref, q_ref, k_ref, v_ref, la_ref, o_ref, s_ref):
    c = pl.program_id(1)
    C = GLA_CHUNK
    n_lvl = m_ref.shape[0]

    @pl.when(c == 0)
    def _():
        s_ref[...] = jnp.zeros_like(s_ref)

    G = g_ref[...]
    eye = (lax.broadcasted_iota(jnp.int32, (C, C), 0) == lax.broadcasted_iota(jnp.int32, (C, C), 1)).astype(F32)
    for h in range(GLA_HEADS):
        ks = slice(h * GLA_HK, (h + 1) * GLA_HK)
        vs = slice(h * GLA_HV, (h + 1) * GLA_HV)
        q = q_ref[:, ks]
        k = k_ref[:, ks]
        v = v_ref[:, vs].astype(BF16)
        hi, mid, lo = _split3(la_ref[:, ks])
        E = jnp.exp(_dot(G, hi) + _dot(G, mid) + _dot(G, lo))
        S = s_ref[0, h]
        o = _dot((q * E[0:C]).astype(BF16), S.astype(BF16))
        att = eye * jnp.sum(q * k, -1, keepdims=True)
        for l in range(n_lvl):
            El = E[(2 + l) * C:(3 + l) * C]
            att = att + m_ref[l] * _dot_nt((q * El).astype(BF16), (k * El).astype(BF16))
        o_ref[:, vs] = o + _dot(att.astype(BF16), v)
        ke_t = (k * E[C:2 * C]).T.astype(BF16)
        d_col = jnp.broadcast_to(E[C - 1:C], (GLA_HK, GLA_HK)).T[:, 0:1]
        s_ref[0, h] = S * d_col + _dot(ke_t, v)


def _gla_prompt(gq, gk, gv, la, batch, lp):
    C = GLA_CHUNK
    nc = lp // C
    G, M = _gla_tables(C)
    G = jnp.asarray(G, BF16)
    M = jnp.asarray(M, F32)
    n = batch * lp
    rows = lambda w: pl.BlockSpec((C, w), lambda b, c: (b * nc + c, 0))
    full = lambda a: pl.BlockSpec(a.shape, lambda b, c: (0,) * a.ndim)
    return pl.pallas_call(
        _gla_prompt_kernel,
        grid=(batch, nc),
        in_specs=[full(G), full(M), rows(GLA_DK), rows(GLA_DK), rows(GLA_DV), rows(GLA_DK)],
        out_specs=[rows(GLA_DV), pl.BlockSpec((1, GLA_HEADS, GLA_HK, GLA_HV), lambda b, c: (b, 0, 0, 0))],
        out_shape=[jax.ShapeDtypeStruct((n, GLA_DV), F32),
                   jax.ShapeDtypeStruct((batch, GLA_HEADS, GLA_HK, GLA_HV), F32)],
        compiler_params=pltpu.CompilerParams(dimension_semantics=("parallel", "arbitrary")),
        name="gla_prompt",
    )(G, M, gq, gk, gv, la)


def _gla_sample_kernel(q_ref, k_ref, v_ref, la_ref, s0_ref, o_ref, s_ref, *, t_s):
    T = t_s
    row = lax.broadcasted_iota(jnp.int32, (T, GLA_HK), 0)
    k_fill = jnp.zeros((GLA_HK - T - 8, GLA_HK), F32)
    v_fill = jnp.zeros((GLA_HK - T, GLA_HV), F32)

    def one_seq(s, carry):
        rs = pl.ds(pl.multiple_of(s * T, T), T)
        for h in range(GLA_HEADS):
            ks = slice(h * GLA_HK, (h + 1) * GLA_HK)
            vs = slice(h * GLA_HV, (h + 1) * GLA_HV)
            q = q_ref[rs, ks]
            k = k_ref[rs, ks]
            v = v_ref[rs, vs]
            b = la_ref[rs, ks]
            sh = 1
            while sh < T:
                b = b + jnp.where(row >= sh, pltpu.roll(b, sh, 0), 0.0)
                sh *= 2
            S = s0_ref[s, h]
            o = _dot((q * jnp.exp(b)).astype(BF16), S.astype(BF16))
            for j in range(T):
                e = jnp.exp(jnp.where(row >= j, b - b[j:j + 1], NEG))
                a_col = jnp.sum(q * k[j:j + 1] * e, -1, keepdims=True)
                o = o + a_col * v[j:j + 1]
            o_ref[rs, vs] = o
            b_last = b[T - 1:T]
            ke = k * jnp.exp(b_last - b)
            kt = jnp.concatenate([ke, jnp.broadcast_to(jnp.exp(b_last), (8, GLA_HK)), k_fill], 0).T
            v_pad = jnp.concatenate([v, v_fill], 0)
            s_ref[s, h] = S * kt[:, T:T + 1] + _dot(kt.astype(BF16), v_pad.astype(BF16))
        return carry

    lax.fori_loop(0, SAMPLE_SEQS, one_seq, 0)


def _gla_sample(gq, gk, gv, la, s0, row0, t_s):
    n_seq = s0.shape[0]
    sb = SAMPLE_SEQS
    rb = sb * t_s
    b0 = row0 // rb
    rows = lambda w: pl.BlockSpec((rb, w), lambda i: (b0 + i, 0))
    st = pl.BlockSpec((sb, GLA_HEADS, GLA_HK, GLA_HV), lambda i: (i, 0, 0, 0))
    return pl.pallas_call(
        functools.partial(_gla_sample_kernel, t_s=t_s),
        grid=(n_seq // sb,),
        in_specs=[rows(GLA_DK), rows(GLA_DK), rows(GLA_DV), rows(GLA_DK), st],
        out_specs=[pl.BlockSpec((rb, GLA_DV), lambda i: (i, 0)), st],
        out_shape=[jax.ShapeDtypeStruct((n_seq * t_s, GLA_DV), F32), jax.ShapeDtypeStruct(s0.shape, F32)],
        compiler_params=pltpu.CompilerParams(dimension_semantics=("parallel",), vmem_limit_bytes=VMEM_LIMIT),
        name="gla_sample",
    )(gq, gk, gv, la, s0)


def _merge_kernel(x_ref, yap_ref, yas_ref, ogp_ref, ogs_ref, gr_ref, gate_ref, eg_ref, eb_ref, ng_ref, bg_ref,
                  wa_ref, wg_ref, wo_ref, g1_ref, b1_ref, h1_ref, *, prompt_tiles):
    h = _ln(x_ref[...], eg_ref[...], eb_ref[...])
    is_prompt = pl.program_id(0) < prompt_tiles
    og = jnp.where(is_prompt, ogp_ref[...], ogs_ref[...])
    ya = jnp.where(is_prompt, yap_ref[...], yas_ref[...])
    parts = []
    for hh in range(GLA_HEADS):
        o = og[:, hh * GLA_HV:(hh + 1) * GLA_HV]
        parts.append(o * lax.rsqrt(jnp.mean(o * o, -1, keepdims=True) + EPS))
    gr = gr_ref[...]
    y_gla = jnp.concatenate(parts, 1) * ng_ref[...] * (gr * _sigmoid(gr))
    a = _dot(ya.astype(BF16), wa_ref[...])
    b = _dot(y_gla.astype(BF16), wg_ref[...])
    gate = _sigmoid(gate_ref[...] + bg_ref[...])
    hm = gate[:, :D_MODEL] * a + gate[:, D_MODEL:] * b
    mix = _dot(hm.astype(BF16), wo_ref[...])
    h1_ref[...] = _ln(DN_ALPHA * h + mix, g1_ref[...], b1_ref[...])


def _merge(x_all, ya_p, ya_s, og_p, og_s, gr, gate, eg, eb, ng, bg, wa, wg, wo, g1, b1):
    n = x_all.shape[0]
    tm = ROW_TILE
    pt = ya_p.shape[0] // tm
    st = ya_s.shape[0] // tm
    row = lambda w: pl.BlockSpec((tm, w), lambda i: (i, 0))
    row_p = lambda w: pl.BlockSpec((tm, w), lambda i: (jnp.minimum(i, pt - 1), 0))
    row_s = lambda w: pl.BlockSpec((tm, w), lambda i: (jnp.clip(i - pt, 0, st - 1), 0))
    full = lambda a: pl.BlockSpec(a.shape, lambda i: (0,) * a.ndim)
    return pl.pallas_call(
        functools.partial(_merge_kernel, prompt_tiles=pt),
        grid=(n // tm,),
        in_specs=[row(D_MODEL), row_p(Q_W), row_s(Q_W), row_p(GLA_DV), row_s(GLA_DV), row(GLA_DV), row(2 * D_MODEL),
                  full(eg), full(eb), full(ng), full(bg), full(wa), full(wg), full(wo), full(g1), full(b1)],
        out_specs=row(D_MODEL),
        out_shape=jax.ShapeDtypeStruct((n, D_MODEL), F32),
        compiler_params=pltpu.CompilerParams(dimension_semantics=("parallel",), vmem_limit_bytes=VMEM_LIMIT),
        name="merge",
    )(x_all, ya_p, ya_s, og_p, og_s, gr, gate, eg, eb, ng, bg, wa, wg, wo, g1, b1)


ROUTER_ROWS = 40


def _router_kernel(h_ref, valid_ref, wr_ref, br_ref, ids_ref, wts_ref):
    tm = h_ref.shape[0]
    lt = lax.dot_general(wr_ref[...], h_ref[...], (((1,), (1,)), ((), ())),
                         precision=lax.Precision.HIGHEST, preferred_element_type=F32) + br_ref[...]
    el = lt[0:N_EXPERTS]
    gl = lt[N_EXPERTS:N_EXPERTS + N_GROUPS]
    g_max = jnp.max(gl, 0, keepdims=True)
    g_row = lax.broadcasted_iota(jnp.int32, (N_GROUPS, tm), 0)
    g_idx = jnp.min(jnp.where(gl == g_max, g_row, N_GROUPS), 0, keepdims=True)
    p_max = 1.0 / jnp.sum(jnp.exp(gl - g_max), 0, keepdims=True)
    e_row = lax.broadcasted_iota(jnp.int32, (N_EXPERTS, tm), 0)
    m1 = jnp.where(e_row // EXP_PER_GROUP == g_idx, el, -jnp.inf)
    v1 = jnp.max(m1, 0, keepdims=True)
    i1 = jnp.min(jnp.where(m1 == v1, e_row, N_EXPERTS), 0, keepdims=True)
    m2 = jnp.where(e_row == i1, -jnp.inf, m1)
    v2 = jnp.max(m2, 0, keepdims=True)
    i2 = jnp.min(jnp.where(m2 == v2, e_row, N_EXPERTS), 0, keepdims=True)
    e2 = jnp.exp(v2 - v1)
    w1 = p_max / (1.0 + e2)
    w2 = p_max * e2 / (1.0 + e2)
    valid = valid_ref[...] > 0.0
    o_row = lax.broadcasted_iota(jnp.int32, (8, tm), 0)
    ids = jnp.where(o_row == 0, i1, jnp.where(o_row == 1, i2, -1))
    ids_ref[...] = jnp.where(valid, ids, -1)
    wts_ref[...] = jnp.where(o_row == 0, w1, jnp.where(o_row == 1, w2, 0.0))


def _router(h1, valid, wr_t, br):
    n = h1.shape[0]
    tm = ROW_TILE
    full = lambda a: pl.BlockSpec(a.shape, lambda i: (0,) * a.ndim)
    return pl.pallas_call(
        _router_kernel,
        grid=(n // tm,),
        in_specs=[pl.BlockSpec((tm, D_MODEL), lambda i: (i, 0)), pl.BlockSpec((1, tm), lambda i: (0, i)),
                  full(wr_t), full(br)],
        out_specs=[pl.BlockSpec((8, tm), lambda i: (0, i)), pl.BlockSpec((8, tm), lambda i: (0, i))],
        out_shape=[jax.ShapeDtypeStruct((8, n), jnp.int32), jax.ShapeDtypeStruct((8, n), F32)],
        compiler_params=pltpu.CompilerParams(dimension_semantics=("parallel",)),
        name="router",
    )(h1, valid, wr_t, br)


def _ffn_kernel(te_ref, nt_ref, src_ref, h_hbm, wg_ref, wu_ref, wd_ref, out_ref, xbuf, sem, wgb, wub, wdb):
    i = pl.program_id(0)
    nt = nt_ref[0]
    T = FFN_TILE

    def gather(tile, slot):
        def body(r, carry):
            tok = src_ref[tile * T + r]
            pltpu.make_async_copy(h_hbm.at[pl.ds(tok, 1), :], xbuf.at[slot, pl.ds(r, 1), :], sem.at[slot]).start()
            return carry
        lax.fori_loop(0, T, body, 0)

    @pl.when(i == 0)
    def _():
        gather(0, 0)

    @pl.when(i + 1 < nt)
    def _():
        gather(i + 1, (i + 1) % 2)

    @pl.when(i < nt)
    def _():
        slot = i % 2
        pltpu.make_async_copy(h_hbm.at[pl.ds(0, T), :], xbuf.at[slot], sem.at[slot]).wait()

        @pl.when((i == 0) | (te_ref[i] != te_ref[jnp.maximum(i - 1, 0)]))
        def _():
            wgb[...] = wg_ref[0].astype(BF16)
            wub[...] = wu_ref[0].astype(BF16)
            wdb[...] = wd_ref[0].astype(BF16)

        x = xbuf[slot].astype(BF16)
        g = _dot(x, wgb[...])
        u = _dot(x, wub[...])
        out_ref[...] = _dot((g * _sigmoid(g) * u).astype(BF16), wdb[...])

    @pl.when(i >= nt)
    def _():
        out_ref[...] = jnp.zeros_like(out_ref)


def _ffn(tile_expert, n_tiles, src_rows, h1, w_g, w_u, w_d):
    T = FFN_TILE
    max_tiles = tile_expert.shape[0]
    wspec = lambda a: pl.BlockSpec((1,) + a.shape[1:], lambda i, te, nt, src: (te[i], 0, 0))
    return pl.pallas_call(
        _ffn_kernel,
        grid_spec=pltpu.PrefetchScalarGridSpec(
            num_scalar_prefetch=3,
            grid=(max_tiles,),
            in_specs=[pl.BlockSpec(memory_space=pl.ANY), wspec(w_g), wspec(w_u), wspec(w_d)],
            out_specs=pl.BlockSpec((T, D_MODEL), lambda i, te, nt, src: (i, 0)),
            scratch_shapes=[pltpu.VMEM((2, T, D_MODEL), F32), pltpu.SemaphoreType.DMA((2,)),
                            pltpu.VMEM((D_MODEL, D_EXPERT), BF16), pltpu.VMEM((D_MODEL, D_EXPERT), BF16),
                            pltpu.VMEM((D_EXPERT, D_MODEL), BF16)]),
        out_shape=jax.ShapeDtypeStruct((max_tiles * T, D_MODEL), F32),
        compiler_params=pltpu.CompilerParams(dimension_semantics=("arbitrary",)),
        name="ffn",
    )(tile_expert, n_tiles, src_rows, h1, w_g, w_u, w_d)


def _combine(dest, h1, wts_t, ys, g2, b2, n_out, first_block, blocks_per_batch, skip_blocks):
    T = CMB_TILE
    n_rows = h1.shape[0]
    if skip_blocks:
        blk = lambda i: first_block + i + (i // blocks_per_batch + 1) * skip_blocks
    else:
        blk = lambda i: first_block + i
    kern = functools.partial(_combine_kernel_generic, blk=blk, n_rows=n_rows)
    full = lambda a: pl.BlockSpec(a.shape, lambda i, d: (0,) * a.ndim)
    return pl.pallas_call(
        kern,
        grid_spec=pltpu.PrefetchScalarGridSpec(
            num_scalar_prefetch=1,
            grid=(n_out // T,),
            in_specs=[pl.BlockSpec((T, D_MODEL), lambda i, d: (blk(i), 0)),
                      pl.BlockSpec((T, 2), lambda i, d: (blk(i), 0)),
                      full(g2), full(b2), pl.BlockSpec(memory_space=pl.ANY)],
            out_specs=pl.BlockSpec((T, D_MODEL), lambda i, d: (i, 0)),
            scratch_shapes=[pltpu.VMEM((2, 2, T, D_MODEL), F32), pltpu.SemaphoreType.DMA((2,))]),
        out_shape=jax.ShapeDtypeStruct((n_out, D_MODEL), F32),
        compiler_params=pltpu.CompilerParams(dimension_semantics=("arbitrary",)),
        name="combine",
    )(dest, h1, wts_t, g2, b2, ys)


def _combine_kernel_generic(dest_ref, h_ref, w_ref, g_ref, b_ref, ys_hbm, y_ref, buf, sem, *, blk, n_rows):
    i = pl.program_id(0)
    n = pl.num_programs(0)
    T = CMB_TILE

    def gather(step, slot):
        base = blk(step) * T

        def body(r, carry):
            for kk in range(2):
                d = dest_ref[kk * n_rows + base + r]
                pltpu.make_async_copy(ys_hbm.at[pl.ds(d, 1), :], buf.at[slot, kk, pl.ds(r, 1), :],
                                      sem.at[slot]).start()
            return carry
        lax.fori_loop(0, T, body, 0)

    @pl.when(i == 0)
    def _():
        gather(0, 0)

    @pl.when(i + 1 < n)
    def _():
        gather(i + 1, (i + 1) % 2)

    slot = i % 2
    for kk in range(2):
        pltpu.make_async_copy(ys_hbm.at[pl.ds(0, T), :], buf.at[slot, kk], sem.at[slot]).wait()
    w = w_ref[...]
    ff = w[:, 0:1] * buf[slot, 0] + w[:, 1:2] * buf[slot, 1]
    y_ref[...] = _ln(DN_ALPHA * h_ref[...] + ff, g_ref[...], b_ref[...])


def _dispatch_plan(ids, n_rows, max_tiles):
    T = FFN_TILE
    e_flat = ids.reshape(-1)
    onehot = (e_flat[:, None] == jnp.arange(N_EXPERTS, dtype=jnp.int32)[None, :]).astype(jnp.int32)
    csum = jnp.cumsum(onehot, axis=0)
    counts = csum[-1]
    rank = jnp.sum(onehot * csum, axis=1) - 1
    tiles_e = (counts + T - 1) // T
    tile_end = jnp.cumsum(tiles_e)
    tile_start = tile_end - tiles_e
    n_tiles = tile_end[-1]
    n_slots = max_tiles * T
    start_of = jnp.sum(onehot * tile_start[None, :], axis=1) * T
    dest = jnp.where(e_flat >= 0, start_of + rank, n_slots).astype(jnp.int32)
    rows = jnp.tile(jnp.arange(n_rows, dtype=jnp.int32), 2)
    src = jnp.zeros((n_slots,), jnp.int32).at[dest].set(rows, mode="drop")
    te = jnp.sum((jnp.arange(max_tiles, dtype=jnp.int32)[:, None] >= tile_end[None, :]).astype(jnp.int32), axis=1)
    te_last = jnp.take(te, jnp.maximum(n_tiles - 1, 0))
    te = jnp.where(jnp.arange(max_tiles) < n_tiles, te, te_last).astype(jnp.int32)
    dest = jnp.minimum(dest, n_slots - 1)
    return te, n_tiles.reshape(1).astype(jnp.int32), src, dest


def kernel(x_prompt, x_sample, state_swa_k, state_swa_v, state_gla, meta_tokens, ln_emb_g, ln_emb_b, w_in, b_gate, attn_sink, w_alpha2, b_alpha, gla_norm_g, w_attn_br, w_gla_br, w_out, ln1_g, ln1_b, w_router_group, b_router_group, w_router_expert, b_router_expert, w_exp_gate, w_exp_up, w_exp_down, ln2_g, ln2_b):
    B, seq, _ = x_prompt.shape
    n_seq, t_s, _ = x_sample.shape
    depth = w_in.shape[0]
    assert depth == 1 and seq % ATT_BLOCK == 0 and t_s == 8
    lp = SKIP_ROWS + seq
    NP, NS = B * lp, n_seq * t_s
    NR = NP + NS
    assert NP % ROW_TILE == 0 and NS % ROW_TILE == 0 and n_seq % SAMPLE_SEQS == 0
    l = 0
    row2 = lambda a: a.reshape(1, -1)

    xp = jnp.concatenate([jnp.zeros((B, FRONT_PAD, D_MODEL), F32),
                          jnp.broadcast_to(meta_tokens[None], (B, N_META, D_MODEL)), x_prompt], axis=1)
    x_all = jnp.concatenate([xp.reshape(NP, D_MODEL), x_sample.reshape(NS, D_MODEL)], axis=0)
    pos = np.arange(NR)
    in_prompt = pos < NP
    keep = jnp.asarray(~(in_prompt & (pos % lp < FRONT_PAD)), F32).reshape(NR, 1)
    moe_valid = jnp.asarray(~(in_prompt & (pos % lp < SKIP_ROWS)), F32).reshape(1, NR)

    wi = w_in[l]
    c_ga = sum((Q_W, KV_W, KV_W, GLA_DK, GLA_DK, GLA_DV))
    w_bf = jnp.concatenate([wi[:, :c_ga], wi[:, c_ga + GLA_RANK:], wi[:, c_ga:c_ga + GLA_RANK],
                            jnp.zeros((D_MODEL, 128 - GLA_RANK), F32)], axis=1).astype(BF16)
    wa2_bf = jnp.concatenate([w_alpha2[l], jnp.zeros((128 - GLA_RANK, GLA_DK), F32)], axis=0).astype(BF16)

    q, k, v, gq, gk, gv, la, gr, gate = _inproj(x_all, keep, row2(ln_emb_g), row2(ln_emb_b), w_bf, wa2_bf,
                                                row2(b_alpha[l]))

    sink = attn_sink[l]
    ya_p = _swa_prompt(sink, q, k, v, B, lp)
    buf_k = state_swa_k[l].reshape(n_seq, WINDOW, KV_W)
    buf_v = state_swa_v[l].reshape(n_seq, WINDOW, KV_W)
    ya_s, nk_s, nv_s = _swa_sample(sink, q, k, v, buf_k, buf_v, NP, t_s)

    og_p, s_p = _gla_prompt(gq, gk, gv, la, B, lp)
    og_s, s_s = _gla_sample(gq, gk, gv, la, state_gla[l], NP, t_s)

    h1 = _merge(x_all, ya_p, ya_s, og_p, og_s, gr, gate, row2(ln_emb_g), row2(ln_emb_b), row2(gla_norm_g[l]), row2(b_gate[l]),
                w_attn_br[l].astype(BF16), w_gla_br[l].astype(BF16), w_out[l].astype(BF16),
                row2(ln1_g[l]), row2(ln1_b[l]))

    wr_t = jnp.concatenate([w_router_expert[l].T, w_router_group[l].T,
                            jnp.zeros((ROUTER_ROWS - N_EXPERTS - N_GROUPS, D_MODEL), F32)], axis=0)
    br = jnp.concatenate([b_router_expert[l], b_router_group[l],
                          jnp.zeros((ROUTER_ROWS - N_EXPERTS - N_GROUPS,), F32)]).reshape(ROUTER_ROWS, 1)
    ids, wts = _router(h1, moe_valid, wr_t, br)

    n_tok = B * seq + NS
    max_tiles = (2 * n_tok) // FFN_TILE + N_EXPERTS
    te, n_tiles, src, dest = _dispatch_plan(ids[0:2], NR, max_tiles)
    ys = _ffn(te, n_tiles, src, h1, w_exp_gate[l], w_exp_up[l], w_exp_down[l])

    wts_t = wts[0:2].T
    g2, b2 = row2(ln2_g[l]), row2(ln2_b[l])
    skip_blocks = SKIP_ROWS // CMB_TILE
    y_p = _combine(dest, h1, wts_t, ys, g2, b2, B * seq, 0, seq // CMB_TILE, skip_blocks)
    y_s = _combine(dest, h1, wts_t, ys, g2, b2, NS, NP // CMB_TILE, 1, 0)

    kv_shape = (B, lp, N_KV, HEAD_DIM)
    k_p = k[:NP].reshape(kv_shape)[:, -WINDOW:]
    v_p = v[:NP].reshape(kv_shape)[:, -WINDOW:]
    return (y_p.reshape(B, seq, D_MODEL), y_s.reshape(n_seq, t_s, D_MODEL),
            k_p[None], v_p[None], s_p[None],
            nk_s.reshape(1, n_seq, WINDOW, N_KV, HEAD_DIM), nv_s.reshape(1, n_seq, WINDOW, N_KV, HEAD_DIM),
            s_s[None])
```

```python
import functools

import numpy as np
import jax
import jax.numpy as jnp
from jax import lax
from jax.experimental import pallas as pl
from jax.experimental.pallas import tpu as pltpu

F32 = jnp.float32
BF16 = jnp.bfloat16

D_MODEL = 1024
N_META = 16
HEAD_DIM = 64
N_HEADS = 8
N_KV = 2
Q_PER_KV = 4
WINDOW = 128
ATT_BLOCK = 128
GLA_HEADS = 4
GLA_HK = 128
GLA_HV = 256
GLA_DK = GLA_HEADS * GLA_HK
GLA_DV = GLA_HEADS * GLA_HV
GLA_RANK = 16
GLA_TAU = 16.0
GLA_CHUNK = 64
N_GROUPS = 4
EXP_PER_GROUP = 8
N_EXPERTS = 32
D_EXPERT = 256
DN_ALPHA = 2.0 ** 0.25
EPS = 1e-5
NEG = -1e30

FRONT_PAD = (-N_META) % ATT_BLOCK
SKIP_ROWS = FRONT_PAD + N_META

Q_W, KV_W = N_HEADS * HEAD_DIM, N_KV * HEAD_DIM
SEG = {}
_o = 0
for _n, _w in (("q", Q_W), ("k", KV_W), ("v", KV_W), ("gq", GLA_DK), ("gk", GLA_DK), ("gv", GLA_DV),
               ("gr", GLA_DV), ("gate", 2 * D_MODEL), ("ga", 128)):
    SEG[_n] = (_o, _o + _w)
    _o += _w
W_IN_COLS = _o

ROW_TILE = 256
FFN_TILE = 256
CMB_TILE = 128
SAMPLE_SEQS = 8
LANES = 128
ROW_CHUNKS = D_MODEL // LANES
VMEM_LIMIT = 56 * 1024 * 1024


def _ln(x, g, b):
    mu = jnp.mean(x, -1, keepdims=True)
    xc = x - mu
    var = jnp.mean(xc * xc, -1, keepdims=True)
    return xc * lax.rsqrt(var + EPS) * g + b


def _sigmoid(x):
    return 0.5 * jnp.tanh(0.5 * x) + 0.5


def _dot(a, b):
    return jnp.dot(a, b, preferred_element_type=F32)


def _dot_nt(a, b):
    return lax.dot_general(a, b, (((1,), (1,)), ((), ())), preferred_element_type=F32)


def _inproj_kernel(x_ref, keep_ref, g_ref, b_ref, w_ref, wa2_ref, ba_ref,
                   q_ref, k_ref, v_ref, gq_ref, gk_ref, gv_ref, la_ref, gr_ref, gate_ref):
    hb = _ln(x_ref[...], g_ref[...], b_ref[...]).astype(BF16)
    keep = keep_ref[...]

    def seg(name):
        a, b = SEG[name]
        return _dot(hb, w_ref[:, a:b])

    q_ref[...] = seg("q")
    k_ref[...] = seg("k")
    v_ref[...] = seg("v")
    gq_ref[...] = seg("gq") * (GLA_HK ** -0.5)
    gk_ref[...] = seg("gk") * keep
    gv_ref[...] = seg("gv") * keep
    gr_ref[...] = seg("gr")
    gate_ref[...] = seg("gate")
    z = _dot(seg("ga").astype(BF16), wa2_ref[...]) + ba_ref[...]
    la = (jnp.minimum(z, 0.0) - jnp.log(1.0 + jnp.exp(-jnp.abs(z)))) * (1.0 / GLA_TAU)
    la_ref[...] = la * keep


def _inproj(x_all, keep, ln_g, ln_b, w_bf, wa2_bf, b_alpha):
    n = x_all.shape[0]
    tm = ROW_TILE
    widths = [Q_W, KV_W, KV_W, GLA_DK, GLA_DK, GLA_DV, GLA_DK, GLA_DV, 2 * D_MODEL]
    row = lambda w: pl.BlockSpec((tm, w), lambda i: (i, 0))
    full = lambda a: pl.BlockSpec(a.shape, lambda i: (0,) * a.ndim)
    return pl.pallas_call(
        _inproj_kernel,
        grid=(n // tm,),
        in_specs=[row(D_MODEL), row(1), full(ln_g), full(ln_b), full(w_bf), full(wa2_bf), full(b_alpha)],
        out_specs=[row(w) for w in widths],
        out_shape=[jax.ShapeDtypeStruct((n, w), F32) for w in widths],
        compiler_params=pltpu.CompilerParams(dimension_semantics=("parallel",), vmem_limit_bytes=VMEM_LIMIT),
        name="inproj",
    )(x_all, keep, ln_g, ln_b, w_bf, wa2_bf, b_alpha)


def _softmax_pv(s, sink, vv):
    m = jnp.maximum(jnp.max(s, -1, keepdims=True), sink)
    p = jnp.exp(s - m)
    l = jnp.sum(p, -1, keepdims=True) + jnp.exp(sink - m)
    return _dot(p.astype(BF16), vv) / l


def _swa_prompt_kernel(sink_ref, q_ref, kp_ref, kc_ref, vp_ref, vc_ref, o_ref):
    j = pl.program_id(1)
    q = q_ref[...]
    kb = jnp.concatenate([kp_ref[...], kc_ref[...]], 0)
    vb = jnp.concatenate([vp_ref[...], vc_ref[...]], 0)
    r = lax.broadcasted_iota(jnp.int32, (ATT_BLOCK, 2 * ATT_BLOCK), 0)
    c = lax.broadcasted_iota(jnp.int32, (ATT_BLOCK, 2 * ATT_BLOCK), 1)
    dist = r - c + ATT_BLOCK
    k_pos = (j - 1) * ATT_BLOCK + c - FRONT_PAD
    mask = (dist >= 0) & (dist < WINDOW) & (k_pos >= 0)
    distf = dist.astype(F32)
    for h in range(N_HEADS):
        kv = h // Q_PER_KV
        qh = q[:, h * HEAD_DIM:(h + 1) * HEAD_DIM].astype(BF16)
        kk = kb[:, kv * HEAD_DIM:(kv + 1) * HEAD_DIM].astype(BF16)
        vv = vb[:, kv * HEAD_DIM:(kv + 1) * HEAD_DIM].astype(BF16)
        s = _dot_nt(qh, kk) * (HEAD_DIM ** -0.5) - (2.0 ** -(h + 1)) * distf
        s = jnp.where(mask, s, NEG)
        o_ref[:, h * HEAD_DIM:(h + 1) * HEAD_DIM] = _softmax_pv(s, sink_ref[h], vv)


def _swa_prompt(sink, q, k, v, batch, lp):
    nb = lp // ATT_BLOCK
    n = batch * lp
    cur = lambda w: pl.BlockSpec((ATT_BLOCK, w), lambda b, j: (b * nb + j, 0))
    prev = lambda w: pl.BlockSpec((ATT_BLOCK, w), lambda b, j: (b * nb + jnp.maximum(j - 1, 0), 0))
    return pl.pallas_call(
        _swa_prompt_kernel,
        grid=(batch, nb),
        in_specs=[pl.BlockSpec(memory_space=pltpu.SMEM), cur(Q_W), prev(KV_W), cur(KV_W), prev(KV_W), cur(KV_W)],
        out_specs=cur(Q_W),
        out_shape=jax.ShapeDtypeStruct((n, Q_W), F32),
        compiler_params=pltpu.CompilerParams(dimension_semantics=("parallel", "parallel")),
        name="swa_prompt",
    )(sink, q, k, k, v, v)


def _swa_sample_kernel(sink_ref, q_ref, k_ref, v_ref, bk_ref, bv_ref, o_ref, nk_ref, nv_ref, *, t_s):
    nbuf = WINDOW
    span = 2 * WINDOW
    rows = Q_PER_KV * t_s
    r = lax.broadcasted_iota(jnp.int32, (rows, span), 0)
    c = lax.broadcasted_iota(jnp.int32, (rows, span), 1)
    t = r % t_s
    dist = t + nbuf - c
    mask = (dist >= 0) & (dist < WINDOW) & (c < nbuf + t_s)
    distf = dist.astype(F32)
    g_col = lax.broadcasted_iota(jnp.int32, (rows, 1), 0) // t_s
    fill = jnp.zeros((span - nbuf - t_s, KV_W), F32)

    def one_seq(s, carry):
        rs = pl.ds(pl.multiple_of(s * t_s, t_s), t_s)
        q = q_ref[rs, :]
        k_new = k_ref[rs, :]
        v_new = v_ref[rs, :]
        bk = bk_ref[s]
        bv = bv_ref[s]
        k_all = jnp.concatenate([bk, k_new, fill], 0)
        v_all = jnp.concatenate([bv, v_new, fill], 0)
        for kv in range(N_KV):
            qg = jnp.concatenate(
                [q[:, (kv * Q_PER_KV + g) * HEAD_DIM:(kv * Q_PER_KV + g + 1) * HEAD_DIM] for g in range(Q_PER_KV)], 0)
            kk = k_all[:, kv * HEAD_DIM:(kv + 1) * HEAD_DIM].astype(BF16)
            vv = v_all[:, kv * HEAD_DIM:(kv + 1) * HEAD_DIM].astype(BF16)
            slope = jnp.zeros((rows, 1), F32)
            sink = jnp.zeros((rows, 1), F32)
            for g in range(Q_PER_KV):
                h = kv * Q_PER_KV + g
                slope = jnp.where(g_col == g, 2.0 ** -(h + 1), slope)
                sink = jnp.where(g_col == g, sink_ref[h], sink)
            sc = _dot_nt(qg.astype(BF16), kk) * (HEAD_DIM ** -0.5) - slope * distf
            sc = jnp.where(mask, sc, NEG)
            o = _softmax_pv(sc, sink, vv)
            for g in range(Q_PER_KV):
                h = kv * Q_PER_KV + g
                o_ref[rs, h * HEAD_DIM:(h + 1) * HEAD_DIM] = o[g * t_s:(g + 1) * t_s]
        nk_ref[s, 0:nbuf - t_s, :] = bk[t_s:, :]
        nk_ref[s, nbuf - t_s:nbuf, :] = k_new
        nv_ref[s, 0:nbuf - t_s, :] = bv[t_s:, :]
        nv_ref[s, nbuf - t_s:nbuf, :] = v_new
        return carry

    lax.fori_loop(0, SAMPLE_SEQS, one_seq, 0)


def _swa_sample(sink, q, k, v, buf_k, buf_v, row0, t_s):
    n_seq = buf_k.shape[0]
    sb = SAMPLE_SEQS
    rb = sb * t_s
    b0 = row0 // rb
    rows = lambda w: pl.BlockSpec((rb, w), lambda i: (b0 + i, 0))
    bufs = pl.BlockSpec((sb, WINDOW, KV_W), lambda i: (i, 0, 0))
    return pl.pallas_call(
        functools.partial(_swa_sample_kernel, t_s=t_s),
        grid=(n_seq // sb,),
        in_specs=[pl.BlockSpec(memory_space=pltpu.SMEM), rows(Q_W), rows(KV_W), rows(KV_W), bufs, bufs],
        out_specs=[pl.BlockSpec((rb, Q_W), lambda i: (i, 0)), bufs, bufs],
        out_shape=[jax.ShapeDtypeStruct((n_seq * t_s, Q_W), F32),
                   jax.ShapeDtypeStruct(buf_k.shape, F32), jax.ShapeDtypeStruct(buf_v.shape, F32)],
        compiler_params=pltpu.CompilerParams(dimension_semantics=("parallel",)),
        name="swa_sample",
    )(sink, q, k, v, buf_k, buf_v)


def _gla_tables(chunk):
    t = np.arange(chunk)[:, None]
    u = np.arange(chunk)[None, :]
    mats = [u <= t, u > t]
    masks = []
    w = chunk // 2
    while w >= 1:
        ref = (t // (2 * w)) * 2 * w + w - 1
        right = (t // w) % 2 == 1
        mats.append(np.where(right, (u > ref) & (u <= t), (u > t) & (u <= ref)))
        masks.append((t // (2 * w) == u // (2 * w)) & right & ((u // w) % 2 == 0))
        w //= 2
    return (np.concatenate(mats, 0).astype(np.float32), np.stack(masks, 0).astype(np.float32))


def _split3(x):
    hi = x.astype(BF16)
    r1 = x - hi.astype(F32)
    mid = r1.astype(BF16)
    lo = (r1 - mid.astype(F32)).astype(BF16)
    return hi, mid, lo


def _gla_prompt_kernel(g_ref, m_ref, q_ref, k_ref, v_ref, la_ref, o_ref, s_ref):
    c = pl.program_id(1)
    C = GLA_CHUNK
    n_lvl = m_ref.shape[0]

    @pl.when(c == 0)
    def _():
        s_ref[...] = jnp.zeros_like(s_ref)

    G = g_ref[...]
    eye = (lax.broadcasted_iota(jnp.int32, (C, C), 0) == lax.broadcasted_iota(jnp.int32, (C, C), 1)).astype(F32)
    for h in range(GLA_HEADS):
        ks = slice(h * GLA_HK, (h + 1) * GLA_HK)
        vs = slice(h * GLA_HV, (h + 1) * GLA_HV)
        q = q_ref[:, ks]
        k = k_ref[:, ks]
        v = v_ref[:, vs].astype(BF16)
        hi, mid, lo = _split3(la_ref[:, ks])
        E = jnp.exp(_dot(G, hi) + _dot(G, mid) + _dot(G, lo))
        S = s_ref[0, h]
        o = _dot((q * E[0:C]).astype(BF16), S.astype(BF16))
        att = eye * jnp.sum(q * k, -1, keepdims=True)
        for l in range(n_lvl):
            El = E[(2 + l) * C:(3 + l) * C]
            att = att + m_ref[l] * _dot_nt((q * El).astype(BF16), (k * El).astype(BF16))
        o_ref[:, vs] = o + _dot(att.astype(BF16), v)
        ke_t = (k * E[C:2 * C]).T.astype(BF16)
        d_col = jnp.broadcast_to(E[C - 1:C], (GLA_HK, GLA_HK)).T[:, 0:1]
        s_ref[0, h] = S * d_col + _dot(ke_t, v)


def _gla_prompt(gq, gk, gv, la, batch, lp):
    C = GLA_CHUNK
    nc = lp // C
    G, M = _gla_tables(C)
    G = jnp.asarray(G, BF16)
    M = jnp.asarray(M, F32)
    n = batch * lp
    rows = lambda w: pl.BlockSpec((C, w), lambda b, c: (b * nc + c, 0))
    full = lambda a: pl.BlockSpec(a.shape, lambda b, c: (0,) * a.ndim)
    return pl.pallas_call(
        _gla_prompt_kernel,
        grid=(batch, nc),
        in_specs=[full(G), full(M), rows(GLA_DK), rows(GLA_DK), rows(GLA_DV), rows(GLA_DK)],
        out_specs=[rows(GLA_DV), pl.BlockSpec((1, GLA_HEADS, GLA_HK, GLA_HV), lambda b, c: (b, 0, 0, 0))],
        out_shape=[jax.ShapeDtypeStruct((n, GLA_DV), F32),
                   jax.ShapeDtypeStruct((batch, GLA_HEADS, GLA_HK, GLA_HV), F32)],
        compiler_params=pltpu.CompilerParams(dimension_semantics=("parallel", "arbitrary")),
        name="gla_prompt",
    )(G, M, gq, gk, gv, la)


def _gla_sample_kernel(q_ref, k_ref, v_ref, la_ref, s0_ref, o_ref, s_ref, *, t_s):
    T = t_s
    row = lax.broadcasted_iota(jnp.int32, (T, GLA_HK), 0)
    k_fill = jnp.zeros((GLA_HK - T - 8, GLA_HK), F32)
    v_fill = jnp.zeros((GLA_HK - T, GLA_HV), F32)

    def one_seq(s, carry):
        rs = pl.ds(pl.multiple_of(s * T, T), T)
        for h in range(GLA_HEADS):
            ks = slice(h * GLA_HK, (h + 1) * GLA_HK)
            vs = slice(h * GLA_HV, (h + 1) * GLA_HV)
            q = q_ref[rs, ks]
            k = k_ref[rs, ks]
            v = v_ref[rs, vs]
            b = la_ref[rs, ks]
            sh = 1
            while sh < T:
                b = b + jnp.where(row >= sh, pltpu.roll(b, sh, 0), 0.0)
                sh *= 2
            S = s0_ref[s, h]
            o = _dot((q * jnp.exp(b)).astype(BF16), S.astype(BF16))
            for j in range(T):
                e = jnp.exp(jnp.where(row >= j, b - b[j:j + 1], NEG))
                a_col = jnp.sum(q * k[j:j + 1] * e, -1, keepdims=True)
                o = o + a_col * v[j:j + 1]
            o_ref[rs, vs] = o
            b_last = b[T - 1:T]
            ke = k * jnp.exp(b_last - b)
            kt = jnp.concatenate([ke, jnp.broadcast_to(jnp.exp(b_last), (8, GLA_HK)), k_fill], 0).T
            v_pad = jnp.concatenate([v, v_fill], 0)
            s_ref[s, h] = S * kt[:, T:T + 1] + _dot(kt.astype(BF16), v_pad.astype(BF16))
        return carry

    lax.fori_loop(0, SAMPLE_SEQS, one_seq, 0)


def _gla_sample(gq, gk, gv, la, s0, row0, t_s):
    n_seq = s0.shape[0]
    sb = SAMPLE_SEQS
    rb = sb * t_s
    b0 = row0 // rb
    rows = lambda w: pl.BlockSpec((rb, w), lambda i: (b0 + i, 0))
    st = pl.BlockSpec((sb, GLA_HEADS, GLA_HK, GLA_HV), lambda i: (i, 0, 0, 0))
    return pl.pallas_call(
        functools.partial(_gla_sample_kernel, t_s=t_s),
        grid=(n_seq // sb,),
        in_specs=[rows(GLA_DK), rows(GLA_DK), rows(GLA_DV), rows(GLA_DK), st],
        out_specs=[pl.BlockSpec((rb, GLA_DV), lambda i: (i, 0)), st],
        out_shape=[jax.ShapeDtypeStruct((n_seq * t_s, GLA_DV), F32), jax.ShapeDtypeStruct(s0.shape, F32)],
        compiler_params=pltpu.CompilerParams(dimension_semantics=("parallel",), vmem_limit_bytes=VMEM_LIMIT),
        name="gla_sample",
    )(gq, gk, gv, la, s0)


def _route(lt, valid):
    tm = lt.shape[1]
    el = lt[0:N_EXPERTS]
    gl = lt[N_EXPERTS:N_EXPERTS + N_GROUPS]
    g_max = jnp.max(gl, 0, keepdims=True)
    g_row = lax.broadcasted_iota(jnp.int32, (N_GROUPS, tm), 0)
    g_idx = jnp.min(jnp.where(gl == g_max, g_row, N_GROUPS), 0, keepdims=True)
    p_max = 1.0 / jnp.sum(jnp.exp(gl - g_max), 0, keepdims=True)
    e_row = lax.broadcasted_iota(jnp.int32, (N_EXPERTS, tm), 0)
    m1 = jnp.where(e_row // EXP_PER_GROUP == g_idx, el, -jnp.inf)
    v1 = jnp.max(m1, 0, keepdims=True)
    i1 = jnp.min(jnp.where(m1 == v1, e_row, N_EXPERTS), 0, keepdims=True)
    m2 = jnp.where(e_row == i1, -jnp.inf, m1)
    v2 = jnp.max(m2, 0, keepdims=True)
    i2 = jnp.min(jnp.where(m2 == v2, e_row, N_EXPERTS), 0, keepdims=True)
    e2 = jnp.exp(v2 - v1)
    w1 = p_max / (1.0 + e2)
    w2 = p_max * e2 / (1.0 + e2)
    o_row = lax.broadcasted_iota(jnp.int32, (8, tm), 0)
    ids = jnp.where(o_row == 0, i1, jnp.where(o_row == 1, i2, -1))
    return jnp.where(valid, ids, -1), jnp.where(o_row == 0, w1, jnp.where(o_row == 1, w2, 0.0))


def _store_row_tiles(ref, x):
    t, d = x.shape
    n = d // LANES
    for s in range(n):
        ref[pl.ds(s, t, stride=n), :] = x[:, s * LANES:(s + 1) * LANES]


def _load_row_tiles(ref, t, n=ROW_CHUNKS):
    return jnp.concatenate([ref[pl.ds(s, t, stride=n), :] for s in range(n)], axis=1)


def _merge_kernel(x_ref, yap_ref, yas_ref, ogp_ref, ogs_ref, gr_ref, gate_ref, valid_ref, eg_ref, eb_ref, ng_ref,
                  bg_ref, wa_ref, wg_ref, wo_ref, g1_ref, b1_ref, wrh_ref, wrl_ref, br_ref, h1t_ref, ids_ref, wts_ref,
                  *, prompt_tiles):
    h = _ln(x_ref[...], eg_ref[...], eb_ref[...])
    is_prompt = pl.program_id(0) < prompt_tiles
    og = jnp.where(is_prompt, ogp_ref[...], ogs_ref[...])
    ya = jnp.where(is_prompt, yap_ref[...], yas_ref[...])
    parts = []
    for hh in range(GLA_HEADS):
        o = og[:, hh * GLA_HV:(hh + 1) * GLA_HV]
        parts.append(o * lax.rsqrt(jnp.mean(o * o, -1, keepdims=True) + EPS))
    gr = gr_ref[...]
    y_gla = jnp.concatenate(parts, 1) * ng_ref[...] * (gr * _sigmoid(gr))
    a = _dot(ya.astype(BF16), wa_ref[...])
    b = _dot(y_gla.astype(BF16), wg_ref[...])
    gate = _sigmoid(gate_ref[...] + bg_ref[...])
    hm = gate[:, :D_MODEL] * a + gate[:, D_MODEL:] * b
    mix = _dot(hm.astype(BF16), wo_ref[...])
    h1 = _ln(DN_ALPHA * h + mix, g1_ref[...], b1_ref[...])
    _store_row_tiles(h1t_ref, h1)
    h_hi = h1.astype(BF16)
    h_lo = (h1 - h_hi.astype(F32)).astype(BF16)
    logits = _dot(h_hi, wrh_ref[...]) + _dot(h_lo, wrh_ref[...]) + _dot(h_hi, wrl_ref[...]) + br_ref[...]
    ids_ref[...], wts_ref[...] = _route(logits.T, valid_ref[...] > 0.0)


def _merge(x_all, ya_p, ya_s, og_p, og_s, gr, gate, valid, eg, eb, ng, bg, wa, wg, wo, g1, b1, wrh, wrl, br):
    n = x_all.shape[0]
    tm = ROW_TILE
    pt = ya_p.shape[0] // tm
    st = ya_s.shape[0] // tm
    row = lambda w: pl.BlockSpec((tm, w), lambda i: (i, 0))
    row_p = lambda w: pl.BlockSpec((tm, w), lambda i: (jnp.minimum(i, pt - 1), 0))
    row_s = lambda w: pl.BlockSpec((tm, w), lambda i: (jnp.clip(i - pt, 0, st - 1), 0))
    lane = lambda r: pl.BlockSpec((r, tm), lambda i: (0, i))
    full = lambda a: pl.BlockSpec(a.shape, lambda i: (0,) * a.ndim)
    return pl.pallas_call(
        functools.partial(_merge_kernel, prompt_tiles=pt),
        grid=(n // tm,),
        in_specs=[row(D_MODEL), row_p(Q_W), row_s(Q_W), row_p(GLA_DV), row_s(GLA_DV), row(GLA_DV), row(2 * D_MODEL),
                  lane(1), full(eg), full(eb), full(ng), full(bg), full(wa), full(wg), full(wo), full(g1), full(b1),
                  full(wrh), full(wrl), full(br)],
        out_specs=[pl.BlockSpec((tm * ROW_CHUNKS, LANES), lambda i: (i, 0)), lane(8), lane(8)],
        out_shape=[jax.ShapeDtypeStruct((n * ROW_CHUNKS, LANES), F32),
                   jax.ShapeDtypeStruct((8, n), jnp.int32), jax.ShapeDtypeStruct((8, n), F32)],
        compiler_params=pltpu.CompilerParams(dimension_semantics=("parallel",), vmem_limit_bytes=VMEM_LIMIT),
        name="merge",
    )(x_all, ya_p, ya_s, og_p, og_s, gr, gate, valid, eg, eb, ng, bg, wa, wg, wo, g1, b1, wrh, wrl, br)


def _gather_row_tiles(idx_ref, idx0, src_hbm, dst, sem, n):
    def body(r, carry):
        t = idx_ref[idx0 + r]
        pltpu.make_async_copy(src_hbm.at[pl.ds(pl.multiple_of(t * ROW_CHUNKS, ROW_CHUNKS), ROW_CHUNKS), :],
                              dst.at[pl.ds(pl.multiple_of(r * ROW_CHUNKS, ROW_CHUNKS), ROW_CHUNKS), :], sem).start()
        return carry
    lax.fori_loop(0, n, body, 0, unroll=8)


def _wait_row_tiles(src_hbm, dst, sem, n):
    pltpu.make_async_copy(src_hbm.at[pl.ds(0, n * ROW_CHUNKS), :], dst, sem).wait()


def _ffn_kernel(te_ref, nt_ref, src_ref, h_hbm, wg_ref, wu_ref, wd_ref, out_ref, xbuf, sem, wgb, wub, wdb):
    i = pl.program_id(0)
    nt = nt_ref[0]
    T = FFN_TILE

    @pl.when(i == 0)
    def _():
        _gather_row_tiles(src_ref, 0, h_hbm, xbuf.at[0], sem.at[0], T)

    @pl.when(i + 1 < nt)
    def _():
        nxt = (i + 1) % 2
        _gather_row_tiles(src_ref, (i + 1) * T, h_hbm, xbuf.at[nxt], sem.at[nxt], T)

    @pl.when(i < nt)
    def _():
        slot = i % 2
        _wait_row_tiles(h_hbm, xbuf.at[slot], sem.at[slot], T)

        @pl.when((i == 0) | (te_ref[i] != te_ref[jnp.maximum(i - 1, 0)]))
        def _():
            wgb[...] = wg_ref[0].astype(BF16)
            wub[...] = wu_ref[0].astype(BF16)
            wdb[...] = wd_ref[0].astype(BF16)

        x = _load_row_tiles(xbuf.at[slot], T).astype(BF16)
        g = _dot(x, wgb[...])
        u = _dot(x, wub[...])
        _store_row_tiles(out_ref, _dot((g * _sigmoid(g) * u).astype(BF16), wdb[...]))

    @pl.when(i >= nt)
    def _():
        out_ref[...] = jnp.zeros_like(out_ref)


def _ffn(tile_expert, n_tiles, src_rows, h1t, w_g, w_u, w_d):
    T = FFN_TILE
    max_tiles = tile_expert.shape[0]
    wspec = lambda a: pl.BlockSpec((1,) + a.shape[1:], lambda i, te, nt, src: (te[i], 0, 0))
    return pl.pallas_call(
        _ffn_kernel,
        grid_spec=pltpu.PrefetchScalarGridSpec(
            num_scalar_prefetch=3,
            grid=(max_tiles,),
            in_specs=[pl.BlockSpec(memory_space=pl.ANY), wspec(w_g), wspec(w_u), wspec(w_d)],
            out_specs=pl.BlockSpec((T * ROW_CHUNKS, LANES), lambda i, te, nt, src: (i, 0)),
            scratch_shapes=[pltpu.VMEM((2, T * ROW_CHUNKS, LANES), F32), pltpu.SemaphoreType.DMA((2,)),
                            pltpu.VMEM((D_MODEL, D_EXPERT), BF16), pltpu.VMEM((D_MODEL, D_EXPERT), BF16),
                            pltpu.VMEM((D_EXPERT, D_MODEL), BF16)]),
        out_shape=jax.ShapeDtypeStruct((max_tiles * T * ROW_CHUNKS, LANES), F32),
        compiler_params=pltpu.CompilerParams(dimension_semantics=("arbitrary",)),
        name="ffn",
    )(tile_expert, n_tiles, src_rows, h1t, w_g, w_u, w_d)


def _combine_kernel(dest_ref, h_ref, w_ref, g_ref, b_ref, ys_hbm, y_ref, buf, sem, *, blk, n_rows):
    i = pl.program_id(0)
    n = pl.num_programs(0)
    T = CMB_TILE

    def gather(step, slot):
        for kk in range(2):
            _gather_row_tiles(dest_ref, kk * n_rows + blk(step) * T, ys_hbm, buf.at[slot, kk], sem.at[slot], T)

    @pl.when(i == 0)
    def _():
        gather(0, 0)

    @pl.when(i + 1 < n)
    def _():
        gather(i + 1, (i + 1) % 2)

    slot = i % 2
    for kk in range(2):
        _wait_row_tiles(ys_hbm, buf.at[slot, kk], sem.at[slot], T)
    w = w_ref[...]
    ff = w[:, 0:1] * _load_row_tiles(buf.at[slot, 0], T) + w[:, 1:2] * _load_row_tiles(buf.at[slot, 1], T)
    y_ref[...] = _ln(DN_ALPHA * _load_row_tiles(h_ref, T) + ff, g_ref[...], b_ref[...])


def _combine(dest, h1t, wts_t, ys, g2, b2, n_out, first_block, blocks_per_batch, skip_blocks):
    T = CMB_TILE
    n_rows = h1t.shape[0] // ROW_CHUNKS
    if skip_blocks:
        blk = lambda i: first_block + i + (i // blocks_per_batch + 1) * skip_blocks
    else:
        blk = lambda i: first_block + i
    full = lambda a: pl.BlockSpec(a.shape, lambda i, d: (0,) * a.ndim)
    return pl.pallas_call(
        functools.partial(_combine_kernel, blk=blk, n_rows=n_rows),
        grid_spec=pltpu.PrefetchScalarGridSpec(
            num_scalar_prefetch=1,
            grid=(n_out // T,),
            in_specs=[pl.BlockSpec((T * ROW_CHUNKS, LANES), lambda i, d: (blk(i), 0)),
                      pl.BlockSpec((T, 2), lambda i, d: (blk(i), 0)),
                      full(g2), full(b2), pl.BlockSpec(memory_space=pl.ANY)],
            out_specs=pl.BlockSpec((T, D_MODEL), lambda i, d: (i, 0)),
            scratch_shapes=[pltpu.VMEM((2, 2, T * ROW_CHUNKS, LANES), F32), pltpu.SemaphoreType.DMA((2,))]),
        out_shape=jax.ShapeDtypeStruct((n_out, D_MODEL), F32),
        compiler_params=pltpu.CompilerParams(dimension_semantics=("arbitrary",)),
        name="combine",
    )(dest, h1t, wts_t, g2, b2, ys)


def _dispatch_plan(ids, n_rows, max_tiles):
    T = FFN_TILE
    e_flat = ids.reshape(-1)
    onehot = (e_flat[:, None] == jnp.arange(N_EXPERTS, dtype=jnp.int32)[None, :]).astype(jnp.int32)
    csum = jnp.cumsum(onehot, axis=0)
    counts = csum[-1]
    rank = jnp.sum(onehot * csum, axis=1) - 1
    tiles_e = (counts + T - 1) // T
    tile_end = jnp.cumsum(tiles_e)
    tile_start = tile_end - tiles_e
    n_tiles = tile_end[-1]
    n_slots = max_tiles * T
    start_of = jnp.sum(onehot * tile_start[None, :], axis=1) * T
    dest = jnp.where(e_flat >= 0, start_of + rank, n_slots).astype(jnp.int32)
    rows = jnp.tile(jnp.arange(n_rows, dtype=jnp.int32), 2)
    src = jnp.zeros((n_slots,), jnp.int32).at[dest].set(rows, mode="drop")
    te = jnp.sum((jnp.arange(max_tiles, dtype=jnp.int32)[:, None] >= tile_end[None, :]).astype(jnp.int32), axis=1)
    te_last = jnp.take(te, jnp.maximum(n_tiles - 1, 0))
    te = jnp.where(jnp.arange(max_tiles) < n_tiles, te, te_last).astype(jnp.int32)
    dest = jnp.minimum(dest, n_slots - 1)
    return te, n_tiles.reshape(1).astype(jnp.int32), src, dest


def kernel(x_prompt, x_sample, state_swa_k, state_swa_v, state_gla, meta_tokens, ln_emb_g, ln_emb_b, w_in, b_gate, attn_sink, w_alpha2, b_alpha, gla_norm_g, w_attn_br, w_gla_br, w_out, ln1_g, ln1_b, w_router_group, b_router_group, w_router_expert, b_router_expert, w_exp_gate, w_exp_up, w_exp_down, ln2_g, ln2_b):
    B, seq, _ = x_prompt.shape
    n_seq, t_s, _ = x_sample.shape
    depth = w_in.shape[0]
    assert depth == 1 and seq % ATT_BLOCK == 0 and t_s == 8
    lp = SKIP_ROWS + seq
    NP, NS = B * lp, n_seq * t_s
    NR = NP + NS
    assert NP % ROW_TILE == 0 and NS % ROW_TILE == 0 and n_seq % SAMPLE_SEQS == 0
    l = 0
    row2 = lambda a: a.reshape(1, -1)

    xp = jnp.concatenate([jnp.zeros((B, FRONT_PAD, D_MODEL), F32),
                          jnp.broadcast_to(meta_tokens[None], (B, N_META, D_MODEL)), x_prompt], axis=1)
    x_all = jnp.concatenate([xp.reshape(NP, D_MODEL), x_sample.reshape(NS, D_MODEL)], axis=0)
    pos = np.arange(NR)
    in_prompt = pos < NP
    keep = jnp.asarray(~(in_prompt & (pos % lp < FRONT_PAD)), F32).reshape(NR, 1)
    moe_valid = jnp.asarray(~(in_prompt & (pos % lp < SKIP_ROWS)), F32).reshape(1, NR)

    wi = w_in[l]
    c_ga = sum((Q_W, KV_W, KV_W, GLA_DK, GLA_DK, GLA_DV))
    w_bf = jnp.concatenate([wi[:, :c_ga], wi[:, c_ga + GLA_RANK:], wi[:, c_ga:c_ga + GLA_RANK],
                            jnp.zeros((D_MODEL, 128 - GLA_RANK), F32)], axis=1).astype(BF16)
    wa2_bf = jnp.concatenate([w_alpha2[l], jnp.zeros((128 - GLA_RANK, GLA_DK), F32)], axis=0).astype(BF16)

    q, k, v, gq, gk, gv, la, gr, gate = _inproj(x_all, keep, row2(ln_emb_g), row2(ln_emb_b), w_bf, wa2_bf,
                                                row2(b_alpha[l]))

    sink = attn_sink[l]
    ya_p = _swa_prompt(sink, q, k, v, B, lp)
    buf_k = state_swa_k[l].reshape(n_seq, WINDOW, KV_W)
    buf_v = state_swa_v[l].reshape(n_seq, WINDOW, KV_W)
    ya_s, nk_s, nv_s = _swa_sample(sink, q, k, v, buf_k, buf_v, NP, t_s)

    og_p, s_p = _gla_prompt(gq, gk, gv, la, B, lp)
    og_s, s_s = _gla_sample(gq, gk, gv, la, state_gla[l], NP, t_s)

    wr = jnp.concatenate([w_router_expert[l], w_router_group[l],
                          jnp.zeros((D_MODEL, LANES - N_EXPERTS - N_GROUPS), F32)], axis=1)
    br = jnp.concatenate([b_router_expert[l], b_router_group[l],
                          jnp.zeros((LANES - N_EXPERTS - N_GROUPS,), F32)]).reshape(1, LANES)
    wr_hi = wr.astype(BF16)
    h1t, ids, wts = _merge(x_all, ya_p, ya_s, og_p, og_s, gr, gate, moe_valid, row2(ln_emb_g), row2(ln_emb_b),
                           row2(gla_norm_g[l]), row2(b_gate[l]), w_attn_br[l].astype(BF16),
                           w_gla_br[l].astype(BF16), w_out[l].astype(BF16), row2(ln1_g[l]), row2(ln1_b[l]),
                           wr_hi, (wr - wr_hi.astype(F32)).astype(BF16), br)

    n_tok = B * seq + NS
    max_tiles = (2 * n_tok) // FFN_TILE + N_EXPERTS
    te, n_tiles, src, dest = _dispatch_plan(ids[0:2], NR, max_tiles)
    ys = _ffn(te, n_tiles, src, h1t, w_exp_gate[l], w_exp_up[l], w_exp_down[l])

    wts_t = wts[0:2].T
    g2, b2 = row2(ln2_g[l]), row2(ln2_b[l])
    skip_blocks = SKIP_ROWS // CMB_TILE
    y_p = _combine(dest, h1t, wts_t, ys, g2, b2, B * seq, 0, seq // CMB_TILE, skip_blocks)
    y_s = _combine(dest, h1t, wts_t, ys, g2, b2, NS, NP // CMB_TILE, 1, 0)

    kv_shape = (B, lp, N_KV, HEAD_DIM)
    k_p = k[:NP].reshape(kv_shape)[:, -WINDOW:]
    v_p = v[:NP].reshape(kv_shape)[:, -WINDOW:]
    return (y_p.reshape(B, seq, D_MODEL), y_s.reshape(n_seq, t_s, D_MODEL),
            k_p[None], v_p[None], s_p[None],
            nk_s.reshape(1, n_seq, WINDOW, N_KV, HEAD_DIM), nv_s.reshape(1, n_seq, WINDOW, N_KV, HEAD_DIM),
            s_s[None])
```

```python
import functools

import numpy as np
import jax
import jax.numpy as jnp
from jax import lax
from jax.experimental import pallas as pl
from jax.experimental.pallas import tpu as pltpu

F32 = jnp.float32
BF16 = jnp.bfloat16

D_MODEL = 1024
N_META = 16
HEAD_DIM = 64
N_HEADS = 8
N_KV = 2
Q_PER_KV = 4
WINDOW = 128
ATT_BLOCK = 128
GLA_HEADS = 4
GLA_HK = 128
GLA_HV = 256
GLA_DK = GLA_HEADS * GLA_HK
GLA_DV = GLA_HEADS * GLA_HV
GLA_RANK = 16
GLA_TAU = 16.0
GLA_CHUNK = 64
N_GROUPS = 4
EXP_PER_GROUP = 8
N_EXPERTS = 32
D_EXPERT = 256
DN_ALPHA = 2.0 ** 0.25
EPS = 1e-5
NEG = -1e30

FRONT_PAD = (-N_META) % ATT_BLOCK
SKIP_ROWS = FRONT_PAD + N_META

Q_W, KV_W = N_HEADS * HEAD_DIM, N_KV * HEAD_DIM
SEG = {}
_o = 0
for _n, _w in (("q", Q_W), ("k", KV_W), ("v", KV_W), ("gq", GLA_DK), ("gk", GLA_DK), ("gv", GLA_DV),
               ("gr", GLA_DV), ("gate", 2 * D_MODEL), ("ga", 128)):
    SEG[_n] = (_o, _o + _w)
    _o += _w
W_IN_COLS = _o

ROW_TILE = 256
FFN_TILE = 256
CMB_TILE = 128
SAMPLE_SEQS = 8
LANES = 128
ROW_CHUNKS = D_MODEL // LANES
VMEM_LIMIT = 56 * 1024 * 1024


def _ln(x, g, b):
    mu = jnp.mean(x, -1, keepdims=True)
    xc = x - mu
    var = jnp.mean(xc * xc, -1, keepdims=True)
    return xc * lax.rsqrt(var + EPS) * g + b


def _sigmoid(x):
    return 0.5 * jnp.tanh(0.5 * x) + 0.5


def _dot(a, b):
    return jnp.dot(a, b, preferred_element_type=F32)


def _dot_nt(a, b):
    return lax.dot_general(a, b, (((1,), (1,)), ((), ())), preferred_element_type=F32)


def _inproj_kernel(x_ref, keep_ref, g_ref, b_ref, w_ref, wa2_ref, ba_ref,
                   q_ref, k_ref, v_ref, gq_ref, gk_ref, gv_ref, la_ref, gr_ref, gate_ref):
    hb = _ln(x_ref[...], g_ref[...], b_ref[...]).astype(BF16)
    keep = keep_ref[...]

    def seg(name):
        a, b = SEG[name]
        return _dot(hb, w_ref[:, a:b])

    q_ref[...] = seg("q")
    k_ref[...] = seg("k")
    v_ref[...] = seg("v")
    gq_ref[...] = seg("gq") * (GLA_HK ** -0.5)
    gk_ref[...] = seg("gk") * keep
    gv_ref[...] = seg("gv") * keep
    gr_ref[...] = seg("gr")
    gate_ref[...] = seg("gate")
    z = _dot(seg("ga").astype(BF16), wa2_ref[...]) + ba_ref[...]
    la = (jnp.minimum(z, 0.0) - jnp.log(1.0 + jnp.exp(-jnp.abs(z)))) * (1.0 / GLA_TAU)
    la_ref[...] = la * keep


def _inproj(x_all, keep, ln_g, ln_b, w_bf, wa2_bf, b_alpha):
    n = x_all.shape[0]
    tm = ROW_TILE
    widths = [Q_W, KV_W, KV_W, GLA_DK, GLA_DK, GLA_DV, GLA_DK, GLA_DV, 2 * D_MODEL]
    row = lambda w: pl.BlockSpec((tm, w), lambda i: (i, 0))
    full = lambda a: pl.BlockSpec(a.shape, lambda i: (0,) * a.ndim)
    return pl.pallas_call(
        _inproj_kernel,
        grid=(n // tm,),
        in_specs=[row(D_MODEL), row(1), full(ln_g), full(ln_b), full(w_bf), full(wa2_bf), full(b_alpha)],
        out_specs=[row(w) for w in widths],
        out_shape=[jax.ShapeDtypeStruct((n, w), F32) for w in widths],
        compiler_params=pltpu.CompilerParams(dimension_semantics=("parallel",), vmem_limit_bytes=VMEM_LIMIT),
        name="inproj",
    )(x_all, keep, ln_g, ln_b, w_bf, wa2_bf, b_alpha)


def _softmax_pv(s, sink, vv):
    m = jnp.maximum(jnp.max(s, -1, keepdims=True), sink)
    p = jnp.exp(s - m)
    l = jnp.sum(p, -1, keepdims=True) + jnp.exp(sink - m)
    return _dot(p.astype(BF16), vv) / l


def _swa_prompt_kernel(sink_ref, q_ref, kp_ref, kc_ref, vp_ref, vc_ref, o_ref):
    j = pl.program_id(1)
    q = q_ref[...]
    kb = jnp.concatenate([kp_ref[...], kc_ref[...]], 0)
    vb = jnp.concatenate([vp_ref[...], vc_ref[...]], 0)
    r = lax.broadcasted_iota(jnp.int32, (ATT_BLOCK, 2 * ATT_BLOCK), 0)
    c = lax.broadcasted_iota(jnp.int32, (ATT_BLOCK, 2 * ATT_BLOCK), 1)
    dist = r - c + ATT_BLOCK
    k_pos = (j - 1) * ATT_BLOCK + c - FRONT_PAD
    mask = (dist >= 0) & (dist < WINDOW) & (k_pos >= 0)
    distf = dist.astype(F32)
    for h in range(N_HEADS):
        kv = h // Q_PER_KV
        qh = q[:, h * HEAD_DIM:(h + 1) * HEAD_DIM].astype(BF16)
        kk = kb[:, kv * HEAD_DIM:(kv + 1) * HEAD_DIM].astype(BF16)
        vv = vb[:, kv * HEAD_DIM:(kv + 1) * HEAD_DIM].astype(BF16)
        s = _dot_nt(qh, kk) * (HEAD_DIM ** -0.5) - (2.0 ** -(h + 1)) * distf
        s = jnp.where(mask, s, NEG)
        o_ref[:, h * HEAD_DIM:(h + 1) * HEAD_DIM] = _softmax_pv(s, sink_ref[h], vv)


def _swa_prompt(sink, q, k, v, batch, lp):
    nb = lp // ATT_BLOCK
    n = batch * lp
    cur = lambda w: pl.BlockSpec((ATT_BLOCK, w), lambda b, j: (b * nb + j, 0))
    prev = lambda w: pl.BlockSpec((ATT_BLOCK, w), lambda b, j: (b * nb + jnp.maximum(j - 1, 0), 0))
    return pl.pallas_call(
        _swa_prompt_kernel,
        grid=(batch, nb),
        in_specs=[pl.BlockSpec(memory_space=pltpu.SMEM), cur(Q_W), prev(KV_W), cur(KV_W), prev(KV_W), cur(KV_W)],
        out_specs=cur(Q_W),
        out_shape=jax.ShapeDtypeStruct((n, Q_W), F32),
        compiler_params=pltpu.CompilerParams(dimension_semantics=("parallel", "parallel")),
        name="swa_prompt",
    )(sink, q, k, k, v, v)


def _swa_sample_kernel(sink_ref, q_ref, k_ref, v_ref, bk_ref, bv_ref, o_ref, nk_ref, nv_ref, *, t_s):
    nbuf = WINDOW
    span = 2 * WINDOW
    rows = Q_PER_KV * t_s
    r = lax.broadcasted_iota(jnp.int32, (rows, span), 0)
    c = lax.broadcasted_iota(jnp.int32, (rows, span), 1)
    t = r % t_s
    dist = t + nbuf - c
    mask = (dist >= 0) & (dist < WINDOW) & (c < nbuf + t_s)
    distf = dist.astype(F32)
    g_col = lax.broadcasted_iota(jnp.int32, (rows, 1), 0) // t_s
    fill = jnp.zeros((span - nbuf - t_s, KV_W), F32)

    def one_seq(s, carry):
        rs = pl.ds(pl.multiple_of(s * t_s, t_s), t_s)
        q = q_ref[rs, :]
        k_new = k_ref[rs, :]
        v_new = v_ref[rs, :]
        bk = bk_ref[s]
        bv = bv_ref[s]
        k_all = jnp.concatenate([bk, k_new, fill], 0)
        v_all = jnp.concatenate([bv, v_new, fill], 0)
        for kv in range(N_KV):
            qg = jnp.concatenate(
                [q[:, (kv * Q_PER_KV + g) * HEAD_DIM:(kv * Q_PER_KV + g + 1) * HEAD_DIM] for g in range(Q_PER_KV)], 0)
            kk = k_all[:, kv * HEAD_DIM:(kv + 1) * HEAD_DIM].astype(BF16)
            vv = v_all[:, kv * HEAD_DIM:(kv + 1) * HEAD_DIM].astype(BF16)
            slope = jnp.zeros((rows, 1), F32)
            sink = jnp.zeros((rows, 1), F32)
            for g in range(Q_PER_KV):
                h = kv * Q_PER_KV + g
                slope = jnp.where(g_col == g, 2.0 ** -(h + 1), slope)
                sink = jnp.where(g_col == g, sink_ref[h], sink)
            sc = _dot_nt(qg.astype(BF16), kk) * (HEAD_DIM ** -0.5) - slope * distf
            sc = jnp.where(mask, sc, NEG)
            o = _softmax_pv(sc, sink, vv)
            for g in range(Q_PER_KV):
                h = kv * Q_PER_KV + g
                o_ref[rs, h * HEAD_DIM:(h + 1) * HEAD_DIM] = o[g * t_s:(g + 1) * t_s]
        nk_ref[s, 0:nbuf - t_s, :] = bk[t_s:, :]
        nk_ref[s, nbuf - t_s:nbuf, :] = k_new
        nv_ref[s, 0:nbuf - t_s, :] = bv[t_s:, :]
        nv_ref[s, nbuf - t_s:nbuf, :] = v_new
        return carry

    lax.fori_loop(0, SAMPLE_SEQS, one_seq, 0)


def _swa_sample(sink, q, k, v, buf_k, buf_v, row0, t_s):
    n_seq = buf_k.shape[0]
    sb = SAMPLE_SEQS
    rb = sb * t_s
    b0 = row0 // rb
    rows = lambda w: pl.BlockSpec((rb, w), lambda i: (b0 + i, 0))
    bufs = pl.BlockSpec((sb, WINDOW, KV_W), lambda i: (i, 0, 0))
    return pl.pallas_call(
        functools.partial(_swa_sample_kernel, t_s=t_s),
        grid=(n_seq // sb,),
        in_specs=[pl.BlockSpec(memory_space=pltpu.SMEM), rows(Q_W), rows(KV_W), rows(KV_W), bufs, bufs],
        out_specs=[pl.BlockSpec((rb, Q_W), lambda i: (i, 0)), bufs, bufs],
        out_shape=[jax.ShapeDtypeStruct((n_seq * t_s, Q_W), F32),
                   jax.ShapeDtypeStruct(buf_k.shape, F32), jax.ShapeDtypeStruct(buf_v.shape, F32)],
        compiler_params=pltpu.CompilerParams(dimension_semantics=("parallel",)),
        name="swa_sample",
    )(sink, q, k, v, buf_k, buf_v)


def _gla_tables(chunk):
    t = np.arange(chunk)[:, None]
    u = np.arange(chunk)[None, :]
    masks = []
    w = chunk // 2
    while w >= 1:
        masks.append((t // (2 * w) == u // (2 * w)) & ((t // w) % 2 == 1) & ((u // w) % 2 == 0))
        w //= 2
    return (u <= t).astype(np.float32), np.stack(masks, 0).astype(np.float32)


def _level_exponents(b, la, w):
    C = b.shape[0]
    row = lax.broadcasted_iota(jnp.int32, b.shape, 0)
    if w >= 4:
        pieces = [jnp.broadcast_to(b[p + w - 1:p + w], (2 * w, b.shape[1])) for p in range(0, C, 2 * w)]
        ref = pieces[0] if len(pieces) == 1 else jnp.concatenate(pieces, 0)
        return jnp.where((row & w) != 0, b - ref, ref - b)
    if w == 2:
        m = row & 3
        nxt = pltpu.roll(la, C - 1, 0)
        prv = pltpu.roll(la, 1, 0)
        return jnp.where(m == 2, la, jnp.where(m == 3, la + prv, jnp.where(m == 0, nxt, 0.0)))
    return jnp.where((row & 1) != 0, la, 0.0)


def _split3(x):
    hi = x.astype(BF16)
    r1 = x - hi.astype(F32)
    mid = r1.astype(BF16)
    lo = (r1 - mid.astype(F32)).astype(BF16)
    return hi, mid, lo


def _gla_intra_kernel(g_ref, m_ref, q_ref, k_ref, v_ref, la_ref, o_ref, qe_ref, ke_ref, vt_ref, d_ref):
    C = GLA_CHUNK
    n_lvl = m_ref.shape[0]
    G = g_ref[...]
    eye = (lax.broadcasted_iota(jnp.int32, (C, C), 0) == lax.broadcasted_iota(jnp.int32, (C, C), 1)).astype(F32)
    la = la_ref[...]
    hi, mid, lo = _split3(la)
    b = _dot(G, hi) + _dot(G, mid) + _dot(G, lo)
    b_last = b[C - 1:C]
    q_all = q_ref[...]
    k_all = k_ref[...]
    qe_ref[0] = (q_all * jnp.exp(b)).astype(BF16)
    ke_ref[0] = (k_all * jnp.exp(b_last - b)).astype(BF16)
    d_ref[0, 0] = jnp.broadcast_to(jnp.exp(b_last), (8, GLA_DK))
    q_lvl, k_lvl = [], []
    for l in range(n_lvl):
        El = jnp.exp(_level_exponents(b, la, C >> (l + 1)))
        q_lvl.append((q_all * El).astype(BF16))
        k_lvl.append((k_all * El).astype(BF16))
    for h in range(GLA_HEADS):
        ks = slice(h * GLA_HK, (h + 1) * GLA_HK)
        vs = slice(h * GLA_HV, (h + 1) * GLA_HV)
        v = v_ref[:, vs]
        att = eye * jnp.sum(q_all[:, ks] * k_all[:, ks], -1, keepdims=True)
        for l in range(n_lvl):
            att = att + m_ref[l] * _dot_nt(q_lvl[l][:, ks], k_lvl[l][:, ks])
        o_ref[0, :, vs] = _dot(att.astype(BF16), v.astype(BF16))
        vt_ref[0, 0, vs, :] = v.T.astype(BF16)


def _gla_inter_kernel(o_ref, qe_ref, ke_ref, vt_ref, d_ref, og_ref, s_ref, st_ref):
    c = pl.program_id(0)
    batch = o_ref.shape[0]

    @pl.when(c == 0)
    def _():
        st_ref[...] = jnp.zeros_like(st_ref)

    for b in range(batch):
        for h in range(GLA_HEADS):
            ks = slice(h * GLA_HK, (h + 1) * GLA_HK)
            vs = slice(h * GLA_HV, (h + 1) * GLA_HV)
            st = st_ref[b * GLA_HEADS + h]
            og_ref[b, :, vs] = o_ref[b, :, vs] + _dot_nt(qe_ref[b, :, ks], st.astype(BF16))
            st_ref[b * GLA_HEADS + h] = st * d_ref[b, 0, 0:1, ks] + _dot(vt_ref[b, 0, vs, :], ke_ref[b, :, ks])

    @pl.when(c == pl.num_programs(0) - 1)
    def _():
        for b in range(batch):
            for h in range(GLA_HEADS):
                s_ref[b, h] = st_ref[b * GLA_HEADS + h].T


def _gla_prompt(gq, gk, gv, la, batch, lp):
    C = GLA_CHUNK
    nc = lp // C
    G, M = _gla_tables(C)
    G = jnp.asarray(G, BF16)
    M = jnp.asarray(M, F32)
    rows = lambda w: pl.BlockSpec((C, w), lambda b, c: (b * nc + c, 0))
    rows3 = lambda w: pl.BlockSpec((1, C, w), lambda b, c: (b, c, 0))
    full = lambda a: pl.BlockSpec(a.shape, lambda b, c: (0,) * a.ndim)
    o_intra, qe, ke, vt, d = pl.pallas_call(
        _gla_intra_kernel,
        grid=(batch, nc),
        in_specs=[full(G), full(M), rows(GLA_DK), rows(GLA_DK), rows(GLA_DV), rows(GLA_DK)],
        out_specs=[rows3(GLA_DV), rows3(GLA_DK), rows3(GLA_DK),
                   pl.BlockSpec((1, 1, GLA_DV, C), lambda b, c: (b, c, 0, 0)),
                   pl.BlockSpec((1, 1, 8, GLA_DK), lambda b, c: (b, c, 0, 0))],
        out_shape=[jax.ShapeDtypeStruct((batch, lp, GLA_DV), F32),
                   jax.ShapeDtypeStruct((batch, lp, GLA_DK), BF16), jax.ShapeDtypeStruct((batch, lp, GLA_DK), BF16),
                   jax.ShapeDtypeStruct((batch, nc, GLA_DV, C), BF16),
                   jax.ShapeDtypeStruct((batch, nc, 8, GLA_DK), F32)],
        compiler_params=pltpu.CompilerParams(dimension_semantics=("parallel", "parallel")),
        name="gla_intra",
    )(G, M, gq, gk, gv, la)
    chunk = lambda w: pl.BlockSpec((batch, C, w), lambda c: (0, c, 0))
    og, s_fin = pl.pallas_call(
        _gla_inter_kernel,
        grid=(nc,),
        in_specs=[chunk(GLA_DV), chunk(GLA_DK), chunk(GLA_DK),
                  pl.BlockSpec((batch, 1, GLA_DV, C), lambda c: (0, c, 0, 0)),
                  pl.BlockSpec((batch, 1, 8, GLA_DK), lambda c: (0, c, 0, 0))],
        out_specs=[chunk(GLA_DV), pl.BlockSpec((batch, GLA_HEADS, GLA_HK, GLA_HV), lambda c: (0, 0, 0, 0))],
        out_shape=[jax.ShapeDtypeStruct((batch, lp, GLA_DV), F32),
                   jax.ShapeDtypeStruct((batch, GLA_HEADS, GLA_HK, GLA_HV), F32)],
        scratch_shapes=[pltpu.VMEM((batch * GLA_HEADS, GLA_HV, GLA_HK), F32)],
        compiler_params=pltpu.CompilerParams(dimension_semantics=("arbitrary",)),
        name="gla_inter",
    )(o_intra, qe, ke, vt, d)
    return og.reshape(batch * lp, GLA_DV), s_fin


def _gla_sample_kernel(q_ref, k_ref, v_ref, la_ref, s0_ref, o_ref, s_ref, *, t_s):
    T = t_s
    row = lax.broadcasted_iota(jnp.int32, (T, GLA_HK), 0)
    k_fill = jnp.zeros((GLA_HK - T - 8, GLA_HK), F32)
    v_fill = jnp.zeros((GLA_HK - T, GLA_HV), F32)

    def one_seq(s, carry):
        rs = pl.ds(pl.multiple_of(s * T, T), T)
        for h in range(GLA_HEADS):
            ks = slice(h * GLA_HK, (h + 1) * GLA_HK)
            vs = slice(h * GLA_HV, (h + 1) * GLA_HV)
            q = q_ref[rs, ks]
            k = k_ref[rs, ks]
            v = v_ref[rs, vs]
            b = la_ref[rs, ks]
            sh = 1
            while sh < T:
                b = b + jnp.where(row >= sh, pltpu.roll(b, sh, 0), 0.0)
                sh *= 2
            S = s0_ref[s, h]
            o = _dot((q * jnp.exp(b)).astype(BF16), S.astype(BF16))
            for j in range(T):
                e = jnp.exp(jnp.where(row >= j, b - b[j:j + 1], NEG))
                a_col = jnp.sum(q * k[j:j + 1] * e, -1, keepdims=True)
                o = o + a_col * v[j:j + 1]
            o_ref[rs, vs] = o
            b_last = b[T - 1:T]
            ke = k * jnp.exp(b_last - b)
            kt = jnp.concatenate([ke, jnp.broadcast_to(jnp.exp(b_last), (8, GLA_HK)), k_fill], 0).T
            v_pad = jnp.concatenate([v, v_fill], 0)
            s_ref[s, h] = S * kt[:, T:T + 1] + _dot(kt.astype(BF16), v_pad.astype(BF16))
        return carry

    lax.fori_loop(0, SAMPLE_SEQS, one_seq, 0)


def _gla_sample(gq, gk, gv, la, s0, row0, t_s):
    n_seq = s0.shape[0]
    sb = SAMPLE_SEQS
    rb = sb * t_s
    b0 = row0 // rb
    rows = lambda w: pl.BlockSpec((rb, w), lambda i: (b0 + i, 0))
    st = pl.BlockSpec((sb, GLA_HEADS, GLA_HK, GLA_HV), lambda i: (i, 0, 0, 0))
    return pl.pallas_call(
        functools.partial(_gla_sample_kernel, t_s=t_s),
        grid=(n_seq // sb,),
        in_specs=[rows(GLA_DK), rows(GLA_DK), rows(GLA_DV), rows(GLA_DK), st],
        out_specs=[pl.BlockSpec((rb, GLA_DV), lambda i: (i, 0)), st],
        out_shape=[jax.ShapeDtypeStruct((n_seq * t_s, GLA_DV), F32), jax.ShapeDtypeStruct(s0.shape, F32)],
        compiler_params=pltpu.CompilerParams(dimension_semantics=("parallel",), vmem_limit_bytes=VMEM_LIMIT),
        name="gla_sample",
    )(gq, gk, gv, la, s0)


def _route(lt, valid):
    tm = lt.shape[1]
    el = lt[0:N_EXPERTS]
    gl = lt[N_EXPERTS:N_EXPERTS + N_GROUPS]
    g_max = jnp.max(gl, 0, keepdims=True)
    g_row = lax.broadcasted_iota(jnp.int32, (N_GROUPS, tm), 0)
    g_idx = jnp.min(jnp.where(gl == g_max, g_row, N_GROUPS), 0, keepdims=True)
    p_max = 1.0 / jnp.sum(jnp.exp(gl - g_max), 0, keepdims=True)
    e_row = lax.broadcasted_iota(jnp.int32, (N_EXPERTS, tm), 0)
    m1 = jnp.where(e_row // EXP_PER_GROUP == g_idx, el, -jnp.inf)
    v1 = jnp.max(m1, 0, keepdims=True)
    i1 = jnp.min(jnp.where(m1 == v1, e_row, N_EXPERTS), 0, keepdims=True)
    m2 = jnp.where(e_row == i1, -jnp.inf, m1)
    v2 = jnp.max(m2, 0, keepdims=True)
    i2 = jnp.min(jnp.where(m2 == v2, e_row, N_EXPERTS), 0, keepdims=True)
    e2 = jnp.exp(v2 - v1)
    w1 = p_max / (1.0 + e2)
    w2 = p_max * e2 / (1.0 + e2)
    o_row = lax.broadcasted_iota(jnp.int32, (8, tm), 0)
    ids = jnp.where(o_row == 0, i1, jnp.where(o_row == 1, i2, -1))
    return jnp.where(valid, ids, -1), jnp.where(o_row == 0, w1, jnp.where(o_row == 1, w2, 0.0))


def _store_row_tiles(ref, x):
    t, d = x.shape
    n = d // LANES
    for s in range(n):
        ref[pl.ds(s, t, stride=n), :] = x[:, s * LANES:(s + 1) * LANES]


def _load_row_tiles(ref, t, n=ROW_CHUNKS):
    return jnp.concatenate([ref[pl.ds(s, t, stride=n), :] for s in range(n)], axis=1)


def _merge_kernel(x_ref, yap_ref, yas_ref, ogp_ref, ogs_ref, gr_ref, gate_ref, valid_ref, eg_ref, eb_ref, ng_ref,
                  bg_ref, wa_ref, wg_ref, wo_ref, g1_ref, b1_ref, wrh_ref, wrl_ref, br_ref, h1t_ref, ids_ref, wts_ref,
                  *, prompt_tiles):
    h = _ln(x_ref[...], eg_ref[...], eb_ref[...])
    is_prompt = pl.program_id(0) < prompt_tiles
    og = jnp.where(is_prompt, ogp_ref[...], ogs_ref[...])
    ya = jnp.where(is_prompt, yap_ref[...], yas_ref[...])
    parts = []
    for hh in range(GLA_HEADS):
        o = og[:, hh * GLA_HV:(hh + 1) * GLA_HV]
        parts.append(o * lax.rsqrt(jnp.mean(o * o, -1, keepdims=True) + EPS))
    gr = gr_ref[...]
    y_gla = jnp.concatenate(parts, 1) * ng_ref[...] * (gr * _sigmoid(gr))
    a = _dot(ya.astype(BF16), wa_ref[...])
    b = _dot(y_gla.astype(BF16), wg_ref[...])
    gate = _sigmoid(gate_ref[...] + bg_ref[...])
    hm = gate[:, :D_MODEL] * a + gate[:, D_MODEL:] * b
    mix = _dot(hm.astype(BF16), wo_ref[...])
    h1 = _ln(DN_ALPHA * h + mix, g1_ref[...], b1_ref[...])
    _store_row_tiles(h1t_ref, h1)
    h_hi = h1.astype(BF16)
    h_lo = (h1 - h_hi.astype(F32)).astype(BF16)
    logits = _dot(h_hi, wrh_ref[...]) + _dot(h_lo, wrh_ref[...]) + _dot(h_hi, wrl_ref[...]) + br_ref[...]
    ids_ref[...], wts_ref[...] = _route(logits.T, valid_ref[...] > 0.0)


def _merge(x_all, ya_p, ya_s, og_p, og_s, gr, gate, valid, eg, eb, ng, bg, wa, wg, wo, g1, b1, wrh, wrl, br):
    n = x_all.shape[0]
    tm = ROW_TILE
    pt = ya_p.shape[0] // tm
    st = ya_s.shape[0] // tm
    row = lambda w: pl.BlockSpec((tm, w), lambda i: (i, 0))
    row_p = lambda w: pl.BlockSpec((tm, w), lambda i: (jnp.minimum(i, pt - 1), 0))
    row_s = lambda w: pl.BlockSpec((tm, w), lambda i: (jnp.clip(i - pt, 0, st - 1), 0))
    lane = lambda r: pl.BlockSpec((r, tm), lambda i: (0, i))
    full = lambda a: pl.BlockSpec(a.shape, lambda i: (0,) * a.ndim)
    return pl.pallas_call(
        functools.partial(_merge_kernel, prompt_tiles=pt),
        grid=(n // tm,),
        in_specs=[row(D_MODEL), row_p(Q_W), row_s(Q_W), row_p(GLA_DV), row_s(GLA_DV), row(GLA_DV), row(2 * D_MODEL),
                  lane(1), full(eg), full(eb), full(ng), full(bg), full(wa), full(wg), full(wo), full(g1), full(b1),
                  full(wrh), full(wrl), full(br)],
        out_specs=[pl.BlockSpec((tm * ROW_CHUNKS, LANES), lambda i: (i, 0)), lane(8), lane(8)],
        out_shape=[jax.ShapeDtypeStruct((n * ROW_CHUNKS, LANES), F32),
                   jax.ShapeDtypeStruct((8, n), jnp.int32), jax.ShapeDtypeStruct((8, n), F32)],
        compiler_params=pltpu.CompilerParams(dimension_semantics=("parallel",), vmem_limit_bytes=VMEM_LIMIT),
        name="merge",
    )(x_all, ya_p, ya_s, og_p, og_s, gr, gate, valid, eg, eb, ng, bg, wa, wg, wo, g1, b1, wrh, wrl, br)


def _gather_row_tiles(idx_ref, idx0, src_hbm, dst, sem, n):
    def body(r, carry):
        t = idx_ref[idx0 + r]
        pltpu.make_async_copy(src_hbm.at[pl.ds(pl.multiple_of(t * ROW_CHUNKS, ROW_CHUNKS), ROW_CHUNKS), :],
                              dst.at[pl.ds(pl.multiple_of(r * ROW_CHUNKS, ROW_CHUNKS), ROW_CHUNKS), :], sem).start()
        return carry
    lax.fori_loop(0, n, body, 0, unroll=8)


def _wait_row_tiles(src_hbm, dst, sem, n):
    pltpu.make_async_copy(src_hbm.at[pl.ds(0, n * ROW_CHUNKS), :], dst, sem).wait()


def _ffn_kernel(te_ref, nt_ref, src_ref, h_hbm, wg_ref, wu_ref, wd_ref, out_ref, xbuf, sem, wgb, wub, wdb):
    i = pl.program_id(0)
    nt = nt_ref[0]
    T = FFN_TILE

    @pl.when(i == 0)
    def _():
        _gather_row_tiles(src_ref, 0, h_hbm, xbuf.at[0], sem.at[0], T)

    @pl.when(i + 1 < nt)
    def _():
        nxt = (i + 1) % 2
        _gather_row_tiles(src_ref, (i + 1) * T, h_hbm, xbuf.at[nxt], sem.at[nxt], T)

    @pl.when(i < nt)
    def _():
        slot = i % 2
        _wait_row_tiles(h_hbm, xbuf.at[slot], sem.at[slot], T)

        @pl.when((i == 0) | (te_ref[i] != te_ref[jnp.maximum(i - 1, 0)]))
        def _():
            wgb[...] = wg_ref[0].astype(BF16)
            wub[...] = wu_ref[0].astype(BF16)
            wdb[...] = wd_ref[0].astype(BF16)

        x = _load_row_tiles(xbuf.at[slot], T).astype(BF16)
        g = _dot(x, wgb[...])
        u = _dot(x, wub[...])
        _store_row_tiles(out_ref, _dot((g * _sigmoid(g) * u).astype(BF16), wdb[...]))

    @pl.when(i >= nt)
    def _():
        out_ref[...] = jnp.zeros_like(out_ref)


def _ffn(tile_expert, n_tiles, src_rows, h1t, w_g, w_u, w_d):
    T = FFN_TILE
    max_tiles = tile_expert.shape[0]
    wspec = lambda a: pl.BlockSpec((1,) + a.shape[1:], lambda i, te, nt, src: (te[i], 0, 0))
    return pl.pallas_call(
        _ffn_kernel,
        grid_spec=pltpu.PrefetchScalarGridSpec(
            num_scalar_prefetch=3,
            grid=(max_tiles,),
            in_specs=[pl.BlockSpec(memory_space=pl.ANY), wspec(w_g), wspec(w_u), wspec(w_d)],
            out_specs=pl.BlockSpec((T * ROW_CHUNKS, LANES), lambda i, te, nt, src: (i, 0)),
            scratch_shapes=[pltpu.VMEM((2, T * ROW_CHUNKS, LANES), F32), pltpu.SemaphoreType.DMA((2,)),
                            pltpu.VMEM((D_MODEL, D_EXPERT), BF16), pltpu.VMEM((D_MODEL, D_EXPERT), BF16),
                            pltpu.VMEM((D_EXPERT, D_MODEL), BF16)]),
        out_shape=jax.ShapeDtypeStruct((max_tiles * T * ROW_CHUNKS, LANES), F32),
        compiler_params=pltpu.CompilerParams(dimension_semantics=("arbitrary",)),
        name="ffn",
    )(tile_expert, n_tiles, src_rows, h1t, w_g, w_u, w_d)


def _combine_kernel(dest_ref, h_ref, w_ref, g_ref, b_ref, ys_hbm, y_ref, buf, sem, *, blk, n_rows):
    i = pl.program_id(0)
    n = pl.num_programs(0)
    T = CMB_TILE

    def gather(step, slot):
        for kk in range(2):
            _gather_row_tiles(dest_ref, kk * n_rows + blk(step) * T, ys_hbm, buf.at[slot, kk], sem.at[slot], T)

    @pl.when(i == 0)
    def _():
        gather(0, 0)

    @pl.when(i + 1 < n)
    def _():
        gather(i + 1, (i + 1) % 2)

    slot = i % 2
    for kk in range(2):
        _wait_row_tiles(ys_hbm, buf.at[slot, kk], sem.at[slot], T)
    w = w_ref[...]
    ff = w[:, 0:1] * _load_row_tiles(buf.at[slot, 0], T) + w[:, 1:2] * _load_row_tiles(buf.at[slot, 1], T)
    y_ref[...] = _ln(DN_ALPHA * _load_row_tiles(h_ref, T) + ff, g_ref[...], b_ref[...])


def _combine(dest, h1t, wts_t, ys, g2, b2, n_out, first_block, blocks_per_batch, skip_blocks):
    T = CMB_TILE
    n_rows = h1t.shape[0] // ROW_CHUNKS
    if skip_blocks:
        blk = lambda i: first_block + i + (i // blocks_per_batch + 1) * skip_blocks
    else:
        blk = lambda i: first_block + i
    full = lambda a: pl.BlockSpec(a.shape, lambda i, d: (0,) * a.ndim)
    return pl.pallas_call(
        functools.partial(_combine_kernel, blk=blk, n_rows=n_rows),
        grid_spec=pltpu.PrefetchScalarGridSpec(
            num_scalar_prefetch=1,
            grid=(n_out // T,),
            in_specs=[pl.BlockSpec((T * ROW_CHUNKS, LANES), lambda i, d: (blk(i), 0)),
                      pl.BlockSpec((T, 2), lambda i, d: (blk(i), 0)),
                      full(g2), full(b2), pl.BlockSpec(memory_space=pl.ANY)],
            out_specs=pl.BlockSpec((T, D_MODEL), lambda i, d: (i, 0)),
            scratch_shapes=[pltpu.VMEM((2, 2, T * ROW_CHUNKS, LANES), F32), pltpu.SemaphoreType.DMA((2,))]),
        out_shape=jax.ShapeDtypeStruct((n_out, D_MODEL), F32),
        compiler_params=pltpu.CompilerParams(dimension_semantics=("arbitrary",)),
        name="combine",
    )(dest, h1t, wts_t, g2, b2, ys)


def _dispatch_plan(ids, n_rows, max_tiles):
    T = FFN_TILE
    e_flat = ids.reshape(-1)
    onehot = (e_flat[:, None] == jnp.arange(N_EXPERTS, dtype=jnp.int32)[None, :]).astype(jnp.int32)
    csum = jnp.cumsum(onehot, axis=0)
    counts = csum[-1]
    rank = jnp.sum(onehot * csum, axis=1) - 1
    tiles_e = (counts + T - 1) // T
    tile_end = jnp.cumsum(tiles_e)
    tile_start = tile_end - tiles_e
    n_tiles = tile_end[-1]
    n_slots = max_tiles * T
    start_of = jnp.sum(onehot * tile_start[None, :], axis=1) * T
    dest = jnp.where(e_flat >= 0, start_of + rank, n_slots).astype(jnp.int32)
    rows = jnp.tile(jnp.arange(n_rows, dtype=jnp.int32), 2)
    src = jnp.zeros((n_slots,), jnp.int32).at[dest].set(rows, mode="drop")
    te = jnp.sum((jnp.arange(max_tiles, dtype=jnp.int32)[:, None] >= tile_end[None, :]).astype(jnp.int32), axis=1)
    te_last = jnp.take(te, jnp.maximum(n_tiles - 1, 0))
    te = jnp.where(jnp.arange(max_tiles) < n_tiles, te, te_last).astype(jnp.int32)
    dest = jnp.minimum(dest, n_slots - 1)
    return te, n_tiles.reshape(1).astype(jnp.int32), src, dest


def kernel(x_prompt, x_sample, state_swa_k, state_swa_v, state_gla, meta_tokens, ln_emb_g, ln_emb_b, w_in, b_gate, attn_sink, w_alpha2, b_alpha, gla_norm_g, w_attn_br, w_gla_br, w_out, ln1_g, ln1_b, w_router_group, b_router_group, w_router_expert, b_router_expert, w_exp_gate, w_exp_up, w_exp_down, ln2_g, ln2_b):
    B, seq, _ = x_prompt.shape
    n_seq, t_s, _ = x_sample.shape
    depth = w_in.shape[0]
    assert depth == 1 and seq % ATT_BLOCK == 0 and t_s == 8
    lp = SKIP_ROWS + seq
    NP, NS = B * lp, n_seq * t_s
    NR = NP + NS
    assert NP % ROW_TILE == 0 and NS % ROW_TILE == 0 and n_seq % SAMPLE_SEQS == 0
    l = 0
    row2 = lambda a: a.reshape(1, -1)

    xp = jnp.concatenate([jnp.zeros((B, FRONT_PAD, D_MODEL), F32),
                          jnp.broadcast_to(meta_tokens[None], (B, N_META, D_MODEL)), x_prompt], axis=1)
    x_all = jnp.concatenate([xp.reshape(NP, D_MODEL), x_sample.reshape(NS, D_MODEL)], axis=0)
    pos = np.arange(NR)
    in_prompt = pos < NP
    keep = jnp.asarray(~(in_prompt & (pos % lp < FRONT_PAD)), F32).reshape(NR, 1)
    moe_valid = jnp.asarray(~(in_prompt & (pos % lp < SKIP_ROWS)), F32).reshape(1, NR)

    wi = w_in[l]
    c_ga = sum((Q_W, KV_W, KV_W, GLA_DK, GLA_DK, GLA_DV))
    w_bf = jnp.concatenate([wi[:, :c_ga], wi[:, c_ga + GLA_RANK:], wi[:, c_ga:c_ga + GLA_RANK],
                            jnp.zeros((D_MODEL, 128 - GLA_RANK), F32)], axis=1).astype(BF16)
    wa2_bf = jnp.concatenate([w_alpha2[l], jnp.zeros((128 - GLA_RANK, GLA_DK), F32)], axis=0).astype(BF16)

    q, k, v, gq, gk, gv, la, gr, gate = _inproj(x_all, keep, row2(ln_emb_g), row2(ln_emb_b), w_bf, wa2_bf,
                                                row2(b_alpha[l]))

    sink = attn_sink[l]
    ya_p = _swa_prompt(sink, q, k, v, B, lp)
    buf_k = state_swa_k[l].reshape(n_seq, WINDOW, KV_W)
    buf_v = state_swa_v[l].reshape(n_seq, WINDOW, KV_W)
    ya_s, nk_s, nv_s = _swa_sample(sink, q, k, v, buf_k, buf_v, NP, t_s)

    og_p, s_p = _gla_prompt(gq, gk, gv, la, B, lp)
    og_s, s_s = _gla_sample(gq, gk, gv, la, state_gla[l], NP, t_s)

    wr = jnp.concatenate([w_router_expert[l], w_router_group[l],
                          jnp.zeros((D_MODEL, LANES - N_EXPERTS - N_GROUPS), F32)], axis=1)
    br = jnp.concatenate([b_router_expert[l], b_router_group[l],
                          jnp.zeros((LANES - N_EXPERTS - N_GROUPS,), F32)]).reshape(1, LANES)
    wr_hi = wr.astype(BF16)
    h1t, ids, wts = _merge(x_all, ya_p, ya_s, og_p, og_s, gr, gate, moe_valid, row2(ln_emb_g), row2(ln_emb_b),
                           row2(gla_norm_g[l]), row2(b_gate[l]), w_attn_br[l].astype(BF16),
                           w_gla_br[l].astype(BF16), w_out[l].astype(BF16), row2(ln1_g[l]), row2(ln1_b[l]),
                           wr_hi, (wr - wr_hi.astype(F32)).astype(BF16), br)

    n_tok = B * seq + NS
    max_tiles = (2 * n_tok) // FFN_TILE + N_EXPERTS
    te, n_tiles, src, dest = _dispatch_plan(ids[0:2], NR, max_tiles)
    ys = _ffn(te, n_tiles, src, h1t, w_exp_gate[l], w_exp_up[l], w_exp_down[l])

    wts_t = wts[0:2].T
    g2, b2 = row2(ln2_g[l]), row2(ln2_b[l])
    skip_blocks = SKIP_ROWS // CMB_TILE
    y_p = _combine(dest, h1t, wts_t, ys, g2, b2, B * seq, 0, seq // CMB_TILE, skip_blocks)
    y_s = _combine(dest, h1t, wts_t, ys, g2, b2, NS, NP // CMB_TILE, 1, 0)

    kv_shape = (B, lp, N_KV, HEAD_DIM)
    k_p = k[:NP].reshape(kv_shape)[:, -WINDOW:]
    v_p = v[:NP].reshape(kv_shape)[:, -WINDOW:]
    return (y_p.reshape(B, seq, D_MODEL), y_s.reshape(n_seq, t_s, D_MODEL),
            k_p[None], v_p[None], s_p[None],
            nk_s.reshape(1, n_seq, WINDOW, N_KV, HEAD_DIM), nv_s.reshape(1, n_seq, WINDOW, N_KV, HEAD_DIM),
            s_s[None])
```

```python
import functools

import numpy as np
import jax
import jax.numpy as jnp
from jax import lax
from jax.experimental import pallas as pl
from jax.experimental.pallas import tpu as pltpu

F32 = jnp.float32
BF16 = jnp.bfloat16

D_MODEL = 1024
N_META = 16
HEAD_DIM = 64
N_HEADS = 8
N_KV = 2
Q_PER_KV = 4
WINDOW = 128
ATT_BLOCK = 128
GLA_HEADS = 4
GLA_HK = 128
GLA_HV = 256
GLA_DK = GLA_HEADS * GLA_HK
GLA_DV = GLA_HEADS * GLA_HV
GLA_RANK = 16
GLA_TAU = 16.0
GLA_CHUNK = 64
N_GROUPS = 4
EXP_PER_GROUP = 8
N_EXPERTS = 32
D_EXPERT = 256
DN_ALPHA = 2.0 ** 0.25
EPS = 1e-5
NEG = -1e30

FRONT_PAD = (-N_META) % ATT_BLOCK
SKIP_ROWS = FRONT_PAD + N_META

Q_W, KV_W = N_HEADS * HEAD_DIM, N_KV * HEAD_DIM
SEG = {}
_o = 0
for _n, _w in (("q", Q_W), ("k", KV_W), ("v", KV_W), ("gq", GLA_DK), ("gk", GLA_DK), ("gv", GLA_DV),
               ("gr", GLA_DV), ("gate", 2 * D_MODEL), ("ga", 128)):
    SEG[_n] = (_o, _o + _w)
    _o += _w
W_IN_COLS = _o

ROW_TILE = 256
FFN_TILE = 256
CMB_TILE = 128
DISPATCH_CHUNK = 128
SAMPLE_SEQS = 8
LANES = 128
ROW_CHUNKS = D_MODEL // LANES
VMEM_LIMIT = 56 * 1024 * 1024


def _ln(x, g, b):
    mu = jnp.mean(x, -1, keepdims=True)
    xc = x - mu
    var = jnp.mean(xc * xc, -1, keepdims=True)
    return xc * lax.rsqrt(var + EPS) * g + b


def _sigmoid(x):
    return 0.5 * jnp.tanh(0.5 * x) + 0.5


def _dot(a, b):
    return jnp.dot(a, b, preferred_element_type=F32)


def _dot_nt(a, b):
    return lax.dot_general(a, b, (((1,), (1,)), ((), ())), preferred_element_type=F32)


def _inproj_kernel(x_ref, keep_ref, g_ref, b_ref, w_ref, wa2_ref, ba_ref,
                   q_ref, k_ref, v_ref, gq_ref, gk_ref, gv_ref, la_ref, gr_ref, gate_ref):
    hb = _ln(x_ref[...], g_ref[...], b_ref[...]).astype(BF16)
    keep = keep_ref[...]

    def seg(name):
        a, b = SEG[name]
        return _dot(hb, w_ref[:, a:b])

    q_ref[...] = seg("q")
    k_ref[...] = seg("k")
    v_ref[...] = seg("v")
    gq_ref[...] = seg("gq") * (GLA_HK ** -0.5)
    gk_ref[...] = seg("gk") * keep
    gv_ref[...] = seg("gv") * keep
    gr_ref[...] = seg("gr")
    gate_ref[...] = seg("gate")
    z = _dot(seg("ga").astype(BF16), wa2_ref[...]) + ba_ref[...]
    la = (jnp.minimum(z, 0.0) - jnp.log(1.0 + jnp.exp(-jnp.abs(z)))) * (1.0 / GLA_TAU)
    la_ref[...] = la * keep


def _inproj(x_all, keep, ln_g, ln_b, w_bf, wa2_bf, b_alpha):
    n = x_all.shape[0]
    tm = ROW_TILE
    widths = [Q_W, KV_W, KV_W, GLA_DK, GLA_DK, GLA_DV, GLA_DK, GLA_DV, 2 * D_MODEL]
    row = lambda w: pl.BlockSpec((tm, w), lambda i: (i, 0))
    full = lambda a: pl.BlockSpec(a.shape, lambda i: (0,) * a.ndim)
    return pl.pallas_call(
        _inproj_kernel,
        grid=(n // tm,),
        in_specs=[row(D_MODEL), row(1), full(ln_g), full(ln_b), full(w_bf), full(wa2_bf), full(b_alpha)],
        out_specs=[row(w) for w in widths],
        out_shape=[jax.ShapeDtypeStruct((n, w), F32) for w in widths],
        compiler_params=pltpu.CompilerParams(dimension_semantics=("parallel",), vmem_limit_bytes=VMEM_LIMIT),
        name="inproj",
    )(x_all, keep, ln_g, ln_b, w_bf, wa2_bf, b_alpha)


def _softmax_pv(s, sink, vv):
    m = jnp.maximum(jnp.max(s, -1, keepdims=True), sink)
    p = jnp.exp(s - m)
    l = jnp.sum(p, -1, keepdims=True) + jnp.exp(sink - m)
    return _dot(p.astype(BF16), vv) / l


def _swa_prompt_kernel(sink_ref, q_ref, kp_ref, kc_ref, vp_ref, vc_ref, o_ref):
    j = pl.program_id(1)
    q = q_ref[...]
    kb = jnp.concatenate([kp_ref[...], kc_ref[...]], 0)
    vb = jnp.concatenate([vp_ref[...], vc_ref[...]], 0)
    r = lax.broadcasted_iota(jnp.int32, (ATT_BLOCK, 2 * ATT_BLOCK), 0)
    c = lax.broadcasted_iota(jnp.int32, (ATT_BLOCK, 2 * ATT_BLOCK), 1)
    dist = r - c + ATT_BLOCK
    k_pos = (j - 1) * ATT_BLOCK + c - FRONT_PAD
    mask = (dist >= 0) & (dist < WINDOW) & (k_pos >= 0)
    distf = dist.astype(F32)
    for h in range(N_HEADS):
        kv = h // Q_PER_KV
        qh = q[:, h * HEAD_DIM:(h + 1) * HEAD_DIM].astype(BF16)
        kk = kb[:, kv * HEAD_DIM:(kv + 1) * HEAD_DIM].astype(BF16)
        vv = vb[:, kv * HEAD_DIM:(kv + 1) * HEAD_DIM].astype(BF16)
        s = _dot_nt(qh, kk) * (HEAD_DIM ** -0.5) - (2.0 ** -(h + 1)) * distf
        s = jnp.where(mask, s, NEG)
        o_ref[:, h * HEAD_DIM:(h + 1) * HEAD_DIM] = _softmax_pv(s, sink_ref[h], vv)


def _swa_prompt(sink, q, k, v, batch, lp):
    nb = lp // ATT_BLOCK
    n = batch * lp
    cur = lambda w: pl.BlockSpec((ATT_BLOCK, w), lambda b, j: (b * nb + j, 0))
    prev = lambda w: pl.BlockSpec((ATT_BLOCK, w), lambda b, j: (b * nb + jnp.maximum(j - 1, 0), 0))
    return pl.pallas_call(
        _swa_prompt_kernel,
        grid=(batch, nb),
        in_specs=[pl.BlockSpec(memory_space=pltpu.SMEM), cur(Q_W), prev(KV_W), cur(KV_W), prev(KV_W), cur(KV_W)],
        out_specs=cur(Q_W),
        out_shape=jax.ShapeDtypeStruct((n, Q_W), F32),
        compiler_params=pltpu.CompilerParams(dimension_semantics=("parallel", "parallel")),
        name="swa_prompt",
    )(sink, q, k, k, v, v)


def _swa_sample_kernel(sink_ref, q_ref, k_ref, v_ref, bk_ref, bv_ref, o_ref, nk_ref, nv_ref, *, t_s):
    nbuf = WINDOW
    span = 2 * WINDOW
    rows = Q_PER_KV * t_s
    r = lax.broadcasted_iota(jnp.int32, (rows, span), 0)
    c = lax.broadcasted_iota(jnp.int32, (rows, span), 1)
    t = r % t_s
    dist = t + nbuf - c
    mask = (dist >= 0) & (dist < WINDOW) & (c < nbuf + t_s)
    distf = dist.astype(F32)
    g_col = lax.broadcasted_iota(jnp.int32, (rows, 1), 0) // t_s
    fill = jnp.zeros((span - nbuf - t_s, KV_W), F32)

    def one_seq(s, carry):
        rs = pl.ds(pl.multiple_of(s * t_s, t_s), t_s)
        q = q_ref[rs, :]
        k_new = k_ref[rs, :]
        v_new = v_ref[rs, :]
        bk = bk_ref[s]
        bv = bv_ref[s]
        k_all = jnp.concatenate([bk, k_new, fill], 0)
        v_all = jnp.concatenate([bv, v_new, fill], 0)
        for kv in range(N_KV):
            qg = jnp.concatenate(
                [q[:, (kv * Q_PER_KV + g) * HEAD_DIM:(kv * Q_PER_KV + g + 1) * HEAD_DIM] for g in range(Q_PER_KV)], 0)
            kk = k_all[:, kv * HEAD_DIM:(kv + 1) * HEAD_DIM].astype(BF16)
            vv = v_all[:, kv * HEAD_DIM:(kv + 1) * HEAD_DIM].astype(BF16)
            slope = jnp.zeros((rows, 1), F32)
            sink = jnp.zeros((rows, 1), F32)
            for g in range(Q_PER_KV):
                h = kv * Q_PER_KV + g
                slope = jnp.where(g_col == g, 2.0 ** -(h + 1), slope)
                sink = jnp.where(g_col == g, sink_ref[h], sink)
            sc = _dot_nt(qg.astype(BF16), kk) * (HEAD_DIM ** -0.5) - slope * distf
            sc = jnp.where(mask, sc, NEG)
            o = _softmax_pv(sc, sink, vv)
            for g in range(Q_PER_KV):
                h = kv * Q_PER_KV + g
                o_ref[rs, h * HEAD_DIM:(h + 1) * HEAD_DIM] = o[g * t_s:(g + 1) * t_s]
        nk_ref[s, 0:nbuf - t_s, :] = bk[t_s:, :]
        nk_ref[s, nbuf - t_s:nbuf, :] = k_new
        nv_ref[s, 0:nbuf - t_s, :] = bv[t_s:, :]
        nv_ref[s, nbuf - t_s:nbuf, :] = v_new
        return carry

    lax.fori_loop(0, SAMPLE_SEQS, one_seq, 0)


def _swa_sample(sink, q, k, v, buf_k, buf_v, row0, t_s):
    n_seq = buf_k.shape[0]
    sb = SAMPLE_SEQS
    rb = sb * t_s
    b0 = row0 // rb
    rows = lambda w: pl.BlockSpec((rb, w), lambda i: (b0 + i, 0))
    bufs = pl.BlockSpec((sb, WINDOW, KV_W), lambda i: (i, 0, 0))
    return pl.pallas_call(
        functools.partial(_swa_sample_kernel, t_s=t_s),
        grid=(n_seq // sb,),
        in_specs=[pl.BlockSpec(memory_space=pltpu.SMEM), rows(Q_W), rows(KV_W), rows(KV_W), bufs, bufs],
        out_specs=[pl.BlockSpec((rb, Q_W), lambda i: (i, 0)), bufs, bufs],
        out_shape=[jax.ShapeDtypeStruct((n_seq * t_s, Q_W), F32),
                   jax.ShapeDtypeStruct(buf_k.shape, F32), jax.ShapeDtypeStruct(buf_v.shape, F32)],
        compiler_params=pltpu.CompilerParams(dimension_semantics=("parallel",)),
        name="swa_sample",
    )(sink, q, k, v, buf_k, buf_v)


def _gla_tables(chunk):
    t = np.arange(chunk)[:, None]
    u = np.arange(chunk)[None, :]
    masks = []
    w = chunk // 2
    while w >= 1:
        masks.append((t // (2 * w) == u // (2 * w)) & ((t // w) % 2 == 1) & ((u // w) % 2 == 0))
        w //= 2
    return (u <= t).astype(np.float32), np.stack(masks, 0).astype(np.float32)


def _level_exponents(b, la, w):
    C = b.shape[0]
    row = lax.broadcasted_iota(jnp.int32, b.shape, 0)
    if w >= 4:
        pieces = [jnp.broadcast_to(b[p + w - 1:p + w], (2 * w, b.shape[1])) for p in range(0, C, 2 * w)]
        ref = pieces[0] if len(pieces) == 1 else jnp.concatenate(pieces, 0)
        return jnp.where((row & w) != 0, b - ref, ref - b)
    if w == 2:
        m = row & 3
        nxt = pltpu.roll(la, C - 1, 0)
        prv = pltpu.roll(la, 1, 0)
        return jnp.where(m == 2, la, jnp.where(m == 3, la + prv, jnp.where(m == 0, nxt, 0.0)))
    return jnp.where((row & 1) != 0, la, 0.0)


def _split3(x):
    hi = x.astype(BF16)
    r1 = x - hi.astype(F32)
    mid = r1.astype(BF16)
    lo = (r1 - mid.astype(F32)).astype(BF16)
    return hi, mid, lo


def _gla_intra_kernel(g_ref, m_ref, q_ref, k_ref, v_ref, la_ref, o_ref, qe_ref, ke_ref, vt_ref, d_ref):
    C = GLA_CHUNK
    n_lvl = m_ref.shape[0]
    G = g_ref[...]
    eye = (lax.broadcasted_iota(jnp.int32, (C, C), 0) == lax.broadcasted_iota(jnp.int32, (C, C), 1)).astype(F32)
    la = la_ref[...]
    hi, mid, lo = _split3(la)
    b = _dot(G, hi) + _dot(G, mid) + _dot(G, lo)
    b_last = b[C - 1:C]
    q_all = q_ref[...]
    k_all = k_ref[...]
    qe_ref[0] = (q_all * jnp.exp(b)).astype(BF16)
    ke_ref[0] = (k_all * jnp.exp(b_last - b)).astype(BF16)
    d_ref[0, 0] = jnp.broadcast_to(jnp.exp(b_last), (8, GLA_DK))
    q_lvl, k_lvl = [], []
    for l in range(n_lvl):
        El = jnp.exp(_level_exponents(b, la, C >> (l + 1)))
        q_lvl.append((q_all * El).astype(BF16))
        k_lvl.append((k_all * El).astype(BF16))
    for h in range(GLA_HEADS):
        ks = slice(h * GLA_HK, (h + 1) * GLA_HK)
        vs = slice(h * GLA_HV, (h + 1) * GLA_HV)
        v = v_ref[:, vs]
        att = eye * jnp.sum(q_all[:, ks] * k_all[:, ks], -1, keepdims=True)
        for l in range(n_lvl):
            att = att + m_ref[l] * _dot_nt(q_lvl[l][:, ks], k_lvl[l][:, ks])
        o_ref[0, :, vs] = _dot(att.astype(BF16), v.astype(BF16))
        vt_ref[0, 0, vs, :] = v.T.astype(BF16)


def _gla_inter_kernel(o_ref, qe_ref, ke_ref, vt_ref, d_ref, og_ref, s_ref, st_ref):
    c = pl.program_id(0)
    batch = o_ref.shape[0]

    @pl.when(c == 0)
    def _():
        st_ref[...] = jnp.zeros_like(st_ref)

    for b in range(batch):
        for h in range(GLA_HEADS):
            ks = slice(h * GLA_HK, (h + 1) * GLA_HK)
            vs = slice(h * GLA_HV, (h + 1) * GLA_HV)
            st = st_ref[b * GLA_HEADS + h]
            og_ref[b, :, vs] = o_ref[b, :, vs] + _dot_nt(qe_ref[b, :, ks], st.astype(BF16))
            st_ref[b * GLA_HEADS + h] = st * d_ref[b, 0, 0:1, ks] + _dot(vt_ref[b, 0, vs, :], ke_ref[b, :, ks])

    @pl.when(c == pl.num_programs(0) - 1)
    def _():
        for b in range(batch):
            for h in range(GLA_HEADS):
                s_ref[b, h] = st_ref[b * GLA_HEADS + h].T


def _gla_prompt(gq, gk, gv, la, batch, lp):
    C = GLA_CHUNK
    nc = lp // C
    G, M = _gla_tables(C)
    G = jnp.asarray(G, BF16)
    M = jnp.asarray(M, F32)
    rows = lambda w: pl.BlockSpec((C, w), lambda b, c: (b * nc + c, 0))
    rows3 = lambda w: pl.BlockSpec((1, C, w), lambda b, c: (b, c, 0))
    full = lambda a: pl.BlockSpec(a.shape, lambda b, c: (0,) * a.ndim)
    o_intra, qe, ke, vt, d = pl.pallas_call(
        _gla_intra_kernel,
        grid=(batch, nc),
        in_specs=[full(G), full(M), rows(GLA_DK), rows(GLA_DK), rows(GLA_DV), rows(GLA_DK)],
        out_specs=[rows3(GLA_DV), rows3(GLA_DK), rows3(GLA_DK),
                   pl.BlockSpec((1, 1, GLA_DV, C), lambda b, c: (b, c, 0, 0)),
                   pl.BlockSpec((1, 1, 8, GLA_DK), lambda b, c: (b, c, 0, 0))],
        out_shape=[jax.ShapeDtypeStruct((batch, lp, GLA_DV), F32),
                   jax.ShapeDtypeStruct((batch, lp, GLA_DK), BF16), jax.ShapeDtypeStruct((batch, lp, GLA_DK), BF16),
                   jax.ShapeDtypeStruct((batch, nc, GLA_DV, C), BF16),
                   jax.ShapeDtypeStruct((batch, nc, 8, GLA_DK), F32)],
        compiler_params=pltpu.CompilerParams(dimension_semantics=("parallel", "parallel")),
        name="gla_intra",
    )(G, M, gq, gk, gv, la)
    chunk = lambda w: pl.BlockSpec((batch, C, w), lambda c: (0, c, 0))
    og, s_fin = pl.pallas_call(
        _gla_inter_kernel,
        grid=(nc,),
        in_specs=[chunk(GLA_DV), chunk(GLA_DK), chunk(GLA_DK),
                  pl.BlockSpec((batch, 1, GLA_DV, C), lambda c: (0, c, 0, 0)),
                  pl.BlockSpec((batch, 1, 8, GLA_DK), lambda c: (0, c, 0, 0))],
        out_specs=[chunk(GLA_DV), pl.BlockSpec((batch, GLA_HEADS, GLA_HK, GLA_HV), lambda c: (0, 0, 0, 0))],
        out_shape=[jax.ShapeDtypeStruct((batch, lp, GLA_DV), F32),
                   jax.ShapeDtypeStruct((batch, GLA_HEADS, GLA_HK, GLA_HV), F32)],
        scratch_shapes=[pltpu.VMEM((batch * GLA_HEADS, GLA_HV, GLA_HK), F32)],
        compiler_params=pltpu.CompilerParams(dimension_semantics=("arbitrary",)),
        name="gla_inter",
    )(o_intra, qe, ke, vt, d)
    return og.reshape(batch * lp, GLA_DV), s_fin


def _gla_sample_kernel(q_ref, k_ref, v_ref, la_ref, s0_ref, o_ref, s_ref, *, t_s):
    T = t_s
    row = lax.broadcasted_iota(jnp.int32, (T, GLA_HK), 0)
    k_fill = jnp.zeros((GLA_HK - T - 8, GLA_HK), F32)
    v_fill = jnp.zeros((GLA_HK - T, GLA_HV), F32)

    def one_seq(s, carry):
        rs = pl.ds(pl.multiple_of(s * T, T), T)
        for h in range(GLA_HEADS):
            ks = slice(h * GLA_HK, (h + 1) * GLA_HK)
            vs = slice(h * GLA_HV, (h + 1) * GLA_HV)
            q = q_ref[rs, ks]
            k = k_ref[rs, ks]
            v = v_ref[rs, vs]
            b = la_ref[rs, ks]
            sh = 1
            while sh < T:
                b = b + jnp.where(row >= sh, pltpu.roll(b, sh, 0), 0.0)
                sh *= 2
            S = s0_ref[s, h]
            o = _dot((q * jnp.exp(b)).astype(BF16), S.astype(BF16))
            for j in range(T):
                e = jnp.exp(jnp.where(row >= j, b - b[j:j + 1], NEG))
                a_col = jnp.sum(q * k[j:j + 1] * e, -1, keepdims=True)
                o = o + a_col * v[j:j + 1]
            o_ref[rs, vs] = o
            b_last = b[T - 1:T]
            ke = k * jnp.exp(b_last - b)
            kt = jnp.concatenate([ke, jnp.broadcast_to(jnp.exp(b_last), (8, GLA_HK)), k_fill], 0).T
            v_pad = jnp.concatenate([v, v_fill], 0)
            s_ref[s, h] = S * kt[:, T:T + 1] + _dot(kt.astype(BF16), v_pad.astype(BF16))
        return carry

    lax.fori_loop(0, SAMPLE_SEQS, one_seq, 0)


def _gla_sample(gq, gk, gv, la, s0, row0, t_s):
    n_seq = s0.shape[0]
    sb = SAMPLE_SEQS
    rb = sb * t_s
    b0 = row0 // rb
    rows = lambda w: pl.BlockSpec((rb, w), lambda i: (b0 + i, 0))
    st = pl.BlockSpec((sb, GLA_HEADS, GLA_HK, GLA_HV), lambda i: (i, 0, 0, 0))
    return pl.pallas_call(
        functools.partial(_gla_sample_kernel, t_s=t_s),
        grid=(n_seq // sb,),
        in_specs=[rows(GLA_DK), rows(GLA_DK), rows(GLA_DV), rows(GLA_DK), st],
        out_specs=[pl.BlockSpec((rb, GLA_DV), lambda i: (i, 0)), st],
        out_shape=[jax.ShapeDtypeStruct((n_seq * t_s, GLA_DV), F32), jax.ShapeDtypeStruct(s0.shape, F32)],
        compiler_params=pltpu.CompilerParams(dimension_semantics=("parallel",), vmem_limit_bytes=VMEM_LIMIT),
        name="gla_sample",
    )(gq, gk, gv, la, s0)


def _route(lt, valid):
    tm = lt.shape[1]
    el = lt[0:N_EXPERTS]
    gl = lt[N_EXPERTS:N_EXPERTS + N_GROUPS]
    g_max = jnp.max(gl, 0, keepdims=True)
    g_row = lax.broadcasted_iota(jnp.int32, (N_GROUPS, tm), 0)
    g_idx = jnp.min(jnp.where(gl == g_max, g_row, N_GROUPS), 0, keepdims=True)
    p_max = 1.0 / jnp.sum(jnp.exp(gl - g_max), 0, keepdims=True)
    e_row = lax.broadcasted_iota(jnp.int32, (N_EXPERTS, tm), 0)
    m1 = jnp.where(e_row // EXP_PER_GROUP == g_idx, el, -jnp.inf)
    v1 = jnp.max(m1, 0, keepdims=True)
    i1 = jnp.min(jnp.where(m1 == v1, e_row, N_EXPERTS), 0, keepdims=True)
    m2 = jnp.where(e_row == i1, -jnp.inf, m1)
    v2 = jnp.max(m2, 0, keepdims=True)
    i2 = jnp.min(jnp.where(m2 == v2, e_row, N_EXPERTS), 0, keepdims=True)
    e2 = jnp.exp(v2 - v1)
    w1 = p_max / (1.0 + e2)
    w2 = p_max * e2 / (1.0 + e2)
    o_row = lax.broadcasted_iota(jnp.int32, (8, tm), 0)
    ids = jnp.where(o_row == 0, i1, jnp.where(o_row == 1, i2, -1))
    return jnp.where(valid, ids, -1), jnp.where(o_row == 0, w1, jnp.where(o_row == 1, w2, 0.0))


def _store_row_tiles(ref, x):
    t, d = x.shape
    n = d // LANES
    for s in range(n):
        ref[pl.ds(s, t, stride=n), :] = x[:, s * LANES:(s + 1) * LANES]


def _load_row_tiles(ref, t, n=ROW_CHUNKS):
    return jnp.concatenate([ref[pl.ds(s, t, stride=n), :] for s in range(n)], axis=1)


def _merge_kernel(x_ref, yap_ref, yas_ref, ogp_ref, ogs_ref, gr_ref, gate_ref, valid_ref, eg_ref, eb_ref, ng_ref,
                  bg_ref, wa_ref, wg_ref, wo_ref, g1_ref, b1_ref, wrh_ref, wrl_ref, br_ref, u_ref,
                  h1t_ref, ids_ref, wts_ref, cnt_ref, run_ref, *, prompt_tiles):
    @pl.when(pl.program_id(0) == 0)
    def _():
        run_ref[...] = jnp.zeros_like(run_ref)

    h = _ln(x_ref[...], eg_ref[...], eb_ref[...])
    is_prompt = pl.program_id(0) < prompt_tiles
    og = jnp.where(is_prompt, ogp_ref[...], ogs_ref[...])
    ya = jnp.where(is_prompt, yap_ref[...], yas_ref[...])
    parts = []
    for hh in range(GLA_HEADS):
        o = og[:, hh * GLA_HV:(hh + 1) * GLA_HV]
        parts.append(o * lax.rsqrt(jnp.mean(o * o, -1, keepdims=True) + EPS))
    gr = gr_ref[...]
    y_gla = jnp.concatenate(parts, 1) * ng_ref[...] * (gr * _sigmoid(gr))
    a = _dot(ya.astype(BF16), wa_ref[...])
    b = _dot(y_gla.astype(BF16), wg_ref[...])
    gate = _sigmoid(gate_ref[...] + bg_ref[...])
    hm = gate[:, :D_MODEL] * a + gate[:, D_MODEL:] * b
    mix = _dot(hm.astype(BF16), wo_ref[...])
    h1 = _ln(DN_ALPHA * h + mix, g1_ref[...], b1_ref[...])
    _store_row_tiles(h1t_ref, h1)
    h_hi = h1.astype(BF16)
    h_lo = (h1 - h_hi.astype(F32)).astype(BF16)
    logits = _dot(h_hi, wrh_ref[...]) + _dot(h_lo, wrh_ref[...]) + _dot(h_hi, wrl_ref[...]) + br_ref[...]
    ids, wts_ref[...] = _route(logits.T, valid_ref[...] > 0.0)
    tm = ids.shape[1]
    e_row = lax.broadcasted_iota(jnp.int32, (N_EXPERTS, tm), 0)
    run = run_ref[:, 0:1]
    ranks = []
    for kk in range(2):
        onehot = (e_row == ids[kk:kk + 1]).astype(F32)
        before = _dot(onehot.astype(BF16), u_ref[...])
        ranks.append(jnp.sum(onehot * (run + before), 0, keepdims=True).astype(jnp.int32))
        run = run + jnp.sum(onehot, 1, keepdims=True)
    run_ref[...] = jnp.broadcast_to(run, run_ref.shape)
    o_row = lax.broadcasted_iota(jnp.int32, (8, tm), 0)
    ids_ref[...] = jnp.where(o_row == 2, ranks[0], jnp.where(o_row == 3, ranks[1], ids))
    cnt_ref[...] = run_ref[...].astype(jnp.int32)


def _merge(x_all, ya_p, ya_s, og_p, og_s, gr, gate, valid, eg, eb, ng, bg, wa, wg, wo, g1, b1, wrh, wrl, br):
    n = x_all.shape[0]
    tm = ROW_TILE
    u = jnp.asarray(np.triu(np.ones((tm, tm), np.float32), 1), BF16)
    pt = ya_p.shape[0] // tm
    st = ya_s.shape[0] // tm
    row = lambda w: pl.BlockSpec((tm, w), lambda i: (i, 0))
    row_p = lambda w: pl.BlockSpec((tm, w), lambda i: (jnp.minimum(i, pt - 1), 0))
    row_s = lambda w: pl.BlockSpec((tm, w), lambda i: (jnp.clip(i - pt, 0, st - 1), 0))
    lane = lambda r: pl.BlockSpec((r, tm), lambda i: (0, i))
    full = lambda a: pl.BlockSpec(a.shape, lambda i: (0,) * a.ndim)
    return pl.pallas_call(
        functools.partial(_merge_kernel, prompt_tiles=pt),
        grid=(n // tm,),
        in_specs=[row(D_MODEL), row_p(Q_W), row_s(Q_W), row_p(GLA_DV), row_s(GLA_DV), row(GLA_DV), row(2 * D_MODEL),
                  lane(1), full(eg), full(eb), full(ng), full(bg), full(wa), full(wg), full(wo), full(g1), full(b1),
                  full(wrh), full(wrl), full(br), full(u)],
        out_specs=[pl.BlockSpec((tm * ROW_CHUNKS, LANES), lambda i: (i, 0)), lane(8), lane(8),
                   pl.BlockSpec((N_EXPERTS, LANES), lambda i: (0, 0))],
        out_shape=[jax.ShapeDtypeStruct((n * ROW_CHUNKS, LANES), F32),
                   jax.ShapeDtypeStruct((8, n), jnp.int32), jax.ShapeDtypeStruct((8, n), F32),
                   jax.ShapeDtypeStruct((N_EXPERTS, LANES), jnp.int32)],
        scratch_shapes=[pltpu.VMEM((N_EXPERTS, LANES), F32)],
        compiler_params=pltpu.CompilerParams(dimension_semantics=("arbitrary",), vmem_limit_bytes=VMEM_LIMIT),
        name="merge",
    )(x_all, ya_p, ya_s, og_p, og_s, gr, gate, valid, eg, eb, ng, bg, wa, wg, wo, g1, b1, wrh, wrl, br, u)


def _gather_row_tiles(idx_ref, idx0, src_hbm, dst, sem, n):
    def body(r, carry):
        t = idx_ref[idx0 + r]
        pltpu.make_async_copy(src_hbm.at[pl.ds(pl.multiple_of(t * ROW_CHUNKS, ROW_CHUNKS), ROW_CHUNKS), :],
                              dst.at[pl.ds(pl.multiple_of(r * ROW_CHUNKS, ROW_CHUNKS), ROW_CHUNKS), :], sem).start()
        return carry
    lax.fori_loop(0, n, body, 0, unroll=8)


def _wait_row_tiles(src_hbm, dst, sem, n):
    pltpu.make_async_copy(src_hbm.at[pl.ds(0, n * ROW_CHUNKS), :], dst, sem).wait()


def _tiles(ref, first, n=1):
    return ref.at[pl.ds(pl.multiple_of(first * ROW_CHUNKS, ROW_CHUNKS), n * ROW_CHUNKS), :]


def _dispatch_kernel(dest_ref, start_ref, cnt_ref, nt_ref, h_hbm, z_hbm, xs_hbm, sem, *, ranges, n_rows, max_tiles):
    CH = DISPATCH_CHUNK
    T = FFN_TILE

    def issue(row0, slot):
        def body(r, carry):
            for kk in range(2):
                d = dest_ref[kk * n_rows + row0 + r]
                pltpu.make_async_copy(_tiles(h_hbm, row0 + r), _tiles(xs_hbm, d), sem.at[slot]).start()
            return carry
        lax.fori_loop(0, CH, body, 0, unroll=8)

    def drain(slot):
        pltpu.make_async_copy(_tiles(h_hbm, 0, 2 * CH), _tiles(xs_hbm, 0, 2 * CH), sem.at[slot]).wait()

    for lo, hi in ranges:
        n_chunks = (hi - lo) // CH

        def chunk(c, carry, lo=lo):
            issue(lo + c * CH, c % 2)

            @pl.when(c > 0)
            def _():
                drain((c - 1) % 2)
            return carry
        lax.fori_loop(0, n_chunks, chunk, 0)
        drain((n_chunks - 1) % 2)

    def tail_copies(e, wait):
        cnt = cnt_ref[e]
        n = (T - (cnt & (T - 1))) & (T - 1)
        first = start_ref[e] + cnt
        for bit in reversed(range(T.bit_length() - 1)):
            size = 1 << bit

            @pl.when((n & size) != 0)
            def _():
                cp = pltpu.make_async_copy(_tiles(z_hbm, 0, size),
                                           _tiles(xs_hbm, first + ((n >> (bit + 1)) << (bit + 1)), size), sem.at[2])
                cp.wait() if wait else cp.start()

    def unused_tile(t, wait):
        cp = pltpu.make_async_copy(z_hbm, _tiles(xs_hbm, t * T, T), sem.at[2])
        cp.wait() if wait else cp.start()

    for wait in (False, True):
        def per_expert(e, carry, wait=wait):
            tail_copies(e, wait)
            return carry

        def per_tile(t, carry, wait=wait):
            unused_tile(t, wait)
            return carry
        lax.fori_loop(0, N_EXPERTS, per_expert, 0)
        lax.fori_loop(nt_ref[0], max_tiles, per_tile, 0)


def _dispatch(dest, start, counts, n_tiles, h1t, ranges, max_tiles):
    T = FFN_TILE
    n_rows = h1t.shape[0] // ROW_CHUNKS
    zeros = jnp.zeros((T * ROW_CHUNKS, LANES), F32)
    return pl.pallas_call(
        functools.partial(_dispatch_kernel, ranges=ranges, n_rows=n_rows, max_tiles=max_tiles),
        grid_spec=pltpu.PrefetchScalarGridSpec(
            num_scalar_prefetch=4,
            grid=(1,),
            in_specs=[pl.BlockSpec(memory_space=pl.ANY), pl.BlockSpec(memory_space=pl.ANY)],
            out_specs=pl.BlockSpec(memory_space=pl.ANY),
            scratch_shapes=[pltpu.SemaphoreType.DMA((3,))]),
        out_shape=jax.ShapeDtypeStruct((max_tiles * T * ROW_CHUNKS, LANES), F32),
        compiler_params=pltpu.CompilerParams(dimension_semantics=("arbitrary",), has_side_effects=True),
        name="dispatch",
    )(dest, start, counts, n_tiles, h1t, zeros)


def _ffn_kernel(te_ref, nt_ref, x_ref, wg_ref, wu_ref, wd_ref, out_ref, wgb, wub, wdb):
    i = pl.program_id(0)
    nt = nt_ref[0]
    T = FFN_TILE

    @pl.when(i < nt)
    def _():
        @pl.when((i == 0) | (te_ref[i] != te_ref[jnp.maximum(i - 1, 0)]))
        def _():
            wgb[...] = wg_ref[0].astype(BF16)
            wub[...] = wu_ref[0].astype(BF16)
            wdb[...] = wd_ref[0].astype(BF16)

        x = _load_row_tiles(x_ref, T).astype(BF16)
        g = _dot(x, wgb[...])
        u = _dot(x, wub[...])
        _store_row_tiles(out_ref, _dot((g * _sigmoid(g) * u).astype(BF16), wdb[...]))

    @pl.when(i >= nt)
    def _():
        out_ref[...] = jnp.zeros_like(out_ref)


def _ffn(tile_expert, n_tiles, xs, w_g, w_u, w_d):
    T = FFN_TILE
    max_tiles = tile_expert.shape[0]
    wspec = lambda a: pl.BlockSpec((1,) + a.shape[1:], lambda i, te, nt: (te[i], 0, 0))
    tile = lambda imap: pl.BlockSpec((T * ROW_CHUNKS, LANES), imap)
    return pl.pallas_call(
        _ffn_kernel,
        grid_spec=pltpu.PrefetchScalarGridSpec(
            num_scalar_prefetch=2,
            grid=(max_tiles,),
            in_specs=[tile(lambda i, te, nt: (jnp.minimum(i, nt[0] - 1), 0)), wspec(w_g), wspec(w_u), wspec(w_d)],
            out_specs=tile(lambda i, te, nt: (i, 0)),
            scratch_shapes=[pltpu.VMEM((D_MODEL, D_EXPERT), BF16), pltpu.VMEM((D_MODEL, D_EXPERT), BF16),
                            pltpu.VMEM((D_EXPERT, D_MODEL), BF16)]),
        out_shape=jax.ShapeDtypeStruct(xs.shape, F32),
        compiler_params=pltpu.CompilerParams(dimension_semantics=("arbitrary",)),
        name="ffn",
    )(tile_expert, n_tiles, xs, w_g, w_u, w_d)


def _combine_kernel(dest_ref, h_ref, w_ref, g_ref, b_ref, ys_hbm, y_ref, buf, sem, *, blk, n_rows):
    i = pl.program_id(0)
    n = pl.num_programs(0)
    T = CMB_TILE

    def gather(step, slot):
        for kk in range(2):
            _gather_row_tiles(dest_ref, kk * n_rows + blk(step) * T, ys_hbm, buf.at[slot, kk], sem.at[slot], T)

    @pl.when(i == 0)
    def _():
        gather(0, 0)

    @pl.when(i + 1 < n)
    def _():
        gather(i + 1, (i + 1) % 2)

    slot = i % 2
    for kk in range(2):
        _wait_row_tiles(ys_hbm, buf.at[slot, kk], sem.at[slot], T)
    w = w_ref[...]
    ff = w[:, 0:1] * _load_row_tiles(buf.at[slot, 0], T) + w[:, 1:2] * _load_row_tiles(buf.at[slot, 1], T)
    y_ref[...] = _ln(DN_ALPHA * _load_row_tiles(h_ref, T) + ff, g_ref[...], b_ref[...])


def _combine(dest, h1t, wts_t, ys, g2, b2, n_out, first_block, blocks_per_batch, skip_blocks):
    T = CMB_TILE
    n_rows = h1t.shape[0] // ROW_CHUNKS
    if skip_blocks:
        blk = lambda i: first_block + i + (i // blocks_per_batch + 1) * skip_blocks
    else:
        blk = lambda i: first_block + i
    full = lambda a: pl.BlockSpec(a.shape, lambda i, d: (0,) * a.ndim)
    return pl.pallas_call(
        functools.partial(_combine_kernel, blk=blk, n_rows=n_rows),
        grid_spec=pltpu.PrefetchScalarGridSpec(
            num_scalar_prefetch=1,
            grid=(n_out // T,),
            in_specs=[pl.BlockSpec((T * ROW_CHUNKS, LANES), lambda i, d: (blk(i), 0)),
                      pl.BlockSpec((T, 2), lambda i, d: (blk(i), 0)),
                      full(g2), full(b2), pl.BlockSpec(memory_space=pl.ANY)],
            out_specs=pl.BlockSpec((T, D_MODEL), lambda i, d: (i, 0)),
            scratch_shapes=[pltpu.VMEM((2, 2, T * ROW_CHUNKS, LANES), F32), pltpu.SemaphoreType.DMA((2,))]),
        out_shape=jax.ShapeDtypeStruct((n_out, D_MODEL), F32),
        compiler_params=pltpu.CompilerParams(dimension_semantics=("arbitrary",)),
        name="combine",
    )(dest, h1t, wts_t, g2, b2, ys)


def _dispatch_plan(routing, counts, max_tiles):
    T = FFN_TILE
    tiles_e = (counts + T - 1) // T
    tile_end = jnp.cumsum(tiles_e)
    n_tiles = tile_end[-1]
    start = (tile_end - tiles_e) * T
    ids, rank = routing[0:2], routing[2:4]
    onehot = (ids[..., None] == jnp.arange(N_EXPERTS, dtype=jnp.int32)).astype(jnp.int32)
    dest = (jnp.sum(onehot * start, axis=-1) + jnp.where(ids >= 0, rank, 0)).reshape(-1).astype(jnp.int32)
    te = jnp.sum((jnp.arange(max_tiles, dtype=jnp.int32)[:, None] >= tile_end[None, :]).astype(jnp.int32), axis=1)
    te_last = jnp.take(te, jnp.maximum(n_tiles - 1, 0))
    te = jnp.where(jnp.arange(max_tiles) < n_tiles, te, te_last).astype(jnp.int32)
    return te, n_tiles.reshape(1).astype(jnp.int32), start.astype(jnp.int32), dest


def kernel(x_prompt, x_sample, state_swa_k, state_swa_v, state_gla, meta_tokens, ln_emb_g, ln_emb_b, w_in, b_gate, attn_sink, w_alpha2, b_alpha, gla_norm_g, w_attn_br, w_gla_br, w_out, ln1_g, ln1_b, w_router_group, b_router_group, w_router_expert, b_router_expert, w_exp_gate, w_exp_up, w_exp_down, ln2_g, ln2_b):
    B, seq, _ = x_prompt.shape
    n_seq, t_s, _ = x_sample.shape
    depth = w_in.shape[0]
    assert depth == 1 and seq % ATT_BLOCK == 0 and t_s == 8
    lp = SKIP_ROWS + seq
    NP, NS = B * lp, n_seq * t_s
    NR = NP + NS
    assert NP % ROW_TILE == 0 and NS % ROW_TILE == 0 and n_seq % SAMPLE_SEQS == 0
    assert seq % DISPATCH_CHUNK == 0 and NS % DISPATCH_CHUNK == 0
    l = 0
    row2 = lambda a: a.reshape(1, -1)

    xp = jnp.concatenate([jnp.zeros((B, FRONT_PAD, D_MODEL), F32),
                          jnp.broadcast_to(meta_tokens[None], (B, N_META, D_MODEL)), x_prompt], axis=1)
    x_all = jnp.concatenate([xp.reshape(NP, D_MODEL), x_sample.reshape(NS, D_MODEL)], axis=0)
    pos = np.arange(NR)
    in_prompt = pos < NP
    keep = jnp.asarray(~(in_prompt & (pos % lp < FRONT_PAD)), F32).reshape(NR, 1)
    moe_valid = jnp.asarray(~(in_prompt & (pos % lp < SKIP_ROWS)), F32).reshape(1, NR)

    wi = w_in[l]
    c_ga = sum((Q_W, KV_W, KV_W, GLA_DK, GLA_DK, GLA_DV))
    w_bf = jnp.concatenate([wi[:, :c_ga], wi[:, c_ga + GLA_RANK:], wi[:, c_ga:c_ga + GLA_RANK],
                            jnp.zeros((D_MODEL, 128 - GLA_RANK), F32)], axis=1).astype(BF16)
    wa2_bf = jnp.concatenate([w_alpha2[l], jnp.zeros((128 - GLA_RANK, GLA_DK), F32)], axis=0).astype(BF16)

    q, k, v, gq, gk, gv, la, gr, gate = _inproj(x_all, keep, row2(ln_emb_g), row2(ln_emb_b), w_bf, wa2_bf,
                                                row2(b_alpha[l]))

    sink = attn_sink[l]
    ya_p = _swa_prompt(sink, q, k, v, B, lp)
    buf_k = state_swa_k[l].reshape(n_seq, WINDOW, KV_W)
    buf_v = state_swa_v[l].reshape(n_seq, WINDOW, KV_W)
    ya_s, nk_s, nv_s = _swa_sample(sink, q, k, v, buf_k, buf_v, NP, t_s)

    og_p, s_p = _gla_prompt(gq, gk, gv, la, B, lp)
    og_s, s_s = _gla_sample(gq, gk, gv, la, state_gla[l], NP, t_s)

    wr = jnp.concatenate([w_router_expert[l], w_router_group[l],
                          jnp.zeros((D_MODEL, LANES - N_EXPERTS - N_GROUPS), F32)], axis=1)
    br = jnp.concatenate([b_router_expert[l], b_router_group[l],
                          jnp.zeros((LANES - N_EXPERTS - N_GROUPS,), F32)]).reshape(1, LANES)
    wr_hi = wr.astype(BF16)
    h1t, routing, wts, counts = _merge(x_all, ya_p, ya_s, og_p, og_s, gr, gate, moe_valid, row2(ln_emb_g),
                                       row2(ln_emb_b), row2(gla_norm_g[l]), row2(b_gate[l]),
                                       w_attn_br[l].astype(BF16), w_gla_br[l].astype(BF16), w_out[l].astype(BF16),
                                       row2(ln1_g[l]), row2(ln1_b[l]), wr_hi, (wr - wr_hi.astype(F32)).astype(BF16), br)

    n_tok = B * seq + NS
    max_tiles = (2 * n_tok) // FFN_TILE + N_EXPERTS
    counts = counts[:, 0]
    te, n_tiles, start, dest = _dispatch_plan(routing[0:4], counts, max_tiles)
    routed = tuple((b * lp + SKIP_ROWS, (b + 1) * lp) for b in range(B)) + ((NP, NR),)
    xs = _dispatch(dest, start, counts, n_tiles, h1t, routed, max_tiles)
    ys = _ffn(te, n_tiles, xs, w_exp_gate[l], w_exp_up[l], w_exp_down[l])

    wts_t = wts[0:2].T
    g2, b2 = row2(ln2_g[l]), row2(ln2_b[l])
    skip_blocks = SKIP_ROWS // CMB_TILE
    y_p = _combine(dest, h1t, wts_t, ys, g2, b2, B * seq, 0, seq // CMB_TILE, skip_blocks)
    y_s = _combine(dest, h1t, wts_t, ys, g2, b2, NS, NP // CMB_TILE, 1, 0)

    kv_shape = (B, lp, N_KV, HEAD_DIM)
    k_p = k[:NP].reshape(kv_shape)[:, -WINDOW:]
    v_p = v[:NP].reshape(kv_shape)[:, -WINDOW:]
    return (y_p.reshape(B, seq, D_MODEL), y_s.reshape(n_seq, t_s, D_MODEL),
            k_p[None], v_p[None], s_p[None],
            nk_s.reshape(1, n_seq, WINDOW, N_KV, HEAD_DIM), nv_s.reshape(1, n_seq, WINDOW, N_KV, HEAD_DIM),
            s_s[None])
```

```python
import functools

import numpy as np
import jax
import jax.numpy as jnp
from jax import lax
from jax.experimental import pallas as pl
from jax.experimental.pallas import tpu as pltpu

F32 = jnp.float32
BF16 = jnp.bfloat16

D_MODEL = 1024
N_META = 16
HEAD_DIM = 64
N_HEADS = 8
N_KV = 2
Q_PER_KV = 4
WINDOW = 128
ATT_BLOCK = 128
GLA_HEADS = 4
GLA_HK = 128
GLA_HV = 256
GLA_DK = GLA_HEADS * GLA_HK
GLA_DV = GLA_HEADS * GLA_HV
GLA_RANK = 16
GLA_TAU = 16.0
GLA_CHUNK = 64
N_GROUPS = 4
EXP_PER_GROUP = 8
N_EXPERTS = 32
D_EXPERT = 256
DN_ALPHA = 2.0 ** 0.25
EPS = 1e-5
NEG = -1e30

FRONT_PAD = (-N_META) % ATT_BLOCK
SKIP_ROWS = FRONT_PAD + N_META

Q_W, KV_W = N_HEADS * HEAD_DIM, N_KV * HEAD_DIM
SEG = {}
_o = 0
for _n, _w in (("q", Q_W), ("k", KV_W), ("v", KV_W), ("gq", GLA_DK), ("gk", GLA_DK), ("gv", GLA_DV),
               ("gr", GLA_DV), ("gate", 2 * D_MODEL), ("ga", 128)):
    SEG[_n] = (_o, _o + _w)
    _o += _w
W_IN_COLS = _o

ROW_TILE = 256
FFN_TILE = 256
CMB_TILE = 128
DISPATCH_CHUNK = 128
SAMPLE_SEQS = 8
LANES = 128
ROW_CHUNKS = D_MODEL // LANES
VMEM_LIMIT = 56 * 1024 * 1024


def _ln(x, g, b):
    mu = jnp.mean(x, -1, keepdims=True)
    xc = x - mu
    var = jnp.mean(xc * xc, -1, keepdims=True)
    return xc * lax.rsqrt(var + EPS) * g + b


def _sigmoid(x):
    return 0.5 * jnp.tanh(0.5 * x) + 0.5


def _dot(a, b):
    return jnp.dot(a, b, preferred_element_type=F32)


def _dot_nt(a, b):
    return lax.dot_general(a, b, (((1,), (1,)), ((), ())), preferred_element_type=F32)


def _inproj_kernel(x_ref, keep_ref, g_ref, b_ref, w_ref, wa2_ref, ba_ref,
                   q_ref, k_ref, v_ref, gq_ref, gk_ref, gv_ref, la_ref, gr_ref, gate_ref):
    hb = _ln(x_ref[...], g_ref[...], b_ref[...]).astype(BF16)
    keep = keep_ref[...]

    def seg(name):
        a, b = SEG[name]
        return _dot(hb, w_ref[:, a:b])

    q_ref[...] = seg("q")
    k_ref[...] = seg("k")
    v_ref[...] = seg("v")
    gq_ref[...] = seg("gq") * (GLA_HK ** -0.5)
    gk_ref[...] = seg("gk") * keep
    gv_ref[...] = seg("gv") * keep
    gr_ref[...] = seg("gr")
    gate_ref[...] = seg("gate")
    z = _dot(seg("ga").astype(BF16), wa2_ref[...]) + ba_ref[...]
    la = (jnp.minimum(z, 0.0) - jnp.log(1.0 + jnp.exp(-jnp.abs(z)))) * (1.0 / GLA_TAU)
    la_ref[...] = la * keep


def _inproj(x_all, keep, ln_g, ln_b, w_bf, wa2_bf, b_alpha):
    n = x_all.shape[0]
    tm = ROW_TILE
    widths = [Q_W, KV_W, KV_W, GLA_DK, GLA_DK, GLA_DV, GLA_DK, GLA_DV, 2 * D_MODEL]
    row = lambda w: pl.BlockSpec((tm, w), lambda i: (i, 0))
    full = lambda a: pl.BlockSpec(a.shape, lambda i: (0,) * a.ndim)
    return pl.pallas_call(
        _inproj_kernel,
        grid=(n // tm,),
        in_specs=[row(D_MODEL), row(1), full(ln_g), full(ln_b), full(w_bf), full(wa2_bf), full(b_alpha)],
        out_specs=[row(w) for w in widths],
        out_shape=[jax.ShapeDtypeStruct((n, w), F32) for w in widths],
        compiler_params=pltpu.CompilerParams(dimension_semantics=("parallel",), vmem_limit_bytes=VMEM_LIMIT),
        name="inproj",
    )(x_all, keep, ln_g, ln_b, w_bf, wa2_bf, b_alpha)


def _softmax_pv(s, sink, vv):
    m = jnp.maximum(jnp.max(s, -1, keepdims=True), sink)
    p = jnp.exp(s - m)
    l = jnp.sum(p, -1, keepdims=True) + jnp.exp(sink - m)
    return _dot(p.astype(BF16), vv) / l


def _swa_prompt_kernel(sink_ref, q_ref, kp_ref, kc_ref, vp_ref, vc_ref, o_ref):
    j = pl.program_id(1)
    q = q_ref[...]
    kb = jnp.concatenate([kp_ref[...], kc_ref[...]], 0)
    vb = jnp.concatenate([vp_ref[...], vc_ref[...]], 0)
    r = lax.broadcasted_iota(jnp.int32, (ATT_BLOCK, 2 * ATT_BLOCK), 0)
    c = lax.broadcasted_iota(jnp.int32, (ATT_BLOCK, 2 * ATT_BLOCK), 1)
    dist = r - c + ATT_BLOCK
    k_pos = (j - 1) * ATT_BLOCK + c - FRONT_PAD
    mask = (dist >= 0) & (dist < WINDOW) & (k_pos >= 0)
    distf = dist.astype(F32)
    for h in range(N_HEADS):
        kv = h // Q_PER_KV
        qh = q[:, h * HEAD_DIM:(h + 1) * HEAD_DIM].astype(BF16)
        kk = kb[:, kv * HEAD_DIM:(kv + 1) * HEAD_DIM].astype(BF16)
        vv = vb[:, kv * HEAD_DIM:(kv + 1) * HEAD_DIM].astype(BF16)
        s = _dot_nt(qh, kk) * (HEAD_DIM ** -0.5) - (2.0 ** -(h + 1)) * distf
        s = jnp.where(mask, s, NEG)
        o_ref[:, h * HEAD_DIM:(h + 1) * HEAD_DIM] = _softmax_pv(s, sink_ref[h], vv)


def _swa_prompt(sink, q, k, v, batch, lp):
    nb = lp // ATT_BLOCK
    n = batch * lp
    cur = lambda w: pl.BlockSpec((ATT_BLOCK, w), lambda b, j: (b * nb + j, 0))
    prev = lambda w: pl.BlockSpec((ATT_BLOCK, w), lambda b, j: (b * nb + jnp.maximum(j - 1, 0), 0))
    return pl.pallas_call(
        _swa_prompt_kernel,
        grid=(batch, nb),
        in_specs=[pl.BlockSpec(memory_space=pltpu.SMEM), cur(Q_W), prev(KV_W), cur(KV_W), prev(KV_W), cur(KV_W)],
        out_specs=cur(Q_W),
        out_shape=jax.ShapeDtypeStruct((n, Q_W), F32),
        compiler_params=pltpu.CompilerParams(dimension_semantics=("parallel", "parallel")),
        name="swa_prompt",
    )(sink, q, k, k, v, v)


def _swa_sample_kernel(sink_ref, q_ref, k_ref, v_ref, bk_ref, bv_ref, o_ref, nk_ref, nv_ref, *, t_s):
    nbuf = WINDOW
    span = 2 * WINDOW
    rows = Q_PER_KV * t_s
    r = lax.broadcasted_iota(jnp.int32, (rows, span), 0)
    c = lax.broadcasted_iota(jnp.int32, (rows, span), 1)
    t = r % t_s
    dist = t + nbuf - c
    mask = (dist >= 0) & (dist < WINDOW) & (c < nbuf + t_s)
    distf = dist.astype(F32)
    g_col = lax.broadcasted_iota(jnp.int32, (rows, 1), 0) // t_s
    fill = jnp.zeros((span - nbuf - t_s, KV_W), F32)

    def one_seq(s, carry):
        rs = pl.ds(pl.multiple_of(s * t_s, t_s), t_s)
        q = q_ref[rs, :]
        k_new = k_ref[rs, :]
        v_new = v_ref[rs, :]
        bk = bk_ref[s]
        bv = bv_ref[s]
        k_all = jnp.concatenate([bk, k_new, fill], 0)
        v_all = jnp.concatenate([bv, v_new, fill], 0)
        for kv in range(N_KV):
            qg = jnp.concatenate(
                [q[:, (kv * Q_PER_KV + g) * HEAD_DIM:(kv * Q_PER_KV + g + 1) * HEAD_DIM] for g in range(Q_PER_KV)], 0)
            kk = k_all[:, kv * HEAD_DIM:(kv + 1) * HEAD_DIM].astype(BF16)
            vv = v_all[:, kv * HEAD_DIM:(kv + 1) * HEAD_DIM].astype(BF16)
            slope = jnp.zeros((rows, 1), F32)
            sink = jnp.zeros((rows, 1), F32)
            for g in range(Q_PER_KV):
                h = kv * Q_PER_KV + g
                slope = jnp.where(g_col == g, 2.0 ** -(h + 1), slope)
                sink = jnp.where(g_col == g, sink_ref[h], sink)
            sc = _dot_nt(qg.astype(BF16), kk) * (HEAD_DIM ** -0.5) - slope * distf
            sc = jnp.where(mask, sc, NEG)
            o = _softmax_pv(sc, sink, vv)
            for g in range(Q_PER_KV):
                h = kv * Q_PER_KV + g
                o_ref[rs, h * HEAD_DIM:(h + 1) * HEAD_DIM] = o[g * t_s:(g + 1) * t_s]
        nk_ref[s, 0:nbuf - t_s, :] = bk[t_s:, :]
        nk_ref[s, nbuf - t_s:nbuf, :] = k_new
        nv_ref[s, 0:nbuf - t_s, :] = bv[t_s:, :]
        nv_ref[s, nbuf - t_s:nbuf, :] = v_new
        return carry

    lax.fori_loop(0, SAMPLE_SEQS, one_seq, 0)


def _swa_sample(sink, q, k, v, buf_k, buf_v, row0, t_s):
    n_seq = buf_k.shape[0]
    sb = SAMPLE_SEQS
    rb = sb * t_s
    b0 = row0 // rb
    rows = lambda w: pl.BlockSpec((rb, w), lambda i: (b0 + i, 0))
    bufs = pl.BlockSpec((sb, WINDOW, KV_W), lambda i: (i, 0, 0))
    return pl.pallas_call(
        functools.partial(_swa_sample_kernel, t_s=t_s),
        grid=(n_seq // sb,),
        in_specs=[pl.BlockSpec(memory_space=pltpu.SMEM), rows(Q_W), rows(KV_W), rows(KV_W), bufs, bufs],
        out_specs=[pl.BlockSpec((rb, Q_W), lambda i: (i, 0)), bufs, bufs],
        out_shape=[jax.ShapeDtypeStruct((n_seq * t_s, Q_W), F32),
                   jax.ShapeDtypeStruct(buf_k.shape, F32), jax.ShapeDtypeStruct(buf_v.shape, F32)],
        compiler_params=pltpu.CompilerParams(dimension_semantics=("parallel",)),
        name="swa_sample",
    )(sink, q, k, v, buf_k, buf_v)


def _gla_tables(chunk):
    t = np.arange(chunk)[:, None]
    u = np.arange(chunk)[None, :]
    masks = []
    w = chunk // 2
    while w >= 1:
        masks.append((t // (2 * w) == u // (2 * w)) & ((t // w) % 2 == 1) & ((u // w) % 2 == 0))
        w //= 2
    return (u <= t).astype(np.float32), np.stack(masks, 0).astype(np.float32)


def _level_exponents(b, la, w):
    C = b.shape[0]
    row = lax.broadcasted_iota(jnp.int32, b.shape, 0)
    if w >= 4:
        pieces = [jnp.broadcast_to(b[p + w - 1:p + w], (2 * w, b.shape[1])) for p in range(0, C, 2 * w)]
        ref = pieces[0] if len(pieces) == 1 else jnp.concatenate(pieces, 0)
        return jnp.where((row & w) != 0, b - ref, ref - b)
    if w == 2:
        m = row & 3
        nxt = pltpu.roll(la, C - 1, 0)
        prv = pltpu.roll(la, 1, 0)
        return jnp.where(m == 2, la, jnp.where(m == 3, la + prv, jnp.where(m == 0, nxt, 0.0)))
    return jnp.where((row & 1) != 0, la, 0.0)


def _split3(x):
    hi = x.astype(BF16)
    r1 = x - hi.astype(F32)
    mid = r1.astype(BF16)
    lo = (r1 - mid.astype(F32)).astype(BF16)
    return hi, mid, lo


def _gla_intra_kernel(g_ref, m_ref, q_ref, k_ref, v_ref, la_ref, o_ref, qe_ref, ke_ref, vt_ref, d_ref):
    C = GLA_CHUNK
    n_lvl = m_ref.shape[0]
    G = g_ref[...]
    eye = (lax.broadcasted_iota(jnp.int32, (C, C), 0) == lax.broadcasted_iota(jnp.int32, (C, C), 1)).astype(F32)
    la = la_ref[...]
    hi, mid, lo = _split3(la)
    b = _dot(G, hi) + _dot(G, mid) + _dot(G, lo)
    b_last = b[C - 1:C]
    q_all = q_ref[...]
    k_all = k_ref[...]
    qe_ref[0] = (q_all * jnp.exp(b)).astype(BF16)
    ke_ref[0] = (k_all * jnp.exp(b_last - b)).astype(BF16)
    d_ref[0, 0] = jnp.broadcast_to(jnp.exp(b_last), (8, GLA_DK))
    q_lvl, k_lvl = [], []
    for l in range(n_lvl):
        El = jnp.exp(_level_exponents(b, la, C >> (l + 1)))
        q_lvl.append((q_all * El).astype(BF16))
        k_lvl.append((k_all * El).astype(BF16))
    for h in range(GLA_HEADS):
        ks = slice(h * GLA_HK, (h + 1) * GLA_HK)
        vs = slice(h * GLA_HV, (h + 1) * GLA_HV)
        v = v_ref[:, vs]
        att = eye * jnp.sum(q_all[:, ks] * k_all[:, ks], -1, keepdims=True)
        for l in range(n_lvl):
            att = att + m_ref[l] * _dot_nt(q_lvl[l][:, ks], k_lvl[l][:, ks])
        o_ref[0, :, vs] = _dot(att.astype(BF16), v.astype(BF16))
        vt_ref[0, 0, vs, :] = v.T.astype(BF16)


def _gla_inter_kernel(o_ref, qe_ref, ke_ref, vt_ref, d_ref, og_ref, s_ref, st_ref):
    c = pl.program_id(0)
    batch = o_ref.shape[0]

    @pl.when(c == 0)
    def _():
        st_ref[...] = jnp.zeros_like(st_ref)

    for b in range(batch):
        for h in range(GLA_HEADS):
            ks = slice(h * GLA_HK, (h + 1) * GLA_HK)
            vs = slice(h * GLA_HV, (h + 1) * GLA_HV)
            st = st_ref[b * GLA_HEADS + h]
            og_ref[b, :, vs] = o_ref[b, :, vs] + _dot_nt(qe_ref[b, :, ks], st.astype(BF16))
            st_ref[b * GLA_HEADS + h] = st * d_ref[b, 0, 0:1, ks] + _dot(vt_ref[b, 0, vs, :], ke_ref[b, :, ks])

    @pl.when(c == pl.num_programs(0) - 1)
    def _():
        for b in range(batch):
            for h in range(GLA_HEADS):
                s_ref[b, h] = st_ref[b * GLA_HEADS + h].T


def _gla_prompt(gq, gk, gv, la, batch, lp):
    C = GLA_CHUNK
    nc = lp // C
    G, M = _gla_tables(C)
    G = jnp.asarray(G, BF16)
    M = jnp.asarray(M, F32)
    rows = lambda w: pl.BlockSpec((C, w), lambda b, c: (b * nc + c, 0))
    rows3 = lambda w: pl.BlockSpec((1, C, w), lambda b, c: (b, c, 0))
    full = lambda a: pl.BlockSpec(a.shape, lambda b, c: (0,) * a.ndim)
    o_intra, qe, ke, vt, d = pl.pallas_call(
        _gla_intra_kernel,
        grid=(batch, nc),
        in_specs=[full(G), full(M), rows(GLA_DK), rows(GLA_DK), rows(GLA_DV), rows(GLA_DK)],
        out_specs=[rows3(GLA_DV), rows3(GLA_DK), rows3(GLA_DK),
                   pl.BlockSpec((1, 1, GLA_DV, C), lambda b, c: (b, c, 0, 0)),
                   pl.BlockSpec((1, 1, 8, GLA_DK), lambda b, c: (b, c, 0, 0))],
        out_shape=[jax.ShapeDtypeStruct((batch, lp, GLA_DV), F32),
                   jax.ShapeDtypeStruct((batch, lp, GLA_DK), BF16), jax.ShapeDtypeStruct((batch, lp, GLA_DK), BF16),
                   jax.ShapeDtypeStruct((batch, nc, GLA_DV, C), BF16),
                   jax.ShapeDtypeStruct((batch, nc, 8, GLA_DK), F32)],
        compiler_params=pltpu.CompilerParams(dimension_semantics=("parallel", "parallel")),
        name="gla_intra",
    )(G, M, gq, gk, gv, la)
    chunk = lambda w: pl.BlockSpec((batch, C, w), lambda c: (0, c, 0))
    og, s_fin = pl.pallas_call(
        _gla_inter_kernel,
        grid=(nc,),
        in_specs=[chunk(GLA_DV), chunk(GLA_DK), chunk(GLA_DK),
                  pl.BlockSpec((batch, 1, GLA_DV, C), lambda c: (0, c, 0, 0)),
                  pl.BlockSpec((batch, 1, 8, GLA_DK), lambda c: (0, c, 0, 0))],
        out_specs=[chunk(GLA_DV), pl.BlockSpec((batch, GLA_HEADS, GLA_HK, GLA_HV), lambda c: (0, 0, 0, 0))],
        out_shape=[jax.ShapeDtypeStruct((batch, lp, GLA_DV), F32),
                   jax.ShapeDtypeStruct((batch, GLA_HEADS, GLA_HK, GLA_HV), F32)],
        scratch_shapes=[pltpu.VMEM((batch * GLA_HEADS, GLA_HV, GLA_HK), F32)],
        compiler_params=pltpu.CompilerParams(dimension_semantics=("arbitrary",)),
        name="gla_inter",
    )(o_intra, qe, ke, vt, d)
    return og.reshape(batch * lp, GLA_DV), s_fin


def _gla_sample_kernel(q_ref, k_ref, v_ref, la_ref, s0_ref, o_ref, s_ref, *, t_s):
    T = t_s
    row = lax.broadcasted_iota(jnp.int32, (T, GLA_HK), 0)
    k_fill = jnp.zeros((GLA_HK - T - 8, GLA_HK), F32)
    v_fill = jnp.zeros((GLA_HK - T, GLA_HV), F32)

    def one_seq(s, carry):
        rs = pl.ds(pl.multiple_of(s * T, T), T)
        for h in range(GLA_HEADS):
            ks = slice(h * GLA_HK, (h + 1) * GLA_HK)
            vs = slice(h * GLA_HV, (h + 1) * GLA_HV)
            q = q_ref[rs, ks]
            k = k_ref[rs, ks]
            v = v_ref[rs, vs]
            b = la_ref[rs, ks]
            sh = 1
            while sh < T:
                b = b + jnp.where(row >= sh, pltpu.roll(b, sh, 0), 0.0)
                sh *= 2
            S = s0_ref[s, h]
            o = _dot((q * jnp.exp(b)).astype(BF16), S.astype(BF16))
            for j in range(T):
                e = jnp.exp(jnp.where(row >= j, b - b[j:j + 1], NEG))
                a_col = jnp.sum(q * k[j:j + 1] * e, -1, keepdims=True)
                o = o + a_col * v[j:j + 1]
            o_ref[rs, vs] = o
            b_last = b[T - 1:T]
            ke = k * jnp.exp(b_last - b)
            kt = jnp.concatenate([ke, jnp.broadcast_to(jnp.exp(b_last), (8, GLA_HK)), k_fill], 0).T
            v_pad = jnp.concatenate([v, v_fill], 0)
            s_ref[s, h] = S * kt[:, T:T + 1] + _dot(kt.astype(BF16), v_pad.astype(BF16))
        return carry

    lax.fori_loop(0, SAMPLE_SEQS, one_seq, 0)


def _gla_sample(gq, gk, gv, la, s0, row0, t_s):
    n_seq = s0.shape[0]
    sb = SAMPLE_SEQS
    rb = sb * t_s
    b0 = row0 // rb
    rows = lambda w: pl.BlockSpec((rb, w), lambda i: (b0 + i, 0))
    st = pl.BlockSpec((sb, GLA_HEADS, GLA_HK, GLA_HV), lambda i: (i, 0, 0, 0))
    return pl.pallas_call(
        functools.partial(_gla_sample_kernel, t_s=t_s),
        grid=(n_seq // sb,),
        in_specs=[rows(GLA_DK), rows(GLA_DK), rows(GLA_DV), rows(GLA_DK), st],
        out_specs=[pl.BlockSpec((rb, GLA_DV), lambda i: (i, 0)), st],
        out_shape=[jax.ShapeDtypeStruct((n_seq * t_s, GLA_DV), F32), jax.ShapeDtypeStruct(s0.shape, F32)],
        compiler_params=pltpu.CompilerParams(dimension_semantics=("parallel",), vmem_limit_bytes=VMEM_LIMIT),
        name="gla_sample",
    )(gq, gk, gv, la, s0)


def _route(lt, valid):
    tm = lt.shape[1]
    el = lt[0:N_EXPERTS]
    gl = lt[N_EXPERTS:N_EXPERTS + N_GROUPS]
    g_max = jnp.max(gl, 0, keepdims=True)
    g_row = lax.broadcasted_iota(jnp.int32, (N_GROUPS, tm), 0)
    g_idx = jnp.min(jnp.where(gl == g_max, g_row, N_GROUPS), 0, keepdims=True)
    p_max = 1.0 / jnp.sum(jnp.exp(gl - g_max), 0, keepdims=True)
    e_row = lax.broadcasted_iota(jnp.int32, (N_EXPERTS, tm), 0)
    m1 = jnp.where(e_row // EXP_PER_GROUP == g_idx, el, -jnp.inf)
    v1 = jnp.max(m1, 0, keepdims=True)
    i1 = jnp.min(jnp.where(m1 == v1, e_row, N_EXPERTS), 0, keepdims=True)
    m2 = jnp.where(e_row == i1, -jnp.inf, m1)
    v2 = jnp.max(m2, 0, keepdims=True)
    i2 = jnp.min(jnp.where(m2 == v2, e_row, N_EXPERTS), 0, keepdims=True)
    e2 = jnp.exp(v2 - v1)
    w1 = p_max / (1.0 + e2)
    w2 = p_max * e2 / (1.0 + e2)
    o_row = lax.broadcasted_iota(jnp.int32, (8, tm), 0)
    ids = jnp.where(o_row == 0, i1, jnp.where(o_row == 1, i2, -1))
    return jnp.where(valid, ids, -1), jnp.where(o_row == 0, w1, jnp.where(o_row == 1, w2, 0.0))


def _store_row_tiles(ref, x):
    t, d = x.shape
    n = d // LANES
    for s in range(n):
        ref[pl.ds(s, t, stride=n), :] = x[:, s * LANES:(s + 1) * LANES]


def _load_row_tiles(ref, t, n=ROW_CHUNKS):
    return jnp.concatenate([ref[pl.ds(s, t, stride=n), :] for s in range(n)], axis=1)


def _merge_kernel(x_ref, yap_ref, yas_ref, ogp_ref, ogs_ref, gr_ref, gate_ref, valid_ref, eg_ref, eb_ref, ng_ref,
                  bg_ref, wa_ref, wg_ref, wo_ref, g1_ref, b1_ref, wrh_ref, wrl_ref, br_ref, u_ref,
                  h1t_ref, ids_ref, wts_ref, cnt_ref, run_ref, *, prompt_tiles):
    @pl.when(pl.program_id(0) == 0)
    def _():
        run_ref[...] = jnp.zeros_like(run_ref)

    h = _ln(x_ref[...], eg_ref[...], eb_ref[...])
    is_prompt = pl.program_id(0) < prompt_tiles
    og = jnp.where(is_prompt, ogp_ref[...], ogs_ref[...])
    ya = jnp.where(is_prompt, yap_ref[...], yas_ref[...])
    parts = []
    for hh in range(GLA_HEADS):
        o = og[:, hh * GLA_HV:(hh + 1) * GLA_HV]
        parts.append(o * lax.rsqrt(jnp.mean(o * o, -1, keepdims=True) + EPS))
    gr = gr_ref[...]
    y_gla = jnp.concatenate(parts, 1) * ng_ref[...] * (gr * _sigmoid(gr))
    a = _dot(ya.astype(BF16), wa_ref[...])
    b = _dot(y_gla.astype(BF16), wg_ref[...])
    gate = _sigmoid(gate_ref[...] + bg_ref[...])
    hm = gate[:, :D_MODEL] * a + gate[:, D_MODEL:] * b
    mix = _dot(hm.astype(BF16), wo_ref[...])
    h1 = _ln(DN_ALPHA * h + mix, g1_ref[...], b1_ref[...])
    _store_row_tiles(h1t_ref, h1)
    h_hi = h1.astype(BF16)
    h_lo = (h1 - h_hi.astype(F32)).astype(BF16)
    logits = _dot(h_hi, wrh_ref[...]) + _dot(h_lo, wrh_ref[...]) + _dot(h_hi, wrl_ref[...]) + br_ref[...]
    ids, wts_ref[...] = _route(logits.T, valid_ref[...] > 0.0)
    tm = ids.shape[1]
    e_row = lax.broadcasted_iota(jnp.int32, (N_EXPERTS, tm), 0)
    run = run_ref[:, 0:1]
    ranks = []
    for kk in range(2):
        onehot = (e_row == ids[kk:kk + 1]).astype(F32)
        before = _dot(onehot.astype(BF16), u_ref[...])
        ranks.append(jnp.sum(onehot * (run + before), 0, keepdims=True).astype(jnp.int32))
        run = run + jnp.sum(onehot, 1, keepdims=True)
    run_ref[...] = jnp.broadcast_to(run, run_ref.shape)
    o_row = lax.broadcasted_iota(jnp.int32, (8, tm), 0)
    ids_ref[...] = jnp.where(o_row == 2, ranks[0], jnp.where(o_row == 3, ranks[1], ids))
    cnt_ref[...] = run_ref[...].astype(jnp.int32)


def _merge(x_all, ya_p, ya_s, og_p, og_s, gr, gate, valid, eg, eb, ng, bg, wa, wg, wo, g1, b1, wrh, wrl, br):
    n = x_all.shape[0]
    tm = ROW_TILE
    u = jnp.asarray(np.triu(np.ones((tm, tm), np.float32), 1), BF16)
    pt = ya_p.shape[0] // tm
    st = ya_s.shape[0] // tm
    row = lambda w: pl.BlockSpec((tm, w), lambda i: (i, 0))
    row_p = lambda w: pl.BlockSpec((tm, w), lambda i: (jnp.minimum(i, pt - 1), 0))
    row_s = lambda w: pl.BlockSpec((tm, w), lambda i: (jnp.clip(i - pt, 0, st - 1), 0))
    lane = lambda r: pl.BlockSpec((r, tm), lambda i: (0, i))
    full = lambda a: pl.BlockSpec(a.shape, lambda i: (0,) * a.ndim)
    return pl.pallas_call(
        functools.partial(_merge_kernel, prompt_tiles=pt),
        grid=(n // tm,),
        in_specs=[row(D_MODEL), row_p(Q_W), row_s(Q_W), row_p(GLA_DV), row_s(GLA_DV), row(GLA_DV), row(2 * D_MODEL),
                  lane(1), full(eg), full(eb), full(ng), full(bg), full(wa), full(wg), full(wo), full(g1), full(b1),
                  full(wrh), full(wrl), full(br), full(u)],
        out_specs=[pl.BlockSpec((tm * ROW_CHUNKS, LANES), lambda i: (i, 0)), lane(8), lane(8),
                   pl.BlockSpec((N_EXPERTS, LANES), lambda i: (0, 0))],
        out_shape=[jax.ShapeDtypeStruct((n * ROW_CHUNKS, LANES), F32),
                   jax.ShapeDtypeStruct((8, n), jnp.int32), jax.ShapeDtypeStruct((8, n), F32),
                   jax.ShapeDtypeStruct((N_EXPERTS, LANES), jnp.int32)],
        scratch_shapes=[pltpu.VMEM((N_EXPERTS, LANES), F32)],
        compiler_params=pltpu.CompilerParams(dimension_semantics=("arbitrary",), vmem_limit_bytes=VMEM_LIMIT),
        name="merge",
    )(x_all, ya_p, ya_s, og_p, og_s, gr, gate, valid, eg, eb, ng, bg, wa, wg, wo, g1, b1, wrh, wrl, br, u)


def _gather_row_tiles(idx_ref, idx0, src_hbm, dst, sem, n):
    def body(r, carry):
        t = idx_ref[idx0 + r]
        pltpu.make_async_copy(src_hbm.at[pl.ds(pl.multiple_of(t * ROW_CHUNKS, ROW_CHUNKS), ROW_CHUNKS), :],
                              dst.at[pl.ds(pl.multiple_of(r * ROW_CHUNKS, ROW_CHUNKS), ROW_CHUNKS), :], sem).start()
        return carry
    lax.fori_loop(0, n, body, 0, unroll=8)


def _wait_row_tiles(src_hbm, dst, sem, n):
    pltpu.make_async_copy(src_hbm.at[pl.ds(0, n * ROW_CHUNKS), :], dst, sem).wait()


def _tiles(ref, first, n=1):
    return ref.at[pl.ds(pl.multiple_of(first * ROW_CHUNKS, ROW_CHUNKS), n * ROW_CHUNKS), :]


def _dispatch_kernel(dest_ref, start_ref, cnt_ref, nt_ref, h_hbm, xs_hbm, buf, zbuf, sem_in, sem_out, sem_z,
                     *, ranges, n_rows, max_tiles):
    CH = DISPATCH_CHUNK
    T = FFN_TILE
    c = pl.program_id(0)
    n_chunks = pl.num_programs(0)

    def load(row0, slot):
        return pltpu.make_async_copy(_tiles(h_hbm, row0, CH), buf.at[slot], sem_in.at[slot])

    def scatter(row0, slot):
        def body(r, carry):
            for kk in range(2):
                d = dest_ref[kk * n_rows + row0 + r]
                pltpu.make_async_copy(_tiles(buf.at[slot], r), _tiles(xs_hbm, d), sem_out.at[slot]).start()
            return carry
        lax.fori_loop(0, CH, body, 0, unroll=8)

    def drain(slot):
        for _ in range(2):
            pltpu.make_async_copy(buf.at[slot], _tiles(xs_hbm, 0, CH), sem_out.at[slot]).wait()

    groups, per_group, first, stride, extra, extra_first = ranges

    def row0(j):
        in_group = first + (j // per_group) * stride + (j % per_group) * CH
        return jnp.where(j < groups * per_group, in_group, extra_first + (j - groups * per_group) * CH)

    @pl.when(c == 0)
    def _():
        for j in range(2):
            load(row0(j), j).start()

    slot = c % 3
    load(row0(c), slot).wait()
    scatter(row0(c), slot)

    @pl.when(c > 0)
    def _():
        drain((c + 2) % 3)

    @pl.when(c + 2 < n_chunks)
    def _():
        load(row0(c + 2), (c + 2) % 3).start()

    @pl.when(c == n_chunks - 1)
    def _():
        drain(slot)
        _zero_unowned_slots(start_ref, cnt_ref, nt_ref, xs_hbm, zbuf, sem_z, max_tiles)


def _zero_unowned_slots(start_ref, cnt_ref, nt_ref, xs_hbm, zbuf, sem_z, max_tiles):
    T = FFN_TILE
    zbuf[...] = jnp.zeros_like(zbuf)

    def tail_copies(e, wait):
        cnt = cnt_ref[e]
        n = (T - (cnt & (T - 1))) & (T - 1)
        first = start_ref[e] + cnt
        for bit in reversed(range(T.bit_length() - 1)):
            size = 1 << bit

            @pl.when((n & size) != 0)
            def _():
                cp = pltpu.make_async_copy(_tiles(zbuf, 0, size),
                                           _tiles(xs_hbm, first + ((n >> (bit + 1)) << (bit + 1)), size), sem_z)
                cp.wait() if wait else cp.start()

    def unused_tile(t, wait):
        cp = pltpu.make_async_copy(zbuf, _tiles(xs_hbm, t * T, T), sem_z)
        cp.wait() if wait else cp.start()

    for wait in (False, True):
        def per_expert(e, carry, wait=wait):
            tail_copies(e, wait)
            return carry

        def per_tile(t, carry, wait=wait):
            unused_tile(t, wait)
            return carry
        lax.fori_loop(0, N_EXPERTS, per_expert, 0)
        lax.fori_loop(nt_ref[0], max_tiles, per_tile, 0)


def _dispatch(dest, start, counts, n_tiles, h1t, ranges, max_tiles):
    T = FFN_TILE
    n_rows = h1t.shape[0] // ROW_CHUNKS
    return pl.pallas_call(
        functools.partial(_dispatch_kernel, ranges=ranges, n_rows=n_rows, max_tiles=max_tiles),
        grid_spec=pltpu.PrefetchScalarGridSpec(
            num_scalar_prefetch=4,
            grid=(ranges[0] * ranges[1] + ranges[4],),
            in_specs=[pl.BlockSpec(memory_space=pl.ANY)],
            out_specs=pl.BlockSpec(memory_space=pl.ANY),
            scratch_shapes=[pltpu.VMEM((3, DISPATCH_CHUNK * ROW_CHUNKS, LANES), F32),
                            pltpu.VMEM((T * ROW_CHUNKS, LANES), F32),
                            pltpu.SemaphoreType.DMA((3,)), pltpu.SemaphoreType.DMA((3,)), pltpu.SemaphoreType.DMA]),
        out_shape=jax.ShapeDtypeStruct((max_tiles * T * ROW_CHUNKS, LANES), F32),
        compiler_params=pltpu.CompilerParams(dimension_semantics=("arbitrary",)),
        name="dispatch",
    )(dest, start, counts, n_tiles, h1t)


def _ffn_kernel(te_ref, nt_ref, x_ref, wg_ref, wu_ref, wd_ref, out_ref, wgb, wub, wdb):
    i = pl.program_id(0)
    nt = nt_ref[0]
    T = FFN_TILE

    @pl.when(i < nt)
    def _():
        @pl.when((i == 0) | (te_ref[i] != te_ref[jnp.maximum(i - 1, 0)]))
        def _():
            wgb[...] = wg_ref[0].astype(BF16)
            wub[...] = wu_ref[0].astype(BF16)
            wdb[...] = wd_ref[0].astype(BF16)

        x = _load_row_tiles(x_ref, T).astype(BF16)
        g = _dot(x, wgb[...])
        u = _dot(x, wub[...])
        _store_row_tiles(out_ref, _dot((g * _sigmoid(g) * u).astype(BF16), wdb[...]))

    @pl.when(i >= nt)
    def _():
        out_ref[...] = jnp.zeros_like(out_ref)


def _ffn(tile_expert, n_tiles, xs, w_g, w_u, w_d):
    T = FFN_TILE
    max_tiles = tile_expert.shape[0]
    wspec = lambda a: pl.BlockSpec((1,) + a.shape[1:], lambda i, te, nt: (te[i], 0, 0))
    tile = lambda imap: pl.BlockSpec((T * ROW_CHUNKS, LANES), imap)
    return pl.pallas_call(
        _ffn_kernel,
        grid_spec=pltpu.PrefetchScalarGridSpec(
            num_scalar_prefetch=2,
            grid=(max_tiles,),
            in_specs=[tile(lambda i, te, nt: (jnp.minimum(i, nt[0] - 1), 0)), wspec(w_g), wspec(w_u), wspec(w_d)],
            out_specs=tile(lambda i, te, nt: (i, 0)),
            scratch_shapes=[pltpu.VMEM((D_MODEL, D_EXPERT), BF16), pltpu.VMEM((D_MODEL, D_EXPERT), BF16),
                            pltpu.VMEM((D_EXPERT, D_MODEL), BF16)]),
        out_shape=jax.ShapeDtypeStruct(xs.shape, F32),
        compiler_params=pltpu.CompilerParams(dimension_semantics=("arbitrary",)),
        name="ffn",
    )(tile_expert, n_tiles, xs, w_g, w_u, w_d)


def _combine_kernel(dest_ref, h_ref, w_ref, g_ref, b_ref, ys_hbm, y_ref, buf, sem, *, blk, n_rows):
    i = pl.program_id(0)
    n = pl.num_programs(0)
    T = CMB_TILE

    def gather(step, slot):
        for kk in range(2):
            _gather_row_tiles(dest_ref, kk * n_rows + blk(step) * T, ys_hbm, buf.at[slot, kk], sem.at[slot], T)

    @pl.when(i == 0)
    def _():
        gather(0, 0)

    @pl.when(i + 1 < n)
    def _():
        gather(i + 1, (i + 1) % 2)

    slot = i % 2
    for kk in range(2):
        _wait_row_tiles(ys_hbm, buf.at[slot, kk], sem.at[slot], T)
    w = w_ref[...]
    ff = w[:, 0:1] * _load_row_tiles(buf.at[slot, 0], T) + w[:, 1:2] * _load_row_tiles(buf.at[slot, 1], T)
    y_ref[...] = _ln(DN_ALPHA * _load_row_tiles(h_ref, T) + ff, g_ref[...], b_ref[...])


def _combine(dest, h1t, wts_t, ys, g2, b2, n_out, first_block, blocks_per_batch, skip_blocks):
    T = CMB_TILE
    n_rows = h1t.shape[0] // ROW_CHUNKS
    if skip_blocks:
        blk = lambda i: first_block + i + (i // blocks_per_batch + 1) * skip_blocks
    else:
        blk = lambda i: first_block + i
    full = lambda a: pl.BlockSpec(a.shape, lambda i, d: (0,) * a.ndim)
    return pl.pallas_call(
        functools.partial(_combine_kernel, blk=blk, n_rows=n_rows),
        grid_spec=pltpu.PrefetchScalarGridSpec(
            num_scalar_prefetch=1,
            grid=(n_out // T,),
            in_specs=[pl.BlockSpec((T * ROW_CHUNKS, LANES), lambda i, d: (blk(i), 0)),
                      pl.BlockSpec((T, 2), lambda i, d: (blk(i), 0)),
                      full(g2), full(b2), pl.BlockSpec(memory_space=pl.ANY)],
            out_specs=pl.BlockSpec((T, D_MODEL), lambda i, d: (i, 0)),
            scratch_shapes=[pltpu.VMEM((2, 2, T * ROW_CHUNKS, LANES), F32), pltpu.SemaphoreType.DMA((2,))]),
        out_shape=jax.ShapeDtypeStruct((n_out, D_MODEL), F32),
        compiler_params=pltpu.CompilerParams(dimension_semantics=("arbitrary",)),
        name="combine",
    )(dest, h1t, wts_t, g2, b2, ys)


def _dispatch_plan(routing, counts, max_tiles):
    T = FFN_TILE
    tiles_e = (counts + T - 1) // T
    tile_end = jnp.cumsum(tiles_e)
    n_tiles = tile_end[-1]
    start = (tile_end - tiles_e) * T
    ids, rank = routing[0:2], routing[2:4]
    onehot = (ids[..., None] == jnp.arange(N_EXPERTS, dtype=jnp.int32)).astype(jnp.int32)
    dest = (jnp.sum(onehot * start, axis=-1) + jnp.where(ids >= 0, rank, 0)).reshape(-1).astype(jnp.int32)
    te = jnp.sum((jnp.arange(max_tiles, dtype=jnp.int32)[:, None] >= tile_end[None, :]).astype(jnp.int32), axis=1)
    te_last = jnp.take(te, jnp.maximum(n_tiles - 1, 0))
    te = jnp.where(jnp.arange(max_tiles) < n_tiles, te, te_last).astype(jnp.int32)
    return te, n_tiles.reshape(1).astype(jnp.int32), start.astype(jnp.int32), dest


def kernel(x_prompt, x_sample, state_swa_k, state_swa_v, state_gla, meta_tokens, ln_emb_g, ln_emb_b, w_in, b_gate, attn_sink, w_alpha2, b_alpha, gla_norm_g, w_attn_br, w_gla_br, w_out, ln1_g, ln1_b, w_router_group, b_router_group, w_router_expert, b_router_expert, w_exp_gate, w_exp_up, w_exp_down, ln2_g, ln2_b):
    B, seq, _ = x_prompt.shape
    n_seq, t_s, _ = x_sample.shape
    depth = w_in.shape[0]
    assert depth == 1 and seq % ATT_BLOCK == 0 and t_s == 8
    lp = SKIP_ROWS + seq
    NP, NS = B * lp, n_seq * t_s
    NR = NP + NS
    assert NP % ROW_TILE == 0 and NS % ROW_TILE == 0 and n_seq % SAMPLE_SEQS == 0
    assert seq % DISPATCH_CHUNK == 0 and NS % DISPATCH_CHUNK == 0
    l = 0
    row2 = lambda a: a.reshape(1, -1)

    xp = jnp.concatenate([jnp.zeros((B, FRONT_PAD, D_MODEL), F32),
                          jnp.broadcast_to(meta_tokens[None], (B, N_META, D_MODEL)), x_prompt], axis=1)
    x_all = jnp.concatenate([xp.reshape(NP, D_MODEL), x_sample.reshape(NS, D_MODEL)], axis=0)
    pos = np.arange(NR)
    in_prompt = pos < NP
    keep = jnp.asarray(~(in_prompt & (pos % lp < FRONT_PAD)), F32).reshape(NR, 1)
    moe_valid = jnp.asarray(~(in_prompt & (pos % lp < SKIP_ROWS)), F32).reshape(1, NR)

    wi = w_in[l]
    c_ga = sum((Q_W, KV_W, KV_W, GLA_DK, GLA_DK, GLA_DV))
    w_bf = jnp.concatenate([wi[:, :c_ga], wi[:, c_ga + GLA_RANK:], wi[:, c_ga:c_ga + GLA_RANK],
                            jnp.zeros((D_MODEL, 128 - GLA_RANK), F32)], axis=1).astype(BF16)
    wa2_bf = jnp.concatenate([w_alpha2[l], jnp.zeros((128 - GLA_RANK, GLA_DK), F32)], axis=0).astype(BF16)

    q, k, v, gq, gk, gv, la, gr, gate = _inproj(x_all, keep, row2(ln_emb_g), row2(ln_emb_b), w_bf, wa2_bf,
                                                row2(b_alpha[l]))

    sink = attn_sink[l]
    ya_p = _swa_prompt(sink, q, k, v, B, lp)
    buf_k = state_swa_k[l].reshape(n_seq, WINDOW, KV_W)
    buf_v = state_swa_v[l].reshape(n_seq, WINDOW, KV_W)
    ya_s, nk_s, nv_s = _swa_sample(sink, q, k, v, buf_k, buf_v, NP, t_s)

    og_p, s_p = _gla_prompt(gq, gk, gv, la, B, lp)
    og_s, s_s = _gla_sample(gq, gk, gv, la, state_gla[l], NP, t_s)

    wr = jnp.concatenate([w_router_expert[l], w_router_group[l],
                          jnp.zeros((D_MODEL, LANES - N_EXPERTS - N_GROUPS), F32)], axis=1)
    br = jnp.concatenate([b_router_expert[l], b_router_group[l],
                          jnp.zeros((LANES - N_EXPERTS - N_GROUPS,), F32)]).reshape(1, LANES)
    wr_hi = wr.astype(BF16)
    h1t, routing, wts, counts = _merge(x_all, ya_p, ya_s, og_p, og_s, gr, gate, moe_valid, row2(ln_emb_g),
                                       row2(ln_emb_b), row2(gla_norm_g[l]), row2(b_gate[l]),
                                       w_attn_br[l].astype(BF16), w_gla_br[l].astype(BF16), w_out[l].astype(BF16),
                                       row2(ln1_g[l]), row2(ln1_b[l]), wr_hi, (wr - wr_hi.astype(F32)).astype(BF16), br)

    n_tok = B * seq + NS
    max_tiles = (2 * n_tok) // FFN_TILE + N_EXPERTS
    counts = counts[:, 0]
    te, n_tiles, start, dest = _dispatch_plan(routing[0:4], counts, max_tiles)
    routed = (B, seq // DISPATCH_CHUNK, SKIP_ROWS, lp, NS // DISPATCH_CHUNK, NP)
    xs = _dispatch(dest, start, counts, n_tiles, h1t, routed, max_tiles)
    ys = _ffn(te, n_tiles, xs, w_exp_gate[l], w_exp_up[l], w_exp_down[l])

    wts_t = wts[0:2].T
    g2, b2 = row2(ln2_g[l]), row2(ln2_b[l])
    skip_blocks = SKIP_ROWS // CMB_TILE
    y_p = _combine(dest, h1t, wts_t, ys, g2, b2, B * seq, 0, seq // CMB_TILE, skip_blocks)
    y_s = _combine(dest, h1t, wts_t, ys, g2, b2, NS, NP // CMB_TILE, 1, 0)

    kv_shape = (B, lp, N_KV, HEAD_DIM)
    k_p = k[:NP].reshape(kv_shape)[:, -WINDOW:]
    v_p = v[:NP].reshape(kv_shape)[:, -WINDOW:]
    return (y_p.reshape(B, seq, D_MODEL), y_s.reshape(n_seq, t_s, D_MODEL),
            k_p[None], v_p[None], s_p[None],
            nk_s.reshape(1, n_seq, WINDOW, N_KV, HEAD_DIM), nv_s.reshape(1, n_seq, WINDOW, N_KV, HEAD_DIM),
            s_s[None])
```

```python
import functools

import numpy as np
import jax
import jax.numpy as jnp
from jax import lax
from jax.experimental import pallas as pl
from jax.experimental.pallas import tpu as pltpu

F32 = jnp.float32
BF16 = jnp.bfloat16

D_MODEL = 1024
N_META = 16
HEAD_DIM = 64
N_HEADS = 8
N_KV = 2
Q_PER_KV = 4
WINDOW = 128
ATT_BLOCK = 128
GLA_HEADS = 4
GLA_HK = 128
GLA_HV = 256
GLA_DK = GLA_HEADS * GLA_HK
GLA_DV = GLA_HEADS * GLA_HV
GLA_RANK = 16
GLA_TAU = 16.0
GLA_CHUNK = 64
N_GROUPS = 4
EXP_PER_GROUP = 8
N_EXPERTS = 32
D_EXPERT = 256
DN_ALPHA = 2.0 ** 0.25
EPS = 1e-5
NEG = -1e30

FRONT_PAD = (-N_META) % ATT_BLOCK
SKIP_ROWS = FRONT_PAD + N_META

Q_W, KV_W = N_HEADS * HEAD_DIM, N_KV * HEAD_DIM
SEG = {}
_o = 0
for _n, _w in (("q", Q_W), ("k", KV_W), ("v", KV_W), ("gq", GLA_DK), ("gk", GLA_DK), ("gv", GLA_DV),
               ("gr", GLA_DV), ("gate", 2 * D_MODEL), ("ga", 128)):
    SEG[_n] = (_o, _o + _w)
    _o += _w
W_IN_COLS = _o

ROW_TILE = 256
FFN_TILE = 256
CMB_TILE = 128
DISPATCH_CHUNK = 128
GLA_INTRA_CHUNKS = 2
SAMPLE_SEQS = 8
LANES = 128
ROW_CHUNKS = D_MODEL // LANES
VMEM_LIMIT = 56 * 1024 * 1024


def _ln(x, g, b):
    mu = jnp.mean(x, -1, keepdims=True)
    xc = x - mu
    var = jnp.mean(xc * xc, -1, keepdims=True)
    return xc * lax.rsqrt(var + EPS) * g + b


def _sigmoid(x):
    return 0.5 * jnp.tanh(0.5 * x) + 0.5


def _dot(a, b):
    return jnp.dot(a, b, preferred_element_type=F32)


def _dot_nt(a, b):
    return lax.dot_general(a, b, (((1,), (1,)), ((), ())), preferred_element_type=F32)


def _inproj_kernel(x_ref, keep_ref, g_ref, b_ref, w_ref, wa2_ref, ba_ref,
                   q_ref, k_ref, v_ref, gq_ref, gk_ref, gv_ref, la_ref, gr_ref, gate_ref):
    hb = _ln(x_ref[...], g_ref[...], b_ref[...]).astype(BF16)
    keep = keep_ref[...]

    def seg(name):
        a, b = SEG[name]
        return _dot(hb, w_ref[:, a:b])

    q_ref[...] = seg("q")
    k_ref[...] = seg("k")
    v_ref[...] = seg("v")
    gq_ref[...] = seg("gq") * (GLA_HK ** -0.5)
    gk_ref[...] = seg("gk") * keep
    gv_ref[...] = seg("gv") * keep
    gr_ref[...] = seg("gr")
    gate_ref[...] = seg("gate")
    z = _dot(seg("ga").astype(BF16), wa2_ref[...]) + ba_ref[...]
    la = (jnp.minimum(z, 0.0) - jnp.log(1.0 + jnp.exp(-jnp.abs(z)))) * (1.0 / GLA_TAU)
    la_ref[...] = la * keep


def _inproj(x_all, keep, ln_g, ln_b, w_bf, wa2_bf, b_alpha):
    n = x_all.shape[0]
    tm = ROW_TILE
    widths = [Q_W, KV_W, KV_W, GLA_DK, GLA_DK, GLA_DV, GLA_DK, GLA_DV, 2 * D_MODEL]
    row = lambda w: pl.BlockSpec((tm, w), lambda i: (i, 0))
    full = lambda a: pl.BlockSpec(a.shape, lambda i: (0,) * a.ndim)
    return pl.pallas_call(
        _inproj_kernel,
        grid=(n // tm,),
        in_specs=[row(D_MODEL), row(1), full(ln_g), full(ln_b), full(w_bf), full(wa2_bf), full(b_alpha)],
        out_specs=[row(w) for w in widths],
        out_shape=[jax.ShapeDtypeStruct((n, w), F32) for w in widths],
        compiler_params=pltpu.CompilerParams(dimension_semantics=("parallel",), vmem_limit_bytes=VMEM_LIMIT),
        name="inproj",
    )(x_all, keep, ln_g, ln_b, w_bf, wa2_bf, b_alpha)


def _softmax_pv(s, sink, vv):
    m = jnp.maximum(jnp.max(s, -1, keepdims=True), sink)
    p = jnp.exp(s - m)
    l = jnp.sum(p, -1, keepdims=True) + jnp.exp(sink - m)
    return _dot(p.astype(BF16), vv) / l


def _swa_prompt_kernel(sink_ref, q_ref, kp_ref, kc_ref, vp_ref, vc_ref, o_ref):
    j = pl.program_id(1)
    q = q_ref[...]
    kb = jnp.concatenate([kp_ref[...], kc_ref[...]], 0)
    vb = jnp.concatenate([vp_ref[...], vc_ref[...]], 0)
    r = lax.broadcasted_iota(jnp.int32, (ATT_BLOCK, 2 * ATT_BLOCK), 0)
    c = lax.broadcasted_iota(jnp.int32, (ATT_BLOCK, 2 * ATT_BLOCK), 1)
    dist = r - c + ATT_BLOCK
    k_pos = (j - 1) * ATT_BLOCK + c - FRONT_PAD
    mask = (dist >= 0) & (dist < WINDOW) & (k_pos >= 0)
    distf = dist.astype(F32)
    for h in range(N_HEADS):
        kv = h // Q_PER_KV
        qh = q[:, h * HEAD_DIM:(h + 1) * HEAD_DIM].astype(BF16)
        kk = kb[:, kv * HEAD_DIM:(kv + 1) * HEAD_DIM].astype(BF16)
        vv = vb[:, kv * HEAD_DIM:(kv + 1) * HEAD_DIM].astype(BF16)
        s = _dot_nt(qh, kk) * (HEAD_DIM ** -0.5) - (2.0 ** -(h + 1)) * distf
        s = jnp.where(mask, s, NEG)
        o_ref[:, h * HEAD_DIM:(h + 1) * HEAD_DIM] = _softmax_pv(s, sink_ref[h], vv)


def _swa_prompt(sink, q, k, v, batch, lp):
    nb = lp // ATT_BLOCK
    n = batch * lp
    cur = lambda w: pl.BlockSpec((ATT_BLOCK, w), lambda b, j: (b * nb + j, 0))
    prev = lambda w: pl.BlockSpec((ATT_BLOCK, w), lambda b, j: (b * nb + jnp.maximum(j - 1, 0), 0))
    return pl.pallas_call(
        _swa_prompt_kernel,
        grid=(batch, nb),
        in_specs=[pl.BlockSpec(memory_space=pltpu.SMEM), cur(Q_W), prev(KV_W), cur(KV_W), prev(KV_W), cur(KV_W)],
        out_specs=cur(Q_W),
        out_shape=jax.ShapeDtypeStruct((n, Q_W), F32),
        compiler_params=pltpu.CompilerParams(dimension_semantics=("parallel", "parallel")),
        name="swa_prompt",
    )(sink, q, k, k, v, v)


def _swa_sample_kernel(sink_ref, q_ref, k_ref, v_ref, bk_ref, bv_ref, o_ref, nk_ref, nv_ref, *, t_s):
    nbuf = WINDOW
    span = 2 * WINDOW
    rows = Q_PER_KV * t_s
    r = lax.broadcasted_iota(jnp.int32, (rows, span), 0)
    c = lax.broadcasted_iota(jnp.int32, (rows, span), 1)
    t = r % t_s
    dist = t + nbuf - c
    mask = (dist >= 0) & (dist < WINDOW) & (c < nbuf + t_s)
    distf = dist.astype(F32)
    g_col = lax.broadcasted_iota(jnp.int32, (rows, 1), 0) // t_s
    fill = jnp.zeros((span - nbuf - t_s, KV_W), F32)

    def one_seq(s, carry):
        rs = pl.ds(pl.multiple_of(s * t_s, t_s), t_s)
        q = q_ref[rs, :]
        k_new = k_ref[rs, :]
        v_new = v_ref[rs, :]
        bk = bk_ref[s]
        bv = bv_ref[s]
        k_all = jnp.concatenate([bk, k_new, fill], 0)
        v_all = jnp.concatenate([bv, v_new, fill], 0)
        for kv in range(N_KV):
            qg = jnp.concatenate(
                [q[:, (kv * Q_PER_KV + g) * HEAD_DIM:(kv * Q_PER_KV + g + 1) * HEAD_DIM] for g in range(Q_PER_KV)], 0)
            kk = k_all[:, kv * HEAD_DIM:(kv + 1) * HEAD_DIM].astype(BF16)
            vv = v_all[:, kv * HEAD_DIM:(kv + 1) * HEAD_DIM].astype(BF16)
            slope = jnp.zeros((rows, 1), F32)
            sink = jnp.zeros((rows, 1), F32)
            for g in range(Q_PER_KV):
                h = kv * Q_PER_KV + g
                slope = jnp.where(g_col == g, 2.0 ** -(h + 1), slope)
                sink = jnp.where(g_col == g, sink_ref[h], sink)
            sc = _dot_nt(qg.astype(BF16), kk) * (HEAD_DIM ** -0.5) - slope * distf
            sc = jnp.where(mask, sc, NEG)
            o = _softmax_pv(sc, sink, vv)
            for g in range(Q_PER_KV):
                h = kv * Q_PER_KV + g
                o_ref[rs, h * HEAD_DIM:(h + 1) * HEAD_DIM] = o[g * t_s:(g + 1) * t_s]
        nk_ref[s, 0:nbuf - t_s, :] = bk[t_s:, :]
        nk_ref[s, nbuf - t_s:nbuf, :] = k_new
        nv_ref[s, 0:nbuf - t_s, :] = bv[t_s:, :]
        nv_ref[s, nbuf - t_s:nbuf, :] = v_new
        return carry

    lax.fori_loop(0, SAMPLE_SEQS, one_seq, 0, unroll=4)


def _swa_sample(sink, q, k, v, buf_k, buf_v, row0, t_s):
    n_seq = buf_k.shape[0]
    sb = SAMPLE_SEQS
    rb = sb * t_s
    b0 = row0 // rb
    rows = lambda w: pl.BlockSpec((rb, w), lambda i: (b0 + i, 0))
    bufs = pl.BlockSpec((sb, WINDOW, KV_W), lambda i: (i, 0, 0))
    return pl.pallas_call(
        functools.partial(_swa_sample_kernel, t_s=t_s),
        grid=(n_seq // sb,),
        in_specs=[pl.BlockSpec(memory_space=pltpu.SMEM), rows(Q_W), rows(KV_W), rows(KV_W), bufs, bufs],
        out_specs=[pl.BlockSpec((rb, Q_W), lambda i: (i, 0)), bufs, bufs],
        out_shape=[jax.ShapeDtypeStruct((n_seq * t_s, Q_W), F32),
                   jax.ShapeDtypeStruct(buf_k.shape, F32), jax.ShapeDtypeStruct(buf_v.shape, F32)],
        compiler_params=pltpu.CompilerParams(dimension_semantics=("parallel",)),
        name="swa_sample",
    )(sink, q, k, v, buf_k, buf_v)


def _gla_tables(chunk):
    t = np.arange(chunk)[:, None]
    u = np.arange(chunk)[None, :]
    masks = []
    w = chunk // 2
    while w >= 1:
        masks.append((t // (2 * w) == u // (2 * w)) & ((t // w) % 2 == 1) & ((u // w) % 2 == 0))
        w //= 2
    return (u <= t).astype(np.float32), np.stack(masks, 0).astype(np.float32)


def _level_exponents(b, la, w):
    C = b.shape[0]
    row = lax.broadcasted_iota(jnp.int32, b.shape, 0)
    if w >= 4:
        pieces = [jnp.broadcast_to(b[p + w - 1:p + w], (2 * w, b.shape[1])) for p in range(0, C, 2 * w)]
        ref = pieces[0] if len(pieces) == 1 else jnp.concatenate(pieces, 0)
        return jnp.where((row & w) != 0, b - ref, ref - b)
    if w == 2:
        m = row & 3
        nxt = pltpu.roll(la, C - 1, 0)
        prv = pltpu.roll(la, 1, 0)
        return jnp.where(m == 2, la, jnp.where(m == 3, la + prv, jnp.where(m == 0, nxt, 0.0)))
    return jnp.where((row & 1) != 0, la, 0.0)


def _split3(x):
    hi = x.astype(BF16)
    r1 = x - hi.astype(F32)
    mid = r1.astype(BF16)
    lo = (r1 - mid.astype(F32)).astype(BF16)
    return hi, mid, lo


def _gla_intra_kernel(g_ref, m_ref, q_ref, k_ref, v_ref, la_ref, o_ref, qe_ref, ke_ref, vt_ref, d_ref):
    C = GLA_CHUNK
    n_lvl = m_ref.shape[0]
    G = g_ref[...]
    eye = (lax.broadcasted_iota(jnp.int32, (C, C), 0) == lax.broadcasted_iota(jnp.int32, (C, C), 1)).astype(F32)
    for j in range(q_ref.shape[0] // C):
        rs = slice(j * C, (j + 1) * C)
        la = la_ref[rs, :]
        hi, mid, lo = _split3(la)
        b = _dot(G, hi) + _dot(G, mid) + _dot(G, lo)
        b_last = b[C - 1:C]
        q_all = q_ref[rs, :]
        k_all = k_ref[rs, :]
        qe_ref[0, rs, :] = (q_all * jnp.exp(b)).astype(BF16)
        ke_ref[0, rs, :] = (k_all * jnp.exp(b_last - b)).astype(BF16)
        d_ref[0, j] = jnp.broadcast_to(jnp.exp(b_last), (8, GLA_DK))
        q_lvl, k_lvl = [], []
        for l in range(n_lvl):
            El = jnp.exp(_level_exponents(b, la, C >> (l + 1)))
            q_lvl.append((q_all * El).astype(BF16))
            k_lvl.append((k_all * El).astype(BF16))
        for h in range(GLA_HEADS):
            ks = slice(h * GLA_HK, (h + 1) * GLA_HK)
            vs = slice(h * GLA_HV, (h + 1) * GLA_HV)
            v = v_ref[rs, vs]
            att = eye * jnp.sum(q_all[:, ks] * k_all[:, ks], -1, keepdims=True)
            for l in range(n_lvl):
                att = att + m_ref[l] * _dot_nt(q_lvl[l][:, ks], k_lvl[l][:, ks])
            o_ref[0, rs, vs] = _dot(att.astype(BF16), v.astype(BF16))
            vt_ref[0, j, vs, :] = v.T.astype(BF16)


def _gla_inter_kernel(o_ref, qe_ref, ke_ref, vt_ref, d_ref, og_ref, s_ref, st_ref):
    c = pl.program_id(0)
    batch = o_ref.shape[0]

    @pl.when(c == 0)
    def _():
        st_ref[...] = jnp.zeros_like(st_ref)

    for b in range(batch):
        for h in range(GLA_HEADS):
            ks = slice(h * GLA_HK, (h + 1) * GLA_HK)
            vs = slice(h * GLA_HV, (h + 1) * GLA_HV)
            st = st_ref[b * GLA_HEADS + h]
            og_ref[b, :, vs] = o_ref[b, :, vs] + _dot_nt(qe_ref[b, :, ks], st.astype(BF16))
            st_ref[b * GLA_HEADS + h] = st * d_ref[b, 0, 0:1, ks] + _dot(vt_ref[b, 0, vs, :], ke_ref[b, :, ks])

    @pl.when(c == pl.num_programs(0) - 1)
    def _():
        for b in range(batch):
            for h in range(GLA_HEADS):
                s_ref[b, h] = st_ref[b * GLA_HEADS + h].T


def _gla_prompt(gq, gk, gv, la, batch, lp):
    C = GLA_CHUNK
    nc = lp // C
    G, M = _gla_tables(C)
    G = jnp.asarray(G, BF16)
    M = jnp.asarray(M, F32)
    cps = GLA_INTRA_CHUNKS
    assert nc % cps == 0
    rows = lambda w: pl.BlockSpec((cps * C, w), lambda b, c: (b * (nc // cps) + c, 0))
    rows3 = lambda w: pl.BlockSpec((1, cps * C, w), lambda b, c: (b, c, 0))
    full = lambda a: pl.BlockSpec(a.shape, lambda b, c: (0,) * a.ndim)
    o_intra, qe, ke, vt, d = pl.pallas_call(
        _gla_intra_kernel,
        grid=(batch, nc // cps),
        in_specs=[full(G), full(M), rows(GLA_DK), rows(GLA_DK), rows(GLA_DV), rows(GLA_DK)],
        out_specs=[rows3(GLA_DV), rows3(GLA_DK), rows3(GLA_DK),
                   pl.BlockSpec((1, cps, GLA_DV, C), lambda b, c: (b, c, 0, 0)),
                   pl.BlockSpec((1, cps, 8, GLA_DK), lambda b, c: (b, c, 0, 0))],
        out_shape=[jax.ShapeDtypeStruct((batch, lp, GLA_DV), F32),
                   jax.ShapeDtypeStruct((batch, lp, GLA_DK), BF16), jax.ShapeDtypeStruct((batch, lp, GLA_DK), BF16),
                   jax.ShapeDtypeStruct((batch, nc, GLA_DV, C), BF16),
                   jax.ShapeDtypeStruct((batch, nc, 8, GLA_DK), F32)],
        compiler_params=pltpu.CompilerParams(dimension_semantics=("parallel", "parallel")),
        name="gla_intra",
    )(G, M, gq, gk, gv, la)
    chunk = lambda w: pl.BlockSpec((batch, C, w), lambda c: (0, c, 0))
    og, s_fin = pl.pallas_call(
        _gla_inter_kernel,
        grid=(nc,),
        in_specs=[chunk(GLA_DV), chunk(GLA_DK), chunk(GLA_DK),
                  pl.BlockSpec((batch, 1, GLA_DV, C), lambda c: (0, c, 0, 0)),
                  pl.BlockSpec((batch, 1, 8, GLA_DK), lambda c: (0, c, 0, 0))],
        out_specs=[chunk(GLA_DV), pl.BlockSpec((batch, GLA_HEADS, GLA_HK, GLA_HV), lambda c: (0, 0, 0, 0))],
        out_shape=[jax.ShapeDtypeStruct((batch, lp, GLA_DV), F32),
                   jax.ShapeDtypeStruct((batch, GLA_HEADS, GLA_HK, GLA_HV), F32)],
        scratch_shapes=[pltpu.VMEM((batch * GLA_HEADS, GLA_HV, GLA_HK), F32)],
        compiler_params=pltpu.CompilerParams(dimension_semantics=("arbitrary",)),
        name="gla_inter",
    )(o_intra, qe, ke, vt, d)
    return og.reshape(batch * lp, GLA_DV), s_fin


def _gla_sample_kernel(q_ref, k_ref, v_ref, la_ref, s0_ref, o_ref, s_ref, *, t_s):
    T = t_s
    row = lax.broadcasted_iota(jnp.int32, (T, GLA_HK), 0)
    k_fill = jnp.zeros((GLA_HK - T - 8, GLA_HK), F32)
    v_fill = jnp.zeros((GLA_HK - T, GLA_HV), F32)

    def one_seq(s, carry):
        rs = pl.ds(pl.multiple_of(s * T, T), T)
        for h in range(GLA_HEADS):
            ks = slice(h * GLA_HK, (h + 1) * GLA_HK)
            vs = slice(h * GLA_HV, (h + 1) * GLA_HV)
            q = q_ref[rs, ks]
            k = k_ref[rs, ks]
            v = v_ref[rs, vs]
            b = la_ref[rs, ks]
            sh = 1
            while sh < T:
                b = b + jnp.where(row >= sh, pltpu.roll(b, sh, 0), 0.0)
                sh *= 2
            S = s0_ref[s, h]
            o = _dot((q * jnp.exp(b)).astype(BF16), S.astype(BF16))
            for j in range(T):
                e = jnp.exp(jnp.where(row >= j, b - b[j:j + 1], NEG))
                a_col = jnp.sum(q * k[j:j + 1] * e, -1, keepdims=True)
                o = o + a_col * v[j:j + 1]
            o_ref[rs, vs] = o
            b_last = b[T - 1:T]
            ke = k * jnp.exp(b_last - b)
            kt = jnp.concatenate([ke, jnp.broadcast_to(jnp.exp(b_last), (8, GLA_HK)), k_fill], 0).T
            v_pad = jnp.concatenate([v, v_fill], 0)
            s_ref[s, h] = S * kt[:, T:T + 1] + _dot(kt.astype(BF16), v_pad.astype(BF16))
        return carry

    lax.fori_loop(0, SAMPLE_SEQS, one_seq, 0)


def _gla_sample(gq, gk, gv, la, s0, row0, t_s):
    n_seq = s0.shape[0]
    sb = SAMPLE_SEQS
    rb = sb * t_s
    b0 = row0 // rb
    rows = lambda w: pl.BlockSpec((rb, w), lambda i: (b0 + i, 0))
    st = pl.BlockSpec((sb, GLA_HEADS, GLA_HK, GLA_HV), lambda i: (i, 0, 0, 0))
    return pl.pallas_call(
        functools.partial(_gla_sample_kernel, t_s=t_s),
        grid=(n_seq // sb,),
        in_specs=[rows(GLA_DK), rows(GLA_DK), rows(GLA_DV), rows(GLA_DK), st],
        out_specs=[pl.BlockSpec((rb, GLA_DV), lambda i: (i, 0)), st],
        out_shape=[jax.ShapeDtypeStruct((n_seq * t_s, GLA_DV), F32), jax.ShapeDtypeStruct(s0.shape, F32)],
        compiler_params=pltpu.CompilerParams(dimension_semantics=("parallel",), vmem_limit_bytes=VMEM_LIMIT),
        name="gla_sample",
    )(gq, gk, gv, la, s0)


def _route(lt, valid):
    tm = lt.shape[1]
    el = lt[0:N_EXPERTS]
    gl = lt[N_EXPERTS:N_EXPERTS + N_GROUPS]
    g_max = jnp.max(gl, 0, keepdims=True)
    g_row = lax.broadcasted_iota(jnp.int32, (N_GROUPS, tm), 0)
    g_idx = jnp.min(jnp.where(gl == g_max, g_row, N_GROUPS), 0, keepdims=True)
    p_max = 1.0 / jnp.sum(jnp.exp(gl - g_max), 0, keepdims=True)
    e_row = lax.broadcasted_iota(jnp.int32, (N_EXPERTS, tm), 0)
    m1 = jnp.where(e_row // EXP_PER_GROUP == g_idx, el, -jnp.inf)
    v1 = jnp.max(m1, 0, keepdims=True)
    i1 = jnp.min(jnp.where(m1 == v1, e_row, N_EXPERTS), 0, keepdims=True)
    m2 = jnp.where(e_row == i1, -jnp.inf, m1)
    v2 = jnp.max(m2, 0, keepdims=True)
    i2 = jnp.min(jnp.where(m2 == v2, e_row, N_EXPERTS), 0, keepdims=True)
    e2 = jnp.exp(v2 - v1)
    w1 = p_max / (1.0 + e2)
    w2 = p_max * e2 / (1.0 + e2)
    o_row = lax.broadcasted_iota(jnp.int32, (8, tm), 0)
    ids = jnp.where(o_row == 0, i1, jnp.where(o_row == 1, i2, -1))
    return jnp.where(valid, ids, -1), jnp.where(o_row == 0, w1, jnp.where(o_row == 1, w2, 0.0))


def _store_row_tiles(ref, x):
    t, d = x.shape
    n = d // LANES
    for s in range(n):
        ref[pl.ds(s, t, stride=n), :] = x[:, s * LANES:(s + 1) * LANES]


def _load_row_tiles(ref, t, n=ROW_CHUNKS):
    return jnp.concatenate([ref[pl.ds(s, t, stride=n), :] for s in range(n)], axis=1)


def _merge_kernel(x_ref, yap_ref, yas_ref, ogp_ref, ogs_ref, gr_ref, gate_ref, valid_ref, eg_ref, eb_ref, ng_ref,
                  bg_ref, wa_ref, wg_ref, wo_ref, g1_ref, b1_ref, wrh_ref, wrl_ref, br_ref, u_ref,
                  h1t_ref, ids_ref, wts_ref, cnt_ref, run_ref, *, prompt_tiles):
    @pl.when(pl.program_id(0) == 0)
    def _():
        run_ref[...] = jnp.zeros_like(run_ref)

    h = _ln(x_ref[...], eg_ref[...], eb_ref[...])
    is_prompt = pl.program_id(0) < prompt_tiles
    og = jnp.where(is_prompt, ogp_ref[...], ogs_ref[...])
    ya = jnp.where(is_prompt, yap_ref[...], yas_ref[...])
    parts = []
    for hh in range(GLA_HEADS):
        o = og[:, hh * GLA_HV:(hh + 1) * GLA_HV]
        parts.append(o * lax.rsqrt(jnp.mean(o * o, -1, keepdims=True) + EPS))
    gr = gr_ref[...]
    y_gla = jnp.concatenate(parts, 1) * ng_ref[...] * (gr * _sigmoid(gr))
    a = _dot(ya.astype(BF16), wa_ref[...])
    b = _dot(y_gla.astype(BF16), wg_ref[...])
    gate = _sigmoid(gate_ref[...] + bg_ref[...])
    hm = gate[:, :D_MODEL] * a + gate[:, D_MODEL:] * b
    mix = _dot(hm.astype(BF16), wo_ref[...])
    h1 = _ln(DN_ALPHA * h + mix, g1_ref[...], b1_ref[...])
    _store_row_tiles(h1t_ref, h1)
    h_hi = h1.astype(BF16)
    h_lo = (h1 - h_hi.astype(F32)).astype(BF16)
    logits = _dot(h_hi, wrh_ref[...]) + _dot(h_lo, wrh_ref[...]) + _dot(h_hi, wrl_ref[...]) + br_ref[...]
    ids, wts_ref[...] = _route(logits.T, valid_ref[...] > 0.0)
    tm = ids.shape[1]
    e_row = lax.broadcasted_iota(jnp.int32, (N_EXPERTS, tm), 0)
    run = run_ref[:, 0:1]
    ranks = []
    for kk in range(2):
        onehot = (e_row == ids[kk:kk + 1]).astype(F32)
        before = _dot(onehot.astype(BF16), u_ref[...])
        ranks.append(jnp.sum(onehot * (run + before), 0, keepdims=True).astype(jnp.int32))
        run = run + jnp.sum(onehot, 1, keepdims=True)
    run_ref[...] = jnp.broadcast_to(run, run_ref.shape)
    o_row = lax.broadcasted_iota(jnp.int32, (8, tm), 0)
    ids_ref[...] = jnp.where(o_row == 2, ranks[0], jnp.where(o_row == 3, ranks[1], ids))
    cnt_ref[...] = run_ref[...].astype(jnp.int32)


def _merge(x_all, ya_p, ya_s, og_p, og_s, gr, gate, valid, eg, eb, ng, bg, wa, wg, wo, g1, b1, wrh, wrl, br):
    n = x_all.shape[0]
    tm = ROW_TILE
    u = jnp.asarray(np.triu(np.ones((tm, tm), np.float32), 1), BF16)
    pt = ya_p.shape[0] // tm
    st = ya_s.shape[0] // tm
    row = lambda w: pl.BlockSpec((tm, w), lambda i: (i, 0))
    row_p = lambda w: pl.BlockSpec((tm, w), lambda i: (jnp.minimum(i, pt - 1), 0))
    row_s = lambda w: pl.BlockSpec((tm, w), lambda i: (jnp.clip(i - pt, 0, st - 1), 0))
    lane = lambda r: pl.BlockSpec((r, tm), lambda i: (0, i))
    full = lambda a: pl.BlockSpec(a.shape, lambda i: (0,) * a.ndim)
    return pl.pallas_call(
        functools.partial(_merge_kernel, prompt_tiles=pt),
        grid=(n // tm,),
        in_specs=[row(D_MODEL), row_p(Q_W), row_s(Q_W), row_p(GLA_DV), row_s(GLA_DV), row(GLA_DV), row(2 * D_MODEL),
                  lane(1), full(eg), full(eb), full(ng), full(bg), full(wa), full(wg), full(wo), full(g1), full(b1),
                  full(wrh), full(wrl), full(br), full(u)],
        out_specs=[pl.BlockSpec((tm * ROW_CHUNKS, LANES), lambda i: (i, 0)), lane(8), lane(8),
                   pl.BlockSpec((N_EXPERTS, LANES), lambda i: (0, 0))],
        out_shape=[jax.ShapeDtypeStruct((n * ROW_CHUNKS, LANES), F32),
                   jax.ShapeDtypeStruct((8, n), jnp.int32), jax.ShapeDtypeStruct((8, n), F32),
                   jax.ShapeDtypeStruct((N_EXPERTS, LANES), jnp.int32)],
        scratch_shapes=[pltpu.VMEM((N_EXPERTS, LANES), F32)],
        compiler_params=pltpu.CompilerParams(dimension_semantics=("arbitrary",), vmem_limit_bytes=VMEM_LIMIT),
        name="merge",
    )(x_all, ya_p, ya_s, og_p, og_s, gr, gate, valid, eg, eb, ng, bg, wa, wg, wo, g1, b1, wrh, wrl, br, u)


def _gather_row_tiles(idx_ref, idx0, src_hbm, dst, sem, n):
    def body(r, carry):
        t = idx_ref[idx0 + r]
        pltpu.make_async_copy(src_hbm.at[pl.ds(pl.multiple_of(t * ROW_CHUNKS, ROW_CHUNKS), ROW_CHUNKS), :],
                              dst.at[pl.ds(pl.multiple_of(r * ROW_CHUNKS, ROW_CHUNKS), ROW_CHUNKS), :], sem).start()
        return carry
    lax.fori_loop(0, n, body, 0, unroll=8)


def _wait_row_tiles(src_hbm, dst, sem, n):
    pltpu.make_async_copy(src_hbm.at[pl.ds(0, n * ROW_CHUNKS), :], dst, sem).wait()


def _tiles(ref, first, n=1):
    return ref.at[pl.ds(pl.multiple_of(first * ROW_CHUNKS, ROW_CHUNKS), n * ROW_CHUNKS), :]


def _dispatch_kernel(dest_ref, start_ref, cnt_ref, nt_ref, h_hbm, xs_hbm, buf, zbuf, sem_in, sem_out, sem_z,
                     *, ranges, n_rows, max_tiles):
    CH = DISPATCH_CHUNK
    T = FFN_TILE
    c = pl.program_id(0)
    n_chunks = pl.num_programs(0)

    def load(row0, slot):
        return pltpu.make_async_copy(_tiles(h_hbm, row0, CH), buf.at[slot], sem_in.at[slot])

    def scatter(row0, slot):
        def body(r, carry):
            for kk in range(2):
                d = dest_ref[kk * n_rows + row0 + r]
                pltpu.make_async_copy(_tiles(buf.at[slot], r), _tiles(xs_hbm, d), sem_out.at[slot]).start()
            return carry
        lax.fori_loop(0, CH, body, 0, unroll=8)

    def drain(slot):
        for _ in range(2):
            pltpu.make_async_copy(buf.at[slot], _tiles(xs_hbm, 0, CH), sem_out.at[slot]).wait()

    groups, per_group, first, stride, extra, extra_first = ranges

    def row0(j):
        in_group = first + (j // per_group) * stride + (j % per_group) * CH
        return jnp.where(j < groups * per_group, in_group, extra_first + (j - groups * per_group) * CH)

    @pl.when(c == 0)
    def _():
        for j in range(2):
            load(row0(j), j).start()

    slot = c % 3
    load(row0(c), slot).wait()
    scatter(row0(c), slot)

    @pl.when(c > 0)
    def _():
        drain((c + 2) % 3)

    @pl.when(c + 2 < n_chunks)
    def _():
        load(row0(c + 2), (c + 2) % 3).start()

    @pl.when(c == n_chunks - 1)
    def _():
        drain(slot)
        _zero_unowned_slots(start_ref, cnt_ref, nt_ref, xs_hbm, zbuf, sem_z, max_tiles)


def _zero_unowned_slots(start_ref, cnt_ref, nt_ref, xs_hbm, zbuf, sem_z, max_tiles):
    T = FFN_TILE
    zbuf[...] = jnp.zeros_like(zbuf)

    def tail_copies(e, wait):
        cnt = cnt_ref[e]
        n = (T - (cnt & (T - 1))) & (T - 1)
        first = start_ref[e] + cnt
        for bit in reversed(range(T.bit_length() - 1)):
            size = 1 << bit

            @pl.when((n & size) != 0)
            def _():
                cp = pltpu.make_async_copy(_tiles(zbuf, 0, size),
                                           _tiles(xs_hbm, first + ((n >> (bit + 1)) << (bit + 1)), size), sem_z)
                cp.wait() if wait else cp.start()

    def unused_tile(t, wait):
        cp = pltpu.make_async_copy(zbuf, _tiles(xs_hbm, t * T, T), sem_z)
        cp.wait() if wait else cp.start()

    for wait in (False, True):
        def per_expert(e, carry, wait=wait):
            tail_copies(e, wait)
            return carry

        def per_tile(t, carry, wait=wait):
            unused_tile(t, wait)
            return carry
        lax.fori_loop(0, N_EXPERTS, per_expert, 0)
        lax.fori_loop(nt_ref[0], max_tiles, per_tile, 0)


def _dispatch(dest, start, counts, n_tiles, h1t, ranges, max_tiles):
    T = FFN_TILE
    n_rows = h1t.shape[0] // ROW_CHUNKS
    return pl.pallas_call(
        functools.partial(_dispatch_kernel, ranges=ranges, n_rows=n_rows, max_tiles=max_tiles),
        grid_spec=pltpu.PrefetchScalarGridSpec(
            num_scalar_prefetch=4,
            grid=(ranges[0] * ranges[1] + ranges[4],),
            in_specs=[pl.BlockSpec(memory_space=pl.ANY)],
            out_specs=pl.BlockSpec(memory_space=pl.ANY),
            scratch_shapes=[pltpu.VMEM((3, DISPATCH_CHUNK * ROW_CHUNKS, LANES), F32),
                            pltpu.VMEM((T * ROW_CHUNKS, LANES), F32),
                            pltpu.SemaphoreType.DMA((3,)), pltpu.SemaphoreType.DMA((3,)), pltpu.SemaphoreType.DMA]),
        out_shape=jax.ShapeDtypeStruct((max_tiles * T * ROW_CHUNKS, LANES), F32),
        compiler_params=pltpu.CompilerParams(dimension_semantics=("arbitrary",)),
        name="dispatch",
    )(dest, start, counts, n_tiles, h1t)


def _ffn_kernel(te_ref, nxt_ref, nt_ref, x_ref, wg_hbm, wu_hbm, wd_hbm, out_ref, stage_g, stage_u, stage_d, sem,
                wgb, wub, wdb):
    i = pl.program_id(0)
    nt = nt_ref[0]
    T = FFN_TILE

    def stage(e):
        return [pltpu.make_async_copy(src.at[e], dst, sem.at[n])
                for n, (src, dst) in enumerate(((wg_hbm, stage_g), (wu_hbm, stage_u), (wd_hbm, stage_d)))]

    @pl.when(i == 0)
    def _():
        for cp in stage(te_ref[0]):
            cp.start()

    @pl.when(i < nt)
    def _():
        @pl.when((i == 0) | (te_ref[i] != te_ref[jnp.maximum(i - 1, 0)]))
        def _():
            for cp in stage(te_ref[i]):
                cp.wait()
            wgb[...] = stage_g[...].astype(BF16)
            wub[...] = stage_u[...].astype(BF16)
            wdb[...] = stage_d[...].astype(BF16)

            @pl.when(nxt_ref[i] >= 0)
            def _():
                for cp in stage(nxt_ref[i]):
                    cp.start()

        x = _load_row_tiles(x_ref, T).astype(BF16)
        g = _dot(x, wgb[...])
        u = _dot(x, wub[...])
        _store_row_tiles(out_ref, _dot((g * _sigmoid(g) * u).astype(BF16), wdb[...]))

    @pl.when(i >= nt)
    def _():
        out_ref[...] = jnp.zeros_like(out_ref)


def _ffn(tile_expert, next_expert, n_tiles, xs, w_g, w_u, w_d):
    T = FFN_TILE
    max_tiles = tile_expert.shape[0]
    hbm = pl.BlockSpec(memory_space=pl.ANY)
    tile = lambda imap: pl.BlockSpec((T * ROW_CHUNKS, LANES), imap)
    return pl.pallas_call(
        _ffn_kernel,
        grid_spec=pltpu.PrefetchScalarGridSpec(
            num_scalar_prefetch=3,
            grid=(max_tiles,),
            in_specs=[tile(lambda i, te, nxt, nt: (jnp.minimum(i, nt[0] - 1), 0)), hbm, hbm, hbm],
            out_specs=tile(lambda i, te, nxt, nt: (i, 0)),
            scratch_shapes=[pltpu.VMEM(w_g.shape[1:], F32), pltpu.VMEM(w_u.shape[1:], F32),
                            pltpu.VMEM(w_d.shape[1:], F32), pltpu.SemaphoreType.DMA((3,)),
                            pltpu.VMEM(w_g.shape[1:], BF16), pltpu.VMEM(w_u.shape[1:], BF16),
                            pltpu.VMEM(w_d.shape[1:], BF16)]),
        out_shape=jax.ShapeDtypeStruct(xs.shape, F32),
        compiler_params=pltpu.CompilerParams(dimension_semantics=("arbitrary",)),
        name="ffn",
    )(tile_expert, next_expert, n_tiles, xs, w_g, w_u, w_d)


def _combine_kernel(dest_ref, h_ref, w_ref, g_ref, b_ref, ys_hbm, y_ref, buf, sem, *, blk, n_rows):
    i = pl.program_id(0)
    n = pl.num_programs(0)
    T = CMB_TILE

    def gather(step, slot):
        for kk in range(2):
            _gather_row_tiles(dest_ref, kk * n_rows + blk(step) * T, ys_hbm, buf.at[slot, kk], sem.at[slot], T)

    @pl.when(i == 0)
    def _():
        gather(0, 0)

    @pl.when(i + 1 < n)
    def _():
        gather(i + 1, (i + 1) % 2)

    slot = i % 2
    for kk in range(2):
        _wait_row_tiles(ys_hbm, buf.at[slot, kk], sem.at[slot], T)
    w = w_ref[...]
    ff = w[:, 0:1] * _load_row_tiles(buf.at[slot, 0], T) + w[:, 1:2] * _load_row_tiles(buf.at[slot, 1], T)
    y_ref[...] = _ln(DN_ALPHA * _load_row_tiles(h_ref, T) + ff, g_ref[...], b_ref[...])


def _combine(dest, h1t, wts_t, ys, g2, b2, n_out, first_block, blocks_per_batch, skip_blocks):
    T = CMB_TILE
    n_rows = h1t.shape[0] // ROW_CHUNKS
    if skip_blocks:
        blk = lambda i: first_block + i + (i // blocks_per_batch + 1) * skip_blocks
    else:
        blk = lambda i: first_block + i
    full = lambda a: pl.BlockSpec(a.shape, lambda i, d: (0,) * a.ndim)
    return pl.pallas_call(
        functools.partial(_combine_kernel, blk=blk, n_rows=n_rows),
        grid_spec=pltpu.PrefetchScalarGridSpec(
            num_scalar_prefetch=1,
            grid=(n_out // T,),
            in_specs=[pl.BlockSpec((T * ROW_CHUNKS, LANES), lambda i, d: (blk(i), 0)),
                      pl.BlockSpec((T, 2), lambda i, d: (blk(i), 0)),
                      full(g2), full(b2), pl.BlockSpec(memory_space=pl.ANY)],
            out_specs=pl.BlockSpec((T, D_MODEL), lambda i, d: (i, 0)),
            scratch_shapes=[pltpu.VMEM((2, 2, T * ROW_CHUNKS, LANES), F32), pltpu.SemaphoreType.DMA((2,))]),
        out_shape=jax.ShapeDtypeStruct((n_out, D_MODEL), F32),
        compiler_params=pltpu.CompilerParams(dimension_semantics=("arbitrary",)),
        name="combine",
    )(dest, h1t, wts_t, g2, b2, ys)


def _dispatch_plan(routing, counts, max_tiles):
    T = FFN_TILE
    tiles_e = (counts + T - 1) // T
    tile_end = jnp.cumsum(tiles_e)
    n_tiles = tile_end[-1]
    start = (tile_end - tiles_e) * T
    ids, rank = routing[0:2], routing[2:4]
    onehot = (ids[..., None] == jnp.arange(N_EXPERTS, dtype=jnp.int32)).astype(jnp.int32)
    dest = (jnp.sum(onehot * start, axis=-1) + jnp.where(ids >= 0, rank, 0)).reshape(-1).astype(jnp.int32)
    te = jnp.sum((jnp.arange(max_tiles, dtype=jnp.int32)[:, None] >= tile_end[None, :]).astype(jnp.int32), axis=1)
    te_last = jnp.take(te, jnp.maximum(n_tiles - 1, 0))
    te = jnp.where(jnp.arange(max_tiles) < n_tiles, te, te_last).astype(jnp.int32)
    used = jnp.where(counts > 0, jnp.arange(N_EXPERTS, dtype=jnp.int32), N_EXPERTS)
    later = jnp.concatenate([jnp.flip(lax.cummin(jnp.flip(used)))[1:], jnp.full((1,), N_EXPERTS, jnp.int32)])
    nxt = jnp.take(later, te)
    nxt = jnp.where(nxt < N_EXPERTS, nxt, -1).astype(jnp.int32)
    return te, nxt, n_tiles.reshape(1).astype(jnp.int32), start.astype(jnp.int32), dest


def kernel(x_prompt, x_sample, state_swa_k, state_swa_v, state_gla, meta_tokens, ln_emb_g, ln_emb_b, w_in, b_gate, attn_sink, w_alpha2, b_alpha, gla_norm_g, w_attn_br, w_gla_br, w_out, ln1_g, ln1_b, w_router_group, b_router_group, w_router_expert, b_router_expert, w_exp_gate, w_exp_up, w_exp_down, ln2_g, ln2_b):
    B, seq, _ = x_prompt.shape
    n_seq, t_s, _ = x_sample.shape
    depth = w_in.shape[0]
    assert depth == 1 and seq % ATT_BLOCK == 0 and t_s == 8
    lp = SKIP_ROWS + seq
    NP, NS = B * lp, n_seq * t_s
    NR = NP + NS
    assert NP % ROW_TILE == 0 and NS % ROW_TILE == 0 and n_seq % SAMPLE_SEQS == 0
    assert seq % DISPATCH_CHUNK == 0 and NS % DISPATCH_CHUNK == 0
    l = 0
    row2 = lambda a: a.reshape(1, -1)

    head = jnp.concatenate([jnp.zeros((FRONT_PAD, D_MODEL), F32), meta_tokens], axis=0)
    x_all = jnp.concatenate([p for b in range(B) for p in (head, x_prompt[b])] + [x_sample.reshape(NS, D_MODEL)],
                            axis=0)
    pos = np.arange(NR)
    in_prompt = pos < NP
    keep = jnp.asarray(~(in_prompt & (pos % lp < FRONT_PAD)), F32).reshape(NR, 1)
    moe_valid = jnp.asarray(~(in_prompt & (pos % lp < SKIP_ROWS)), F32).reshape(1, NR)

    wi = w_in[l]
    c_ga = sum((Q_W, KV_W, KV_W, GLA_DK, GLA_DK, GLA_DV))
    w_bf = jnp.concatenate([wi[:, :c_ga], wi[:, c_ga + GLA_RANK:], wi[:, c_ga:c_ga + GLA_RANK],
                            jnp.zeros((D_MODEL, 128 - GLA_RANK), F32)], axis=1).astype(BF16)
    wa2_bf = jnp.concatenate([w_alpha2[l], jnp.zeros((128 - GLA_RANK, GLA_DK), F32)], axis=0).astype(BF16)

    q, k, v, gq, gk, gv, la, gr, gate = _inproj(x_all, keep, row2(ln_emb_g), row2(ln_emb_b), w_bf, wa2_bf,
                                                row2(b_alpha[l]))

    sink = attn_sink[l]
    ya_p = _swa_prompt(sink, q, k, v, B, lp)
    buf_k = state_swa_k[l].reshape(n_seq, WINDOW, KV_W)
    buf_v = state_swa_v[l].reshape(n_seq, WINDOW, KV_W)
    ya_s, nk_s, nv_s = _swa_sample(sink, q, k, v, buf_k, buf_v, NP, t_s)

    og_p, s_p = _gla_prompt(gq, gk, gv, la, B, lp)
    og_s, s_s = _gla_sample(gq, gk, gv, la, state_gla[l], NP, t_s)

    wr = jnp.concatenate([w_router_expert[l], w_router_group[l],
                          jnp.zeros((D_MODEL, LANES - N_EXPERTS - N_GROUPS), F32)], axis=1)
    br = jnp.concatenate([b_router_expert[l], b_router_group[l],
                          jnp.zeros((LANES - N_EXPERTS - N_GROUPS,), F32)]).reshape(1, LANES)
    wr_hi = wr.astype(BF16)
    h1t, routing, wts, counts = _merge(x_all, ya_p, ya_s, og_p, og_s, gr, gate, moe_valid, row2(ln_emb_g),
                                       row2(ln_emb_b), row2(gla_norm_g[l]), row2(b_gate[l]),
                                       w_attn_br[l].astype(BF16), w_gla_br[l].astype(BF16), w_out[l].astype(BF16),
                                       row2(ln1_g[l]), row2(ln1_b[l]), wr_hi, (wr - wr_hi.astype(F32)).astype(BF16), br)

    n_tok = B * seq + NS
    max_tiles = (2 * n_tok) // FFN_TILE + N_EXPERTS
    counts = counts[:, 0]
    te, nxt, n_tiles, start, dest = _dispatch_plan(routing[0:4], counts, max_tiles)
    routed = (B, seq // DISPATCH_CHUNK, SKIP_ROWS, lp, NS // DISPATCH_CHUNK, NP)
    xs = _dispatch(dest, start, counts, n_tiles, h1t, routed, max_tiles)
    ys = _ffn(te, nxt, n_tiles, xs, w_exp_gate[l], w_exp_up[l], w_exp_down[l])

    wts_t = wts[0:2].T
    g2, b2 = row2(ln2_g[l]), row2(ln2_b[l])
    skip_blocks = SKIP_ROWS // CMB_TILE
    y_p = _combine(dest, h1t, wts_t, ys, g2, b2, B * seq, 0, seq // CMB_TILE, skip_blocks)
    y_s = _combine(dest, h1t, wts_t, ys, g2, b2, NS, NP // CMB_TILE, 1, 0)

    kv_shape = (B, lp, N_KV, HEAD_DIM)
    k_p = k[:NP].reshape(kv_shape)[:, -WINDOW:]
    v_p = v[:NP].reshape(kv_shape)[:, -WINDOW:]
    return (y_p.reshape(B, seq, D_MODEL), y_s.reshape(n_seq, t_s, D_MODEL),
            k_p[None], v_p[None], s_p[None],
            nk_s.reshape(1, n_seq, WINDOW, N_KV, HEAD_DIM), nv_s.reshape(1, n_seq, WINDOW, N_KV, HEAD_DIM),
            s_s[None])
```

```python
import functools

import numpy as np
import jax
import jax.numpy as jnp
from jax import lax
from jax.experimental import pallas as pl
from jax.experimental.pallas import tpu as pltpu

F32 = jnp.float32
BF16 = jnp.bfloat16

D_MODEL = 1024
N_META = 16
HEAD_DIM = 64
N_HEADS = 8
N_KV = 2
Q_PER_KV = 4
WINDOW = 128
ATT_BLOCK = 128
GLA_HEADS = 4
GLA_HK = 128
GLA_HV = 256
GLA_DK = GLA_HEADS * GLA_HK
GLA_DV = GLA_HEADS * GLA_HV
GLA_RANK = 16
GLA_TAU = 16.0
GLA_CHUNK = 64
N_GROUPS = 4
EXP_PER_GROUP = 8
N_EXPERTS = 32
D_EXPERT = 256
DN_ALPHA = 2.0 ** 0.25
EPS = 1e-5
NEG = -1e30

FRONT_PAD = (-N_META) % ATT_BLOCK
SKIP_ROWS = FRONT_PAD + N_META

Q_W, KV_W = N_HEADS * HEAD_DIM, N_KV * HEAD_DIM
SEG = {}
_o = 0
for _n, _w in (("q", Q_W), ("k", KV_W), ("v", KV_W), ("gq", GLA_DK), ("gk", GLA_DK), ("gv", GLA_DV),
               ("gr", GLA_DV), ("gate", 2 * D_MODEL), ("ga", 128)):
    SEG[_n] = (_o, _o + _w)
    _o += _w
W_IN_COLS = _o

ROW_TILE = 256
FFN_TILE = 256
CMB_TILE = 128
DISPATCH_CHUNK = 128
GLA_INTRA_CHUNKS = 2
INPROJ_TILE = 512
SAMPLE_SEQS = 8
LANES = 128
ROW_CHUNKS = D_MODEL // LANES
VMEM_LIMIT = 56 * 1024 * 1024


def _ln(x, g, b):
    mu = jnp.mean(x, -1, keepdims=True)
    xc = x - mu
    var = jnp.mean(xc * xc, -1, keepdims=True)
    return xc * lax.rsqrt(var + EPS) * g + b


def _sigmoid(x):
    return 0.5 * jnp.tanh(0.5 * x) + 0.5


def _dot(a, b):
    return jnp.dot(a, b, preferred_element_type=F32)


def _dot_nt(a, b):
    return lax.dot_general(a, b, (((1,), (1,)), ((), ())), preferred_element_type=F32)


def _inproj_kernel(*refs, blocks, batch_blocks, prompt_blocks):
    xp_refs, xs_refs = refs[:blocks], refs[blocks:2 * blocks]
    (head_ref, g_ref, b_ref, w_ref, wa2_ref, ba_ref,
     h_ref, q_ref, k_ref, v_ref, gq_ref, gk_ref, gv_ref, la_ref, gr_ref, gate_ref) = refs[2 * blocks:]
    row = lax.broadcasted_iota(jnp.int32, (ATT_BLOCK, 1), 0)
    xs, keeps = [], []
    for s in range(blocks):
        p = pl.program_id(0) * blocks + s
        is_sample = p >= prompt_blocks
        is_head = jnp.logical_and(jnp.logical_not(is_sample), p % batch_blocks == 0)
        xs.append(jnp.where(is_sample, xs_refs[s][...], jnp.where(is_head, head_ref[...], xp_refs[s][...])))
        keeps.append(jnp.where(jnp.logical_and(is_head, row < FRONT_PAD), 0.0, 1.0))
    h = _ln(jnp.concatenate(xs, 0), g_ref[...], b_ref[...])
    h_ref[...] = h
    hb = h.astype(BF16)
    keep = jnp.concatenate(keeps, 0)

    def seg(name):
        a, b = SEG[name]
        return _dot(hb, w_ref[:, a:b])

    q_ref[...] = seg("q")
    k_ref[...] = seg("k")
    v_ref[...] = seg("v")
    gq_ref[...] = seg("gq") * (GLA_HK ** -0.5)
    gk_ref[...] = seg("gk") * keep
    gv_ref[...] = seg("gv") * keep
    gr_ref[...] = seg("gr")
    gate_ref[...] = seg("gate")
    z = _dot(seg("ga").astype(BF16), wa2_ref[...]) + ba_ref[...]
    la = (jnp.minimum(z, 0.0) - jnp.log(1.0 + jnp.exp(-jnp.abs(z)))) * (1.0 / GLA_TAU)
    la_ref[...] = la * keep


def _inproj(x_prompt, x_sample, head, ln_g, ln_b, w_bf, wa2_bf, b_alpha):
    B, seq, _ = x_prompt.shape
    blk = ATT_BLOCK
    seq_blocks = seq // blk
    batch_blocks = seq_blocks + 1
    prompt_blocks = B * batch_blocks
    sample_blocks = x_sample.shape[0] // blk
    n = (prompt_blocks + sample_blocks) * blk
    tm = INPROJ_TILE
    blocks = tm // blk
    xp = x_prompt.reshape(B * seq, D_MODEL)

    def prompt_block(s):
        def imap(i):
            p = jnp.minimum(i * blocks + s, prompt_blocks - 1)
            return (p // batch_blocks * seq_blocks + jnp.maximum(p % batch_blocks - 1, 0), 0)
        return pl.BlockSpec((blk, D_MODEL), imap)

    def sample_block(s):
        return pl.BlockSpec((blk, D_MODEL), lambda i: (jnp.clip(i * blocks + s - prompt_blocks, 0, sample_blocks - 1), 0))

    widths = [D_MODEL, Q_W, KV_W, KV_W, GLA_DK, GLA_DK, GLA_DV, GLA_DK, GLA_DV, 2 * D_MODEL]
    row = lambda w: pl.BlockSpec((tm, w), lambda i: (i, 0))
    const = lambda a: pl.BlockSpec(a.shape, lambda i: (0,) * a.ndim, pipeline_mode=pl.Buffered(1))
    return pl.pallas_call(
        functools.partial(_inproj_kernel, blocks=blocks, batch_blocks=batch_blocks, prompt_blocks=prompt_blocks),
        grid=(n // tm,),
        in_specs=[prompt_block(s) for s in range(blocks)] + [sample_block(s) for s in range(blocks)]
                 + [const(head), const(ln_g), const(ln_b), const(w_bf), const(wa2_bf), const(b_alpha)],
        out_specs=[row(w) for w in widths],
        out_shape=[jax.ShapeDtypeStruct((n, w), F32) for w in widths],
        compiler_params=pltpu.CompilerParams(dimension_semantics=("parallel",), vmem_limit_bytes=VMEM_LIMIT),
        name="inproj",
    )(*([xp] * blocks), *([x_sample] * blocks), head, ln_g, ln_b, w_bf, wa2_bf, b_alpha)


def _softmax_pv(s, sink, vv):
    m = jnp.maximum(jnp.max(s, -1, keepdims=True), sink)
    p = jnp.exp(s - m)
    l = jnp.sum(p, -1, keepdims=True) + jnp.exp(sink - m)
    return _dot(p.astype(BF16), vv) / l


def _swa_prompt_kernel(sink_ref, bias_ref, q_ref, kp_ref, kc_ref, vp_ref, vc_ref, o_ref, kl_ref, vl_ref):
    q = q_ref[...] * (HEAD_DIM ** -0.5)
    kb = jnp.concatenate([kp_ref[...], kc_ref[...]], 0)
    vb = jnp.concatenate([vp_ref[...], vc_ref[...]], 0)
    low = lax.broadcasted_iota(jnp.int32, (1, 2 * HEAD_DIM), 1) < HEAD_DIM
    k_low = jnp.where(low, kb, 0.0)
    k_high = jnp.where(low, 0.0, kb)
    keys = {(0, 0): k_low.astype(BF16), (0, 1): pltpu.roll(k_low, HEAD_DIM, 1).astype(BF16),
            (1, 0): pltpu.roll(k_high, HEAD_DIM, 1).astype(BF16), (1, 1): k_high.astype(BF16)}
    ones_col = (lax.broadcasted_iota(jnp.int32, (1, 2 * HEAD_DIM), 1) == HEAD_DIM).astype(F32)
    values = [jnp.where(low, vb, ones_col).astype(BF16),
              jnp.where(low, pltpu.roll(vb, HEAD_DIM, 1), ones_col).astype(BF16)]
    for pair in range(N_HEADS // 2):
        qp = q[:, pair * 2 * HEAD_DIM:(pair + 1) * 2 * HEAD_DIM]
        outs = []
        for half in range(2):
            h = 2 * pair + half
            kv = h // Q_PER_KV
            qm = jnp.where(low if half == 0 else jnp.logical_not(low), qp, 0.0).astype(BF16)
            s = _dot_nt(qm, keys[(kv, half)]) + bias_ref[0, h]
            m = jnp.maximum(jnp.max(s, -1, keepdims=True), sink_ref[h])
            pv = _dot(jnp.exp(s - m).astype(BF16), values[kv])
            outs.append(pv / (pv[:, HEAD_DIM:HEAD_DIM + 1] + jnp.exp(sink_ref[h] - m)))
        o_ref[:, pair * 2 * HEAD_DIM:(pair + 1) * 2 * HEAD_DIM] = jnp.where(low, outs[0],
                                                                            pltpu.roll(outs[1], HEAD_DIM, 1))

    @pl.when(pl.program_id(1) == pl.num_programs(1) - 1)
    def _():
        kl_ref[0] = kc_ref[...]
        vl_ref[0] = vc_ref[...]


def _swa_bias_table():
    r = np.arange(ATT_BLOCK)[:, None]
    c = np.arange(2 * ATT_BLOCK)[None, :]
    dist = r - c + ATT_BLOCK
    slopes = 2.0 ** -(np.arange(N_HEADS) + 1.0)
    table = np.empty((3, N_HEADS, ATT_BLOCK, 2 * ATT_BLOCK), np.float32)
    for j in range(3):
        seen = (dist >= 0) & (dist < WINDOW) & ((j - 1) * ATT_BLOCK + c - FRONT_PAD >= 0)
        table[j] = np.where(seen[None], -slopes[:, None, None] * dist[None], NEG)
    return table


def _swa_prompt(sink, q, k, v, batch, lp):
    nb = lp // ATT_BLOCK
    assert nb >= 3
    n = batch * lp
    bias = jnp.asarray(_swa_bias_table())
    cur = lambda w: pl.BlockSpec((ATT_BLOCK, w), lambda b, j: (b * nb + j, 0))
    prev = lambda w: pl.BlockSpec((ATT_BLOCK, w), lambda b, j: (b * nb + jnp.maximum(j - 1, 0), 0))
    last = pl.BlockSpec((1, ATT_BLOCK, KV_W), lambda b, j: (b, 0, 0))
    return pl.pallas_call(
        _swa_prompt_kernel,
        grid=(batch, nb),
        in_specs=[pl.BlockSpec(memory_space=pltpu.SMEM),
                  pl.BlockSpec((1,) + bias.shape[1:], lambda b, j: (jnp.minimum(j, 2), 0, 0, 0)),
                  cur(Q_W), prev(KV_W), cur(KV_W), prev(KV_W), cur(KV_W)],
        out_specs=[cur(Q_W), last, last],
        out_shape=[jax.ShapeDtypeStruct((n, Q_W), F32), jax.ShapeDtypeStruct((batch, ATT_BLOCK, KV_W), F32),
                   jax.ShapeDtypeStruct((batch, ATT_BLOCK, KV_W), F32)],
        compiler_params=pltpu.CompilerParams(dimension_semantics=("parallel", "arbitrary")),
        name="swa_prompt",
    )(sink, bias, q, k, k, v, v)


def _swa_sample_kernel(sink_ref, q_ref, k_ref, v_ref, bk_ref, bv_ref, o_ref, nk_ref, nv_ref, *, t_s):
    nbuf = WINDOW
    span = 2 * WINDOW
    rows = Q_PER_KV * t_s
    r = lax.broadcasted_iota(jnp.int32, (rows, span), 0)
    c = lax.broadcasted_iota(jnp.int32, (rows, span), 1)
    t = r % t_s
    dist = t + nbuf - c
    mask = (dist >= 0) & (dist < WINDOW) & (c < nbuf + t_s)
    distf = dist.astype(F32)
    g_col = lax.broadcasted_iota(jnp.int32, (rows, 1), 0) // t_s
    fill = jnp.zeros((span - nbuf - t_s, KV_W), F32)

    def one_seq(s, carry):
        rs = pl.ds(pl.multiple_of(s * t_s, t_s), t_s)
        q = q_ref[rs, :]
        k_new = k_ref[rs, :]
        v_new = v_ref[rs, :]
        bk = bk_ref[s]
        bv = bv_ref[s]
        k_all = jnp.concatenate([bk, k_new, fill], 0)
        v_all = jnp.concatenate([bv, v_new, fill], 0)
        for kv in range(N_KV):
            qg = jnp.concatenate(
                [q[:, (kv * Q_PER_KV + g) * HEAD_DIM:(kv * Q_PER_KV + g + 1) * HEAD_DIM] for g in range(Q_PER_KV)], 0)
            kk = k_all[:, kv * HEAD_DIM:(kv + 1) * HEAD_DIM].astype(BF16)
            vv = v_all[:, kv * HEAD_DIM:(kv + 1) * HEAD_DIM].astype(BF16)
            slope = jnp.zeros((rows, 1), F32)
            sink = jnp.zeros((rows, 1), F32)
            for g in range(Q_PER_KV):
                h = kv * Q_PER_KV + g
                slope = jnp.where(g_col == g, 2.0 ** -(h + 1), slope)
                sink = jnp.where(g_col == g, sink_ref[h], sink)
            sc = _dot_nt(qg.astype(BF16), kk) * (HEAD_DIM ** -0.5) - slope * distf
            sc = jnp.where(mask, sc, NEG)
            o = _softmax_pv(sc, sink, vv)
            for g in range(Q_PER_KV):
                h = kv * Q_PER_KV + g
                o_ref[rs, h * HEAD_DIM:(h + 1) * HEAD_DIM] = o[g * t_s:(g + 1) * t_s]
        nk_ref[s, 0:nbuf - t_s, :] = bk[t_s:, :]
        nk_ref[s, nbuf - t_s:nbuf, :] = k_new
        nv_ref[s, 0:nbuf - t_s, :] = bv[t_s:, :]
        nv_ref[s, nbuf - t_s:nbuf, :] = v_new
        return carry

    lax.fori_loop(0, SAMPLE_SEQS, one_seq, 0, unroll=4)


def _swa_sample(sink, q, k, v, buf_k, buf_v, row0, t_s):
    n_seq = buf_k.shape[0]
    sb = SAMPLE_SEQS
    rb = sb * t_s
    b0 = row0 // rb
    rows = lambda w: pl.BlockSpec((rb, w), lambda i: (b0 + i, 0))
    bufs = pl.BlockSpec((sb, WINDOW, KV_W), lambda i: (i, 0, 0))
    return pl.pallas_call(
        functools.partial(_swa_sample_kernel, t_s=t_s),
        grid=(n_seq // sb,),
        in_specs=[pl.BlockSpec(memory_space=pltpu.SMEM), rows(Q_W), rows(KV_W), rows(KV_W), bufs, bufs],
        out_specs=[pl.BlockSpec((rb, Q_W), lambda i: (i, 0)), bufs, bufs],
        out_shape=[jax.ShapeDtypeStruct((n_seq * t_s, Q_W), F32),
                   jax.ShapeDtypeStruct(buf_k.shape, F32), jax.ShapeDtypeStruct(buf_v.shape, F32)],
        compiler_params=pltpu.CompilerParams(dimension_semantics=("parallel",)),
        name="swa_sample",
    )(sink, q, k, v, buf_k, buf_v)


def _gla_tables(chunk):
    t = np.arange(chunk)[:, None]
    u = np.arange(chunk)[None, :]
    masks = []
    w = chunk // 2
    while w >= 1:
        masks.append((t // (2 * w) == u // (2 * w)) & ((t // w) % 2 == 1) & ((u // w) % 2 == 0))
        w //= 2
    return (u <= t).astype(np.float32), np.stack(masks, 0).astype(np.float32)


def _level_exponents(b, la, w):
    C = b.shape[0]
    row = lax.broadcasted_iota(jnp.int32, b.shape, 0)
    if w >= 4:
        pieces = [jnp.broadcast_to(b[p + w - 1:p + w], (2 * w, b.shape[1])) for p in range(0, C, 2 * w)]
        ref = pieces[0] if len(pieces) == 1 else jnp.concatenate(pieces, 0)
        return jnp.where((row & w) != 0, b - ref, ref - b)
    if w == 2:
        m = row & 3
        nxt = pltpu.roll(la, C - 1, 0)
        prv = pltpu.roll(la, 1, 0)
        return jnp.where(m == 2, la, jnp.where(m == 3, la + prv, jnp.where(m == 0, nxt, 0.0)))
    return jnp.where((row & 1) != 0, la, 0.0)


def _split3(x):
    hi = x.astype(BF16)
    r1 = x - hi.astype(F32)
    mid = r1.astype(BF16)
    lo = (r1 - mid.astype(F32)).astype(BF16)
    return hi, mid, lo


def _gla_intra_kernel(g_ref, m_ref, q_ref, k_ref, v_ref, la_ref, o_ref, qe_ref, ke_ref, vt_ref, d_ref):
    C = GLA_CHUNK
    n_lvl = m_ref.shape[0]
    G = g_ref[...]
    eye = (lax.broadcasted_iota(jnp.int32, (C, C), 0) == lax.broadcasted_iota(jnp.int32, (C, C), 1)).astype(F32)
    for j in range(q_ref.shape[0] // C):
        rs = slice(j * C, (j + 1) * C)
        la = la_ref[rs, :]
        hi, mid, lo = _split3(la)
        b = _dot(G, hi) + _dot(G, mid) + _dot(G, lo)
        b_last = b[C - 1:C]
        q_all = q_ref[rs, :]
        k_all = k_ref[rs, :]
        qe_ref[0, rs, :] = (q_all * jnp.exp(b)).astype(BF16)
        ke_ref[0, rs, :] = (k_all * jnp.exp(b_last - b)).astype(BF16)
        d_ref[0, j] = jnp.broadcast_to(jnp.exp(b_last), (8, GLA_DK))
        q_lvl, k_lvl = [], []
        for l in range(n_lvl):
            El = jnp.exp(_level_exponents(b, la, C >> (l + 1)))
            q_lvl.append((q_all * El).astype(BF16))
            k_lvl.append((k_all * El).astype(BF16))
        for h in range(GLA_HEADS):
            ks = slice(h * GLA_HK, (h + 1) * GLA_HK)
            vs = slice(h * GLA_HV, (h + 1) * GLA_HV)
            v = v_ref[rs, vs]
            att = eye * jnp.sum(q_all[:, ks] * k_all[:, ks], -1, keepdims=True)
            for l in range(n_lvl):
                att = att + m_ref[l] * _dot_nt(q_lvl[l][:, ks], k_lvl[l][:, ks])
            o_ref[0, rs, vs] = _dot(att.astype(BF16), v.astype(BF16))
            vt_ref[0, j, vs, :] = v.T.astype(BF16)


def _gla_inter_kernel(o_ref, qe_ref, ke_ref, vt_ref, d_ref, og_ref, s_ref, st_ref):
    c = pl.program_id(0)
    batch = o_ref.shape[0]

    @pl.when(c == 0)
    def _():
        st_ref[...] = jnp.zeros_like(st_ref)

    for b in range(batch):
        for h in range(GLA_HEADS):
            ks = slice(h * GLA_HK, (h + 1) * GLA_HK)
            vs = slice(h * GLA_HV, (h + 1) * GLA_HV)
            st = st_ref[b * GLA_HEADS + h]
            og_ref[b, :, vs] = o_ref[b, :, vs] + _dot_nt(qe_ref[b, :, ks], st.astype(BF16))
            st_ref[b * GLA_HEADS + h] = st * d_ref[b, 0, 0:1, ks] + _dot(vt_ref[b, 0, vs, :], ke_ref[b, :, ks])

    @pl.when(c == pl.num_programs(0) - 1)
    def _():
        for b in range(batch):
            for h in range(GLA_HEADS):
                s_ref[b, h] = st_ref[b * GLA_HEADS + h].T


def _gla_prompt(gq, gk, gv, la, batch, lp):
    C = GLA_CHUNK
    nc = lp // C
    G, M = _gla_tables(C)
    G = jnp.asarray(G, BF16)
    M = jnp.asarray(M, F32)
    cps = GLA_INTRA_CHUNKS
    assert nc % cps == 0
    rows = lambda w: pl.BlockSpec((cps * C, w), lambda b, c: (b * (nc // cps) + c, 0))
    rows3 = lambda w: pl.BlockSpec((1, cps * C, w), lambda b, c: (b, c, 0))
    full = lambda a: pl.BlockSpec(a.shape, lambda b, c: (0,) * a.ndim)
    o_intra, qe, ke, vt, d = pl.pallas_call(
        _gla_intra_kernel,
        grid=(batch, nc // cps),
        in_specs=[full(G), full(M), rows(GLA_DK), rows(GLA_DK), rows(GLA_DV), rows(GLA_DK)],
        out_specs=[rows3(GLA_DV), rows3(GLA_DK), rows3(GLA_DK),
                   pl.BlockSpec((1, cps, GLA_DV, C), lambda b, c: (b, c, 0, 0)),
                   pl.BlockSpec((1, cps, 8, GLA_DK), lambda b, c: (b, c, 0, 0))],
        out_shape=[jax.ShapeDtypeStruct((batch, lp, GLA_DV), F32),
                   jax.ShapeDtypeStruct((batch, lp, GLA_DK), BF16), jax.ShapeDtypeStruct((batch, lp, GLA_DK), BF16),
                   jax.ShapeDtypeStruct((batch, nc, GLA_DV, C), BF16),
                   jax.ShapeDtypeStruct((batch, nc, 8, GLA_DK), F32)],
        compiler_params=pltpu.CompilerParams(dimension_semantics=("parallel", "parallel")),
        name="gla_intra",
    )(G, M, gq, gk, gv, la)
    chunk = lambda w: pl.BlockSpec((batch, C, w), lambda c: (0, c, 0))
    og, s_fin = pl.pallas_call(
        _gla_inter_kernel,
        grid=(nc,),
        in_specs=[chunk(GLA_DV), chunk(GLA_DK), chunk(GLA_DK),
                  pl.BlockSpec((batch, 1, GLA_DV, C), lambda c: (0, c, 0, 0)),
                  pl.BlockSpec((batch, 1, 8, GLA_DK), lambda c: (0, c, 0, 0))],
        out_specs=[chunk(GLA_DV), pl.BlockSpec((batch, GLA_HEADS, GLA_HK, GLA_HV), lambda c: (0, 0, 0, 0))],
        out_shape=[jax.ShapeDtypeStruct((batch, lp, GLA_DV), F32),
                   jax.ShapeDtypeStruct((batch, GLA_HEADS, GLA_HK, GLA_HV), F32)],
        scratch_shapes=[pltpu.VMEM((batch * GLA_HEADS, GLA_HV, GLA_HK), F32)],
        compiler_params=pltpu.CompilerParams(dimension_semantics=("arbitrary",)),
        name="gla_inter",
    )(o_intra, qe, ke, vt, d)
    return og.reshape(batch * lp, GLA_DV), s_fin


def _gla_sample_kernel(q_ref, k_ref, v_ref, la_ref, s0_ref, o_ref, s_ref, *, t_s):
    T = t_s
    row = lax.broadcasted_iota(jnp.int32, (T, GLA_HK), 0)
    k_fill = jnp.zeros((GLA_HK - T - 8, GLA_HK), F32)
    v_fill = jnp.zeros((GLA_HK - T, GLA_HV), F32)

    def one_seq(s, carry):
        rs = pl.ds(pl.multiple_of(s * T, T), T)
        for h in range(GLA_HEADS):
            ks = slice(h * GLA_HK, (h + 1) * GLA_HK)
            vs = slice(h * GLA_HV, (h + 1) * GLA_HV)
            q = q_ref[rs, ks]
            k = k_ref[rs, ks]
            v = v_ref[rs, vs]
            b = la_ref[rs, ks]
            sh = 1
            while sh < T:
                b = b + jnp.where(row >= sh, pltpu.roll(b, sh, 0), 0.0)
                sh *= 2
            S = s0_ref[s, h]
            o = _dot((q * jnp.exp(b)).astype(BF16), S.astype(BF16))
            for j in range(T):
                e = jnp.exp(jnp.where(row >= j, b - b[j:j + 1], NEG))
                a_col = jnp.sum(q * k[j:j + 1] * e, -1, keepdims=True)
                o = o + a_col * v[j:j + 1]
            o_ref[rs, vs] = o
            b_last = b[T - 1:T]
            ke = k * jnp.exp(b_last - b)
            kt = jnp.concatenate([ke, jnp.broadcast_to(jnp.exp(b_last), (8, GLA_HK)), k_fill], 0).T
            v_pad = jnp.concatenate([v, v_fill], 0)
            s_ref[s, h] = S * kt[:, T:T + 1] + _dot(kt.astype(BF16), v_pad.astype(BF16))
        return carry

    lax.fori_loop(0, SAMPLE_SEQS, one_seq, 0)


def _gla_sample(gq, gk, gv, la, s0, row0, t_s):
    n_seq = s0.shape[0]
    sb = SAMPLE_SEQS
    rb = sb * t_s
    b0 = row0 // rb
    rows = lambda w: pl.BlockSpec((rb, w), lambda i: (b0 + i, 0))
    st = pl.BlockSpec((sb, GLA_HEADS, GLA_HK, GLA_HV), lambda i: (i, 0, 0, 0))
    return pl.pallas_call(
        functools.partial(_gla_sample_kernel, t_s=t_s),
        grid=(n_seq // sb,),
        in_specs=[rows(GLA_DK), rows(GLA_DK), rows(GLA_DV), rows(GLA_DK), st],
        out_specs=[pl.BlockSpec((rb, GLA_DV), lambda i: (i, 0)), st],
        out_shape=[jax.ShapeDtypeStruct((n_seq * t_s, GLA_DV), F32), jax.ShapeDtypeStruct(s0.shape, F32)],
        compiler_params=pltpu.CompilerParams(dimension_semantics=("parallel",), vmem_limit_bytes=VMEM_LIMIT),
        name="gla_sample",
    )(gq, gk, gv, la, s0)


def _route(lt, valid):
    tm = lt.shape[1]
    el = lt[0:N_EXPERTS]
    gl = lt[N_EXPERTS:N_EXPERTS + N_GROUPS]
    g_max = jnp.max(gl, 0, keepdims=True)
    g_row = lax.broadcasted_iota(jnp.int32, (N_GROUPS, tm), 0)
    g_idx = jnp.min(jnp.where(gl == g_max, g_row, N_GROUPS), 0, keepdims=True)
    p_max = 1.0 / jnp.sum(jnp.exp(gl - g_max), 0, keepdims=True)
    e_row = lax.broadcasted_iota(jnp.int32, (N_EXPERTS, tm), 0)
    m1 = jnp.where(e_row // EXP_PER_GROUP == g_idx, el, -jnp.inf)
    v1 = jnp.max(m1, 0, keepdims=True)
    i1 = jnp.min(jnp.where(m1 == v1, e_row, N_EXPERTS), 0, keepdims=True)
    m2 = jnp.where(e_row == i1, -jnp.inf, m1)
    v2 = jnp.max(m2, 0, keepdims=True)
    i2 = jnp.min(jnp.where(m2 == v2, e_row, N_EXPERTS), 0, keepdims=True)
    e2 = jnp.exp(v2 - v1)
    w1 = p_max / (1.0 + e2)
    w2 = p_max * e2 / (1.0 + e2)
    o_row = lax.broadcasted_iota(jnp.int32, (8, tm), 0)
    ids = jnp.where(o_row == 0, i1, jnp.where(o_row == 1, i2, -1))
    return jnp.where(valid, ids, -1), jnp.where(o_row == 0, w1, jnp.where(o_row == 1, w2, 0.0))


def _store_row_tiles(ref, x):
    t, d = x.shape
    n = d // LANES
    for s in range(n):
        ref[pl.ds(s, t, stride=n), :] = x[:, s * LANES:(s + 1) * LANES]


def _load_row_tiles(ref, t, n=ROW_CHUNKS):
    return jnp.concatenate([ref[pl.ds(s, t, stride=n), :] for s in range(n)], axis=1)


def _merge_kernel(h_ref, yap_ref, yas_ref, ogp_ref, ogs_ref, gr_ref, gate_ref, valid_ref, ng_ref,
                  bg_ref, wa_ref, wg_ref, wo_ref, g1_ref, b1_ref, wrh_ref, wrl_ref, br_ref, u_ref,
                  h1t_ref, ids_ref, wts_ref, cnt_ref, run_ref, *, prompt_tiles):
    @pl.when(pl.program_id(0) == 0)
    def _():
        run_ref[...] = jnp.zeros_like(run_ref)

    h = h_ref[...]
    is_prompt = pl.program_id(0) < prompt_tiles
    og = jnp.where(is_prompt, ogp_ref[...], ogs_ref[...])
    ya = jnp.where(is_prompt, yap_ref[...], yas_ref[...])
    parts = []
    for hh in range(GLA_HEADS):
        o = og[:, hh * GLA_HV:(hh + 1) * GLA_HV]
        parts.append(o * lax.rsqrt(jnp.mean(o * o, -1, keepdims=True) + EPS))
    gr = gr_ref[...]
    y_gla = jnp.concatenate(parts, 1) * ng_ref[...] * (gr * _sigmoid(gr))
    a = _dot(ya.astype(BF16), wa_ref[...])
    b = _dot(y_gla.astype(BF16), wg_ref[...])
    gate = _sigmoid(gate_ref[...] + bg_ref[...])
    hm = gate[:, :D_MODEL] * a + gate[:, D_MODEL:] * b
    mix = _dot(hm.astype(BF16), wo_ref[...])
    h1 = _ln(DN_ALPHA * h + mix, g1_ref[...], b1_ref[...])
    _store_row_tiles(h1t_ref, h1)
    h_hi = h1.astype(BF16)
    h_lo = (h1 - h_hi.astype(F32)).astype(BF16)
    logits = _dot(h_hi, wrh_ref[...]) + _dot(h_lo, wrh_ref[...]) + _dot(h_hi, wrl_ref[...]) + br_ref[...]
    ids, wts_ref[...] = _route(logits.T, valid_ref[...] > 0.0)
    tm = ids.shape[1]
    e_row = lax.broadcasted_iota(jnp.int32, (N_EXPERTS, tm), 0)
    run = run_ref[:, 0:1]
    ranks = []
    for kk in range(2):
        onehot = (e_row == ids[kk:kk + 1]).astype(F32)
        before = _dot(onehot.astype(BF16), u_ref[...])
        ranks.append(jnp.sum(onehot * (run + before), 0, keepdims=True).astype(jnp.int32))
        run = run + jnp.sum(onehot, 1, keepdims=True)
    run_ref[...] = jnp.broadcast_to(run, run_ref.shape)
    o_row = lax.broadcasted_iota(jnp.int32, (8, tm), 0)
    ids_ref[...] = jnp.where(o_row == 2, ranks[0], jnp.where(o_row == 3, ranks[1], ids))
    cnt_ref[...] = run_ref[...].astype(jnp.int32)


def _merge(h, ya_p, ya_s, og_p, og_s, gr, gate, valid, ng, bg, wa, wg, wo, g1, b1, wrh, wrl, br):
    n = h.shape[0]
    tm = ROW_TILE
    u = jnp.asarray(np.triu(np.ones((tm, tm), np.float32), 1), BF16)
    pt = ya_p.shape[0] // tm
    st = ya_s.shape[0] // tm
    row = lambda w: pl.BlockSpec((tm, w), lambda i: (i, 0))
    row_p = lambda w: pl.BlockSpec((tm, w), lambda i: (jnp.minimum(i, pt - 1), 0))
    row_s = lambda w: pl.BlockSpec((tm, w), lambda i: (jnp.clip(i - pt, 0, st - 1), 0))
    lane = lambda r: pl.BlockSpec((r, tm), lambda i: (0, i))
    full = lambda a: pl.BlockSpec(a.shape, lambda i: (0,) * a.ndim)
    return pl.pallas_call(
        functools.partial(_merge_kernel, prompt_tiles=pt),
        grid=(n // tm,),
        in_specs=[row(D_MODEL), row_p(Q_W), row_s(Q_W), row_p(GLA_DV), row_s(GLA_DV), row(GLA_DV), row(2 * D_MODEL),
                  lane(1), full(ng), full(bg), full(wa), full(wg), full(wo), full(g1), full(b1),
                  full(wrh), full(wrl), full(br), full(u)],
        out_specs=[pl.BlockSpec((tm * ROW_CHUNKS, LANES), lambda i: (i, 0)), lane(8), lane(8),
                   pl.BlockSpec((N_EXPERTS, LANES), lambda i: (0, 0))],
        out_shape=[jax.ShapeDtypeStruct((n * ROW_CHUNKS, LANES), F32),
                   jax.ShapeDtypeStruct((8, n), jnp.int32), jax.ShapeDtypeStruct((8, n), F32),
                   jax.ShapeDtypeStruct((N_EXPERTS, LANES), jnp.int32)],
        scratch_shapes=[pltpu.VMEM((N_EXPERTS, LANES), F32)],
        compiler_params=pltpu.CompilerParams(dimension_semantics=("arbitrary",), vmem_limit_bytes=VMEM_LIMIT),
        name="merge",
    )(h, ya_p, ya_s, og_p, og_s, gr, gate, valid, ng, bg, wa, wg, wo, g1, b1, wrh, wrl, br, u)


def _gather_row_tiles(idx_ref, idx0, src_hbm, dst, sem, n):
    def body(r, carry):
        t = idx_ref[idx0 + r]
        pltpu.make_async_copy(src_hbm.at[pl.ds(pl.multiple_of(t * ROW_CHUNKS, ROW_CHUNKS), ROW_CHUNKS), :],
                              dst.at[pl.ds(pl.multiple_of(r * ROW_CHUNKS, ROW_CHUNKS), ROW_CHUNKS), :], sem).start()
        return carry
    lax.fori_loop(0, n, body, 0, unroll=8)


def _wait_row_tiles(src_hbm, dst, sem, n):
    pltpu.make_async_copy(src_hbm.at[pl.ds(0, n * ROW_CHUNKS), :], dst, sem).wait()


def _tiles(ref, first, n=1):
    return ref.at[pl.ds(pl.multiple_of(first * ROW_CHUNKS, ROW_CHUNKS), n * ROW_CHUNKS), :]


def _dispatch_kernel(dest_ref, start_ref, cnt_ref, nt_ref, h_hbm, xs_hbm, buf, zbuf, sem_in, sem_out, sem_z,
                     *, ranges, n_rows, max_tiles):
    CH = DISPATCH_CHUNK
    T = FFN_TILE
    c = pl.program_id(0)
    n_chunks = pl.num_programs(0)

    def load(row0, slot):
        return pltpu.make_async_copy(_tiles(h_hbm, row0, CH), buf.at[slot], sem_in.at[slot])

    def scatter(row0, slot):
        def body(r, carry):
            for kk in range(2):
                d = dest_ref[kk * n_rows + row0 + r]
                pltpu.make_async_copy(_tiles(buf.at[slot], r), _tiles(xs_hbm, d), sem_out.at[slot]).start()
            return carry
        lax.fori_loop(0, CH, body, 0, unroll=8)

    def drain(slot):
        for _ in range(2):
            pltpu.make_async_copy(buf.at[slot], _tiles(xs_hbm, 0, CH), sem_out.at[slot]).wait()

    groups, per_group, first, stride, extra, extra_first = ranges

    def row0(j):
        in_group = first + (j // per_group) * stride + (j % per_group) * CH
        return jnp.where(j < groups * per_group, in_group, extra_first + (j - groups * per_group) * CH)

    @pl.when(c == 0)
    def _():
        for j in range(2):
            load(row0(j), j).start()

    slot = c % 3
    load(row0(c), slot).wait()
    scatter(row0(c), slot)

    @pl.when(c > 0)
    def _():
        drain((c + 2) % 3)

    @pl.when(c + 2 < n_chunks)
    def _():
        load(row0(c + 2), (c + 2) % 3).start()

    @pl.when(c == n_chunks - 1)
    def _():
        drain(slot)
        _zero_unowned_slots(start_ref, cnt_ref, nt_ref, xs_hbm, zbuf, sem_z, max_tiles)


def _zero_unowned_slots(start_ref, cnt_ref, nt_ref, xs_hbm, zbuf, sem_z, max_tiles):
    T = FFN_TILE
    zbuf[...] = jnp.zeros_like(zbuf)

    def tail_copies(e, wait):
        cnt = cnt_ref[e]
        n = (T - (cnt & (T - 1))) & (T - 1)
        first = start_ref[e] + cnt
        for bit in reversed(range(T.bit_length() - 1)):
            size = 1 << bit

            @pl.when((n & size) != 0)
            def _():
                cp = pltpu.make_async_copy(_tiles(zbuf, 0, size),
                                           _tiles(xs_hbm, first + ((n >> (bit + 1)) << (bit + 1)), size), sem_z)
                cp.wait() if wait else cp.start()

    def unused_tile(t, wait):
        cp = pltpu.make_async_copy(zbuf, _tiles(xs_hbm, t * T, T), sem_z)
        cp.wait() if wait else cp.start()

    for wait in (False, True):
        def per_expert(e, carry, wait=wait):
            tail_copies(e, wait)
            return carry

        def per_tile(t, carry, wait=wait):
            unused_tile(t, wait)
            return carry
        lax.fori_loop(0, N_EXPERTS, per_expert, 0)
        lax.fori_loop(nt_ref[0], max_tiles, per_tile, 0)


def _dispatch(dest, start, counts, n_tiles, h1t, ranges, max_tiles):
    T = FFN_TILE
    n_rows = h1t.shape[0] // ROW_CHUNKS
    return pl.pallas_call(
        functools.partial(_dispatch_kernel, ranges=ranges, n_rows=n_rows, max_tiles=max_tiles),
        grid_spec=pltpu.PrefetchScalarGridSpec(
            num_scalar_prefetch=4,
            grid=(ranges[0] * ranges[1] + ranges[4],),
            in_specs=[pl.BlockSpec(memory_space=pl.ANY)],
            out_specs=pl.BlockSpec(memory_space=pl.ANY),
            scratch_shapes=[pltpu.VMEM((3, DISPATCH_CHUNK * ROW_CHUNKS, LANES), F32),
                            pltpu.VMEM((T * ROW_CHUNKS, LANES), F32),
                            pltpu.SemaphoreType.DMA((3,)), pltpu.SemaphoreType.DMA((3,)), pltpu.SemaphoreType.DMA]),
        out_shape=jax.ShapeDtypeStruct((max_tiles * T * ROW_CHUNKS, LANES), F32),
        compiler_params=pltpu.CompilerParams(dimension_semantics=("arbitrary",)),
        name="dispatch",
    )(dest, start, counts, n_tiles, h1t)


def _ffn_kernel(te_ref, nxt_ref, nt_ref, x_ref, wg_hbm, wu_hbm, wd_hbm, out_ref, stage_g, stage_u, stage_d, sem,
                wgb, wub, wdb):
    i = pl.program_id(0)
    nt = nt_ref[0]
    T = FFN_TILE

    def stage(e):
        return [pltpu.make_async_copy(src.at[e], dst, sem.at[n])
                for n, (src, dst) in enumerate(((wg_hbm, stage_g), (wu_hbm, stage_u), (wd_hbm, stage_d)))]

    @pl.when(i == 0)
    def _():
        for cp in stage(te_ref[0]):
            cp.start()

    @pl.when(i < nt)
    def _():
        @pl.when((i == 0) | (te_ref[i] != te_ref[jnp.maximum(i - 1, 0)]))
        def _():
            for cp in stage(te_ref[i]):
                cp.wait()
            wgb[...] = stage_g[...].astype(BF16)
            wub[...] = stage_u[...].astype(BF16)
            wdb[...] = stage_d[...].astype(BF16)

            @pl.when(nxt_ref[i] >= 0)
            def _():
                for cp in stage(nxt_ref[i]):
                    cp.start()

        x = _load_row_tiles(x_ref, T).astype(BF16)
        g = _dot(x, wgb[...])
        u = _dot(x, wub[...])
        _store_row_tiles(out_ref, _dot((g * _sigmoid(g) * u).astype(BF16), wdb[...]))

    @pl.when(i >= nt)
    def _():
        out_ref[...] = jnp.zeros_like(out_ref)


def _ffn(tile_expert, next_expert, n_tiles, xs, w_g, w_u, w_d):
    T = FFN_TILE
    max_tiles = tile_expert.shape[0]
    hbm = pl.BlockSpec(memory_space=pl.ANY)
    tile = lambda imap: pl.BlockSpec((T * ROW_CHUNKS, LANES), imap)
    return pl.pallas_call(
        _ffn_kernel,
        grid_spec=pltpu.PrefetchScalarGridSpec(
            num_scalar_prefetch=3,
            grid=(max_tiles,),
            in_specs=[tile(lambda i, te, nxt, nt: (jnp.minimum(i, nt[0] - 1), 0)), hbm, hbm, hbm],
            out_specs=tile(lambda i, te, nxt, nt: (i, 0)),
            scratch_shapes=[pltpu.VMEM(w_g.shape[1:], F32), pltpu.VMEM(w_u.shape[1:], F32),
                            pltpu.VMEM(w_d.shape[1:], F32), pltpu.SemaphoreType.DMA((3,)),
                            pltpu.VMEM(w_g.shape[1:], BF16), pltpu.VMEM(w_u.shape[1:], BF16),
                            pltpu.VMEM(w_d.shape[1:], BF16)]),
        out_shape=jax.ShapeDtypeStruct(xs.shape, F32),
        compiler_params=pltpu.CompilerParams(dimension_semantics=("arbitrary",)),
        name="ffn",
    )(tile_expert, next_expert, n_tiles, xs, w_g, w_u, w_d)


def _combine_kernel(dest_ref, h_ref, w_ref, g_ref, b_ref, ys_hbm, y_ref, buf, sem, *, blk, n_rows):
    i = pl.program_id(0)
    n = pl.num_programs(0)
    T = CMB_TILE

    def gather(step, slot):
        for kk in range(2):
            _gather_row_tiles(dest_ref, kk * n_rows + blk(step) * T, ys_hbm, buf.at[slot, kk], sem.at[slot], T)

    @pl.when(i == 0)
    def _():
        gather(0, 0)

    @pl.when(i + 1 < n)
    def _():
        gather(i + 1, (i + 1) % 2)

    slot = i % 2
    for kk in range(2):
        _wait_row_tiles(ys_hbm, buf.at[slot, kk], sem.at[slot], T)
    w = w_ref[...]
    ff = w[:, 0:1] * _load_row_tiles(buf.at[slot, 0], T) + w[:, 1:2] * _load_row_tiles(buf.at[slot, 1], T)
    y_ref[...] = _ln(DN_ALPHA * _load_row_tiles(h_ref, T) + ff, g_ref[...], b_ref[...])


def _combine(dest, h1t, wts_t, ys, g2, b2, n_out, first_block, blocks_per_batch, skip_blocks):
    T = CMB_TILE
    n_rows = h1t.shape[0] // ROW_CHUNKS
    if skip_blocks:
        blk = lambda i: first_block + i + (i // blocks_per_batch + 1) * skip_blocks
    else:
        blk = lambda i: first_block + i
    full = lambda a: pl.BlockSpec(a.shape, lambda i, d: (0,) * a.ndim)
    return pl.pallas_call(
        functools.partial(_combine_kernel, blk=blk, n_rows=n_rows),
        grid_spec=pltpu.PrefetchScalarGridSpec(
            num_scalar_prefetch=1,
            grid=(n_out // T,),
            in_specs=[pl.BlockSpec((T * ROW_CHUNKS, LANES), lambda i, d: (blk(i), 0)),
                      pl.BlockSpec((T, 2), lambda i, d: (blk(i), 0)),
                      full(g2), full(b2), pl.BlockSpec(memory_space=pl.ANY)],
            out_specs=pl.BlockSpec((T, D_MODEL), lambda i, d: (i, 0)),
            scratch_shapes=[pltpu.VMEM((2, 2, T * ROW_CHUNKS, LANES), F32), pltpu.SemaphoreType.DMA((2,))]),
        out_shape=jax.ShapeDtypeStruct((n_out, D_MODEL), F32),
        compiler_params=pltpu.CompilerParams(dimension_semantics=("arbitrary",)),
        name="combine",
    )(dest, h1t, wts_t, g2, b2, ys)


def _dispatch_plan(routing, counts, max_tiles):
    T = FFN_TILE
    tiles_e = (counts + T - 1) // T
    tile_end = jnp.cumsum(tiles_e)
    n_tiles = tile_end[-1]
    start = (tile_end - tiles_e) * T
    ids, rank = routing[0:2], routing[2:4]
    onehot = (ids[..., None] == jnp.arange(N_EXPERTS, dtype=jnp.int32)).astype(jnp.int32)
    dest = (jnp.sum(onehot * start, axis=-1) + jnp.where(ids >= 0, rank, 0)).reshape(-1).astype(jnp.int32)
    te = jnp.sum((jnp.arange(max_tiles, dtype=jnp.int32)[:, None] >= tile_end[None, :]).astype(jnp.int32), axis=1)
    te_last = jnp.take(te, jnp.maximum(n_tiles - 1, 0))
    te = jnp.where(jnp.arange(max_tiles) < n_tiles, te, te_last).astype(jnp.int32)
    used = jnp.where(counts > 0, jnp.arange(N_EXPERTS, dtype=jnp.int32), N_EXPERTS)
    later = jnp.concatenate([jnp.flip(lax.cummin(jnp.flip(used)))[1:], jnp.full((1,), N_EXPERTS, jnp.int32)])
    nxt = jnp.take(later, te)
    nxt = jnp.where(nxt < N_EXPERTS, nxt, -1).astype(jnp.int32)
    return te, nxt, n_tiles.reshape(1).astype(jnp.int32), start.astype(jnp.int32), dest


def kernel(x_prompt, x_sample, state_swa_k, state_swa_v, state_gla, meta_tokens, ln_emb_g, ln_emb_b, w_in, b_gate, attn_sink, w_alpha2, b_alpha, gla_norm_g, w_attn_br, w_gla_br, w_out, ln1_g, ln1_b, w_router_group, b_router_group, w_router_expert, b_router_expert, w_exp_gate, w_exp_up, w_exp_down, ln2_g, ln2_b):
    B, seq, _ = x_prompt.shape
    n_seq, t_s, _ = x_sample.shape
    depth = w_in.shape[0]
    assert depth == 1 and seq % ATT_BLOCK == 0 and t_s == 8 and SKIP_ROWS == ATT_BLOCK == WINDOW
    lp = SKIP_ROWS + seq
    NP, NS = B * lp, n_seq * t_s
    NR = NP + NS
    assert NP % ROW_TILE == 0 and NS % ROW_TILE == 0 and n_seq % SAMPLE_SEQS == 0
    assert seq % DISPATCH_CHUNK == 0 and NS % DISPATCH_CHUNK == 0
    l = 0
    row2 = lambda a: a.reshape(1, -1)

    head = jnp.concatenate([jnp.zeros((FRONT_PAD, D_MODEL), F32), meta_tokens], axis=0)
    pos = np.arange(NR)
    moe_valid = jnp.asarray(~((pos < NP) & (pos % lp < SKIP_ROWS)), F32).reshape(1, NR)

    wi = w_in[l]
    c_ga = sum((Q_W, KV_W, KV_W, GLA_DK, GLA_DK, GLA_DV))
    w_bf = jnp.concatenate([wi[:, :c_ga].astype(BF16), wi[:, c_ga + GLA_RANK:].astype(BF16),
                            wi[:, c_ga:c_ga + GLA_RANK].astype(BF16),
                            jnp.zeros((D_MODEL, 128 - GLA_RANK), BF16)], axis=1)
    wa2_bf = jnp.concatenate([w_alpha2[l], jnp.zeros((128 - GLA_RANK, GLA_DK), F32)], axis=0).astype(BF16)

    h, q, k, v, gq, gk, gv, la, gr, gate = _inproj(x_prompt, x_sample.reshape(NS, D_MODEL), head, row2(ln_emb_g),
                                                   row2(ln_emb_b), w_bf, wa2_bf, row2(b_alpha[l]))

    sink = attn_sink[l]
    ya_p, k_p, v_p = _swa_prompt(sink, q, k, v, B, lp)
    buf_k = state_swa_k[l].reshape(n_seq, WINDOW, KV_W)
    buf_v = state_swa_v[l].reshape(n_seq, WINDOW, KV_W)
    ya_s, nk_s, nv_s = _swa_sample(sink, q, k, v, buf_k, buf_v, NP, t_s)

    og_p, s_p = _gla_prompt(gq, gk, gv, la, B, lp)
    og_s, s_s = _gla_sample(gq, gk, gv, la, state_gla[l], NP, t_s)

    wr = jnp.concatenate([w_router_expert[l], w_router_group[l],
                          jnp.zeros((D_MODEL, LANES - N_EXPERTS - N_GROUPS), F32)], axis=1)
    br = jnp.concatenate([b_router_expert[l], b_router_group[l],
                          jnp.zeros((LANES - N_EXPERTS - N_GROUPS,), F32)]).reshape(1, LANES)
    wr_hi = wr.astype(BF16)
    h1t, routing, wts, counts = _merge(h, ya_p, ya_s, og_p, og_s, gr, gate, moe_valid,
                                       row2(gla_norm_g[l]), row2(b_gate[l]),
                                       w_attn_br[l].astype(BF16), w_gla_br[l].astype(BF16), w_out[l].astype(BF16),
                                       row2(ln1_g[l]), row2(ln1_b[l]), wr_hi, (wr - wr_hi.astype(F32)).astype(BF16), br)

    n_tok = B * seq + NS
    max_tiles = (2 * n_tok) // FFN_TILE + N_EXPERTS
    counts = counts[:, 0]
    te, nxt, n_tiles, start, dest = _dispatch_plan(routing[0:4], counts, max_tiles)
    routed = (B, seq // DISPATCH_CHUNK, SKIP_ROWS, lp, NS // DISPATCH_CHUNK, NP)
    xs = _dispatch(dest, start, counts, n_tiles, h1t, routed, max_tiles)
    ys = _ffn(te, nxt, n_tiles, xs, w_exp_gate[l], w_exp_up[l], w_exp_down[l])

    wts_t = wts[0:2].T
    g2, b2 = row2(ln2_g[l]), row2(ln2_b[l])
    skip_blocks = SKIP_ROWS // CMB_TILE
    y_p = _combine(dest, h1t, wts_t, ys, g2, b2, B * seq, 0, seq // CMB_TILE, skip_blocks)
    y_s = _combine(dest, h1t, wts_t, ys, g2, b2, NS, NP // CMB_TILE, 1, 0)

    kv_shape = (1, B, WINDOW, N_KV, HEAD_DIM)
    return (y_p.reshape(B, seq, D_MODEL), y_s.reshape(n_seq, t_s, D_MODEL),
            k_p.reshape(kv_shape), v_p.reshape(kv_shape), s_p[None],
            nk_s.reshape(1, n_seq, WINDOW, N_KV, HEAD_DIM), nv_s.reshape(1, n_seq, WINDOW, N_KV, HEAD_DIM),
            s_s[None])
```

```python
import functools

import numpy as np
import jax
import jax.numpy as jnp
from jax import lax
from jax.experimental import pallas as pl
from jax.experimental.pallas import tpu as pltpu

F32 = jnp.float32
BF16 = jnp.bfloat16

D_MODEL = 1024
N_META = 16
HEAD_DIM = 64
N_HEADS = 8
N_KV = 2
Q_PER_KV = 4
WINDOW = 128
ATT_BLOCK = 128
GLA_HEADS = 4
GLA_HK = 128
GLA_HV = 256
GLA_DK = GLA_HEADS * GLA_HK
GLA_DV = GLA_HEADS * GLA_HV
GLA_RANK = 16
GLA_TAU = 16.0
GLA_CHUNK = 64
N_GROUPS = 4
EXP_PER_GROUP = 8
N_EXPERTS = 32
D_EXPERT = 256
DN_ALPHA = 2.0 ** 0.25
EPS = 1e-5
NEG = -1e30

FRONT_PAD = (-N_META) % ATT_BLOCK
SKIP_ROWS = FRONT_PAD + N_META

Q_W, KV_W = N_HEADS * HEAD_DIM, N_KV * HEAD_DIM
SEG = {}
_o = 0
for _n, _w in (("q", Q_W), ("k", KV_W), ("v", KV_W), ("gq", GLA_DK), ("gk", GLA_DK), ("gv", GLA_DV)):
    SEG[_n] = (0, _o, _o + _w)
    _o += _w
W_IN_SPLIT = _o
SEG["gr"] = (1, 0, GLA_DV)
SEG["gate"] = (1, GLA_DV, GLA_DV + 2 * D_MODEL)
SEG["ga"] = (2, 0, 128)

ROW_TILE = 256
FFN_TILE = 256
CMB_TILE = 128
DISPATCH_CHUNK = 128
GLA_INTRA_CHUNKS = 2
INPROJ_TILE = 512
SAMPLE_SEQS = 8
LANES = 128
ROW_CHUNKS = D_MODEL // LANES
VMEM_LIMIT = 56 * 1024 * 1024


def _ln(x, g, b):
    mu = jnp.mean(x, -1, keepdims=True)
    xc = x - mu
    var = jnp.mean(xc * xc, -1, keepdims=True)
    return xc * lax.rsqrt(var + EPS) * g + b


def _sigmoid(x):
    return 0.5 * jnp.tanh(0.5 * x) + 0.5


def _dot(a, b):
    return jnp.dot(a, b, preferred_element_type=F32)


def _dot_nt(a, b):
    return lax.dot_general(a, b, (((1,), (1,)), ((), ())), preferred_element_type=F32)


def _inproj_kernel(*refs, blocks, batch_blocks, prompt_blocks):
    xp_refs, xs_refs = refs[:blocks], refs[blocks:2 * blocks]
    (head_ref, g_ref, b_ref, w0_ref, w1_ref, w2_ref, wa2_ref, ba_ref, bg_ref,
     h_ref, q_ref, k_ref, v_ref, gq_ref, gk_ref, gv_ref, la_ref, gr_ref, gate_ref) = refs[2 * blocks:]
    w_refs = (w0_ref, w1_ref, w2_ref)
    row = lax.broadcasted_iota(jnp.int32, (ATT_BLOCK, 1), 0)
    xs, keeps = [], []
    for s in range(blocks):
        p = pl.program_id(0) * blocks + s
        is_sample = p >= prompt_blocks
        is_head = jnp.logical_and(jnp.logical_not(is_sample), p % batch_blocks == 0)
        xs.append(jnp.where(is_sample, xs_refs[s][...], jnp.where(is_head, head_ref[...], xp_refs[s][...])))
        keeps.append(jnp.where(jnp.logical_and(is_head, row < FRONT_PAD), 0.0, 1.0))
    h = _ln(jnp.concatenate(xs, 0), g_ref[...], b_ref[...])
    h_ref[...] = h
    hb = h.astype(BF16)
    keep = jnp.concatenate(keeps, 0)

    def seg(name):
        piece, a, b = SEG[name]
        return _dot(hb, w_refs[piece][:, a:b])

    q_ref[...] = seg("q")
    k_ref[...] = seg("k")
    v_ref[...] = seg("v")
    gq_ref[...] = seg("gq") * (GLA_HK ** -0.5)
    gk_ref[...] = seg("gk") * keep
    gv_ref[...] = seg("gv") * keep
    gr = seg("gr")
    gr_ref[...] = (gr * _sigmoid(gr)).astype(BF16)
    gate_ref[...] = _sigmoid(seg("gate") + bg_ref[...]).astype(BF16)
    z = _dot(seg("ga").astype(BF16), wa2_ref[...]) + ba_ref[...]
    la = (jnp.minimum(z, 0.0) - jnp.log(1.0 + jnp.exp(-jnp.abs(z)))) * (1.0 / GLA_TAU)
    la_ref[...] = la * keep


def _inproj(x_prompt, x_sample, head, ln_g, ln_b, w_pieces, wa2_bf, b_alpha, b_gate):
    B, seq, _ = x_prompt.shape
    blk = ATT_BLOCK
    seq_blocks = seq // blk
    batch_blocks = seq_blocks + 1
    prompt_blocks = B * batch_blocks
    sample_blocks = x_sample.shape[0] // blk
    n = (prompt_blocks + sample_blocks) * blk
    tm = INPROJ_TILE
    blocks = tm // blk
    xp = x_prompt.reshape(B * seq, D_MODEL)

    def prompt_block(s):
        def imap(i):
            p = jnp.minimum(i * blocks + s, prompt_blocks - 1)
            return (p // batch_blocks * seq_blocks + jnp.maximum(p % batch_blocks - 1, 0), 0)
        return pl.BlockSpec((blk, D_MODEL), imap)

    def sample_block(s):
        return pl.BlockSpec((blk, D_MODEL), lambda i: (jnp.clip(i * blocks + s - prompt_blocks, 0, sample_blocks - 1), 0))

    widths = [D_MODEL, Q_W, KV_W, KV_W, GLA_DK, GLA_DK, GLA_DV, GLA_DK, GLA_DV, 2 * D_MODEL]
    row = lambda w: pl.BlockSpec((tm, w), lambda i: (i, 0))
    const = lambda a: pl.BlockSpec(a.shape, lambda i: (0,) * a.ndim, pipeline_mode=pl.Buffered(1))
    return pl.pallas_call(
        functools.partial(_inproj_kernel, blocks=blocks, batch_blocks=batch_blocks, prompt_blocks=prompt_blocks),
        grid=(n // tm,),
        in_specs=[prompt_block(s) for s in range(blocks)] + [sample_block(s) for s in range(blocks)]
                 + [const(a) for a in (head, ln_g, ln_b, *w_pieces, wa2_bf, b_alpha, b_gate)],
        out_specs=[row(w) for w in widths],
        out_shape=[jax.ShapeDtypeStruct((n, w), BF16 if i >= len(widths) - 2 else F32) for i, w in enumerate(widths)],
        compiler_params=pltpu.CompilerParams(dimension_semantics=("parallel",), vmem_limit_bytes=VMEM_LIMIT),
        name="inproj",
    )(*([xp] * blocks), *([x_sample] * blocks), head, ln_g, ln_b, *w_pieces, wa2_bf, b_alpha, b_gate)


def _softmax_pv(s, sink, vv):
    m = jnp.maximum(jnp.max(s, -1, keepdims=True), sink)
    p = jnp.exp(s - m)
    l = jnp.sum(p, -1, keepdims=True) + jnp.exp(sink - m)
    return _dot(p.astype(BF16), vv) / l


def _swa_prompt_kernel(sink_ref, bias_ref, q_ref, kp_ref, kc_ref, vp_ref, vc_ref, o_ref, kl_ref, vl_ref):
    q = q_ref[...] * (HEAD_DIM ** -0.5)
    kb = jnp.concatenate([kp_ref[...], kc_ref[...]], 0)
    vb = jnp.concatenate([vp_ref[...], vc_ref[...]], 0)
    low = lax.broadcasted_iota(jnp.int32, (1, 2 * HEAD_DIM), 1) < HEAD_DIM
    k_low = jnp.where(low, kb, 0.0)
    k_high = jnp.where(low, 0.0, kb)
    keys = {(0, 0): k_low.astype(BF16), (0, 1): pltpu.roll(k_low, HEAD_DIM, 1).astype(BF16),
            (1, 0): pltpu.roll(k_high, HEAD_DIM, 1).astype(BF16), (1, 1): k_high.astype(BF16)}
    ones_col = (lax.broadcasted_iota(jnp.int32, (1, 2 * HEAD_DIM), 1) == HEAD_DIM).astype(F32)
    values = [jnp.where(low, vb, ones_col).astype(BF16),
              jnp.where(low, pltpu.roll(vb, HEAD_DIM, 1), ones_col).astype(BF16)]
    for pair in range(N_HEADS // 2):
        qp = q[:, pair * 2 * HEAD_DIM:(pair + 1) * 2 * HEAD_DIM]
        outs = []
        for half in range(2):
            h = 2 * pair + half
            kv = h // Q_PER_KV
            qm = jnp.where(low if half == 0 else jnp.logical_not(low), qp, 0.0).astype(BF16)
            s = _dot_nt(qm, keys[(kv, half)]) + bias_ref[0, h]
            m = jnp.maximum(jnp.max(s, -1, keepdims=True), sink_ref[h])
            pv = _dot(jnp.exp(s - m).astype(BF16), values[kv])
            outs.append(pv / (pv[:, HEAD_DIM:HEAD_DIM + 1] + jnp.exp(sink_ref[h] - m)))
        o_ref[:, pair * 2 * HEAD_DIM:(pair + 1) * 2 * HEAD_DIM] = jnp.where(low, outs[0],
                                                                            pltpu.roll(outs[1], HEAD_DIM, 1))

    @pl.when(pl.program_id(1) == pl.num_programs(1) - 1)
    def _():
        kl_ref[0] = kc_ref[...]
        vl_ref[0] = vc_ref[...]


def _swa_bias_table():
    r = np.arange(ATT_BLOCK)[:, None]
    c = np.arange(2 * ATT_BLOCK)[None, :]
    dist = r - c + ATT_BLOCK
    slopes = 2.0 ** -(np.arange(N_HEADS) + 1.0)
    table = np.empty((3, N_HEADS, ATT_BLOCK, 2 * ATT_BLOCK), np.float32)
    for j in range(3):
        seen = (dist >= 0) & (dist < WINDOW) & ((j - 1) * ATT_BLOCK + c - FRONT_PAD >= 0)
        table[j] = np.where(seen[None], -slopes[:, None, None] * dist[None], NEG)
    return table


def _swa_prompt(sink, q, k, v, batch, lp):
    nb = lp // ATT_BLOCK
    assert nb >= 3
    n = batch * lp
    bias = jnp.asarray(_swa_bias_table())
    cur = lambda w: pl.BlockSpec((ATT_BLOCK, w), lambda b, j: (b * nb + j, 0))
    prev = lambda w: pl.BlockSpec((ATT_BLOCK, w), lambda b, j: (b * nb + jnp.maximum(j - 1, 0), 0))
    last = pl.BlockSpec((1, ATT_BLOCK, KV_W), lambda b, j: (b, 0, 0))
    return pl.pallas_call(
        _swa_prompt_kernel,
        grid=(batch, nb),
        in_specs=[pl.BlockSpec(memory_space=pltpu.SMEM),
                  pl.BlockSpec((1,) + bias.shape[1:], lambda b, j: (jnp.minimum(j, 2), 0, 0, 0)),
                  cur(Q_W), prev(KV_W), cur(KV_W), prev(KV_W), cur(KV_W)],
        out_specs=[cur(Q_W), last, last],
        out_shape=[jax.ShapeDtypeStruct((n, Q_W), F32), jax.ShapeDtypeStruct((batch, ATT_BLOCK, KV_W), F32),
                   jax.ShapeDtypeStruct((batch, ATT_BLOCK, KV_W), F32)],
        compiler_params=pltpu.CompilerParams(dimension_semantics=("parallel", "arbitrary")),
        name="swa_prompt",
    )(sink, bias, q, k, k, v, v)


def _swa_sample_kernel(sink_ref, q_ref, k_ref, v_ref, bk_ref, bv_ref, o_ref, nk_ref, nv_ref, *, t_s):
    nbuf = WINDOW
    span = 2 * WINDOW
    rows = Q_PER_KV * t_s
    r = lax.broadcasted_iota(jnp.int32, (rows, span), 0)
    c = lax.broadcasted_iota(jnp.int32, (rows, span), 1)
    t = r % t_s
    dist = t + nbuf - c
    mask = (dist >= 0) & (dist < WINDOW) & (c < nbuf + t_s)
    distf = dist.astype(F32)
    g_col = lax.broadcasted_iota(jnp.int32, (rows, 1), 0) // t_s
    fill = jnp.zeros((span - nbuf - t_s, KV_W), F32)

    def one_seq(s, carry):
        rs = pl.ds(pl.multiple_of(s * t_s, t_s), t_s)
        q = q_ref[rs, :]
        k_new = k_ref[rs, :]
        v_new = v_ref[rs, :]
        bk = bk_ref[s]
        bv = bv_ref[s]
        k_all = jnp.concatenate([bk, k_new, fill], 0)
        v_all = jnp.concatenate([bv, v_new, fill], 0)
        for kv in range(N_KV):
            qg = jnp.concatenate(
                [q[:, (kv * Q_PER_KV + g) * HEAD_DIM:(kv * Q_PER_KV + g + 1) * HEAD_DIM] for g in range(Q_PER_KV)], 0)
            kk = k_all[:, kv * HEAD_DIM:(kv + 1) * HEAD_DIM].astype(BF16)
            vv = v_all[:, kv * HEAD_DIM:(kv + 1) * HEAD_DIM].astype(BF16)
            slope = jnp.zeros((rows, 1), F32)
            sink = jnp.zeros((rows, 1), F32)
            for g in range(Q_PER_KV):
                h = kv * Q_PER_KV + g
                slope = jnp.where(g_col == g, 2.0 ** -(h + 1), slope)
                sink = jnp.where(g_col == g, sink_ref[h], sink)
            sc = _dot_nt(qg.astype(BF16), kk) * (HEAD_DIM ** -0.5) - slope * distf
            sc = jnp.where(mask, sc, NEG)
            o = _softmax_pv(sc, sink, vv)
            for g in range(Q_PER_KV):
                h = kv * Q_PER_KV + g
                o_ref[rs, h * HEAD_DIM:(h + 1) * HEAD_DIM] = o[g * t_s:(g + 1) * t_s]
        nk_ref[s, 0:nbuf - t_s, :] = bk[t_s:, :]
        nk_ref[s, nbuf - t_s:nbuf, :] = k_new
        nv_ref[s, 0:nbuf - t_s, :] = bv[t_s:, :]
        nv_ref[s, nbuf - t_s:nbuf, :] = v_new
        return carry

    lax.fori_loop(0, SAMPLE_SEQS, one_seq, 0, unroll=True)


def _swa_sample(sink, q, k, v, buf_k, buf_v, row0, t_s):
    n_seq = buf_k.shape[0]
    sb = SAMPLE_SEQS
    rb = sb * t_s
    b0 = row0 // rb
    rows = lambda w: pl.BlockSpec((rb, w), lambda i: (b0 + i, 0))
    bufs = pl.BlockSpec((sb, WINDOW, KV_W), lambda i: (i, 0, 0))
    return pl.pallas_call(
        functools.partial(_swa_sample_kernel, t_s=t_s),
        grid=(n_seq // sb,),
        in_specs=[pl.BlockSpec(memory_space=pltpu.SMEM), rows(Q_W), rows(KV_W), rows(KV_W), bufs, bufs],
        out_specs=[pl.BlockSpec((rb, Q_W), lambda i: (i, 0)), bufs, bufs],
        out_shape=[jax.ShapeDtypeStruct((n_seq * t_s, Q_W), F32),
                   jax.ShapeDtypeStruct(buf_k.shape, F32), jax.ShapeDtypeStruct(buf_v.shape, F32)],
        compiler_params=pltpu.CompilerParams(dimension_semantics=("parallel",)),
        name="swa_sample",
    )(sink, q, k, v, buf_k, buf_v)


def _gla_tables(chunk):
    t = np.arange(chunk)[:, None]
    u = np.arange(chunk)[None, :]
    masks = []
    w = chunk // 2
    while w >= 1:
        masks.append((t // (2 * w) == u // (2 * w)) & ((t // w) % 2 == 1) & ((u // w) % 2 == 0))
        w //= 2
    return (u <= t).astype(np.float32), np.stack(masks, 0).astype(np.float32)


def _level_exponents(b, la, w):
    C = b.shape[0]
    row = lax.broadcasted_iota(jnp.int32, b.shape, 0)
    if w >= 4:
        pieces = [jnp.broadcast_to(b[p + w - 1:p + w], (2 * w, b.shape[1])) for p in range(0, C, 2 * w)]
        ref = pieces[0] if len(pieces) == 1 else jnp.concatenate(pieces, 0)
        return jnp.where((row & w) != 0, b - ref, ref - b)
    if w == 2:
        m = row & 3
        nxt = pltpu.roll(la, C - 1, 0)
        prv = pltpu.roll(la, 1, 0)
        return jnp.where(m == 2, la, jnp.where(m == 3, la + prv, jnp.where(m == 0, nxt, 0.0)))
    return jnp.where((row & 1) != 0, la, 0.0)


def _split3(x):
    hi = x.astype(BF16)
    r1 = x - hi.astype(F32)
    mid = r1.astype(BF16)
    lo = (r1 - mid.astype(F32)).astype(BF16)
    return hi, mid, lo


def _gla_intra_kernel(g_ref, m_ref, q_ref, k_ref, v_ref, la_ref, o_ref, qe_ref, ke_ref, vt_ref, d_ref):
    C = GLA_CHUNK
    n_lvl = m_ref.shape[0]
    G = g_ref[...]
    eye = (lax.broadcasted_iota(jnp.int32, (C, C), 0) == lax.broadcasted_iota(jnp.int32, (C, C), 1)).astype(F32)
    for j in range(q_ref.shape[0] // C):
        rs = slice(j * C, (j + 1) * C)
        la = la_ref[rs, :]
        hi, mid, lo = _split3(la)
        b = _dot(G, hi) + _dot(G, mid) + _dot(G, lo)
        b_last = b[C - 1:C]
        q_all = q_ref[rs, :]
        k_all = k_ref[rs, :]
        qe_ref[0, rs, :] = (q_all * jnp.exp(b)).astype(BF16)
        ke_ref[0, rs, :] = (k_all * jnp.exp(b_last - b)).astype(BF16)
        d_ref[0, j] = jnp.broadcast_to(jnp.exp(b_last), (8, GLA_DK))
        q_lvl, k_lvl = [], []
        for l in range(n_lvl):
            El = jnp.exp(_level_exponents(b, la, C >> (l + 1)))
            q_lvl.append((q_all * El).astype(BF16))
            k_lvl.append((k_all * El).astype(BF16))
        for h in range(GLA_HEADS):
            ks = slice(h * GLA_HK, (h + 1) * GLA_HK)
            vs = slice(h * GLA_HV, (h + 1) * GLA_HV)
            v = v_ref[rs, vs]
            att = eye * jnp.sum(q_all[:, ks] * k_all[:, ks], -1, keepdims=True)
            for l in range(n_lvl):
                att = att + m_ref[l] * _dot_nt(q_lvl[l][:, ks], k_lvl[l][:, ks])
            o_ref[0, rs, vs] = _dot(att.astype(BF16), v.astype(BF16))
            vt_ref[0, j, vs, :] = v.T.astype(BF16)


def _gla_inter_kernel(o_ref, qe_ref, ke_ref, vt_ref, d_ref, og_ref, s_ref, st_ref):
    c = pl.program_id(0)
    batch = o_ref.shape[0]

    @pl.when(c == 0)
    def _():
        st_ref[...] = jnp.zeros_like(st_ref)

    for b in range(batch):
        for h in range(GLA_HEADS):
            ks = slice(h * GLA_HK, (h + 1) * GLA_HK)
            vs = slice(h * GLA_HV, (h + 1) * GLA_HV)
            st = st_ref[b * GLA_HEADS + h]
            og_ref[b, :, vs] = o_ref[b, :, vs] + _dot_nt(qe_ref[b, :, ks], st.astype(BF16))
            st_ref[b * GLA_HEADS + h] = st * d_ref[b, 0, 0:1, ks] + _dot(vt_ref[b, 0, vs, :], ke_ref[b, :, ks])

    @pl.when(c == pl.num_programs(0) - 1)
    def _():
        for b in range(batch):
            for h in range(GLA_HEADS):
                s_ref[b, h] = st_ref[b * GLA_HEADS + h].T


def _gla_prompt(gq, gk, gv, la, batch, lp):
    C = GLA_CHUNK
    nc = lp // C
    G, M = _gla_tables(C)
    G = jnp.asarray(G, BF16)
    M = jnp.asarray(M, F32)
    cps = GLA_INTRA_CHUNKS
    assert nc % cps == 0
    rows = lambda w: pl.BlockSpec((cps * C, w), lambda b, c: (b * (nc // cps) + c, 0))
    rows3 = lambda w: pl.BlockSpec((1, cps * C, w), lambda b, c: (b, c, 0))
    full = lambda a: pl.BlockSpec(a.shape, lambda b, c: (0,) * a.ndim)
    o_intra, qe, ke, vt, d = pl.pallas_call(
        _gla_intra_kernel,
        grid=(batch, nc // cps),
        in_specs=[full(G), full(M), rows(GLA_DK), rows(GLA_DK), rows(GLA_DV), rows(GLA_DK)],
        out_specs=[rows3(GLA_DV), rows3(GLA_DK), rows3(GLA_DK),
                   pl.BlockSpec((1, cps, GLA_DV, C), lambda b, c: (b, c, 0, 0)),
                   pl.BlockSpec((1, cps, 8, GLA_DK), lambda b, c: (b, c, 0, 0))],
        out_shape=[jax.ShapeDtypeStruct((batch, lp, GLA_DV), F32),
                   jax.ShapeDtypeStruct((batch, lp, GLA_DK), BF16), jax.ShapeDtypeStruct((batch, lp, GLA_DK), BF16),
                   jax.ShapeDtypeStruct((batch, nc, GLA_DV, C), BF16),
                   jax.ShapeDtypeStruct((batch, nc, 8, GLA_DK), F32)],
        compiler_params=pltpu.CompilerParams(dimension_semantics=("parallel", "parallel")),
        name="gla_intra",
    )(G, M, gq, gk, gv, la)
    chunk = lambda w: pl.BlockSpec((batch, C, w), lambda c: (0, c, 0))
    og, s_fin = pl.pallas_call(
        _gla_inter_kernel,
        grid=(nc,),
        in_specs=[chunk(GLA_DV), chunk(GLA_DK), chunk(GLA_DK),
                  pl.BlockSpec((batch, 1, GLA_DV, C), lambda c: (0, c, 0, 0)),
                  pl.BlockSpec((batch, 1, 8, GLA_DK), lambda c: (0, c, 0, 0))],
        out_specs=[chunk(GLA_DV), pl.BlockSpec((batch, GLA_HEADS, GLA_HK, GLA_HV), lambda c: (0, 0, 0, 0))],
        out_shape=[jax.ShapeDtypeStruct((batch, lp, GLA_DV), F32),
                   jax.ShapeDtypeStruct((batch, GLA_HEADS, GLA_HK, GLA_HV), F32)],
        scratch_shapes=[pltpu.VMEM((batch * GLA_HEADS, GLA_HV, GLA_HK), F32)],
        compiler_params=pltpu.CompilerParams(dimension_semantics=("arbitrary",)),
        name="gla_inter",
    )(o_intra, qe, ke, vt, d)
    return og.reshape(batch * lp, GLA_DV), s_fin


def _gla_sample_kernel(q_ref, k_ref, v_ref, la_ref, s0_ref, o_ref, s_ref, *, t_s):
    T = t_s
    row = lax.broadcasted_iota(jnp.int32, (T, GLA_HK), 0)
    k_fill = jnp.zeros((GLA_HK - T - 8, GLA_HK), F32)
    v_fill = jnp.zeros((GLA_HK - T, GLA_HV), F32)

    def one_seq(s, carry):
        rs = pl.ds(pl.multiple_of(s * T, T), T)
        for h in range(GLA_HEADS):
            ks = slice(h * GLA_HK, (h + 1) * GLA_HK)
            vs = slice(h * GLA_HV, (h + 1) * GLA_HV)
            q = q_ref[rs, ks]
            k = k_ref[rs, ks]
            v = v_ref[rs, vs]
            b = la_ref[rs, ks]
            sh = 1
            while sh < T:
                b = b + jnp.where(row >= sh, pltpu.roll(b, sh, 0), 0.0)
                sh *= 2
            S = s0_ref[s, h]
            o = _dot((q * jnp.exp(b)).astype(BF16), S.astype(BF16))
            for j in range(T):
                e = jnp.exp(jnp.where(row >= j, b - b[j:j + 1], NEG))
                a_col = jnp.sum(q * k[j:j + 1] * e, -1, keepdims=True)
                o = o + a_col * v[j:j + 1]
            o_ref[rs, vs] = o
            b_last = b[T - 1:T]
            ke = k * jnp.exp(b_last - b)
            kt = jnp.concatenate([ke, jnp.broadcast_to(jnp.exp(b_last), (8, GLA_HK)), k_fill], 0).T
            v_pad = jnp.concatenate([v, v_fill], 0)
            s_ref[s, h] = S * kt[:, T:T + 1] + _dot(kt.astype(BF16), v_pad.astype(BF16))
        return carry

    lax.fori_loop(0, SAMPLE_SEQS, one_seq, 0)


def _gla_sample(gq, gk, gv, la, s0, row0, t_s):
    n_seq = s0.shape[0]
    sb = SAMPLE_SEQS
    rb = sb * t_s
    b0 = row0 // rb
    rows = lambda w: pl.BlockSpec((rb, w), lambda i: (b0 + i, 0))
    st = pl.BlockSpec((sb, GLA_HEADS, GLA_HK, GLA_HV), lambda i: (i, 0, 0, 0))
    return pl.pallas_call(
        functools.partial(_gla_sample_kernel, t_s=t_s),
        grid=(n_seq // sb,),
        in_specs=[rows(GLA_DK), rows(GLA_DK), rows(GLA_DV), rows(GLA_DK), st],
        out_specs=[pl.BlockSpec((rb, GLA_DV), lambda i: (i, 0)), st],
        out_shape=[jax.ShapeDtypeStruct((n_seq * t_s, GLA_DV), F32), jax.ShapeDtypeStruct(s0.shape, F32)],
        compiler_params=pltpu.CompilerParams(dimension_semantics=("parallel",), vmem_limit_bytes=VMEM_LIMIT),
        name="gla_sample",
    )(gq, gk, gv, la, s0)


def _route(lt, valid):
    tm = lt.shape[1]
    el = lt[0:N_EXPERTS]
    gl = lt[N_EXPERTS:N_EXPERTS + N_GROUPS]
    g_max = jnp.max(gl, 0, keepdims=True)
    g_row = lax.broadcasted_iota(jnp.int32, (N_GROUPS, tm), 0)
    g_idx = jnp.min(jnp.where(gl == g_max, g_row, N_GROUPS), 0, keepdims=True)
    p_max = 1.0 / jnp.sum(jnp.exp(gl - g_max), 0, keepdims=True)
    e_row = lax.broadcasted_iota(jnp.int32, (N_EXPERTS, tm), 0)
    m1 = jnp.where(e_row // EXP_PER_GROUP == g_idx, el, -jnp.inf)
    v1 = jnp.max(m1, 0, keepdims=True)
    i1 = jnp.min(jnp.where(m1 == v1, e_row, N_EXPERTS), 0, keepdims=True)
    m2 = jnp.where(e_row == i1, -jnp.inf, m1)
    v2 = jnp.max(m2, 0, keepdims=True)
    i2 = jnp.min(jnp.where(m2 == v2, e_row, N_EXPERTS), 0, keepdims=True)
    e2 = jnp.exp(v2 - v1)
    w1 = p_max / (1.0 + e2)
    w2 = p_max * e2 / (1.0 + e2)
    o_row = lax.broadcasted_iota(jnp.int32, (8, tm), 0)
    ids = jnp.where(o_row == 0, i1, jnp.where(o_row == 1, i2, -1))
    return jnp.where(valid, ids, -1), jnp.where(o_row == 0, w1, jnp.where(o_row == 1, w2, 0.0))


def _store_row_tiles(ref, x):
    t, d = x.shape
    n = d // LANES
    for s in range(n):
        ref[pl.ds(s, t, stride=n), :] = x[:, s * LANES:(s + 1) * LANES]


def _load_row_tiles(ref, t, n=ROW_CHUNKS):
    return jnp.concatenate([ref[pl.ds(s, t, stride=n), :] for s in range(n)], axis=1)


def _merge_kernel(h_ref, yap_ref, yas_ref, ogp_ref, ogs_ref, gr_ref, gate_ref, valid_ref, ng_ref,
                  wa_ref, wg_ref, wo_ref, g1_ref, b1_ref, wrh_ref, wrl_ref, br_ref, u_ref,
                  h1t_ref, ids_ref, wts_ref, cnt_ref, run_ref, *, prompt_tiles):
    @pl.when(pl.program_id(0) == 0)
    def _():
        run_ref[...] = jnp.zeros_like(run_ref)

    h = h_ref[...]
    is_prompt = pl.program_id(0) < prompt_tiles
    og = jnp.where(is_prompt, ogp_ref[...], ogs_ref[...])
    ya = jnp.where(is_prompt, yap_ref[...], yas_ref[...])
    parts = []
    for hh in range(GLA_HEADS):
        o = og[:, hh * GLA_HV:(hh + 1) * GLA_HV]
        parts.append(o * lax.rsqrt(jnp.mean(o * o, -1, keepdims=True) + EPS))
    y_gla = jnp.concatenate(parts, 1) * ng_ref[...] * gr_ref[...].astype(F32)
    a = _dot(ya.astype(BF16), wa_ref[...])
    b = _dot(y_gla.astype(BF16), wg_ref[...])
    hm = gate_ref[:, :D_MODEL].astype(F32) * a + gate_ref[:, D_MODEL:].astype(F32) * b
    mix = _dot(hm.astype(BF16), wo_ref[...])
    h1 = _ln(DN_ALPHA * h + mix, g1_ref[...], b1_ref[...])
    _store_row_tiles(h1t_ref, h1)
    h_hi = h1.astype(BF16)
    h_lo = (h1 - h_hi.astype(F32)).astype(BF16)
    logits = _dot(h_hi, wrh_ref[...]) + _dot(h_lo, wrh_ref[...]) + _dot(h_hi, wrl_ref[...]) + br_ref[...]
    ids, wts_ref[...] = _route(logits.T, valid_ref[...] > 0.0)
    tm = ids.shape[1]
    e_row = lax.broadcasted_iota(jnp.int32, (N_EXPERTS, tm), 0)
    run = run_ref[:, 0:1]
    ranks = []
    for kk in range(2):
        onehot = (e_row == ids[kk:kk + 1]).astype(F32)
        before = _dot(onehot.astype(BF16), u_ref[...])
        ranks.append(jnp.sum(onehot * (run + before), 0, keepdims=True).astype(jnp.int32))
        run = run + jnp.sum(onehot, 1, keepdims=True)
    run_ref[...] = jnp.broadcast_to(run, run_ref.shape)
    o_row = lax.broadcasted_iota(jnp.int32, (8, tm), 0)
    ids_ref[...] = jnp.where(o_row == 2, ranks[0], jnp.where(o_row == 3, ranks[1], ids))
    cnt_ref[...] = run_ref[...].astype(jnp.int32)


def _merge(h, ya_p, ya_s, og_p, og_s, gr, gate, valid, ng, wa, wg, wo, g1, b1, wrh, wrl, br):
    n = h.shape[0]
    tm = ROW_TILE
    u = jnp.asarray(np.triu(np.ones((tm, tm), np.float32), 1), BF16)
    pt = ya_p.shape[0] // tm
    st = ya_s.shape[0] // tm
    row = lambda w: pl.BlockSpec((tm, w), lambda i: (i, 0))
    row_p = lambda w: pl.BlockSpec((tm, w), lambda i: (jnp.minimum(i, pt - 1), 0))
    row_s = lambda w: pl.BlockSpec((tm, w), lambda i: (jnp.clip(i - pt, 0, st - 1), 0))
    lane = lambda r: pl.BlockSpec((r, tm), lambda i: (0, i))
    full = lambda a: pl.BlockSpec(a.shape, lambda i: (0,) * a.ndim)
    return pl.pallas_call(
        functools.partial(_merge_kernel, prompt_tiles=pt),
        grid=(n // tm,),
        in_specs=[row(D_MODEL), row_p(Q_W), row_s(Q_W), row_p(GLA_DV), row_s(GLA_DV), row(GLA_DV), row(2 * D_MODEL),
                  lane(1), full(ng), full(wa), full(wg), full(wo), full(g1), full(b1),
                  full(wrh), full(wrl), full(br), full(u)],
        out_specs=[pl.BlockSpec((tm * ROW_CHUNKS, LANES), lambda i: (i, 0)), lane(8), lane(8),
                   pl.BlockSpec((N_EXPERTS, LANES), lambda i: (0, 0))],
        out_shape=[jax.ShapeDtypeStruct((n * ROW_CHUNKS, LANES), F32),
                   jax.ShapeDtypeStruct((8, n), jnp.int32), jax.ShapeDtypeStruct((8, n), F32),
                   jax.ShapeDtypeStruct((N_EXPERTS, LANES), jnp.int32)],
        scratch_shapes=[pltpu.VMEM((N_EXPERTS, LANES), F32)],
        compiler_params=pltpu.CompilerParams(dimension_semantics=("arbitrary",), vmem_limit_bytes=VMEM_LIMIT),
        name="merge",
    )(h, ya_p, ya_s, og_p, og_s, gr, gate, valid, ng, wa, wg, wo, g1, b1, wrh, wrl, br, u)


def _gather_row_tiles(idx_ref, idx0, src_hbm, dst, sem, n):
    def body(r, carry):
        t = idx_ref[idx0 + r]
        pltpu.make_async_copy(src_hbm.at[pl.ds(pl.multiple_of(t * ROW_CHUNKS, ROW_CHUNKS), ROW_CHUNKS), :],
                              dst.at[pl.ds(pl.multiple_of(r * ROW_CHUNKS, ROW_CHUNKS), ROW_CHUNKS), :], sem).start()
        return carry
    lax.fori_loop(0, n, body, 0, unroll=8)


def _wait_row_tiles(src_hbm, dst, sem, n):
    pltpu.make_async_copy(src_hbm.at[pl.ds(0, n * ROW_CHUNKS), :], dst, sem).wait()


def _tiles(ref, first, n=1):
    return ref.at[pl.ds(pl.multiple_of(first * ROW_CHUNKS, ROW_CHUNKS), n * ROW_CHUNKS), :]


def _dispatch_kernel(dest_ref, start_ref, cnt_ref, nt_ref, h_hbm, xs_hbm, buf, zbuf, sem_in, sem_out, sem_z,
                     *, ranges, n_rows, max_tiles):
    CH = DISPATCH_CHUNK
    T = FFN_TILE
    c = pl.program_id(0)
    n_chunks = pl.num_programs(0)

    def load(row0, slot):
        return pltpu.make_async_copy(_tiles(h_hbm, row0, CH), buf.at[slot], sem_in.at[slot])

    def scatter(row0, slot):
        def body(r, carry):
            for kk in range(2):
                d = dest_ref[kk * n_rows + row0 + r]
                pltpu.make_async_copy(_tiles(buf.at[slot], r), _tiles(xs_hbm, d), sem_out.at[slot]).start()
            return carry
        lax.fori_loop(0, CH, body, 0, unroll=8)

    def drain(slot):
        for _ in range(2):
            pltpu.make_async_copy(buf.at[slot], _tiles(xs_hbm, 0, CH), sem_out.at[slot]).wait()

    groups, per_group, first, stride, extra, extra_first = ranges

    def row0(j):
        in_group = first + (j // per_group) * stride + (j % per_group) * CH
        return jnp.where(j < groups * per_group, in_group, extra_first + (j - groups * per_group) * CH)

    @pl.when(c == 0)
    def _():
        for j in range(2):
            load(row0(j), j).start()

    slot = c % 3
    load(row0(c), slot).wait()
    scatter(row0(c), slot)

    @pl.when(c > 0)
    def _():
        drain((c + 2) % 3)

    @pl.when(c + 2 < n_chunks)
    def _():
        load(row0(c + 2), (c + 2) % 3).start()

    @pl.when(c == n_chunks - 1)
    def _():
        drain(slot)
        _zero_unowned_slots(start_ref, cnt_ref, nt_ref, xs_hbm, zbuf, sem_z, max_tiles)


def _zero_unowned_slots(start_ref, cnt_ref, nt_ref, xs_hbm, zbuf, sem_z, max_tiles):
    T = FFN_TILE
    zbuf[...] = jnp.zeros_like(zbuf)

    def tail_copies(e, wait):
        cnt = cnt_ref[e]
        n = (T - (cnt & (T - 1))) & (T - 1)
        first = start_ref[e] + cnt
        for bit in reversed(range(T.bit_length() - 1)):
            size = 1 << bit

            @pl.when((n & size) != 0)
            def _():
                cp = pltpu.make_async_copy(_tiles(zbuf, 0, size),
                                           _tiles(xs_hbm, first + ((n >> (bit + 1)) << (bit + 1)), size), sem_z)
                cp.wait() if wait else cp.start()

    def unused_tile(t, wait):
        cp = pltpu.make_async_copy(zbuf, _tiles(xs_hbm, t * T, T), sem_z)
        cp.wait() if wait else cp.start()

    for wait in (False, True):
        def per_expert(e, carry, wait=wait):
            tail_copies(e, wait)
            return carry

        def per_tile(t, carry, wait=wait):
            unused_tile(t, wait)
            return carry
        lax.fori_loop(0, N_EXPERTS, per_expert, 0)
        lax.fori_loop(nt_ref[0], max_tiles, per_tile, 0)


def _dispatch(dest, start, counts, n_tiles, h1t, ranges, max_tiles):
    T = FFN_TILE
    n_rows = h1t.shape[0] // ROW_CHUNKS
    return pl.pallas_call(
        functools.partial(_dispatch_kernel, ranges=ranges, n_rows=n_rows, max_tiles=max_tiles),
        grid_spec=pltpu.PrefetchScalarGridSpec(
            num_scalar_prefetch=4,
            grid=(ranges[0] * ranges[1] + ranges[4],),
            in_specs=[pl.BlockSpec(memory_space=pl.ANY)],
            out_specs=pl.BlockSpec(memory_space=pl.ANY),
            scratch_shapes=[pltpu.VMEM((3, DISPATCH_CHUNK * ROW_CHUNKS, LANES), F32),
                            pltpu.VMEM((T * ROW_CHUNKS, LANES), F32),
                            pltpu.SemaphoreType.DMA((3,)), pltpu.SemaphoreType.DMA((3,)), pltpu.SemaphoreType.DMA]),
        out_shape=jax.ShapeDtypeStruct((max_tiles * T * ROW_CHUNKS, LANES), F32),
        compiler_params=pltpu.CompilerParams(dimension_semantics=("arbitrary",)),
        name="dispatch",
    )(dest, start, counts, n_tiles, h1t)


def _ffn_kernel(te_ref, nxt_ref, nt_ref, x_ref, wg_hbm, wu_hbm, wd_hbm, out_ref, stage_g, stage_u, stage_d, sem,
                wgb, wub, wdb):
    i = pl.program_id(0)
    nt = nt_ref[0]
    T = FFN_TILE

    def stage(e):
        return [pltpu.make_async_copy(src.at[e], dst, sem.at[n])
                for n, (src, dst) in enumerate(((wg_hbm, stage_g), (wu_hbm, stage_u), (wd_hbm, stage_d)))]

    @pl.when(i == 0)
    def _():
        for cp in stage(te_ref[0]):
            cp.start()

    @pl.when(i < nt)
    def _():
        @pl.when((i == 0) | (te_ref[i] != te_ref[jnp.maximum(i - 1, 0)]))
        def _():
            for cp in stage(te_ref[i]):
                cp.wait()
            wgb[...] = stage_g[...].astype(BF16)
            wub[...] = stage_u[...].astype(BF16)
            wdb[...] = stage_d[...].astype(BF16)

            @pl.when(nxt_ref[i] >= 0)
            def _():
                for cp in stage(nxt_ref[i]):
                    cp.start()

        x = _load_row_tiles(x_ref, T).astype(BF16)
        g = _dot(x, wgb[...])
        u = _dot(x, wub[...])
        _store_row_tiles(out_ref, _dot((g * _sigmoid(g) * u).astype(BF16), wdb[...]))

    @pl.when(i >= nt)
    def _():
        out_ref[...] = jnp.zeros_like(out_ref)


def _ffn(tile_expert, next_expert, n_tiles, xs, w_g, w_u, w_d):
    T = FFN_TILE
    max_tiles = tile_expert.shape[0]
    hbm = pl.BlockSpec(memory_space=pl.ANY)
    tile = lambda imap: pl.BlockSpec((T * ROW_CHUNKS, LANES), imap)
    return pl.pallas_call(
        _ffn_kernel,
        grid_spec=pltpu.PrefetchScalarGridSpec(
            num_scalar_prefetch=3,
            grid=(max_tiles,),
            in_specs=[tile(lambda i, te, nxt, nt: (jnp.minimum(i, nt[0] - 1), 0)), hbm, hbm, hbm],
            out_specs=tile(lambda i, te, nxt, nt: (i, 0)),
            scratch_shapes=[pltpu.VMEM(w_g.shape[1:], F32), pltpu.VMEM(w_u.shape[1:], F32),
                            pltpu.VMEM(w_d.shape[1:], F32), pltpu.SemaphoreType.DMA((3,)),
                            pltpu.VMEM(w_g.shape[1:], BF16), pltpu.VMEM(w_u.shape[1:], BF16),
                            pltpu.VMEM(w_d.shape[1:], BF16)]),
        out_shape=jax.ShapeDtypeStruct(xs.shape, F32),
        compiler_params=pltpu.CompilerParams(dimension_semantics=("arbitrary",)),
        name="ffn",
    )(tile_expert, next_expert, n_tiles, xs, w_g, w_u, w_d)


def _combine_kernel(dest_ref, h_ref, w_ref, g_ref, b_ref, ys_hbm, y_ref, buf, sem, *, blk, n_rows):
    i = pl.program_id(0)
    n = pl.num_programs(0)
    T = CMB_TILE

    def gather(step, slot):
        for kk in range(2):
            _gather_row_tiles(dest_ref, kk * n_rows + blk(step) * T, ys_hbm, buf.at[slot, kk], sem.at[slot], T)

    @pl.when(i == 0)
    def _():
        gather(0, 0)

    @pl.when(i + 1 < n)
    def _():
        gather(i + 1, (i + 1) % 2)

    slot = i % 2
    for kk in range(2):
        _wait_row_tiles(ys_hbm, buf.at[slot, kk], sem.at[slot], T)
    w = w_ref[...]
    ff = w[:, 0:1] * _load_row_tiles(buf.at[slot, 0], T) + w[:, 1:2] * _load_row_tiles(buf.at[slot, 1], T)
    y_ref[...] = _ln(DN_ALPHA * _load_row_tiles(h_ref, T) + ff, g_ref[...], b_ref[...])


def _combine(dest, h1t, wts_t, ys, g2, b2, n_out, first_block, blocks_per_batch, skip_blocks):
    T = CMB_TILE
    n_rows = h1t.shape[0] // ROW_CHUNKS
    if skip_blocks:
        blk = lambda i: first_block + i + (i // blocks_per_batch + 1) * skip_blocks
    else:
        blk = lambda i: first_block + i
    full = lambda a: pl.BlockSpec(a.shape, lambda i, d: (0,) * a.ndim)
    return pl.pallas_call(
        functools.partial(_combine_kernel, blk=blk, n_rows=n_rows),
        grid_spec=pltpu.PrefetchScalarGridSpec(
            num_scalar_prefetch=1,
            grid=(n_out // T,),
            in_specs=[pl.BlockSpec((T * ROW_CHUNKS, LANES), lambda i, d: (blk(i), 0)),
                      pl.BlockSpec((T, 2), lambda i, d: (blk(i), 0)),
                      full(g2), full(b2), pl.BlockSpec(memory_space=pl.ANY)],
            out_specs=pl.BlockSpec((T, D_MODEL), lambda i, d: (i, 0)),
            scratch_shapes=[pltpu.VMEM((2, 2, T * ROW_CHUNKS, LANES), F32), pltpu.SemaphoreType.DMA((2,))]),
        out_shape=jax.ShapeDtypeStruct((n_out, D_MODEL), F32),
        compiler_params=pltpu.CompilerParams(dimension_semantics=("arbitrary",)),
        name="combine",
    )(dest, h1t, wts_t, g2, b2, ys)


def _dispatch_plan(routing, counts, max_tiles):
    T = FFN_TILE
    tiles_e = (counts + T - 1) // T
    tile_end = jnp.cumsum(tiles_e)
    n_tiles = tile_end[-1]
    start = (tile_end - tiles_e) * T
    ids, rank = routing[0:2], routing[2:4]
    onehot = (ids[..., None] == jnp.arange(N_EXPERTS, dtype=jnp.int32)).astype(jnp.int32)
    dest = (jnp.sum(onehot * start, axis=-1) + jnp.where(ids >= 0, rank, 0)).reshape(-1).astype(jnp.int32)
    experts = jnp.arange(N_EXPERTS, dtype=jnp.int32)
    tiles = jnp.arange(max_tiles, dtype=jnp.int32)
    te = jnp.sum((tiles[:, None] >= tile_end[None, :]).astype(jnp.int32), axis=1)
    te_last = jnp.max(jnp.where(counts > 0, experts, 0))
    te = jnp.where(tiles < n_tiles, te, te_last).astype(jnp.int32)
    later = jnp.where((counts > 0)[None, :] & (experts[None, :] > te[:, None]), experts[None, :], N_EXPERTS)
    nxt = jnp.min(later, axis=1)
    nxt = jnp.where(nxt < N_EXPERTS, nxt, -1).astype(jnp.int32)
    return te, nxt, n_tiles.reshape(1).astype(jnp.int32), start.astype(jnp.int32), dest


def kernel(x_prompt, x_sample, state_swa_k, state_swa_v, state_gla, meta_tokens, ln_emb_g, ln_emb_b, w_in, b_gate, attn_sink, w_alpha2, b_alpha, gla_norm_g, w_attn_br, w_gla_br, w_out, ln1_g, ln1_b, w_router_group, b_router_group, w_router_expert, b_router_expert, w_exp_gate, w_exp_up, w_exp_down, ln2_g, ln2_b):
    B, seq, _ = x_prompt.shape
    n_seq, t_s, _ = x_sample.shape
    depth = w_in.shape[0]
    assert depth == 1 and seq % ATT_BLOCK == 0 and t_s == 8 and SKIP_ROWS == ATT_BLOCK == WINDOW
    lp = SKIP_ROWS + seq
    NP, NS = B * lp, n_seq * t_s
    NR = NP + NS
    assert NP % ROW_TILE == 0 and NS % ROW_TILE == 0 and n_seq % SAMPLE_SEQS == 0
    assert seq % DISPATCH_CHUNK == 0 and NS % DISPATCH_CHUNK == 0
    l = 0
    row2 = lambda a: a.reshape(1, -1)

    head = jnp.concatenate([jnp.zeros((FRONT_PAD, D_MODEL), F32), meta_tokens], axis=0)
    pos = np.arange(NR)
    moe_valid = jnp.asarray(~((pos < NP) & (pos % lp < SKIP_ROWS)), F32).reshape(1, NR)

    wi = w_in[l]
    c_ga = sum((Q_W, KV_W, KV_W, GLA_DK, GLA_DK, GLA_DV))
    assert c_ga == W_IN_SPLIT
    w_pieces = (wi[:, :c_ga].astype(BF16), wi[:, c_ga + GLA_RANK:].astype(BF16),
                jnp.pad(wi[:, c_ga:c_ga + GLA_RANK], ((0, 0), (0, 128 - GLA_RANK))).astype(BF16))
    wa2_bf = jnp.concatenate([w_alpha2[l], jnp.zeros((128 - GLA_RANK, GLA_DK), F32)], axis=0).astype(BF16)

    h, q, k, v, gq, gk, gv, la, gr, gate = _inproj(x_prompt, x_sample.reshape(NS, D_MODEL), head, row2(ln_emb_g),
                                                   row2(ln_emb_b), w_pieces, wa2_bf, row2(b_alpha[l]),
                                                   row2(b_gate[l]))

    sink = attn_sink[l]
    ya_p, k_p, v_p = _swa_prompt(sink, q, k, v, B, lp)
    buf_k = state_swa_k[l].reshape(n_seq, WINDOW, KV_W)
    buf_v = state_swa_v[l].reshape(n_seq, WINDOW, KV_W)
    ya_s, nk_s, nv_s = _swa_sample(sink, q, k, v, buf_k, buf_v, NP, t_s)

    og_p, s_p = _gla_prompt(gq, gk, gv, la, B, lp)
    og_s, s_s = _gla_sample(gq, gk, gv, la, state_gla[l], NP, t_s)

    wr = jnp.concatenate([w_router_expert[l], w_router_group[l],
                          jnp.zeros((D_MODEL, LANES - N_EXPERTS - N_GROUPS), F32)], axis=1)
    br = jnp.concatenate([b_router_expert[l], b_router_group[l],
                          jnp.zeros((LANES - N_EXPERTS - N_GROUPS,), F32)]).reshape(1, LANES)
    wr_hi = wr.astype(BF16)
    h1t, routing, wts, counts = _merge(h, ya_p, ya_s, og_p, og_s, gr, gate, moe_valid,
                                       row2(gla_norm_g[l]),
                                       w_attn_br[l].astype(BF16), w_gla_br[l].astype(BF16), w_out[l].astype(BF16),
                                       row2(ln1_g[l]), row2(ln1_b[l]), wr_hi, (wr - wr_hi.astype(F32)).astype(BF16), br)

    n_tok = B * seq + NS
    max_tiles = (2 * n_tok) // FFN_TILE + N_EXPERTS
    counts = counts[:, 0]
    te, nxt, n_tiles, start, dest = _dispatch_plan(routing[0:4], counts, max_tiles)
    routed = (B, seq // DISPATCH_CHUNK, SKIP_ROWS, lp, NS // DISPATCH_CHUNK, NP)
    xs = _dispatch(dest, start, counts, n_tiles, h1t, routed, max_tiles)
    ys = _ffn(te, nxt, n_tiles, xs, w_exp_gate[l], w_exp_up[l], w_exp_down[l])

    wts_t = wts[0:2].T
    g2, b2 = row2(ln2_g[l]), row2(ln2_b[l])
    skip_blocks = SKIP_ROWS // CMB_TILE
    y_p = _combine(dest, h1t, wts_t, ys, g2, b2, B * seq, 0, seq // CMB_TILE, skip_blocks)
    y_s = _combine(dest, h1t, wts_t, ys, g2, b2, NS, NP // CMB_TILE, 1, 0)

    kv_shape = (1, B, WINDOW, N_KV, HEAD_DIM)
    return (y_p.reshape(B, seq, D_MODEL), y_s.reshape(n_seq, t_s, D_MODEL),
            k_p.reshape(kv_shape), v_p.reshape(kv_shape), s_p[None],
            nk_s.reshape(1, n_seq, WINDOW, N_KV, HEAD_DIM), nv_s.reshape(1, n_seq, WINDOW, N_KV, HEAD_DIM),
            s_s[None])
```

```python
import functools

import numpy as np
import jax
import jax.numpy as jnp
from jax import lax
from jax.experimental import pallas as pl
from jax.experimental.pallas import tpu as pltpu

F32 = jnp.float32
BF16 = jnp.bfloat16

D_MODEL = 1024
N_META = 16
HEAD_DIM = 64
N_HEADS = 8
N_KV = 2
Q_PER_KV = 4
WINDOW = 128
ATT_BLOCK = 128
GLA_HEADS = 4
GLA_HK = 128
GLA_HV = 256
GLA_DK = GLA_HEADS * GLA_HK
GLA_DV = GLA_HEADS * GLA_HV
GLA_RANK = 16
GLA_TAU = 16.0
GLA_CHUNK = 64
N_GROUPS = 4
EXP_PER_GROUP = 8
N_EXPERTS = 32
D_EXPERT = 256
DN_ALPHA = 2.0 ** 0.25
EPS = 1e-5
NEG = -1e30

FRONT_PAD = (-N_META) % ATT_BLOCK
SKIP_ROWS = FRONT_PAD + N_META

Q_W, KV_W = N_HEADS * HEAD_DIM, N_KV * HEAD_DIM
SEG = {}
_o = 0
for _n, _w in (("q", Q_W), ("k", KV_W), ("v", KV_W), ("gq", GLA_DK), ("gk", GLA_DK), ("gv", GLA_DV)):
    SEG[_n] = (0, _o, _o + _w)
    _o += _w
W_IN_SPLIT = _o
SEG["gr"] = (1, 0, GLA_DV)
SEG["gate"] = (1, GLA_DV, GLA_DV + 2 * D_MODEL)
SEG["ga"] = (2, 0, 128)

ROW_TILE = 512
FFN_TILE = 256
CMB_TILE = 128
DISPATCH_CHUNK = 128
GLA_INTRA_CHUNKS = 3
INPROJ_TILE = 512
SAMPLE_SEQS = 8
LANES = 128
ROW_CHUNKS = D_MODEL // LANES
VMEM_LIMIT = 56 * 1024 * 1024


def _ln(x, g, b):
    mu = jnp.mean(x, -1, keepdims=True)
    xc = x - mu
    var = jnp.mean(xc * xc, -1, keepdims=True)
    return xc * lax.rsqrt(var + EPS) * g + b


def _sigmoid(x):
    return 0.5 * jnp.tanh(0.5 * x) + 0.5


def _dot(a, b):
    return jnp.dot(a, b, preferred_element_type=F32)


def _dot_nt(a, b):
    return lax.dot_general(a, b, (((1,), (1,)), ((), ())), preferred_element_type=F32)


def _inproj_kernel(*refs, blocks, batch_blocks, prompt_blocks):
    xp_refs, xs_refs = refs[:blocks], refs[blocks:2 * blocks]
    (head_ref, g_ref, b_ref, w0_ref, w1_ref, w2_ref, wa2_ref, ba_ref, bg_ref,
     h_ref, q_ref, k_ref, v_ref, gq_ref, gk_ref, gv_ref, la_ref, gr_ref, gate_ref) = refs[2 * blocks:]
    w_refs = (w0_ref, w1_ref, w2_ref)
    row = lax.broadcasted_iota(jnp.int32, (ATT_BLOCK, 1), 0)
    xs, keeps = [], []
    for s in range(blocks):
        p = pl.program_id(0) * blocks + s
        is_sample = p >= prompt_blocks
        is_head = jnp.logical_and(jnp.logical_not(is_sample), p % batch_blocks == 0)
        xs.append(jnp.where(is_sample, xs_refs[s][...], jnp.where(is_head, head_ref[...], xp_refs[s][...])))
        keeps.append(jnp.where(jnp.logical_and(is_head, row < FRONT_PAD), 0.0, 1.0))
    h = _ln(jnp.concatenate(xs, 0), g_ref[...], b_ref[...])
    h_ref[...] = h
    hb = h.astype(BF16)
    keep = jnp.concatenate(keeps, 0)

    def seg(name):
        piece, a, b = SEG[name]
        return _dot(hb, w_refs[piece][:, a:b])

    q_ref[...] = seg("q")
    k_ref[...] = seg("k")
    v_ref[...] = seg("v")
    gq_ref[...] = seg("gq") * (GLA_HK ** -0.5)
    gk_ref[...] = seg("gk") * keep
    gv_ref[...] = seg("gv") * keep
    gr = seg("gr")
    gr_ref[...] = (gr * _sigmoid(gr)).astype(BF16)
    gate_ref[...] = _sigmoid(seg("gate") + bg_ref[...]).astype(BF16)
    z = _dot(seg("ga").astype(BF16), wa2_ref[...]) + ba_ref[...]
    la = (jnp.minimum(z, 0.0) - jnp.log(1.0 + jnp.exp(-jnp.abs(z)))) * (1.0 / GLA_TAU)
    la_ref[...] = la * keep


def _inproj(x_prompt, x_sample, head, ln_g, ln_b, w_pieces, wa2_bf, b_alpha, b_gate):
    B, seq, _ = x_prompt.shape
    blk = ATT_BLOCK
    seq_blocks = seq // blk
    batch_blocks = seq_blocks + 1
    prompt_blocks = B * batch_blocks
    sample_blocks = x_sample.shape[0] // blk
    n = (prompt_blocks + sample_blocks) * blk
    tm = INPROJ_TILE
    blocks = tm // blk
    xp = x_prompt.reshape(B * seq, D_MODEL)

    def prompt_block(s):
        def imap(i):
            p = jnp.minimum(i * blocks + s, prompt_blocks - 1)
            return (p // batch_blocks * seq_blocks + jnp.maximum(p % batch_blocks - 1, 0), 0)
        return pl.BlockSpec((blk, D_MODEL), imap)

    def sample_block(s):
        return pl.BlockSpec((blk, D_MODEL), lambda i: (jnp.clip(i * blocks + s - prompt_blocks, 0, sample_blocks - 1), 0))

    widths = [D_MODEL, Q_W, KV_W, KV_W, GLA_DK, GLA_DK, GLA_DV, GLA_DK, GLA_DV, 2 * D_MODEL]
    row = lambda w: pl.BlockSpec((tm, w), lambda i: (i, 0))
    const = lambda a: pl.BlockSpec(a.shape, lambda i: (0,) * a.ndim, pipeline_mode=pl.Buffered(1))
    return pl.pallas_call(
        functools.partial(_inproj_kernel, blocks=blocks, batch_blocks=batch_blocks, prompt_blocks=prompt_blocks),
        grid=(n // tm,),
        in_specs=[prompt_block(s) for s in range(blocks)] + [sample_block(s) for s in range(blocks)]
                 + [const(a) for a in (head, ln_g, ln_b, *w_pieces, wa2_bf, b_alpha, b_gate)],
        out_specs=[row(w) for w in widths],
        out_shape=[jax.ShapeDtypeStruct((n, w), BF16 if i >= len(widths) - 2 else F32) for i, w in enumerate(widths)],
        compiler_params=pltpu.CompilerParams(dimension_semantics=("parallel",), vmem_limit_bytes=VMEM_LIMIT),
        name="inproj",
    )(*([xp] * blocks), *([x_sample] * blocks), head, ln_g, ln_b, *w_pieces, wa2_bf, b_alpha, b_gate)


def _softmax_pv(s, sink, vv):
    m = jnp.maximum(jnp.max(s, -1, keepdims=True), sink)
    p = jnp.exp(s - m)
    l = jnp.sum(p, -1, keepdims=True) + jnp.exp(sink - m)
    return _dot(p.astype(BF16), vv) / l


def _swa_prompt_kernel(sink_ref, bias_ref, q_ref, kp_ref, kc_ref, vp_ref, vc_ref, o_ref, kl_ref, vl_ref):
    q = q_ref[...] * (HEAD_DIM ** -0.5)
    kb = jnp.concatenate([kp_ref[...], kc_ref[...]], 0)
    vb = jnp.concatenate([vp_ref[...], vc_ref[...]], 0)
    low = lax.broadcasted_iota(jnp.int32, (1, 2 * HEAD_DIM), 1) < HEAD_DIM
    k_low = jnp.where(low, kb, 0.0)
    k_high = jnp.where(low, 0.0, kb)
    keys = {(0, 0): k_low.astype(BF16), (0, 1): pltpu.roll(k_low, HEAD_DIM, 1).astype(BF16),
            (1, 0): pltpu.roll(k_high, HEAD_DIM, 1).astype(BF16), (1, 1): k_high.astype(BF16)}
    ones_col = (lax.broadcasted_iota(jnp.int32, (1, 2 * HEAD_DIM), 1) == HEAD_DIM).astype(F32)
    values = [jnp.where(low, vb, ones_col).astype(BF16),
              jnp.where(low, pltpu.roll(vb, HEAD_DIM, 1), ones_col).astype(BF16)]
    for pair in range(N_HEADS // 2):
        qp = q[:, pair * 2 * HEAD_DIM:(pair + 1) * 2 * HEAD_DIM]
        outs = []
        for half in range(2):
            h = 2 * pair + half
            kv = h // Q_PER_KV
            qm = jnp.where(low if half == 0 else jnp.logical_not(low), qp, 0.0).astype(BF16)
            s = _dot_nt(qm, keys[(kv, half)]) + bias_ref[0, h]
            m = jnp.maximum(jnp.max(s, -1, keepdims=True), sink_ref[h])
            pv = _dot(jnp.exp(s - m).astype(BF16), values[kv])
            outs.append(pv / (pv[:, HEAD_DIM:HEAD_DIM + 1] + jnp.exp(sink_ref[h] - m)))
        o_ref[:, pair * 2 * HEAD_DIM:(pair + 1) * 2 * HEAD_DIM] = jnp.where(low, outs[0],
                                                                            pltpu.roll(outs[1], HEAD_DIM, 1))

    @pl.when(pl.program_id(1) == pl.num_programs(1) - 1)
    def _():
        kl_ref[0] = kc_ref[...]
        vl_ref[0] = vc_ref[...]


def _swa_bias_table():
    r = np.arange(ATT_BLOCK)[:, None]
    c = np.arange(2 * ATT_BLOCK)[None, :]
    dist = r - c + ATT_BLOCK
    slopes = 2.0 ** -(np.arange(N_HEADS) + 1.0)
    table = np.empty((3, N_HEADS, ATT_BLOCK, 2 * ATT_BLOCK), np.float32)
    for j in range(3):
        seen = (dist >= 0) & (dist < WINDOW) & ((j - 1) * ATT_BLOCK + c - FRONT_PAD >= 0)
        table[j] = np.where(seen[None], -slopes[:, None, None] * dist[None], NEG)
    return table


def _swa_prompt(sink, q, k, v, batch, lp):
    nb = lp // ATT_BLOCK
    assert nb >= 3
    n = batch * lp
    bias = jnp.asarray(_swa_bias_table())
    cur = lambda w: pl.BlockSpec((ATT_BLOCK, w), lambda b, j: (b * nb + j, 0))
    prev = lambda w: pl.BlockSpec((ATT_BLOCK, w), lambda b, j: (b * nb + jnp.maximum(j - 1, 0), 0))
    last = pl.BlockSpec((1, ATT_BLOCK, KV_W), lambda b, j: (b, 0, 0))
    return pl.pallas_call(
        _swa_prompt_kernel,
        grid=(batch, nb),
        in_specs=[pl.BlockSpec(memory_space=pltpu.SMEM),
                  pl.BlockSpec((1,) + bias.shape[1:], lambda b, j: (jnp.minimum(j, 2), 0, 0, 0)),
                  cur(Q_W), prev(KV_W), cur(KV_W), prev(KV_W), cur(KV_W)],
        out_specs=[cur(Q_W), last, last],
        out_shape=[jax.ShapeDtypeStruct((n, Q_W), F32), jax.ShapeDtypeStruct((batch, ATT_BLOCK, KV_W), F32),
                   jax.ShapeDtypeStruct((batch, ATT_BLOCK, KV_W), F32)],
        compiler_params=pltpu.CompilerParams(dimension_semantics=("parallel", "arbitrary")),
        name="swa_prompt",
    )(sink, bias, q, k, k, v, v)


def _swa_sample_kernel(sink_ref, q_ref, k_ref, v_ref, bk_ref, bv_ref, o_ref, nk_ref, nv_ref, *, t_s):
    nbuf = WINDOW
    span = 2 * WINDOW
    rows = Q_PER_KV * t_s
    r = lax.broadcasted_iota(jnp.int32, (rows, span), 0)
    c = lax.broadcasted_iota(jnp.int32, (rows, span), 1)
    t = r % t_s
    dist = t + nbuf - c
    mask = (dist >= 0) & (dist < WINDOW) & (c < nbuf + t_s)
    distf = dist.astype(F32)
    g_col = lax.broadcasted_iota(jnp.int32, (rows, 1), 0) // t_s
    fill = jnp.zeros((span - nbuf - t_s, KV_W), F32)

    def one_seq(s, carry):
        rs = pl.ds(pl.multiple_of(s * t_s, t_s), t_s)
        q = q_ref[rs, :]
        k_new = k_ref[rs, :]
        v_new = v_ref[rs, :]
        bk = bk_ref[s]
        bv = bv_ref[s]
        k_all = jnp.concatenate([bk, k_new, fill], 0)
        v_all = jnp.concatenate([bv, v_new, fill], 0)
        for kv in range(N_KV):
            qg = jnp.concatenate(
                [q[:, (kv * Q_PER_KV + g) * HEAD_DIM:(kv * Q_PER_KV + g + 1) * HEAD_DIM] for g in range(Q_PER_KV)], 0)
            kk = k_all[:, kv * HEAD_DIM:(kv + 1) * HEAD_DIM].astype(BF16)
            vv = v_all[:, kv * HEAD_DIM:(kv + 1) * HEAD_DIM].astype(BF16)
            slope = jnp.zeros((rows, 1), F32)
            sink = jnp.zeros((rows, 1), F32)
            for g in range(Q_PER_KV):
                h = kv * Q_PER_KV + g
                slope = jnp.where(g_col == g, 2.0 ** -(h + 1), slope)
                sink = jnp.where(g_col == g, sink_ref[h], sink)
            sc = _dot_nt(qg.astype(BF16), kk) * (HEAD_DIM ** -0.5) - slope * distf
            sc = jnp.where(mask, sc, NEG)
            o = _softmax_pv(sc, sink, vv)
            for g in range(Q_PER_KV):
                h = kv * Q_PER_KV + g
                o_ref[rs, h * HEAD_DIM:(h + 1) * HEAD_DIM] = o[g * t_s:(g + 1) * t_s]
        nk_ref[s, 0:nbuf - t_s, :] = bk[t_s:, :]
        nk_ref[s, nbuf - t_s:nbuf, :] = k_new
        nv_ref[s, 0:nbuf - t_s, :] = bv[t_s:, :]
        nv_ref[s, nbuf - t_s:nbuf, :] = v_new
        return carry

    lax.fori_loop(0, SAMPLE_SEQS, one_seq, 0, unroll=True)


def _swa_sample(sink, q, k, v, buf_k, buf_v, row0, t_s):
    n_seq = buf_k.shape[0]
    sb = SAMPLE_SEQS
    rb = sb * t_s
    b0 = row0 // rb
    rows = lambda w: pl.BlockSpec((rb, w), lambda i: (b0 + i, 0))
    bufs = pl.BlockSpec((sb, WINDOW, KV_W), lambda i: (i, 0, 0))
    return pl.pallas_call(
        functools.partial(_swa_sample_kernel, t_s=t_s),
        grid=(n_seq // sb,),
        in_specs=[pl.BlockSpec(memory_space=pltpu.SMEM), rows(Q_W), rows(KV_W), rows(KV_W), bufs, bufs],
        out_specs=[pl.BlockSpec((rb, Q_W), lambda i: (i, 0)), bufs, bufs],
        out_shape=[jax.ShapeDtypeStruct((n_seq * t_s, Q_W), F32),
                   jax.ShapeDtypeStruct(buf_k.shape, F32), jax.ShapeDtypeStruct(buf_v.shape, F32)],
        compiler_params=pltpu.CompilerParams(dimension_semantics=("parallel",)),
        name="swa_sample",
    )(sink, q, k, v, buf_k, buf_v)


def _gla_tables(chunk):
    t = np.arange(chunk)[:, None]
    u = np.arange(chunk)[None, :]
    masks = []
    w = chunk // 2
    while w >= 1:
        masks.append((t // (2 * w) == u // (2 * w)) & ((t // w) % 2 == 1) & ((u // w) % 2 == 0))
        w //= 2
    return (u <= t).astype(np.float32), np.stack(masks, 0).astype(np.float32)


def _level_exponents(b, la, w):
    C = b.shape[0]
    row = lax.broadcasted_iota(jnp.int32, b.shape, 0)
    if w >= 4:
        pieces = [jnp.broadcast_to(b[p + w - 1:p + w], (2 * w, b.shape[1])) for p in range(0, C, 2 * w)]
        ref = pieces[0] if len(pieces) == 1 else jnp.concatenate(pieces, 0)
        return jnp.where((row & w) != 0, b - ref, ref - b)
    if w == 2:
        m = row & 3
        nxt = pltpu.roll(la, C - 1, 0)
        prv = pltpu.roll(la, 1, 0)
        return jnp.where(m == 2, la, jnp.where(m == 3, la + prv, jnp.where(m == 0, nxt, 0.0)))
    return jnp.where((row & 1) != 0, la, 0.0)


def _split3(x):
    hi = x.astype(BF16)
    r1 = x - hi.astype(F32)
    mid = r1.astype(BF16)
    lo = (r1 - mid.astype(F32)).astype(BF16)
    return hi, mid, lo


def _gla_intra_kernel(g_ref, m_ref, q_ref, k_ref, v_ref, la_ref, o_ref, qe_ref, ke_ref, vt_ref, d_ref):
    C = GLA_CHUNK
    n_lvl = m_ref.shape[0]
    G = g_ref[...]
    eye = (lax.broadcasted_iota(jnp.int32, (C, C), 0) == lax.broadcasted_iota(jnp.int32, (C, C), 1)).astype(F32)
    for j in range(q_ref.shape[0] // C):
        rs = slice(j * C, (j + 1) * C)
        la = la_ref[rs, :]
        hi, mid, lo = _split3(la)
        b = _dot(G, hi) + _dot(G, mid) + _dot(G, lo)
        b_last = b[C - 1:C]
        q_all = q_ref[rs, :]
        k_all = k_ref[rs, :]
        qe_ref[0, rs, :] = (q_all * jnp.exp(b)).astype(BF16)
        ke_ref[0, rs, :] = (k_all * jnp.exp(b_last - b)).astype(BF16)
        d_ref[0, j] = jnp.broadcast_to(jnp.exp(b_last), (8, GLA_DK))
        q_lvl, k_lvl = [], []
        for l in range(n_lvl):
            El = jnp.exp(_level_exponents(b, la, C >> (l + 1)))
            q_lvl.append((q_all * El).astype(BF16))
            k_lvl.append((k_all * El).astype(BF16))
        for h in range(GLA_HEADS):
            ks = slice(h * GLA_HK, (h + 1) * GLA_HK)
            vs = slice(h * GLA_HV, (h + 1) * GLA_HV)
            v = v_ref[rs, vs]
            att = eye * jnp.sum(q_all[:, ks] * k_all[:, ks], -1, keepdims=True)
            for l in range(n_lvl):
                att = att + m_ref[l] * _dot_nt(q_lvl[l][:, ks], k_lvl[l][:, ks])
            o_ref[0, rs, vs] = _dot(att.astype(BF16), v.astype(BF16))
            vt_ref[0, j, vs, :] = v.T.astype(BF16)


def _gla_inter_kernel(o_ref, qe_ref, ke_ref, vt_ref, d_ref, og_ref, s_ref, st_ref):
    c = pl.program_id(0)
    batch = o_ref.shape[0]

    @pl.when(c == 0)
    def _():
        st_ref[...] = jnp.zeros_like(st_ref)

    for b in range(batch):
        for h in range(GLA_HEADS):
            ks = slice(h * GLA_HK, (h + 1) * GLA_HK)
            vs = slice(h * GLA_HV, (h + 1) * GLA_HV)
            st = st_ref[b * GLA_HEADS + h]
            og_ref[b, :, vs] = o_ref[b, :, vs] + _dot_nt(qe_ref[b, :, ks], st.astype(BF16))
            st_ref[b * GLA_HEADS + h] = st * d_ref[b, 0, 0:1, ks] + _dot(vt_ref[b, 0, vs, :], ke_ref[b, :, ks])

    @pl.when(c == pl.num_programs(0) - 1)
    def _():
        for b in range(batch):
            for h in range(GLA_HEADS):
                s_ref[b, h] = st_ref[b * GLA_HEADS + h].T


def _gla_prompt(gq, gk, gv, la, batch, lp):
    C = GLA_CHUNK
    nc = lp // C
    G, M = _gla_tables(C)
    G = jnp.asarray(G, BF16)
    M = jnp.asarray(M, F32)
    cps = GLA_INTRA_CHUNKS
    assert nc % cps == 0
    rows = lambda w: pl.BlockSpec((cps * C, w), lambda b, c: (b * (nc // cps) + c, 0))
    rows3 = lambda w: pl.BlockSpec((1, cps * C, w), lambda b, c: (b, c, 0))
    full = lambda a: pl.BlockSpec(a.shape, lambda b, c: (0,) * a.ndim)
    o_intra, qe, ke, vt, d = pl.pallas_call(
        _gla_intra_kernel,
        grid=(batch, nc // cps),
        in_specs=[full(G), full(M), rows(GLA_DK), rows(GLA_DK), rows(GLA_DV), rows(GLA_DK)],
        out_specs=[rows3(GLA_DV), rows3(GLA_DK), rows3(GLA_DK),
                   pl.BlockSpec((1, cps, GLA_DV, C), lambda b, c: (b, c, 0, 0)),
                   pl.BlockSpec((1, cps, 8, GLA_DK), lambda b, c: (b, c, 0, 0))],
        out_shape=[jax.ShapeDtypeStruct((batch, lp, GLA_DV), F32),
                   jax.ShapeDtypeStruct((batch, lp, GLA_DK), BF16), jax.ShapeDtypeStruct((batch, lp, GLA_DK), BF16),
                   jax.ShapeDtypeStruct((batch, nc, GLA_DV, C), BF16),
                   jax.ShapeDtypeStruct((batch, nc, 8, GLA_DK), F32)],
        compiler_params=pltpu.CompilerParams(dimension_semantics=("parallel", "parallel")),
        name="gla_intra",
    )(G, M, gq, gk, gv, la)
    chunk = lambda w: pl.BlockSpec((batch, C, w), lambda c: (0, c, 0))
    og, s_fin = pl.pallas_call(
        _gla_inter_kernel,
        grid=(nc,),
        in_specs=[chunk(GLA_DV), chunk(GLA_DK), chunk(GLA_DK),
                  pl.BlockSpec((batch, 1, GLA_DV, C), lambda c: (0, c, 0, 0)),
                  pl.BlockSpec((batch, 1, 8, GLA_DK), lambda c: (0, c, 0, 0))],
        out_specs=[chunk(GLA_DV), pl.BlockSpec((batch, GLA_HEADS, GLA_HK, GLA_HV), lambda c: (0, 0, 0, 0))],
        out_shape=[jax.ShapeDtypeStruct((batch, lp, GLA_DV), F32),
                   jax.ShapeDtypeStruct((batch, GLA_HEADS, GLA_HK, GLA_HV), F32)],
        scratch_shapes=[pltpu.VMEM((batch * GLA_HEADS, GLA_HV, GLA_HK), F32)],
        compiler_params=pltpu.CompilerParams(dimension_semantics=("arbitrary",)),
        name="gla_inter",
    )(o_intra, qe, ke, vt, d)
    return og.reshape(batch * lp, GLA_DV), s_fin


def _gla_sample_kernel(q_ref, k_ref, v_ref, la_ref, s0_ref, o_ref, s_ref, *, t_s):
    T = t_s
    row = lax.broadcasted_iota(jnp.int32, (T, GLA_HK), 0)
    k_fill = jnp.zeros((GLA_HK - T - 8, GLA_HK), F32)
    v_fill = jnp.zeros((GLA_HK - T, GLA_HV), F32)

    def one_seq(s, carry):
        rs = pl.ds(pl.multiple_of(s * T, T), T)
        for h in range(GLA_HEADS):
            ks = slice(h * GLA_HK, (h + 1) * GLA_HK)
            vs = slice(h * GLA_HV, (h + 1) * GLA_HV)
            q = q_ref[rs, ks]
            k = k_ref[rs, ks]
            v = v_ref[rs, vs]
            b = la_ref[rs, ks]
            sh = 1
            while sh < T:
                b = b + jnp.where(row >= sh, pltpu.roll(b, sh, 0), 0.0)
                sh *= 2
            S = s0_ref[s, h]
            o = _dot((q * jnp.exp(b)).astype(BF16), S.astype(BF16))
            for j in range(T):
                e = jnp.exp(jnp.where(row >= j, b - b[j:j + 1], NEG))
                a_col = jnp.sum(q * k[j:j + 1] * e, -1, keepdims=True)
                o = o + a_col * v[j:j + 1]
            o_ref[rs, vs] = o
            b_last = b[T - 1:T]
            ke = k * jnp.exp(b_last - b)
            kt = jnp.concatenate([ke, jnp.broadcast_to(jnp.exp(b_last), (8, GLA_HK)), k_fill], 0).T
            v_pad = jnp.concatenate([v, v_fill], 0)
            s_ref[s, h] = S * kt[:, T:T + 1] + _dot(kt.astype(BF16), v_pad.astype(BF16))
        return carry

    lax.fori_loop(0, SAMPLE_SEQS, one_seq, 0)


def _gla_sample(gq, gk, gv, la, s0, row0, t_s):
    n_seq = s0.shape[0]
    sb = SAMPLE_SEQS
    rb = sb * t_s
    b0 = row0 // rb
    rows = lambda w: pl.BlockSpec((rb, w), lambda i: (b0 + i, 0))
    st = pl.BlockSpec((sb, GLA_HEADS, GLA_HK, GLA_HV), lambda i: (i, 0, 0, 0))
    return pl.pallas_call(
        functools.partial(_gla_sample_kernel, t_s=t_s),
        grid=(n_seq // sb,),
        in_specs=[rows(GLA_DK), rows(GLA_DK), rows(GLA_DV), rows(GLA_DK), st],
        out_specs=[pl.BlockSpec((rb, GLA_DV), lambda i: (i, 0)), st],
        out_shape=[jax.ShapeDtypeStruct((n_seq * t_s, GLA_DV), F32), jax.ShapeDtypeStruct(s0.shape, F32)],
        compiler_params=pltpu.CompilerParams(dimension_semantics=("parallel",), vmem_limit_bytes=VMEM_LIMIT),
        name="gla_sample",
    )(gq, gk, gv, la, s0)


def _route(lt, valid):
    tm = lt.shape[1]
    el = lt[0:N_EXPERTS]
    gl = lt[N_EXPERTS:N_EXPERTS + N_GROUPS]
    g_max = jnp.max(gl, 0, keepdims=True)
    g_row = lax.broadcasted_iota(jnp.int32, (N_GROUPS, tm), 0)
    g_idx = jnp.min(jnp.where(gl == g_max, g_row, N_GROUPS), 0, keepdims=True)
    p_max = 1.0 / jnp.sum(jnp.exp(gl - g_max), 0, keepdims=True)
    e_row = lax.broadcasted_iota(jnp.int32, (N_EXPERTS, tm), 0)
    m1 = jnp.where(e_row // EXP_PER_GROUP == g_idx, el, -jnp.inf)
    v1 = jnp.max(m1, 0, keepdims=True)
    i1 = jnp.min(jnp.where(m1 == v1, e_row, N_EXPERTS), 0, keepdims=True)
    m2 = jnp.where(e_row == i1, -jnp.inf, m1)
    v2 = jnp.max(m2, 0, keepdims=True)
    i2 = jnp.min(jnp.where(m2 == v2, e_row, N_EXPERTS), 0, keepdims=True)
    e2 = jnp.exp(v2 - v1)
    w1 = p_max / (1.0 + e2)
    w2 = p_max * e2 / (1.0 + e2)
    o_row = lax.broadcasted_iota(jnp.int32, (8, tm), 0)
    ids = jnp.where(o_row == 0, i1, jnp.where(o_row == 1, i2, -1))
    return jnp.where(valid, ids, -1), jnp.where(o_row == 0, w1, jnp.where(o_row == 1, w2, 0.0))


def _store_row_tiles(ref, x):
    t, d = x.shape
    n = d // LANES
    for s in range(n):
        ref[pl.ds(s, t, stride=n), :] = x[:, s * LANES:(s + 1) * LANES]


def _load_row_tiles(ref, t, n=ROW_CHUNKS):
    return jnp.concatenate([ref[pl.ds(s, t, stride=n), :] for s in range(n)], axis=1)


def _merge_kernel(h_ref, yap_ref, yas_ref, ogp_ref, ogs_ref, gr_ref, gate_ref, valid_ref, ng_ref,
                  wa_ref, wg_ref, wo_ref, g1_ref, b1_ref, wrh_ref, wrl_ref, br_ref, u_ref,
                  h1t_ref, ids_ref, wts_ref, cnt_ref, run_ref, *, prompt_tiles):
    @pl.when(pl.program_id(0) == 0)
    def _():
        run_ref[...] = jnp.zeros_like(run_ref)

    h = h_ref[...]
    is_prompt = pl.program_id(0) < prompt_tiles
    og = jnp.where(is_prompt, ogp_ref[...], ogs_ref[...])
    ya = jnp.where(is_prompt, yap_ref[...], yas_ref[...])
    parts = []
    for hh in range(GLA_HEADS):
        o = og[:, hh * GLA_HV:(hh + 1) * GLA_HV]
        parts.append(o * lax.rsqrt(jnp.mean(o * o, -1, keepdims=True) + EPS))
    y_gla = jnp.concatenate(parts, 1) * ng_ref[...] * gr_ref[...].astype(F32)
    a = _dot(ya.astype(BF16), wa_ref[...])
    b = _dot(y_gla.astype(BF16), wg_ref[...])
    hm = gate_ref[:, :D_MODEL].astype(F32) * a + gate_ref[:, D_MODEL:].astype(F32) * b
    mix = _dot(hm.astype(BF16), wo_ref[...])
    h1 = _ln(DN_ALPHA * h + mix, g1_ref[...], b1_ref[...])
    _store_row_tiles(h1t_ref, h1)
    h_hi = h1.astype(BF16)
    h_lo = (h1 - h_hi.astype(F32)).astype(BF16)
    logits = _dot(h_hi, wrh_ref[...]) + _dot(h_lo, wrh_ref[...]) + _dot(h_hi, wrl_ref[...]) + br_ref[...]
    ids, wts_ref[...] = _route(logits.T, valid_ref[...] > 0.0)
    tm = ids.shape[1]
    e_row = lax.broadcasted_iota(jnp.int32, (N_EXPERTS, tm), 0)
    run = run_ref[:, 0:1]
    ranks = []
    for kk in range(2):
        onehot = (e_row == ids[kk:kk + 1]).astype(F32)
        before = _dot(onehot.astype(BF16), u_ref[...])
        ranks.append(jnp.sum(onehot * (run + before), 0, keepdims=True).astype(jnp.int32))
        run = run + jnp.sum(onehot, 1, keepdims=True)
    run_ref[...] = jnp.broadcast_to(run, run_ref.shape)
    o_row = lax.broadcasted_iota(jnp.int32, (8, tm), 0)
    ids_ref[...] = jnp.where(o_row == 2, ranks[0], jnp.where(o_row == 3, ranks[1], ids))
    cnt_ref[...] = run_ref[...].astype(jnp.int32)


def _merge(h, ya_p, ya_s, og_p, og_s, gr, gate, valid, ng, wa, wg, wo, g1, b1, wrh, wrl, br):
    n = h.shape[0]
    tm = ROW_TILE
    u = jnp.asarray(np.triu(np.ones((tm, tm), np.float32), 1), BF16)
    pt = ya_p.shape[0] // tm
    st = ya_s.shape[0] // tm
    row = lambda w: pl.BlockSpec((tm, w), lambda i: (i, 0))
    row_p = lambda w: pl.BlockSpec((tm, w), lambda i: (jnp.minimum(i, pt - 1), 0))
    row_s = lambda w: pl.BlockSpec((tm, w), lambda i: (jnp.clip(i - pt, 0, st - 1), 0))
    lane = lambda r: pl.BlockSpec((r, tm), lambda i: (0, i))
    full = lambda a: pl.BlockSpec(a.shape, lambda i: (0,) * a.ndim)
    return pl.pallas_call(
        functools.partial(_merge_kernel, prompt_tiles=pt),
        grid=(n // tm,),
        in_specs=[row(D_MODEL), row_p(Q_W), row_s(Q_W), row_p(GLA_DV), row_s(GLA_DV), row(GLA_DV), row(2 * D_MODEL),
                  lane(1), full(ng), full(wa), full(wg), full(wo), full(g1), full(b1),
                  full(wrh), full(wrl), full(br), full(u)],
        out_specs=[pl.BlockSpec((tm * ROW_CHUNKS, LANES), lambda i: (i, 0)), lane(8), lane(8),
                   pl.BlockSpec((N_EXPERTS, LANES), lambda i: (0, 0))],
        out_shape=[jax.ShapeDtypeStruct((n * ROW_CHUNKS, LANES), F32),
                   jax.ShapeDtypeStruct((8, n), jnp.int32), jax.ShapeDtypeStruct((8, n), F32),
                   jax.ShapeDtypeStruct((N_EXPERTS, LANES), jnp.int32)],
        scratch_shapes=[pltpu.VMEM((N_EXPERTS, LANES), F32)],
        compiler_params=pltpu.CompilerParams(dimension_semantics=("arbitrary",), vmem_limit_bytes=VMEM_LIMIT),
        name="merge",
    )(h, ya_p, ya_s, og_p, og_s, gr, gate, valid, ng, wa, wg, wo, g1, b1, wrh, wrl, br, u)


def _gather_row_tiles(idx_ref, idx0, src_hbm, dst, sem, n):
    def body(r, carry):
        t = idx_ref[idx0 + r]
        pltpu.make_async_copy(src_hbm.at[pl.ds(pl.multiple_of(t * ROW_CHUNKS, ROW_CHUNKS), ROW_CHUNKS), :],
                              dst.at[pl.ds(pl.multiple_of(r * ROW_CHUNKS, ROW_CHUNKS), ROW_CHUNKS), :], sem).start()
        return carry
    lax.fori_loop(0, n, body, 0, unroll=8)


def _wait_row_tiles(src_hbm, dst, sem, n):
    pltpu.make_async_copy(src_hbm.at[pl.ds(0, n * ROW_CHUNKS), :], dst, sem).wait()


def _tiles(ref, first, n=1):
    return ref.at[pl.ds(pl.multiple_of(first * ROW_CHUNKS, ROW_CHUNKS), n * ROW_CHUNKS), :]


def _dispatch_kernel(dest_ref, start_ref, cnt_ref, nt_ref, h_hbm, xs_hbm, buf, zbuf, sem_in, sem_out, sem_z,
                     *, ranges, n_rows, max_tiles):
    CH = DISPATCH_CHUNK
    T = FFN_TILE
    c = pl.program_id(0)
    n_chunks = pl.num_programs(0)

    def load(row0, slot):
        return pltpu.make_async_copy(_tiles(h_hbm, row0, CH), buf.at[slot], sem_in.at[slot])

    def scatter(row0, slot):
        def body(r, carry):
            for kk in range(2):
                d = dest_ref[kk * n_rows + row0 + r]
                pltpu.make_async_copy(_tiles(buf.at[slot], r), _tiles(xs_hbm, d), sem_out.at[slot]).start()
            return carry
        lax.fori_loop(0, CH, body, 0, unroll=8)

    def drain(slot):
        for _ in range(2):
            pltpu.make_async_copy(buf.at[slot], _tiles(xs_hbm, 0, CH), sem_out.at[slot]).wait()

    groups, per_group, first, stride, extra, extra_first = ranges

    def row0(j):
        in_group = first + (j // per_group) * stride + (j % per_group) * CH
        return jnp.where(j < groups * per_group, in_group, extra_first + (j - groups * per_group) * CH)

    @pl.when(c == 0)
    def _():
        for j in range(2):
            load(row0(j), j).start()

    slot = c % 3
    load(row0(c), slot).wait()
    scatter(row0(c), slot)

    @pl.when(c > 0)
    def _():
        drain((c + 2) % 3)

    @pl.when(c + 2 < n_chunks)
    def _():
        load(row0(c + 2), (c + 2) % 3).start()

    @pl.when(c == n_chunks - 1)
    def _():
        drain(slot)
        _zero_unowned_slots(start_ref, cnt_ref, nt_ref, xs_hbm, zbuf, sem_z, max_tiles)


def _zero_unowned_slots(start_ref, cnt_ref, nt_ref, xs_hbm, zbuf, sem_z, max_tiles):
    T = FFN_TILE
    zbuf[...] = jnp.zeros_like(zbuf)

    def tail_copies(e, wait):
        cnt = cnt_ref[e]
        n = (T - (cnt & (T - 1))) & (T - 1)
        first = start_ref[e] + cnt
        for bit in reversed(range(T.bit_length() - 1)):
            size = 1 << bit

            @pl.when((n & size) != 0)
            def _():
                cp = pltpu.make_async_copy(_tiles(zbuf, 0, size),
                                           _tiles(xs_hbm, first + ((n >> (bit + 1)) << (bit + 1)), size), sem_z)
                cp.wait() if wait else cp.start()

    def unused_tile(t, wait):
        cp = pltpu.make_async_copy(zbuf, _tiles(xs_hbm, t * T, T), sem_z)
        cp.wait() if wait else cp.start()

    for wait in (False, True):
        def per_expert(e, carry, wait=wait):
            tail_copies(e, wait)
            return carry

        def per_tile(t, carry, wait=wait):
            unused_tile(t, wait)
            return carry
        lax.fori_loop(0, N_EXPERTS, per_expert, 0)
        lax.fori_loop(nt_ref[0], max_tiles, per_tile, 0)


def _dispatch(dest, start, counts, n_tiles, h1t, ranges, max_tiles):
    T = FFN_TILE
    n_rows = h1t.shape[0] // ROW_CHUNKS
    return pl.pallas_call(
        functools.partial(_dispatch_kernel, ranges=ranges, n_rows=n_rows, max_tiles=max_tiles),
        grid_spec=pltpu.PrefetchScalarGridSpec(
            num_scalar_prefetch=4,
            grid=(ranges[0] * ranges[1] + ranges[4],),
            in_specs=[pl.BlockSpec(memory_space=pl.ANY)],
            out_specs=pl.BlockSpec(memory_space=pl.ANY),
            scratch_shapes=[pltpu.VMEM((3, DISPATCH_CHUNK * ROW_CHUNKS, LANES), F32),
                            pltpu.VMEM((T * ROW_CHUNKS, LANES), F32),
                            pltpu.SemaphoreType.DMA((3,)), pltpu.SemaphoreType.DMA((3,)), pltpu.SemaphoreType.DMA]),
        out_shape=jax.ShapeDtypeStruct((max_tiles * T * ROW_CHUNKS, LANES), F32),
        compiler_params=pltpu.CompilerParams(dimension_semantics=("arbitrary",)),
        name="dispatch",
    )(dest, start, counts, n_tiles, h1t)


def _ffn_kernel(te_ref, nxt_ref, nt_ref, x_ref, wg_hbm, wu_hbm, wd_hbm, out_ref, stage_g, stage_u, stage_d, sem,
                wgb, wub, wdb):
    i = pl.program_id(0)
    nt = nt_ref[0]
    T = FFN_TILE

    def stage(e):
        return [pltpu.make_async_copy(src.at[e], dst, sem.at[n])
                for n, (src, dst) in enumerate(((wg_hbm, stage_g), (wu_hbm, stage_u), (wd_hbm, stage_d)))]

    @pl.when(i == 0)
    def _():
        for cp in stage(te_ref[0]):
            cp.start()

    @pl.when(i < nt)
    def _():
        @pl.when((i == 0) | (te_ref[i] != te_ref[jnp.maximum(i - 1, 0)]))
        def _():
            for cp in stage(te_ref[i]):
                cp.wait()
            wgb[...] = stage_g[...].astype(BF16)
            wub[...] = stage_u[...].astype(BF16)
            wdb[...] = stage_d[...].astype(BF16)

            @pl.when(nxt_ref[i] >= 0)
            def _():
                for cp in stage(nxt_ref[i]):
                    cp.start()

        x = _load_row_tiles(x_ref, T).astype(BF16)
        g = _dot(x, wgb[...])
        u = _dot(x, wub[...])
        _store_row_tiles(out_ref, _dot((g * _sigmoid(g) * u).astype(BF16), wdb[...]))

    @pl.when(i >= nt)
    def _():
        out_ref[...] = jnp.zeros_like(out_ref)


def _ffn(tile_expert, next_expert, n_tiles, xs, w_g, w_u, w_d):
    T = FFN_TILE
    max_tiles = tile_expert.shape[0]
    hbm = pl.BlockSpec(memory_space=pl.ANY)
    tile = lambda imap: pl.BlockSpec((T * ROW_CHUNKS, LANES), imap)
    return pl.pallas_call(
        _ffn_kernel,
        grid_spec=pltpu.PrefetchScalarGridSpec(
            num_scalar_prefetch=3,
            grid=(max_tiles,),
            in_specs=[tile(lambda i, te, nxt, nt: (jnp.minimum(i, nt[0] - 1), 0)), hbm, hbm, hbm],
            out_specs=tile(lambda i, te, nxt, nt: (i, 0)),
            scratch_shapes=[pltpu.VMEM(w_g.shape[1:], F32), pltpu.VMEM(w_u.shape[1:], F32),
                            pltpu.VMEM(w_d.shape[1:], F32), pltpu.SemaphoreType.DMA((3,)),
                            pltpu.VMEM(w_g.shape[1:], BF16), pltpu.VMEM(w_u.shape[1:], BF16),
                            pltpu.VMEM(w_d.shape[1:], BF16)]),
        out_shape=jax.ShapeDtypeStruct(xs.shape, F32),
        compiler_params=pltpu.CompilerParams(dimension_semantics=("arbitrary",)),
        name="ffn",
    )(tile_expert, next_expert, n_tiles, xs, w_g, w_u, w_d)


def _combine_kernel(dest_ref, h_ref, w_ref, g_ref, b_ref, ys_hbm, y_ref, buf, sem, *, blk, n_rows):
    i = pl.program_id(0)
    n = pl.num_programs(0)
    T = CMB_TILE

    def gather(step, slot):
        for kk in range(2):
            _gather_row_tiles(dest_ref, kk * n_rows + blk(step) * T, ys_hbm, buf.at[slot, kk], sem.at[slot], T)

    @pl.when(i == 0)
    def _():
        gather(0, 0)

    @pl.when(i + 1 < n)
    def _():
        gather(i + 1, (i + 1) % 2)

    slot = i % 2
    for kk in range(2):
        _wait_row_tiles(ys_hbm, buf.at[slot, kk], sem.at[slot], T)
    w = w_ref[...]
    ff = w[:, 0:1] * _load_row_tiles(buf.at[slot, 0], T) + w[:, 1:2] * _load_row_tiles(buf.at[slot, 1], T)
    y_ref[...] = _ln(DN_ALPHA * _load_row_tiles(h_ref, T) + ff, g_ref[...], b_ref[...])


def _combine(dest, h1t, wts_t, ys, g2, b2, n_out, first_block, blocks_per_batch, skip_blocks):
    T = CMB_TILE
    n_rows = h1t.shape[0] // ROW_CHUNKS
    if skip_blocks:
        blk = lambda i: first_block + i + (i // blocks_per_batch + 1) * skip_blocks
    else:
        blk = lambda i: first_block + i
    full = lambda a: pl.BlockSpec(a.shape, lambda i, d: (0,) * a.ndim)
    return pl.pallas_call(
        functools.partial(_combine_kernel, blk=blk, n_rows=n_rows),
        grid_spec=pltpu.PrefetchScalarGridSpec(
            num_scalar_prefetch=1,
            grid=(n_out // T,),
            in_specs=[pl.BlockSpec((T * ROW_CHUNKS, LANES), lambda i, d: (blk(i), 0)),
                      pl.BlockSpec((T, 2), lambda i, d: (blk(i), 0)),
                      full(g2), full(b2), pl.BlockSpec(memory_space=pl.ANY)],
            out_specs=pl.BlockSpec((T, D_MODEL), lambda i, d: (i, 0)),
            scratch_shapes=[pltpu.VMEM((2, 2, T * ROW_CHUNKS, LANES), F32), pltpu.SemaphoreType.DMA((2,))]),
        out_shape=jax.ShapeDtypeStruct((n_out, D_MODEL), F32),
        compiler_params=pltpu.CompilerParams(dimension_semantics=("arbitrary",)),
        name="combine",
    )(dest, h1t, wts_t, g2, b2, ys)


def _dispatch_plan(routing, counts, max_tiles):
    T = FFN_TILE
    tiles_e = (counts + T - 1) // T
    tile_end = jnp.cumsum(tiles_e)
    n_tiles = tile_end[-1]
    start = (tile_end - tiles_e) * T
    ids, rank = routing[0:2], routing[2:4]
    onehot = (ids[..., None] == jnp.arange(N_EXPERTS, dtype=jnp.int32)).astype(jnp.int32)
    dest = (jnp.sum(onehot * start, axis=-1) + jnp.where(ids >= 0, rank, 0)).reshape(-1).astype(jnp.int32)
    experts = jnp.arange(N_EXPERTS, dtype=jnp.int32)
    tiles = jnp.arange(max_tiles, dtype=jnp.int32)
    te = jnp.sum((tiles[:, None] >= tile_end[None, :]).astype(jnp.int32), axis=1)
    te_last = jnp.max(jnp.where(counts > 0, experts, 0))
    te = jnp.where(tiles < n_tiles, te, te_last).astype(jnp.int32)
    later = jnp.where((counts > 0)[None, :] & (experts[None, :] > te[:, None]), experts[None, :], N_EXPERTS)
    nxt = jnp.min(later, axis=1)
    nxt = jnp.where(nxt < N_EXPERTS, nxt, -1).astype(jnp.int32)
    return te, nxt, n_tiles.reshape(1).astype(jnp.int32), start.astype(jnp.int32), dest


def kernel(x_prompt, x_sample, state_swa_k, state_swa_v, state_gla, meta_tokens, ln_emb_g, ln_emb_b, w_in, b_gate, attn_sink, w_alpha2, b_alpha, gla_norm_g, w_attn_br, w_gla_br, w_out, ln1_g, ln1_b, w_router_group, b_router_group, w_router_expert, b_router_expert, w_exp_gate, w_exp_up, w_exp_down, ln2_g, ln2_b):
    B, seq, _ = x_prompt.shape
    n_seq, t_s, _ = x_sample.shape
    depth = w_in.shape[0]
    assert depth == 1 and seq % ATT_BLOCK == 0 and t_s == 8 and SKIP_ROWS == ATT_BLOCK == WINDOW
    lp = SKIP_ROWS + seq
    NP, NS = B * lp, n_seq * t_s
    NR = NP + NS
    assert NP % ROW_TILE == 0 and NS % ROW_TILE == 0 and n_seq % SAMPLE_SEQS == 0
    assert seq % DISPATCH_CHUNK == 0 and NS % DISPATCH_CHUNK == 0
    l = 0
    row2 = lambda a: a.reshape(1, -1)

    head = jnp.concatenate([jnp.zeros((FRONT_PAD, D_MODEL), F32), meta_tokens], axis=0)
    pos = np.arange(NR)
    moe_valid = jnp.asarray(~((pos < NP) & (pos % lp < SKIP_ROWS)), F32).reshape(1, NR)

    wi = w_in[l]
    c_ga = sum((Q_W, KV_W, KV_W, GLA_DK, GLA_DK, GLA_DV))
    assert c_ga == W_IN_SPLIT
    w_pieces = (wi[:, :c_ga].astype(BF16), wi[:, c_ga + GLA_RANK:].astype(BF16),
                jnp.pad(wi[:, c_ga:c_ga + GLA_RANK], ((0, 0), (0, 128 - GLA_RANK))).astype(BF16))
    wa2_bf = jnp.concatenate([w_alpha2[l], jnp.zeros((128 - GLA_RANK, GLA_DK), F32)], axis=0).astype(BF16)

    h, q, k, v, gq, gk, gv, la, gr, gate = _inproj(x_prompt, x_sample.reshape(NS, D_MODEL), head, row2(ln_emb_g),
                                                   row2(ln_emb_b), w_pieces, wa2_bf, row2(b_alpha[l]),
                                                   row2(b_gate[l]))

    sink = attn_sink[l]
    ya_p, k_p, v_p = _swa_prompt(sink, q, k, v, B, lp)
    buf_k = state_swa_k[l].reshape(n_seq, WINDOW, KV_W)
    buf_v = state_swa_v[l].reshape(n_seq, WINDOW, KV_W)
    ya_s, nk_s, nv_s = _swa_sample(sink, q, k, v, buf_k, buf_v, NP, t_s)

    og_p, s_p = _gla_prompt(gq, gk, gv, la, B, lp)
    og_s, s_s = _gla_sample(gq, gk, gv, la, state_gla[l], NP, t_s)

    wr = jnp.concatenate([w_router_expert[l], w_router_group[l],
                          jnp.zeros((D_MODEL, LANES - N_EXPERTS - N_GROUPS), F32)], axis=1)
    br = jnp.concatenate([b_router_expert[l], b_router_group[l],
                          jnp.zeros((LANES - N_EXPERTS - N_GROUPS,), F32)]).reshape(1, LANES)
    wr_hi = wr.astype(BF16)
    h1t, routing, wts, counts = _merge(h, ya_p, ya_s, og_p, og_s, gr, gate, moe_valid,
                                       row2(gla_norm_g[l]),
                                       w_attn_br[l].astype(BF16), w_gla_br[l].astype(BF16), w_out[l].astype(BF16),
                                       row2(ln1_g[l]), row2(ln1_b[l]), wr_hi, (wr - wr_hi.astype(F32)).astype(BF16), br)

    n_tok = B * seq + NS
    max_tiles = (2 * n_tok) // FFN_TILE + N_EXPERTS
    counts = counts[:, 0]
    te, nxt, n_tiles, start, dest = _dispatch_plan(routing[0:4], counts, max_tiles)
    routed = (B, seq // DISPATCH_CHUNK, SKIP_ROWS, lp, NS // DISPATCH_CHUNK, NP)
    xs = _dispatch(dest, start, counts, n_tiles, h1t, routed, max_tiles)
    ys = _ffn(te, nxt, n_tiles, xs, w_exp_gate[l], w_exp_up[l], w_exp_down[l])

    wts_t = wts[0:2].T
    g2, b2 = row2(ln2_g[l]), row2(ln2_b[l])
    skip_blocks = SKIP_ROWS // CMB_TILE
    y_p = _combine(dest, h1t, wts_t, ys, g2, b2, B * seq, 0, seq // CMB_TILE, skip_blocks)
    y_s = _combine(dest, h1t, wts_t, ys, g2, b2, NS, NP // CMB_TILE, 1, 0)

    kv_shape = (1, B, WINDOW, N_KV, HEAD_DIM)
    return (y_p.reshape(B, seq, D_MODEL), y_s.reshape(n_seq, t_s, D_MODEL),
            k_p.reshape(kv_shape), v_p.reshape(kv_shape), s_p[None],
            nk_s.reshape(1, n_seq, WINDOW, N_KV, HEAD_DIM), nv_s.reshape(1, n_seq, WINDOW, N_KV, HEAD_DIM),
            s_s[None])
```

```python
import functools

import numpy as np
import jax
import jax.numpy as jnp
from jax import lax
from jax.experimental import pallas as pl
from jax.experimental.pallas import tpu as pltpu

F32 = jnp.float32
BF16 = jnp.bfloat16

D_MODEL = 1024
N_META = 16
HEAD_DIM = 64
N_HEADS = 8
N_KV = 2
Q_PER_KV = 4
WINDOW = 128
ATT_BLOCK = 128
GLA_HEADS = 4
GLA_HK = 128
GLA_HV = 256
GLA_DK = GLA_HEADS * GLA_HK
GLA_DV = GLA_HEADS * GLA_HV
GLA_RANK = 16
GLA_TAU = 16.0
GLA_CHUNK = 64
N_GROUPS = 4
EXP_PER_GROUP = 8
N_EXPERTS = 32
D_EXPERT = 256
DN_ALPHA = 2.0 ** 0.25
EPS = 1e-5
NEG = -1e30

FRONT_PAD = (-N_META) % ATT_BLOCK
SKIP_ROWS = FRONT_PAD + N_META

Q_W, KV_W = N_HEADS * HEAD_DIM, N_KV * HEAD_DIM
SEG = {}
_o = 0
for _n, _w in (("q", Q_W), ("k", KV_W), ("v", KV_W), ("gq", GLA_DK), ("gk", GLA_DK), ("gv", GLA_DV)):
    SEG[_n] = (0, _o, _o + _w)
    _o += _w
W_IN_SPLIT = _o
SEG["gr"] = (1, 0, GLA_DV)
SEG["gate"] = (1, GLA_DV, GLA_DV + 2 * D_MODEL)
SEG["ga"] = (2, 0, 128)

ROW_TILE = 512
FFN_TILE = 256
CMB_TILE = 128
DISPATCH_CHUNK = 128
GLA_INTRA_CHUNKS = 3
INPROJ_TILE = 512
FFN_STREAMS = 4
SAMPLE_SEQS = 8
LANES = 128
ROW_CHUNKS = D_MODEL // LANES
VMEM_LIMIT = 56 * 1024 * 1024


def _ln(x, g, b):
    mu = jnp.mean(x, -1, keepdims=True)
    xc = x - mu
    var = jnp.mean(xc * xc, -1, keepdims=True)
    return xc * lax.rsqrt(var + EPS) * g + b


def _sigmoid(x):
    return 0.5 * jnp.tanh(0.5 * x) + 0.5


def _dot(a, b):
    return jnp.dot(a, b, preferred_element_type=F32)


def _dot_nt(a, b):
    return lax.dot_general(a, b, (((1,), (1,)), ((), ())), preferred_element_type=F32)


def _inproj_kernel(*refs, blocks, batch_blocks, prompt_blocks):
    xp_refs, xs_refs = refs[:blocks], refs[blocks:2 * blocks]
    (head_ref, g_ref, b_ref, w0_ref, w1_ref, w2_ref, wa2_ref, ba_ref, bg_ref,
     h_ref, q_ref, k_ref, v_ref, gq_ref, gk_ref, gv_ref, la_ref, gr_ref, gate_ref) = refs[2 * blocks:]
    w_refs = (w0_ref, w1_ref, w2_ref)
    row = lax.broadcasted_iota(jnp.int32, (ATT_BLOCK, 1), 0)
    xs, keeps = [], []
    for s in range(blocks):
        p = pl.program_id(0) * blocks + s
        is_sample = p >= prompt_blocks
        is_head = jnp.logical_and(jnp.logical_not(is_sample), p % batch_blocks == 0)
        xs.append(jnp.where(is_sample, xs_refs[s][...], jnp.where(is_head, head_ref[...], xp_refs[s][...])))
        keeps.append(jnp.where(jnp.logical_and(is_head, row < FRONT_PAD), 0.0, 1.0))
    h = _ln(jnp.concatenate(xs, 0), g_ref[...], b_ref[...])
    h_ref[...] = h
    hb = h.astype(BF16)
    keep = jnp.concatenate(keeps, 0)

    def seg(name):
        piece, a, b = SEG[name]
        return _dot(hb, w_refs[piece][:, a:b])

    q_ref[...] = seg("q")
    k_ref[...] = seg("k")
    v_ref[...] = seg("v")
    gq_ref[...] = seg("gq") * (GLA_HK ** -0.5)
    gk_ref[...] = seg("gk") * keep
    gv_ref[...] = seg("gv") * keep
    gr = seg("gr")
    gr_ref[...] = (gr * _sigmoid(gr)).astype(BF16)
    gate_ref[...] = _sigmoid(seg("gate") + bg_ref[...]).astype(BF16)
    z = _dot(seg("ga").astype(BF16), wa2_ref[...]) + ba_ref[...]
    la = (jnp.minimum(z, 0.0) - jnp.log(1.0 + jnp.exp(-jnp.abs(z)))) * (1.0 / GLA_TAU)
    la_ref[...] = la * keep


def _inproj(x_prompt, x_sample, head, ln_g, ln_b, w_pieces, wa2_bf, b_alpha, b_gate):
    B, seq, _ = x_prompt.shape
    blk = ATT_BLOCK
    seq_blocks = seq // blk
    batch_blocks = seq_blocks + 1
    prompt_blocks = B * batch_blocks
    sample_blocks = x_sample.shape[0] // blk
    n = (prompt_blocks + sample_blocks) * blk
    tm = INPROJ_TILE
    blocks = tm // blk
    xp = x_prompt.reshape(B * seq, D_MODEL)

    def prompt_block(s):
        def imap(i):
            p = jnp.minimum(i * blocks + s, prompt_blocks - 1)
            return (p // batch_blocks * seq_blocks + jnp.maximum(p % batch_blocks - 1, 0), 0)
        return pl.BlockSpec((blk, D_MODEL), imap)

    def sample_block(s):
        return pl.BlockSpec((blk, D_MODEL), lambda i: (jnp.clip(i * blocks + s - prompt_blocks, 0, sample_blocks - 1), 0))

    widths = [D_MODEL, Q_W, KV_W, KV_W, GLA_DK, GLA_DK, GLA_DV, GLA_DK, GLA_DV, 2 * D_MODEL]
    row = lambda w: pl.BlockSpec((tm, w), lambda i: (i, 0))
    const = lambda a: pl.BlockSpec(a.shape, lambda i: (0,) * a.ndim, pipeline_mode=pl.Buffered(1))
    return pl.pallas_call(
        functools.partial(_inproj_kernel, blocks=blocks, batch_blocks=batch_blocks, prompt_blocks=prompt_blocks),
        grid=(n // tm,),
        in_specs=[prompt_block(s) for s in range(blocks)] + [sample_block(s) for s in range(blocks)]
                 + [const(a) for a in (head, ln_g, ln_b, *w_pieces, wa2_bf, b_alpha, b_gate)],
        out_specs=[row(w) for w in widths],
        out_shape=[jax.ShapeDtypeStruct((n, w), BF16 if i >= len(widths) - 2 else F32) for i, w in enumerate(widths)],
        compiler_params=pltpu.CompilerParams(dimension_semantics=("parallel",), vmem_limit_bytes=VMEM_LIMIT),
        name="inproj",
    )(*([xp] * blocks), *([x_sample] * blocks), head, ln_g, ln_b, *w_pieces, wa2_bf, b_alpha, b_gate)


def _softmax_pv(s, sink, vv):
    m = jnp.maximum(jnp.max(s, -1, keepdims=True), sink)
    p = jnp.exp(s - m)
    l = jnp.sum(p, -1, keepdims=True) + jnp.exp(sink - m)
    return _dot(p.astype(BF16), vv) / l


def _swa_prompt_kernel(sink_ref, bias_ref, q_ref, kp_ref, kc_ref, vp_ref, vc_ref, o_ref, kl_ref, vl_ref):
    q = q_ref[...] * (HEAD_DIM ** -0.5)
    kb = jnp.concatenate([kp_ref[...], kc_ref[...]], 0)
    vb = jnp.concatenate([vp_ref[...], vc_ref[...]], 0)
    low = lax.broadcasted_iota(jnp.int32, (1, 2 * HEAD_DIM), 1) < HEAD_DIM
    k_low = jnp.where(low, kb, 0.0)
    k_high = jnp.where(low, 0.0, kb)
    keys = {(0, 0): k_low.astype(BF16), (0, 1): pltpu.roll(k_low, HEAD_DIM, 1).astype(BF16),
            (1, 0): pltpu.roll(k_high, HEAD_DIM, 1).astype(BF16), (1, 1): k_high.astype(BF16)}
    ones_col = (lax.broadcasted_iota(jnp.int32, (1, 2 * HEAD_DIM), 1) == HEAD_DIM).astype(F32)
    values = [jnp.where(low, vb, ones_col).astype(BF16),
              jnp.where(low, pltpu.roll(vb, HEAD_DIM, 1), ones_col).astype(BF16)]
    for pair in range(N_HEADS // 2):
        qp = q[:, pair * 2 * HEAD_DIM:(pair + 1) * 2 * HEAD_DIM]
        outs = []
        for half in range(2):
            h = 2 * pair + half
            kv = h // Q_PER_KV
            qm = jnp.where(low if half == 0 else jnp.logical_not(low), qp, 0.0).astype(BF16)
            s = _dot_nt(qm, keys[(kv, half)]) + bias_ref[0, h]
            m = jnp.maximum(jnp.max(s, -1, keepdims=True), sink_ref[h])
            pv = _dot(jnp.exp(s - m).astype(BF16), values[kv])
            outs.append(pv / (pv[:, HEAD_DIM:HEAD_DIM + 1] + jnp.exp(sink_ref[h] - m)))
        o_ref[:, pair * 2 * HEAD_DIM:(pair + 1) * 2 * HEAD_DIM] = jnp.where(low, outs[0],
                                                                            pltpu.roll(outs[1], HEAD_DIM, 1))

    @pl.when(pl.program_id(1) == pl.num_programs(1) - 1)
    def _():
        kl_ref[0] = kc_ref[...]
        vl_ref[0] = vc_ref[...]


def _swa_bias_table():
    r = np.arange(ATT_BLOCK)[:, None]
    c = np.arange(2 * ATT_BLOCK)[None, :]
    dist = r - c + ATT_BLOCK
    slopes = 2.0 ** -(np.arange(N_HEADS) + 1.0)
    table = np.empty((3, N_HEADS, ATT_BLOCK, 2 * ATT_BLOCK), np.float32)
    for j in range(3):
        seen = (dist >= 0) & (dist < WINDOW) & ((j - 1) * ATT_BLOCK + c - FRONT_PAD >= 0)
        table[j] = np.where(seen[None], -slopes[:, None, None] * dist[None], NEG)
    return table


def _swa_prompt(sink, q, k, v, batch, lp):
    nb = lp // ATT_BLOCK
    assert nb >= 3
    n = batch * lp
    bias = jnp.asarray(_swa_bias_table())
    cur = lambda w: pl.BlockSpec((ATT_BLOCK, w), lambda b, j: (b * nb + j, 0))
    prev = lambda w: pl.BlockSpec((ATT_BLOCK, w), lambda b, j: (b * nb + jnp.maximum(j - 1, 0), 0))
    last = pl.BlockSpec((1, ATT_BLOCK, KV_W), lambda b, j: (b, 0, 0))
    return pl.pallas_call(
        _swa_prompt_kernel,
        grid=(batch, nb),
        in_specs=[pl.BlockSpec(memory_space=pltpu.SMEM),
                  pl.BlockSpec((1,) + bias.shape[1:], lambda b, j: (jnp.minimum(j, 2), 0, 0, 0)),
                  cur(Q_W), prev(KV_W), cur(KV_W), prev(KV_W), cur(KV_W)],
        out_specs=[cur(Q_W), last, last],
        out_shape=[jax.ShapeDtypeStruct((n, Q_W), F32), jax.ShapeDtypeStruct((batch, ATT_BLOCK, KV_W), F32),
                   jax.ShapeDtypeStruct((batch, ATT_BLOCK, KV_W), F32)],
        compiler_params=pltpu.CompilerParams(dimension_semantics=("parallel", "arbitrary")),
        name="swa_prompt",
    )(sink, bias, q, k, k, v, v)


def _swa_sample_kernel(sink_ref, q_ref, k_ref, v_ref, bk_ref, bv_ref, o_ref, nk_ref, nv_ref, *, t_s):
    nbuf = WINDOW
    span = 2 * WINDOW
    rows = Q_PER_KV * t_s
    r = lax.broadcasted_iota(jnp.int32, (rows, span), 0)
    c = lax.broadcasted_iota(jnp.int32, (rows, span), 1)
    t = r % t_s
    dist = t + nbuf - c
    mask = (dist >= 0) & (dist < WINDOW) & (c < nbuf + t_s)
    distf = dist.astype(F32)
    g_col = lax.broadcasted_iota(jnp.int32, (rows, 1), 0) // t_s
    fill = jnp.zeros((span - nbuf - t_s, KV_W), F32)

    def one_seq(s, carry):
        rs = pl.ds(pl.multiple_of(s * t_s, t_s), t_s)
        q = q_ref[rs, :]
        k_new = k_ref[rs, :]
        v_new = v_ref[rs, :]
        bk = bk_ref[s]
        bv = bv_ref[s]
        k_all = jnp.concatenate([bk, k_new, fill], 0)
        v_all = jnp.concatenate([bv, v_new, fill], 0)
        for kv in range(N_KV):
            qg = jnp.concatenate(
                [q[:, (kv * Q_PER_KV + g) * HEAD_DIM:(kv * Q_PER_KV + g + 1) * HEAD_DIM] for g in range(Q_PER_KV)], 0)
            kk = k_all[:, kv * HEAD_DIM:(kv + 1) * HEAD_DIM].astype(BF16)
            vv = v_all[:, kv * HEAD_DIM:(kv + 1) * HEAD_DIM].astype(BF16)
            slope = jnp.zeros((rows, 1), F32)
            sink = jnp.zeros((rows, 1), F32)
            for g in range(Q_PER_KV):
                h = kv * Q_PER_KV + g
                slope = jnp.where(g_col == g, 2.0 ** -(h + 1), slope)
                sink = jnp.where(g_col == g, sink_ref[h], sink)
            sc = _dot_nt(qg.astype(BF16), kk) * (HEAD_DIM ** -0.5) - slope * distf
            sc = jnp.where(mask, sc, NEG)
            o = _softmax_pv(sc, sink, vv)
            for g in range(Q_PER_KV):
                h = kv * Q_PER_KV + g
                o_ref[rs, h * HEAD_DIM:(h + 1) * HEAD_DIM] = o[g * t_s:(g + 1) * t_s]
        nk_ref[s, 0:nbuf - t_s, :] = bk[t_s:, :]
        nk_ref[s, nbuf - t_s:nbuf, :] = k_new
        nv_ref[s, 0:nbuf - t_s, :] = bv[t_s:, :]
        nv_ref[s, nbuf - t_s:nbuf, :] = v_new
        return carry

    lax.fori_loop(0, SAMPLE_SEQS, one_seq, 0, unroll=True)


def _swa_sample(sink, q, k, v, buf_k, buf_v, row0, t_s):
    n_seq = buf_k.shape[0]
    sb = SAMPLE_SEQS
    rb = sb * t_s
    b0 = row0 // rb
    rows = lambda w: pl.BlockSpec((rb, w), lambda i: (b0 + i, 0))
    bufs = pl.BlockSpec((sb, WINDOW, KV_W), lambda i: (i, 0, 0))
    return pl.pallas_call(
        functools.partial(_swa_sample_kernel, t_s=t_s),
        grid=(n_seq // sb,),
        in_specs=[pl.BlockSpec(memory_space=pltpu.SMEM), rows(Q_W), rows(KV_W), rows(KV_W), bufs, bufs],
        out_specs=[pl.BlockSpec((rb, Q_W), lambda i: (i, 0)), bufs, bufs],
        out_shape=[jax.ShapeDtypeStruct((n_seq * t_s, Q_W), F32),
                   jax.ShapeDtypeStruct(buf_k.shape, F32), jax.ShapeDtypeStruct(buf_v.shape, F32)],
        compiler_params=pltpu.CompilerParams(dimension_semantics=("parallel",)),
        name="swa_sample",
    )(sink, q, k, v, buf_k, buf_v)


def _gla_tables(chunk):
    t = np.arange(chunk)[:, None]
    u = np.arange(chunk)[None, :]
    masks = []
    w = chunk // 2
    while w >= 1:
        masks.append((t // (2 * w) == u // (2 * w)) & ((t // w) % 2 == 1) & ((u // w) % 2 == 0))
        w //= 2
    return (u <= t).astype(np.float32), np.stack(masks, 0).astype(np.float32)


def _level_exponents(b, la, w):
    C = b.shape[0]
    row = lax.broadcasted_iota(jnp.int32, b.shape, 0)
    if w >= 4:
        pieces = [jnp.broadcast_to(b[p + w - 1:p + w], (2 * w, b.shape[1])) for p in range(0, C, 2 * w)]
        ref = pieces[0] if len(pieces) == 1 else jnp.concatenate(pieces, 0)
        return jnp.where((row & w) != 0, b - ref, ref - b)
    if w == 2:
        m = row & 3
        nxt = pltpu.roll(la, C - 1, 0)
        prv = pltpu.roll(la, 1, 0)
        return jnp.where(m == 2, la, jnp.where(m == 3, la + prv, jnp.where(m == 0, nxt, 0.0)))
    return jnp.where((row & 1) != 0, la, 0.0)


def _split3(x):
    hi = x.astype(BF16)
    r1 = x - hi.astype(F32)
    mid = r1.astype(BF16)
    lo = (r1 - mid.astype(F32)).astype(BF16)
    return hi, mid, lo


def _gla_intra_kernel(g_ref, m_ref, q_ref, k_ref, v_ref, la_ref, o_ref, qe_ref, ke_ref, vt_ref, d_ref):
    C = GLA_CHUNK
    n_lvl = m_ref.shape[0]
    G = g_ref[...]
    eye = (lax.broadcasted_iota(jnp.int32, (C, C), 0) == lax.broadcasted_iota(jnp.int32, (C, C), 1)).astype(F32)
    for j in range(q_ref.shape[0] // C):
        rs = slice(j * C, (j + 1) * C)
        la = la_ref[rs, :]
        hi, mid, lo = _split3(la)
        b = _dot(G, hi) + _dot(G, mid) + _dot(G, lo)
        b_last = b[C - 1:C]
        q_all = q_ref[rs, :]
        k_all = k_ref[rs, :]
        qe_ref[0, rs, :] = (q_all * jnp.exp(b)).astype(BF16)
        ke_ref[0, rs, :] = (k_all * jnp.exp(b_last - b)).astype(BF16)
        d_ref[0, j] = jnp.broadcast_to(jnp.exp(b_last), (8, GLA_DK))
        q_lvl, k_lvl = [], []
        for l in range(n_lvl):
            El = jnp.exp(_level_exponents(b, la, C >> (l + 1)))
            q_lvl.append((q_all * El).astype(BF16))
            k_lvl.append((k_all * El).astype(BF16))
        for h in range(GLA_HEADS):
            ks = slice(h * GLA_HK, (h + 1) * GLA_HK)
            vs = slice(h * GLA_HV, (h + 1) * GLA_HV)
            v = v_ref[rs, vs]
            att = eye * jnp.sum(q_all[:, ks] * k_all[:, ks], -1, keepdims=True)
            for l in range(n_lvl):
                att = att + m_ref[l] * _dot_nt(q_lvl[l][:, ks], k_lvl[l][:, ks])
            o_ref[0, rs, vs] = _dot(att.astype(BF16), v.astype(BF16))
            vt_ref[0, j, vs, :] = v.T.astype(BF16)


def _gla_inter_kernel(o_ref, qe_ref, ke_ref, vt_ref, d_ref, og_ref, s_ref, st_ref):
    c = pl.program_id(0)
    batch = o_ref.shape[0]

    @pl.when(c == 0)
    def _():
        st_ref[...] = jnp.zeros_like(st_ref)

    for b in range(batch):
        for h in range(GLA_HEADS):
            ks = slice(h * GLA_HK, (h + 1) * GLA_HK)
            vs = slice(h * GLA_HV, (h + 1) * GLA_HV)
            st = st_ref[b * GLA_HEADS + h]
            og_ref[b, :, vs] = o_ref[b, :, vs] + _dot_nt(qe_ref[b, :, ks], st.astype(BF16))
            st_ref[b * GLA_HEADS + h] = st * d_ref[b, 0, 0:1, ks] + _dot(vt_ref[b, 0, vs, :], ke_ref[b, :, ks])

    @pl.when(c == pl.num_programs(0) - 1)
    def _():
        for b in range(batch):
            for h in range(GLA_HEADS):
                s_ref[b, h] = st_ref[b * GLA_HEADS + h].T


def _gla_prompt(gq, gk, gv, la, batch, lp):
    C = GLA_CHUNK
    nc = lp // C
    G, M = _gla_tables(C)
    G = jnp.asarray(G, BF16)
    M = jnp.asarray(M, F32)
    cps = GLA_INTRA_CHUNKS
    assert nc % cps == 0
    rows = lambda w: pl.BlockSpec((cps * C, w), lambda b, c: (b * (nc // cps) + c, 0))
    rows3 = lambda w: pl.BlockSpec((1, cps * C, w), lambda b, c: (b, c, 0))
    full = lambda a: pl.BlockSpec(a.shape, lambda b, c: (0,) * a.ndim)
    o_intra, qe, ke, vt, d = pl.pallas_call(
        _gla_intra_kernel,
        grid=(batch, nc // cps),
        in_specs=[full(G), full(M), rows(GLA_DK), rows(GLA_DK), rows(GLA_DV), rows(GLA_DK)],
        out_specs=[rows3(GLA_DV), rows3(GLA_DK), rows3(GLA_DK),
                   pl.BlockSpec((1, cps, GLA_DV, C), lambda b, c: (b, c, 0, 0)),
                   pl.BlockSpec((1, cps, 8, GLA_DK), lambda b, c: (b, c, 0, 0))],
        out_shape=[jax.ShapeDtypeStruct((batch, lp, GLA_DV), F32),
                   jax.ShapeDtypeStruct((batch, lp, GLA_DK), BF16), jax.ShapeDtypeStruct((batch, lp, GLA_DK), BF16),
                   jax.ShapeDtypeStruct((batch, nc, GLA_DV, C), BF16),
                   jax.ShapeDtypeStruct((batch, nc, 8, GLA_DK), F32)],
        compiler_params=pltpu.CompilerParams(dimension_semantics=("parallel", "parallel")),
        name="gla_intra",
    )(G, M, gq, gk, gv, la)
    chunk = lambda w: pl.BlockSpec((batch, C, w), lambda c: (0, c, 0))
    og, s_fin = pl.pallas_call(
        _gla_inter_kernel,
        grid=(nc,),
        in_specs=[chunk(GLA_DV), chunk(GLA_DK), chunk(GLA_DK),
                  pl.BlockSpec((batch, 1, GLA_DV, C), lambda c: (0, c, 0, 0)),
                  pl.BlockSpec((batch, 1, 8, GLA_DK), lambda c: (0, c, 0, 0))],
        out_specs=[chunk(GLA_DV), pl.BlockSpec((batch, GLA_HEADS, GLA_HK, GLA_HV), lambda c: (0, 0, 0, 0))],
        out_shape=[jax.ShapeDtypeStruct((batch, lp, GLA_DV), F32),
                   jax.ShapeDtypeStruct((batch, GLA_HEADS, GLA_HK, GLA_HV), F32)],
        scratch_shapes=[pltpu.VMEM((batch * GLA_HEADS, GLA_HV, GLA_HK), F32)],
        compiler_params=pltpu.CompilerParams(dimension_semantics=("arbitrary",)),
        name="gla_inter",
    )(o_intra, qe, ke, vt, d)
    return og.reshape(batch * lp, GLA_DV), s_fin


def _gla_sample_kernel(q_ref, k_ref, v_ref, la_ref, s0_ref, o_ref, s_ref, *, t_s):
    T = t_s
    row = lax.broadcasted_iota(jnp.int32, (T, GLA_HK), 0)
    k_fill = jnp.zeros((GLA_HK - T - 8, GLA_HK), F32)
    v_fill = jnp.zeros((GLA_HK - T, GLA_HV), F32)

    def one_seq(s, carry):
        rs = pl.ds(pl.multiple_of(s * T, T), T)
        for h in range(GLA_HEADS):
            ks = slice(h * GLA_HK, (h + 1) * GLA_HK)
            vs = slice(h * GLA_HV, (h + 1) * GLA_HV)
            q = q_ref[rs, ks]
            k = k_ref[rs, ks]
            v = v_ref[rs, vs]
            b = la_ref[rs, ks]
            sh = 1
            while sh < T:
                b = b + jnp.where(row >= sh, pltpu.roll(b, sh, 0), 0.0)
                sh *= 2
            S = s0_ref[s, h]
            o = _dot((q * jnp.exp(b)).astype(BF16), S.astype(BF16))
            for j in range(T):
                e = jnp.exp(jnp.where(row >= j, b - b[j:j + 1], NEG))
                a_col = jnp.sum(q * k[j:j + 1] * e, -1, keepdims=True)
                o = o + a_col * v[j:j + 1]
            o_ref[rs, vs] = o
            b_last = b[T - 1:T]
            ke = k * jnp.exp(b_last - b)
            kt = jnp.concatenate([ke, jnp.broadcast_to(jnp.exp(b_last), (8, GLA_HK)), k_fill], 0).T
            v_pad = jnp.concatenate([v, v_fill], 0)
            s_ref[s, h] = S * kt[:, T:T + 1] + _dot(kt.astype(BF16), v_pad.astype(BF16))
        return carry

    lax.fori_loop(0, SAMPLE_SEQS, one_seq, 0)


def _gla_sample(gq, gk, gv, la, s0, row0, t_s):
    n_seq = s0.shape[0]
    sb = SAMPLE_SEQS
    rb = sb * t_s
    b0 = row0 // rb
    rows = lambda w: pl.BlockSpec((rb, w), lambda i: (b0 + i, 0))
    st = pl.BlockSpec((sb, GLA_HEADS, GLA_HK, GLA_HV), lambda i: (i, 0, 0, 0))
    return pl.pallas_call(
        functools.partial(_gla_sample_kernel, t_s=t_s),
        grid=(n_seq // sb,),
        in_specs=[rows(GLA_DK), rows(GLA_DK), rows(GLA_DV), rows(GLA_DK), st],
        out_specs=[pl.BlockSpec((rb, GLA_DV), lambda i: (i, 0)), st],
        out_shape=[jax.ShapeDtypeStruct((n_seq * t_s, GLA_DV), F32), jax.ShapeDtypeStruct(s0.shape, F32)],
        compiler_params=pltpu.CompilerParams(dimension_semantics=("parallel",), vmem_limit_bytes=VMEM_LIMIT),
        name="gla_sample",
    )(gq, gk, gv, la, s0)


def _route(lt, valid):
    tm = lt.shape[1]
    el = lt[0:N_EXPERTS]
    gl = lt[N_EXPERTS:N_EXPERTS + N_GROUPS]
    g_max = jnp.max(gl, 0, keepdims=True)
    g_row = lax.broadcasted_iota(jnp.int32, (N_GROUPS, tm), 0)
    g_idx = jnp.min(jnp.where(gl == g_max, g_row, N_GROUPS), 0, keepdims=True)
    p_max = 1.0 / jnp.sum(jnp.exp(gl - g_max), 0, keepdims=True)
    e_row = lax.broadcasted_iota(jnp.int32, (N_EXPERTS, tm), 0)
    m1 = jnp.where(e_row // EXP_PER_GROUP == g_idx, el, -jnp.inf)
    v1 = jnp.max(m1, 0, keepdims=True)
    i1 = jnp.min(jnp.where(m1 == v1, e_row, N_EXPERTS), 0, keepdims=True)
    m2 = jnp.where(e_row == i1, -jnp.inf, m1)
    v2 = jnp.max(m2, 0, keepdims=True)
    i2 = jnp.min(jnp.where(m2 == v2, e_row, N_EXPERTS), 0, keepdims=True)
    e2 = jnp.exp(v2 - v1)
    w1 = p_max / (1.0 + e2)
    w2 = p_max * e2 / (1.0 + e2)
    o_row = lax.broadcasted_iota(jnp.int32, (8, tm), 0)
    ids = jnp.where(o_row == 0, i1, jnp.where(o_row == 1, i2, -1))
    return jnp.where(valid, ids, -1), jnp.where(o_row == 0, w1, jnp.where(o_row == 1, w2, 0.0))


def _store_row_tiles(ref, x):
    t, d = x.shape
    n = d // LANES
    for s in range(n):
        ref[pl.ds(s, t, stride=n), :] = x[:, s * LANES:(s + 1) * LANES]


def _load_row_tiles(ref, t, n=ROW_CHUNKS):
    return jnp.concatenate([ref[pl.ds(s, t, stride=n), :] for s in range(n)], axis=1)


def _merge_kernel(h_ref, yap_ref, yas_ref, ogp_ref, ogs_ref, gr_ref, gate_ref, valid_ref, ng_ref,
                  wa_ref, wg_ref, wo_ref, g1_ref, b1_ref, wrh_ref, wrl_ref, br_ref, u_ref,
                  h1t_ref, ids_ref, wts_ref, cnt_ref, run_ref, *, prompt_tiles):
    @pl.when(pl.program_id(0) == 0)
    def _():
        run_ref[...] = jnp.zeros_like(run_ref)

    h = h_ref[...]
    is_prompt = pl.program_id(0) < prompt_tiles
    og = jnp.where(is_prompt, ogp_ref[...], ogs_ref[...])
    ya = jnp.where(is_prompt, yap_ref[...], yas_ref[...])
    parts = []
    for hh in range(GLA_HEADS):
        o = og[:, hh * GLA_HV:(hh + 1) * GLA_HV]
        parts.append(o * lax.rsqrt(jnp.mean(o * o, -1, keepdims=True) + EPS))
    y_gla = jnp.concatenate(parts, 1) * ng_ref[...] * gr_ref[...].astype(F32)
    a = _dot(ya.astype(BF16), wa_ref[...])
    b = _dot(y_gla.astype(BF16), wg_ref[...])
    hm = gate_ref[:, :D_MODEL].astype(F32) * a + gate_ref[:, D_MODEL:].astype(F32) * b
    mix = _dot(hm.astype(BF16), wo_ref[...])
    h1 = _ln(DN_ALPHA * h + mix, g1_ref[...], b1_ref[...])
    _store_row_tiles(h1t_ref, h1)
    h_hi = h1.astype(BF16)
    h_lo = (h1 - h_hi.astype(F32)).astype(BF16)
    logits = _dot(h_hi, wrh_ref[...]) + _dot(h_lo, wrh_ref[...]) + _dot(h_hi, wrl_ref[...]) + br_ref[...]
    ids, wts_ref[...] = _route(logits.T, valid_ref[...] > 0.0)
    tm = ids.shape[1]
    e_row = lax.broadcasted_iota(jnp.int32, (N_EXPERTS, tm), 0)
    run = run_ref[:, 0:1]
    ranks = []
    for kk in range(2):
        onehot = (e_row == ids[kk:kk + 1]).astype(F32)
        before = _dot(onehot.astype(BF16), u_ref[...])
        ranks.append(jnp.sum(onehot * (run + before), 0, keepdims=True).astype(jnp.int32))
        run = run + jnp.sum(onehot, 1, keepdims=True)
    run_ref[...] = jnp.broadcast_to(run, run_ref.shape)
    o_row = lax.broadcasted_iota(jnp.int32, (8, tm), 0)
    ids_ref[...] = jnp.where(o_row == 2, ranks[0], jnp.where(o_row == 3, ranks[1], ids))
    cnt_ref[...] = run_ref[...].astype(jnp.int32)


def _merge(h, ya_p, ya_s, og_p, og_s, gr, gate, valid, ng, wa, wg, wo, g1, b1, wrh, wrl, br):
    n = h.shape[0]
    tm = ROW_TILE
    u = jnp.asarray(np.triu(np.ones((tm, tm), np.float32), 1), BF16)
    pt = ya_p.shape[0] // tm
    st = ya_s.shape[0] // tm
    row = lambda w: pl.BlockSpec((tm, w), lambda i: (i, 0))
    row_p = lambda w: pl.BlockSpec((tm, w), lambda i: (jnp.minimum(i, pt - 1), 0))
    row_s = lambda w: pl.BlockSpec((tm, w), lambda i: (jnp.clip(i - pt, 0, st - 1), 0))
    lane = lambda r: pl.BlockSpec((r, tm), lambda i: (0, i))
    full = lambda a: pl.BlockSpec(a.shape, lambda i: (0,) * a.ndim)
    return pl.pallas_call(
        functools.partial(_merge_kernel, prompt_tiles=pt),
        grid=(n // tm,),
        in_specs=[row(D_MODEL), row_p(Q_W), row_s(Q_W), row_p(GLA_DV), row_s(GLA_DV), row(GLA_DV), row(2 * D_MODEL),
                  lane(1), full(ng), full(wa), full(wg), full(wo), full(g1), full(b1),
                  full(wrh), full(wrl), full(br), full(u)],
        out_specs=[pl.BlockSpec((tm * ROW_CHUNKS, LANES), lambda i: (i, 0)), lane(8), lane(8),
                   pl.BlockSpec((N_EXPERTS, LANES), lambda i: (0, 0))],
        out_shape=[jax.ShapeDtypeStruct((n * ROW_CHUNKS, LANES), F32),
                   jax.ShapeDtypeStruct((8, n), jnp.int32), jax.ShapeDtypeStruct((8, n), F32),
                   jax.ShapeDtypeStruct((N_EXPERTS, LANES), jnp.int32)],
        scratch_shapes=[pltpu.VMEM((N_EXPERTS, LANES), F32)],
        compiler_params=pltpu.CompilerParams(dimension_semantics=("arbitrary",), vmem_limit_bytes=VMEM_LIMIT),
        name="merge",
    )(h, ya_p, ya_s, og_p, og_s, gr, gate, valid, ng, wa, wg, wo, g1, b1, wrh, wrl, br, u)


def _gather_row_tiles(idx_ref, idx0, src_hbm, dst, sem, n):
    def body(r, carry):
        t = idx_ref[idx0 + r]
        pltpu.make_async_copy(src_hbm.at[pl.ds(pl.multiple_of(t * ROW_CHUNKS, ROW_CHUNKS), ROW_CHUNKS), :],
                              dst.at[pl.ds(pl.multiple_of(r * ROW_CHUNKS, ROW_CHUNKS), ROW_CHUNKS), :], sem).start()
        return carry
    lax.fori_loop(0, n, body, 0, unroll=8)


def _wait_row_tiles(src_hbm, dst, sem, n):
    pltpu.make_async_copy(src_hbm.at[pl.ds(0, n * ROW_CHUNKS), :], dst, sem).wait()


def _tiles(ref, first, n=1):
    return ref.at[pl.ds(pl.multiple_of(first * ROW_CHUNKS, ROW_CHUNKS), n * ROW_CHUNKS), :]


def _dispatch_kernel(dest_ref, start_ref, cnt_ref, nt_ref, h_hbm, xs_hbm, buf, zbuf, sem_in, sem_out, sem_z,
                     *, ranges, n_rows, max_tiles):
    CH = DISPATCH_CHUNK
    T = FFN_TILE
    c = pl.program_id(0)
    n_chunks = pl.num_programs(0)

    def load(row0, slot):
        return pltpu.make_async_copy(_tiles(h_hbm, row0, CH), buf.at[slot], sem_in.at[slot])

    def scatter(row0, slot):
        def body(r, carry):
            for kk in range(2):
                d = dest_ref[kk * n_rows + row0 + r]
                pltpu.make_async_copy(_tiles(buf.at[slot], r), _tiles(xs_hbm, d), sem_out.at[slot]).start()
            return carry
        lax.fori_loop(0, CH, body, 0, unroll=8)

    def drain(slot):
        for _ in range(2):
            pltpu.make_async_copy(buf.at[slot], _tiles(xs_hbm, 0, CH), sem_out.at[slot]).wait()

    groups, per_group, first, stride, extra, extra_first = ranges

    def row0(j):
        in_group = first + (j // per_group) * stride + (j % per_group) * CH
        return jnp.where(j < groups * per_group, in_group, extra_first + (j - groups * per_group) * CH)

    @pl.when(c == 0)
    def _():
        for j in range(2):
            load(row0(j), j).start()

    slot = c % 3
    load(row0(c), slot).wait()
    scatter(row0(c), slot)

    @pl.when(c > 0)
    def _():
        drain((c + 2) % 3)

    @pl.when(c + 2 < n_chunks)
    def _():
        load(row0(c + 2), (c + 2) % 3).start()

    @pl.when(c == n_chunks - 1)
    def _():
        drain(slot)
        _zero_unowned_slots(start_ref, cnt_ref, nt_ref, xs_hbm, zbuf, sem_z, max_tiles)


def _zero_unowned_slots(start_ref, cnt_ref, nt_ref, xs_hbm, zbuf, sem_z, max_tiles):
    T = FFN_TILE
    zbuf[...] = jnp.zeros_like(zbuf)

    def tail_copies(e, wait):
        cnt = cnt_ref[e]
        n = (T - (cnt & (T - 1))) & (T - 1)
        first = start_ref[e] + cnt
        for bit in reversed(range(T.bit_length() - 1)):
            size = 1 << bit

            @pl.when((n & size) != 0)
            def _():
                cp = pltpu.make_async_copy(_tiles(zbuf, 0, size),
                                           _tiles(xs_hbm, first + ((n >> (bit + 1)) << (bit + 1)), size), sem_z)
                cp.wait() if wait else cp.start()

    def unused_tile(t, wait):
        cp = pltpu.make_async_copy(zbuf, _tiles(xs_hbm, t * T, T), sem_z)
        cp.wait() if wait else cp.start()

    for wait in (False, True):
        def per_expert(e, carry, wait=wait):
            tail_copies(e, wait)
            return carry

        def per_tile(t, carry, wait=wait):
            unused_tile(t, wait)
            return carry
        lax.fori_loop(0, N_EXPERTS, per_expert, 0)
        lax.fori_loop(nt_ref[0], max_tiles, per_tile, 0)


def _dispatch(dest, start, counts, n_tiles, h1t, ranges, max_tiles):
    T = FFN_TILE
    n_rows = h1t.shape[0] // ROW_CHUNKS
    return pl.pallas_call(
        functools.partial(_dispatch_kernel, ranges=ranges, n_rows=n_rows, max_tiles=max_tiles),
        grid_spec=pltpu.PrefetchScalarGridSpec(
            num_scalar_prefetch=4,
            grid=(ranges[0] * ranges[1] + ranges[4],),
            in_specs=[pl.BlockSpec(memory_space=pl.ANY)],
            out_specs=pl.BlockSpec(memory_space=pl.ANY),
            scratch_shapes=[pltpu.VMEM((3, DISPATCH_CHUNK * ROW_CHUNKS, LANES), F32),
                            pltpu.VMEM((T * ROW_CHUNKS, LANES), F32),
                            pltpu.SemaphoreType.DMA((3,)), pltpu.SemaphoreType.DMA((3,)), pltpu.SemaphoreType.DMA]),
        out_shape=jax.ShapeDtypeStruct((max_tiles * T * ROW_CHUNKS, LANES), F32),
        compiler_params=pltpu.CompilerParams(dimension_semantics=("arbitrary",)),
        name="dispatch",
    )(dest, start, counts, n_tiles, h1t)


def _ffn_kernel(te_ref, nxt_ref, nt_ref, *refs):
    x_refs = refs[:FFN_STREAMS]
    wg_hbm, wu_hbm, wd_hbm, ys_hbm, stage_g, stage_u, stage_d, sem, wgb, wub, wdb, ybuf, sem_y = refs[FFN_STREAMS:]
    i = pl.program_id(0)
    nt = nt_ref[0]
    T = FFN_TILE
    part = T * ROW_CHUNKS // FFN_STREAMS

    def store(tile, slot):
        for p in range(FFN_STREAMS):
            pltpu.make_async_copy(ybuf.at[slot, pl.ds(p * part, part), :],
                                  ys_hbm.at[pl.ds(pl.multiple_of((tile * FFN_STREAMS + p) * part, part), part), :],
                                  sem_y.at[slot]).start()

    def wait_store(slot):
        pltpu.make_async_copy(ybuf.at[slot], _tiles(ys_hbm, 0, T), sem_y.at[slot]).wait()

    def stage(e):
        return [pltpu.make_async_copy(src.at[e], dst, sem.at[n])
                for n, (src, dst) in enumerate(((wg_hbm, stage_g), (wu_hbm, stage_u), (wd_hbm, stage_d)))]

    @pl.when(i == 0)
    def _():
        for cp in stage(te_ref[0]):
            cp.start()

    @pl.when(i < nt)
    def _():
        @pl.when((i == 0) | (te_ref[i] != te_ref[jnp.maximum(i - 1, 0)]))
        def _():
            for cp in stage(te_ref[i]):
                cp.wait()
            wgb[...] = stage_g[...].astype(BF16)
            wub[...] = stage_u[...].astype(BF16)
            wdb[...] = stage_d[...].astype(BF16)

            @pl.when(nxt_ref[i] >= 0)
            def _():
                for cp in stage(nxt_ref[i]):
                    cp.start()

        x = jnp.concatenate([_load_row_tiles(r, T // FFN_STREAMS) for r in x_refs], 0).astype(BF16)
        g = _dot(x, wgb[...])
        u = _dot(x, wub[...])
        y = _dot((g * _sigmoid(g) * u).astype(BF16), wdb[...])
        slot = i % 2

        @pl.when(i >= 2)
        def _():
            wait_store(slot)
        _store_row_tiles(ybuf.at[slot], y)
        store(i, slot)

        @pl.when(i == nt - 1)
        def _():
            wait_store(slot)

            @pl.when(i >= 1)
            def _():
                wait_store(1 - slot)

    @pl.when(i >= nt)
    def _():
        ybuf[0] = jnp.zeros(ybuf.shape[1:], F32)
        store(i, 0)
        wait_store(0)


def _ffn(tile_expert, next_expert, n_tiles, xs, w_g, w_u, w_d):
    T = FFN_TILE
    max_tiles = tile_expert.shape[0]
    hbm = pl.BlockSpec(memory_space=pl.ANY)
    part = T * ROW_CHUNKS // FFN_STREAMS

    def x_part(p):
        return pl.BlockSpec((part, LANES), lambda i, te, nxt, nt: (jnp.minimum(i, nt[0] - 1) * FFN_STREAMS + p, 0))

    return pl.pallas_call(
        _ffn_kernel,
        grid_spec=pltpu.PrefetchScalarGridSpec(
            num_scalar_prefetch=3,
            grid=(max_tiles,),
            in_specs=[x_part(p) for p in range(FFN_STREAMS)] + [hbm, hbm, hbm],
            out_specs=hbm,
            scratch_shapes=[pltpu.VMEM(w_g.shape[1:], F32), pltpu.VMEM(w_u.shape[1:], F32),
                            pltpu.VMEM(w_d.shape[1:], F32), pltpu.SemaphoreType.DMA((3,)),
                            pltpu.VMEM(w_g.shape[1:], BF16), pltpu.VMEM(w_u.shape[1:], BF16),
                            pltpu.VMEM(w_d.shape[1:], BF16),
                            pltpu.VMEM((2, T * ROW_CHUNKS, LANES), F32), pltpu.SemaphoreType.DMA((2,))]),
        out_shape=jax.ShapeDtypeStruct(xs.shape, F32),
        compiler_params=pltpu.CompilerParams(dimension_semantics=("arbitrary",)),
        name="ffn",
    )(tile_expert, next_expert, n_tiles, *([xs] * FFN_STREAMS), w_g, w_u, w_d)


def _combine_kernel(dest_ref, h_ref, w_ref, g_ref, b_ref, ys_hbm, y_ref, buf, sem, *, blk, n_rows):
    i = pl.program_id(0)
    n = pl.num_programs(0)
    T = CMB_TILE

    def gather(step, slot):
        for kk in range(2):
            _gather_row_tiles(dest_ref, kk * n_rows + blk(step) * T, ys_hbm, buf.at[slot, kk], sem.at[slot], T)

    @pl.when(i == 0)
    def _():
        gather(0, 0)

    @pl.when(i + 1 < n)
    def _():
        gather(i + 1, (i + 1) % 2)

    slot = i % 2
    for kk in range(2):
        _wait_row_tiles(ys_hbm, buf.at[slot, kk], sem.at[slot], T)
    w = w_ref[...]
    ff = w[:, 0:1] * _load_row_tiles(buf.at[slot, 0], T) + w[:, 1:2] * _load_row_tiles(buf.at[slot, 1], T)
    y_ref[...] = _ln(DN_ALPHA * _load_row_tiles(h_ref, T) + ff, g_ref[...], b_ref[...])


def _combine(dest, h1t, wts_t, ys, g2, b2, n_out, first_block, blocks_per_batch, skip_blocks):
    T = CMB_TILE
    n_rows = h1t.shape[0] // ROW_CHUNKS
    if skip_blocks:
        blk = lambda i: first_block + i + (i // blocks_per_batch + 1) * skip_blocks
    else:
        blk = lambda i: first_block + i
    full = lambda a: pl.BlockSpec(a.shape, lambda i, d: (0,) * a.ndim)
    return pl.pallas_call(
        functools.partial(_combine_kernel, blk=blk, n_rows=n_rows),
        grid_spec=pltpu.PrefetchScalarGridSpec(
            num_scalar_prefetch=1,
            grid=(n_out // T,),
            in_specs=[pl.BlockSpec((T * ROW_CHUNKS, LANES), lambda i, d: (blk(i), 0)),
                      pl.BlockSpec((T, 2), lambda i, d: (blk(i), 0)),
                      full(g2), full(b2), pl.BlockSpec(memory_space=pl.ANY)],
            out_specs=pl.BlockSpec((T, D_MODEL), lambda i, d: (i, 0)),
            scratch_shapes=[pltpu.VMEM((2, 2, T * ROW_CHUNKS, LANES), F32), pltpu.SemaphoreType.DMA((2,))]),
        out_shape=jax.ShapeDtypeStruct((n_out, D_MODEL), F32),
        compiler_params=pltpu.CompilerParams(dimension_semantics=("arbitrary",)),
        name="combine",
    )(dest, h1t, wts_t, g2, b2, ys)


def _dispatch_plan(routing, counts, max_tiles):
    T = FFN_TILE
    tiles_e = (counts + T - 1) // T
    tile_end = jnp.cumsum(tiles_e)
    n_tiles = tile_end[-1]
    start = (tile_end - tiles_e) * T
    ids, rank = routing[0:2], routing[2:4]
    onehot = (ids[..., None] == jnp.arange(N_EXPERTS, dtype=jnp.int32)).astype(jnp.int32)
    dest = (jnp.sum(onehot * start, axis=-1) + jnp.where(ids >= 0, rank, 0)).reshape(-1).astype(jnp.int32)
    experts = jnp.arange(N_EXPERTS, dtype=jnp.int32)
    tiles = jnp.arange(max_tiles, dtype=jnp.int32)
    te = jnp.sum((tiles[:, None] >= tile_end[None, :]).astype(jnp.int32), axis=1)
    te_last = jnp.max(jnp.where(counts > 0, experts, 0))
    te = jnp.where(tiles < n_tiles, te, te_last).astype(jnp.int32)
    later = jnp.where((counts > 0)[None, :] & (experts[None, :] > te[:, None]), experts[None, :], N_EXPERTS)
    nxt = jnp.min(later, axis=1)
    nxt = jnp.where(nxt < N_EXPERTS, nxt, -1).astype(jnp.int32)
    return te, nxt, n_tiles.reshape(1).astype(jnp.int32), start.astype(jnp.int32), dest


def kernel(x_prompt, x_sample, state_swa_k, state_swa_v, state_gla, meta_tokens, ln_emb_g, ln_emb_b, w_in, b_gate, attn_sink, w_alpha2, b_alpha, gla_norm_g, w_attn_br, w_gla_br, w_out, ln1_g, ln1_b, w_router_group, b_router_group, w_router_expert, b_router_expert, w_exp_gate, w_exp_up, w_exp_down, ln2_g, ln2_b):
    B, seq, _ = x_prompt.shape
    n_seq, t_s, _ = x_sample.shape
    depth = w_in.shape[0]
    assert depth == 1 and seq % ATT_BLOCK == 0 and t_s == 8 and SKIP_ROWS == ATT_BLOCK == WINDOW
    lp = SKIP_ROWS + seq
    NP, NS = B * lp, n_seq * t_s
    NR = NP + NS
    assert NP % ROW_TILE == 0 and NS % ROW_TILE == 0 and n_seq % SAMPLE_SEQS == 0
    assert seq % DISPATCH_CHUNK == 0 and NS % DISPATCH_CHUNK == 0
    l = 0
    row2 = lambda a: a.reshape(1, -1)

    head = jnp.concatenate([jnp.zeros((FRONT_PAD, D_MODEL), F32), meta_tokens], axis=0)
    pos = np.arange(NR)
    moe_valid = jnp.asarray(~((pos < NP) & (pos % lp < SKIP_ROWS)), F32).reshape(1, NR)

    wi = w_in[l]
    c_ga = sum((Q_W, KV_W, KV_W, GLA_DK, GLA_DK, GLA_DV))
    assert c_ga == W_IN_SPLIT
    w_pieces = (wi[:, :c_ga].astype(BF16), wi[:, c_ga + GLA_RANK:].astype(BF16),
                jnp.pad(wi[:, c_ga:c_ga + GLA_RANK], ((0, 0), (0, 128 - GLA_RANK))).astype(BF16))
    wa2_bf = jnp.concatenate([w_alpha2[l], jnp.zeros((128 - GLA_RANK, GLA_DK), F32)], axis=0).astype(BF16)

    h, q, k, v, gq, gk, gv, la, gr, gate = _inproj(x_prompt, x_sample.reshape(NS, D_MODEL), head, row2(ln_emb_g),
                                                   row2(ln_emb_b), w_pieces, wa2_bf, row2(b_alpha[l]),
                                                   row2(b_gate[l]))

    sink = attn_sink[l]
    ya_p, k_p, v_p = _swa_prompt(sink, q, k, v, B, lp)
    buf_k = state_swa_k[l].reshape(n_seq, WINDOW, KV_W)
    buf_v = state_swa_v[l].reshape(n_seq, WINDOW, KV_W)
    ya_s, nk_s, nv_s = _swa_sample(sink, q, k, v, buf_k, buf_v, NP, t_s)

    og_p, s_p = _gla_prompt(gq, gk, gv, la, B, lp)
    og_s, s_s = _gla_sample(gq, gk, gv, la, state_gla[l], NP, t_s)

    wr = jnp.concatenate([w_router_expert[l], w_router_group[l],
                          jnp.zeros((D_MODEL, LANES - N_EXPERTS - N_GROUPS), F32)], axis=1)
    br = jnp.concatenate([b_router_expert[l], b_router_group[l],
                          jnp.zeros((LANES - N_EXPERTS - N_GROUPS,), F32)]).reshape(1, LANES)
    wr_hi = wr.astype(BF16)
    h1t, routing, wts, counts = _merge(h, ya_p, ya_s, og_p, og_s, gr, gate, moe_valid,
                                       row2(gla_norm_g[l]),
                                       w_attn_br[l].astype(BF16), w_gla_br[l].astype(BF16), w_out[l].astype(BF16),
                                       row2(ln1_g[l]), row2(ln1_b[l]), wr_hi, (wr - wr_hi.astype(F32)).astype(BF16), br)

    n_tok = B * seq + NS
    max_tiles = (2 * n_tok) // FFN_TILE + N_EXPERTS
    counts = counts[:, 0]
    te, nxt, n_tiles, start, dest = _dispatch_plan(routing[0:4], counts, max_tiles)
    routed = (B, seq // DISPATCH_CHUNK, SKIP_ROWS, lp, NS // DISPATCH_CHUNK, NP)
    xs = _dispatch(dest, start, counts, n_tiles, h1t, routed, max_tiles)
    ys = _ffn(te, nxt, n_tiles, xs, w_exp_gate[l], w_exp_up[l], w_exp_down[l])

    wts_t = wts[0:2].T
    g2, b2 = row2(ln2_g[l]), row2(ln2_b[l])
    skip_blocks = SKIP_ROWS // CMB_TILE
    y_p = _combine(dest, h1t, wts_t, ys, g2, b2, B * seq, 0, seq // CMB_TILE, skip_blocks)
    y_s = _combine(dest, h1t, wts_t, ys, g2, b2, NS, NP // CMB_TILE, 1, 0)

    kv_shape = (1, B, WINDOW, N_KV, HEAD_DIM)
    return (y_p.reshape(B, seq, D_MODEL), y_s.reshape(n_seq, t_s, D_MODEL),
            k_p.reshape(kv_shape), v_p.reshape(kv_shape), s_p[None],
            nk_s.reshape(1, n_seq, WINDOW, N_KV, HEAD_DIM), nv_s.reshape(1, n_seq, WINDOW, N_KV, HEAD_DIM),
            s_s[None])
```

```python
import functools

import numpy as np
import jax
import jax.numpy as jnp
from jax import lax
from jax.experimental import pallas as pl
from jax.experimental.pallas import tpu as pltpu

F32 = jnp.float32
BF16 = jnp.bfloat16

D_MODEL = 1024
N_META = 16
HEAD_DIM = 64
N_HEADS = 8
N_KV = 2
Q_PER_KV = 4
WINDOW = 128
ATT_BLOCK = 128
GLA_HEADS = 4
GLA_HK = 128
GLA_HV = 256
GLA_DK = GLA_HEADS * GLA_HK
GLA_DV = GLA_HEADS * GLA_HV
GLA_RANK = 16
GLA_TAU = 16.0
GLA_CHUNK = 64
N_GROUPS = 4
EXP_PER_GROUP = 8
N_EXPERTS = 32
D_EXPERT = 256
DN_ALPHA = 2.0 ** 0.25
EPS = 1e-5
NEG = -1e30

FRONT_PAD = (-N_META) % ATT_BLOCK
SKIP_ROWS = FRONT_PAD + N_META

Q_W, KV_W = N_HEADS * HEAD_DIM, N_KV * HEAD_DIM
SEG = {}
_o = 0
for _n, _w in (("q", Q_W), ("k", KV_W), ("v", KV_W), ("gq", GLA_DK), ("gk", GLA_DK), ("gv", GLA_DV)):
    SEG[_n] = (0, _o, _o + _w)
    _o += _w
W_IN_SPLIT = _o
SEG["gr"] = (1, 0, GLA_DV)
SEG["gate"] = (1, GLA_DV, GLA_DV + 2 * D_MODEL)
SEG["ga"] = (2, 0, 128)

ROW_TILE = 512
FFN_TILE = 256
CMB_TILE = 128
DISPATCH_CHUNK = 128
GLA_INTRA_CHUNKS = 6
SWA_BLOCKS = 1
INPROJ_TILE = 512
SAMPLE_SEQS = 8
LANES = 128
ROW_CHUNKS = D_MODEL // LANES
VMEM_LIMIT = 56 * 1024 * 1024


def _ln(x, g, b):
    mu = jnp.mean(x, -1, keepdims=True)
    xc = x - mu
    var = jnp.mean(xc * xc, -1, keepdims=True)
    return xc * lax.rsqrt(var + EPS) * g + b


def _sigmoid(x):
    return 0.5 * jnp.tanh(0.5 * x) + 0.5


def _dot(a, b):
    return jnp.dot(a, b, preferred_element_type=F32)


def _dot_nt(a, b):
    return lax.dot_general(a, b, (((1,), (1,)), ((), ())), preferred_element_type=F32)


def _inproj_kernel(*refs, blocks, batch_blocks, prompt_blocks):
    xp_refs, xs_refs = refs[:blocks], refs[blocks:2 * blocks]
    (head_ref, g_ref, b_ref, w0_ref, w1_ref, w2_ref, wa2_ref, ba_ref, bg_ref,
     h_ref, q_ref, k_ref, v_ref, gq_ref, gk_ref, gv_ref, la_ref, gr_ref, gate_ref) = refs[2 * blocks:]
    w_refs = (w0_ref, w1_ref, w2_ref)
    row = lax.broadcasted_iota(jnp.int32, (ATT_BLOCK, 1), 0)
    xs, keeps = [], []
    for s in range(blocks):
        p = pl.program_id(0) * blocks + s
        is_sample = p >= prompt_blocks
        is_head = jnp.logical_and(jnp.logical_not(is_sample), p % batch_blocks == 0)
        xs.append(jnp.where(is_sample, xs_refs[s][...], jnp.where(is_head, head_ref[...], xp_refs[s][...])))
        keeps.append(jnp.where(jnp.logical_and(is_head, row < FRONT_PAD), 0.0, 1.0))
    h = _ln(jnp.concatenate(xs, 0), g_ref[...], b_ref[...])
    h_ref[...] = h
    hb = h.astype(BF16)
    keep = jnp.concatenate(keeps, 0)

    def seg(name):
        piece, a, b = SEG[name]
        return _dot(hb, w_refs[piece][:, a:b])

    q_ref[...] = seg("q")
    k_ref[...] = seg("k")
    v_ref[...] = seg("v")
    gq_ref[...] = seg("gq") * (GLA_HK ** -0.5)
    gk_ref[...] = seg("gk") * keep
    gv_ref[...] = seg("gv") * keep
    gr = seg("gr")
    gr_ref[...] = (gr * _sigmoid(gr)).astype(BF16)
    gate_ref[...] = _sigmoid(seg("gate") + bg_ref[...]).astype(BF16)
    z = _dot(seg("ga").astype(BF16), wa2_ref[...]) + ba_ref[...]
    la = (jnp.minimum(z, 0.0) - jnp.log(1.0 + jnp.exp(-jnp.abs(z)))) * (1.0 / GLA_TAU)
    la_ref[...] = la * keep


def _inproj(x_prompt, x_sample, head, ln_g, ln_b, w_pieces, wa2_bf, b_alpha, b_gate):
    B, seq, _ = x_prompt.shape
    blk = ATT_BLOCK
    seq_blocks = seq // blk
    batch_blocks = seq_blocks + 1
    prompt_blocks = B * batch_blocks
    sample_blocks = x_sample.shape[0] // blk
    n = (prompt_blocks + sample_blocks) * blk
    tm = INPROJ_TILE
    blocks = tm // blk
    xp = x_prompt.reshape(B * seq, D_MODEL)

    def prompt_block(s):
        def imap(i):
            p = jnp.minimum(i * blocks + s, prompt_blocks - 1)
            return (p // batch_blocks * seq_blocks + jnp.maximum(p % batch_blocks - 1, 0), 0)
        return pl.BlockSpec((blk, D_MODEL), imap)

    def sample_block(s):
        return pl.BlockSpec((blk, D_MODEL), lambda i: (jnp.clip(i * blocks + s - prompt_blocks, 0, sample_blocks - 1), 0))

    widths = [D_MODEL, Q_W, KV_W, KV_W, GLA_DK, GLA_DK, GLA_DV, GLA_DK, GLA_DV, 2 * D_MODEL]
    row = lambda w: pl.BlockSpec((tm, w), lambda i: (i, 0))
    const = lambda a: pl.BlockSpec(a.shape, lambda i: (0,) * a.ndim, pipeline_mode=pl.Buffered(1))
    return pl.pallas_call(
        functools.partial(_inproj_kernel, blocks=blocks, batch_blocks=batch_blocks, prompt_blocks=prompt_blocks),
        grid=(n // tm,),
        in_specs=[prompt_block(s) for s in range(blocks)] + [sample_block(s) for s in range(blocks)]
                 + [const(a) for a in (head, ln_g, ln_b, *w_pieces, wa2_bf, b_alpha, b_gate)],
        out_specs=[row(w) for w in widths],
        out_shape=[jax.ShapeDtypeStruct((n, w), BF16 if i >= len(widths) - 2 else F32) for i, w in enumerate(widths)],
        compiler_params=pltpu.CompilerParams(dimension_semantics=("parallel",), vmem_limit_bytes=VMEM_LIMIT),
        name="inproj",
    )(*([xp] * blocks), *([x_sample] * blocks), head, ln_g, ln_b, *w_pieces, wa2_bf, b_alpha, b_gate)


def _softmax_pv(s, sink, vv):
    m = jnp.maximum(jnp.max(s, -1, keepdims=True), sink)
    p = jnp.exp(s - m)
    l = jnp.sum(p, -1, keepdims=True) + jnp.exp(sink - m)
    return _dot(p.astype(BF16), vv) / l


def _swa_prompt_kernel(sink_ref, bias_ref, q_ref, kp_ref, kc_ref, vp_ref, vc_ref, o_ref, kl_ref, vl_ref):
    nq = q_ref.shape[0] // ATT_BLOCK
    q_all = q_ref[...] * (HEAD_DIM ** -0.5)
    kb = jnp.concatenate([kp_ref[...], kc_ref[...]], 0)
    vb = jnp.concatenate([vp_ref[...], vc_ref[...]], 0)
    low = lax.broadcasted_iota(jnp.int32, (1, 2 * HEAD_DIM), 1) < HEAD_DIM
    k_low = jnp.where(low, kb, 0.0)
    k_high = jnp.where(low, 0.0, kb)
    keys = {(0, 0): k_low.astype(BF16), (0, 1): pltpu.roll(k_low, HEAD_DIM, 1).astype(BF16),
            (1, 0): pltpu.roll(k_high, HEAD_DIM, 1).astype(BF16), (1, 1): k_high.astype(BF16)}
    ones_col = (lax.broadcasted_iota(jnp.int32, (1, 2 * HEAD_DIM), 1) == HEAD_DIM).astype(F32)
    values = [jnp.where(low, vb, ones_col).astype(BF16),
              jnp.where(low, pltpu.roll(vb, HEAD_DIM, 1), ones_col).astype(BF16)]
    for blk in range(nq):
        rows = slice(blk * ATT_BLOCK, (blk + 1) * ATT_BLOCK)
        band = slice(blk * ATT_BLOCK, (blk + 2) * ATT_BLOCK)
        variant = jnp.minimum(pl.program_id(1) * nq + blk, 2)
        for pair in range(N_HEADS // 2):
            qp = q_all[rows, pair * 2 * HEAD_DIM:(pair + 1) * 2 * HEAD_DIM]
            outs = []
            for half in range(2):
                h = 2 * pair + half
                kv = h // Q_PER_KV
                qm = jnp.where(low if half == 0 else jnp.logical_not(low), qp, 0.0).astype(BF16)
                s = _dot_nt(qm, keys[(kv, half)][band]) + bias_ref[variant, h]
                m = jnp.maximum(jnp.max(s, -1, keepdims=True), sink_ref[h])
                pv = _dot(jnp.exp(s - m).astype(BF16), values[kv][band])
                outs.append(pv / (pv[:, HEAD_DIM:HEAD_DIM + 1] + jnp.exp(sink_ref[h] - m)))
            o_ref[rows, pair * 2 * HEAD_DIM:(pair + 1) * 2 * HEAD_DIM] = jnp.where(low, outs[0],
                                                                                   pltpu.roll(outs[1], HEAD_DIM, 1))

    @pl.when(pl.program_id(1) == pl.num_programs(1) - 1)
    def _():
        kl_ref[0] = kc_ref[(nq - 1) * ATT_BLOCK:nq * ATT_BLOCK, :]
        vl_ref[0] = vc_ref[(nq - 1) * ATT_BLOCK:nq * ATT_BLOCK, :]


def _swa_bias_table():
    r = np.arange(ATT_BLOCK)[:, None]
    c = np.arange(2 * ATT_BLOCK)[None, :]
    dist = r - c + ATT_BLOCK
    slopes = 2.0 ** -(np.arange(N_HEADS) + 1.0)
    table = np.empty((3, N_HEADS, ATT_BLOCK, 2 * ATT_BLOCK), np.float32)
    for j in range(3):
        seen = (dist >= 0) & (dist < WINDOW) & ((j - 1) * ATT_BLOCK + c - FRONT_PAD >= 0)
        table[j] = np.where(seen[None], -slopes[:, None, None] * dist[None], NEG)
    return table


def _swa_prompt(sink, q, k, v, batch, lp):
    nq = SWA_BLOCKS
    nb = lp // ATT_BLOCK
    assert nb >= 3 and nb % nq == 0
    steps = nb // nq
    n = batch * lp
    bias = jnp.asarray(_swa_bias_table())
    cur = lambda w: pl.BlockSpec((nq * ATT_BLOCK, w), lambda b, j: (b * steps + j, 0))
    prev = lambda w: pl.BlockSpec((ATT_BLOCK, w), lambda b, j: (b * nb + jnp.maximum(j * nq - 1, 0), 0))
    last = pl.BlockSpec((1, ATT_BLOCK, KV_W), lambda b, j: (b, 0, 0))
    return pl.pallas_call(
        _swa_prompt_kernel,
        grid=(batch, steps),
        in_specs=[pl.BlockSpec(memory_space=pltpu.SMEM), pl.BlockSpec(bias.shape, lambda b, j: (0, 0, 0, 0)),
                  cur(Q_W), prev(KV_W), cur(KV_W), prev(KV_W), cur(KV_W)],
        out_specs=[cur(Q_W), last, last],
        out_shape=[jax.ShapeDtypeStruct((n, Q_W), F32), jax.ShapeDtypeStruct((batch, ATT_BLOCK, KV_W), F32),
                   jax.ShapeDtypeStruct((batch, ATT_BLOCK, KV_W), F32)],
        compiler_params=pltpu.CompilerParams(dimension_semantics=("parallel", "arbitrary")),
        name="swa_prompt",
    )(sink, bias, q, k, k, v, v)


def _swa_sample_kernel(sink_ref, q_ref, k_ref, v_ref, bk_ref, bv_ref, o_ref, nk_ref, nv_ref, *, t_s):
    nbuf = WINDOW
    span = 2 * WINDOW
    rows = Q_PER_KV * t_s
    r = lax.broadcasted_iota(jnp.int32, (rows, span), 0)
    c = lax.broadcasted_iota(jnp.int32, (rows, span), 1)
    t = r % t_s
    dist = t + nbuf - c
    mask = (dist >= 0) & (dist < WINDOW) & (c < nbuf + t_s)
    distf = dist.astype(F32)
    g_col = lax.broadcasted_iota(jnp.int32, (rows, 1), 0) // t_s
    fill = jnp.zeros((span - nbuf - t_s, KV_W), F32)

    def one_seq(s, carry):
        rs = pl.ds(pl.multiple_of(s * t_s, t_s), t_s)
        q = q_ref[rs, :]
        k_new = k_ref[rs, :]
        v_new = v_ref[rs, :]
        bk = bk_ref[s]
        bv = bv_ref[s]
        k_all = jnp.concatenate([bk, k_new, fill], 0)
        v_all = jnp.concatenate([bv, v_new, fill], 0)
        for kv in range(N_KV):
            qg = jnp.concatenate(
                [q[:, (kv * Q_PER_KV + g) * HEAD_DIM:(kv * Q_PER_KV + g + 1) * HEAD_DIM] for g in range(Q_PER_KV)], 0)
            kk = k_all[:, kv * HEAD_DIM:(kv + 1) * HEAD_DIM].astype(BF16)
            vv = v_all[:, kv * HEAD_DIM:(kv + 1) * HEAD_DIM].astype(BF16)
            slope = jnp.zeros((rows, 1), F32)
            sink = jnp.zeros((rows, 1), F32)
            for g in range(Q_PER_KV):
                h = kv * Q_PER_KV + g
                slope = jnp.where(g_col == g, 2.0 ** -(h + 1), slope)
                sink = jnp.where(g_col == g, sink_ref[h], sink)
            sc = _dot_nt(qg.astype(BF16), kk) * (HEAD_DIM ** -0.5) - slope * distf
            sc = jnp.where(mask, sc, NEG)
            o = _softmax_pv(sc, sink, vv)
            for g in range(Q_PER_KV):
                h = kv * Q_PER_KV + g
                o_ref[rs, h * HEAD_DIM:(h + 1) * HEAD_DIM] = o[g * t_s:(g + 1) * t_s]
        nk_ref[s, 0:nbuf - t_s, :] = bk[t_s:, :]
        nk_ref[s, nbuf - t_s:nbuf, :] = k_new
        nv_ref[s, 0:nbuf - t_s, :] = bv[t_s:, :]
        nv_ref[s, nbuf - t_s:nbuf, :] = v_new
        return carry

    lax.fori_loop(0, SAMPLE_SEQS, one_seq, 0, unroll=True)


def _swa_sample(sink, q, k, v, buf_k, buf_v, row0, t_s):
    n_seq = buf_k.shape[0]
    sb = SAMPLE_SEQS
    rb = sb * t_s
    b0 = row0 // rb
    rows = lambda w: pl.BlockSpec((rb, w), lambda i: (b0 + i, 0))
    bufs = pl.BlockSpec((sb, WINDOW, KV_W), lambda i: (i, 0, 0))
    return pl.pallas_call(
        functools.partial(_swa_sample_kernel, t_s=t_s),
        grid=(n_seq // sb,),
        in_specs=[pl.BlockSpec(memory_space=pltpu.SMEM), rows(Q_W), rows(KV_W), rows(KV_W), bufs, bufs],
        out_specs=[pl.BlockSpec((rb, Q_W), lambda i: (i, 0)), bufs, bufs],
        out_shape=[jax.ShapeDtypeStruct((n_seq * t_s, Q_W), F32),
                   jax.ShapeDtypeStruct(buf_k.shape, F32), jax.ShapeDtypeStruct(buf_v.shape, F32)],
        compiler_params=pltpu.CompilerParams(dimension_semantics=("parallel",)),
        name="swa_sample",
    )(sink, q, k, v, buf_k, buf_v)


def _gla_tables(chunk):
    t = np.arange(chunk)[:, None]
    u = np.arange(chunk)[None, :]
    masks = []
    w = chunk // 2
    while w >= 1:
        masks.append((t // (2 * w) == u // (2 * w)) & ((t // w) % 2 == 1) & ((u // w) % 2 == 0))
        w //= 2
    return (u <= t).astype(np.float32), np.stack(masks, 0).astype(np.float32)


def _level_exponents(b, la, w):
    C = b.shape[0]
    row = lax.broadcasted_iota(jnp.int32, b.shape, 0)
    if w >= 4:
        pieces = [jnp.broadcast_to(b[p + w - 1:p + w], (2 * w, b.shape[1])) for p in range(0, C, 2 * w)]
        ref = pieces[0] if len(pieces) == 1 else jnp.concatenate(pieces, 0)
        return jnp.where((row & w) != 0, b - ref, ref - b)
    if w == 2:
        m = row & 3
        nxt = pltpu.roll(la, C - 1, 0)
        prv = pltpu.roll(la, 1, 0)
        return jnp.where(m == 2, la, jnp.where(m == 3, la + prv, jnp.where(m == 0, nxt, 0.0)))
    return jnp.where((row & 1) != 0, la, 0.0)


def _split3(x):
    hi = x.astype(BF16)
    r1 = x - hi.astype(F32)
    mid = r1.astype(BF16)
    lo = (r1 - mid.astype(F32)).astype(BF16)
    return hi, mid, lo


def _gla_intra_kernel(g_ref, m_ref, q_ref, k_ref, v_ref, la_ref, o_ref, qe_ref, ke_ref, vt_ref, d_ref):
    C = GLA_CHUNK
    n_lvl = m_ref.shape[0]
    G = g_ref[...]
    eye = (lax.broadcasted_iota(jnp.int32, (C, C), 0) == lax.broadcasted_iota(jnp.int32, (C, C), 1)).astype(F32)
    for j in range(q_ref.shape[0] // C):
        rs = slice(j * C, (j + 1) * C)
        la = la_ref[rs, :]
        hi, mid, lo = _split3(la)
        b = _dot(G, hi) + _dot(G, mid) + _dot(G, lo)
        b_last = b[C - 1:C]
        q_all = q_ref[rs, :]
        k_all = k_ref[rs, :]
        qe_ref[0, rs, :] = (q_all * jnp.exp(b)).astype(BF16)
        ke_ref[0, rs, :] = (k_all * jnp.exp(b_last - b)).astype(BF16)
        d_ref[0, j] = jnp.broadcast_to(jnp.exp(b_last), (8, GLA_DK))
        q_lvl, k_lvl = [], []
        for l in range(n_lvl):
            El = jnp.exp(_level_exponents(b, la, C >> (l + 1)))
            q_lvl.append((q_all * El).astype(BF16))
            k_lvl.append((k_all * El).astype(BF16))
        for h in range(GLA_HEADS):
            ks = slice(h * GLA_HK, (h + 1) * GLA_HK)
            vs = slice(h * GLA_HV, (h + 1) * GLA_HV)
            v = v_ref[rs, vs]
            att = eye * jnp.sum(q_all[:, ks] * k_all[:, ks], -1, keepdims=True)
            for l in range(n_lvl):
                att = att + m_ref[l] * _dot_nt(q_lvl[l][:, ks], k_lvl[l][:, ks])
            o_ref[0, rs, vs] = _dot(att.astype(BF16), v.astype(BF16))
            vt_ref[0, j, vs, :] = v.T.astype(BF16)


def _gla_inter_kernel(o_ref, qe_ref, ke_ref, vt_ref, d_ref, og_ref, s_ref, st_ref):
    c = pl.program_id(0)
    batch = o_ref.shape[0]

    @pl.when(c == 0)
    def _():
        st_ref[...] = jnp.zeros_like(st_ref)

    for b in range(batch):
        for h in range(GLA_HEADS):
            ks = slice(h * GLA_HK, (h + 1) * GLA_HK)
            vs = slice(h * GLA_HV, (h + 1) * GLA_HV)
            st = st_ref[b * GLA_HEADS + h]
            og_ref[b, :, vs] = o_ref[b, :, vs] + _dot_nt(qe_ref[b, :, ks], st.astype(BF16))
            st_ref[b * GLA_HEADS + h] = st * d_ref[b, 0, 0:1, ks] + _dot(vt_ref[b, 0, vs, :], ke_ref[b, :, ks])

    @pl.when(c == pl.num_programs(0) - 1)
    def _():
        for b in range(batch):
            for h in range(GLA_HEADS):
                s_ref[b, h] = st_ref[b * GLA_HEADS + h].T


def _gla_prompt(gq, gk, gv, la, batch, lp):
    C = GLA_CHUNK
    nc = lp // C
    G, M = _gla_tables(C)
    G = jnp.asarray(G, BF16)
    M = jnp.asarray(M, F32)
    cps = GLA_INTRA_CHUNKS
    assert nc % cps == 0
    rows = lambda w: pl.BlockSpec((cps * C, w), lambda b, c: (b * (nc // cps) + c, 0))
    rows3 = lambda w: pl.BlockSpec((1, cps * C, w), lambda b, c: (b, c, 0))
    full = lambda a: pl.BlockSpec(a.shape, lambda b, c: (0,) * a.ndim)
    o_intra, qe, ke, vt, d = pl.pallas_call(
        _gla_intra_kernel,
        grid=(batch, nc // cps),
        in_specs=[full(G), full(M), rows(GLA_DK), rows(GLA_DK), rows(GLA_DV), rows(GLA_DK)],
        out_specs=[rows3(GLA_DV), rows3(GLA_DK), rows3(GLA_DK),
                   pl.BlockSpec((1, cps, GLA_DV, C), lambda b, c: (b, c, 0, 0)),
                   pl.BlockSpec((1, cps, 8, GLA_DK), lambda b, c: (b, c, 0, 0))],
        out_shape=[jax.ShapeDtypeStruct((batch, lp, GLA_DV), F32),
                   jax.ShapeDtypeStruct((batch, lp, GLA_DK), BF16), jax.ShapeDtypeStruct((batch, lp, GLA_DK), BF16),
                   jax.ShapeDtypeStruct((batch, nc, GLA_DV, C), BF16),
                   jax.ShapeDtypeStruct((batch, nc, 8, GLA_DK), F32)],
        compiler_params=pltpu.CompilerParams(dimension_semantics=("parallel", "parallel")),
        name="gla_intra",
    )(G, M, gq, gk, gv, la)
    chunk = lambda w: pl.BlockSpec((batch, C, w), lambda c: (0, c, 0))
    og, s_fin = pl.pallas_call(
        _gla_inter_kernel,
        grid=(nc,),
        in_specs=[chunk(GLA_DV), chunk(GLA_DK), chunk(GLA_DK),
                  pl.BlockSpec((batch, 1, GLA_DV, C), lambda c: (0, c, 0, 0)),
                  pl.BlockSpec((batch, 1, 8, GLA_DK), lambda c: (0, c, 0, 0))],
        out_specs=[chunk(GLA_DV), pl.BlockSpec((batch, GLA_HEADS, GLA_HK, GLA_HV), lambda c: (0, 0, 0, 0))],
        out_shape=[jax.ShapeDtypeStruct((batch, lp, GLA_DV), F32),
                   jax.ShapeDtypeStruct((batch, GLA_HEADS, GLA_HK, GLA_HV), F32)],
        scratch_shapes=[pltpu.VMEM((batch * GLA_HEADS, GLA_HV, GLA_HK), F32)],
        compiler_params=pltpu.CompilerParams(dimension_semantics=("arbitrary",)),
        name="gla_inter",
    )(o_intra, qe, ke, vt, d)
    return og.reshape(batch * lp, GLA_DV), s_fin


def _gla_sample_kernel(q_ref, k_ref, v_ref, la_ref, s0_ref, o_ref, s_ref, *, t_s):
    T = t_s
    row = lax.broadcasted_iota(jnp.int32, (T, GLA_HK), 0)
    k_fill = jnp.zeros((GLA_HK - T - 8, GLA_HK), F32)
    v_fill = jnp.zeros((GLA_HK - T, GLA_HV), F32)

    def one_seq(s, carry):
        rs = pl.ds(pl.multiple_of(s * T, T), T)
        for h in range(GLA_HEADS):
            ks = slice(h * GLA_HK, (h + 1) * GLA_HK)
            vs = slice(h * GLA_HV, (h + 1) * GLA_HV)
            q = q_ref[rs, ks]
            k = k_ref[rs, ks]
            v = v_ref[rs, vs]
            b = la_ref[rs, ks]
            sh = 1
            while sh < T:
                b = b + jnp.where(row >= sh, pltpu.roll(b, sh, 0), 0.0)
                sh *= 2
            S = s0_ref[s, h]
            o = _dot((q * jnp.exp(b)).astype(BF16), S.astype(BF16))
            for j in range(T):
                e = jnp.exp(jnp.where(row >= j, b - b[j:j + 1], NEG))
                a_col = jnp.sum(q * k[j:j + 1] * e, -1, keepdims=True)
                o = o + a_col * v[j:j + 1]
            o_ref[rs, vs] = o
            b_last = b[T - 1:T]
            ke = k * jnp.exp(b_last - b)
            kt = jnp.concatenate([ke, jnp.broadcast_to(jnp.exp(b_last), (8, GLA_HK)), k_fill], 0).T
            v_pad = jnp.concatenate([v, v_fill], 0)
            s_ref[s, h] = S * kt[:, T:T + 1] + _dot(kt.astype(BF16), v_pad.astype(BF16))
        return carry

    lax.fori_loop(0, SAMPLE_SEQS, one_seq, 0)


def _gla_sample(gq, gk, gv, la, s0, row0, t_s):
    n_seq = s0.shape[0]
    sb = SAMPLE_SEQS
    rb = sb * t_s
    b0 = row0 // rb
    rows = lambda w: pl.BlockSpec((rb, w), lambda i: (b0 + i, 0))
    st = pl.BlockSpec((sb, GLA_HEADS, GLA_HK, GLA_HV), lambda i: (i, 0, 0, 0))
    return pl.pallas_call(
        functools.partial(_gla_sample_kernel, t_s=t_s),
        grid=(n_seq // sb,),
        in_specs=[rows(GLA_DK), rows(GLA_DK), rows(GLA_DV), rows(GLA_DK), st],
        out_specs=[pl.BlockSpec((rb, GLA_DV), lambda i: (i, 0)), st],
        out_shape=[jax.ShapeDtypeStruct((n_seq * t_s, GLA_DV), F32), jax.ShapeDtypeStruct(s0.shape, F32)],
        compiler_params=pltpu.CompilerParams(dimension_semantics=("parallel",), vmem_limit_bytes=VMEM_LIMIT),
        name="gla_sample",
    )(gq, gk, gv, la, s0)


def _route(lt, valid):
    tm = lt.shape[1]
    el = lt[0:N_EXPERTS]
    gl = lt[N_EXPERTS:N_EXPERTS + N_GROUPS]
    g_max = jnp.max(gl, 0, keepdims=True)
    g_row = lax.broadcasted_iota(jnp.int32, (N_GROUPS, tm), 0)
    g_idx = jnp.min(jnp.where(gl == g_max, g_row, N_GROUPS), 0, keepdims=True)
    p_max = 1.0 / jnp.sum(jnp.exp(gl - g_max), 0, keepdims=True)
    e_row = lax.broadcasted_iota(jnp.int32, (N_EXPERTS, tm), 0)
    m1 = jnp.where(e_row // EXP_PER_GROUP == g_idx, el, -jnp.inf)
    v1 = jnp.max(m1, 0, keepdims=True)
    i1 = jnp.min(jnp.where(m1 == v1, e_row, N_EXPERTS), 0, keepdims=True)
    m2 = jnp.where(e_row == i1, -jnp.inf, m1)
    v2 = jnp.max(m2, 0, keepdims=True)
    i2 = jnp.min(jnp.where(m2 == v2, e_row, N_EXPERTS), 0, keepdims=True)
    e2 = jnp.exp(v2 - v1)
    w1 = p_max / (1.0 + e2)
    w2 = p_max * e2 / (1.0 + e2)
    o_row = lax.broadcasted_iota(jnp.int32, (8, tm), 0)
    ids = jnp.where(o_row == 0, i1, jnp.where(o_row == 1, i2, -1))
    return jnp.where(valid, ids, -1), jnp.where(o_row == 0, w1, jnp.where(o_row == 1, w2, 0.0))


def _store_row_tiles(ref, x):
    t, d = x.shape
    n = d // LANES
    for s in range(n):
        ref[pl.ds(s, t, stride=n), :] = x[:, s * LANES:(s + 1) * LANES]


def _load_row_tiles(ref, t, n=ROW_CHUNKS):
    return jnp.concatenate([ref[pl.ds(s, t, stride=n), :] for s in range(n)], axis=1)


def _merge_kernel(h_ref, yap_ref, yas_ref, ogp_ref, ogs_ref, gr_ref, gate_ref, valid_ref, ng_ref,
                  wa_ref, wg_ref, wo_ref, g1_ref, b1_ref, wrh_ref, wrl_ref, br_ref, u_ref,
                  h1t_ref, ids_ref, wts_ref, cnt_ref, run_ref, *, prompt_tiles):
    @pl.when(pl.program_id(0) == 0)
    def _():
        run_ref[...] = jnp.zeros_like(run_ref)

    h = h_ref[...]
    is_prompt = pl.program_id(0) < prompt_tiles
    og = jnp.where(is_prompt, ogp_ref[...], ogs_ref[...])
    ya = jnp.where(is_prompt, yap_ref[...], yas_ref[...])
    parts = []
    for hh in range(GLA_HEADS):
        o = og[:, hh * GLA_HV:(hh + 1) * GLA_HV]
        parts.append(o * lax.rsqrt(jnp.mean(o * o, -1, keepdims=True) + EPS))
    y_gla = jnp.concatenate(parts, 1) * ng_ref[...] * gr_ref[...].astype(F32)
    a = _dot(ya.astype(BF16), wa_ref[...])
    b = _dot(y_gla.astype(BF16), wg_ref[...])
    hm = gate_ref[:, :D_MODEL].astype(F32) * a + gate_ref[:, D_MODEL:].astype(F32) * b
    mix = _dot(hm.astype(BF16), wo_ref[...])
    h1 = _ln(DN_ALPHA * h + mix, g1_ref[...], b1_ref[...])
    _store_row_tiles(h1t_ref, h1)
    h_hi = h1.astype(BF16)
    h_lo = (h1 - h_hi.astype(F32)).astype(BF16)
    logits = _dot(h_hi, wrh_ref[...]) + _dot(h_lo, wrh_ref[...]) + _dot(h_hi, wrl_ref[...]) + br_ref[...]
    ids, wts_ref[...] = _route(logits.T, valid_ref[...] > 0.0)
    tm = ids.shape[1]
    e_row = lax.broadcasted_iota(jnp.int32, (N_EXPERTS, tm), 0)
    run = run_ref[:, 0:1]
    ranks = []
    for kk in range(2):
        onehot = (e_row == ids[kk:kk + 1]).astype(F32)
        before = _dot(onehot.astype(BF16), u_ref[...])
        ranks.append(jnp.sum(onehot * (run + before), 0, keepdims=True).astype(jnp.int32))
        run = run + jnp.sum(onehot, 1, keepdims=True)
    run_ref[...] = jnp.broadcast_to(run, run_ref.shape)
    o_row = lax.broadcasted_iota(jnp.int32, (8, tm), 0)
    ids_ref[...] = jnp.where(o_row == 2, ranks[0], jnp.where(o_row == 3, ranks[1], ids))
    cnt_ref[...] = run_ref[...].astype(jnp.int32)


def _merge(h, ya_p, ya_s, og_p, og_s, gr, gate, valid, ng, wa, wg, wo, g1, b1, wrh, wrl, br):
    n = h.shape[0]
    tm = ROW_TILE
    u = jnp.asarray(np.triu(np.ones((tm, tm), np.float32), 1), BF16)
    pt = ya_p.shape[0] // tm
    st = ya_s.shape[0] // tm
    row = lambda w: pl.BlockSpec((tm, w), lambda i: (i, 0))
    row_p = lambda w: pl.BlockSpec((tm, w), lambda i: (jnp.minimum(i, pt - 1), 0))
    row_s = lambda w: pl.BlockSpec((tm, w), lambda i: (jnp.clip(i - pt, 0, st - 1), 0))
    lane = lambda r: pl.BlockSpec((r, tm), lambda i: (0, i))
    full = lambda a: pl.BlockSpec(a.shape, lambda i: (0,) * a.ndim)
    return pl.pallas_call(
        functools.partial(_merge_kernel, prompt_tiles=pt),
        grid=(n // tm,),
        in_specs=[row(D_MODEL), row_p(Q_W), row_s(Q_W), row_p(GLA_DV), row_s(GLA_DV), row(GLA_DV), row(2 * D_MODEL),
                  lane(1), full(ng), full(wa), full(wg), full(wo), full(g1), full(b1),
                  full(wrh), full(wrl), full(br), full(u)],
        out_specs=[pl.BlockSpec((tm * ROW_CHUNKS, LANES), lambda i: (i, 0)), lane(8), lane(8),
                   pl.BlockSpec((N_EXPERTS, LANES), lambda i: (0, 0))],
        out_shape=[jax.ShapeDtypeStruct((n * ROW_CHUNKS, LANES), F32),
                   jax.ShapeDtypeStruct((8, n), jnp.int32), jax.ShapeDtypeStruct((8, n), F32),
                   jax.ShapeDtypeStruct((N_EXPERTS, LANES), jnp.int32)],
        scratch_shapes=[pltpu.VMEM((N_EXPERTS, LANES), F32)],
        compiler_params=pltpu.CompilerParams(dimension_semantics=("arbitrary",), vmem_limit_bytes=VMEM_LIMIT),
        name="merge",
    )(h, ya_p, ya_s, og_p, og_s, gr, gate, valid, ng, wa, wg, wo, g1, b1, wrh, wrl, br, u)


def _gather_row_tiles(idx_ref, idx0, src_hbm, dst, sem, n):
    def body(r, carry):
        t = idx_ref[idx0 + r]
        pltpu.make_async_copy(src_hbm.at[pl.ds(pl.multiple_of(t * ROW_CHUNKS, ROW_CHUNKS), ROW_CHUNKS), :],
                              dst.at[pl.ds(pl.multiple_of(r * ROW_CHUNKS, ROW_CHUNKS), ROW_CHUNKS), :], sem).start()
        return carry
    lax.fori_loop(0, n, body, 0, unroll=8)


def _wait_row_tiles(src_hbm, dst, sem, n):
    pltpu.make_async_copy(src_hbm.at[pl.ds(0, n * ROW_CHUNKS), :], dst, sem).wait()


def _tiles(ref, first, n=1):
    return ref.at[pl.ds(pl.multiple_of(first * ROW_CHUNKS, ROW_CHUNKS), n * ROW_CHUNKS), :]


def _dispatch_kernel(dest_ref, start_ref, cnt_ref, nt_ref, h_hbm, xs_hbm, buf, zbuf, sem_in, sem_out, sem_z,
                     *, ranges, n_rows, max_tiles):
    CH = DISPATCH_CHUNK
    T = FFN_TILE
    c = pl.program_id(0)
    n_chunks = pl.num_programs(0)

    def load(row0, slot):
        return pltpu.make_async_copy(_tiles(h_hbm, row0, CH), buf.at[slot], sem_in.at[slot])

    def scatter(row0, slot):
        def body(r, carry):
            for kk in range(2):
                d = dest_ref[kk * n_rows + row0 + r]
                pltpu.make_async_copy(_tiles(buf.at[slot], r), _tiles(xs_hbm, d), sem_out.at[slot]).start()
            return carry
        lax.fori_loop(0, CH, body, 0, unroll=8)

    def drain(slot):
        for _ in range(2):
            pltpu.make_async_copy(buf.at[slot], _tiles(xs_hbm, 0, CH), sem_out.at[slot]).wait()

    groups, per_group, first, stride, extra, extra_first = ranges

    def row0(j):
        in_group = first + (j // per_group) * stride + (j % per_group) * CH
        return jnp.where(j < groups * per_group, in_group, extra_first + (j - groups * per_group) * CH)

    @pl.when(c == 0)
    def _():
        for j in range(2):
            load(row0(j), j).start()

    slot = c % 3
    load(row0(c), slot).wait()
    scatter(row0(c), slot)

    @pl.when(c > 0)
    def _():
        drain((c + 2) % 3)

    @pl.when(c + 2 < n_chunks)
    def _():
        load(row0(c + 2), (c + 2) % 3).start()

    @pl.when(c == n_chunks - 1)
    def _():
        drain(slot)
        _zero_unowned_slots(start_ref, cnt_ref, nt_ref, xs_hbm, zbuf, sem_z, max_tiles)


def _zero_unowned_slots(start_ref, cnt_ref, nt_ref, xs_hbm, zbuf, sem_z, max_tiles):
    T = FFN_TILE
    zbuf[...] = jnp.zeros_like(zbuf)

    def tail_copies(e, wait):
        cnt = cnt_ref[e]
        n = (T - (cnt & (T - 1))) & (T - 1)
        first = start_ref[e] + cnt
        for bit in reversed(range(T.bit_length() - 1)):
            size = 1 << bit

            @pl.when((n & size) != 0)
            def _():
                cp = pltpu.make_async_copy(_tiles(zbuf, 0, size),
                                           _tiles(xs_hbm, first + ((n >> (bit + 1)) << (bit + 1)), size), sem_z)
                cp.wait() if wait else cp.start()

    def unused_tile(t, wait):
        cp = pltpu.make_async_copy(zbuf, _tiles(xs_hbm, t * T, T), sem_z)
        cp.wait() if wait else cp.start()

    for wait in (False, True):
        def per_expert(e, carry, wait=wait):
            tail_copies(e, wait)
            return carry

        def per_tile(t, carry, wait=wait):
            unused_tile(t, wait)
            return carry
        lax.fori_loop(0, N_EXPERTS, per_expert, 0)
        lax.fori_loop(nt_ref[0], max_tiles, per_tile, 0)


def _dispatch(dest, start, counts, n_tiles, h1t, ranges, max_tiles):
    T = FFN_TILE
    n_rows = h1t.shape[0] // ROW_CHUNKS
    return pl.pallas_call(
        functools.partial(_dispatch_kernel, ranges=ranges, n_rows=n_rows, max_tiles=max_tiles),
        grid_spec=pltpu.PrefetchScalarGridSpec(
            num_scalar_prefetch=4,
            grid=(ranges[0] * ranges[1] + ranges[4],),
            in_specs=[pl.BlockSpec(memory_space=pl.ANY)],
            out_specs=pl.BlockSpec(memory_space=pl.ANY),
            scratch_shapes=[pltpu.VMEM((3, DISPATCH_CHUNK * ROW_CHUNKS, LANES), F32),
                            pltpu.VMEM((T * ROW_CHUNKS, LANES), F32),
                            pltpu.SemaphoreType.DMA((3,)), pltpu.SemaphoreType.DMA((3,)), pltpu.SemaphoreType.DMA]),
        out_shape=jax.ShapeDtypeStruct((max_tiles * T * ROW_CHUNKS, LANES), F32),
        compiler_params=pltpu.CompilerParams(dimension_semantics=("arbitrary",)),
        name="dispatch",
    )(dest, start, counts, n_tiles, h1t)


def _ffn_kernel(te_ref, nxt_ref, nt_ref, x_ref, wg_hbm, wu_hbm, wd_hbm, out_ref, stage_g, stage_u, stage_d, sem,
                wgb, wub, wdb):
    i = pl.program_id(0)
    nt = nt_ref[0]
    T = FFN_TILE

    def stage(e):
        return [pltpu.make_async_copy(src.at[e], dst, sem.at[n])
                for n, (src, dst) in enumerate(((wg_hbm, stage_g), (wu_hbm, stage_u), (wd_hbm, stage_d)))]

    @pl.when(i == 0)
    def _():
        for cp in stage(te_ref[0]):
            cp.start()

    @pl.when(i < nt)
    def _():
        @pl.when((i == 0) | (te_ref[i] != te_ref[jnp.maximum(i - 1, 0)]))
        def _():
            for cp in stage(te_ref[i]):
                cp.wait()
            wgb[...] = stage_g[...].astype(BF16)
            wub[...] = stage_u[...].astype(BF16)
            wdb[...] = stage_d[...].astype(BF16)

            @pl.when(nxt_ref[i] >= 0)
            def _():
                for cp in stage(nxt_ref[i]):
                    cp.start()

        x = _load_row_tiles(x_ref, T).astype(BF16)
        g = _dot(x, wgb[...])
        u = _dot(x, wub[...])
        _store_row_tiles(out_ref, _dot((g * _sigmoid(g) * u).astype(BF16), wdb[...]))

    @pl.when(i >= nt)
    def _():
        out_ref[...] = jnp.zeros_like(out_ref)


def _ffn(tile_expert, next_expert, n_tiles, xs, w_g, w_u, w_d):
    T = FFN_TILE
    max_tiles = tile_expert.shape[0]
    hbm = pl.BlockSpec(memory_space=pl.ANY)
    tile = lambda imap: pl.BlockSpec((T * ROW_CHUNKS, LANES), imap)
    return pl.pallas_call(
        _ffn_kernel,
        grid_spec=pltpu.PrefetchScalarGridSpec(
            num_scalar_prefetch=3,
            grid=(max_tiles,),
            in_specs=[tile(lambda i, te, nxt, nt: (jnp.minimum(i, nt[0] - 1), 0)), hbm, hbm, hbm],
            out_specs=tile(lambda i, te, nxt, nt: (i, 0)),
            scratch_shapes=[pltpu.VMEM(w_g.shape[1:], F32), pltpu.VMEM(w_u.shape[1:], F32),
                            pltpu.VMEM(w_d.shape[1:], F32), pltpu.SemaphoreType.DMA((3,)),
                            pltpu.VMEM(w_g.shape[1:], BF16), pltpu.VMEM(w_u.shape[1:], BF16),
                            pltpu.VMEM(w_d.shape[1:], BF16)]),
        out_shape=jax.ShapeDtypeStruct(xs.shape, F32),
        compiler_params=pltpu.CompilerParams(dimension_semantics=("arbitrary",)),
        name="ffn",
    )(tile_expert, next_expert, n_tiles, xs, w_g, w_u, w_d)


def _combine_kernel(dest_ref, h_ref, w_ref, g_ref, b_ref, ys_hbm, y_ref, buf, sem, *, blk, n_rows):
    i = pl.program_id(0)
    n = pl.num_programs(0)
    T = CMB_TILE

    def gather(step, slot):
        for kk in range(2):
            _gather_row_tiles(dest_ref, kk * n_rows + blk(step) * T, ys_hbm, buf.at[slot, kk], sem.at[slot], T)

    @pl.when(i == 0)
    def _():
        gather(0, 0)

    @pl.when(i + 1 < n)
    def _():
        gather(i + 1, (i + 1) % 2)

    slot = i % 2
    for kk in range(2):
        _wait_row_tiles(ys_hbm, buf.at[slot, kk], sem.at[slot], T)
    w = w_ref[...]
    ff = w[:, 0:1] * _load_row_tiles(buf.at[slot, 0], T) + w[:, 1:2] * _load_row_tiles(buf.at[slot, 1], T)
    y_ref[...] = _ln(DN_ALPHA * _load_row_tiles(h_ref, T) + ff, g_ref[...], b_ref[...])


def _combine(dest, h1t, wts_t, ys, g2, b2, n_out, first_block, blocks_per_batch, skip_blocks):
    T = CMB_TILE
    n_rows = h1t.shape[0] // ROW_CHUNKS
    if skip_blocks:
        blk = lambda i: first_block + i + (i // blocks_per_batch + 1) * skip_blocks
    else:
        blk = lambda i: first_block + i
    full = lambda a: pl.BlockSpec(a.shape, lambda i, d: (0,) * a.ndim)
    return pl.pallas_call(
        functools.partial(_combine_kernel, blk=blk, n_rows=n_rows),
        grid_spec=pltpu.PrefetchScalarGridSpec(
            num_scalar_prefetch=1,
            grid=(n_out // T,),
            in_specs=[pl.BlockSpec((T * ROW_CHUNKS, LANES), lambda i, d: (blk(i), 0)),
                      pl.BlockSpec((T, 2), lambda i, d: (blk(i), 0)),
                      full(g2), full(b2), pl.BlockSpec(memory_space=pl.ANY)],
            out_specs=pl.BlockSpec((T, D_MODEL), lambda i, d: (i, 0)),
            scratch_shapes=[pltpu.VMEM((2, 2, T * ROW_CHUNKS, LANES), F32), pltpu.SemaphoreType.DMA((2,))]),
        out_shape=jax.ShapeDtypeStruct((n_out, D_MODEL), F32),
        compiler_params=pltpu.CompilerParams(dimension_semantics=("arbitrary",)),
        name="combine",
    )(dest, h1t, wts_t, g2, b2, ys)


def _dispatch_plan(routing, counts, max_tiles):
    T = FFN_TILE
    tiles_e = (counts + T - 1) // T
    tile_end = jnp.cumsum(tiles_e)
    n_tiles = tile_end[-1]
    start = (tile_end - tiles_e) * T
    ids, rank = routing[0:2], routing[2:4]
    onehot = (ids[..., None] == jnp.arange(N_EXPERTS, dtype=jnp.int32)).astype(jnp.int32)
    dest = (jnp.sum(onehot * start, axis=-1) + jnp.where(ids >= 0, rank, 0)).reshape(-1).astype(jnp.int32)
    experts = jnp.arange(N_EXPERTS, dtype=jnp.int32)
    tiles = jnp.arange(max_tiles, dtype=jnp.int32)
    te = jnp.sum((tiles[:, None] >= tile_end[None, :]).astype(jnp.int32), axis=1)
    te_last = jnp.max(jnp.where(counts > 0, experts, 0))
    te = jnp.where(tiles < n_tiles, te, te_last).astype(jnp.int32)
    later = jnp.where((counts > 0)[None, :] & (experts[None, :] > te[:, None]), experts[None, :], N_EXPERTS)
    nxt = jnp.min(later, axis=1)
    nxt = jnp.where(nxt < N_EXPERTS, nxt, -1).astype(jnp.int32)
    return te, nxt, n_tiles.reshape(1).astype(jnp.int32), start.astype(jnp.int32), dest


def kernel(x_prompt, x_sample, state_swa_k, state_swa_v, state_gla, meta_tokens, ln_emb_g, ln_emb_b, w_in, b_gate, attn_sink, w_alpha2, b_alpha, gla_norm_g, w_attn_br, w_gla_br, w_out, ln1_g, ln1_b, w_router_group, b_router_group, w_router_expert, b_router_expert, w_exp_gate, w_exp_up, w_exp_down, ln2_g, ln2_b):
    B, seq, _ = x_prompt.shape
    n_seq, t_s, _ = x_sample.shape
    depth = w_in.shape[0]
    assert depth == 1 and seq % ATT_BLOCK == 0 and t_s == 8 and SKIP_ROWS == ATT_BLOCK == WINDOW
    lp = SKIP_ROWS + seq
    NP, NS = B * lp, n_seq * t_s
    NR = NP + NS
    assert NP % ROW_TILE == 0 and NS % ROW_TILE == 0 and n_seq % SAMPLE_SEQS == 0
    assert seq % DISPATCH_CHUNK == 0 and NS % DISPATCH_CHUNK == 0
    l = 0
    row2 = lambda a: a.reshape(1, -1)

    head = jnp.concatenate([jnp.zeros((FRONT_PAD, D_MODEL), F32), meta_tokens], axis=0)
    pos = np.arange(NR)
    moe_valid = jnp.asarray(~((pos < NP) & (pos % lp < SKIP_ROWS)), F32).reshape(1, NR)

    wi = w_in[l]
    c_ga = sum((Q_W, KV_W, KV_W, GLA_DK, GLA_DK, GLA_DV))
    assert c_ga == W_IN_SPLIT
    w_pieces = (wi[:, :c_ga].astype(BF16), wi[:, c_ga + GLA_RANK:].astype(BF16),
                jnp.pad(wi[:, c_ga:c_ga + GLA_RANK], ((0, 0), (0, 128 - GLA_RANK))).astype(BF16))
    wa2_bf = jnp.concatenate([w_alpha2[l], jnp.zeros((128 - GLA_RANK, GLA_DK), F32)], axis=0).astype(BF16)

    h, q, k, v, gq, gk, gv, la, gr, gate = _inproj(x_prompt, x_sample.reshape(NS, D_MODEL), head, row2(ln_emb_g),
                                                   row2(ln_emb_b), w_pieces, wa2_bf, row2(b_alpha[l]),
                                                   row2(b_gate[l]))

    sink = attn_sink[l]
    ya_p, k_p, v_p = _swa_prompt(sink, q, k, v, B, lp)
    buf_k = state_swa_k[l].reshape(n_seq, WINDOW, KV_W)
    buf_v = state_swa_v[l].reshape(n_seq, WINDOW, KV_W)
    ya_s, nk_s, nv_s = _swa_sample(sink, q, k, v, buf_k, buf_v, NP, t_s)

    og_p, s_p = _gla_prompt(gq, gk, gv, la, B, lp)
    og_s, s_s = _gla_sample(gq, gk, gv, la, state_gla[l], NP, t_s)

    wr = jnp.concatenate([w_router_expert[l], w_router_group[l],
                          jnp.zeros((D_MODEL, LANES - N_EXPERTS - N_GROUPS), F32)], axis=1)
    br = jnp.concatenate([b_router_expert[l], b_router_group[l],
                          jnp.zeros((LANES - N_EXPERTS - N_GROUPS,), F32)]).reshape(1, LANES)
    wr_hi = wr.astype(BF16)
    h1t, routing, wts, counts = _merge(h, ya_p, ya_s, og_p, og_s, gr, gate, moe_valid,
                                       row2(gla_norm_g[l]),
                                       w_attn_br[l].astype(BF16), w_gla_br[l].astype(BF16), w_out[l].astype(BF16),
                                       row2(ln1_g[l]), row2(ln1_b[l]), wr_hi, (wr - wr_hi.astype(F32)).astype(BF16), br)

    n_tok = B * seq + NS
    max_tiles = (2 * n_tok) // FFN_TILE + N_EXPERTS
    counts = counts[:, 0]
    te, nxt, n_tiles, start, dest = _dispatch_plan(routing[0:4], counts, max_tiles)
    routed = (B, seq // DISPATCH_CHUNK, SKIP_ROWS, lp, NS // DISPATCH_CHUNK, NP)
    xs = _dispatch(dest, start, counts, n_tiles, h1t, routed, max_tiles)
    ys = _ffn(te, nxt, n_tiles, xs, w_exp_gate[l], w_exp_up[l], w_exp_down[l])

    wts_t = wts[0:2].T
    g2, b2 = row2(ln2_g[l]), row2(ln2_b[l])
    skip_blocks = SKIP_ROWS // CMB_TILE
    y_p = _combine(dest, h1t, wts_t, ys, g2, b2, B * seq, 0, seq // CMB_TILE, skip_blocks)
    y_s = _combine(dest, h1t, wts_t, ys, g2, b2, NS, NP // CMB_TILE, 1, 0)

    kv_shape = (1, B, WINDOW, N_KV, HEAD_DIM)
    return (y_p.reshape(B, seq, D_MODEL), y_s.reshape(n_seq, t_s, D_MODEL),
            k_p.reshape(kv_shape), v_p.reshape(kv_shape), s_p[None],
            nk_s.reshape(1, n_seq, WINDOW, N_KV, HEAD_DIM), nv_s.reshape(1, n_seq, WINDOW, N_KV, HEAD_DIM),
            s_s[None])
```

```python
import functools

import numpy as np
import jax
import jax.numpy as jnp
from jax import lax
from jax.experimental import pallas as pl
from jax.experimental.pallas import tpu as pltpu

F32 = jnp.float32
BF16 = jnp.bfloat16

D_MODEL = 1024
N_META = 16
HEAD_DIM = 64
N_HEADS = 8
N_KV = 2
Q_PER_KV = 4
WINDOW = 128
ATT_BLOCK = 128
GLA_HEADS = 4
GLA_HK = 128
GLA_HV = 256
GLA_DK = GLA_HEADS * GLA_HK
GLA_DV = GLA_HEADS * GLA_HV
GLA_RANK = 16
GLA_TAU = 16.0
GLA_CHUNK = 64
N_GROUPS = 4
EXP_PER_GROUP = 8
N_EXPERTS = 32
D_EXPERT = 256
DN_ALPHA = 2.0 ** 0.25
EPS = 1e-5
NEG = -1e30

FRONT_PAD = (-N_META) % ATT_BLOCK
SKIP_ROWS = FRONT_PAD + N_META

Q_W, KV_W = N_HEADS * HEAD_DIM, N_KV * HEAD_DIM
SEG = {}
_o = 0
for _n, _w in (("q", Q_W), ("k", KV_W), ("v", KV_W), ("gq", GLA_DK), ("gk", GLA_DK), ("gv", GLA_DV)):
    SEG[_n] = (0, _o, _o + _w)
    _o += _w
W_IN_SPLIT = _o
SEG["gr"] = (1, 0, GLA_DV)
SEG["gate"] = (1, GLA_DV, GLA_DV + 2 * D_MODEL)
SEG["ga"] = (2, 0, 128)

ROW_TILE = 512
FFN_TILE = 256
CMB_TILE = 128
DISPATCH_CHUNK = 128
GLA_INTRA_CHUNKS = 6
SWA_BLOCKS = 1
INPROJ_TILE = 512
SAMPLE_SEQS = 8
LANES = 128
ROW_CHUNKS = D_MODEL // LANES
VMEM_LIMIT = 56 * 1024 * 1024


def _ln(x, g, b):
    mu = jnp.mean(x, -1, keepdims=True)
    xc = x - mu
    var = jnp.mean(xc * xc, -1, keepdims=True)
    return xc * lax.rsqrt(var + EPS) * g + b


def _sigmoid(x):
    return 0.5 * jnp.tanh(0.5 * x) + 0.5


def _dot(a, b):
    return jnp.dot(a, b, preferred_element_type=F32)


def _dot_nt(a, b):
    return lax.dot_general(a, b, (((1,), (1,)), ((), ())), preferred_element_type=F32)


def _inproj_kernel(*refs, blocks, batch_blocks, prompt_blocks):
    xp_refs, xs_refs = refs[:blocks], refs[blocks:2 * blocks]
    (head_ref, g_ref, b_ref, w0_ref, w1_ref, w2_ref, wa2_ref, ba_ref, bg_ref,
     h_ref, q_ref, k_ref, v_ref, gq_ref, gk_ref, gv_ref, la_ref, gr_ref, gate_ref) = refs[2 * blocks:]
    w_refs = (w0_ref, w1_ref, w2_ref)
    row = lax.broadcasted_iota(jnp.int32, (ATT_BLOCK, 1), 0)
    xs, keeps = [], []
    for s in range(blocks):
        p = pl.program_id(0) * blocks + s
        is_sample = p >= prompt_blocks
        is_head = jnp.logical_and(jnp.logical_not(is_sample), p % batch_blocks == 0)
        xs.append(jnp.where(is_sample, xs_refs[s][...], jnp.where(is_head, head_ref[...], xp_refs[s][...])))
        keeps.append(jnp.where(jnp.logical_and(is_head, row < FRONT_PAD), 0.0, 1.0))
    h = _ln(jnp.concatenate(xs, 0), g_ref[...], b_ref[...])
    h_ref[...] = h
    hb = h.astype(BF16)
    keep = jnp.concatenate(keeps, 0)

    def seg(name):
        piece, a, b = SEG[name]
        return _dot(hb, w_refs[piece][:, a:b])

    q_ref[...] = seg("q")
    k_ref[...] = seg("k")
    v_ref[...] = seg("v")
    gq_ref[...] = seg("gq") * (GLA_HK ** -0.5)
    gk_ref[...] = seg("gk") * keep
    gv_ref[...] = seg("gv") * keep
    gr = seg("gr")
    gr_ref[...] = (gr * _sigmoid(gr)).astype(BF16)
    gate_ref[...] = _sigmoid(seg("gate") + bg_ref[...]).astype(BF16)
    z = _dot(seg("ga").astype(BF16), wa2_ref[...]) + ba_ref[...]
    la = (jnp.minimum(z, 0.0) - jnp.log(1.0 + jnp.exp(-jnp.abs(z)))) * (1.0 / GLA_TAU)
    la_ref[...] = la * keep


def _inproj(x_prompt, x_sample, head, ln_g, ln_b, w_pieces, wa2_bf, b_alpha, b_gate):
    B, seq, _ = x_prompt.shape
    blk = ATT_BLOCK
    seq_blocks = seq // blk
    batch_blocks = seq_blocks + 1
    prompt_blocks = B * batch_blocks
    sample_blocks = x_sample.shape[0] // blk
    n = (prompt_blocks + sample_blocks) * blk
    tm = INPROJ_TILE
    blocks = tm // blk
    xp = x_prompt.reshape(B * seq, D_MODEL)

    def prompt_block(s):
        def imap(i):
            p = jnp.minimum(i * blocks + s, prompt_blocks - 1)
            return (p // batch_blocks * seq_blocks + jnp.maximum(p % batch_blocks - 1, 0), 0)
        return pl.BlockSpec((blk, D_MODEL), imap)

    def sample_block(s):
        return pl.BlockSpec((blk, D_MODEL), lambda i: (jnp.clip(i * blocks + s - prompt_blocks, 0, sample_blocks - 1), 0))

    widths = [D_MODEL, Q_W, KV_W, KV_W, GLA_DK, GLA_DK, GLA_DV, GLA_DK, GLA_DV, 2 * D_MODEL]
    row = lambda w: pl.BlockSpec((tm, w), lambda i: (i, 0))
    const = lambda a: pl.BlockSpec(a.shape, lambda i: (0,) * a.ndim, pipeline_mode=pl.Buffered(1))
    return pl.pallas_call(
        functools.partial(_inproj_kernel, blocks=blocks, batch_blocks=batch_blocks, prompt_blocks=prompt_blocks),
        grid=(n // tm,),
        in_specs=[prompt_block(s) for s in range(blocks)] + [sample_block(s) for s in range(blocks)]
                 + [const(a) for a in (head, ln_g, ln_b, *w_pieces, wa2_bf, b_alpha, b_gate)],
        out_specs=[row(w) for w in widths],
        out_shape=[jax.ShapeDtypeStruct((n, w), BF16 if i >= len(widths) - 2 else F32) for i, w in enumerate(widths)],
        compiler_params=pltpu.CompilerParams(dimension_semantics=("parallel",), vmem_limit_bytes=VMEM_LIMIT),
        name="inproj",
    )(*([xp] * blocks), *([x_sample] * blocks), head, ln_g, ln_b, *w_pieces, wa2_bf, b_alpha, b_gate)


def _softmax_pv(s, sink, vv):
    m = jnp.maximum(jnp.max(s, -1, keepdims=True), sink)
    p = jnp.exp(s - m)
    l = jnp.sum(p, -1, keepdims=True) + jnp.exp(sink - m)
    return _dot(p.astype(BF16), vv) / l


def _swa_prompt_kernel(sink_ref, bias_ref, q_ref, kp_ref, kc_ref, vp_ref, vc_ref, o_ref, kl_ref, vl_ref):
    nq = q_ref.shape[0] // ATT_BLOCK
    q_all = q_ref[...] * (HEAD_DIM ** -0.5)
    kb = jnp.concatenate([kp_ref[...], kc_ref[...]], 0)
    vb = jnp.concatenate([vp_ref[...], vc_ref[...]], 0)
    low = lax.broadcasted_iota(jnp.int32, (1, 2 * HEAD_DIM), 1) < HEAD_DIM
    k_low = jnp.where(low, kb, 0.0)
    k_high = jnp.where(low, 0.0, kb)
    keys = {(0, 0): k_low.astype(BF16), (0, 1): pltpu.roll(k_low, HEAD_DIM, 1).astype(BF16),
            (1, 0): pltpu.roll(k_high, HEAD_DIM, 1).astype(BF16), (1, 1): k_high.astype(BF16)}
    ones_col = (lax.broadcasted_iota(jnp.int32, (1, 2 * HEAD_DIM), 1) == HEAD_DIM).astype(F32)
    values = [jnp.where(low, vb, ones_col).astype(BF16),
              jnp.where(low, pltpu.roll(vb, HEAD_DIM, 1), ones_col).astype(BF16)]
    for blk in range(nq):
        rows = slice(blk * ATT_BLOCK, (blk + 1) * ATT_BLOCK)
        band = slice(blk * ATT_BLOCK, (blk + 2) * ATT_BLOCK)
        variant = jnp.minimum(pl.program_id(1) * nq + blk, 2)
        for pair in range(N_HEADS // 2):
            qp = q_all[rows, pair * 2 * HEAD_DIM:(pair + 1) * 2 * HEAD_DIM]
            outs = []
            for half in range(2):
                h = 2 * pair + half
                kv = h // Q_PER_KV
                qm = jnp.where(low if half == 0 else jnp.logical_not(low), qp, 0.0).astype(BF16)
                s = _dot_nt(qm, keys[(kv, half)][band]) + bias_ref[variant, h]
                m = jnp.maximum(jnp.max(s, -1, keepdims=True), sink_ref[h])
                pv = _dot(jnp.exp(s - m).astype(BF16), values[kv][band])
                outs.append(pv / (pv[:, HEAD_DIM:HEAD_DIM + 1] + jnp.exp(sink_ref[h] - m)))
            o_ref[rows, pair * 2 * HEAD_DIM:(pair + 1) * 2 * HEAD_DIM] = jnp.where(low, outs[0],
                                                                                   pltpu.roll(outs[1], HEAD_DIM, 1))

    @pl.when(pl.program_id(1) == pl.num_programs(1) - 1)
    def _():
        kl_ref[0] = kc_ref[(nq - 1) * ATT_BLOCK:nq * ATT_BLOCK, :]
        vl_ref[0] = vc_ref[(nq - 1) * ATT_BLOCK:nq * ATT_BLOCK, :]


def _swa_bias_table():
    r = np.arange(ATT_BLOCK)[:, None]
    c = np.arange(2 * ATT_BLOCK)[None, :]
    dist = r - c + ATT_BLOCK
    slopes = 2.0 ** -(np.arange(N_HEADS) + 1.0)
    table = np.empty((3, N_HEADS, ATT_BLOCK, 2 * ATT_BLOCK), np.float32)
    for j in range(3):
        seen = (dist >= 0) & (dist < WINDOW) & ((j - 1) * ATT_BLOCK + c - FRONT_PAD >= 0)
        table[j] = np.where(seen[None], -slopes[:, None, None] * dist[None], NEG)
    return table


def _swa_prompt(sink, q, k, v, batch, lp):
    nq = SWA_BLOCKS
    nb = lp // ATT_BLOCK
    assert nb >= 3 and nb % nq == 0
    steps = nb // nq
    n = batch * lp
    bias = jnp.asarray(_swa_bias_table())
    cur = lambda w: pl.BlockSpec((nq * ATT_BLOCK, w), lambda b, j: (b * steps + j, 0))
    prev = lambda w: pl.BlockSpec((ATT_BLOCK, w), lambda b, j: (b * nb + jnp.maximum(j * nq - 1, 0), 0))
    last = pl.BlockSpec((1, ATT_BLOCK, KV_W), lambda b, j: (b, 0, 0))
    return pl.pallas_call(
        _swa_prompt_kernel,
        grid=(batch, steps),
        in_specs=[pl.BlockSpec(memory_space=pltpu.SMEM), pl.BlockSpec(bias.shape, lambda b, j: (0, 0, 0, 0)),
                  cur(Q_W), prev(KV_W), cur(KV_W), prev(KV_W), cur(KV_W)],
        out_specs=[cur(Q_W), last, last],
        out_shape=[jax.ShapeDtypeStruct((n, Q_W), F32), jax.ShapeDtypeStruct((batch, ATT_BLOCK, KV_W), F32),
                   jax.ShapeDtypeStruct((batch, ATT_BLOCK, KV_W), F32)],
        compiler_params=pltpu.CompilerParams(dimension_semantics=("parallel", "arbitrary")),
        name="swa_prompt",
    )(sink, bias, q, k, k, v, v)


def _swa_sample_kernel(sink_ref, q_ref, k_ref, v_ref, bk_ref, bv_ref, o_ref, nk_ref, nv_ref, *, t_s):
    nbuf = WINDOW
    span = 2 * WINDOW
    rows = Q_PER_KV * t_s
    r = lax.broadcasted_iota(jnp.int32, (rows, span), 0)
    c = lax.broadcasted_iota(jnp.int32, (rows, span), 1)
    t = r % t_s
    dist = t + nbuf - c
    mask = (dist >= 0) & (dist < WINDOW) & (c < nbuf + t_s)
    distf = dist.astype(F32)
    g_col = lax.broadcasted_iota(jnp.int32, (rows, 1), 0) // t_s
    fill = jnp.zeros((span - nbuf - t_s, KV_W), F32)

    def one_seq(s, carry):
        rs = pl.ds(pl.multiple_of(s * t_s, t_s), t_s)
        q = q_ref[rs, :]
        k_new = k_ref[rs, :]
        v_new = v_ref[rs, :]
        bk = bk_ref[s]
        bv = bv_ref[s]
        k_all = jnp.concatenate([bk, k_new, fill], 0)
        v_all = jnp.concatenate([bv, v_new, fill], 0)
        for kv in range(N_KV):
            qg = jnp.concatenate(
                [q[:, (kv * Q_PER_KV + g) * HEAD_DIM:(kv * Q_PER_KV + g + 1) * HEAD_DIM] for g in range(Q_PER_KV)], 0)
            kk = k_all[:, kv * HEAD_DIM:(kv + 1) * HEAD_DIM].astype(BF16)
            vv = v_all[:, kv * HEAD_DIM:(kv + 1) * HEAD_DIM].astype(BF16)
            slope = jnp.zeros((rows, 1), F32)
            sink = jnp.zeros((rows, 1), F32)
            for g in range(Q_PER_KV):
                h = kv * Q_PER_KV + g
                slope = jnp.where(g_col == g, 2.0 ** -(h + 1), slope)
                sink = jnp.where(g_col == g, sink_ref[h], sink)
            sc = _dot_nt(qg.astype(BF16), kk) * (HEAD_DIM ** -0.5) - slope * distf
            sc = jnp.where(mask, sc, NEG)
            o = _softmax_pv(sc, sink, vv)
            for g in range(Q_PER_KV):
                h = kv * Q_PER_KV + g
                o_ref[rs, h * HEAD_DIM:(h + 1) * HEAD_DIM] = o[g * t_s:(g + 1) * t_s]
        nk_ref[s, 0:nbuf - t_s, :] = bk[t_s:, :]
        nk_ref[s, nbuf - t_s:nbuf, :] = k_new
        nv_ref[s, 0:nbuf - t_s, :] = bv[t_s:, :]
        nv_ref[s, nbuf - t_s:nbuf, :] = v_new
        return carry

    lax.fori_loop(0, SAMPLE_SEQS, one_seq, 0, unroll=True)


def _swa_sample(sink, q, k, v, buf_k, buf_v, row0, t_s):
    n_seq = buf_k.shape[0]
    sb = SAMPLE_SEQS
    rb = sb * t_s
    b0 = row0 // rb
    rows = lambda w: pl.BlockSpec((rb, w), lambda i: (b0 + i, 0))
    bufs = pl.BlockSpec((sb, WINDOW, KV_W), lambda i: (i, 0, 0))
    return pl.pallas_call(
        functools.partial(_swa_sample_kernel, t_s=t_s),
        grid=(n_seq // sb,),
        in_specs=[pl.BlockSpec(memory_space=pltpu.SMEM), rows(Q_W), rows(KV_W), rows(KV_W), bufs, bufs],
        out_specs=[pl.BlockSpec((rb, Q_W), lambda i: (i, 0)), bufs, bufs],
        out_shape=[jax.ShapeDtypeStruct((n_seq * t_s, Q_W), F32),
                   jax.ShapeDtypeStruct(buf_k.shape, F32), jax.ShapeDtypeStruct(buf_v.shape, F32)],
        compiler_params=pltpu.CompilerParams(dimension_semantics=("parallel",)),
        name="swa_sample",
    )(sink, q, k, v, buf_k, buf_v)


def _gla_tables(chunk):
    t = np.arange(chunk)[:, None]
    u = np.arange(chunk)[None, :]
    masks = []
    w = chunk // 2
    while w >= 1:
        masks.append((t // (2 * w) == u // (2 * w)) & ((t // w) % 2 == 1) & ((u // w) % 2 == 0))
        w //= 2
    return (u <= t).astype(np.float32), np.stack(masks, 0).astype(np.float32)


def _level_exponents(b, la, w):
    C = b.shape[0]
    row = lax.broadcasted_iota(jnp.int32, b.shape, 0)
    if w >= 4:
        pieces = [jnp.broadcast_to(b[p + w - 1:p + w], (2 * w, b.shape[1])) for p in range(0, C, 2 * w)]
        ref = pieces[0] if len(pieces) == 1 else jnp.concatenate(pieces, 0)
        return jnp.where((row & w) != 0, b - ref, ref - b)
    if w == 2:
        m = row & 3
        nxt = pltpu.roll(la, C - 1, 0)
        prv = pltpu.roll(la, 1, 0)
        return jnp.where(m == 2, la, jnp.where(m == 3, la + prv, jnp.where(m == 0, nxt, 0.0)))
    return jnp.where((row & 1) != 0, la, 0.0)


def _split3(x):
    hi = x.astype(BF16)
    r1 = x - hi.astype(F32)
    mid = r1.astype(BF16)
    lo = (r1 - mid.astype(F32)).astype(BF16)
    return hi, mid, lo


def _gla_intra_kernel(g_ref, m_ref, q_ref, k_ref, v_ref, la_ref, o_ref, qe_ref, ke_ref, vt_ref, d_ref):
    C = GLA_CHUNK
    n_lvl = m_ref.shape[0]
    G = g_ref[...]
    eye = (lax.broadcasted_iota(jnp.int32, (C, C), 0) == lax.broadcasted_iota(jnp.int32, (C, C), 1)).astype(F32)
    for j in range(q_ref.shape[0] // C):
        rs = slice(j * C, (j + 1) * C)
        la = la_ref[rs, :]
        hi, mid, lo = _split3(la)
        b = _dot(G, hi) + _dot(G, mid) + _dot(G, lo)
        b_last = b[C - 1:C]
        q_all = q_ref[rs, :]
        k_all = k_ref[rs, :]
        qe_ref[0, rs, :] = (q_all * jnp.exp(b)).astype(BF16)
        ke_ref[0, rs, :] = (k_all * jnp.exp(b_last - b)).astype(BF16)
        d_ref[0, j] = jnp.broadcast_to(jnp.exp(b_last), (8, GLA_DK))
        q_lvl, k_lvl = [], []
        for l in range(n_lvl):
            El = jnp.exp(_level_exponents(b, la, C >> (l + 1)))
            q_lvl.append((q_all * El).astype(BF16))
            k_lvl.append((k_all * El).astype(BF16))
        for h in range(GLA_HEADS):
            ks = slice(h * GLA_HK, (h + 1) * GLA_HK)
            vs = slice(h * GLA_HV, (h + 1) * GLA_HV)
            v = v_ref[rs, vs]
            att = eye * jnp.sum(q_all[:, ks] * k_all[:, ks], -1, keepdims=True)
            for l in range(n_lvl):
                att = att + m_ref[l] * _dot_nt(q_lvl[l][:, ks], k_lvl[l][:, ks])
            o_ref[0, rs, vs] = _dot(att.astype(BF16), v.astype(BF16))
            vt_ref[0, j // 2, vs, (j % 2) * C:(j % 2 + 1) * C] = v.T.astype(BF16)


def _gla_inter_kernel(o_ref, qe_ref, ke_ref, vt_ref, d_ref, og_ref, s_ref, st_ref):
    c = pl.program_id(0)
    batch = o_ref.shape[0]

    @pl.when(c == 0)
    def _():
        st_ref[...] = jnp.zeros_like(st_ref)

    even = c % 2 == 0
    for b in range(batch):
        for h in range(GLA_HEADS):
            ks = slice(h * GLA_HK, (h + 1) * GLA_HK)
            vs = slice(h * GLA_HV, (h + 1) * GLA_HV)
            st = st_ref[b * GLA_HEADS + h]
            og_ref[b, :, vs] = o_ref[b, :, vs] + _dot_nt(qe_ref[b, :, ks], st.astype(BF16))
            ke = ke_ref[b, :, ks]
            zero = jnp.zeros_like(ke)
            ke2 = jnp.concatenate([jnp.where(even, ke, zero), jnp.where(even, zero, ke)], 0)
            st_ref[b * GLA_HEADS + h] = st * d_ref[b, 0, 0:1, ks] + _dot(vt_ref[b, 0, vs, :], ke2)

    @pl.when(c == pl.num_programs(0) - 1)
    def _():
        for b in range(batch):
            for h in range(GLA_HEADS):
                s_ref[b, h] = st_ref[b * GLA_HEADS + h].T


def _gla_prompt(gq, gk, gv, la, batch, lp):
    C = GLA_CHUNK
    nc = lp // C
    G, M = _gla_tables(C)
    G = jnp.asarray(G, BF16)
    M = jnp.asarray(M, F32)
    cps = GLA_INTRA_CHUNKS
    assert nc % cps == 0 and cps % 2 == 0
    rows = lambda w: pl.BlockSpec((cps * C, w), lambda b, c: (b * (nc // cps) + c, 0))
    rows3 = lambda w: pl.BlockSpec((1, cps * C, w), lambda b, c: (b, c, 0))
    full = lambda a: pl.BlockSpec(a.shape, lambda b, c: (0,) * a.ndim)
    o_intra, qe, ke, vt, d = pl.pallas_call(
        _gla_intra_kernel,
        grid=(batch, nc // cps),
        in_specs=[full(G), full(M), rows(GLA_DK), rows(GLA_DK), rows(GLA_DV), rows(GLA_DK)],
        out_specs=[rows3(GLA_DV), rows3(GLA_DK), rows3(GLA_DK),
                   pl.BlockSpec((1, cps // 2, GLA_DV, 2 * C), lambda b, c: (b, c, 0, 0)),
                   pl.BlockSpec((1, cps, 8, GLA_DK), lambda b, c: (b, c, 0, 0))],
        out_shape=[jax.ShapeDtypeStruct((batch, lp, GLA_DV), F32),
                   jax.ShapeDtypeStruct((batch, lp, GLA_DK), BF16), jax.ShapeDtypeStruct((batch, lp, GLA_DK), BF16),
                   jax.ShapeDtypeStruct((batch, nc // 2, GLA_DV, 2 * C), BF16),
                   jax.ShapeDtypeStruct((batch, nc, 8, GLA_DK), F32)],
        compiler_params=pltpu.CompilerParams(dimension_semantics=("parallel", "parallel")),
        name="gla_intra",
    )(G, M, gq, gk, gv, la)
    chunk = lambda w: pl.BlockSpec((batch, C, w), lambda c: (0, c, 0))
    og, s_fin = pl.pallas_call(
        _gla_inter_kernel,
        grid=(nc,),
        in_specs=[chunk(GLA_DV), chunk(GLA_DK), chunk(GLA_DK),
                  pl.BlockSpec((batch, 1, GLA_DV, 2 * C), lambda c: (0, c // 2, 0, 0)),
                  pl.BlockSpec((batch, 1, 8, GLA_DK), lambda c: (0, c, 0, 0))],
        out_specs=[chunk(GLA_DV), pl.BlockSpec((batch, GLA_HEADS, GLA_HK, GLA_HV), lambda c: (0, 0, 0, 0))],
        out_shape=[jax.ShapeDtypeStruct((batch, lp, GLA_DV), F32),
                   jax.ShapeDtypeStruct((batch, GLA_HEADS, GLA_HK, GLA_HV), F32)],
        scratch_shapes=[pltpu.VMEM((batch * GLA_HEADS, GLA_HV, GLA_HK), F32)],
        compiler_params=pltpu.CompilerParams(dimension_semantics=("arbitrary",)),
        name="gla_inter",
    )(o_intra, qe, ke, vt, d)
    return og.reshape(batch * lp, GLA_DV), s_fin


def _gla_sample_kernel(q_ref, k_ref, v_ref, la_ref, s0_ref, o_ref, s_ref, *, t_s):
    T = t_s
    row = lax.broadcasted_iota(jnp.int32, (T, GLA_HK), 0)
    k_fill = jnp.zeros((GLA_HK - T - 8, GLA_HK), F32)
    v_fill = jnp.zeros((GLA_HK - T, GLA_HV), F32)

    def one_seq(s, carry):
        rs = pl.ds(pl.multiple_of(s * T, T), T)
        for h in range(GLA_HEADS):
            ks = slice(h * GLA_HK, (h + 1) * GLA_HK)
            vs = slice(h * GLA_HV, (h + 1) * GLA_HV)
            q = q_ref[rs, ks]
            k = k_ref[rs, ks]
            v = v_ref[rs, vs]
            b = la_ref[rs, ks]
            sh = 1
            while sh < T:
                b = b + jnp.where(row >= sh, pltpu.roll(b, sh, 0), 0.0)
                sh *= 2
            S = s0_ref[s, h]
            o = _dot((q * jnp.exp(b)).astype(BF16), S.astype(BF16))
            for j in range(T):
                e = jnp.exp(jnp.where(row >= j, b - b[j:j + 1], NEG))
                a_col = jnp.sum(q * k[j:j + 1] * e, -1, keepdims=True)
                o = o + a_col * v[j:j + 1]
            o_ref[rs, vs] = o
            b_last = b[T - 1:T]
            ke = k * jnp.exp(b_last - b)
            kt = jnp.concatenate([ke, jnp.broadcast_to(jnp.exp(b_last), (8, GLA_HK)), k_fill], 0).T
            v_pad = jnp.concatenate([v, v_fill], 0)
            s_ref[s, h] = S * kt[:, T:T + 1] + _dot(kt.astype(BF16), v_pad.astype(BF16))
        return carry

    lax.fori_loop(0, SAMPLE_SEQS, one_seq, 0)


def _gla_sample(gq, gk, gv, la, s0, row0, t_s):
    n_seq = s0.shape[0]
    sb = SAMPLE_SEQS
    rb = sb * t_s
    b0 = row0 // rb
    rows = lambda w: pl.BlockSpec((rb, w), lambda i: (b0 + i, 0))
    st = pl.BlockSpec((sb, GLA_HEADS, GLA_HK, GLA_HV), lambda i: (i, 0, 0, 0))
    return pl.pallas_call(
        functools.partial(_gla_sample_kernel, t_s=t_s),
        grid=(n_seq // sb,),
        in_specs=[rows(GLA_DK), rows(GLA_DK), rows(GLA_DV), rows(GLA_DK), st],
        out_specs=[pl.BlockSpec((rb, GLA_DV), lambda i: (i, 0)), st],
        out_shape=[jax.ShapeDtypeStruct((n_seq * t_s, GLA_DV), F32), jax.ShapeDtypeStruct(s0.shape, F32)],
        compiler_params=pltpu.CompilerParams(dimension_semantics=("parallel",), vmem_limit_bytes=VMEM_LIMIT),
        name="gla_sample",
    )(gq, gk, gv, la, s0)


def _route(lt, valid):
    tm = lt.shape[1]
    el = lt[0:N_EXPERTS]
    gl = lt[N_EXPERTS:N_EXPERTS + N_GROUPS]
    g_max = jnp.max(gl, 0, keepdims=True)
    g_row = lax.broadcasted_iota(jnp.int32, (N_GROUPS, tm), 0)
    g_idx = jnp.min(jnp.where(gl == g_max, g_row, N_GROUPS), 0, keepdims=True)
    p_max = 1.0 / jnp.sum(jnp.exp(gl - g_max), 0, keepdims=True)
    e_row = lax.broadcasted_iota(jnp.int32, (N_EXPERTS, tm), 0)
    m1 = jnp.where(e_row // EXP_PER_GROUP == g_idx, el, -jnp.inf)
    v1 = jnp.max(m1, 0, keepdims=True)
    i1 = jnp.min(jnp.where(m1 == v1, e_row, N_EXPERTS), 0, keepdims=True)
    m2 = jnp.where(e_row == i1, -jnp.inf, m1)
    v2 = jnp.max(m2, 0, keepdims=True)
    i2 = jnp.min(jnp.where(m2 == v2, e_row, N_EXPERTS), 0, keepdims=True)
    e2 = jnp.exp(v2 - v1)
    w1 = p_max / (1.0 + e2)
    w2 = p_max * e2 / (1.0 + e2)
    o_row = lax.broadcasted_iota(jnp.int32, (8, tm), 0)
    ids = jnp.where(o_row == 0, i1, jnp.where(o_row == 1, i2, -1))
    return jnp.where(valid, ids, -1), jnp.where(o_row == 0, w1, jnp.where(o_row == 1, w2, 0.0))


def _store_row_tiles(ref, x):
    t, d = x.shape
    n = d // LANES
    for s in range(n):
        ref[pl.ds(s, t, stride=n), :] = x[:, s * LANES:(s + 1) * LANES]


def _load_row_tiles(ref, t, n=ROW_CHUNKS):
    return jnp.concatenate([ref[pl.ds(s, t, stride=n), :] for s in range(n)], axis=1)


def _merge_kernel(h_ref, yap_ref, yas_ref, ogp_ref, ogs_ref, gr_ref, gate_ref, valid_ref, ng_ref,
                  wa_ref, wg_ref, wo_ref, g1_ref, b1_ref, wrh_ref, wrl_ref, br_ref, u_ref,
                  h1t_ref, ids_ref, wts_ref, cnt_ref, run_ref, *, prompt_tiles):
    @pl.when(pl.program_id(0) == 0)
    def _():
        run_ref[...] = jnp.zeros_like(run_ref)

    h = h_ref[...]
    is_prompt = pl.program_id(0) < prompt_tiles
    og = jnp.where(is_prompt, ogp_ref[...], ogs_ref[...])
    ya = jnp.where(is_prompt, yap_ref[...], yas_ref[...])
    parts = []
    for hh in range(GLA_HEADS):
        o = og[:, hh * GLA_HV:(hh + 1) * GLA_HV]
        parts.append(o * lax.rsqrt(jnp.mean(o * o, -1, keepdims=True) + EPS))
    y_gla = jnp.concatenate(parts, 1) * ng_ref[...] * gr_ref[...].astype(F32)
    a = _dot(ya.astype(BF16), wa_ref[...])
    b = _dot(y_gla.astype(BF16), wg_ref[...])
    hm = gate_ref[:, :D_MODEL].astype(F32) * a + gate_ref[:, D_MODEL:].astype(F32) * b
    mix = _dot(hm.astype(BF16), wo_ref[...])
    h1 = _ln(DN_ALPHA * h + mix, g1_ref[...], b1_ref[...])
    _store_row_tiles(h1t_ref, h1)
    h_hi = h1.astype(BF16)
    h_lo = (h1 - h_hi.astype(F32)).astype(BF16)
    logits = _dot(h_hi, wrh_ref[...]) + _dot(h_lo, wrh_ref[...]) + _dot(h_hi, wrl_ref[...]) + br_ref[...]
    ids, wts_ref[...] = _route(logits.T, valid_ref[...] > 0.0)
    tm = ids.shape[1]
    e_row = lax.broadcasted_iota(jnp.int32, (N_EXPERTS, tm), 0)
    run = run_ref[:, 0:1]
    ranks = []
    for kk in range(2):
        onehot = (e_row == ids[kk:kk + 1]).astype(F32)
        before = _dot(onehot.astype(BF16), u_ref[...])
        ranks.append(jnp.sum(onehot * (run + before), 0, keepdims=True).astype(jnp.int32))
        run = run + jnp.sum(onehot, 1, keepdims=True)
    run_ref[...] = jnp.broadcast_to(run, run_ref.shape)
    o_row = lax.broadcasted_iota(jnp.int32, (8, tm), 0)
    ids_ref[...] = jnp.where(o_row == 2, ranks[0], jnp.where(o_row == 3, ranks[1], ids))
    cnt_ref[...] = run_ref[...].astype(jnp.int32)


def _merge(h, ya_p, ya_s, og_p, og_s, gr, gate, valid, ng, wa, wg, wo, g1, b1, wrh, wrl, br):
    n = h.shape[0]
    tm = ROW_TILE
    u = jnp.asarray(np.triu(np.ones((tm, tm), np.float32), 1), BF16)
    pt = ya_p.shape[0] // tm
    st = ya_s.shape[0] // tm
    row = lambda w: pl.BlockSpec((tm, w), lambda i: (i, 0))
    row_p = lambda w: pl.BlockSpec((tm, w), lambda i: (jnp.minimum(i, pt - 1), 0))
    row_s = lambda w: pl.BlockSpec((tm, w), lambda i: (jnp.clip(i - pt, 0, st - 1), 0))
    lane = lambda r: pl.BlockSpec((r, tm), lambda i: (0, i))
    full = lambda a: pl.BlockSpec(a.shape, lambda i: (0,) * a.ndim)
    return pl.pallas_call(
        functools.partial(_merge_kernel, prompt_tiles=pt),
        grid=(n // tm,),
        in_specs=[row(D_MODEL), row_p(Q_W), row_s(Q_W), row_p(GLA_DV), row_s(GLA_DV), row(GLA_DV), row(2 * D_MODEL),
                  lane(1), full(ng), full(wa), full(wg), full(wo), full(g1), full(b1),
                  full(wrh), full(wrl), full(br), full(u)],
        out_specs=[pl.BlockSpec((tm * ROW_CHUNKS, LANES), lambda i: (i, 0)), lane(8), lane(8),
                   pl.BlockSpec((N_EXPERTS, LANES), lambda i: (0, 0))],
        out_shape=[jax.ShapeDtypeStruct((n * ROW_CHUNKS, LANES), F32),
                   jax.ShapeDtypeStruct((8, n), jnp.int32), jax.ShapeDtypeStruct((8, n), F32),
                   jax.ShapeDtypeStruct((N_EXPERTS, LANES), jnp.int32)],
        scratch_shapes=[pltpu.VMEM((N_EXPERTS, LANES), F32)],
        compiler_params=pltpu.CompilerParams(dimension_semantics=("arbitrary",), vmem_limit_bytes=VMEM_LIMIT),
        name="merge",
    )(h, ya_p, ya_s, og_p, og_s, gr, gate, valid, ng, wa, wg, wo, g1, b1, wrh, wrl, br, u)


def _gather_row_tiles(idx_ref, idx0, src_hbm, dst, sem, n):
    def body(r, carry):
        t = idx_ref[idx0 + r]
        pltpu.make_async_copy(src_hbm.at[pl.ds(pl.multiple_of(t * ROW_CHUNKS, ROW_CHUNKS), ROW_CHUNKS), :],
                              dst.at[pl.ds(pl.multiple_of(r * ROW_CHUNKS, ROW_CHUNKS), ROW_CHUNKS), :], sem).start()
        return carry
    lax.fori_loop(0, n, body, 0, unroll=8)


def _wait_row_tiles(src_hbm, dst, sem, n):
    pltpu.make_async_copy(src_hbm.at[pl.ds(0, n * ROW_CHUNKS), :], dst, sem).wait()


def _tiles(ref, first, n=1):
    return ref.at[pl.ds(pl.multiple_of(first * ROW_CHUNKS, ROW_CHUNKS), n * ROW_CHUNKS), :]


def _dispatch_kernel(dest_ref, start_ref, cnt_ref, nt_ref, h_hbm, xs_hbm, buf, zbuf, sem_in, sem_out, sem_z,
                     *, ranges, n_rows, max_tiles):
    CH = DISPATCH_CHUNK
    T = FFN_TILE
    c = pl.program_id(0)
    n_chunks = pl.num_programs(0)

    def load(row0, slot):
        return pltpu.make_async_copy(_tiles(h_hbm, row0, CH), buf.at[slot], sem_in.at[slot])

    def scatter(row0, slot):
        def body(r, carry):
            for kk in range(2):
                d = dest_ref[kk * n_rows + row0 + r]
                pltpu.make_async_copy(_tiles(buf.at[slot], r), _tiles(xs_hbm, d), sem_out.at[slot]).start()
            return carry
        lax.fori_loop(0, CH, body, 0, unroll=8)

    def drain(slot):
        for _ in range(2):
            pltpu.make_async_copy(buf.at[slot], _tiles(xs_hbm, 0, CH), sem_out.at[slot]).wait()

    groups, per_group, first, stride, extra, extra_first = ranges

    def row0(j):
        in_group = first + (j // per_group) * stride + (j % per_group) * CH
        return jnp.where(j < groups * per_group, in_group, extra_first + (j - groups * per_group) * CH)

    @pl.when(c == 0)
    def _():
        for j in range(2):
            load(row0(j), j).start()

    slot = c % 3
    load(row0(c), slot).wait()
    scatter(row0(c), slot)

    @pl.when(c > 0)
    def _():
        drain((c + 2) % 3)

    @pl.when(c + 2 < n_chunks)
    def _():
        load(row0(c + 2), (c + 2) % 3).start()

    @pl.when(c == n_chunks - 1)
    def _():
        drain(slot)
        _zero_unowned_slots(start_ref, cnt_ref, nt_ref, xs_hbm, zbuf, sem_z, max_tiles)


def _zero_unowned_slots(start_ref, cnt_ref, nt_ref, xs_hbm, zbuf, sem_z, max_tiles):
    T = FFN_TILE
    zbuf[...] = jnp.zeros_like(zbuf)

    def tail_copies(e, wait):
        cnt = cnt_ref[e]
        n = (T - (cnt & (T - 1))) & (T - 1)
        first = start_ref[e] + cnt
        for bit in reversed(range(T.bit_length() - 1)):
            size = 1 << bit

            @pl.when((n & size) != 0)
            def _():
                cp = pltpu.make_async_copy(_tiles(zbuf, 0, size),
                                           _tiles(xs_hbm, first + ((n >> (bit + 1)) << (bit + 1)), size), sem_z)
                cp.wait() if wait else cp.start()

    def unused_tile(t, wait):
        cp = pltpu.make_async_copy(zbuf, _tiles(xs_hbm, t * T, T), sem_z)
        cp.wait() if wait else cp.start()

    for wait in (False, True):
        def per_expert(e, carry, wait=wait):
            tail_copies(e, wait)
            return carry

        def per_tile(t, carry, wait=wait):
            unused_tile(t, wait)
            return carry
        lax.fori_loop(0, N_EXPERTS, per_expert, 0)
        lax.fori_loop(nt_ref[0], max_tiles, per_tile, 0)


def _dispatch(dest, start, counts, n_tiles, h1t, ranges, max_tiles):
    T = FFN_TILE
    n_rows = h1t.shape[0] // ROW_CHUNKS
    return pl.pallas_call(
        functools.partial(_dispatch_kernel, ranges=ranges, n_rows=n_rows, max_tiles=max_tiles),
        grid_spec=pltpu.PrefetchScalarGridSpec(
            num_scalar_prefetch=4,
            grid=(ranges[0] * ranges[1] + ranges[4],),
            in_specs=[pl.BlockSpec(memory_space=pl.ANY)],
            out_specs=pl.BlockSpec(memory_space=pl.ANY),
            scratch_shapes=[pltpu.VMEM((3, DISPATCH_CHUNK * ROW_CHUNKS, LANES), F32),
                            pltpu.VMEM((T * ROW_CHUNKS, LANES), F32),
                            pltpu.SemaphoreType.DMA((3,)), pltpu.SemaphoreType.DMA((3,)), pltpu.SemaphoreType.DMA]),
        out_shape=jax.ShapeDtypeStruct((max_tiles * T * ROW_CHUNKS, LANES), F32),
        compiler_params=pltpu.CompilerParams(dimension_semantics=("arbitrary",)),
        name="dispatch",
    )(dest, start, counts, n_tiles, h1t)


def _ffn_kernel(te_ref, nxt_ref, nt_ref, x_ref, wg_hbm, wu_hbm, wd_hbm, out_ref, stage_g, stage_u, stage_d, sem,
                wgb, wub, wdb):
    i = pl.program_id(0)
    nt = nt_ref[0]
    T = FFN_TILE

    def stage(e):
        return [pltpu.make_async_copy(src.at[e], dst, sem.at[n])
                for n, (src, dst) in enumerate(((wg_hbm, stage_g), (wu_hbm, stage_u), (wd_hbm, stage_d)))]

    @pl.when(i == 0)
    def _():
        for cp in stage(te_ref[0]):
            cp.start()

    @pl.when(i < nt)
    def _():
        @pl.when((i == 0) | (te_ref[i] != te_ref[jnp.maximum(i - 1, 0)]))
        def _():
            for cp in stage(te_ref[i]):
                cp.wait()
            wgb[...] = stage_g[...].astype(BF16)
            wub[...] = stage_u[...].astype(BF16)
            wdb[...] = stage_d[...].astype(BF16)

            @pl.when(nxt_ref[i] >= 0)
            def _():
                for cp in stage(nxt_ref[i]):
                    cp.start()

        x = _load_row_tiles(x_ref, T).astype(BF16)
        g = _dot(x, wgb[...])
        u = _dot(x, wub[...])
        _store_row_tiles(out_ref, _dot((g * _sigmoid(g) * u).astype(BF16), wdb[...]))

    @pl.when(i >= nt)
    def _():
        out_ref[...] = jnp.zeros_like(out_ref)


def _ffn(tile_expert, next_expert, n_tiles, xs, w_g, w_u, w_d):
    T = FFN_TILE
    max_tiles = tile_expert.shape[0]
    hbm = pl.BlockSpec(memory_space=pl.ANY)
    tile = lambda imap: pl.BlockSpec((T * ROW_CHUNKS, LANES), imap)
    return pl.pallas_call(
        _ffn_kernel,
        grid_spec=pltpu.PrefetchScalarGridSpec(
            num_scalar_prefetch=3,
            grid=(max_tiles,),
            in_specs=[tile(lambda i, te, nxt, nt: (jnp.minimum(i, nt[0] - 1), 0)), hbm, hbm, hbm],
            out_specs=tile(lambda i, te, nxt, nt: (i, 0)),
            scratch_shapes=[pltpu.VMEM(w_g.shape[1:], F32), pltpu.VMEM(w_u.shape[1:], F32),
                            pltpu.VMEM(w_d.shape[1:], F32), pltpu.SemaphoreType.DMA((3,)),
                            pltpu.VMEM(w_g.shape[1:], BF16), pltpu.VMEM(w_u.shape[1:], BF16),
                            pltpu.VMEM(w_d.shape[1:], BF16)]),
        out_shape=jax.ShapeDtypeStruct(xs.shape, F32),
        compiler_params=pltpu.CompilerParams(dimension_semantics=("arbitrary",)),
        name="ffn",
    )(tile_expert, next_expert, n_tiles, xs, w_g, w_u, w_d)


def _combine_kernel(dest_ref, h_ref, w_ref, g_ref, b_ref, ys_hbm, y_ref, buf, sem, *, blk, n_rows):
    i = pl.program_id(0)
    n = pl.num_programs(0)
    T = CMB_TILE

    def gather(step, slot):
        for kk in range(2):
            _gather_row_tiles(dest_ref, kk * n_rows + blk(step) * T, ys_hbm, buf.at[slot, kk], sem.at[slot], T)

    @pl.when(i == 0)
    def _():
        gather(0, 0)

    @pl.when(i + 1 < n)
    def _():
        gather(i + 1, (i + 1) % 2)

    slot = i % 2
    for kk in range(2):
        _wait_row_tiles(ys_hbm, buf.at[slot, kk], sem.at[slot], T)
    w = w_ref[...]
    ff = w[:, 0:1] * _load_row_tiles(buf.at[slot, 0], T) + w[:, 1:2] * _load_row_tiles(buf.at[slot, 1], T)
    y_ref[...] = _ln(DN_ALPHA * _load_row_tiles(h_ref, T) + ff, g_ref[...], b_ref[...])


def _combine(dest, h1t, wts_t, ys, g2, b2, n_out, first_block, blocks_per_batch, skip_blocks):
    T = CMB_TILE
    n_rows = h1t.shape[0] // ROW_CHUNKS
    if skip_blocks:
        blk = lambda i: first_block + i + (i // blocks_per_batch + 1) * skip_blocks
    else:
        blk = lambda i: first_block + i
    full = lambda a: pl.BlockSpec(a.shape, lambda i, d: (0,) * a.ndim)
    return pl.pallas_call(
        functools.partial(_combine_kernel, blk=blk, n_rows=n_rows),
        grid_spec=pltpu.PrefetchScalarGridSpec(
            num_scalar_prefetch=1,
            grid=(n_out // T,),
            in_specs=[pl.BlockSpec((T * ROW_CHUNKS, LANES), lambda i, d: (blk(i), 0)),
                      pl.BlockSpec((T, 2), lambda i, d: (blk(i), 0)),
                      full(g2), full(b2), pl.BlockSpec(memory_space=pl.ANY)],
            out_specs=pl.BlockSpec((T, D_MODEL), lambda i, d: (i, 0)),
            scratch_shapes=[pltpu.VMEM((2, 2, T * ROW_CHUNKS, LANES), F32), pltpu.SemaphoreType.DMA((2,))]),
        out_shape=jax.ShapeDtypeStruct((n_out, D_MODEL), F32),
        compiler_params=pltpu.CompilerParams(dimension_semantics=("arbitrary",)),
        name="combine",
    )(dest, h1t, wts_t, g2, b2, ys)


def _dispatch_plan(routing, counts, max_tiles):
    T = FFN_TILE
    tiles_e = (counts + T - 1) // T
    tile_end = jnp.cumsum(tiles_e)
    n_tiles = tile_end[-1]
    start = (tile_end - tiles_e) * T
    ids, rank = routing[0:2], routing[2:4]
    onehot = (ids[..., None] == jnp.arange(N_EXPERTS, dtype=jnp.int32)).astype(jnp.int32)
    dest = (jnp.sum(onehot * start, axis=-1) + jnp.where(ids >= 0, rank, 0)).reshape(-1).astype(jnp.int32)
    experts = jnp.arange(N_EXPERTS, dtype=jnp.int32)
    tiles = jnp.arange(max_tiles, dtype=jnp.int32)
    te = jnp.sum((tiles[:, None] >= tile_end[None, :]).astype(jnp.int32), axis=1)
    te_last = jnp.max(jnp.where(counts > 0, experts, 0))
    te = jnp.where(tiles < n_tiles, te, te_last).astype(jnp.int32)
    later = jnp.where((counts > 0)[None, :] & (experts[None, :] > te[:, None]), experts[None, :], N_EXPERTS)
    nxt = jnp.min(later, axis=1)
    nxt = jnp.where(nxt < N_EXPERTS, nxt, -1).astype(jnp.int32)
    return te, nxt, n_tiles.reshape(1).astype(jnp.int32), start.astype(jnp.int32), dest


def kernel(x_prompt, x_sample, state_swa_k, state_swa_v, state_gla, meta_tokens, ln_emb_g, ln_emb_b, w_in, b_gate, attn_sink, w_alpha2, b_alpha, gla_norm_g, w_attn_br, w_gla_br, w_out, ln1_g, ln1_b, w_router_group, b_router_group, w_router_expert, b_router_expert, w_exp_gate, w_exp_up, w_exp_down, ln2_g, ln2_b):
    B, seq, _ = x_prompt.shape
    n_seq, t_s, _ = x_sample.shape
    depth = w_in.shape[0]
    assert depth == 1 and seq % ATT_BLOCK == 0 and t_s == 8 and SKIP_ROWS == ATT_BLOCK == WINDOW
    lp = SKIP_ROWS + seq
    NP, NS = B * lp, n_seq * t_s
    NR = NP + NS
    assert NP % ROW_TILE == 0 and NS % ROW_TILE == 0 and n_seq % SAMPLE_SEQS == 0
    assert seq % DISPATCH_CHUNK == 0 and NS % DISPATCH_CHUNK == 0
    l = 0
    row2 = lambda a: a.reshape(1, -1)

    head = jnp.concatenate([jnp.zeros((FRONT_PAD, D_MODEL), F32), meta_tokens], axis=0)
    pos = np.arange(NR)
    moe_valid = jnp.asarray(~((pos < NP) & (pos % lp < SKIP_ROWS)), F32).reshape(1, NR)

    wi = w_in[l]
    c_ga = sum((Q_W, KV_W, KV_W, GLA_DK, GLA_DK, GLA_DV))
    assert c_ga == W_IN_SPLIT
    w_pieces = (wi[:, :c_ga].astype(BF16), wi[:, c_ga + GLA_RANK:].astype(BF16),
                jnp.pad(wi[:, c_ga:c_ga + GLA_RANK], ((0, 0), (0, 128 - GLA_RANK))).astype(BF16))
    wa2_bf = jnp.concatenate([w_alpha2[l], jnp.zeros((128 - GLA_RANK, GLA_DK), F32)], axis=0).astype(BF16)

    h, q, k, v, gq, gk, gv, la, gr, gate = _inproj(x_prompt, x_sample.reshape(NS, D_MODEL), head, row2(ln_emb_g),
                                                   row2(ln_emb_b), w_pieces, wa2_bf, row2(b_alpha[l]),
                                                   row2(b_gate[l]))

    sink = attn_sink[l]
    ya_p, k_p, v_p = _swa_prompt(sink, q, k, v, B, lp)
    buf_k = state_swa_k[l].reshape(n_seq, WINDOW, KV_W)
    buf_v = state_swa_v[l].reshape(n_seq, WINDOW, KV_W)
    ya_s, nk_s, nv_s = _swa_sample(sink, q, k, v, buf_k, buf_v, NP, t_s)

    og_p, s_p = _gla_prompt(gq, gk, gv, la, B, lp)
    og_s, s_s = _gla_sample(gq, gk, gv, la, state_gla[l], NP, t_s)

    wr = jnp.concatenate([w_router_expert[l], w_router_group[l],
                          jnp.zeros((D_MODEL, LANES - N_EXPERTS - N_GROUPS), F32)], axis=1)
    br = jnp.concatenate([b_router_expert[l], b_router_group[l],
                          jnp.zeros((LANES - N_EXPERTS - N_GROUPS,), F32)]).reshape(1, LANES)
    wr_hi = wr.astype(BF16)
    h1t, routing, wts, counts = _merge(h, ya_p, ya_s, og_p, og_s, gr, gate, moe_valid,
                                       row2(gla_norm_g[l]),
                                       w_attn_br[l].astype(BF16), w_gla_br[l].astype(BF16), w_out[l].astype(BF16),
                                       row2(ln1_g[l]), row2(ln1_b[l]), wr_hi, (wr - wr_hi.astype(F32)).astype(BF16), br)

    n_tok = B * seq + NS
    max_tiles = (2 * n_tok) // FFN_TILE + N_EXPERTS
    counts = counts[:, 0]
    te, nxt, n_tiles, start, dest = _dispatch_plan(routing[0:4], counts, max_tiles)
    routed = (B, seq // DISPATCH_CHUNK, SKIP_ROWS, lp, NS // DISPATCH_CHUNK, NP)
    xs = _dispatch(dest, start, counts, n_tiles, h1t, routed, max_tiles)
    ys = _ffn(te, nxt, n_tiles, xs, w_exp_gate[l], w_exp_up[l], w_exp_down[l])

    wts_t = wts[0:2].T
    g2, b2 = row2(ln2_g[l]), row2(ln2_b[l])
    skip_blocks = SKIP_ROWS // CMB_TILE
    y_p = _combine(dest, h1t, wts_t, ys, g2, b2, B * seq, 0, seq // CMB_TILE, skip_blocks)
    y_s = _combine(dest, h1t, wts_t, ys, g2, b2, NS, NP // CMB_TILE, 1, 0)

    kv_shape = (1, B, WINDOW, N_KV, HEAD_DIM)
    return (y_p.reshape(B, seq, D_MODEL), y_s.reshape(n_seq, t_s, D_MODEL),
            k_p.reshape(kv_shape), v_p.reshape(kv_shape), s_p[None],
            nk_s.reshape(1, n_seq, WINDOW, N_KV, HEAD_DIM), nv_s.reshape(1, n_seq, WINDOW, N_KV, HEAD_DIM),
            s_s[None])
```

```python
import functools

import numpy as np
import jax
import jax.numpy as jnp
from jax import lax
from jax.experimental import pallas as pl
from jax.experimental.pallas import tpu as pltpu

F32 = jnp.float32
BF16 = jnp.bfloat16

D_MODEL = 1024
N_META = 16
HEAD_DIM = 64
N_HEADS = 8
N_KV = 2
Q_PER_KV = 4
WINDOW = 128
ATT_BLOCK = 128
GLA_HEADS = 4
GLA_HK = 128
GLA_HV = 256
GLA_DK = GLA_HEADS * GLA_HK
GLA_DV = GLA_HEADS * GLA_HV
GLA_RANK = 16
GLA_TAU = 16.0
GLA_CHUNK = 64
N_GROUPS = 4
EXP_PER_GROUP = 8
N_EXPERTS = 32
D_EXPERT = 256
DN_ALPHA = 2.0 ** 0.25
EPS = 1e-5
NEG = -1e30

FRONT_PAD = (-N_META) % ATT_BLOCK
SKIP_ROWS = FRONT_PAD + N_META

Q_W, KV_W = N_HEADS * HEAD_DIM, N_KV * HEAD_DIM
SEG = {}
_o = 0
for _n, _w in (("q", Q_W), ("k", KV_W), ("v", KV_W), ("gq", GLA_DK), ("gk", GLA_DK), ("gv", GLA_DV)):
    SEG[_n] = (0, _o, _o + _w)
    _o += _w
W_IN_SPLIT = _o
SEG["gr"] = (1, 0, GLA_DV)
SEG["gate"] = (1, GLA_DV, GLA_DV + 2 * D_MODEL)
SEG["ga"] = (2, 0, 128)

ROW_TILE = 512
FFN_TILE = 256
CMB_TILE = 128
DISPATCH_CHUNK = 256
GLA_INTRA_CHUNKS = 6
SWA_BLOCKS = 1
INPROJ_TILE = 512
SAMPLE_SEQS = 8
LANES = 128
ROW_CHUNKS = D_MODEL // LANES
VMEM_LIMIT = 56 * 1024 * 1024


def _ln(x, g, b):
    mu = jnp.mean(x, -1, keepdims=True)
    xc = x - mu
    var = jnp.mean(xc * xc, -1, keepdims=True)
    return xc * lax.rsqrt(var + EPS) * g + b


def _sigmoid(x):
    return 0.5 * jnp.tanh(0.5 * x) + 0.5


def _dot(a, b):
    return jnp.dot(a, b, preferred_element_type=F32)


def _dot_nt(a, b):
    return lax.dot_general(a, b, (((1,), (1,)), ((), ())), preferred_element_type=F32)


def _inproj_kernel(*refs, blocks, batch_blocks, prompt_blocks):
    xp_refs, xs_refs = refs[:blocks], refs[blocks:2 * blocks]
    (head_ref, g_ref, b_ref, w0_ref, w1_ref, w2_ref, wa2_ref, ba_ref, bg_ref,
     h_ref, q_ref, k_ref, v_ref, gq_ref, gk_ref, gv_ref, la_ref, gr_ref, gate_ref) = refs[2 * blocks:]
    w_refs = (w0_ref, w1_ref, w2_ref)
    row = lax.broadcasted_iota(jnp.int32, (ATT_BLOCK, 1), 0)
    xs, keeps = [], []
    for s in range(blocks):
        p = pl.program_id(0) * blocks + s
        is_sample = p >= prompt_blocks
        is_head = jnp.logical_and(jnp.logical_not(is_sample), p % batch_blocks == 0)
        xs.append(jnp.where(is_sample, xs_refs[s][...], jnp.where(is_head, head_ref[...], xp_refs[s][...])))
        keeps.append(jnp.where(jnp.logical_and(is_head, row < FRONT_PAD), 0.0, 1.0))
    h = _ln(jnp.concatenate(xs, 0), g_ref[...], b_ref[...])
    h_ref[...] = h
    hb = h.astype(BF16)
    keep = jnp.concatenate(keeps, 0)

    def seg(name):
        piece, a, b = SEG[name]
        return _dot(hb, w_refs[piece][:, a:b])

    qkv = _dot(hb, w0_ref[:, SEG["q"][1]:SEG["v"][2]])
    q_ref[...] = qkv[:, :Q_W]
    k_ref[...] = qkv[:, Q_W:Q_W + KV_W]
    v_ref[...] = qkv[:, Q_W + KV_W:]
    gq_ref[...] = seg("gq") * (GLA_HK ** -0.5)
    gk_ref[...] = seg("gk") * keep
    gv_ref[...] = seg("gv") * keep
    gr = seg("gr")
    gr_ref[...] = (gr * _sigmoid(gr)).astype(BF16)
    gate_ref[...] = _sigmoid(seg("gate") + bg_ref[...]).astype(BF16)
    z = _dot(seg("ga").astype(BF16), wa2_ref[...]) + ba_ref[...]
    la = (jnp.minimum(z, 0.0) - jnp.log(1.0 + jnp.exp(-jnp.abs(z)))) * (1.0 / GLA_TAU)
    la_ref[...] = la * keep


def _inproj(x_prompt, x_sample, head, ln_g, ln_b, w_pieces, wa2_bf, b_alpha, b_gate):
    B, seq, _ = x_prompt.shape
    blk = ATT_BLOCK
    seq_blocks = seq // blk
    batch_blocks = seq_blocks + 1
    prompt_blocks = B * batch_blocks
    sample_blocks = x_sample.shape[0] // blk
    n = (prompt_blocks + sample_blocks) * blk
    tm = INPROJ_TILE
    blocks = tm // blk
    xp = x_prompt.reshape(B * seq, D_MODEL)

    def prompt_block(s):
        def imap(i):
            p = jnp.minimum(i * blocks + s, prompt_blocks - 1)
            return (p // batch_blocks * seq_blocks + jnp.maximum(p % batch_blocks - 1, 0), 0)
        return pl.BlockSpec((blk, D_MODEL), imap)

    def sample_block(s):
        return pl.BlockSpec((blk, D_MODEL), lambda i: (jnp.clip(i * blocks + s - prompt_blocks, 0, sample_blocks - 1), 0))

    widths = [D_MODEL, Q_W, KV_W, KV_W, GLA_DK, GLA_DK, GLA_DV, GLA_DK, GLA_DV, 2 * D_MODEL]
    row = lambda w: pl.BlockSpec((tm, w), lambda i: (i, 0))
    const = lambda a: pl.BlockSpec(a.shape, lambda i: (0,) * a.ndim, pipeline_mode=pl.Buffered(1))
    return pl.pallas_call(
        functools.partial(_inproj_kernel, blocks=blocks, batch_blocks=batch_blocks, prompt_blocks=prompt_blocks),
        grid=(n // tm,),
        in_specs=[prompt_block(s) for s in range(blocks)] + [sample_block(s) for s in range(blocks)]
                 + [const(a) for a in (head, ln_g, ln_b, *w_pieces, wa2_bf, b_alpha, b_gate)],
        out_specs=[row(w) for w in widths],
        out_shape=[jax.ShapeDtypeStruct((n, w), BF16 if i >= len(widths) - 2 else F32) for i, w in enumerate(widths)],
        compiler_params=pltpu.CompilerParams(dimension_semantics=("parallel",), vmem_limit_bytes=VMEM_LIMIT),
        name="inproj",
    )(*([xp] * blocks), *([x_sample] * blocks), head, ln_g, ln_b, *w_pieces, wa2_bf, b_alpha, b_gate)


def _softmax_pv(s, sink, vv):
    m = jnp.maximum(jnp.max(s, -1, keepdims=True), sink)
    p = jnp.exp(s - m)
    l = jnp.sum(p, -1, keepdims=True) + jnp.exp(sink - m)
    return _dot(p.astype(BF16), vv) / l


def _swa_prompt_kernel(sink_ref, bias_ref, q_ref, kp_ref, kc_ref, vp_ref, vc_ref, o_ref, kl_ref, vl_ref):
    nq = q_ref.shape[0] // ATT_BLOCK
    q_all = q_ref[...] * (HEAD_DIM ** -0.5)
    kb = jnp.concatenate([kp_ref[...], kc_ref[...]], 0)
    vb = jnp.concatenate([vp_ref[...], vc_ref[...]], 0)
    low = lax.broadcasted_iota(jnp.int32, (1, 2 * HEAD_DIM), 1) < HEAD_DIM
    k_low = jnp.where(low, kb, 0.0)
    k_high = jnp.where(low, 0.0, kb)
    keys = {(0, 0): k_low.astype(BF16), (0, 1): pltpu.roll(k_low, HEAD_DIM, 1).astype(BF16),
            (1, 0): pltpu.roll(k_high, HEAD_DIM, 1).astype(BF16), (1, 1): k_high.astype(BF16)}
    ones_col = (lax.broadcasted_iota(jnp.int32, (1, 2 * HEAD_DIM), 1) == HEAD_DIM).astype(F32)
    values = [jnp.where(low, vb, ones_col).astype(BF16),
              jnp.where(low, pltpu.roll(vb, HEAD_DIM, 1), ones_col).astype(BF16)]
    for blk in range(nq):
        rows = slice(blk * ATT_BLOCK, (blk + 1) * ATT_BLOCK)
        band = slice(blk * ATT_BLOCK, (blk + 2) * ATT_BLOCK)
        variant = jnp.minimum(pl.program_id(1) * nq + blk, 2)
        for pair in range(N_HEADS // 2):
            qp = q_all[rows, pair * 2 * HEAD_DIM:(pair + 1) * 2 * HEAD_DIM]
            outs = []
            for half in range(2):
                h = 2 * pair + half
                kv = h // Q_PER_KV
                qm = jnp.where(low if half == 0 else jnp.logical_not(low), qp, 0.0).astype(BF16)
                s = _dot_nt(qm, keys[(kv, half)][band]) + bias_ref[variant, h]
                m = jnp.maximum(jnp.max(s, -1, keepdims=True), sink_ref[h])
                pv = _dot(jnp.exp(s - m).astype(BF16), values[kv][band])
                outs.append(pv / (pv[:, HEAD_DIM:HEAD_DIM + 1] + jnp.exp(sink_ref[h] - m)))
            o_ref[rows, pair * 2 * HEAD_DIM:(pair + 1) * 2 * HEAD_DIM] = jnp.where(low, outs[0],
                                                                                   pltpu.roll(outs[1], HEAD_DIM, 1))

    @pl.when(pl.program_id(1) == pl.num_programs(1) - 1)
    def _():
        kl_ref[0] = kc_ref[(nq - 1) * ATT_BLOCK:nq * ATT_BLOCK, :]
        vl_ref[0] = vc_ref[(nq - 1) * ATT_BLOCK:nq * ATT_BLOCK, :]


def _swa_bias_table():
    r = np.arange(ATT_BLOCK)[:, None]
    c = np.arange(2 * ATT_BLOCK)[None, :]
    dist = r - c + ATT_BLOCK
    slopes = 2.0 ** -(np.arange(N_HEADS) + 1.0)
    table = np.empty((3, N_HEADS, ATT_BLOCK, 2 * ATT_BLOCK), np.float32)
    for j in range(3):
        seen = (dist >= 0) & (dist < WINDOW) & ((j - 1) * ATT_BLOCK + c - FRONT_PAD >= 0)
        table[j] = np.where(seen[None], -slopes[:, None, None] * dist[None], NEG)
    return table


def _swa_prompt(sink, q, k, v, batch, lp):
    nq = SWA_BLOCKS
    nb = lp // ATT_BLOCK
    assert nb >= 3 and nb % nq == 0
    steps = nb // nq
    n = batch * lp
    bias = jnp.asarray(_swa_bias_table())
    cur = lambda w: pl.BlockSpec((nq * ATT_BLOCK, w), lambda b, j: (b * steps + j, 0))
    prev = lambda w: pl.BlockSpec((ATT_BLOCK, w), lambda b, j: (b * nb + jnp.maximum(j * nq - 1, 0), 0))
    last = pl.BlockSpec((1, ATT_BLOCK, KV_W), lambda b, j: (b, 0, 0))
    return pl.pallas_call(
        _swa_prompt_kernel,
        grid=(batch, steps),
        in_specs=[pl.BlockSpec(memory_space=pltpu.SMEM), pl.BlockSpec(bias.shape, lambda b, j: (0, 0, 0, 0)),
                  cur(Q_W), prev(KV_W), cur(KV_W), prev(KV_W), cur(KV_W)],
        out_specs=[cur(Q_W), last, last],
        out_shape=[jax.ShapeDtypeStruct((n, Q_W), F32), jax.ShapeDtypeStruct((batch, ATT_BLOCK, KV_W), F32),
                   jax.ShapeDtypeStruct((batch, ATT_BLOCK, KV_W), F32)],
        compiler_params=pltpu.CompilerParams(dimension_semantics=("parallel", "arbitrary")),
        name="swa_prompt",
    )(sink, bias, q, k, k, v, v)


def _swa_sample_kernel(sink_ref, q_ref, k_ref, v_ref, bk_ref, bv_ref, o_ref, nk_ref, nv_ref, *, t_s):
    nbuf = WINDOW
    span = 2 * WINDOW
    rows = Q_PER_KV * t_s
    r = lax.broadcasted_iota(jnp.int32, (rows, span), 0)
    c = lax.broadcasted_iota(jnp.int32, (rows, span), 1)
    t = r % t_s
    dist = t + nbuf - c
    mask = (dist >= 0) & (dist < WINDOW) & (c < nbuf + t_s)
    distf = dist.astype(F32)
    g_col = lax.broadcasted_iota(jnp.int32, (rows, 1), 0) // t_s
    fill = jnp.zeros((span - nbuf - t_s, KV_W), F32)

    def one_seq(s, carry):
        rs = pl.ds(pl.multiple_of(s * t_s, t_s), t_s)
        q = q_ref[rs, :]
        k_new = k_ref[rs, :]
        v_new = v_ref[rs, :]
        bk = bk_ref[s]
        bv = bv_ref[s]
        k_all = jnp.concatenate([bk, k_new, fill], 0)
        v_all = jnp.concatenate([bv, v_new, fill], 0)
        for kv in range(N_KV):
            qg = jnp.concatenate(
                [q[:, (kv * Q_PER_KV + g) * HEAD_DIM:(kv * Q_PER_KV + g + 1) * HEAD_DIM] for g in range(Q_PER_KV)], 0)
            kk = k_all[:, kv * HEAD_DIM:(kv + 1) * HEAD_DIM].astype(BF16)
            vv = v_all[:, kv * HEAD_DIM:(kv + 1) * HEAD_DIM].astype(BF16)
            slope = jnp.zeros((rows, 1), F32)
            sink = jnp.zeros((rows, 1), F32)
            for g in range(Q_PER_KV):
                h = kv * Q_PER_KV + g
                slope = jnp.where(g_col == g, 2.0 ** -(h + 1), slope)
                sink = jnp.where(g_col == g, sink_ref[h], sink)
            sc = _dot_nt(qg.astype(BF16), kk) * (HEAD_DIM ** -0.5) - slope * distf
            sc = jnp.where(mask, sc, NEG)
            o = _softmax_pv(sc, sink, vv)
            for g in range(Q_PER_KV):
                h = kv * Q_PER_KV + g
                o_ref[rs, h * HEAD_DIM:(h + 1) * HEAD_DIM] = o[g * t_s:(g + 1) * t_s]
        nk_ref[s, 0:nbuf - t_s, :] = bk[t_s:, :]
        nk_ref[s, nbuf - t_s:nbuf, :] = k_new
        nv_ref[s, 0:nbuf - t_s, :] = bv[t_s:, :]
        nv_ref[s, nbuf - t_s:nbuf, :] = v_new
        return carry

    lax.fori_loop(0, SAMPLE_SEQS, one_seq, 0, unroll=True)


def _swa_sample(sink, q, k, v, buf_k, buf_v, row0, t_s):
    n_seq = buf_k.shape[0]
    sb = SAMPLE_SEQS
    rb = sb * t_s
    b0 = row0 // rb
    rows = lambda w: pl.BlockSpec((rb, w), lambda i: (b0 + i, 0))
    bufs = pl.BlockSpec((sb, WINDOW, KV_W), lambda i: (i, 0, 0))
    return pl.pallas_call(
        functools.partial(_swa_sample_kernel, t_s=t_s),
        grid=(n_seq // sb,),
        in_specs=[pl.BlockSpec(memory_space=pltpu.SMEM), rows(Q_W), rows(KV_W), rows(KV_W), bufs, bufs],
        out_specs=[pl.BlockSpec((rb, Q_W), lambda i: (i, 0)), bufs, bufs],
        out_shape=[jax.ShapeDtypeStruct((n_seq * t_s, Q_W), F32),
                   jax.ShapeDtypeStruct(buf_k.shape, F32), jax.ShapeDtypeStruct(buf_v.shape, F32)],
        compiler_params=pltpu.CompilerParams(dimension_semantics=("parallel",)),
        name="swa_sample",
    )(sink, q, k, v, buf_k, buf_v)


def _gla_tables(chunk):
    t = np.arange(chunk)[:, None]
    u = np.arange(chunk)[None, :]
    masks = []
    w = chunk // 2
    while w >= 1:
        masks.append((t // (2 * w) == u // (2 * w)) & ((t // w) % 2 == 1) & ((u // w) % 2 == 0))
        w //= 2
    return (u <= t).astype(np.float32), np.stack(masks, 0).astype(np.float32)


def _level_exponents(b, la, w):
    C = b.shape[0]
    row = lax.broadcasted_iota(jnp.int32, b.shape, 0)
    if w >= 4:
        pieces = [jnp.broadcast_to(b[p + w - 1:p + w], (2 * w, b.shape[1])) for p in range(0, C, 2 * w)]
        ref = pieces[0] if len(pieces) == 1 else jnp.concatenate(pieces, 0)
        return jnp.where((row & w) != 0, b - ref, ref - b)
    if w == 2:
        m = row & 3
        nxt = pltpu.roll(la, C - 1, 0)
        prv = pltpu.roll(la, 1, 0)
        return jnp.where(m == 2, la, jnp.where(m == 3, la + prv, jnp.where(m == 0, nxt, 0.0)))
    return jnp.where((row & 1) != 0, la, 0.0)


def _split3(x):
    hi = x.astype(BF16)
    r1 = x - hi.astype(F32)
    mid = r1.astype(BF16)
    lo = (r1 - mid.astype(F32)).astype(BF16)
    return hi, mid, lo


def _gla_intra_kernel(g_ref, m_ref, q_ref, k_ref, v_ref, la_ref, o_ref, qe_ref, ke_ref, vt_ref, d_ref):
    C = GLA_CHUNK
    n_lvl = m_ref.shape[0]
    G = g_ref[...]
    eye = (lax.broadcasted_iota(jnp.int32, (C, C), 0) == lax.broadcasted_iota(jnp.int32, (C, C), 1)).astype(F32)
    for j in range(q_ref.shape[0] // C):
        rs = slice(j * C, (j + 1) * C)
        la = la_ref[rs, :]
        hi, mid, lo = _split3(la)
        b = _dot(G, hi) + _dot(G, mid) + _dot(G, lo)
        b_last = b[C - 1:C]
        q_all = q_ref[rs, :]
        k_all = k_ref[rs, :]
        qe_ref[0, rs, :] = (q_all * jnp.exp(b)).astype(BF16)
        ke_ref[0, rs, :] = (k_all * jnp.exp(b_last - b)).astype(BF16)
        d_ref[0, j] = jnp.broadcast_to(jnp.exp(b_last), (8, GLA_DK))
        q_lvl, k_lvl = [], []
        for l in range(n_lvl):
            El = jnp.exp(_level_exponents(b, la, C >> (l + 1)))
            q_lvl.append((q_all * El).astype(BF16))
            k_lvl.append((k_all * El).astype(BF16))
        for h in range(GLA_HEADS):
            ks = slice(h * GLA_HK, (h + 1) * GLA_HK)
            vs = slice(h * GLA_HV, (h + 1) * GLA_HV)
            v = v_ref[rs, vs]
            att = eye * jnp.sum(q_all[:, ks] * k_all[:, ks], -1, keepdims=True)
            for l in range(n_lvl):
                att = att + m_ref[l] * _dot_nt(q_lvl[l][:, ks], k_lvl[l][:, ks])
            o_ref[0, rs, vs] = _dot(att.astype(BF16), v.astype(BF16))
            vt_ref[0, j, vs, :] = v.T.astype(BF16)


def _gla_inter_kernel(o_ref, qe_ref, ke_ref, vt_ref, d_ref, og_ref, s_ref, st_ref):
    c = pl.program_id(0)
    batch = o_ref.shape[0]

    @pl.when(c == 0)
    def _():
        st_ref[...] = jnp.zeros_like(st_ref)

    for b in range(batch):
        for h in range(GLA_HEADS):
            ks = slice(h * GLA_HK, (h + 1) * GLA_HK)
            vs = slice(h * GLA_HV, (h + 1) * GLA_HV)
            st = st_ref[b * GLA_HEADS + h]
            og_ref[b, :, vs] = o_ref[b, :, vs] + _dot_nt(qe_ref[b, :, ks], st.astype(BF16))
            st_ref[b * GLA_HEADS + h] = st * d_ref[b, 0, 0:1, ks] + _dot(vt_ref[b, 0, vs, :], ke_ref[b, :, ks])

    @pl.when(c == pl.num_programs(0) - 1)
    def _():
        for b in range(batch):
            for h in range(GLA_HEADS):
                s_ref[b, h] = st_ref[b * GLA_HEADS + h].T


def _gla_prompt(gq, gk, gv, la, batch, lp):
    C = GLA_CHUNK
    nc = lp // C
    G, M = _gla_tables(C)
    G = jnp.asarray(G, BF16)
    M = jnp.asarray(M, F32)
    cps = GLA_INTRA_CHUNKS
    assert nc % cps == 0
    rows = lambda w: pl.BlockSpec((cps * C, w), lambda b, c: (b * (nc // cps) + c, 0))
    rows3 = lambda w: pl.BlockSpec((1, cps * C, w), lambda b, c: (b, c, 0))
    full = lambda a: pl.BlockSpec(a.shape, lambda b, c: (0,) * a.ndim)
    o_intra, qe, ke, vt, d = pl.pallas_call(
        _gla_intra_kernel,
        grid=(batch, nc // cps),
        in_specs=[full(G), full(M), rows(GLA_DK), rows(GLA_DK), rows(GLA_DV), rows(GLA_DK)],
        out_specs=[rows3(GLA_DV), rows3(GLA_DK), rows3(GLA_DK),
                   pl.BlockSpec((1, cps, GLA_DV, C), lambda b, c: (b, c, 0, 0)),
                   pl.BlockSpec((1, cps, 8, GLA_DK), lambda b, c: (b, c, 0, 0))],
        out_shape=[jax.ShapeDtypeStruct((batch, lp, GLA_DV), F32),
                   jax.ShapeDtypeStruct((batch, lp, GLA_DK), BF16), jax.ShapeDtypeStruct((batch, lp, GLA_DK), BF16),
                   jax.ShapeDtypeStruct((batch, nc, GLA_DV, C), BF16),
                   jax.ShapeDtypeStruct((batch, nc, 8, GLA_DK), F32)],
        compiler_params=pltpu.CompilerParams(dimension_semantics=("parallel", "parallel")),
        name="gla_intra",
    )(G, M, gq, gk, gv, la)
    chunk = lambda w: pl.BlockSpec((batch, C, w), lambda c: (0, c, 0))
    og, s_fin = pl.pallas_call(
        _gla_inter_kernel,
        grid=(nc,),
        in_specs=[chunk(GLA_DV), chunk(GLA_DK), chunk(GLA_DK),
                  pl.BlockSpec((batch, 1, GLA_DV, C), lambda c: (0, c, 0, 0)),
                  pl.BlockSpec((batch, 1, 8, GLA_DK), lambda c: (0, c, 0, 0))],
        out_specs=[chunk(GLA_DV), pl.BlockSpec((batch, GLA_HEADS, GLA_HK, GLA_HV), lambda c: (0, 0, 0, 0))],
        out_shape=[jax.ShapeDtypeStruct((batch, lp, GLA_DV), F32),
                   jax.ShapeDtypeStruct((batch, GLA_HEADS, GLA_HK, GLA_HV), F32)],
        scratch_shapes=[pltpu.VMEM((batch * GLA_HEADS, GLA_HV, GLA_HK), F32)],
        compiler_params=pltpu.CompilerParams(dimension_semantics=("arbitrary",)),
        name="gla_inter",
    )(o_intra, qe, ke, vt, d)
    return og.reshape(batch * lp, GLA_DV), s_fin


def _gla_sample_kernel(q_ref, k_ref, v_ref, la_ref, s0_ref, o_ref, s_ref, *, t_s):
    T = t_s
    row = lax.broadcasted_iota(jnp.int32, (T, GLA_HK), 0)
    k_fill = jnp.zeros((GLA_HK - T - 8, GLA_HK), F32)
    v_fill = jnp.zeros((GLA_HK - T, GLA_HV), F32)

    def one_seq(s, carry):
        rs = pl.ds(pl.multiple_of(s * T, T), T)
        for h in range(GLA_HEADS):
            ks = slice(h * GLA_HK, (h + 1) * GLA_HK)
            vs = slice(h * GLA_HV, (h + 1) * GLA_HV)
            q = q_ref[rs, ks]
            k = k_ref[rs, ks]
            v = v_ref[rs, vs]
            b = la_ref[rs, ks]
            sh = 1
            while sh < T:
                b = b + jnp.where(row >= sh, pltpu.roll(b, sh, 0), 0.0)
                sh *= 2
            S = s0_ref[s, h]
            o = _dot((q * jnp.exp(b)).astype(BF16), S.astype(BF16))
            for j in range(T):
                e = jnp.exp(jnp.where(row >= j, b - b[j:j + 1], NEG))
                a_col = jnp.sum(q * k[j:j + 1] * e, -1, keepdims=True)
                o = o + a_col * v[j:j + 1]
            o_ref[rs, vs] = o
            b_last = b[T - 1:T]
            ke = k * jnp.exp(b_last - b)
            kt = jnp.concatenate([ke, jnp.broadcast_to(jnp.exp(b_last), (8, GLA_HK)), k_fill], 0).T
            v_pad = jnp.concatenate([v, v_fill], 0)
            s_ref[s, h] = S * kt[:, T:T + 1] + _dot(kt.astype(BF16), v_pad.astype(BF16))
        return carry

    lax.fori_loop(0, SAMPLE_SEQS, one_seq, 0)


def _gla_sample(gq, gk, gv, la, s0, row0, t_s):
    n_seq = s0.shape[0]
    sb = SAMPLE_SEQS
    rb = sb * t_s
    b0 = row0 // rb
    rows = lambda w: pl.BlockSpec((rb, w), lambda i: (b0 + i, 0))
    st = pl.BlockSpec((sb, GLA_HEADS, GLA_HK, GLA_HV), lambda i: (i, 0, 0, 0))
    return pl.pallas_call(
        functools.partial(_gla_sample_kernel, t_s=t_s),
        grid=(n_seq // sb,),
        in_specs=[rows(GLA_DK), rows(GLA_DK), rows(GLA_DV), rows(GLA_DK), st],
        out_specs=[pl.BlockSpec((rb, GLA_DV), lambda i: (i, 0)), st],
        out_shape=[jax.ShapeDtypeStruct((n_seq * t_s, GLA_DV), F32), jax.ShapeDtypeStruct(s0.shape, F32)],
        compiler_params=pltpu.CompilerParams(dimension_semantics=("parallel",), vmem_limit_bytes=VMEM_LIMIT),
        name="gla_sample",
    )(gq, gk, gv, la, s0)


def _route(lt, valid):
    tm = lt.shape[1]
    el = lt[0:N_EXPERTS]
    gl = lt[N_EXPERTS:N_EXPERTS + N_GROUPS]
    g_max = jnp.max(gl, 0, keepdims=True)
    g_row = lax.broadcasted_iota(jnp.int32, (N_GROUPS, tm), 0)
    g_idx = jnp.min(jnp.where(gl == g_max, g_row, N_GROUPS), 0, keepdims=True)
    p_max = 1.0 / jnp.sum(jnp.exp(gl - g_max), 0, keepdims=True)
    e_row = lax.broadcasted_iota(jnp.int32, (N_EXPERTS, tm), 0)
    m1 = jnp.where(e_row // EXP_PER_GROUP == g_idx, el, -jnp.inf)
    v1 = jnp.max(m1, 0, keepdims=True)
    i1 = jnp.min(jnp.where(m1 == v1, e_row, N_EXPERTS), 0, keepdims=True)
    m2 = jnp.where(e_row == i1, -jnp.inf, m1)
    v2 = jnp.max(m2, 0, keepdims=True)
    i2 = jnp.min(jnp.where(m2 == v2, e_row, N_EXPERTS), 0, keepdims=True)
    e2 = jnp.exp(v2 - v1)
    w1 = p_max / (1.0 + e2)
    w2 = p_max * e2 / (1.0 + e2)
    o_row = lax.broadcasted_iota(jnp.int32, (8, tm), 0)
    ids = jnp.where(o_row == 0, i1, jnp.where(o_row == 1, i2, -1))
    return jnp.where(valid, ids, -1), jnp.where(o_row == 0, w1, jnp.where(o_row == 1, w2, 0.0))


def _store_row_tiles(ref, x):
    t, d = x.shape
    n = d // LANES
    for s in range(n):
        ref[pl.ds(s, t, stride=n), :] = x[:, s * LANES:(s + 1) * LANES]


def _load_row_tiles(ref, t, n=ROW_CHUNKS):
    return jnp.concatenate([ref[pl.ds(s, t, stride=n), :] for s in range(n)], axis=1)


def _merge_kernel(h_ref, yap_ref, yas_ref, ogp_ref, ogs_ref, gr_ref, gate_ref, valid_ref, ng_ref,
                  wa_ref, wg_ref, wo_ref, g1_ref, b1_ref, wrh_ref, wrl_ref, br_ref, u_ref,
                  h1t_ref, ids_ref, wts_ref, cnt_ref, run_ref, *, prompt_tiles):
    @pl.when(pl.program_id(0) == 0)
    def _():
        run_ref[...] = jnp.zeros_like(run_ref)

    h = h_ref[...]
    is_prompt = pl.program_id(0) < prompt_tiles
    og = jnp.where(is_prompt, ogp_ref[...], ogs_ref[...])
    ya = jnp.where(is_prompt, yap_ref[...], yas_ref[...])
    parts = []
    for hh in range(GLA_HEADS):
        o = og[:, hh * GLA_HV:(hh + 1) * GLA_HV]
        parts.append(o * lax.rsqrt(jnp.mean(o * o, -1, keepdims=True) + EPS))
    y_gla = jnp.concatenate(parts, 1) * ng_ref[...] * gr_ref[...].astype(F32)
    a = _dot(ya.astype(BF16), wa_ref[...])
    b = _dot(y_gla.astype(BF16), wg_ref[...])
    hm = gate_ref[:, :D_MODEL].astype(F32) * a + gate_ref[:, D_MODEL:].astype(F32) * b
    mix = _dot(hm.astype(BF16), wo_ref[...])
    h1 = _ln(DN_ALPHA * h + mix, g1_ref[...], b1_ref[...])
    _store_row_tiles(h1t_ref, h1)
    h_hi = h1.astype(BF16)
    h_lo = (h1 - h_hi.astype(F32)).astype(BF16)
    logits = _dot(h_hi, wrh_ref[...]) + _dot(h_lo, wrh_ref[...]) + _dot(h_hi, wrl_ref[...]) + br_ref[...]
    ids, wts_ref[...] = _route(logits.T, valid_ref[...] > 0.0)
    tm = ids.shape[1]
    e_row = lax.broadcasted_iota(jnp.int32, (N_EXPERTS, tm), 0)
    run = run_ref[:, 0:1]
    ranks = []
    for kk in range(2):
        onehot = (e_row == ids[kk:kk + 1]).astype(F32)
        before = _dot(onehot.astype(BF16), u_ref[...])
        ranks.append(jnp.sum(onehot * (run + before), 0, keepdims=True).astype(jnp.int32))
        run = run + jnp.sum(onehot, 1, keepdims=True)
    run_ref[...] = jnp.broadcast_to(run, run_ref.shape)
    o_row = lax.broadcasted_iota(jnp.int32, (8, tm), 0)
    ids_ref[...] = jnp.where(o_row == 2, ranks[0], jnp.where(o_row == 3, ranks[1], ids))
    cnt_ref[...] = run_ref[...].astype(jnp.int32)


def _merge(h, ya_p, ya_s, og_p, og_s, gr, gate, valid, ng, wa, wg, wo, g1, b1, wrh, wrl, br):
    n = h.shape[0]
    tm = ROW_TILE
    u = jnp.asarray(np.triu(np.ones((tm, tm), np.float32), 1), BF16)
    pt = ya_p.shape[0] // tm
    st = ya_s.shape[0] // tm
    row = lambda w: pl.BlockSpec((tm, w), lambda i: (i, 0))
    row_p = lambda w: pl.BlockSpec((tm, w), lambda i: (jnp.minimum(i, pt - 1), 0))
    row_s = lambda w: pl.BlockSpec((tm, w), lambda i: (jnp.clip(i - pt, 0, st - 1), 0))
    lane = lambda r: pl.BlockSpec((r, tm), lambda i: (0, i))
    full = lambda a: pl.BlockSpec(a.shape, lambda i: (0,) * a.ndim)
    return pl.pallas_call(
        functools.partial(_merge_kernel, prompt_tiles=pt),
        grid=(n // tm,),
        in_specs=[row(D_MODEL), row_p(Q_W), row_s(Q_W), row_p(GLA_DV), row_s(GLA_DV), row(GLA_DV), row(2 * D_MODEL),
                  lane(1), full(ng), full(wa), full(wg), full(wo), full(g1), full(b1),
                  full(wrh), full(wrl), full(br), full(u)],
        out_specs=[pl.BlockSpec((tm * ROW_CHUNKS, LANES), lambda i: (i, 0)), lane(8), lane(8),
                   pl.BlockSpec((N_EXPERTS, LANES), lambda i: (0, 0))],
        out_shape=[jax.ShapeDtypeStruct((n * ROW_CHUNKS, LANES), F32),
                   jax.ShapeDtypeStruct((8, n), jnp.int32), jax.ShapeDtypeStruct((8, n), F32),
                   jax.ShapeDtypeStruct((N_EXPERTS, LANES), jnp.int32)],
        scratch_shapes=[pltpu.VMEM((N_EXPERTS, LANES), F32)],
        compiler_params=pltpu.CompilerParams(dimension_semantics=("arbitrary",), vmem_limit_bytes=VMEM_LIMIT),
        name="merge",
    )(h, ya_p, ya_s, og_p, og_s, gr, gate, valid, ng, wa, wg, wo, g1, b1, wrh, wrl, br, u)


def _gather_row_tiles(idx_ref, idx0, src_hbm, dst, sem, n):
    def body(r, carry):
        t = idx_ref[idx0 + r]
        pltpu.make_async_copy(src_hbm.at[pl.ds(pl.multiple_of(t * ROW_CHUNKS, ROW_CHUNKS), ROW_CHUNKS), :],
                              dst.at[pl.ds(pl.multiple_of(r * ROW_CHUNKS, ROW_CHUNKS), ROW_CHUNKS), :], sem).start()
        return carry
    lax.fori_loop(0, n, body, 0, unroll=8)


def _wait_row_tiles(src_hbm, dst, sem, n):
    pltpu.make_async_copy(src_hbm.at[pl.ds(0, n * ROW_CHUNKS), :], dst, sem).wait()


def _tiles(ref, first, n=1):
    return ref.at[pl.ds(pl.multiple_of(first * ROW_CHUNKS, ROW_CHUNKS), n * ROW_CHUNKS), :]


def _dispatch_kernel(dest_ref, start_ref, cnt_ref, nt_ref, h_hbm, xs_hbm, buf, zbuf, sem_in, sem_out, sem_z,
                     *, ranges, n_rows, max_tiles):
    CH = DISPATCH_CHUNK
    T = FFN_TILE
    c = pl.program_id(0)
    n_chunks = pl.num_programs(0)

    def load(row0, slot):
        return pltpu.make_async_copy(_tiles(h_hbm, row0, CH), buf.at[slot], sem_in.at[slot])

    def scatter(row0, slot):
        def body(r, carry):
            for kk in range(2):
                d = dest_ref[kk * n_rows + row0 + r]
                pltpu.make_async_copy(_tiles(buf.at[slot], r), _tiles(xs_hbm, d), sem_out.at[slot]).start()
            return carry
        lax.fori_loop(0, CH, body, 0, unroll=8)

    def drain(slot):
        for _ in range(2):
            pltpu.make_async_copy(buf.at[slot], _tiles(xs_hbm, 0, CH), sem_out.at[slot]).wait()

    groups, per_group, first, stride, extra, extra_first = ranges

    def row0(j):
        in_group = first + (j // per_group) * stride + (j % per_group) * CH
        return jnp.where(j < groups * per_group, in_group, extra_first + (j - groups * per_group) * CH)

    @pl.when(c == 0)
    def _():
        for j in range(2):
            load(row0(j), j).start()

    slot = c % 3
    load(row0(c), slot).wait()
    scatter(row0(c), slot)

    @pl.when(c > 0)
    def _():
        drain((c + 2) % 3)

    @pl.when(c + 2 < n_chunks)
    def _():
        load(row0(c + 2), (c + 2) % 3).start()

    @pl.when(c == n_chunks - 1)
    def _():
        drain(slot)
        _zero_unowned_slots(start_ref, cnt_ref, nt_ref, xs_hbm, zbuf, sem_z, max_tiles)


def _zero_unowned_slots(start_ref, cnt_ref, nt_ref, xs_hbm, zbuf, sem_z, max_tiles):
    T = FFN_TILE
    zbuf[...] = jnp.zeros_like(zbuf)

    def tail_copies(e, wait):
        cnt = cnt_ref[e]
        n = (T - (cnt & (T - 1))) & (T - 1)
        first = start_ref[e] + cnt
        for bit in reversed(range(T.bit_length() - 1)):
            size = 1 << bit

            @pl.when((n & size) != 0)
            def _():
                cp = pltpu.make_async_copy(_tiles(zbuf, 0, size),
                                           _tiles(xs_hbm, first + ((n >> (bit + 1)) << (bit + 1)), size), sem_z)
                cp.wait() if wait else cp.start()

    def unused_tile(t, wait):
        cp = pltpu.make_async_copy(zbuf, _tiles(xs_hbm, t * T, T), sem_z)
        cp.wait() if wait else cp.start()

    for wait in (False, True):
        def per_expert(e, carry, wait=wait):
            tail_copies(e, wait)
            return carry

        def per_tile(t, carry, wait=wait):
            unused_tile(t, wait)
            return carry
        lax.fori_loop(0, N_EXPERTS, per_expert, 0)
        lax.fori_loop(nt_ref[0], max_tiles, per_tile, 0)


def _dispatch(dest, start, counts, n_tiles, h1t, ranges, max_tiles):
    T = FFN_TILE
    n_rows = h1t.shape[0] // ROW_CHUNKS
    return pl.pallas_call(
        functools.partial(_dispatch_kernel, ranges=ranges, n_rows=n_rows, max_tiles=max_tiles),
        grid_spec=pltpu.PrefetchScalarGridSpec(
            num_scalar_prefetch=4,
            grid=(ranges[0] * ranges[1] + ranges[4],),
            in_specs=[pl.BlockSpec(memory_space=pl.ANY)],
            out_specs=pl.BlockSpec(memory_space=pl.ANY),
            scratch_shapes=[pltpu.VMEM((3, DISPATCH_CHUNK * ROW_CHUNKS, LANES), F32),
                            pltpu.VMEM((T * ROW_CHUNKS, LANES), F32),
                            pltpu.SemaphoreType.DMA((3,)), pltpu.SemaphoreType.DMA((3,)), pltpu.SemaphoreType.DMA]),
        out_shape=jax.ShapeDtypeStruct((max_tiles * T * ROW_CHUNKS, LANES), F32),
        compiler_params=pltpu.CompilerParams(dimension_semantics=("arbitrary",)),
        name="dispatch",
    )(dest, start, counts, n_tiles, h1t)


def _ffn_kernel(te_ref, nxt_ref, nt_ref, x_ref, wg_hbm, wu_hbm, wd_hbm, out_ref, stage_g, stage_u, stage_d, sem,
                wgb, wub, wdb):
    i = pl.program_id(0)
    nt = nt_ref[0]
    T = FFN_TILE

    def stage(e):
        return [pltpu.make_async_copy(src.at[e], dst, sem.at[n])
                for n, (src, dst) in enumerate(((wg_hbm, stage_g), (wu_hbm, stage_u), (wd_hbm, stage_d)))]

    @pl.when(i == 0)
    def _():
        for cp in stage(te_ref[0]):
            cp.start()

    @pl.when(i < nt)
    def _():
        @pl.when((i == 0) | (te_ref[i] != te_ref[jnp.maximum(i - 1, 0)]))
        def _():
            for cp in stage(te_ref[i]):
                cp.wait()
            wgb[...] = stage_g[...].astype(BF16)
            wub[...] = stage_u[...].astype(BF16)
            wdb[...] = stage_d[...].astype(BF16)

            @pl.when(nxt_ref[i] >= 0)
            def _():
                for cp in stage(nxt_ref[i]):
                    cp.start()

        x = _load_row_tiles(x_ref, T).astype(BF16)
        g = _dot(x, wgb[...])
        u = _dot(x, wub[...])
        _store_row_tiles(out_ref, _dot((g * _sigmoid(g) * u).astype(BF16), wdb[...]))

    @pl.when(i >= nt)
    def _():
        out_ref[...] = jnp.zeros_like(out_ref)


def _ffn(tile_expert, next_expert, n_tiles, xs, w_g, w_u, w_d):
    T = FFN_TILE
    max_tiles = tile_expert.shape[0]
    hbm = pl.BlockSpec(memory_space=pl.ANY)
    tile = lambda imap: pl.BlockSpec((T * ROW_CHUNKS, LANES), imap)
    return pl.pallas_call(
        _ffn_kernel,
        grid_spec=pltpu.PrefetchScalarGridSpec(
            num_scalar_prefetch=3,
            grid=(max_tiles,),
            in_specs=[tile(lambda i, te, nxt, nt: (jnp.minimum(i, nt[0] - 1), 0)), hbm, hbm, hbm],
            out_specs=tile(lambda i, te, nxt, nt: (i, 0)),
            scratch_shapes=[pltpu.VMEM(w_g.shape[1:], F32), pltpu.VMEM(w_u.shape[1:], F32),
                            pltpu.VMEM(w_d.shape[1:], F32), pltpu.SemaphoreType.DMA((3,)),
                            pltpu.VMEM(w_g.shape[1:], BF16), pltpu.VMEM(w_u.shape[1:], BF16),
                            pltpu.VMEM(w_d.shape[1:], BF16)]),
        out_shape=jax.ShapeDtypeStruct(xs.shape, F32),
        compiler_params=pltpu.CompilerParams(dimension_semantics=("arbitrary",)),
        name="ffn",
    )(tile_expert, next_expert, n_tiles, xs, w_g, w_u, w_d)


def _combine_kernel(dest_ref, h_ref, w_ref, g_ref, b_ref, ys_hbm, y_ref, buf, sem, *, blk, n_rows):
    i = pl.program_id(0)
    n = pl.num_programs(0)
    T = CMB_TILE

    def gather(step, slot):
        for kk in range(2):
            _gather_row_tiles(dest_ref, kk * n_rows + blk(step) * T, ys_hbm, buf.at[slot, kk], sem.at[slot], T)

    @pl.when(i == 0)
    def _():
        gather(0, 0)

    @pl.when(i + 1 < n)
    def _():
        gather(i + 1, (i + 1) % 2)

    slot = i % 2
    for kk in range(2):
        _wait_row_tiles(ys_hbm, buf.at[slot, kk], sem.at[slot], T)
    w = w_ref[...]
    ff = w[:, 0:1] * _load_row_tiles(buf.at[slot, 0], T) + w[:, 1:2] * _load_row_tiles(buf.at[slot, 1], T)
    y_ref[...] = _ln(DN_ALPHA * _load_row_tiles(h_ref, T) + ff, g_ref[...], b_ref[...])


def _combine(dest, h1t, wts_t, ys, g2, b2, n_out, first_block, blocks_per_batch, skip_blocks):
    T = CMB_TILE
    n_rows = h1t.shape[0] // ROW_CHUNKS
    if skip_blocks:
        blk = lambda i: first_block + i + (i // blocks_per_batch + 1) * skip_blocks
    else:
        blk = lambda i: first_block + i
    full = lambda a: pl.BlockSpec(a.shape, lambda i, d: (0,) * a.ndim)
    return pl.pallas_call(
        functools.partial(_combine_kernel, blk=blk, n_rows=n_rows),
        grid_spec=pltpu.PrefetchScalarGridSpec(
            num_scalar_prefetch=1,
            grid=(n_out // T,),
            in_specs=[pl.BlockSpec((T * ROW_CHUNKS, LANES), lambda i, d: (blk(i), 0)),
                      pl.BlockSpec((T, 2), lambda i, d: (blk(i), 0)),
                      full(g2), full(b2), pl.BlockSpec(memory_space=pl.ANY)],
            out_specs=pl.BlockSpec((T, D_MODEL), lambda i, d: (i, 0)),
            scratch_shapes=[pltpu.VMEM((2, 2, T * ROW_CHUNKS, LANES), F32), pltpu.SemaphoreType.DMA((2,))]),
        out_shape=jax.ShapeDtypeStruct((n_out, D_MODEL), F32),
        compiler_params=pltpu.CompilerParams(dimension_semantics=("arbitrary",)),
        name="combine",
    )(dest, h1t, wts_t, g2, b2, ys)


def _dispatch_plan(routing, counts, max_tiles):
    T = FFN_TILE
    tiles_e = (counts + T - 1) // T
    tile_end = jnp.cumsum(tiles_e)
    n_tiles = tile_end[-1]
    start = (tile_end - tiles_e) * T
    ids, rank = routing[0:2], routing[2:4]
    onehot = (ids[..., None] == jnp.arange(N_EXPERTS, dtype=jnp.int32)).astype(jnp.int32)
    dest = (jnp.sum(onehot * start, axis=-1) + jnp.where(ids >= 0, rank, 0)).reshape(-1).astype(jnp.int32)
    experts = jnp.arange(N_EXPERTS, dtype=jnp.int32)
    tiles = jnp.arange(max_tiles, dtype=jnp.int32)
    te = jnp.sum((tiles[:, None] >= tile_end[None, :]).astype(jnp.int32), axis=1)
    te_last = jnp.max(jnp.where(counts > 0, experts, 0))
    te = jnp.where(tiles < n_tiles, te, te_last).astype(jnp.int32)
    later = jnp.where((counts > 0)[None, :] & (experts[None, :] > te[:, None]), experts[None, :], N_EXPERTS)
    nxt = jnp.min(later, axis=1)
    nxt = jnp.where(nxt < N_EXPERTS, nxt, -1).astype(jnp.int32)
    return te, nxt, n_tiles.reshape(1).astype(jnp.int32), start.astype(jnp.int32), dest


def kernel(x_prompt, x_sample, state_swa_k, state_swa_v, state_gla, meta_tokens, ln_emb_g, ln_emb_b, w_in, b_gate, attn_sink, w_alpha2, b_alpha, gla_norm_g, w_attn_br, w_gla_br, w_out, ln1_g, ln1_b, w_router_group, b_router_group, w_router_expert, b_router_expert, w_exp_gate, w_exp_up, w_exp_down, ln2_g, ln2_b):
    B, seq, _ = x_prompt.shape
    n_seq, t_s, _ = x_sample.shape
    depth = w_in.shape[0]
    assert depth == 1 and seq % ATT_BLOCK == 0 and t_s == 8 and SKIP_ROWS == ATT_BLOCK == WINDOW
    lp = SKIP_ROWS + seq
    NP, NS = B * lp, n_seq * t_s
    NR = NP + NS
    assert NP % ROW_TILE == 0 and NS % ROW_TILE == 0 and n_seq % SAMPLE_SEQS == 0
    assert seq % DISPATCH_CHUNK == 0 and NS % DISPATCH_CHUNK == 0
    l = 0
    row2 = lambda a: a.reshape(1, -1)

    head = jnp.concatenate([jnp.zeros((FRONT_PAD, D_MODEL), F32), meta_tokens], axis=0)
    pos = np.arange(NR)
    moe_valid = jnp.asarray(~((pos < NP) & (pos % lp < SKIP_ROWS)), F32).reshape(1, NR)

    wi = w_in[l]
    c_ga = sum((Q_W, KV_W, KV_W, GLA_DK, GLA_DK, GLA_DV))
    assert c_ga == W_IN_SPLIT
    w_pieces = (wi[:, :c_ga].astype(BF16), wi[:, c_ga + GLA_RANK:].astype(BF16),
                jnp.pad(wi[:, c_ga:c_ga + GLA_RANK], ((0, 0), (0, 128 - GLA_RANK))).astype(BF16))
    wa2_bf = jnp.concatenate([w_alpha2[l], jnp.zeros((128 - GLA_RANK, GLA_DK), F32)], axis=0).astype(BF16)

    h, q, k, v, gq, gk, gv, la, gr, gate = _inproj(x_prompt, x_sample.reshape(NS, D_MODEL), head, row2(ln_emb_g),
                                                   row2(ln_emb_b), w_pieces, wa2_bf, row2(b_alpha[l]),
                                                   row2(b_gate[l]))

    sink = attn_sink[l]
    ya_p, k_p, v_p = _swa_prompt(sink, q, k, v, B, lp)
    buf_k = state_swa_k[l].reshape(n_seq, WINDOW, KV_W)
    buf_v = state_swa_v[l].reshape(n_seq, WINDOW, KV_W)
    ya_s, nk_s, nv_s = _swa_sample(sink, q, k, v, buf_k, buf_v, NP, t_s)

    og_p, s_p = _gla_prompt(gq, gk, gv, la, B, lp)
    og_s, s_s = _gla_sample(gq, gk, gv, la, state_gla[l], NP, t_s)

    wr = jnp.concatenate([w_router_expert[l], w_router_group[l],
                          jnp.zeros((D_MODEL, LANES - N_EXPERTS - N_GROUPS), F32)], axis=1)
    br = jnp.concatenate([b_router_expert[l], b_router_group[l],
                          jnp.zeros((LANES - N_EXPERTS - N_GROUPS,), F32)]).reshape(1, LANES)
    wr_hi = wr.astype(BF16)
    h1t, routing, wts, counts = _merge(h, ya_p, ya_s, og_p, og_s, gr, gate, moe_valid,
                                       row2(gla_norm_g[l]),
                                       w_attn_br[l].astype(BF16), w_gla_br[l].astype(BF16), w_out[l].astype(BF16),
                                       row2(ln1_g[l]), row2(ln1_b[l]), wr_hi, (wr - wr_hi.astype(F32)).astype(BF16), br)

    n_tok = B * seq + NS
    max_tiles = (2 * n_tok) // FFN_TILE + N_EXPERTS
    counts = counts[:, 0]
    te, nxt, n_tiles, start, dest = _dispatch_plan(routing[0:4], counts, max_tiles)
    routed = (B, seq // DISPATCH_CHUNK, SKIP_ROWS, lp, NS // DISPATCH_CHUNK, NP)
    xs = _dispatch(dest, start, counts, n_tiles, h1t, routed, max_tiles)
    ys = _ffn(te, nxt, n_tiles, xs, w_exp_gate[l], w_exp_up[l], w_exp_down[l])

    wts_t = wts[0:2].T
    g2, b2 = row2(ln2_g[l]), row2(ln2_b[l])
    skip_blocks = SKIP_ROWS // CMB_TILE
    y_p = _combine(dest, h1t, wts_t, ys, g2, b2, B * seq, 0, seq // CMB_TILE, skip_blocks)
    y_s = _combine(dest, h1t, wts_t, ys, g2, b2, NS, NP // CMB_TILE, 1, 0)

    kv_shape = (1, B, WINDOW, N_KV, HEAD_DIM)
    return (y_p.reshape(B, seq, D_MODEL), y_s.reshape(n_seq, t_s, D_MODEL),
            k_p.reshape(kv_shape), v_p.reshape(kv_shape), s_p[None],
            nk_s.reshape(1, n_seq, WINDOW, N_KV, HEAD_DIM), nv_s.reshape(1, n_seq, WINDOW, N_KV, HEAD_DIM),
            s_s[None])
```

```python
import functools

import numpy as np
import jax
import jax.numpy as jnp
from jax import lax
from jax.experimental import pallas as pl
from jax.experimental.pallas import tpu as pltpu

F32 = jnp.float32
BF16 = jnp.bfloat16

D_MODEL = 1024
N_META = 16
HEAD_DIM = 64
N_HEADS = 8
N_KV = 2
Q_PER_KV = 4
WINDOW = 128
ATT_BLOCK = 128
GLA_HEADS = 4
GLA_HK = 128
GLA_HV = 256
GLA_DK = GLA_HEADS * GLA_HK
GLA_DV = GLA_HEADS * GLA_HV
GLA_RANK = 16
GLA_TAU = 16.0
GLA_CHUNK = 64
N_GROUPS = 4
EXP_PER_GROUP = 8
N_EXPERTS = 32
D_EXPERT = 256
DN_ALPHA = 2.0 ** 0.25
EPS = 1e-5
NEG = -1e30

FRONT_PAD = (-N_META) % ATT_BLOCK
SKIP_ROWS = FRONT_PAD + N_META

Q_W, KV_W = N_HEADS * HEAD_DIM, N_KV * HEAD_DIM
SEG = {}
_o = 0
for _n, _w in (("q", Q_W), ("k", KV_W), ("v", KV_W), ("gq", GLA_DK), ("gk", GLA_DK), ("gv", GLA_DV)):
    SEG[_n] = (0, _o, _o + _w)
    _o += _w
W_IN_SPLIT = _o
SEG["gr"] = (1, 0, GLA_DV)
SEG["gate"] = (1, GLA_DV, GLA_DV + 2 * D_MODEL)
SEG["ga"] = (2, 0, 128)

ROW_TILE = 512
FFN_TILE = 256
CMB_TILE = 128
DISPATCH_CHUNK = 256
GLA_INTRA_CHUNKS = 6
SWA_BLOCKS = 1
INPROJ_TILE = 512
SAMPLE_SEQS = 8
LANES = 128
ROW_CHUNKS = D_MODEL // LANES
SUBLANES = 8
VMEM_LIMIT = 56 * 1024 * 1024


def _ln(x, g, b):
    mu = jnp.mean(x, -1, keepdims=True)
    xc = x - mu
    var = jnp.mean(xc * xc, -1, keepdims=True)
    return xc * lax.rsqrt(var + EPS) * g + b


def _sigmoid(x):
    return 0.5 * jnp.tanh(0.5 * x) + 0.5


def _dot(a, b):
    return jnp.dot(a, b, preferred_element_type=F32)


def _dot_nt(a, b):
    return lax.dot_general(a, b, (((1,), (1,)), ((), ())), preferred_element_type=F32)


def _inproj_kernel(*refs, blocks, batch_blocks, prompt_blocks):
    xp_refs, xs_refs = refs[:blocks], refs[blocks:2 * blocks]
    (head_ref, g_ref, b_ref, w0_ref, w1_ref, w2_ref, wa2_ref, ba_ref, bg_ref,
     h_ref, q_ref, k_ref, v_ref, gq_ref, gk_ref, gv_ref, la_ref, gr_ref, gate_ref) = refs[2 * blocks:]
    w_refs = (w0_ref, w1_ref, w2_ref)
    row = lax.broadcasted_iota(jnp.int32, (ATT_BLOCK, 1), 0)
    xs, keeps = [], []
    for s in range(blocks):
        p = pl.program_id(0) * blocks + s
        is_sample = p >= prompt_blocks
        is_head = jnp.logical_and(jnp.logical_not(is_sample), p % batch_blocks == 0)
        xs.append(jnp.where(is_sample, xs_refs[s][...], jnp.where(is_head, head_ref[...], xp_refs[s][...])))
        keeps.append(jnp.where(jnp.logical_and(is_head, row < FRONT_PAD), 0.0, 1.0))
    h = _ln(jnp.concatenate(xs, 0), g_ref[...], b_ref[...])
    h_ref[...] = h
    hb = h.astype(BF16)
    keep = jnp.concatenate(keeps, 0)

    def seg(name):
        piece, a, b = SEG[name]
        return _dot(hb, w_refs[piece][:, a:b])

    qkv = _dot(hb, w0_ref[:, SEG["q"][1]:SEG["v"][2]])
    q_ref[...] = qkv[:, :Q_W]
    k_ref[...] = qkv[:, Q_W:Q_W + KV_W]
    v_ref[...] = qkv[:, Q_W + KV_W:]
    gq_ref[...] = seg("gq") * (GLA_HK ** -0.5)
    gk_ref[...] = seg("gk") * keep
    gv_ref[...] = seg("gv") * keep
    gr = seg("gr")
    gr_ref[...] = (gr * _sigmoid(gr)).astype(BF16)
    gate_ref[...] = _sigmoid(seg("gate") + bg_ref[...]).astype(BF16)
    z = _dot(seg("ga").astype(BF16), wa2_ref[...]) + ba_ref[...]
    la = (jnp.minimum(z, 0.0) - jnp.log(1.0 + jnp.exp(-jnp.abs(z)))) * (1.0 / GLA_TAU)
    la_ref[...] = la * keep


def _inproj(x_prompt, x_sample, head, ln_g, ln_b, w_pieces, wa2_bf, b_alpha, b_gate):
    B, seq, _ = x_prompt.shape
    blk = ATT_BLOCK
    seq_blocks = seq // blk
    batch_blocks = seq_blocks + 1
    prompt_blocks = B * batch_blocks
    sample_blocks = x_sample.shape[0] // blk
    n = (prompt_blocks + sample_blocks) * blk
    tm = INPROJ_TILE
    blocks = tm // blk
    xp = x_prompt.reshape(B * seq, D_MODEL)

    def prompt_block(s):
        def imap(i):
            p = jnp.minimum(i * blocks + s, prompt_blocks - 1)
            return (p // batch_blocks * seq_blocks + jnp.maximum(p % batch_blocks - 1, 0), 0)
        return pl.BlockSpec((blk, D_MODEL), imap)

    def sample_block(s):
        return pl.BlockSpec((blk, D_MODEL), lambda i: (jnp.clip(i * blocks + s - prompt_blocks, 0, sample_blocks - 1), 0))

    widths = [D_MODEL, Q_W, KV_W, KV_W, GLA_DK, GLA_DK, GLA_DV, GLA_DK, GLA_DV, 2 * D_MODEL]
    row = lambda w: pl.BlockSpec((tm, w), lambda i: (i, 0))
    const = lambda a: pl.BlockSpec(a.shape, lambda i: (0,) * a.ndim, pipeline_mode=pl.Buffered(1))
    return pl.pallas_call(
        functools.partial(_inproj_kernel, blocks=blocks, batch_blocks=batch_blocks, prompt_blocks=prompt_blocks),
        grid=(n // tm,),
        in_specs=[prompt_block(s) for s in range(blocks)] + [sample_block(s) for s in range(blocks)]
                 + [const(a) for a in (head, ln_g, ln_b, *w_pieces, wa2_bf, b_alpha, b_gate)],
        out_specs=[row(w) for w in widths],
        out_shape=[jax.ShapeDtypeStruct((n, w), BF16 if i >= len(widths) - 2 else F32) for i, w in enumerate(widths)],
        compiler_params=pltpu.CompilerParams(dimension_semantics=("parallel",), vmem_limit_bytes=VMEM_LIMIT),
        name="inproj",
    )(*([xp] * blocks), *([x_sample] * blocks), head, ln_g, ln_b, *w_pieces, wa2_bf, b_alpha, b_gate)


def _softmax_pv(s, sink, vv):
    m = jnp.maximum(jnp.max(s, -1, keepdims=True), sink)
    p = jnp.exp(s - m)
    l = jnp.sum(p, -1, keepdims=True) + jnp.exp(sink - m)
    return _dot(p.astype(BF16), vv) / l


def _swa_prompt_kernel(sink_ref, bias_ref, q_ref, kp_ref, kc_ref, vp_ref, vc_ref, o_ref, kl_ref, vl_ref):
    nq = q_ref.shape[0] // ATT_BLOCK
    q_all = q_ref[...] * (HEAD_DIM ** -0.5)
    kb = jnp.concatenate([kp_ref[...], kc_ref[...]], 0)
    vb = jnp.concatenate([vp_ref[...], vc_ref[...]], 0)
    low = lax.broadcasted_iota(jnp.int32, (1, 2 * HEAD_DIM), 1) < HEAD_DIM
    k_low = jnp.where(low, kb, 0.0)
    k_high = jnp.where(low, 0.0, kb)
    keys = {(0, 0): k_low.astype(BF16), (0, 1): pltpu.roll(k_low, HEAD_DIM, 1).astype(BF16),
            (1, 0): pltpu.roll(k_high, HEAD_DIM, 1).astype(BF16), (1, 1): k_high.astype(BF16)}
    ones_col = (lax.broadcasted_iota(jnp.int32, (1, 2 * HEAD_DIM), 1) == HEAD_DIM).astype(F32)
    values = [jnp.where(low, vb, ones_col).astype(BF16),
              jnp.where(low, pltpu.roll(vb, HEAD_DIM, 1), ones_col).astype(BF16)]
    for blk in range(nq):
        rows = slice(blk * ATT_BLOCK, (blk + 1) * ATT_BLOCK)
        band = slice(blk * ATT_BLOCK, (blk + 2) * ATT_BLOCK)
        variant = jnp.minimum(pl.program_id(1) * nq + blk, 2)
        for pair in range(N_HEADS // 2):
            qp = q_all[rows, pair * 2 * HEAD_DIM:(pair + 1) * 2 * HEAD_DIM]
            outs = []
            for half in range(2):
                h = 2 * pair + half
                kv = h // Q_PER_KV
                qm = jnp.where(low if half == 0 else jnp.logical_not(low), qp, 0.0).astype(BF16)
                s = _dot_nt(qm, keys[(kv, half)][band]) + bias_ref[variant, h]
                m = jnp.maximum(jnp.max(s, -1, keepdims=True), sink_ref[h])
                pv = _dot(jnp.exp(s - m).astype(BF16), values[kv][band])
                outs.append(pv / (pv[:, HEAD_DIM:HEAD_DIM + 1] + jnp.exp(sink_ref[h] - m)))
            o_ref[rows, pair * 2 * HEAD_DIM:(pair + 1) * 2 * HEAD_DIM] = jnp.where(low, outs[0],
                                                                                   pltpu.roll(outs[1], HEAD_DIM, 1))

    @pl.when(pl.program_id(1) == pl.num_programs(1) - 1)
    def _():
        kl_ref[0] = kc_ref[(nq - 1) * ATT_BLOCK:nq * ATT_BLOCK, :]
        vl_ref[0] = vc_ref[(nq - 1) * ATT_BLOCK:nq * ATT_BLOCK, :]


def _swa_bias_table():
    r = np.arange(ATT_BLOCK)[:, None]
    c = np.arange(2 * ATT_BLOCK)[None, :]
    dist = r - c + ATT_BLOCK
    slopes = 2.0 ** -(np.arange(N_HEADS) + 1.0)
    table = np.empty((3, N_HEADS, ATT_BLOCK, 2 * ATT_BLOCK), np.float32)
    for j in range(3):
        seen = (dist >= 0) & (dist < WINDOW) & ((j - 1) * ATT_BLOCK + c - FRONT_PAD >= 0)
        table[j] = np.where(seen[None], -slopes[:, None, None] * dist[None], NEG)
    return table


def _swa_prompt(sink, q, k, v, batch, lp):
    nq = SWA_BLOCKS
    nb = lp // ATT_BLOCK
    assert nb >= 3 and nb % nq == 0
    steps = nb // nq
    n = batch * lp
    bias = jnp.asarray(_swa_bias_table())
    cur = lambda w: pl.BlockSpec((nq * ATT_BLOCK, w), lambda b, j: (b * steps + j, 0))
    prev = lambda w: pl.BlockSpec((ATT_BLOCK, w), lambda b, j: (b * nb + jnp.maximum(j * nq - 1, 0), 0))
    last = pl.BlockSpec((1, ATT_BLOCK, KV_W), lambda b, j: (b, 0, 0))
    return pl.pallas_call(
        _swa_prompt_kernel,
        grid=(batch, steps),
        in_specs=[pl.BlockSpec(memory_space=pltpu.SMEM), pl.BlockSpec(bias.shape, lambda b, j: (0, 0, 0, 0)),
                  cur(Q_W), prev(KV_W), cur(KV_W), prev(KV_W), cur(KV_W)],
        out_specs=[cur(Q_W), last, last],
        out_shape=[jax.ShapeDtypeStruct((n, Q_W), F32), jax.ShapeDtypeStruct((batch, ATT_BLOCK, KV_W), F32),
                   jax.ShapeDtypeStruct((batch, ATT_BLOCK, KV_W), F32)],
        compiler_params=pltpu.CompilerParams(dimension_semantics=("parallel", "arbitrary")),
        name="swa_prompt",
    )(sink, bias, q, k, k, v, v)


def _swa_sample_kernel(sink_ref, q_ref, k_ref, v_ref, bk_ref, bv_ref, o_ref, nk_ref, nv_ref, *, t_s):
    nbuf = WINDOW
    span = 2 * WINDOW
    rows = Q_PER_KV * t_s
    r = lax.broadcasted_iota(jnp.int32, (rows, span), 0)
    c = lax.broadcasted_iota(jnp.int32, (rows, span), 1)
    t = r % t_s
    dist = t + nbuf - c
    mask = (dist >= 0) & (dist < WINDOW) & (c < nbuf + t_s)
    distf = dist.astype(F32)
    g_col = lax.broadcasted_iota(jnp.int32, (rows, 1), 0) // t_s
    fill = jnp.zeros((span - nbuf - t_s, KV_W), F32)

    def one_seq(s, carry):
        rs = pl.ds(pl.multiple_of(s * t_s, t_s), t_s)
        q = q_ref[rs, :]
        k_new = k_ref[rs, :]
        v_new = v_ref[rs, :]
        bk = bk_ref[s]
        bv = bv_ref[s]
        k_all = jnp.concatenate([bk, k_new, fill], 0)
        v_all = jnp.concatenate([bv, v_new, fill], 0)
        for kv in range(N_KV):
            qg = jnp.concatenate(
                [q[:, (kv * Q_PER_KV + g) * HEAD_DIM:(kv * Q_PER_KV + g + 1) * HEAD_DIM] for g in range(Q_PER_KV)], 0)
            kk = k_all[:, kv * HEAD_DIM:(kv + 1) * HEAD_DIM].astype(BF16)
            vv = v_all[:, kv * HEAD_DIM:(kv + 1) * HEAD_DIM].astype(BF16)
            slope = jnp.zeros((rows, 1), F32)
            sink = jnp.zeros((rows, 1), F32)
            for g in range(Q_PER_KV):
                h = kv * Q_PER_KV + g
                slope = jnp.where(g_col == g, 2.0 ** -(h + 1), slope)
                sink = jnp.where(g_col == g, sink_ref[h], sink)
            sc = _dot_nt(qg.astype(BF16), kk) * (HEAD_DIM ** -0.5) - slope * distf
            sc = jnp.where(mask, sc, NEG)
            o = _softmax_pv(sc, sink, vv)
            for g in range(Q_PER_KV):
                h = kv * Q_PER_KV + g
                o_ref[rs, h * HEAD_DIM:(h + 1) * HEAD_DIM] = o[g * t_s:(g + 1) * t_s]
        nk_ref[s, 0:nbuf - t_s, :] = bk[t_s:, :]
        nk_ref[s, nbuf - t_s:nbuf, :] = k_new
        nv_ref[s, 0:nbuf - t_s, :] = bv[t_s:, :]
        nv_ref[s, nbuf - t_s:nbuf, :] = v_new
        return carry

    lax.fori_loop(0, SAMPLE_SEQS, one_seq, 0, unroll=True)


def _swa_sample(sink, q, k, v, buf_k, buf_v, row0, t_s):
    n_seq = buf_k.shape[0]
    sb = SAMPLE_SEQS
    rb = sb * t_s
    b0 = row0 // rb
    rows = lambda w: pl.BlockSpec((rb, w), lambda i: (b0 + i, 0))
    bufs = pl.BlockSpec((sb, WINDOW, KV_W), lambda i: (i, 0, 0))
    return pl.pallas_call(
        functools.partial(_swa_sample_kernel, t_s=t_s),
        grid=(n_seq // sb,),
        in_specs=[pl.BlockSpec(memory_space=pltpu.SMEM), rows(Q_W), rows(KV_W), rows(KV_W), bufs, bufs],
        out_specs=[pl.BlockSpec((rb, Q_W), lambda i: (i, 0)), bufs, bufs],
        out_shape=[jax.ShapeDtypeStruct((n_seq * t_s, Q_W), F32),
                   jax.ShapeDtypeStruct(buf_k.shape, F32), jax.ShapeDtypeStruct(buf_v.shape, F32)],
        compiler_params=pltpu.CompilerParams(dimension_semantics=("parallel",)),
        name="swa_sample",
    )(sink, q, k, v, buf_k, buf_v)


def _gla_tables(chunk):
    t = np.arange(chunk)[:, None]
    u = np.arange(chunk)[None, :]
    masks = []
    w = chunk // 2
    while w >= 1:
        masks.append((t // (2 * w) == u // (2 * w)) & ((t // w) % 2 == 1) & ((u // w) % 2 == 0))
        w //= 2
    return (u <= t).astype(np.float32), np.stack(masks, 0).astype(np.float32)


def _level_exponents(b, la, w):
    C = b.shape[0]
    row = lax.broadcasted_iota(jnp.int32, b.shape, 0)
    if w >= 4:
        pieces = [jnp.broadcast_to(b[p + w - 1:p + w], (2 * w, b.shape[1])) for p in range(0, C, 2 * w)]
        ref = pieces[0] if len(pieces) == 1 else jnp.concatenate(pieces, 0)
        return jnp.where((row & w) != 0, b - ref, ref - b)
    if w == 2:
        m = row & 3
        nxt = pltpu.roll(la, C - 1, 0)
        prv = pltpu.roll(la, 1, 0)
        return jnp.where(m == 2, la, jnp.where(m == 3, la + prv, jnp.where(m == 0, nxt, 0.0)))
    return jnp.where((row & 1) != 0, la, 0.0)


def _split3(x):
    hi = x.astype(BF16)
    r1 = x - hi.astype(F32)
    mid = r1.astype(BF16)
    lo = (r1 - mid.astype(F32)).astype(BF16)
    return hi, mid, lo


def _gla_intra_kernel(g_ref, m_ref, q_ref, k_ref, v_ref, la_ref, o_ref, qe_ref, ke_ref, vt_ref, d_ref):
    C = GLA_CHUNK
    n_lvl = m_ref.shape[0]
    G = g_ref[...]
    eye = (lax.broadcasted_iota(jnp.int32, (C, C), 0) == lax.broadcasted_iota(jnp.int32, (C, C), 1)).astype(F32)
    for j in range(q_ref.shape[0] // C):
        rs = slice(j * C, (j + 1) * C)
        la = la_ref[rs, :]
        hi, mid, lo = _split3(la)
        b = _dot(G, hi) + _dot(G, mid) + _dot(G, lo)
        b_last = b[C - 1:C]
        q_all = q_ref[rs, :]
        k_all = k_ref[rs, :]
        qe_ref[0, rs, :] = (q_all * jnp.exp(b)).astype(BF16)
        ke_ref[0, rs, :] = (k_all * jnp.exp(b_last - b)).astype(BF16)
        d_ref[0, j] = jnp.broadcast_to(jnp.exp(b_last), (SUBLANES, GLA_DK))
        q_lvl, k_lvl = [], []
        for l in range(n_lvl):
            El = jnp.exp(_level_exponents(b, la, C >> (l + 1)))
            q_lvl.append((q_all * El).astype(BF16))
            k_lvl.append((k_all * El).astype(BF16))
        for h in range(GLA_HEADS):
            ks = slice(h * GLA_HK, (h + 1) * GLA_HK)
            vs = slice(h * GLA_HV, (h + 1) * GLA_HV)
            v = v_ref[rs, vs]
            att = eye * jnp.sum(q_all[:, ks] * k_all[:, ks], -1, keepdims=True)
            for l in range(n_lvl):
                att = att + m_ref[l] * _dot_nt(q_lvl[l][:, ks], k_lvl[l][:, ks])
            o_ref[0, rs, vs] = _dot(att.astype(BF16), v.astype(BF16))
            vt_ref[0, j, vs, :] = v.T.astype(BF16)


def _gla_inter_kernel(o_ref, qe_ref, ke_ref, vt_ref, d_ref, og_ref, s_ref, st_ref):
    c = pl.program_id(0)
    batch = o_ref.shape[0]

    @pl.when(c == 0)
    def _():
        st_ref[...] = jnp.zeros_like(st_ref)

    for b in range(batch):
        for h in range(GLA_HEADS):
            ks = slice(h * GLA_HK, (h + 1) * GLA_HK)
            vs = slice(h * GLA_HV, (h + 1) * GLA_HV)
            st = st_ref[b * GLA_HEADS + h]
            og_ref[b, :, vs] = o_ref[b, :, vs] + _dot_nt(qe_ref[b, :, ks], st.astype(BF16))
            st_ref[b * GLA_HEADS + h] = st * d_ref[b, 0, 0:1, ks] + _dot(vt_ref[b, 0, vs, :], ke_ref[b, :, ks])

    @pl.when(c == pl.num_programs(0) - 1)
    def _():
        for b in range(batch):
            for h in range(GLA_HEADS):
                s_ref[b, h] = st_ref[b * GLA_HEADS + h].T


def _gla_prompt(gq, gk, gv, la, batch, lp):
    C = GLA_CHUNK
    nc = lp // C
    G, M = _gla_tables(C)
    G = jnp.asarray(G, BF16)
    M = jnp.asarray(M, F32)
    cps = GLA_INTRA_CHUNKS
    assert nc % cps == 0
    rows = lambda w: pl.BlockSpec((cps * C, w), lambda b, c: (b * (nc // cps) + c, 0))
    rows3 = lambda w: pl.BlockSpec((1, cps * C, w), lambda b, c: (b, c, 0))
    full = lambda a: pl.BlockSpec(a.shape, lambda b, c: (0,) * a.ndim)
    o_intra, qe, ke, vt, d = pl.pallas_call(
        _gla_intra_kernel,
        grid=(batch, nc // cps),
        in_specs=[full(G), full(M), rows(GLA_DK), rows(GLA_DK), rows(GLA_DV), rows(GLA_DK)],
        out_specs=[rows3(GLA_DV), rows3(GLA_DK), rows3(GLA_DK),
                   pl.BlockSpec((1, cps, GLA_DV, C), lambda b, c: (b, c, 0, 0)),
                   pl.BlockSpec((1, cps, SUBLANES, GLA_DK), lambda b, c: (b, c, 0, 0))],
        out_shape=[jax.ShapeDtypeStruct((batch, lp, GLA_DV), F32),
                   jax.ShapeDtypeStruct((batch, lp, GLA_DK), BF16), jax.ShapeDtypeStruct((batch, lp, GLA_DK), BF16),
                   jax.ShapeDtypeStruct((batch, nc, GLA_DV, C), BF16),
                   jax.ShapeDtypeStruct((batch, nc, SUBLANES, GLA_DK), F32)],
        compiler_params=pltpu.CompilerParams(dimension_semantics=("parallel", "parallel")),
        name="gla_intra",
    )(G, M, gq, gk, gv, la)
    chunk = lambda w: pl.BlockSpec((batch, C, w), lambda c: (0, c, 0))
    og, s_fin = pl.pallas_call(
        _gla_inter_kernel,
        grid=(nc,),
        in_specs=[chunk(GLA_DV), chunk(GLA_DK), chunk(GLA_DK),
                  pl.BlockSpec((batch, 1, GLA_DV, C), lambda c: (0, c, 0, 0)),
                  pl.BlockSpec((batch, 1, SUBLANES, GLA_DK), lambda c: (0, c, 0, 0))],
        out_specs=[chunk(GLA_DV), pl.BlockSpec((batch, GLA_HEADS, GLA_HK, GLA_HV), lambda c: (0, 0, 0, 0))],
        out_shape=[jax.ShapeDtypeStruct((batch, lp, GLA_DV), F32),
                   jax.ShapeDtypeStruct((batch, GLA_HEADS, GLA_HK, GLA_HV), F32)],
        scratch_shapes=[pltpu.VMEM((batch * GLA_HEADS, GLA_HV, GLA_HK), F32)],
        compiler_params=pltpu.CompilerParams(dimension_semantics=("arbitrary",)),
        name="gla_inter",
    )(o_intra, qe, ke, vt, d)
    return og.reshape(batch * lp, GLA_DV), s_fin


def _gla_sample_kernel(q_ref, k_ref, v_ref, la_ref, s0_ref, o_ref, s_ref, *, t_s):
    T = t_s
    row = lax.broadcasted_iota(jnp.int32, (T, GLA_HK), 0)
    k_fill = jnp.zeros((GLA_HK - T - SUBLANES, GLA_HK), F32)
    v_fill = jnp.zeros((GLA_HK - T, GLA_HV), F32)

    def one_seq(s, carry):
        rs = pl.ds(pl.multiple_of(s * T, T), T)
        for h in range(GLA_HEADS):
            ks = slice(h * GLA_HK, (h + 1) * GLA_HK)
            vs = slice(h * GLA_HV, (h + 1) * GLA_HV)
            q = q_ref[rs, ks]
            k = k_ref[rs, ks]
            v = v_ref[rs, vs]
            b = la_ref[rs, ks]
            sh = 1
            while sh < T:
                b = b + jnp.where(row >= sh, pltpu.roll(b, sh, 0), 0.0)
                sh *= 2
            S = s0_ref[s, h]
            o = _dot((q * jnp.exp(b)).astype(BF16), S.astype(BF16))
            for j in range(T):
                e = jnp.exp(jnp.where(row >= j, b - b[j:j + 1], NEG))
                a_col = jnp.sum(q * k[j:j + 1] * e, -1, keepdims=True)
                o = o + a_col * v[j:j + 1]
            o_ref[rs, vs] = o
            b_last = b[T - 1:T]
            ke = k * jnp.exp(b_last - b)
            kt = jnp.concatenate([ke, jnp.broadcast_to(jnp.exp(b_last), (SUBLANES, GLA_HK)), k_fill], 0).T
            v_pad = jnp.concatenate([v, v_fill], 0)
            s_ref[s, h] = S * kt[:, T:T + 1] + _dot(kt.astype(BF16), v_pad.astype(BF16))
        return carry

    lax.fori_loop(0, SAMPLE_SEQS, one_seq, 0)


def _gla_sample(gq, gk, gv, la, s0, row0, t_s):
    n_seq = s0.shape[0]
    sb = SAMPLE_SEQS
    rb = sb * t_s
    b0 = row0 // rb
    rows = lambda w: pl.BlockSpec((rb, w), lambda i: (b0 + i, 0))
    st = pl.BlockSpec((sb, GLA_HEADS, GLA_HK, GLA_HV), lambda i: (i, 0, 0, 0))
    return pl.pallas_call(
        functools.partial(_gla_sample_kernel, t_s=t_s),
        grid=(n_seq // sb,),
        in_specs=[rows(GLA_DK), rows(GLA_DK), rows(GLA_DV), rows(GLA_DK), st],
        out_specs=[pl.BlockSpec((rb, GLA_DV), lambda i: (i, 0)), st],
        out_shape=[jax.ShapeDtypeStruct((n_seq * t_s, GLA_DV), F32), jax.ShapeDtypeStruct(s0.shape, F32)],
        compiler_params=pltpu.CompilerParams(dimension_semantics=("parallel",), vmem_limit_bytes=VMEM_LIMIT),
        name="gla_sample",
    )(gq, gk, gv, la, s0)


def _route(lt, valid):
    tm = lt.shape[1]
    el = lt[0:N_EXPERTS]
    gl = lt[N_EXPERTS:N_EXPERTS + N_GROUPS]
    g_max = jnp.max(gl, 0, keepdims=True)
    g_row = lax.broadcasted_iota(jnp.int32, (N_GROUPS, tm), 0)
    g_idx = jnp.min(jnp.where(gl == g_max, g_row, N_GROUPS), 0, keepdims=True)
    p_max = 1.0 / jnp.sum(jnp.exp(gl - g_max), 0, keepdims=True)
    e_row = lax.broadcasted_iota(jnp.int32, (N_EXPERTS, tm), 0)
    m1 = jnp.where(e_row // EXP_PER_GROUP == g_idx, el, -jnp.inf)
    v1 = jnp.max(m1, 0, keepdims=True)
    i1 = jnp.min(jnp.where(m1 == v1, e_row, N_EXPERTS), 0, keepdims=True)
    m2 = jnp.where(e_row == i1, -jnp.inf, m1)
    v2 = jnp.max(m2, 0, keepdims=True)
    i2 = jnp.min(jnp.where(m2 == v2, e_row, N_EXPERTS), 0, keepdims=True)
    e2 = jnp.exp(v2 - v1)
    w1 = p_max / (1.0 + e2)
    w2 = p_max * e2 / (1.0 + e2)
    o_row = lax.broadcasted_iota(jnp.int32, (SUBLANES, tm), 0)
    ids = jnp.where(o_row == 0, i1, jnp.where(o_row == 1, i2, -1))
    return jnp.where(valid, ids, -1), jnp.where(o_row == 0, w1, jnp.where(o_row == 1, w2, 0.0))


def _store_row_tiles(ref, x):
    t, d = x.shape
    n = d // LANES
    for s in range(n):
        ref[pl.ds(s, t, stride=n), :] = x[:, s * LANES:(s + 1) * LANES]


def _load_row_tiles(ref, t, n=ROW_CHUNKS):
    return jnp.concatenate([ref[pl.ds(s, t, stride=n), :] for s in range(n)], axis=1)


def _merge_kernel(h_ref, yap_ref, yas_ref, ogp_ref, ogs_ref, gr_ref, gate_ref, valid_ref, ng_ref,
                  wa_ref, wg_ref, wo_ref, g1_ref, b1_ref, wrh_ref, wrl_ref, br_ref, u_ref,
                  h1t_ref, ids_ref, wts_ref, cnt_ref, run_ref, *, prompt_tiles):
    @pl.when(pl.program_id(0) == 0)
    def _():
        run_ref[...] = jnp.zeros_like(run_ref)

    h = h_ref[...]
    is_prompt = pl.program_id(0) < prompt_tiles
    og = jnp.where(is_prompt, ogp_ref[...], ogs_ref[...])
    ya = jnp.where(is_prompt, yap_ref[...], yas_ref[...])
    parts = []
    for hh in range(GLA_HEADS):
        o = og[:, hh * GLA_HV:(hh + 1) * GLA_HV]
        parts.append(o * lax.rsqrt(jnp.mean(o * o, -1, keepdims=True) + EPS))
    y_gla = jnp.concatenate(parts, 1) * ng_ref[...] * gr_ref[...].astype(F32)
    a = _dot(ya.astype(BF16), wa_ref[...])
    b = _dot(y_gla.astype(BF16), wg_ref[...])
    hm = gate_ref[:, :D_MODEL].astype(F32) * a + gate_ref[:, D_MODEL:].astype(F32) * b
    mix = _dot(hm.astype(BF16), wo_ref[...])
    h1 = _ln(DN_ALPHA * h + mix, g1_ref[...], b1_ref[...])
    _store_row_tiles(h1t_ref, h1)
    h_hi = h1.astype(BF16)
    h_lo = (h1 - h_hi.astype(F32)).astype(BF16)
    logits = _dot(h_hi, wrh_ref[...]) + _dot(h_lo, wrh_ref[...]) + _dot(h_hi, wrl_ref[...]) + br_ref[...]
    ids, wts_ref[...] = _route(logits.T, valid_ref[...] > 0.0)
    tm = ids.shape[1]
    e_row = lax.broadcasted_iota(jnp.int32, (N_EXPERTS, tm), 0)
    run = run_ref[:, 0:1]
    ranks = []
    for kk in range(2):
        onehot = (e_row == ids[kk:kk + 1]).astype(F32)
        before = _dot(onehot.astype(BF16), u_ref[...])
        ranks.append(jnp.sum(onehot * (run + before), 0, keepdims=True).astype(jnp.int32))
        run = run + jnp.sum(onehot, 1, keepdims=True)
    run_ref[...] = jnp.broadcast_to(run, run_ref.shape)
    o_row = lax.broadcasted_iota(jnp.int32, (SUBLANES, tm), 0)
    ids_ref[...] = jnp.where(o_row == 2, ranks[0], jnp.where(o_row == 3, ranks[1], ids))
    cnt_ref[...] = run_ref[...].astype(jnp.int32)


def _merge(h, ya_p, ya_s, og_p, og_s, gr, gate, valid, ng, wa, wg, wo, g1, b1, wrh, wrl, br):
    n = h.shape[0]
    tm = ROW_TILE
    u = jnp.asarray(np.triu(np.ones((tm, tm), np.float32), 1), BF16)
    pt = ya_p.shape[0] // tm
    st = ya_s.shape[0] // tm
    row = lambda w: pl.BlockSpec((tm, w), lambda i: (i, 0))
    row_p = lambda w: pl.BlockSpec((tm, w), lambda i: (jnp.minimum(i, pt - 1), 0))
    row_s = lambda w: pl.BlockSpec((tm, w), lambda i: (jnp.clip(i - pt, 0, st - 1), 0))
    lane = lambda r: pl.BlockSpec((r, tm), lambda i: (0, i))
    full = lambda a: pl.BlockSpec(a.shape, lambda i: (0,) * a.ndim)
    return pl.pallas_call(
        functools.partial(_merge_kernel, prompt_tiles=pt),
        grid=(n // tm,),
        in_specs=[row(D_MODEL), row_p(Q_W), row_s(Q_W), row_p(GLA_DV), row_s(GLA_DV), row(GLA_DV), row(2 * D_MODEL),
                  lane(1), full(ng), full(wa), full(wg), full(wo), full(g1), full(b1),
                  full(wrh), full(wrl), full(br), full(u)],
        out_specs=[pl.BlockSpec((tm * ROW_CHUNKS, LANES), lambda i: (i, 0)), lane(SUBLANES), lane(SUBLANES),
                   pl.BlockSpec((N_EXPERTS, LANES), lambda i: (0, 0))],
        out_shape=[jax.ShapeDtypeStruct((n * ROW_CHUNKS, LANES), F32),
                   jax.ShapeDtypeStruct((SUBLANES, n), jnp.int32), jax.ShapeDtypeStruct((SUBLANES, n), F32),
                   jax.ShapeDtypeStruct((N_EXPERTS, LANES), jnp.int32)],
        scratch_shapes=[pltpu.VMEM((N_EXPERTS, LANES), F32)],
        compiler_params=pltpu.CompilerParams(dimension_semantics=("arbitrary",), vmem_limit_bytes=VMEM_LIMIT),
        name="merge",
    )(h, ya_p, ya_s, og_p, og_s, gr, gate, valid, ng, wa, wg, wo, g1, b1, wrh, wrl, br, u)


def _gather_row_tiles(idx_ref, idx0, src_hbm, dst, sem, n):
    def body(r, carry):
        t = idx_ref[idx0 + r]
        pltpu.make_async_copy(src_hbm.at[pl.ds(pl.multiple_of(t * ROW_CHUNKS, ROW_CHUNKS), ROW_CHUNKS), :],
                              dst.at[pl.ds(pl.multiple_of(r * ROW_CHUNKS, ROW_CHUNKS), ROW_CHUNKS), :], sem).start()
        return carry
    lax.fori_loop(0, n, body, 0, unroll=8)


def _wait_row_tiles(src_hbm, dst, sem, n):
    pltpu.make_async_copy(src_hbm.at[pl.ds(0, n * ROW_CHUNKS), :], dst, sem).wait()


def _tiles(ref, first, n=1):
    return ref.at[pl.ds(pl.multiple_of(first * ROW_CHUNKS, ROW_CHUNKS), n * ROW_CHUNKS), :]


def _dispatch_kernel(dest_ref, start_ref, cnt_ref, nt_ref, h_hbm, xs_hbm, buf, zbuf, sem_in, sem_out, sem_z,
                     *, ranges, n_rows, max_tiles):
    CH = DISPATCH_CHUNK
    T = FFN_TILE
    c = pl.program_id(0)
    n_chunks = pl.num_programs(0)

    def load(row0, slot):
        return pltpu.make_async_copy(_tiles(h_hbm, row0, CH), buf.at[slot], sem_in.at[slot])

    def scatter(row0, slot):
        def body(r, carry):
            for kk in range(2):
                d = dest_ref[kk * n_rows + row0 + r]
                pltpu.make_async_copy(_tiles(buf.at[slot], r), _tiles(xs_hbm, d), sem_out.at[slot]).start()
            return carry
        lax.fori_loop(0, CH, body, 0, unroll=8)

    def drain(slot):
        for _ in range(2):
            pltpu.make_async_copy(buf.at[slot], _tiles(xs_hbm, 0, CH), sem_out.at[slot]).wait()

    groups, per_group, first, stride, extra, extra_first = ranges

    def row0(j):
        in_group = first + (j // per_group) * stride + (j % per_group) * CH
        return jnp.where(j < groups * per_group, in_group, extra_first + (j - groups * per_group) * CH)

    @pl.when(c == 0)
    def _():
        for j in range(2):
            load(row0(j), j).start()

    slot = c % 3
    load(row0(c), slot).wait()
    scatter(row0(c), slot)

    @pl.when(c > 0)
    def _():
        drain((c + 2) % 3)

    @pl.when(c + 2 < n_chunks)
    def _():
        load(row0(c + 2), (c + 2) % 3).start()

    @pl.when(c == n_chunks - 1)
    def _():
        drain(slot)
        _zero_unowned_slots(start_ref, cnt_ref, nt_ref, xs_hbm, zbuf, sem_z, max_tiles)


def _zero_unowned_slots(start_ref, cnt_ref, nt_ref, xs_hbm, zbuf, sem_z, max_tiles):
    T = FFN_TILE
    zbuf[...] = jnp.zeros_like(zbuf)

    def tail_copies(e, wait):
        cnt = cnt_ref[e]
        n = (T - (cnt & (T - 1))) & (T - 1)
        first = start_ref[e] + cnt
        for bit in reversed(range(T.bit_length() - 1)):
            size = 1 << bit

            @pl.when((n & size) != 0)
            def _():
                cp = pltpu.make_async_copy(_tiles(zbuf, 0, size),
                                           _tiles(xs_hbm, first + ((n >> (bit + 1)) << (bit + 1)), size), sem_z)
                cp.wait() if wait else cp.start()

    def unused_tile(t, wait):
        cp = pltpu.make_async_copy(zbuf, _tiles(xs_hbm, t * T, T), sem_z)
        cp.wait() if wait else cp.start()

    for wait in (False, True):
        def per_expert(e, carry, wait=wait):
            tail_copies(e, wait)
            return carry

        def per_tile(t, carry, wait=wait):
            unused_tile(t, wait)
            return carry
        lax.fori_loop(0, N_EXPERTS, per_expert, 0)
        lax.fori_loop(nt_ref[0], max_tiles, per_tile, 0)


def _dispatch(dest, start, counts, n_tiles, h1t, ranges, max_tiles):
    T = FFN_TILE
    n_rows = h1t.shape[0] // ROW_CHUNKS
    return pl.pallas_call(
        functools.partial(_dispatch_kernel, ranges=ranges, n_rows=n_rows, max_tiles=max_tiles),
        grid_spec=pltpu.PrefetchScalarGridSpec(
            num_scalar_prefetch=4,
            grid=(ranges[0] * ranges[1] + ranges[4],),
            in_specs=[pl.BlockSpec(memory_space=pl.ANY)],
            out_specs=pl.BlockSpec(memory_space=pl.ANY),
            scratch_shapes=[pltpu.VMEM((3, DISPATCH_CHUNK * ROW_CHUNKS, LANES), F32),
                            pltpu.VMEM((T * ROW_CHUNKS, LANES), F32),
                            pltpu.SemaphoreType.DMA((3,)), pltpu.SemaphoreType.DMA((3,)), pltpu.SemaphoreType.DMA]),
        out_shape=jax.ShapeDtypeStruct((max_tiles * T * ROW_CHUNKS, LANES), F32),
        compiler_params=pltpu.CompilerParams(dimension_semantics=("arbitrary",)),
        name="dispatch",
    )(dest, start, counts, n_tiles, h1t)


def _ffn_kernel(te_ref, nxt_ref, nt_ref, x_ref, wg_hbm, wu_hbm, wd_hbm, out_ref, stage_g, stage_u, stage_d, sem,
                wgb, wub, wdb):
    i = pl.program_id(0)
    nt = nt_ref[0]
    T = FFN_TILE

    def stage(e):
        return [pltpu.make_async_copy(src.at[e], dst, sem.at[n])
                for n, (src, dst) in enumerate(((wg_hbm, stage_g), (wu_hbm, stage_u), (wd_hbm, stage_d)))]

    @pl.when(i == 0)
    def _():
        for cp in stage(te_ref[0]):
            cp.start()

    @pl.when(i < nt)
    def _():
        @pl.when((i == 0) | (te_ref[i] != te_ref[jnp.maximum(i - 1, 0)]))
        def _():
            for cp in stage(te_ref[i]):
                cp.wait()
            wgb[...] = stage_g[...].astype(BF16)
            wub[...] = stage_u[...].astype(BF16)
            wdb[...] = stage_d[...].astype(BF16)

            @pl.when(nxt_ref[i] >= 0)
            def _():
                for cp in stage(nxt_ref[i]):
                    cp.start()

        x = _load_row_tiles(x_ref, T).astype(BF16)
        g = _dot(x, wgb[...])
        u = _dot(x, wub[...])
        _store_row_tiles(out_ref, _dot((g * _sigmoid(g) * u).astype(BF16), wdb[...]))

    @pl.when(i >= nt)
    def _():
        out_ref[...] = jnp.zeros_like(out_ref)


def _ffn(tile_expert, next_expert, n_tiles, xs, w_g, w_u, w_d):
    T = FFN_TILE
    max_tiles = tile_expert.shape[0]
    hbm = pl.BlockSpec(memory_space=pl.ANY)
    tile = lambda imap: pl.BlockSpec((T * ROW_CHUNKS, LANES), imap)
    return pl.pallas_call(
        _ffn_kernel,
        grid_spec=pltpu.PrefetchScalarGridSpec(
            num_scalar_prefetch=3,
            grid=(max_tiles,),
            in_specs=[tile(lambda i, te, nxt, nt: (jnp.minimum(i, nt[0] - 1), 0)), hbm, hbm, hbm],
            out_specs=tile(lambda i, te, nxt, nt: (i, 0)),
            scratch_shapes=[pltpu.VMEM(w_g.shape[1:], F32), pltpu.VMEM(w_u.shape[1:], F32),
                            pltpu.VMEM(w_d.shape[1:], F32), pltpu.SemaphoreType.DMA((3,)),
                            pltpu.VMEM(w_g.shape[1:], BF16), pltpu.VMEM(w_u.shape[1:], BF16),
                            pltpu.VMEM(w_d.shape[1:], BF16)]),
        out_shape=jax.ShapeDtypeStruct(xs.shape, F32),
        compiler_params=pltpu.CompilerParams(dimension_semantics=("arbitrary",)),
        name="ffn",
    )(tile_expert, next_expert, n_tiles, xs, w_g, w_u, w_d)


def _combine_kernel(dest_ref, h_ref, w_ref, g_ref, b_ref, ys_hbm, y_ref, buf, sem, *, blk, n_rows):
    i = pl.program_id(0)
    n = pl.num_programs(0)
    T = CMB_TILE

    def gather(step, slot):
        for kk in range(2):
            _gather_row_tiles(dest_ref, kk * n_rows + blk(step) * T, ys_hbm, buf.at[slot, kk], sem.at[slot], T)

    @pl.when(i == 0)
    def _():
        gather(0, 0)

    @pl.when(i + 1 < n)
    def _():
        gather(i + 1, (i + 1) % 2)

    slot = i % 2
    for kk in range(2):
        _wait_row_tiles(ys_hbm, buf.at[slot, kk], sem.at[slot], T)
    w = w_ref[...]
    ff = w[:, 0:1] * _load_row_tiles(buf.at[slot, 0], T) + w[:, 1:2] * _load_row_tiles(buf.at[slot, 1], T)
    y_ref[...] = _ln(DN_ALPHA * _load_row_tiles(h_ref, T) + ff, g_ref[...], b_ref[...])


def _combine(dest, h1t, wts_t, ys, g2, b2, n_out, first_block, blocks_per_batch, skip_blocks):
    T = CMB_TILE
    n_rows = h1t.shape[0] // ROW_CHUNKS
    if skip_blocks:
        blk = lambda i: first_block + i + (i // blocks_per_batch + 1) * skip_blocks
    else:
        blk = lambda i: first_block + i
    full = lambda a: pl.BlockSpec(a.shape, lambda i, d: (0,) * a.ndim)
    return pl.pallas_call(
        functools.partial(_combine_kernel, blk=blk, n_rows=n_rows),
        grid_spec=pltpu.PrefetchScalarGridSpec(
            num_scalar_prefetch=1,
            grid=(n_out // T,),
            in_specs=[pl.BlockSpec((T * ROW_CHUNKS, LANES), lambda i, d: (blk(i), 0)),
                      pl.BlockSpec((T, 2), lambda i, d: (blk(i), 0)),
                      full(g2), full(b2), pl.BlockSpec(memory_space=pl.ANY)],
            out_specs=pl.BlockSpec((T, D_MODEL), lambda i, d: (i, 0)),
            scratch_shapes=[pltpu.VMEM((2, 2, T * ROW_CHUNKS, LANES), F32), pltpu.SemaphoreType.DMA((2,))]),
        out_shape=jax.ShapeDtypeStruct((n_out, D_MODEL), F32),
        compiler_params=pltpu.CompilerParams(dimension_semantics=("arbitrary",)),
        name="combine",
    )(dest, h1t, wts_t, g2, b2, ys)


def _dispatch_plan(routing, counts, max_tiles):
    T = FFN_TILE
    tiles_e = (counts + T - 1) // T
    tile_end = jnp.cumsum(tiles_e)
    n_tiles = tile_end[-1]
    start = (tile_end - tiles_e) * T
    ids, rank = routing[0:2], routing[2:4]
    onehot = (ids[..., None] == jnp.arange(N_EXPERTS, dtype=jnp.int32)).astype(jnp.int32)
    dest = (jnp.sum(onehot * start, axis=-1) + jnp.where(ids >= 0, rank, 0)).reshape(-1).astype(jnp.int32)
    experts = jnp.arange(N_EXPERTS, dtype=jnp.int32)
    tiles = jnp.arange(max_tiles, dtype=jnp.int32)
    te = jnp.sum((tiles[:, None] >= tile_end[None, :]).astype(jnp.int32), axis=1)
    te_last = jnp.max(jnp.where(counts > 0, experts, 0))
    te = jnp.where(tiles < n_tiles, te, te_last).astype(jnp.int32)
    later = jnp.where((counts > 0)[None, :] & (experts[None, :] > te[:, None]), experts[None, :], N_EXPERTS)
    nxt = jnp.min(later, axis=1)
    nxt = jnp.where(nxt < N_EXPERTS, nxt, -1).astype(jnp.int32)
    return te, nxt, n_tiles.reshape(1).astype(jnp.int32), start.astype(jnp.int32), dest


def kernel(x_prompt, x_sample, state_swa_k, state_swa_v, state_gla, meta_tokens, ln_emb_g, ln_emb_b, w_in, b_gate, attn_sink, w_alpha2, b_alpha, gla_norm_g, w_attn_br, w_gla_br, w_out, ln1_g, ln1_b, w_router_group, b_router_group, w_router_expert, b_router_expert, w_exp_gate, w_exp_up, w_exp_down, ln2_g, ln2_b):
    B, seq, _ = x_prompt.shape
    n_seq, t_s, _ = x_sample.shape
    depth = w_in.shape[0]
    assert depth == 1 and seq % ATT_BLOCK == 0 and t_s == 8 and SKIP_ROWS == ATT_BLOCK == WINDOW
    lp = SKIP_ROWS + seq
    NP, NS = B * lp, n_seq * t_s
    NR = NP + NS
    assert NP % ROW_TILE == 0 and NS % ROW_TILE == 0 and n_seq % SAMPLE_SEQS == 0
    assert seq % DISPATCH_CHUNK == 0 and NS % DISPATCH_CHUNK == 0
    l = 0
    row2 = lambda a: a.reshape(1, -1)

    head = jnp.concatenate([jnp.zeros((FRONT_PAD, D_MODEL), F32), meta_tokens], axis=0)
    pos = np.arange(NR)
    moe_valid = jnp.asarray(~((pos < NP) & (pos % lp < SKIP_ROWS)), F32).reshape(1, NR)

    wi = w_in[l]
    c_ga = sum((Q_W, KV_W, KV_W, GLA_DK, GLA_DK, GLA_DV))
    assert c_ga == W_IN_SPLIT
    w_pieces = (wi[:, :c_ga].astype(BF16), wi[:, c_ga + GLA_RANK:].astype(BF16),
                jnp.pad(wi[:, c_ga:c_ga + GLA_RANK], ((0, 0), (0, LANES - GLA_RANK))).astype(BF16))
    wa2_bf = jnp.concatenate([w_alpha2[l], jnp.zeros((LANES - GLA_RANK, GLA_DK), F32)], axis=0).astype(BF16)

    h, q, k, v, gq, gk, gv, la, gr, gate = _inproj(x_prompt, x_sample.reshape(NS, D_MODEL), head, row2(ln_emb_g),
                                                   row2(ln_emb_b), w_pieces, wa2_bf, row2(b_alpha[l]),
                                                   row2(b_gate[l]))

    sink = attn_sink[l]
    ya_p, k_p, v_p = _swa_prompt(sink, q, k, v, B, lp)
    buf_k = state_swa_k[l].reshape(n_seq, WINDOW, KV_W)
    buf_v = state_swa_v[l].reshape(n_seq, WINDOW, KV_W)
    ya_s, nk_s, nv_s = _swa_sample(sink, q, k, v, buf_k, buf_v, NP, t_s)

    og_p, s_p = _gla_prompt(gq, gk, gv, la, B, lp)
    og_s, s_s = _gla_sample(gq, gk, gv, la, state_gla[l], NP, t_s)

    wr = jnp.concatenate([w_router_expert[l], w_router_group[l],
                          jnp.zeros((D_MODEL, LANES - N_EXPERTS - N_GROUPS), F32)], axis=1)
    br = jnp.concatenate([b_router_expert[l], b_router_group[l],
                          jnp.zeros((LANES - N_EXPERTS - N_GROUPS,), F32)]).reshape(1, LANES)
    wr_hi = wr.astype(BF16)
    h1t, routing, wts, counts = _merge(h, ya_p, ya_s, og_p, og_s, gr, gate, moe_valid,
                                       row2(gla_norm_g[l]),
                                       w_attn_br[l].astype(BF16), w_gla_br[l].astype(BF16), w_out[l].astype(BF16),
                                       row2(ln1_g[l]), row2(ln1_b[l]), wr_hi, (wr - wr_hi.astype(F32)).astype(BF16), br)

    n_tok = B * seq + NS
    max_tiles = (2 * n_tok) // FFN_TILE + N_EXPERTS
    counts = counts[:, 0]
    te, nxt, n_tiles, start, dest = _dispatch_plan(routing[0:4], counts, max_tiles)
    routed = (B, seq // DISPATCH_CHUNK, SKIP_ROWS, lp, NS // DISPATCH_CHUNK, NP)
    xs = _dispatch(dest, start, counts, n_tiles, h1t, routed, max_tiles)
    ys = _ffn(te, nxt, n_tiles, xs, w_exp_gate[l], w_exp_up[l], w_exp_down[l])

    wts_t = wts[0:2].T
    g2, b2 = row2(ln2_g[l]), row2(ln2_b[l])
    skip_blocks = SKIP_ROWS // CMB_TILE
    y_p = _combine(dest, h1t, wts_t, ys, g2, b2, B * seq, 0, seq // CMB_TILE, skip_blocks)
    y_s = _combine(dest, h1t, wts_t, ys, g2, b2, NS, NP // CMB_TILE, 1, 0)

    kv_shape = (1, B, WINDOW, N_KV, HEAD_DIM)
    return (y_p.reshape(B, seq, D_MODEL), y_s.reshape(n_seq, t_s, D_MODEL),
            k_p.reshape(kv_shape), v_p.reshape(kv_shape), s_p[None],
            nk_s.reshape(1, n_seq, WINDOW, N_KV, HEAD_DIM), nv_s.reshape(1, n_seq, WINDOW, N_KV, HEAD_DIM),
            s_s[None])
```

```python
import functools

import numpy as np
import jax
import jax.numpy as jnp
from jax import lax
from jax.experimental import pallas as pl
from jax.experimental.pallas import tpu as pltpu

F32 = jnp.float32
BF16 = jnp.bfloat16

D_MODEL = 1024
N_META = 16
HEAD_DIM = 64
N_HEADS = 8
N_KV = 2
Q_PER_KV = 4
WINDOW = 128
ATT_BLOCK = 128
GLA_HEADS = 4
GLA_HK = 128
GLA_HV = 256
GLA_DK = GLA_HEADS * GLA_HK
GLA_DV = GLA_HEADS * GLA_HV
GLA_RANK = 16
GLA_TAU = 16.0
GLA_CHUNK = 64
N_GROUPS = 4
EXP_PER_GROUP = 8
N_EXPERTS = 32
D_EXPERT = 256
DN_ALPHA = 2.0 ** 0.25
EPS = 1e-5
NEG = -1e30

FRONT_PAD = (-N_META) % ATT_BLOCK
SKIP_ROWS = FRONT_PAD + N_META

Q_W, KV_W = N_HEADS * HEAD_DIM, N_KV * HEAD_DIM
SEG = {}
_o = 0
for _n, _w in (("q", Q_W), ("k", KV_W), ("v", KV_W), ("gq", GLA_DK), ("gk", GLA_DK), ("gv", GLA_DV)):
    SEG[_n] = (0, _o, _o + _w)
    _o += _w
W_IN_SPLIT = _o
SEG["gr"] = (1, 0, GLA_DV)
SEG["gate"] = (1, GLA_DV, GLA_DV + 2 * D_MODEL)
SEG["ga"] = (2, 0, 128)

ROW_TILE = 512
FFN_TILE = 256
CMB_TILE = 128
DISPATCH_CHUNK = 256
GLA_INTRA_CHUNKS = 6
SWA_BLOCKS = 1
INPROJ_TILE = 512
SAMPLE_SEQS = 8
LANES = 128
ROW_CHUNKS = D_MODEL // LANES
SUBLANES = 8
VMEM_LIMIT = 56 * 1024 * 1024


def _ln(x, g, b):
    mu = jnp.mean(x, -1, keepdims=True)
    xc = x - mu
    var = jnp.mean(xc * xc, -1, keepdims=True)
    return xc * lax.rsqrt(var + EPS) * g + b


def _sigmoid(x):
    return 0.5 * jnp.tanh(0.5 * x) + 0.5


def _dot(a, b):
    return jnp.dot(a, b, preferred_element_type=F32)


def _dot_nt(a, b):
    return lax.dot_general(a, b, (((1,), (1,)), ((), ())), preferred_element_type=F32)


def _inproj_kernel(*refs, blocks, batch_blocks, prompt_blocks):
    xp_refs, xs_refs = refs[:blocks], refs[blocks:2 * blocks]
    (head_ref, g_ref, b_ref, w0_ref, w1_ref, w2_ref, wa2_ref, ba_ref, bg_ref,
     h_ref, q_ref, k_ref, v_ref, gq_ref, gk_ref, gv_ref, la_ref, gr_ref, gate_ref) = refs[2 * blocks:]
    w_refs = (w0_ref, w1_ref, w2_ref)
    row = lax.broadcasted_iota(jnp.int32, (ATT_BLOCK, 1), 0)
    xs, keeps = [], []
    for s in range(blocks):
        p = pl.program_id(0) * blocks + s
        is_sample = p >= prompt_blocks
        is_head = jnp.logical_and(jnp.logical_not(is_sample), p % batch_blocks == 0)
        xs.append(jnp.where(is_sample, xs_refs[s][...], jnp.where(is_head, head_ref[...], xp_refs[s][...])))
        keeps.append(jnp.where(jnp.logical_and(is_head, row < FRONT_PAD), 0.0, 1.0))
    h = _ln(jnp.concatenate(xs, 0), g_ref[...], b_ref[...])
    h_ref[...] = h
    hb = h.astype(BF16)
    keep = jnp.concatenate(keeps, 0)

    def seg(name):
        piece, a, b = SEG[name]
        return _dot(hb, w_refs[piece][:, a:b])

    qkv = _dot(hb, w0_ref[:, SEG["q"][1]:SEG["v"][2]])
    q_ref[...] = qkv[:, :Q_W]
    k_ref[...] = qkv[:, Q_W:Q_W + KV_W]
    v_ref[...] = qkv[:, Q_W + KV_W:]
    gq_ref[...] = seg("gq") * (GLA_HK ** -0.5)
    gk_ref[...] = seg("gk") * keep
    gv_ref[...] = seg("gv") * keep
    gr = seg("gr")
    gr_ref[...] = (gr * _sigmoid(gr)).astype(BF16)
    gate_ref[...] = _sigmoid(seg("gate") + bg_ref[...]).astype(BF16)
    z = _dot(seg("ga").astype(BF16), wa2_ref[...]) + ba_ref[...]
    la = (jnp.minimum(z, 0.0) - jnp.log(1.0 + jnp.exp(-jnp.abs(z)))) * (1.0 / GLA_TAU)
    la_ref[...] = la * keep


def _inproj(x_prompt, x_sample, head, ln_g, ln_b, w_pieces, wa2_bf, b_alpha, b_gate):
    B, seq, _ = x_prompt.shape
    blk = ATT_BLOCK
    seq_blocks = seq // blk
    batch_blocks = seq_blocks + 1
    prompt_blocks = B * batch_blocks
    sample_blocks = x_sample.shape[0] // blk
    n = (prompt_blocks + sample_blocks) * blk
    tm = INPROJ_TILE
    blocks = tm // blk
    xp = x_prompt.reshape(B * seq, D_MODEL)

    def prompt_block(s):
        def imap(i):
            p = jnp.minimum(i * blocks + s, prompt_blocks - 1)
            return (p // batch_blocks * seq_blocks + jnp.maximum(p % batch_blocks - 1, 0), 0)
        return pl.BlockSpec((blk, D_MODEL), imap)

    def sample_block(s):
        return pl.BlockSpec((blk, D_MODEL), lambda i: (jnp.clip(i * blocks + s - prompt_blocks, 0, sample_blocks - 1), 0))

    widths = [D_MODEL, Q_W, KV_W, KV_W, GLA_DK, GLA_DK, GLA_DV, GLA_DK, GLA_DV, 2 * D_MODEL]
    row = lambda w: pl.BlockSpec((tm, w), lambda i: (i, 0))
    const = lambda a: pl.BlockSpec(a.shape, lambda i: (0,) * a.ndim, pipeline_mode=pl.Buffered(1))
    return pl.pallas_call(
        functools.partial(_inproj_kernel, blocks=blocks, batch_blocks=batch_blocks, prompt_blocks=prompt_blocks),
        grid=(n // tm,),
        in_specs=[prompt_block(s) for s in range(blocks)] + [sample_block(s) for s in range(blocks)]
                 + [const(a) for a in (head, ln_g, ln_b, *w_pieces, wa2_bf, b_alpha, b_gate)],
        out_specs=[row(w) for w in widths],
        out_shape=[jax.ShapeDtypeStruct((n, w), BF16 if i >= len(widths) - 2 else F32) for i, w in enumerate(widths)],
        compiler_params=pltpu.CompilerParams(dimension_semantics=("parallel",), vmem_limit_bytes=VMEM_LIMIT),
        name="inproj",
    )(*([xp] * blocks), *([x_sample] * blocks), head, ln_g, ln_b, *w_pieces, wa2_bf, b_alpha, b_gate)


def _softmax_pv(s, sink, vv):
    m = jnp.maximum(jnp.max(s, -1, keepdims=True), sink)
    p = jnp.exp(s - m)
    l = jnp.sum(p, -1, keepdims=True) + jnp.exp(sink - m)
    return _dot(p.astype(BF16), vv) / l


def _swa_prompt_kernel(sink_ref, bias_ref, q_ref, kp_ref, kc_ref, vp_ref, vc_ref, o_ref, kl_ref, vl_ref):
    nq = q_ref.shape[0] // ATT_BLOCK
    q_all = q_ref[...] * (HEAD_DIM ** -0.5)
    kb = jnp.concatenate([kp_ref[...], kc_ref[...]], 0)
    vb = jnp.concatenate([vp_ref[...], vc_ref[...]], 0)
    low = lax.broadcasted_iota(jnp.int32, (1, 2 * HEAD_DIM), 1) < HEAD_DIM
    k_low = jnp.where(low, kb, 0.0)
    k_high = jnp.where(low, 0.0, kb)
    keys = {(0, 0): k_low.astype(BF16), (0, 1): pltpu.roll(k_low, HEAD_DIM, 1).astype(BF16),
            (1, 0): pltpu.roll(k_high, HEAD_DIM, 1).astype(BF16), (1, 1): k_high.astype(BF16)}
    ones_col = (lax.broadcasted_iota(jnp.int32, (1, 2 * HEAD_DIM), 1) == HEAD_DIM).astype(F32)
    values = [jnp.where(low, vb, ones_col).astype(BF16),
              jnp.where(low, pltpu.roll(vb, HEAD_DIM, 1), ones_col).astype(BF16)]
    for blk in range(nq):
        rows = slice(blk * ATT_BLOCK, (blk + 1) * ATT_BLOCK)
        band = slice(blk * ATT_BLOCK, (blk + 2) * ATT_BLOCK)
        variant = jnp.minimum(pl.program_id(1) * nq + blk, 2)
        for pair in range(N_HEADS // 2):
            qp = q_all[rows, pair * 2 * HEAD_DIM:(pair + 1) * 2 * HEAD_DIM]
            outs = []
            for half in range(2):
                h = 2 * pair + half
                kv = h // Q_PER_KV
                qm = jnp.where(low if half == 0 else jnp.logical_not(low), qp, 0.0).astype(BF16)
                s = _dot_nt(qm, keys[(kv, half)][band]) + bias_ref[variant, h]
                m = jnp.maximum(jnp.max(s, -1, keepdims=True), sink_ref[h])
                pv = _dot(jnp.exp(s - m).astype(BF16), values[kv][band])
                outs.append(pv / (pv[:, HEAD_DIM:HEAD_DIM + 1] + jnp.exp(sink_ref[h] - m)))
            o_ref[rows, pair * 2 * HEAD_DIM:(pair + 1) * 2 * HEAD_DIM] = jnp.where(low, outs[0],
                                                                                   pltpu.roll(outs[1], HEAD_DIM, 1))

    @pl.when(pl.program_id(1) == pl.num_programs(1) - 1)
    def _():
        kl_ref[0] = kc_ref[(nq - 1) * ATT_BLOCK:nq * ATT_BLOCK, :]
        vl_ref[0] = vc_ref[(nq - 1) * ATT_BLOCK:nq * ATT_BLOCK, :]


def _swa_bias_table():
    r = np.arange(ATT_BLOCK)[:, None]
    c = np.arange(2 * ATT_BLOCK)[None, :]
    dist = r - c + ATT_BLOCK
    slopes = 2.0 ** -(np.arange(N_HEADS) + 1.0)
    table = np.empty((3, N_HEADS, ATT_BLOCK, 2 * ATT_BLOCK), np.float32)
    for j in range(3):
        seen = (dist >= 0) & (dist < WINDOW) & ((j - 1) * ATT_BLOCK + c - FRONT_PAD >= 0)
        table[j] = np.where(seen[None], -slopes[:, None, None] * dist[None], NEG)
    return table


def _swa_prompt(sink, q, k, v, batch, lp):
    nq = SWA_BLOCKS
    nb = lp // ATT_BLOCK
    assert nb >= 3 and nb % nq == 0
    steps = nb // nq
    n = batch * lp
    bias = jnp.asarray(_swa_bias_table())
    cur = lambda w: pl.BlockSpec((nq * ATT_BLOCK, w), lambda b, j: (b * steps + j, 0))
    prev = lambda w: pl.BlockSpec((ATT_BLOCK, w), lambda b, j: (b * nb + jnp.maximum(j * nq - 1, 0), 0))
    last = pl.BlockSpec((1, ATT_BLOCK, KV_W), lambda b, j: (b, 0, 0))
    return pl.pallas_call(
        _swa_prompt_kernel,
        grid=(batch, steps),
        in_specs=[pl.BlockSpec(memory_space=pltpu.SMEM), pl.BlockSpec(bias.shape, lambda b, j: (0, 0, 0, 0)),
                  cur(Q_W), prev(KV_W), cur(KV_W), prev(KV_W), cur(KV_W)],
        out_specs=[cur(Q_W), last, last],
        out_shape=[jax.ShapeDtypeStruct((n, Q_W), F32), jax.ShapeDtypeStruct((batch, ATT_BLOCK, KV_W), F32),
                   jax.ShapeDtypeStruct((batch, ATT_BLOCK, KV_W), F32)],
        compiler_params=pltpu.CompilerParams(dimension_semantics=("parallel", "arbitrary")),
        name="swa_prompt",
    )(sink, bias, q, k, k, v, v)


def _swa_sample_kernel(sink_ref, q_ref, k_ref, v_ref, bk_ref, bv_ref, o_ref, nk_ref, nv_ref, *, t_s):
    nbuf = WINDOW
    span = 2 * WINDOW
    rows = Q_PER_KV * t_s
    r = lax.broadcasted_iota(jnp.int32, (rows, span), 0)
    c = lax.broadcasted_iota(jnp.int32, (rows, span), 1)
    t = r % t_s
    dist = t + nbuf - c
    mask = (dist >= 0) & (dist < WINDOW) & (c < nbuf + t_s)
    distf = dist.astype(F32)
    g_col = lax.broadcasted_iota(jnp.int32, (rows, 1), 0) // t_s
    fill = jnp.zeros((span - nbuf - t_s, KV_W), F32)

    def one_seq(s, carry):
        rs = pl.ds(pl.multiple_of(s * t_s, t_s), t_s)
        q = q_ref[rs, :]
        k_new = k_ref[rs, :]
        v_new = v_ref[rs, :]
        bk = bk_ref[s]
        bv = bv_ref[s]
        k_all = jnp.concatenate([bk, k_new, fill], 0)
        v_all = jnp.concatenate([bv, v_new, fill], 0)
        for kv in range(N_KV):
            qg = jnp.concatenate(
                [q[:, (kv * Q_PER_KV + g) * HEAD_DIM:(kv * Q_PER_KV + g + 1) * HEAD_DIM] for g in range(Q_PER_KV)], 0)
            kk = k_all[:, kv * HEAD_DIM:(kv + 1) * HEAD_DIM].astype(BF16)
            vv = v_all[:, kv * HEAD_DIM:(kv + 1) * HEAD_DIM].astype(BF16)
            slope = jnp.zeros((rows, 1), F32)
            sink = jnp.zeros((rows, 1), F32)
            for g in range(Q_PER_KV):
                h = kv * Q_PER_KV + g
                slope = jnp.where(g_col == g, 2.0 ** -(h + 1), slope)
                sink = jnp.where(g_col == g, sink_ref[h], sink)
            sc = _dot_nt(qg.astype(BF16), kk) * (HEAD_DIM ** -0.5) - slope * distf
            sc = jnp.where(mask, sc, NEG)
            o = _softmax_pv(sc, sink, vv)
            for g in range(Q_PER_KV):
                h = kv * Q_PER_KV + g
                o_ref[rs, h * HEAD_DIM:(h + 1) * HEAD_DIM] = o[g * t_s:(g + 1) * t_s]
        nk_ref[s, 0:nbuf - t_s, :] = bk[t_s:, :]
        nk_ref[s, nbuf - t_s:nbuf, :] = k_new
        nv_ref[s, 0:nbuf - t_s, :] = bv[t_s:, :]
        nv_ref[s, nbuf - t_s:nbuf, :] = v_new
        return carry

    lax.fori_loop(0, SAMPLE_SEQS, one_seq, 0, unroll=True)


def _swa_sample(sink, q, k, v, buf_k, buf_v, row0, t_s):
    n_seq = buf_k.shape[0]
    sb = SAMPLE_SEQS
    rb = sb * t_s
    b0 = row0 // rb
    rows = lambda w: pl.BlockSpec((rb, w), lambda i: (b0 + i, 0))
    bufs = pl.BlockSpec((sb, WINDOW, KV_W), lambda i: (i, 0, 0))
    return pl.pallas_call(
        functools.partial(_swa_sample_kernel, t_s=t_s),
        grid=(n_seq // sb,),
        in_specs=[pl.BlockSpec(memory_space=pltpu.SMEM), rows(Q_W), rows(KV_W), rows(KV_W), bufs, bufs],
        out_specs=[pl.BlockSpec((rb, Q_W), lambda i: (i, 0)), bufs, bufs],
        out_shape=[jax.ShapeDtypeStruct((n_seq * t_s, Q_W), F32),
                   jax.ShapeDtypeStruct(buf_k.shape, F32), jax.ShapeDtypeStruct(buf_v.shape, F32)],
        compiler_params=pltpu.CompilerParams(dimension_semantics=("parallel",)),
        name="swa_sample",
    )(sink, q, k, v, buf_k, buf_v)


def _gla_tables(chunk):
    t = np.arange(chunk)[:, None]
    u = np.arange(chunk)[None, :]
    masks = []
    w = chunk // 2
    while w >= 1:
        masks.append((t // (2 * w) == u // (2 * w)) & ((t // w) % 2 == 1) & ((u // w) % 2 == 0))
        w //= 2
    return (u <= t).astype(np.float32), np.stack(masks, 0).astype(np.float32)


def _level_exponents(b, la, w):
    C = b.shape[0]
    row = lax.broadcasted_iota(jnp.int32, b.shape, 0)
    if w >= 4:
        pieces = [jnp.broadcast_to(b[p + w - 1:p + w], (2 * w, b.shape[1])) for p in range(0, C, 2 * w)]
        ref = pieces[0] if len(pieces) == 1 else jnp.concatenate(pieces, 0)
        return jnp.where((row & w) != 0, b - ref, ref - b)
    if w == 2:
        m = row & 3
        nxt = pltpu.roll(la, C - 1, 0)
        prv = pltpu.roll(la, 1, 0)
        return jnp.where(m == 2, la, jnp.where(m == 3, la + prv, jnp.where(m == 0, nxt, 0.0)))
    return jnp.where((row & 1) != 0, la, 0.0)


def _split3(x):
    hi = x.astype(BF16)
    r1 = x - hi.astype(F32)
    mid = r1.astype(BF16)
    lo = (r1 - mid.astype(F32)).astype(BF16)
    return hi, mid, lo


def _gla_intra_kernel(g_ref, m_ref, q_ref, k_ref, v_ref, la_ref, o_ref, qe_ref, ke_ref, vt_ref, d_ref):
    C = GLA_CHUNK
    n_lvl = m_ref.shape[0]
    G = g_ref[...]
    eye = (lax.broadcasted_iota(jnp.int32, (C, C), 0) == lax.broadcasted_iota(jnp.int32, (C, C), 1)).astype(F32)
    for j in range(q_ref.shape[0] // C):
        rs = slice(j * C, (j + 1) * C)
        la = la_ref[rs, :]
        hi, mid, lo = _split3(la)
        b = _dot(G, hi) + _dot(G, mid) + _dot(G, lo)
        b_last = b[C - 1:C]
        q_all = q_ref[rs, :]
        k_all = k_ref[rs, :]
        qe_ref[0, rs, :] = (q_all * jnp.exp(b)).astype(BF16)
        ke_ref[0, rs, :] = (k_all * jnp.exp(b_last - b)).astype(BF16)
        d_ref[0, j] = jnp.broadcast_to(jnp.exp(b_last), (SUBLANES, GLA_DK))
        q_lvl, k_lvl = [], []
        for l in range(n_lvl):
            El = jnp.exp(_level_exponents(b, la, C >> (l + 1)))
            q_lvl.append((q_all * El).astype(BF16))
            k_lvl.append((k_all * El).astype(BF16))
        for h in range(GLA_HEADS):
            ks = slice(h * GLA_HK, (h + 1) * GLA_HK)
            vs = slice(h * GLA_HV, (h + 1) * GLA_HV)
            v = v_ref[rs, vs]
            att = eye * jnp.sum(q_all[:, ks] * k_all[:, ks], -1, keepdims=True)
            for l in range(n_lvl):
                att = att + m_ref[l] * _dot_nt(q_lvl[l][:, ks], k_lvl[l][:, ks])
            o_ref[0, rs, vs] = _dot(att.astype(BF16), v.astype(BF16))
            vt_ref[0, j, vs, :] = v.T.astype(BF16)


def _gla_inter_kernel(o_ref, qe_ref, ke_ref, vt_ref, d_ref, og_ref, s_ref, st_ref):
    c = pl.program_id(0)
    batch = o_ref.shape[0]

    @pl.when(c == 0)
    def _():
        st_ref[...] = jnp.zeros_like(st_ref)

    for b in range(batch):
        for h in range(GLA_HEADS):
            ks = slice(h * GLA_HK, (h + 1) * GLA_HK)
            vs = slice(h * GLA_HV, (h + 1) * GLA_HV)
            st = st_ref[b * GLA_HEADS + h]
            og_ref[b, :, vs] = o_ref[b, :, vs] + _dot_nt(qe_ref[b, :, ks], st.astype(BF16))
            st_ref[b * GLA_HEADS + h] = st * d_ref[b, 0, 0:1, ks] + _dot(vt_ref[b, 0, vs, :], ke_ref[b, :, ks])

    @pl.when(c == pl.num_programs(0) - 1)
    def _():
        for b in range(batch):
            for h in range(GLA_HEADS):
                s_ref[b, h] = st_ref[b * GLA_HEADS + h].T


def _gla_prompt(gq, gk, gv, la, batch, lp):
    C = GLA_CHUNK
    nc = lp // C
    G, M = _gla_tables(C)
    G = jnp.asarray(G, BF16)
    M = jnp.asarray(M, F32)
    cps = GLA_INTRA_CHUNKS
    assert nc % cps == 0
    rows = lambda w: pl.BlockSpec((cps * C, w), lambda b, c: (b * (nc // cps) + c, 0))
    rows3 = lambda w: pl.BlockSpec((1, cps * C, w), lambda b, c: (b, c, 0))
    full = lambda a: pl.BlockSpec(a.shape, lambda b, c: (0,) * a.ndim)
    o_intra, qe, ke, vt, d = pl.pallas_call(
        _gla_intra_kernel,
        grid=(batch, nc // cps),
        in_specs=[full(G), full(M), rows(GLA_DK), rows(GLA_DK), rows(GLA_DV), rows(GLA_DK)],
        out_specs=[rows3(GLA_DV), rows3(GLA_DK), rows3(GLA_DK),
                   pl.BlockSpec((1, cps, GLA_DV, C), lambda b, c: (b, c, 0, 0)),
                   pl.BlockSpec((1, cps, SUBLANES, GLA_DK), lambda b, c: (b, c, 0, 0))],
        out_shape=[jax.ShapeDtypeStruct((batch, lp, GLA_DV), F32),
                   jax.ShapeDtypeStruct((batch, lp, GLA_DK), BF16), jax.ShapeDtypeStruct((batch, lp, GLA_DK), BF16),
                   jax.ShapeDtypeStruct((batch, nc, GLA_DV, C), BF16),
                   jax.ShapeDtypeStruct((batch, nc, SUBLANES, GLA_DK), F32)],
        compiler_params=pltpu.CompilerParams(dimension_semantics=("parallel", "parallel")),
        name="gla_intra",
    )(G, M, gq, gk, gv, la)
    chunk = lambda w: pl.BlockSpec((batch, C, w), lambda c: (0, c, 0))
    og, s_fin = pl.pallas_call(
        _gla_inter_kernel,
        grid=(nc,),
        in_specs=[chunk(GLA_DV), chunk(GLA_DK), chunk(GLA_DK),
                  pl.BlockSpec((batch, 1, GLA_DV, C), lambda c: (0, c, 0, 0)),
                  pl.BlockSpec((batch, 1, SUBLANES, GLA_DK), lambda c: (0, c, 0, 0))],
        out_specs=[chunk(GLA_DV), pl.BlockSpec((batch, GLA_HEADS, GLA_HK, GLA_HV), lambda c: (0, 0, 0, 0))],
        out_shape=[jax.ShapeDtypeStruct((batch, lp, GLA_DV), F32),
                   jax.ShapeDtypeStruct((batch, GLA_HEADS, GLA_HK, GLA_HV), F32)],
        scratch_shapes=[pltpu.VMEM((batch * GLA_HEADS, GLA_HV, GLA_HK), F32)],
        compiler_params=pltpu.CompilerParams(dimension_semantics=("arbitrary",)),
        name="gla_inter",
    )(o_intra, qe, ke, vt, d)
    return og.reshape(batch * lp, GLA_DV), s_fin


def _gla_sample_kernel(q_ref, k_ref, v_ref, la_ref, s0_ref, o_ref, s_ref, *, t_s):
    T = t_s
    row = lax.broadcasted_iota(jnp.int32, (T, GLA_HK), 0)
    k_fill = jnp.zeros((GLA_HK - T - SUBLANES, GLA_HK), F32)
    v_fill = jnp.zeros((GLA_HK - T, GLA_HV), F32)

    def one_seq(s, carry):
        rs = pl.ds(pl.multiple_of(s * T, T), T)
        for h in range(GLA_HEADS):
            ks = slice(h * GLA_HK, (h + 1) * GLA_HK)
            vs = slice(h * GLA_HV, (h + 1) * GLA_HV)
            q = q_ref[rs, ks]
            k = k_ref[rs, ks]
            v = v_ref[rs, vs]
            b = la_ref[rs, ks]
            sh = 1
            while sh < T:
                b = b + jnp.where(row >= sh, pltpu.roll(b, sh, 0), 0.0)
                sh *= 2
            S = s0_ref[s, h]
            o = _dot((q * jnp.exp(b)).astype(BF16), S.astype(BF16))
            for j in range(T):
                e = jnp.exp(jnp.where(row >= j, b - b[j:j + 1], NEG))
                a_col = jnp.sum(q * k[j:j + 1] * e, -1, keepdims=True)
                o = o + a_col * v[j:j + 1]
            o_ref[rs, vs] = o
            b_last = b[T - 1:T]
            ke = k * jnp.exp(b_last - b)
            kt = jnp.concatenate([ke, jnp.broadcast_to(jnp.exp(b_last), (SUBLANES, GLA_HK)), k_fill], 0).T
            v_pad = jnp.concatenate([v, v_fill], 0)
            s_ref[s, h] = S * kt[:, T:T + 1] + _dot(kt.astype(BF16), v_pad.astype(BF16))
        return carry

    lax.fori_loop(0, SAMPLE_SEQS, one_seq, 0)


def _gla_sample(gq, gk, gv, la, s0, row0, t_s):
    n_seq = s0.shape[0]
    sb = SAMPLE_SEQS
    rb = sb * t_s
    b0 = row0 // rb
    rows = lambda w: pl.BlockSpec((rb, w), lambda i: (b0 + i, 0))
    st = pl.BlockSpec((sb, GLA_HEADS, GLA_HK, GLA_HV), lambda i: (i, 0, 0, 0))
    return pl.pallas_call(
        functools.partial(_gla_sample_kernel, t_s=t_s),
        grid=(n_seq // sb,),
        in_specs=[rows(GLA_DK), rows(GLA_DK), rows(GLA_DV), rows(GLA_DK), st],
        out_specs=[pl.BlockSpec((rb, GLA_DV), lambda i: (i, 0)), st],
        out_shape=[jax.ShapeDtypeStruct((n_seq * t_s, GLA_DV), F32), jax.ShapeDtypeStruct(s0.shape, F32)],
        compiler_params=pltpu.CompilerParams(dimension_semantics=("parallel",), vmem_limit_bytes=VMEM_LIMIT),
        name="gla_sample",
    )(gq, gk, gv, la, s0)


def _route(lt, valid):
    tm = lt.shape[1]
    el = lt[0:N_EXPERTS]
    gl = lt[N_EXPERTS:N_EXPERTS + N_GROUPS]
    g_max = jnp.max(gl, 0, keepdims=True)
    g_row = lax.broadcasted_iota(jnp.int32, (N_GROUPS, tm), 0)
    g_idx = jnp.min(jnp.where(gl == g_max, g_row, N_GROUPS), 0, keepdims=True)
    p_max = 1.0 / jnp.sum(jnp.exp(gl - g_max), 0, keepdims=True)
    e_row = lax.broadcasted_iota(jnp.int32, (N_EXPERTS, tm), 0)
    m1 = jnp.where(e_row // EXP_PER_GROUP == g_idx, el, -jnp.inf)
    v1 = jnp.max(m1, 0, keepdims=True)
    i1 = jnp.min(jnp.where(m1 == v1, e_row, N_EXPERTS), 0, keepdims=True)
    m2 = jnp.where(e_row == i1, -jnp.inf, m1)
    v2 = jnp.max(m2, 0, keepdims=True)
    i2 = jnp.min(jnp.where(m2 == v2, e_row, N_EXPERTS), 0, keepdims=True)
    e2 = jnp.exp(v2 - v1)
    w1 = p_max / (1.0 + e2)
    w2 = p_max * e2 / (1.0 + e2)
    o_row = lax.broadcasted_iota(jnp.int32, (SUBLANES, tm), 0)
    ids = jnp.where(o_row == 0, i1, jnp.where(o_row == 1, i2, -1))
    return jnp.where(valid, ids, -1), jnp.where(o_row == 0, w1, jnp.where(o_row == 1, w2, 0.0))


def _store_row_tiles(ref, x):
    t, d = x.shape
    n = d // LANES
    for s in range(n):
        ref[pl.ds(s, t, stride=n), :] = x[:, s * LANES:(s + 1) * LANES]


def _load_row_tiles(ref, t, n=ROW_CHUNKS):
    return jnp.concatenate([ref[pl.ds(s, t, stride=n), :] for s in range(n)], axis=1)


def _merge_kernel(h_ref, yap_ref, yas_ref, ogp_ref, ogs_ref, gr_ref, gate_ref, valid_ref, ng_ref,
                  wa_ref, wg_ref, wo_ref, g1_ref, b1_ref, wrh_ref, wrl_ref, br_ref, u_ref,
                  h1t_ref, ids_ref, wts_ref, cnt_ref, run_ref, *, prompt_tiles):
    @pl.when(pl.program_id(0) == 0)
    def _():
        run_ref[...] = jnp.zeros_like(run_ref)

    h = h_ref[...]
    is_prompt = pl.program_id(0) < prompt_tiles
    og = jnp.where(is_prompt, ogp_ref[...], ogs_ref[...])
    ya = jnp.where(is_prompt, yap_ref[...], yas_ref[...])
    parts = []
    for hh in range(GLA_HEADS):
        o = og[:, hh * GLA_HV:(hh + 1) * GLA_HV]
        parts.append(o * lax.rsqrt(jnp.mean(o * o, -1, keepdims=True) + EPS))
    y_gla = jnp.concatenate(parts, 1) * ng_ref[...] * gr_ref[...].astype(F32)
    a = _dot(ya.astype(BF16), wa_ref[...])
    b = _dot(y_gla.astype(BF16), wg_ref[...])
    hm = gate_ref[:, :D_MODEL].astype(F32) * a + gate_ref[:, D_MODEL:].astype(F32) * b
    mix = _dot(hm.astype(BF16), wo_ref[...])
    h1 = _ln(DN_ALPHA * h + mix, g1_ref[...], b1_ref[...])
    _store_row_tiles(h1t_ref, h1)
    h_hi = h1.astype(BF16)
    h_lo = (h1 - h_hi.astype(F32)).astype(BF16)
    both = _dot(h_hi, jnp.concatenate([wrh_ref[...], wrl_ref[...]], 1))
    logits = both[:, :LANES] + both[:, LANES:] + _dot(h_lo, wrh_ref[...]) + br_ref[...]
    ids, wts_ref[...] = _route(logits.T, valid_ref[...] > 0.0)
    tm = ids.shape[1]
    e_row = lax.broadcasted_iota(jnp.int32, (N_EXPERTS, tm), 0)
    run = run_ref[:, 0:1]
    onehots = [(e_row == ids[kk:kk + 1]).astype(F32) for kk in range(2)]
    befores = _dot(jnp.concatenate(onehots, 0).astype(BF16), u_ref[...])
    ranks = []
    for kk in range(2):
        onehot = onehots[kk]
        before = befores[kk * N_EXPERTS:(kk + 1) * N_EXPERTS]
        ranks.append(jnp.sum(onehot * (run + before), 0, keepdims=True).astype(jnp.int32))
        run = run + jnp.sum(onehot, 1, keepdims=True)
    run_ref[...] = jnp.broadcast_to(run, run_ref.shape)
    o_row = lax.broadcasted_iota(jnp.int32, (SUBLANES, tm), 0)
    ids_ref[...] = jnp.where(o_row == 2, ranks[0], jnp.where(o_row == 3, ranks[1], ids))
    cnt_ref[...] = run_ref[...].astype(jnp.int32)


def _merge(h, ya_p, ya_s, og_p, og_s, gr, gate, valid, ng, wa, wg, wo, g1, b1, wrh, wrl, br):
    n = h.shape[0]
    tm = ROW_TILE
    u = jnp.asarray(np.triu(np.ones((tm, tm), np.float32), 1), BF16)
    pt = ya_p.shape[0] // tm
    st = ya_s.shape[0] // tm
    row = lambda w: pl.BlockSpec((tm, w), lambda i: (i, 0))
    row_p = lambda w: pl.BlockSpec((tm, w), lambda i: (jnp.minimum(i, pt - 1), 0))
    row_s = lambda w: pl.BlockSpec((tm, w), lambda i: (jnp.clip(i - pt, 0, st - 1), 0))
    lane = lambda r: pl.BlockSpec((r, tm), lambda i: (0, i))
    full = lambda a: pl.BlockSpec(a.shape, lambda i: (0,) * a.ndim)
    return pl.pallas_call(
        functools.partial(_merge_kernel, prompt_tiles=pt),
        grid=(n // tm,),
        in_specs=[row(D_MODEL), row_p(Q_W), row_s(Q_W), row_p(GLA_DV), row_s(GLA_DV), row(GLA_DV), row(2 * D_MODEL),
                  lane(1), full(ng), full(wa), full(wg), full(wo), full(g1), full(b1),
                  full(wrh), full(wrl), full(br), full(u)],
        out_specs=[pl.BlockSpec((tm * ROW_CHUNKS, LANES), lambda i: (i, 0)), lane(SUBLANES), lane(SUBLANES),
                   pl.BlockSpec((N_EXPERTS, LANES), lambda i: (0, 0))],
        out_shape=[jax.ShapeDtypeStruct((n * ROW_CHUNKS, LANES), F32),
                   jax.ShapeDtypeStruct((SUBLANES, n), jnp.int32), jax.ShapeDtypeStruct((SUBLANES, n), F32),
                   jax.ShapeDtypeStruct((N_EXPERTS, LANES), jnp.int32)],
        scratch_shapes=[pltpu.VMEM((N_EXPERTS, LANES), F32)],
        compiler_params=pltpu.CompilerParams(dimension_semantics=("arbitrary",), vmem_limit_bytes=VMEM_LIMIT),
        name="merge",
    )(h, ya_p, ya_s, og_p, og_s, gr, gate, valid, ng, wa, wg, wo, g1, b1, wrh, wrl, br, u)


def _gather_row_tiles(idx_ref, idx0, src_hbm, dst, sem, n):
    def body(r, carry):
        t = idx_ref[idx0 + r]
        pltpu.make_async_copy(src_hbm.at[pl.ds(pl.multiple_of(t * ROW_CHUNKS, ROW_CHUNKS), ROW_CHUNKS), :],
                              dst.at[pl.ds(pl.multiple_of(r * ROW_CHUNKS, ROW_CHUNKS), ROW_CHUNKS), :], sem).start()
        return carry
    lax.fori_loop(0, n, body, 0, unroll=8)


def _wait_row_tiles(src_hbm, dst, sem, n):
    pltpu.make_async_copy(src_hbm.at[pl.ds(0, n * ROW_CHUNKS), :], dst, sem).wait()


def _tiles(ref, first, n=1):
    return ref.at[pl.ds(pl.multiple_of(first * ROW_CHUNKS, ROW_CHUNKS), n * ROW_CHUNKS), :]


def _dispatch_kernel(dest_ref, start_ref, cnt_ref, nt_ref, h_hbm, xs_hbm, buf, zbuf, sem_in, sem_out, sem_z,
                     *, ranges, n_rows, max_tiles):
    CH = DISPATCH_CHUNK
    T = FFN_TILE
    c = pl.program_id(0)
    n_chunks = pl.num_programs(0)

    def load(row0, slot):
        return pltpu.make_async_copy(_tiles(h_hbm, row0, CH), buf.at[slot], sem_in.at[slot])

    def scatter(row0, slot):
        def body(r, carry):
            for kk in range(2):
                d = dest_ref[kk * n_rows + row0 + r]
                pltpu.make_async_copy(_tiles(buf.at[slot], r), _tiles(xs_hbm, d), sem_out.at[slot]).start()
            return carry
        lax.fori_loop(0, CH, body, 0, unroll=8)

    def drain(slot):
        for _ in range(2):
            pltpu.make_async_copy(buf.at[slot], _tiles(xs_hbm, 0, CH), sem_out.at[slot]).wait()

    groups, per_group, first, stride, extra, extra_first = ranges

    def row0(j):
        in_group = first + (j // per_group) * stride + (j % per_group) * CH
        return jnp.where(j < groups * per_group, in_group, extra_first + (j - groups * per_group) * CH)

    @pl.when(c == 0)
    def _():
        for j in range(2):
            load(row0(j), j).start()

    slot = c % 3
    load(row0(c), slot).wait()
    scatter(row0(c), slot)

    @pl.when(c > 0)
    def _():
        drain((c + 2) % 3)

    @pl.when(c + 2 < n_chunks)
    def _():
        load(row0(c + 2), (c + 2) % 3).start()

    @pl.when(c == n_chunks - 1)
    def _():
        drain(slot)
        _zero_unowned_slots(start_ref, cnt_ref, nt_ref, xs_hbm, zbuf, sem_z, max_tiles)


def _zero_unowned_slots(start_ref, cnt_ref, nt_ref, xs_hbm, zbuf, sem_z, max_tiles):
    T = FFN_TILE
    zbuf[...] = jnp.zeros_like(zbuf)

    def tail_copies(e, wait):
        cnt = cnt_ref[e]
        n = (T - (cnt & (T - 1))) & (T - 1)
        first = start_ref[e] + cnt
        for bit in reversed(range(T.bit_length() - 1)):
            size = 1 << bit

            @pl.when((n & size) != 0)
            def _():
                cp = pltpu.make_async_copy(_tiles(zbuf, 0, size),
                                           _tiles(xs_hbm, first + ((n >> (bit + 1)) << (bit + 1)), size), sem_z)
                cp.wait() if wait else cp.start()

    def unused_tile(t, wait):
        cp = pltpu.make_async_copy(zbuf, _tiles(xs_hbm, t * T, T), sem_z)
        cp.wait() if wait else cp.start()

    for wait in (False, True):
        def per_expert(e, carry, wait=wait):
            tail_copies(e, wait)
            return carry

        def per_tile(t, carry, wait=wait):
            unused_tile(t, wait)
            return carry
        lax.fori_loop(0, N_EXPERTS, per_expert, 0)
        lax.fori_loop(nt_ref[0], max_tiles, per_tile, 0)


def _dispatch(dest, start, counts, n_tiles, h1t, ranges, max_tiles):
    T = FFN_TILE
    n_rows = h1t.shape[0] // ROW_CHUNKS
    return pl.pallas_call(
        functools.partial(_dispatch_kernel, ranges=ranges, n_rows=n_rows, max_tiles=max_tiles),
        grid_spec=pltpu.PrefetchScalarGridSpec(
            num_scalar_prefetch=4,
            grid=(ranges[0] * ranges[1] + ranges[4],),
            in_specs=[pl.BlockSpec(memory_space=pl.ANY)],
            out_specs=pl.BlockSpec(memory_space=pl.ANY),
            scratch_shapes=[pltpu.VMEM((3, DISPATCH_CHUNK * ROW_CHUNKS, LANES), F32),
                            pltpu.VMEM((T * ROW_CHUNKS, LANES), F32),
                            pltpu.SemaphoreType.DMA((3,)), pltpu.SemaphoreType.DMA((3,)), pltpu.SemaphoreType.DMA]),
        out_shape=jax.ShapeDtypeStruct((max_tiles * T * ROW_CHUNKS, LANES), F32),
        compiler_params=pltpu.CompilerParams(dimension_semantics=("arbitrary",)),
        name="dispatch",
    )(dest, start, counts, n_tiles, h1t)


def _ffn_kernel(te_ref, nxt_ref, nt_ref, x_ref, wg_hbm, wu_hbm, wd_hbm, out_ref, stage_g, stage_u, stage_d, sem,
                wgb, wub, wdb):
    i = pl.program_id(0)
    nt = nt_ref[0]
    T = FFN_TILE

    def stage(e):
        return [pltpu.make_async_copy(src.at[e], dst, sem.at[n])
                for n, (src, dst) in enumerate(((wg_hbm, stage_g), (wu_hbm, stage_u), (wd_hbm, stage_d)))]

    @pl.when(i == 0)
    def _():
        for cp in stage(te_ref[0]):
            cp.start()

    @pl.when(i < nt)
    def _():
        @pl.when((i == 0) | (te_ref[i] != te_ref[jnp.maximum(i - 1, 0)]))
        def _():
            for cp in stage(te_ref[i]):
                cp.wait()
            wgb[...] = stage_g[...].astype(BF16)
            wub[...] = stage_u[...].astype(BF16)
            wdb[...] = stage_d[...].astype(BF16)

            @pl.when(nxt_ref[i] >= 0)
            def _():
                for cp in stage(nxt_ref[i]):
                    cp.start()

        x = _load_row_tiles(x_ref, T).astype(BF16)
        g = _dot(x, wgb[...])
        u = _dot(x, wub[...])
        _store_row_tiles(out_ref, _dot((g * _sigmoid(g) * u).astype(BF16), wdb[...]))

    @pl.when(i >= nt)
    def _():
        out_ref[...] = jnp.zeros_like(out_ref)


def _ffn(tile_expert, next_expert, n_tiles, xs, w_g, w_u, w_d):
    T = FFN_TILE
    max_tiles = tile_expert.shape[0]
    hbm = pl.BlockSpec(memory_space=pl.ANY)
    tile = lambda imap: pl.BlockSpec((T * ROW_CHUNKS, LANES), imap)
    return pl.pallas_call(
        _ffn_kernel,
        grid_spec=pltpu.PrefetchScalarGridSpec(
            num_scalar_prefetch=3,
            grid=(max_tiles,),
            in_specs=[tile(lambda i, te, nxt, nt: (jnp.minimum(i, nt[0] - 1), 0)), hbm, hbm, hbm],
            out_specs=tile(lambda i, te, nxt, nt: (i, 0)),
            scratch_shapes=[pltpu.VMEM(w_g.shape[1:], F32), pltpu.VMEM(w_u.shape[1:], F32),
                            pltpu.VMEM(w_d.shape[1:], F32), pltpu.SemaphoreType.DMA((3,)),
                            pltpu.VMEM(w_g.shape[1:], BF16), pltpu.VMEM(w_u.shape[1:], BF16),
                            pltpu.VMEM(w_d.shape[1:], BF16)]),
        out_shape=jax.ShapeDtypeStruct(xs.shape, F32),
        compiler_params=pltpu.CompilerParams(dimension_semantics=("arbitrary",)),
        name="ffn",
    )(tile_expert, next_expert, n_tiles, xs, w_g, w_u, w_d)


def _combine_kernel(dest_ref, h_ref, w_ref, g_ref, b_ref, ys_hbm, y_ref, buf, sem, *, blk, n_rows):
    i = pl.program_id(0)
    n = pl.num_programs(0)
    T = CMB_TILE

    def gather(step, slot):
        for kk in range(2):
            _gather_row_tiles(dest_ref, kk * n_rows + blk(step) * T, ys_hbm, buf.at[slot, kk], sem.at[slot], T)

    @pl.when(i == 0)
    def _():
        gather(0, 0)

    @pl.when(i + 1 < n)
    def _():
        gather(i + 1, (i + 1) % 2)

    slot = i % 2
    for kk in range(2):
        _wait_row_tiles(ys_hbm, buf.at[slot, kk], sem.at[slot], T)
    w = w_ref[...]
    ff = w[:, 0:1] * _load_row_tiles(buf.at[slot, 0], T) + w[:, 1:2] * _load_row_tiles(buf.at[slot, 1], T)
    y_ref[...] = _ln(DN_ALPHA * _load_row_tiles(h_ref, T) + ff, g_ref[...], b_ref[...])


def _combine(dest, h1t, wts_t, ys, g2, b2, n_out, first_block, blocks_per_batch, skip_blocks):
    T = CMB_TILE
    n_rows = h1t.shape[0] // ROW_CHUNKS
    if skip_blocks:
        blk = lambda i: first_block + i + (i // blocks_per_batch + 1) * skip_blocks
    else:
        blk = lambda i: first_block + i
    full = lambda a: pl.BlockSpec(a.shape, lambda i, d: (0,) * a.ndim)
    return pl.pallas_call(
        functools.partial(_combine_kernel, blk=blk, n_rows=n_rows),
        grid_spec=pltpu.PrefetchScalarGridSpec(
            num_scalar_prefetch=1,
            grid=(n_out // T,),
            in_specs=[pl.BlockSpec((T * ROW_CHUNKS, LANES), lambda i, d: (blk(i), 0)),
                      pl.BlockSpec((T, 2), lambda i, d: (blk(i), 0)),
                      full(g2), full(b2), pl.BlockSpec(memory_space=pl.ANY)],
            out_specs=pl.BlockSpec((T, D_MODEL), lambda i, d: (i, 0)),
            scratch_shapes=[pltpu.VMEM((2, 2, T * ROW_CHUNKS, LANES), F32), pltpu.SemaphoreType.DMA((2,))]),
        out_shape=jax.ShapeDtypeStruct((n_out, D_MODEL), F32),
        compiler_params=pltpu.CompilerParams(dimension_semantics=("arbitrary",)),
        name="combine",
    )(dest, h1t, wts_t, g2, b2, ys)


def _dispatch_plan(routing, counts, max_tiles):
    T = FFN_TILE
    tiles_e = (counts + T - 1) // T
    tile_end = jnp.cumsum(tiles_e)
    n_tiles = tile_end[-1]
    start = (tile_end - tiles_e) * T
    ids, rank = routing[0:2], routing[2:4]
    onehot = (ids[..., None] == jnp.arange(N_EXPERTS, dtype=jnp.int32)).astype(jnp.int32)
    dest = (jnp.sum(onehot * start, axis=-1) + jnp.where(ids >= 0, rank, 0)).reshape(-1).astype(jnp.int32)
    experts = jnp.arange(N_EXPERTS, dtype=jnp.int32)
    tiles = jnp.arange(max_tiles, dtype=jnp.int32)
    te = jnp.sum((tiles[:, None] >= tile_end[None, :]).astype(jnp.int32), axis=1)
    te_last = jnp.max(jnp.where(counts > 0, experts, 0))
    te = jnp.where(tiles < n_tiles, te, te_last).astype(jnp.int32)
    later = jnp.where((counts > 0)[None, :] & (experts[None, :] > te[:, None]), experts[None, :], N_EXPERTS)
    nxt = jnp.min(later, axis=1)
    nxt = jnp.where(nxt < N_EXPERTS, nxt, -1).astype(jnp.int32)
    return te, nxt, n_tiles.reshape(1).astype(jnp.int32), start.astype(jnp.int32), dest


def kernel(x_prompt, x_sample, state_swa_k, state_swa_v, state_gla, meta_tokens, ln_emb_g, ln_emb_b, w_in, b_gate, attn_sink, w_alpha2, b_alpha, gla_norm_g, w_attn_br, w_gla_br, w_out, ln1_g, ln1_b, w_router_group, b_router_group, w_router_expert, b_router_expert, w_exp_gate, w_exp_up, w_exp_down, ln2_g, ln2_b):
    B, seq, _ = x_prompt.shape
    n_seq, t_s, _ = x_sample.shape
    depth = w_in.shape[0]
    assert depth == 1 and seq % ATT_BLOCK == 0 and t_s == 8 and SKIP_ROWS == ATT_BLOCK == WINDOW
    lp = SKIP_ROWS + seq
    NP, NS = B * lp, n_seq * t_s
    NR = NP + NS
    assert NP % ROW_TILE == 0 and NS % ROW_TILE == 0 and n_seq % SAMPLE_SEQS == 0
    assert seq % DISPATCH_CHUNK == 0 and NS % DISPATCH_CHUNK == 0
    l = 0
    row2 = lambda a: a.reshape(1, -1)

    head = jnp.concatenate([jnp.zeros((FRONT_PAD, D_MODEL), F32), meta_tokens], axis=0)
    pos = np.arange(NR)
    moe_valid = jnp.asarray(~((pos < NP) & (pos % lp < SKIP_ROWS)), F32).reshape(1, NR)

    wi = w_in[l]
    c_ga = sum((Q_W, KV_W, KV_W, GLA_DK, GLA_DK, GLA_DV))
    assert c_ga == W_IN_SPLIT
    w_pieces = (wi[:, :c_ga].astype(BF16), wi[:, c_ga + GLA_RANK:].astype(BF16),
                jnp.pad(wi[:, c_ga:c_ga + GLA_RANK], ((0, 0), (0, LANES - GLA_RANK))).astype(BF16))
    wa2_bf = jnp.concatenate([w_alpha2[l], jnp.zeros((LANES - GLA_RANK, GLA_DK), F32)], axis=0).astype(BF16)

    h, q, k, v, gq, gk, gv, la, gr, gate = _inproj(x_prompt, x_sample.reshape(NS, D_MODEL), head, row2(ln_emb_g),
                                                   row2(ln_emb_b), w_pieces, wa2_bf, row2(b_alpha[l]),
                                                   row2(b_gate[l]))

    sink = attn_sink[l]
    ya_p, k_p, v_p = _swa_prompt(sink, q, k, v, B, lp)
    buf_k = state_swa_k[l].reshape(n_seq, WINDOW, KV_W)
    buf_v = state_swa_v[l].reshape(n_seq, WINDOW, KV_W)
    ya_s, nk_s, nv_s = _swa_sample(sink, q, k, v, buf_k, buf_v, NP, t_s)

    og_p, s_p = _gla_prompt(gq, gk, gv, la, B, lp)
    og_s, s_s = _gla_sample(gq, gk, gv, la, state_gla[l], NP, t_s)

    wr = jnp.concatenate([w_router_expert[l], w_router_group[l],
                          jnp.zeros((D_MODEL, LANES - N_EXPERTS - N_GROUPS), F32)], axis=1)
    br = jnp.concatenate([b_router_expert[l], b_router_group[l],
                          jnp.zeros((LANES - N_EXPERTS - N_GROUPS,), F32)]).reshape(1, LANES)
    wr_hi = wr.astype(BF16)
    h1t, routing, wts, counts = _merge(h, ya_p, ya_s, og_p, og_s, gr, gate, moe_valid,
                                       row2(gla_norm_g[l]),
                                       w_attn_br[l].astype(BF16), w_gla_br[l].astype(BF16), w_out[l].astype(BF16),
                                       row2(ln1_g[l]), row2(ln1_b[l]), wr_hi, (wr - wr_hi.astype(F32)).astype(BF16), br)

    n_tok = B * seq + NS
    max_tiles = (2 * n_tok) // FFN_TILE + N_EXPERTS
    counts = counts[:, 0]
    te, nxt, n_tiles, start, dest = _dispatch_plan(routing[0:4], counts, max_tiles)
    routed = (B, seq // DISPATCH_CHUNK, SKIP_ROWS, lp, NS // DISPATCH_CHUNK, NP)
    xs = _dispatch(dest, start, counts, n_tiles, h1t, routed, max_tiles)
    ys = _ffn(te, nxt, n_tiles, xs, w_exp_gate[l], w_exp_up[l], w_exp_down[l])

    wts_t = wts[0:2].T
    g2, b2 = row2(ln2_g[l]), row2(ln2_b[l])
    skip_blocks = SKIP_ROWS // CMB_TILE
    y_p = _combine(dest, h1t, wts_t, ys, g2, b2, B * seq, 0, seq // CMB_TILE, skip_blocks)
    y_s = _combine(dest, h1t, wts_t, ys, g2, b2, NS, NP // CMB_TILE, 1, 0)

    kv_shape = (1, B, WINDOW, N_KV, HEAD_DIM)
    return (y_p.reshape(B, seq, D_MODEL), y_s.reshape(n_seq, t_s, D_MODEL),
            k_p.reshape(kv_shape), v_p.reshape(kv_shape), s_p[None],
            nk_s.reshape(1, n_seq, WINDOW, N_KV, HEAD_DIM), nv_s.reshape(1, n_seq, WINDOW, N_KV, HEAD_DIM),
            s_s[None])
```

```python
import functools

import numpy as np
import jax
import jax.numpy as jnp
from jax import lax
from jax.experimental import pallas as pl
from jax.experimental.pallas import tpu as pltpu

F32 = jnp.float32
BF16 = jnp.bfloat16

D_MODEL = 1024
N_META = 16
HEAD_DIM = 64
N_HEADS = 8
N_KV = 2
Q_PER_KV = 4
WINDOW = 128
ATT_BLOCK = 128
GLA_HEADS = 4
GLA_HK = 128
GLA_HV = 256
GLA_DK = GLA_HEADS * GLA_HK
GLA_DV = GLA_HEADS * GLA_HV
GLA_RANK = 16
GLA_TAU = 16.0
GLA_CHUNK = 64
N_GROUPS = 4
EXP_PER_GROUP = 8
N_EXPERTS = 32
D_EXPERT = 256
DN_ALPHA = 2.0 ** 0.25
EPS = 1e-5
NEG = -1e30

FRONT_PAD = (-N_META) % ATT_BLOCK
SKIP_ROWS = FRONT_PAD + N_META

Q_W, KV_W = N_HEADS * HEAD_DIM, N_KV * HEAD_DIM
SEG = {}
_o = 0
for _n, _w in (("q", Q_W), ("k", KV_W), ("v", KV_W), ("gq", GLA_DK), ("gk", GLA_DK), ("gv", GLA_DV)):
    SEG[_n] = (0, _o, _o + _w)
    _o += _w
W_IN_SPLIT = _o
SEG["gr"] = (1, 0, GLA_DV)
SEG["gate"] = (1, GLA_DV, GLA_DV + 2 * D_MODEL)
SEG["ga"] = (2, 0, 128)

ROW_TILE = 512
FFN_TILE = 256
CMB_TILE = 128
DISPATCH_CHUNK = 256
GLA_INTRA_CHUNKS = 6
SWA_BLOCKS = 1
INPROJ_TILE = 512
SAMPLE_SEQS = 8
LANES = 128
ROW_CHUNKS = D_MODEL // LANES
SUBLANES = 8
VMEM_LIMIT = 56 * 1024 * 1024


def _ln(x, g, b):
    mu = jnp.mean(x, -1, keepdims=True)
    xc = x - mu
    var = jnp.mean(xc * xc, -1, keepdims=True)
    return xc * lax.rsqrt(var + EPS) * g + b


def _sigmoid(x):
    return 0.5 * jnp.tanh(0.5 * x) + 0.5


def _dot(a, b):
    return jnp.dot(a, b, preferred_element_type=F32)


def _dot_nt(a, b):
    return lax.dot_general(a, b, (((1,), (1,)), ((), ())), preferred_element_type=F32)


def _inproj_kernel(*refs, blocks, batch_blocks, prompt_blocks):
    xp_refs, xs_refs = refs[:blocks], refs[blocks:2 * blocks]
    (head_ref, g_ref, b_ref, w0_ref, w1_ref, w2_ref, wa2_ref, ba_ref, bg_ref,
     h_ref, q_ref, k_ref, v_ref, gq_ref, gk_ref, gv_ref, la_ref, gr_ref, gate_ref) = refs[2 * blocks:]
    w_refs = (w0_ref, w1_ref, w2_ref)
    row = lax.broadcasted_iota(jnp.int32, (ATT_BLOCK, 1), 0)
    xs, keeps = [], []
    for s in range(blocks):
        p = pl.program_id(0) * blocks + s
        is_sample = p >= prompt_blocks
        is_head = jnp.logical_and(jnp.logical_not(is_sample), p % batch_blocks == 0)
        xs.append(jnp.where(is_sample, xs_refs[s][...], jnp.where(is_head, head_ref[...], xp_refs[s][...])))
        keeps.append(jnp.where(jnp.logical_and(is_head, row < FRONT_PAD), 0.0, 1.0))
    h = _ln(jnp.concatenate(xs, 0), g_ref[...], b_ref[...])
    h_ref[...] = h
    hb = h.astype(BF16)
    keep = jnp.concatenate(keeps, 0)

    def seg(name):
        piece, a, b = SEG[name]
        return _dot(hb, w_refs[piece][:, a:b])

    qkv = _dot(hb, w0_ref[:, SEG["q"][1]:SEG["v"][2]])
    q_ref[...] = qkv[:, :Q_W]
    k_ref[...] = qkv[:, Q_W:Q_W + KV_W]
    v_ref[...] = qkv[:, Q_W + KV_W:]
    gq_ref[...] = seg("gq") * (GLA_HK ** -0.5)
    gk_ref[...] = seg("gk") * keep
    gv_ref[...] = seg("gv") * keep
    gr = seg("gr")
    gr_ref[...] = (gr * _sigmoid(gr)).astype(BF16)
    gate_ref[...] = _sigmoid(seg("gate") + bg_ref[...]).astype(BF16)
    z = _dot(seg("ga").astype(BF16), wa2_ref[...]) + ba_ref[...]
    la = (jnp.minimum(z, 0.0) - jnp.log(1.0 + jnp.exp(-jnp.abs(z)))) * (1.0 / GLA_TAU)
    la_ref[...] = la * keep


def _inproj(x_prompt, x_sample, head, ln_g, ln_b, w_pieces, wa2_bf, b_alpha, b_gate):
    B, seq, _ = x_prompt.shape
    blk = ATT_BLOCK
    seq_blocks = seq // blk
    batch_blocks = seq_blocks + 1
    prompt_blocks = B * batch_blocks
    sample_blocks = x_sample.shape[0] // blk
    n = (prompt_blocks + sample_blocks) * blk
    tm = INPROJ_TILE
    blocks = tm // blk
    xp = x_prompt.reshape(B * seq, D_MODEL)

    def prompt_block(s):
        def imap(i):
            p = jnp.minimum(i * blocks + s, prompt_blocks - 1)
            return (p // batch_blocks * seq_blocks + jnp.maximum(p % batch_blocks - 1, 0), 0)
        return pl.BlockSpec((blk, D_MODEL), imap)

    def sample_block(s):
        return pl.BlockSpec((blk, D_MODEL), lambda i: (jnp.clip(i * blocks + s - prompt_blocks, 0, sample_blocks - 1), 0))

    widths = [D_MODEL, Q_W, KV_W, KV_W, GLA_DK, GLA_DK, GLA_DV, GLA_DK, GLA_DV, 2 * D_MODEL]
    row = lambda w: pl.BlockSpec((tm, w), lambda i: (i, 0))
    const = lambda a: pl.BlockSpec(a.shape, lambda i: (0,) * a.ndim, pipeline_mode=pl.Buffered(1))
    return pl.pallas_call(
        functools.partial(_inproj_kernel, blocks=blocks, batch_blocks=batch_blocks, prompt_blocks=prompt_blocks),
        grid=(n // tm,),
        in_specs=[prompt_block(s) for s in range(blocks)] + [sample_block(s) for s in range(blocks)]
                 + [const(a) for a in (head, ln_g, ln_b, *w_pieces, wa2_bf, b_alpha, b_gate)],
        out_specs=[row(w) for w in widths],
        out_shape=[jax.ShapeDtypeStruct((n, w), BF16 if i >= len(widths) - 2 else F32) for i, w in enumerate(widths)],
        compiler_params=pltpu.CompilerParams(dimension_semantics=("parallel",), vmem_limit_bytes=VMEM_LIMIT),
        name="inproj",
    )(*([xp] * blocks), *([x_sample] * blocks), head, ln_g, ln_b, *w_pieces, wa2_bf, b_alpha, b_gate)


def _softmax_pv(s, sink, vv):
    m = jnp.maximum(jnp.max(s, -1, keepdims=True), sink)
    p = jnp.exp(s - m)
    l = jnp.sum(p, -1, keepdims=True) + jnp.exp(sink - m)
    return _dot(p.astype(BF16), vv) / l


def _swa_prompt_kernel(sink_ref, bias_ref, q_ref, kp_ref, kc_ref, vp_ref, vc_ref, o_ref, kl_ref, vl_ref):
    nq = q_ref.shape[0] // ATT_BLOCK
    q_all = q_ref[...] * (HEAD_DIM ** -0.5)
    kb = jnp.concatenate([kp_ref[...], kc_ref[...]], 0)
    vb = jnp.concatenate([vp_ref[...], vc_ref[...]], 0)
    low = lax.broadcasted_iota(jnp.int32, (1, 2 * HEAD_DIM), 1) < HEAD_DIM
    k_low = jnp.where(low, kb, 0.0)
    k_high = jnp.where(low, 0.0, kb)
    keys = {(0, 0): k_low.astype(BF16), (0, 1): pltpu.roll(k_low, HEAD_DIM, 1).astype(BF16),
            (1, 0): pltpu.roll(k_high, HEAD_DIM, 1).astype(BF16), (1, 1): k_high.astype(BF16)}
    ones_col = (lax.broadcasted_iota(jnp.int32, (1, 2 * HEAD_DIM), 1) == HEAD_DIM).astype(F32)
    values = [jnp.where(low, vb, ones_col).astype(BF16),
              jnp.where(low, pltpu.roll(vb, HEAD_DIM, 1), ones_col).astype(BF16)]
    for blk in range(nq):
        rows = slice(blk * ATT_BLOCK, (blk + 1) * ATT_BLOCK)
        band = slice(blk * ATT_BLOCK, (blk + 2) * ATT_BLOCK)
        variant = jnp.minimum(pl.program_id(1) * nq + blk, 2)
        outs = {}
        for kv in range(N_KV):
            for half in range(2):
                heads = [kv * Q_PER_KV + half, kv * Q_PER_KV + half + 2]
                qms = [jnp.where(low if half == 0 else jnp.logical_not(low),
                                 q_all[rows, (h // 2) * 2 * HEAD_DIM:(h // 2 + 1) * 2 * HEAD_DIM], 0.0).astype(BF16)
                       for h in heads]
                s2 = _dot_nt(jnp.concatenate(qms, 0), keys[(kv, half)][band])
                for n, h in enumerate(heads):
                    s = s2[n * ATT_BLOCK:(n + 1) * ATT_BLOCK] + bias_ref[variant, h]
                    m = jnp.maximum(jnp.max(s, -1, keepdims=True), sink_ref[h])
                    pv = _dot(jnp.exp(s - m).astype(BF16), values[kv][band])
                    outs[h] = pv / (pv[:, HEAD_DIM:HEAD_DIM + 1] + jnp.exp(sink_ref[h] - m))
        for pair in range(N_HEADS // 2):
            o_ref[rows, pair * 2 * HEAD_DIM:(pair + 1) * 2 * HEAD_DIM] = jnp.where(
                low, outs[2 * pair], pltpu.roll(outs[2 * pair + 1], HEAD_DIM, 1))

    @pl.when(pl.program_id(1) == pl.num_programs(1) - 1)
    def _():
        kl_ref[0] = kc_ref[(nq - 1) * ATT_BLOCK:nq * ATT_BLOCK, :]
        vl_ref[0] = vc_ref[(nq - 1) * ATT_BLOCK:nq * ATT_BLOCK, :]


def _swa_bias_table():
    r = np.arange(ATT_BLOCK)[:, None]
    c = np.arange(2 * ATT_BLOCK)[None, :]
    dist = r - c + ATT_BLOCK
    slopes = 2.0 ** -(np.arange(N_HEADS) + 1.0)
    table = np.empty((3, N_HEADS, ATT_BLOCK, 2 * ATT_BLOCK), np.float32)
    for j in range(3):
        seen = (dist >= 0) & (dist < WINDOW) & ((j - 1) * ATT_BLOCK + c - FRONT_PAD >= 0)
        table[j] = np.where(seen[None], -slopes[:, None, None] * dist[None], NEG)
    return table


def _swa_prompt(sink, q, k, v, batch, lp):
    nq = SWA_BLOCKS
    nb = lp // ATT_BLOCK
    assert nb >= 3 and nb % nq == 0
    steps = nb // nq
    n = batch * lp
    bias = jnp.asarray(_swa_bias_table())
    cur = lambda w: pl.BlockSpec((nq * ATT_BLOCK, w), lambda b, j: (b * steps + j, 0))
    prev = lambda w: pl.BlockSpec((ATT_BLOCK, w), lambda b, j: (b * nb + jnp.maximum(j * nq - 1, 0), 0))
    last = pl.BlockSpec((1, ATT_BLOCK, KV_W), lambda b, j: (b, 0, 0))
    return pl.pallas_call(
        _swa_prompt_kernel,
        grid=(batch, steps),
        in_specs=[pl.BlockSpec(memory_space=pltpu.SMEM), pl.BlockSpec(bias.shape, lambda b, j: (0, 0, 0, 0)),
                  cur(Q_W), prev(KV_W), cur(KV_W), prev(KV_W), cur(KV_W)],
        out_specs=[cur(Q_W), last, last],
        out_shape=[jax.ShapeDtypeStruct((n, Q_W), F32), jax.ShapeDtypeStruct((batch, ATT_BLOCK, KV_W), F32),
                   jax.ShapeDtypeStruct((batch, ATT_BLOCK, KV_W), F32)],
        compiler_params=pltpu.CompilerParams(dimension_semantics=("parallel", "arbitrary")),
        name="swa_prompt",
    )(sink, bias, q, k, k, v, v)


def _swa_sample_kernel(sink_ref, q_ref, k_ref, v_ref, bk_ref, bv_ref, o_ref, nk_ref, nv_ref, *, t_s):
    nbuf = WINDOW
    span = 2 * WINDOW
    rows = Q_PER_KV * t_s
    r = lax.broadcasted_iota(jnp.int32, (rows, span), 0)
    c = lax.broadcasted_iota(jnp.int32, (rows, span), 1)
    t = r % t_s
    dist = t + nbuf - c
    mask = (dist >= 0) & (dist < WINDOW) & (c < nbuf + t_s)
    distf = dist.astype(F32)
    g_col = lax.broadcasted_iota(jnp.int32, (rows, 1), 0) // t_s
    fill = jnp.zeros((span - nbuf - t_s, KV_W), F32)

    def one_seq(s, carry):
        rs = pl.ds(pl.multiple_of(s * t_s, t_s), t_s)
        q = q_ref[rs, :]
        k_new = k_ref[rs, :]
        v_new = v_ref[rs, :]
        bk = bk_ref[s]
        bv = bv_ref[s]
        k_all = jnp.concatenate([bk, k_new, fill], 0)
        v_all = jnp.concatenate([bv, v_new, fill], 0)
        for kv in range(N_KV):
            qg = jnp.concatenate(
                [q[:, (kv * Q_PER_KV + g) * HEAD_DIM:(kv * Q_PER_KV + g + 1) * HEAD_DIM] for g in range(Q_PER_KV)], 0)
            kk = k_all[:, kv * HEAD_DIM:(kv + 1) * HEAD_DIM].astype(BF16)
            vv = v_all[:, kv * HEAD_DIM:(kv + 1) * HEAD_DIM].astype(BF16)
            slope = jnp.zeros((rows, 1), F32)
            sink = jnp.zeros((rows, 1), F32)
            for g in range(Q_PER_KV):
                h = kv * Q_PER_KV + g
                slope = jnp.where(g_col == g, 2.0 ** -(h + 1), slope)
                sink = jnp.where(g_col == g, sink_ref[h], sink)
            sc = _dot_nt(qg.astype(BF16), kk) * (HEAD_DIM ** -0.5) - slope * distf
            sc = jnp.where(mask, sc, NEG)
            o = _softmax_pv(sc, sink, vv)
            for g in range(Q_PER_KV):
                h = kv * Q_PER_KV + g
                o_ref[rs, h * HEAD_DIM:(h + 1) * HEAD_DIM] = o[g * t_s:(g + 1) * t_s]
        nk_ref[s, 0:nbuf - t_s, :] = bk[t_s:, :]
        nk_ref[s, nbuf - t_s:nbuf, :] = k_new
        nv_ref[s, 0:nbuf - t_s, :] = bv[t_s:, :]
        nv_ref[s, nbuf - t_s:nbuf, :] = v_new
        return carry

    lax.fori_loop(0, SAMPLE_SEQS, one_seq, 0, unroll=True)


def _swa_sample(sink, q, k, v, buf_k, buf_v, row0, t_s):
    n_seq = buf_k.shape[0]
    sb = SAMPLE_SEQS
    rb = sb * t_s
    b0 = row0 // rb
    rows = lambda w: pl.BlockSpec((rb, w), lambda i: (b0 + i, 0))
    bufs = pl.BlockSpec((sb, WINDOW, KV_W), lambda i: (i, 0, 0))
    return pl.pallas_call(
        functools.partial(_swa_sample_kernel, t_s=t_s),
        grid=(n_seq // sb,),
        in_specs=[pl.BlockSpec(memory_space=pltpu.SMEM), rows(Q_W), rows(KV_W), rows(KV_W), bufs, bufs],
        out_specs=[pl.BlockSpec((rb, Q_W), lambda i: (i, 0)), bufs, bufs],
        out_shape=[jax.ShapeDtypeStruct((n_seq * t_s, Q_W), F32),
                   jax.ShapeDtypeStruct(buf_k.shape, F32), jax.ShapeDtypeStruct(buf_v.shape, F32)],
        compiler_params=pltpu.CompilerParams(dimension_semantics=("parallel",)),
        name="swa_sample",
    )(sink, q, k, v, buf_k, buf_v)


def _gla_tables(chunk):
    t = np.arange(chunk)[:, None]
    u = np.arange(chunk)[None, :]
    masks = []
    w = chunk // 2
    while w >= 1:
        masks.append((t // (2 * w) == u // (2 * w)) & ((t // w) % 2 == 1) & ((u // w) % 2 == 0))
        w //= 2
    return (u <= t).astype(np.float32), np.stack(masks, 0).astype(np.float32)


def _level_exponents(b, la, w):
    C = b.shape[0]
    row = lax.broadcasted_iota(jnp.int32, b.shape, 0)
    if w >= 4:
        pieces = [jnp.broadcast_to(b[p + w - 1:p + w], (2 * w, b.shape[1])) for p in range(0, C, 2 * w)]
        ref = pieces[0] if len(pieces) == 1 else jnp.concatenate(pieces, 0)
        return jnp.where((row & w) != 0, b - ref, ref - b)
    if w == 2:
        m = row & 3
        nxt = pltpu.roll(la, C - 1, 0)
        prv = pltpu.roll(la, 1, 0)
        return jnp.where(m == 2, la, jnp.where(m == 3, la + prv, jnp.where(m == 0, nxt, 0.0)))
    return jnp.where((row & 1) != 0, la, 0.0)


def _split3(x):
    hi = x.astype(BF16)
    r1 = x - hi.astype(F32)
    mid = r1.astype(BF16)
    lo = (r1 - mid.astype(F32)).astype(BF16)
    return hi, mid, lo


def _gla_intra_kernel(g_ref, m_ref, q_ref, k_ref, v_ref, la_ref, o_ref, qe_ref, ke_ref, vt_ref, d_ref):
    C = GLA_CHUNK
    n_lvl = m_ref.shape[0]
    G = g_ref[...]
    eye = (lax.broadcasted_iota(jnp.int32, (C, C), 0) == lax.broadcasted_iota(jnp.int32, (C, C), 1)).astype(F32)
    for j in range(q_ref.shape[0] // C):
        rs = slice(j * C, (j + 1) * C)
        la = la_ref[rs, :]
        hi, mid, lo = _split3(la)
        b = _dot(G, hi) + _dot(G, mid) + _dot(G, lo)
        b_last = b[C - 1:C]
        q_all = q_ref[rs, :]
        k_all = k_ref[rs, :]
        qe_ref[0, rs, :] = (q_all * jnp.exp(b)).astype(BF16)
        ke_ref[0, rs, :] = (k_all * jnp.exp(b_last - b)).astype(BF16)
        d_ref[0, j] = jnp.broadcast_to(jnp.exp(b_last), (SUBLANES, GLA_DK))
        q_lvl, k_lvl = [], []
        for l in range(n_lvl):
            El = jnp.exp(_level_exponents(b, la, C >> (l + 1)))
            q_lvl.append((q_all * El).astype(BF16))
            k_lvl.append((k_all * El).astype(BF16))
        for h in range(GLA_HEADS):
            ks = slice(h * GLA_HK, (h + 1) * GLA_HK)
            vs = slice(h * GLA_HV, (h + 1) * GLA_HV)
            v = v_ref[rs, vs]
            att = eye * jnp.sum(q_all[:, ks] * k_all[:, ks], -1, keepdims=True)
            for l in range(n_lvl):
                att = att + m_ref[l] * _dot_nt(q_lvl[l][:, ks], k_lvl[l][:, ks])
            o_ref[0, rs, vs] = _dot(att.astype(BF16), v.astype(BF16))
            vt_ref[0, j, vs, :] = v.T.astype(BF16)


def _gla_inter_kernel(o_ref, qe_ref, ke_ref, vt_ref, d_ref, og_ref, s_ref, st_ref):
    c = pl.program_id(0)
    batch = o_ref.shape[0]

    @pl.when(c == 0)
    def _():
        st_ref[...] = jnp.zeros_like(st_ref)

    for b in range(batch):
        for h in range(GLA_HEADS):
            ks = slice(h * GLA_HK, (h + 1) * GLA_HK)
            vs = slice(h * GLA_HV, (h + 1) * GLA_HV)
            st = st_ref[b * GLA_HEADS + h]
            og_ref[b, :, vs] = o_ref[b, :, vs] + _dot_nt(qe_ref[b, :, ks], st.astype(BF16))
            st_ref[b * GLA_HEADS + h] = st * d_ref[b, 0, 0:1, ks] + _dot(vt_ref[b, 0, vs, :], ke_ref[b, :, ks])

    @pl.when(c == pl.num_programs(0) - 1)
    def _():
        for b in range(batch):
            for h in range(GLA_HEADS):
                s_ref[b, h] = st_ref[b * GLA_HEADS + h].T


def _gla_prompt(gq, gk, gv, la, batch, lp):
    C = GLA_CHUNK
    nc = lp // C
    G, M = _gla_tables(C)
    G = jnp.asarray(G, BF16)
    M = jnp.asarray(M, F32)
    cps = GLA_INTRA_CHUNKS
    assert nc % cps == 0
    rows = lambda w: pl.BlockSpec((cps * C, w), lambda b, c: (b * (nc // cps) + c, 0))
    rows3 = lambda w: pl.BlockSpec((1, cps * C, w), lambda b, c: (b, c, 0))
    full = lambda a: pl.BlockSpec(a.shape, lambda b, c: (0,) * a.ndim)
    o_intra, qe, ke, vt, d = pl.pallas_call(
        _gla_intra_kernel,
        grid=(batch, nc // cps),
        in_specs=[full(G), full(M), rows(GLA_DK), rows(GLA_DK), rows(GLA_DV), rows(GLA_DK)],
        out_specs=[rows3(GLA_DV), rows3(GLA_DK), rows3(GLA_DK),
                   pl.BlockSpec((1, cps, GLA_DV, C), lambda b, c: (b, c, 0, 0)),
                   pl.BlockSpec((1, cps, SUBLANES, GLA_DK), lambda b, c: (b, c, 0, 0))],
        out_shape=[jax.ShapeDtypeStruct((batch, lp, GLA_DV), F32),
                   jax.ShapeDtypeStruct((batch, lp, GLA_DK), BF16), jax.ShapeDtypeStruct((batch, lp, GLA_DK), BF16),
                   jax.ShapeDtypeStruct((batch, nc, GLA_DV, C), BF16),
                   jax.ShapeDtypeStruct((batch, nc, SUBLANES, GLA_DK), F32)],
        compiler_params=pltpu.CompilerParams(dimension_semantics=("parallel", "parallel")),
        name="gla_intra",
    )(G, M, gq, gk, gv, la)
    chunk = lambda w: pl.BlockSpec((batch, C, w), lambda c: (0, c, 0))
    og, s_fin = pl.pallas_call(
        _gla_inter_kernel,
        grid=(nc,),
        in_specs=[chunk(GLA_DV), chunk(GLA_DK), chunk(GLA_DK),
                  pl.BlockSpec((batch, 1, GLA_DV, C), lambda c: (0, c, 0, 0)),
                  pl.BlockSpec((batch, 1, SUBLANES, GLA_DK), lambda c: (0, c, 0, 0))],
        out_specs=[chunk(GLA_DV), pl.BlockSpec((batch, GLA_HEADS, GLA_HK, GLA_HV), lambda c: (0, 0, 0, 0))],
        out_shape=[jax.ShapeDtypeStruct((batch, lp, GLA_DV), F32),
                   jax.ShapeDtypeStruct((batch, GLA_HEADS, GLA_HK, GLA_HV), F32)],
        scratch_shapes=[pltpu.VMEM((batch * GLA_HEADS, GLA_HV, GLA_HK), F32)],
        compiler_params=pltpu.CompilerParams(dimension_semantics=("arbitrary",)),
        name="gla_inter",
    )(o_intra, qe, ke, vt, d)
    return og.reshape(batch * lp, GLA_DV), s_fin


def _gla_sample_kernel(q_ref, k_ref, v_ref, la_ref, s0_ref, o_ref, s_ref, *, t_s):
    T = t_s
    row = lax.broadcasted_iota(jnp.int32, (T, GLA_HK), 0)
    k_fill = jnp.zeros((GLA_HK - T - SUBLANES, GLA_HK), F32)
    v_fill = jnp.zeros((GLA_HK - T, GLA_HV), F32)

    def one_seq(s, carry):
        rs = pl.ds(pl.multiple_of(s * T, T), T)
        for h in range(GLA_HEADS):
            ks = slice(h * GLA_HK, (h + 1) * GLA_HK)
            vs = slice(h * GLA_HV, (h + 1) * GLA_HV)
            q = q_ref[rs, ks]
            k = k_ref[rs, ks]
            v = v_ref[rs, vs]
            b = la_ref[rs, ks]
            sh = 1
            while sh < T:
                b = b + jnp.where(row >= sh, pltpu.roll(b, sh, 0), 0.0)
                sh *= 2
            S = s0_ref[s, h]
            o = _dot((q * jnp.exp(b)).astype(BF16), S.astype(BF16))
            for j in range(T):
                e = jnp.exp(jnp.where(row >= j, b - b[j:j + 1], NEG))
                a_col = jnp.sum(q * k[j:j + 1] * e, -1, keepdims=True)
                o = o + a_col * v[j:j + 1]
            o_ref[rs, vs] = o
            b_last = b[T - 1:T]
            ke = k * jnp.exp(b_last - b)
            kt = jnp.concatenate([ke, jnp.broadcast_to(jnp.exp(b_last), (SUBLANES, GLA_HK)), k_fill], 0).T
            v_pad = jnp.concatenate([v, v_fill], 0)
            s_ref[s, h] = S * kt[:, T:T + 1] + _dot(kt.astype(BF16), v_pad.astype(BF16))
        return carry

    lax.fori_loop(0, SAMPLE_SEQS, one_seq, 0)


def _gla_sample(gq, gk, gv, la, s0, row0, t_s):
    n_seq = s0.shape[0]
    sb = SAMPLE_SEQS
    rb = sb * t_s
    b0 = row0 // rb
    rows = lambda w: pl.BlockSpec((rb, w), lambda i: (b0 + i, 0))
    st = pl.BlockSpec((sb, GLA_HEADS, GLA_HK, GLA_HV), lambda i: (i, 0, 0, 0))
    return pl.pallas_call(
        functools.partial(_gla_sample_kernel, t_s=t_s),
        grid=(n_seq // sb,),
        in_specs=[rows(GLA_DK), rows(GLA_DK), rows(GLA_DV), rows(GLA_DK), st],
        out_specs=[pl.BlockSpec((rb, GLA_DV), lambda i: (i, 0)), st],
        out_shape=[jax.ShapeDtypeStruct((n_seq * t_s, GLA_DV), F32), jax.ShapeDtypeStruct(s0.shape, F32)],
        compiler_params=pltpu.CompilerParams(dimension_semantics=("parallel",), vmem_limit_bytes=VMEM_LIMIT),
        name="gla_sample",
    )(gq, gk, gv, la, s0)


def _route(lt, valid):
    tm = lt.shape[1]
    el = lt[0:N_EXPERTS]
    gl = lt[N_EXPERTS:N_EXPERTS + N_GROUPS]
    g_max = jnp.max(gl, 0, keepdims=True)
    g_row = lax.broadcasted_iota(jnp.int32, (N_GROUPS, tm), 0)
    g_idx = jnp.min(jnp.where(gl == g_max, g_row, N_GROUPS), 0, keepdims=True)
    p_max = 1.0 / jnp.sum(jnp.exp(gl - g_max), 0, keepdims=True)
    e_row = lax.broadcasted_iota(jnp.int32, (N_EXPERTS, tm), 0)
    m1 = jnp.where(e_row // EXP_PER_GROUP == g_idx, el, -jnp.inf)
    v1 = jnp.max(m1, 0, keepdims=True)
    i1 = jnp.min(jnp.where(m1 == v1, e_row, N_EXPERTS), 0, keepdims=True)
    m2 = jnp.where(e_row == i1, -jnp.inf, m1)
    v2 = jnp.max(m2, 0, keepdims=True)
    i2 = jnp.min(jnp.where(m2 == v2, e_row, N_EXPERTS), 0, keepdims=True)
    e2 = jnp.exp(v2 - v1)
    w1 = p_max / (1.0 + e2)
    w2 = p_max * e2 / (1.0 + e2)
    o_row = lax.broadcasted_iota(jnp.int32, (SUBLANES, tm), 0)
    ids = jnp.where(o_row == 0, i1, jnp.where(o_row == 1, i2, -1))
    return jnp.where(valid, ids, -1), jnp.where(o_row == 0, w1, jnp.where(o_row == 1, w2, 0.0))


def _store_row_tiles(ref, x):
    t, d = x.shape
    n = d // LANES
    for s in range(n):
        ref[pl.ds(s, t, stride=n), :] = x[:, s * LANES:(s + 1) * LANES]


def _load_row_tiles(ref, t, n=ROW_CHUNKS):
    return jnp.concatenate([ref[pl.ds(s, t, stride=n), :] for s in range(n)], axis=1)


def _merge_kernel(h_ref, yap_ref, yas_ref, ogp_ref, ogs_ref, gr_ref, gate_ref, valid_ref, ng_ref,
                  wa_ref, wg_ref, wo_ref, g1_ref, b1_ref, wrh_ref, wrl_ref, br_ref, u_ref,
                  h1t_ref, ids_ref, wts_ref, cnt_ref, run_ref, *, prompt_tiles):
    @pl.when(pl.program_id(0) == 0)
    def _():
        run_ref[...] = jnp.zeros_like(run_ref)

    h = h_ref[...]
    is_prompt = pl.program_id(0) < prompt_tiles
    og = jnp.where(is_prompt, ogp_ref[...], ogs_ref[...])
    ya = jnp.where(is_prompt, yap_ref[...], yas_ref[...])
    parts = []
    for hh in range(GLA_HEADS):
        o = og[:, hh * GLA_HV:(hh + 1) * GLA_HV]
        parts.append(o * lax.rsqrt(jnp.mean(o * o, -1, keepdims=True) + EPS))
    y_gla = jnp.concatenate(parts, 1) * ng_ref[...] * gr_ref[...].astype(F32)
    a = _dot(ya.astype(BF16), wa_ref[...])
    b = _dot(y_gla.astype(BF16), wg_ref[...])
    hm = gate_ref[:, :D_MODEL].astype(F32) * a + gate_ref[:, D_MODEL:].astype(F32) * b
    mix = _dot(hm.astype(BF16), wo_ref[...])
    h1 = _ln(DN_ALPHA * h + mix, g1_ref[...], b1_ref[...])
    _store_row_tiles(h1t_ref, h1)
    h_hi = h1.astype(BF16)
    h_lo = (h1 - h_hi.astype(F32)).astype(BF16)
    both = _dot(h_hi, jnp.concatenate([wrh_ref[...], wrl_ref[...]], 1))
    logits = both[:, :LANES] + both[:, LANES:] + _dot(h_lo, wrh_ref[...]) + br_ref[...]
    ids, wts_ref[...] = _route(logits.T, valid_ref[...] > 0.0)
    tm = ids.shape[1]
    e_row = lax.broadcasted_iota(jnp.int32, (N_EXPERTS, tm), 0)
    run = run_ref[:, 0:1]
    onehots = [(e_row == ids[kk:kk + 1]).astype(F32) for kk in range(2)]
    befores = _dot(jnp.concatenate(onehots, 0).astype(BF16), u_ref[...])
    ranks = []
    for kk in range(2):
        onehot = onehots[kk]
        before = befores[kk * N_EXPERTS:(kk + 1) * N_EXPERTS]
        ranks.append(jnp.sum(onehot * (run + before), 0, keepdims=True).astype(jnp.int32))
        run = run + jnp.sum(onehot, 1, keepdims=True)
    run_ref[...] = jnp.broadcast_to(run, run_ref.shape)
    o_row = lax.broadcasted_iota(jnp.int32, (SUBLANES, tm), 0)
    ids_ref[...] = jnp.where(o_row == 2, ranks[0], jnp.where(o_row == 3, ranks[1], ids))
    cnt_ref[...] = run_ref[...].astype(jnp.int32)


def _merge(h, ya_p, ya_s, og_p, og_s, gr, gate, valid, ng, wa, wg, wo, g1, b1, wrh, wrl, br):
    n = h.shape[0]
    tm = ROW_TILE
    u = jnp.asarray(np.triu(np.ones((tm, tm), np.float32), 1), BF16)
    pt = ya_p.shape[0] // tm
    st = ya_s.shape[0] // tm
    row = lambda w: pl.BlockSpec((tm, w), lambda i: (i, 0))
    row_p = lambda w: pl.BlockSpec((tm, w), lambda i: (jnp.minimum(i, pt - 1), 0))
    row_s = lambda w: pl.BlockSpec((tm, w), lambda i: (jnp.clip(i - pt, 0, st - 1), 0))
    lane = lambda r: pl.BlockSpec((r, tm), lambda i: (0, i))
    full = lambda a: pl.BlockSpec(a.shape, lambda i: (0,) * a.ndim)
    return pl.pallas_call(
        functools.partial(_merge_kernel, prompt_tiles=pt),
        grid=(n // tm,),
        in_specs=[row(D_MODEL), row_p(Q_W), row_s(Q_W), row_p(GLA_DV), row_s(GLA_DV), row(GLA_DV), row(2 * D_MODEL),
                  lane(1), full(ng), full(wa), full(wg), full(wo), full(g1), full(b1),
                  full(wrh), full(wrl), full(br), full(u)],
        out_specs=[pl.BlockSpec((tm * ROW_CHUNKS, LANES), lambda i: (i, 0)), lane(SUBLANES), lane(SUBLANES),
                   pl.BlockSpec((N_EXPERTS, LANES), lambda i: (0, 0))],
        out_shape=[jax.ShapeDtypeStruct((n * ROW_CHUNKS, LANES), F32),
                   jax.ShapeDtypeStruct((SUBLANES, n), jnp.int32), jax.ShapeDtypeStruct((SUBLANES, n), F32),
                   jax.ShapeDtypeStruct((N_EXPERTS, LANES), jnp.int32)],
        scratch_shapes=[pltpu.VMEM((N_EXPERTS, LANES), F32)],
        compiler_params=pltpu.CompilerParams(dimension_semantics=("arbitrary",), vmem_limit_bytes=VMEM_LIMIT),
        name="merge",
    )(h, ya_p, ya_s, og_p, og_s, gr, gate, valid, ng, wa, wg, wo, g1, b1, wrh, wrl, br, u)


def _gather_row_tiles(idx_ref, idx0, src_hbm, dst, sem, n):
    def body(r, carry):
        t = idx_ref[idx0 + r]
        pltpu.make_async_copy(src_hbm.at[pl.ds(pl.multiple_of(t * ROW_CHUNKS, ROW_CHUNKS), ROW_CHUNKS), :],
                              dst.at[pl.ds(pl.multiple_of(r * ROW_CHUNKS, ROW_CHUNKS), ROW_CHUNKS), :], sem).start()
        return carry
    lax.fori_loop(0, n, body, 0, unroll=8)


def _wait_row_tiles(src_hbm, dst, sem, n):
    pltpu.make_async_copy(src_hbm.at[pl.ds(0, n * ROW_CHUNKS), :], dst, sem).wait()


def _tiles(ref, first, n=1):
    return ref.at[pl.ds(pl.multiple_of(first * ROW_CHUNKS, ROW_CHUNKS), n * ROW_CHUNKS), :]


def _dispatch_kernel(dest_ref, start_ref, cnt_ref, nt_ref, h_hbm, xs_hbm, buf, zbuf, sem_in, sem_out, sem_z,
                     *, ranges, n_rows, max_tiles):
    CH = DISPATCH_CHUNK
    T = FFN_TILE
    c = pl.program_id(0)
    n_chunks = pl.num_programs(0)

    def load(row0, slot):
        return pltpu.make_async_copy(_tiles(h_hbm, row0, CH), buf.at[slot], sem_in.at[slot])

    def scatter(row0, slot):
        def body(r, carry):
            for kk in range(2):
                d = dest_ref[kk * n_rows + row0 + r]
                pltpu.make_async_copy(_tiles(buf.at[slot], r), _tiles(xs_hbm, d), sem_out.at[slot]).start()
            return carry
        lax.fori_loop(0, CH, body, 0, unroll=8)

    def drain(slot):
        for _ in range(2):
            pltpu.make_async_copy(buf.at[slot], _tiles(xs_hbm, 0, CH), sem_out.at[slot]).wait()

    groups, per_group, first, stride, extra, extra_first = ranges

    def row0(j):
        in_group = first + (j // per_group) * stride + (j % per_group) * CH
        return jnp.where(j < groups * per_group, in_group, extra_first + (j - groups * per_group) * CH)

    @pl.when(c == 0)
    def _():
        for j in range(2):
            load(row0(j), j).start()

    slot = c % 3
    load(row0(c), slot).wait()
    scatter(row0(c), slot)

    @pl.when(c > 0)
    def _():
        drain((c + 2) % 3)

    @pl.when(c + 2 < n_chunks)
    def _():
        load(row0(c + 2), (c + 2) % 3).start()

    @pl.when(c == n_chunks - 1)
    def _():
        drain(slot)
        _zero_unowned_slots(start_ref, cnt_ref, nt_ref, xs_hbm, zbuf, sem_z, max_tiles)


def _zero_unowned_slots(start_ref, cnt_ref, nt_ref, xs_hbm, zbuf, sem_z, max_tiles):
    T = FFN_TILE
    zbuf[...] = jnp.zeros_like(zbuf)

    def tail_copies(e, wait):
        cnt = cnt_ref[e]
        n = (T - (cnt & (T - 1))) & (T - 1)
        first = start_ref[e] + cnt
        for bit in reversed(range(T.bit_length() - 1)):
            size = 1 << bit

            @pl.when((n & size) != 0)
            def _():
                cp = pltpu.make_async_copy(_tiles(zbuf, 0, size),
                                           _tiles(xs_hbm, first + ((n >> (bit + 1)) << (bit + 1)), size), sem_z)
                cp.wait() if wait else cp.start()

    def unused_tile(t, wait):
        cp = pltpu.make_async_copy(zbuf, _tiles(xs_hbm, t * T, T), sem_z)
        cp.wait() if wait else cp.start()

    for wait in (False, True):
        def per_expert(e, carry, wait=wait):
            tail_copies(e, wait)
            return carry

        def per_tile(t, carry, wait=wait):
            unused_tile(t, wait)
            return carry
        lax.fori_loop(0, N_EXPERTS, per_expert, 0)
        lax.fori_loop(nt_ref[0], max_tiles, per_tile, 0)


def _dispatch(dest, start, counts, n_tiles, h1t, ranges, max_tiles):
    T = FFN_TILE
    n_rows = h1t.shape[0] // ROW_CHUNKS
    return pl.pallas_call(
        functools.partial(_dispatch_kernel, ranges=ranges, n_rows=n_rows, max_tiles=max_tiles),
        grid_spec=pltpu.PrefetchScalarGridSpec(
            num_scalar_prefetch=4,
            grid=(ranges[0] * ranges[1] + ranges[4],),
            in_specs=[pl.BlockSpec(memory_space=pl.ANY)],
            out_specs=pl.BlockSpec(memory_space=pl.ANY),
            scratch_shapes=[pltpu.VMEM((3, DISPATCH_CHUNK * ROW_CHUNKS, LANES), F32),
                            pltpu.VMEM((T * ROW_CHUNKS, LANES), F32),
                            pltpu.SemaphoreType.DMA((3,)), pltpu.SemaphoreType.DMA((3,)), pltpu.SemaphoreType.DMA]),
        out_shape=jax.ShapeDtypeStruct((max_tiles * T * ROW_CHUNKS, LANES), F32),
        compiler_params=pltpu.CompilerParams(dimension_semantics=("arbitrary",)),
        name="dispatch",
    )(dest, start, counts, n_tiles, h1t)


def _ffn_kernel(te_ref, nxt_ref, nt_ref, x_ref, wg_hbm, wu_hbm, wd_hbm, out_ref, stage_g, stage_u, stage_d, sem,
                wgb, wub, wdb):
    i = pl.program_id(0)
    nt = nt_ref[0]
    T = FFN_TILE

    def stage(e):
        return [pltpu.make_async_copy(src.at[e], dst, sem.at[n])
                for n, (src, dst) in enumerate(((wg_hbm, stage_g), (wu_hbm, stage_u), (wd_hbm, stage_d)))]

    @pl.when(i == 0)
    def _():
        for cp in stage(te_ref[0]):
            cp.start()

    @pl.when(i < nt)
    def _():
        @pl.when((i == 0) | (te_ref[i] != te_ref[jnp.maximum(i - 1, 0)]))
        def _():
            for cp in stage(te_ref[i]):
                cp.wait()
            wgb[...] = stage_g[...].astype(BF16)
            wub[...] = stage_u[...].astype(BF16)
            wdb[...] = stage_d[...].astype(BF16)

            @pl.when(nxt_ref[i] >= 0)
            def _():
                for cp in stage(nxt_ref[i]):
                    cp.start()

        x = _load_row_tiles(x_ref, T).astype(BF16)
        g = _dot(x, wgb[...])
        u = _dot(x, wub[...])
        _store_row_tiles(out_ref, _dot((g * _sigmoid(g) * u).astype(BF16), wdb[...]))

    @pl.when(i >= nt)
    def _():
        out_ref[...] = jnp.zeros_like(out_ref)


def _ffn(tile_expert, next_expert, n_tiles, xs, w_g, w_u, w_d):
    T = FFN_TILE
    max_tiles = tile_expert.shape[0]
    hbm = pl.BlockSpec(memory_space=pl.ANY)
    tile = lambda imap: pl.BlockSpec((T * ROW_CHUNKS, LANES), imap)
    return pl.pallas_call(
        _ffn_kernel,
        grid_spec=pltpu.PrefetchScalarGridSpec(
            num_scalar_prefetch=3,
            grid=(max_tiles,),
            in_specs=[tile(lambda i, te, nxt, nt: (jnp.minimum(i, nt[0] - 1), 0)), hbm, hbm, hbm],
            out_specs=tile(lambda i, te, nxt, nt: (i, 0)),
            scratch_shapes=[pltpu.VMEM(w_g.shape[1:], F32), pltpu.VMEM(w_u.shape[1:], F32),
                            pltpu.VMEM(w_d.shape[1:], F32), pltpu.SemaphoreType.DMA((3,)),
                            pltpu.VMEM(w_g.shape[1:], BF16), pltpu.VMEM(w_u.shape[1:], BF16),
                            pltpu.VMEM(w_d.shape[1:], BF16)]),
        out_shape=jax.ShapeDtypeStruct(xs.shape, F32),
        compiler_params=pltpu.CompilerParams(dimension_semantics=("arbitrary",)),
        name="ffn",
    )(tile_expert, next_expert, n_tiles, xs, w_g, w_u, w_d)


def _combine_kernel(dest_ref, h_ref, w_ref, g_ref, b_ref, ys_hbm, y_ref, buf, sem, *, blk, n_rows):
    i = pl.program_id(0)
    n = pl.num_programs(0)
    T = CMB_TILE

    def gather(step, slot):
        for kk in range(2):
            _gather_row_tiles(dest_ref, kk * n_rows + blk(step) * T, ys_hbm, buf.at[slot, kk], sem.at[slot], T)

    @pl.when(i == 0)
    def _():
        gather(0, 0)

    @pl.when(i + 1 < n)
    def _():
        gather(i + 1, (i + 1) % 2)

    slot = i % 2
    for kk in range(2):
        _wait_row_tiles(ys_hbm, buf.at[slot, kk], sem.at[slot], T)
    w = w_ref[...]
    ff = w[:, 0:1] * _load_row_tiles(buf.at[slot, 0], T) + w[:, 1:2] * _load_row_tiles(buf.at[slot, 1], T)
    y_ref[...] = _ln(DN_ALPHA * _load_row_tiles(h_ref, T) + ff, g_ref[...], b_ref[...])


def _combine(dest, h1t, wts_t, ys, g2, b2, n_out, first_block, blocks_per_batch, skip_blocks):
    T = CMB_TILE
    n_rows = h1t.shape[0] // ROW_CHUNKS
    if skip_blocks:
        blk = lambda i: first_block + i + (i // blocks_per_batch + 1) * skip_blocks
    else:
        blk = lambda i: first_block + i
    full = lambda a: pl.BlockSpec(a.shape, lambda i, d: (0,) * a.ndim)
    return pl.pallas_call(
        functools.partial(_combine_kernel, blk=blk, n_rows=n_rows),
        grid_spec=pltpu.PrefetchScalarGridSpec(
            num_scalar_prefetch=1,
            grid=(n_out // T,),
            in_specs=[pl.BlockSpec((T * ROW_CHUNKS, LANES), lambda i, d: (blk(i), 0)),
                      pl.BlockSpec((T, 2), lambda i, d: (blk(i), 0)),
                      full(g2), full(b2), pl.BlockSpec(memory_space=pl.ANY)],
            out_specs=pl.BlockSpec((T, D_MODEL), lambda i, d: (i, 0)),
            scratch_shapes=[pltpu.VMEM((2, 2, T * ROW_CHUNKS, LANES), F32), pltpu.SemaphoreType.DMA((2,))]),
        out_shape=jax.ShapeDtypeStruct((n_out, D_MODEL), F32),
        compiler_params=pltpu.CompilerParams(dimension_semantics=("arbitrary",)),
        name="combine",
    )(dest, h1t, wts_t, g2, b2, ys)


def _dispatch_plan(routing, counts, max_tiles):
    T = FFN_TILE
    tiles_e = (counts + T - 1) // T
    tile_end = jnp.cumsum(tiles_e)
    n_tiles = tile_end[-1]
    start = (tile_end - tiles_e) * T
    ids, rank = routing[0:2], routing[2:4]
    onehot = (ids[..., None] == jnp.arange(N_EXPERTS, dtype=jnp.int32)).astype(jnp.int32)
    dest = (jnp.sum(onehot * start, axis=-1) + jnp.where(ids >= 0, rank, 0)).reshape(-1).astype(jnp.int32)
    experts = jnp.arange(N_EXPERTS, dtype=jnp.int32)
    tiles = jnp.arange(max_tiles, dtype=jnp.int32)
    te = jnp.sum((tiles[:, None] >= tile_end[None, :]).astype(jnp.int32), axis=1)
    te_last = jnp.max(jnp.where(counts > 0, experts, 0))
    te = jnp.where(tiles < n_tiles, te, te_last).astype(jnp.int32)
    later = jnp.where((counts > 0)[None, :] & (experts[None, :] > te[:, None]), experts[None, :], N_EXPERTS)
    nxt = jnp.min(later, axis=1)
    nxt = jnp.where(nxt < N_EXPERTS, nxt, -1).astype(jnp.int32)
    return te, nxt, n_tiles.reshape(1).astype(jnp.int32), start.astype(jnp.int32), dest


def kernel(x_prompt, x_sample, state_swa_k, state_swa_v, state_gla, meta_tokens, ln_emb_g, ln_emb_b, w_in, b_gate, attn_sink, w_alpha2, b_alpha, gla_norm_g, w_attn_br, w_gla_br, w_out, ln1_g, ln1_b, w_router_group, b_router_group, w_router_expert, b_router_expert, w_exp_gate, w_exp_up, w_exp_down, ln2_g, ln2_b):
    B, seq, _ = x_prompt.shape
    n_seq, t_s, _ = x_sample.shape
    depth = w_in.shape[0]
    assert depth == 1 and seq % ATT_BLOCK == 0 and t_s == 8 and SKIP_ROWS == ATT_BLOCK == WINDOW
    lp = SKIP_ROWS + seq
    NP, NS = B * lp, n_seq * t_s
    NR = NP + NS
    assert NP % ROW_TILE == 0 and NS % ROW_TILE == 0 and n_seq % SAMPLE_SEQS == 0
    assert seq % DISPATCH_CHUNK == 0 and NS % DISPATCH_CHUNK == 0
    l = 0
    row2 = lambda a: a.reshape(1, -1)

    head = jnp.concatenate([jnp.zeros((FRONT_PAD, D_MODEL), F32), meta_tokens], axis=0)
    pos = np.arange(NR)
    moe_valid = jnp.asarray(~((pos < NP) & (pos % lp < SKIP_ROWS)), F32).reshape(1, NR)

    wi = w_in[l]
    c_ga = sum((Q_W, KV_W, KV_W, GLA_DK, GLA_DK, GLA_DV))
    assert c_ga == W_IN_SPLIT
    w_pieces = (wi[:, :c_ga].astype(BF16), wi[:, c_ga + GLA_RANK:].astype(BF16),
                jnp.pad(wi[:, c_ga:c_ga + GLA_RANK], ((0, 0), (0, LANES - GLA_RANK))).astype(BF16))
    wa2_bf = jnp.concatenate([w_alpha2[l], jnp.zeros((LANES - GLA_RANK, GLA_DK), F32)], axis=0).astype(BF16)

    h, q, k, v, gq, gk, gv, la, gr, gate = _inproj(x_prompt, x_sample.reshape(NS, D_MODEL), head, row2(ln_emb_g),
                                                   row2(ln_emb_b), w_pieces, wa2_bf, row2(b_alpha[l]),
                                                   row2(b_gate[l]))

    sink = attn_sink[l]
    ya_p, k_p, v_p = _swa_prompt(sink, q, k, v, B, lp)
    buf_k = state_swa_k[l].reshape(n_seq, WINDOW, KV_W)
    buf_v = state_swa_v[l].reshape(n_seq, WINDOW, KV_W)
    ya_s, nk_s, nv_s = _swa_sample(sink, q, k, v, buf_k, buf_v, NP, t_s)

    og_p, s_p = _gla_prompt(gq, gk, gv, la, B, lp)
    og_s, s_s = _gla_sample(gq, gk, gv, la, state_gla[l], NP, t_s)

    wr = jnp.concatenate([w_router_expert[l], w_router_group[l],
                          jnp.zeros((D_MODEL, LANES - N_EXPERTS - N_GROUPS), F32)], axis=1)
    br = jnp.concatenate([b_router_expert[l], b_router_group[l],
                          jnp.zeros((LANES - N_EXPERTS - N_GROUPS,), F32)]).reshape(1, LANES)
    wr_hi = wr.astype(BF16)
    h1t, routing, wts, counts = _merge(h, ya_p, ya_s, og_p, og_s, gr, gate, moe_valid,
                                       row2(gla_norm_g[l]),
                                       w_attn_br[l].astype(BF16), w_gla_br[l].astype(BF16), w_out[l].astype(BF16),
                                       row2(ln1_g[l]), row2(ln1_b[l]), wr_hi, (wr - wr_hi.astype(F32)).astype(BF16), br)

    n_tok = B * seq + NS
    max_tiles = (2 * n_tok) // FFN_TILE + N_EXPERTS
    counts = counts[:, 0]
    te, nxt, n_tiles, start, dest = _dispatch_plan(routing[0:4], counts, max_tiles)
    routed = (B, seq // DISPATCH_CHUNK, SKIP_ROWS, lp, NS // DISPATCH_CHUNK, NP)
    xs = _dispatch(dest, start, counts, n_tiles, h1t, routed, max_tiles)
    ys = _ffn(te, nxt, n_tiles, xs, w_exp_gate[l], w_exp_up[l], w_exp_down[l])

    wts_t = wts[0:2].T
    g2, b2 = row2(ln2_g[l]), row2(ln2_b[l])
    skip_blocks = SKIP_ROWS // CMB_TILE
    y_p = _combine(dest, h1t, wts_t, ys, g2, b2, B * seq, 0, seq // CMB_TILE, skip_blocks)
    y_s = _combine(dest, h1t, wts_t, ys, g2, b2, NS, NP // CMB_TILE, 1, 0)

    kv_shape = (1, B, WINDOW, N_KV, HEAD_DIM)
    return (y_p.reshape(B, seq, D_MODEL), y_s.reshape(n_seq, t_s, D_MODEL),
            k_p.reshape(kv_shape), v_p.reshape(kv_shape), s_p[None],
            nk_s.reshape(1, n_seq, WINDOW, N_KV, HEAD_DIM), nv_s.reshape(1, n_seq, WINDOW, N_KV, HEAD_DIM),
            s_s[None])
```

```python
import functools

import numpy as np
import jax
import jax.numpy as jnp
from jax import lax
from jax.experimental import pallas as pl
from jax.experimental.pallas import tpu as pltpu

F32 = jnp.float32
BF16 = jnp.bfloat16

D_MODEL = 1024
N_META = 16
HEAD_DIM = 64
N_HEADS = 8
N_KV = 2
Q_PER_KV = 4
WINDOW = 128
ATT_BLOCK = 128
GLA_HEADS = 4
GLA_HK = 128
GLA_HV = 256
GLA_DK = GLA_HEADS * GLA_HK
GLA_DV = GLA_HEADS * GLA_HV
GLA_RANK = 16
GLA_TAU = 16.0
GLA_CHUNK = 64
N_GROUPS = 4
EXP_PER_GROUP = 8
N_EXPERTS = 32
D_EXPERT = 256
DN_ALPHA = 2.0 ** 0.25
EPS = 1e-5
NEG = -1e30

FRONT_PAD = (-N_META) % ATT_BLOCK
SKIP_ROWS = FRONT_PAD + N_META

Q_W, KV_W = N_HEADS * HEAD_DIM, N_KV * HEAD_DIM
SEG = {}
_o = 0
for _n, _w in (("q", Q_W), ("k", KV_W), ("v", KV_W), ("gq", GLA_DK), ("gk", GLA_DK), ("gv", GLA_DV)):
    SEG[_n] = (0, _o, _o + _w)
    _o += _w
W_IN_SPLIT = _o
SEG["gr"] = (1, 0, GLA_DV)
SEG["gate"] = (1, GLA_DV, GLA_DV + 2 * D_MODEL)
SEG["ga"] = (2, 0, 128)

ROW_TILE = 512
FFN_TILE = 256
CMB_TILE = 128
DISPATCH_CHUNK = 256
GLA_INTRA_CHUNKS = 6
SWA_BLOCKS = 1
INPROJ_TILE = 512
SAMPLE_SEQS = 8
LANES = 128
ROW_CHUNKS = D_MODEL // LANES
SUBLANES = 8
VMEM_LIMIT = 56 * 1024 * 1024


def _ln(x, g, b):
    mu = jnp.mean(x, -1, keepdims=True)
    xc = x - mu
    var = jnp.mean(xc * xc, -1, keepdims=True)
    return xc * lax.rsqrt(var + EPS) * g + b


def _sigmoid(x):
    return 0.5 * jnp.tanh(0.5 * x) + 0.5


def _dot(a, b):
    return jnp.dot(a, b, preferred_element_type=F32)


def _dot_nt(a, b):
    return lax.dot_general(a, b, (((1,), (1,)), ((), ())), preferred_element_type=F32)


def _inproj_kernel(*refs, blocks, batch_blocks, prompt_blocks):
    xp_refs, xs_refs = refs[:blocks], refs[blocks:2 * blocks]
    (head_ref, g_ref, b_ref, w0_ref, w1_ref, w2_ref, wa2_ref, ba_ref, bg_ref,
     h_ref, q_ref, k_ref, v_ref, gq_ref, gk_ref, gv_ref, la_ref, gr_ref, gate_ref) = refs[2 * blocks:]
    w_refs = (w0_ref, w1_ref, w2_ref)
    row = lax.broadcasted_iota(jnp.int32, (ATT_BLOCK, 1), 0)
    xs, keeps = [], []
    for s in range(blocks):
        p = pl.program_id(0) * blocks + s
        is_sample = p >= prompt_blocks
        is_head = jnp.logical_and(jnp.logical_not(is_sample), p % batch_blocks == 0)
        xs.append(jnp.where(is_sample, xs_refs[s][...], jnp.where(is_head, head_ref[...], xp_refs[s][...])))
        keeps.append(jnp.where(jnp.logical_and(is_head, row < FRONT_PAD), 0.0, 1.0))
    h = _ln(jnp.concatenate(xs, 0), g_ref[...], b_ref[...])
    h_ref[...] = h
    hb = h.astype(BF16)
    keep = jnp.concatenate(keeps, 0)

    def seg(name):
        piece, a, b = SEG[name]
        return _dot(hb, w_refs[piece][:, a:b])

    qkv = _dot(hb, w0_ref[:, SEG["q"][1]:SEG["v"][2]])
    q_ref[...] = qkv[:, :Q_W]
    k_ref[...] = qkv[:, Q_W:Q_W + KV_W]
    v_ref[...] = qkv[:, Q_W + KV_W:]
    gq_ref[...] = seg("gq") * (GLA_HK ** -0.5)
    gk_ref[...] = seg("gk") * keep
    gv_ref[...] = seg("gv") * keep
    gr = seg("gr")
    gr_ref[...] = (gr * _sigmoid(gr)).astype(BF16)
    gate_ref[...] = _sigmoid(seg("gate") + bg_ref[...]).astype(BF16)
    z = _dot(seg("ga").astype(BF16), wa2_ref[...]) + ba_ref[...]
    la = (jnp.minimum(z, 0.0) - jnp.log(1.0 + jnp.exp(-jnp.abs(z)))) * (1.0 / GLA_TAU)
    la_ref[...] = la * keep


def _inproj(x_prompt, x_sample, head, ln_g, ln_b, w_pieces, wa2_bf, b_alpha, b_gate):
    B, seq, _ = x_prompt.shape
    blk = ATT_BLOCK
    seq_blocks = seq // blk
    batch_blocks = seq_blocks + 1
    prompt_blocks = B * batch_blocks
    sample_blocks = x_sample.shape[0] // blk
    n = (prompt_blocks + sample_blocks) * blk
    tm = INPROJ_TILE
    blocks = tm // blk
    xp = x_prompt.reshape(B * seq, D_MODEL)

    def prompt_block(s):
        def imap(i):
            p = jnp.minimum(i * blocks + s, prompt_blocks - 1)
            return (p // batch_blocks * seq_blocks + jnp.maximum(p % batch_blocks - 1, 0), 0)
        return pl.BlockSpec((blk, D_MODEL), imap)

    def sample_block(s):
        return pl.BlockSpec((blk, D_MODEL), lambda i: (jnp.clip(i * blocks + s - prompt_blocks, 0, sample_blocks - 1), 0))

    widths = [D_MODEL, Q_W, KV_W, KV_W, GLA_DK, GLA_DK, GLA_DV, GLA_DK, GLA_DV, 2 * D_MODEL]
    row = lambda w: pl.BlockSpec((tm, w), lambda i: (i, 0))
    const = lambda a: pl.BlockSpec(a.shape, lambda i: (0,) * a.ndim, pipeline_mode=pl.Buffered(1))
    return pl.pallas_call(
        functools.partial(_inproj_kernel, blocks=blocks, batch_blocks=batch_blocks, prompt_blocks=prompt_blocks),
        grid=(n // tm,),
        in_specs=[prompt_block(s) for s in range(blocks)] + [sample_block(s) for s in range(blocks)]
                 + [const(a) for a in (head, ln_g, ln_b, *w_pieces, wa2_bf, b_alpha, b_gate)],
        out_specs=[row(w) for w in widths],
        out_shape=[jax.ShapeDtypeStruct((n, w), BF16 if i >= len(widths) - 2 else F32) for i, w in enumerate(widths)],
        compiler_params=pltpu.CompilerParams(dimension_semantics=("parallel",), vmem_limit_bytes=VMEM_LIMIT),
        name="inproj",
    )(*([xp] * blocks), *([x_sample] * blocks), head, ln_g, ln_b, *w_pieces, wa2_bf, b_alpha, b_gate)


def _softmax_pv(s, sink, vv):
    m = jnp.maximum(jnp.max(s, -1, keepdims=True), sink)
    p = jnp.exp(s - m)
    l = jnp.sum(p, -1, keepdims=True) + jnp.exp(sink - m)
    return _dot(p.astype(BF16), vv) / l


def _swa_prompt_kernel(sink_ref, bias_ref, q_ref, kp_ref, kc_ref, vp_ref, vc_ref, o_ref, kl_ref, vl_ref):
    nq = q_ref.shape[0] // ATT_BLOCK
    q_all = q_ref[...] * (HEAD_DIM ** -0.5)
    kb = jnp.concatenate([kp_ref[...], kc_ref[...]], 0)
    vb = jnp.concatenate([vp_ref[...], vc_ref[...]], 0)
    low = lax.broadcasted_iota(jnp.int32, (1, 2 * HEAD_DIM), 1) < HEAD_DIM
    k_low = jnp.where(low, kb, 0.0)
    k_high = jnp.where(low, 0.0, kb)
    keys = {(0, 0): k_low.astype(BF16), (0, 1): pltpu.roll(k_low, HEAD_DIM, 1).astype(BF16),
            (1, 0): pltpu.roll(k_high, HEAD_DIM, 1).astype(BF16), (1, 1): k_high.astype(BF16)}
    ones_col = (lax.broadcasted_iota(jnp.int32, (1, 2 * HEAD_DIM), 1) == HEAD_DIM).astype(F32)
    values = [jnp.where(low, vb, ones_col).astype(BF16),
              jnp.where(low, pltpu.roll(vb, HEAD_DIM, 1), ones_col).astype(BF16)]
    for blk in range(nq):
        rows = slice(blk * ATT_BLOCK, (blk + 1) * ATT_BLOCK)
        band = slice(blk * ATT_BLOCK, (blk + 2) * ATT_BLOCK)
        variant = jnp.minimum(pl.program_id(1) * nq + blk, 2)
        outs = {}
        for kv in range(N_KV):
            for half in range(2):
                heads = [kv * Q_PER_KV + half, kv * Q_PER_KV + half + 2]
                qms = [jnp.where(low if half == 0 else jnp.logical_not(low),
                                 q_all[rows, (h // 2) * 2 * HEAD_DIM:(h // 2 + 1) * 2 * HEAD_DIM], 0.0).astype(BF16)
                       for h in heads]
                s2 = _dot_nt(jnp.concatenate(qms, 0), keys[(kv, half)][band])
                for n, h in enumerate(heads):
                    s = s2[n * ATT_BLOCK:(n + 1) * ATT_BLOCK] + bias_ref[variant, h]
                    m = jnp.maximum(jnp.max(s, -1, keepdims=True), sink_ref[h])
                    pv = _dot(jnp.exp(s - m).astype(BF16), values[kv][band])
                    outs[h] = pv / (pv[:, HEAD_DIM:HEAD_DIM + 1] + jnp.exp(sink_ref[h] - m))
        for pair in range(N_HEADS // 2):
            o_ref[rows, pair * 2 * HEAD_DIM:(pair + 1) * 2 * HEAD_DIM] = jnp.where(
                low, outs[2 * pair], pltpu.roll(outs[2 * pair + 1], HEAD_DIM, 1))

    @pl.when(pl.program_id(1) == pl.num_programs(1) - 1)
    def _():
        kl_ref[0] = kc_ref[(nq - 1) * ATT_BLOCK:nq * ATT_BLOCK, :]
        vl_ref[0] = vc_ref[(nq - 1) * ATT_BLOCK:nq * ATT_BLOCK, :]


def _swa_bias_table():
    r = np.arange(ATT_BLOCK)[:, None]
    c = np.arange(2 * ATT_BLOCK)[None, :]
    dist = r - c + ATT_BLOCK
    slopes = 2.0 ** -(np.arange(N_HEADS) + 1.0)
    table = np.empty((3, N_HEADS, ATT_BLOCK, 2 * ATT_BLOCK), np.float32)
    for j in range(3):
        seen = (dist >= 0) & (dist < WINDOW) & ((j - 1) * ATT_BLOCK + c - FRONT_PAD >= 0)
        table[j] = np.where(seen[None], -slopes[:, None, None] * dist[None], NEG)
    return table


def _swa_prompt(sink, q, k, v, batch, lp):
    nq = SWA_BLOCKS
    nb = lp // ATT_BLOCK
    assert nb >= 3 and nb % nq == 0
    steps = nb // nq
    n = batch * lp
    bias = jnp.asarray(_swa_bias_table())
    cur = lambda w: pl.BlockSpec((nq * ATT_BLOCK, w), lambda b, j: (b * steps + j, 0))
    prev = lambda w: pl.BlockSpec((ATT_BLOCK, w), lambda b, j: (b * nb + jnp.maximum(j * nq - 1, 0), 0))
    last = pl.BlockSpec((1, ATT_BLOCK, KV_W), lambda b, j: (b, 0, 0))
    return pl.pallas_call(
        _swa_prompt_kernel,
        grid=(batch, steps),
        in_specs=[pl.BlockSpec(memory_space=pltpu.SMEM), pl.BlockSpec(bias.shape, lambda b, j: (0, 0, 0, 0)),
                  cur(Q_W), prev(KV_W), cur(KV_W), prev(KV_W), cur(KV_W)],
        out_specs=[cur(Q_W), last, last],
        out_shape=[jax.ShapeDtypeStruct((n, Q_W), F32), jax.ShapeDtypeStruct((batch, ATT_BLOCK, KV_W), F32),
                   jax.ShapeDtypeStruct((batch, ATT_BLOCK, KV_W), F32)],
        compiler_params=pltpu.CompilerParams(dimension_semantics=("parallel", "arbitrary")),
        name="swa_prompt",
    )(sink, bias, q, k, k, v, v)


def _swa_sample_kernel(sink_ref, q_ref, k_ref, v_ref, bk_ref, bv_ref, o_ref, nk_ref, nv_ref, *, t_s):
    nbuf = WINDOW
    span = 2 * WINDOW
    rows = Q_PER_KV * t_s
    r = lax.broadcasted_iota(jnp.int32, (rows, span), 0)
    c = lax.broadcasted_iota(jnp.int32, (rows, span), 1)
    t = r % t_s
    dist = t + nbuf - c
    mask = (dist >= 0) & (dist < WINDOW) & (c < nbuf + t_s)
    distf = dist.astype(F32)
    g_col = lax.broadcasted_iota(jnp.int32, (rows, 1), 0) // t_s
    fill = jnp.zeros((span - nbuf - t_s, KV_W), F32)

    def one_seq(s, carry):
        rs = pl.ds(pl.multiple_of(s * t_s, t_s), t_s)
        q = q_ref[rs, :]
        k_new = k_ref[rs, :]
        v_new = v_ref[rs, :]
        bk = bk_ref[s]
        bv = bv_ref[s]
        k_all = jnp.concatenate([bk, k_new, fill], 0)
        v_all = jnp.concatenate([bv, v_new, fill], 0)
        for kv in range(N_KV):
            qg = jnp.concatenate(
                [q[:, (kv * Q_PER_KV + g) * HEAD_DIM:(kv * Q_PER_KV + g + 1) * HEAD_DIM] for g in range(Q_PER_KV)], 0)
            kk = k_all[:, kv * HEAD_DIM:(kv + 1) * HEAD_DIM].astype(BF16)
            vv = v_all[:, kv * HEAD_DIM:(kv + 1) * HEAD_DIM].astype(BF16)
            slope = jnp.zeros((rows, 1), F32)
            sink = jnp.zeros((rows, 1), F32)
            for g in range(Q_PER_KV):
                h = kv * Q_PER_KV + g
                slope = jnp.where(g_col == g, 2.0 ** -(h + 1), slope)
                sink = jnp.where(g_col == g, sink_ref[h], sink)
            sc = _dot_nt(qg.astype(BF16), kk) * (HEAD_DIM ** -0.5) - slope * distf
            sc = jnp.where(mask, sc, NEG)
            o = _softmax_pv(sc, sink, vv)
            for g in range(Q_PER_KV):
                h = kv * Q_PER_KV + g
                o_ref[rs, h * HEAD_DIM:(h + 1) * HEAD_DIM] = o[g * t_s:(g + 1) * t_s]
        nk_ref[s, 0:nbuf - t_s, :] = bk[t_s:, :]
        nk_ref[s, nbuf - t_s:nbuf, :] = k_new
        nv_ref[s, 0:nbuf - t_s, :] = bv[t_s:, :]
        nv_ref[s, nbuf - t_s:nbuf, :] = v_new
        return carry

    lax.fori_loop(0, SAMPLE_SEQS, one_seq, 0, unroll=True)


def _swa_sample(sink, q, k, v, buf_k, buf_v, row0, t_s):
    n_seq = buf_k.shape[0]
    sb = SAMPLE_SEQS
    rb = sb * t_s
    b0 = row0 // rb
    rows = lambda w: pl.BlockSpec((rb, w), lambda i: (b0 + i, 0))
    bufs = pl.BlockSpec((sb, WINDOW, KV_W), lambda i: (i, 0, 0))
    return pl.pallas_call(
        functools.partial(_swa_sample_kernel, t_s=t_s),
        grid=(n_seq // sb,),
        in_specs=[pl.BlockSpec(memory_space=pltpu.SMEM), rows(Q_W), rows(KV_W), rows(KV_W), bufs, bufs],
        out_specs=[pl.BlockSpec((rb, Q_W), lambda i: (i, 0)), bufs, bufs],
        out_shape=[jax.ShapeDtypeStruct((n_seq * t_s, Q_W), F32),
                   jax.ShapeDtypeStruct(buf_k.shape, F32), jax.ShapeDtypeStruct(buf_v.shape, F32)],
        compiler_params=pltpu.CompilerParams(dimension_semantics=("parallel",)),
        name="swa_sample",
    )(sink, q, k, v, buf_k, buf_v)


def _gla_tables(chunk):
    t = np.arange(chunk)[:, None]
    u = np.arange(chunk)[None, :]
    masks = []
    w = chunk // 2
    while w >= 1:
        masks.append((t // (2 * w) == u // (2 * w)) & ((t // w) % 2 == 1) & ((u // w) % 2 == 0))
        w //= 2
    return (u <= t).astype(np.float32), np.stack(masks, 0).astype(np.float32)


def _level_exponents(b, la, w):
    C = b.shape[0]
    row = lax.broadcasted_iota(jnp.int32, b.shape, 0)
    if w >= 4:
        pieces = [jnp.broadcast_to(b[p + w - 1:p + w], (2 * w, b.shape[1])) for p in range(0, C, 2 * w)]
        ref = pieces[0] if len(pieces) == 1 else jnp.concatenate(pieces, 0)
        return jnp.where((row & w) != 0, b - ref, ref - b)
    if w == 2:
        m = row & 3
        nxt = pltpu.roll(la, C - 1, 0)
        prv = pltpu.roll(la, 1, 0)
        return jnp.where(m == 2, la, jnp.where(m == 3, la + prv, jnp.where(m == 0, nxt, 0.0)))
    return jnp.where((row & 1) != 0, la, 0.0)


def _split3(x):
    hi = x.astype(BF16)
    r1 = x - hi.astype(F32)
    mid = r1.astype(BF16)
    lo = (r1 - mid.astype(F32)).astype(BF16)
    return hi, mid, lo


def _gla_intra_kernel(g_ref, m_ref, q_ref, k_ref, v_ref, la_ref, o_ref, qe_ref, ke_ref, vt_ref, d_ref):
    C = GLA_CHUNK
    n_lvl = m_ref.shape[0]
    G = g_ref[...]
    eye = (lax.broadcasted_iota(jnp.int32, (C, C), 0) == lax.broadcasted_iota(jnp.int32, (C, C), 1)).astype(F32)
    for j in range(q_ref.shape[0] // C):
        rs = slice(j * C, (j + 1) * C)
        la = la_ref[rs, :]
        hi, mid, lo = _split3(la)
        b = _dot(G, hi) + _dot(G, mid) + _dot(G, lo)
        b_last = b[C - 1:C]
        q_all = q_ref[rs, :]
        k_all = k_ref[rs, :]
        qe_ref[0, rs, :] = (q_all * jnp.exp(b)).astype(BF16)
        ke_ref[0, rs, :] = (k_all * jnp.exp(b_last - b)).astype(BF16)
        d_ref[0, j] = jnp.broadcast_to(jnp.exp(b_last), (SUBLANES, GLA_DK))
        q_lvl, k_lvl = [], []
        for l in range(n_lvl):
            El = jnp.exp(_level_exponents(b, la, C >> (l + 1)))
            q_lvl.append((q_all * El).astype(BF16))
            k_lvl.append((k_all * El).astype(BF16))
        for h in range(GLA_HEADS):
            ks = slice(h * GLA_HK, (h + 1) * GLA_HK)
            vs = slice(h * GLA_HV, (h + 1) * GLA_HV)
            v = v_ref[rs, vs]
            att = eye * jnp.sum(q_all[:, ks] * k_all[:, ks], -1, keepdims=True)
            for l in range(n_lvl):
                att = att + m_ref[l] * _dot_nt(q_lvl[l][:, ks], k_lvl[l][:, ks])
            o_ref[0, rs, vs] = _dot(att.astype(BF16), v.astype(BF16))
            vt_ref[0, j, vs, :] = v.T.astype(BF16)


def _gla_inter_kernel(o_ref, qe_ref, ke_ref, vt_ref, d_ref, og_ref, s_ref, st_ref):
    c = pl.program_id(0)
    batch = o_ref.shape[0]

    @pl.when(c == 0)
    def _():
        st_ref[...] = jnp.zeros_like(st_ref)

    for b in range(batch):
        for h in range(GLA_HEADS):
            ks = slice(h * GLA_HK, (h + 1) * GLA_HK)
            vs = slice(h * GLA_HV, (h + 1) * GLA_HV)
            st = st_ref[b * GLA_HEADS + h]
            og_ref[b, :, vs] = o_ref[b, :, vs] + _dot_nt(qe_ref[b, :, ks], st.astype(BF16))
            st_ref[b * GLA_HEADS + h] = st * d_ref[b, 0, 0:1, ks] + _dot(vt_ref[b, 0, vs, :], ke_ref[b, :, ks])

    @pl.when(c == pl.num_programs(0) - 1)
    def _():
        for b in range(batch):
            for h in range(GLA_HEADS):
                s_ref[b, h] = st_ref[b * GLA_HEADS + h].T


def _gla_prompt(gq, gk, gv, la, batch, lp):
    C = GLA_CHUNK
    nc = lp // C
    G, M = _gla_tables(C)
    G = jnp.asarray(G, BF16)
    M = jnp.asarray(M, F32)
    cps = GLA_INTRA_CHUNKS
    assert nc % cps == 0
    rows = lambda w: pl.BlockSpec((cps * C, w), lambda b, c: (b * (nc // cps) + c, 0))
    rows3 = lambda w: pl.BlockSpec((1, cps * C, w), lambda b, c: (b, c, 0))
    full = lambda a: pl.BlockSpec(a.shape, lambda b, c: (0,) * a.ndim)
    o_intra, qe, ke, vt, d = pl.pallas_call(
        _gla_intra_kernel,
        grid=(batch, nc // cps),
        in_specs=[full(G), full(M), rows(GLA_DK), rows(GLA_DK), rows(GLA_DV), rows(GLA_DK)],
        out_specs=[rows3(GLA_DV), rows3(GLA_DK), rows3(GLA_DK),
                   pl.BlockSpec((1, cps, GLA_DV, C), lambda b, c: (b, c, 0, 0)),
                   pl.BlockSpec((1, cps, SUBLANES, GLA_DK), lambda b, c: (b, c, 0, 0))],
        out_shape=[jax.ShapeDtypeStruct((batch, lp, GLA_DV), F32),
                   jax.ShapeDtypeStruct((batch, lp, GLA_DK), BF16), jax.ShapeDtypeStruct((batch, lp, GLA_DK), BF16),
                   jax.ShapeDtypeStruct((batch, nc, GLA_DV, C), BF16),
                   jax.ShapeDtypeStruct((batch, nc, SUBLANES, GLA_DK), F32)],
        compiler_params=pltpu.CompilerParams(dimension_semantics=("parallel", "parallel")),
        name="gla_intra",
    )(G, M, gq, gk, gv, la)
    chunk = lambda w: pl.BlockSpec((batch, C, w), lambda c: (0, c, 0))
    og, s_fin = pl.pallas_call(
        _gla_inter_kernel,
        grid=(nc,),
        in_specs=[chunk(GLA_DV), chunk(GLA_DK), chunk(GLA_DK),
                  pl.BlockSpec((batch, 1, GLA_DV, C), lambda c: (0, c, 0, 0)),
                  pl.BlockSpec((batch, 1, SUBLANES, GLA_DK), lambda c: (0, c, 0, 0))],
        out_specs=[chunk(GLA_DV), pl.BlockSpec((batch, GLA_HEADS, GLA_HK, GLA_HV), lambda c: (0, 0, 0, 0))],
        out_shape=[jax.ShapeDtypeStruct((batch, lp, GLA_DV), F32),
                   jax.ShapeDtypeStruct((batch, GLA_HEADS, GLA_HK, GLA_HV), F32)],
        scratch_shapes=[pltpu.VMEM((batch * GLA_HEADS, GLA_HV, GLA_HK), F32)],
        compiler_params=pltpu.CompilerParams(dimension_semantics=("arbitrary",)),
        name="gla_inter",
    )(o_intra, qe, ke, vt, d)
    return og.reshape(batch * lp, GLA_DV), s_fin


def _gla_sample_kernel(q_ref, k_ref, v_ref, la_ref, s0_ref, o_ref, s_ref, *, t_s):
    T = t_s
    row = lax.broadcasted_iota(jnp.int32, (T, GLA_HK), 0)
    k_fill = jnp.zeros((GLA_HK - T - SUBLANES, GLA_HK), F32)
    v_fill = jnp.zeros((GLA_HK - T, GLA_HV), F32)

    def one_seq(s, carry):
        rs = pl.ds(pl.multiple_of(s * T, T), T)
        for h in range(GLA_HEADS):
            ks = slice(h * GLA_HK, (h + 1) * GLA_HK)
            vs = slice(h * GLA_HV, (h + 1) * GLA_HV)
            q = q_ref[rs, ks]
            k = k_ref[rs, ks]
            v = v_ref[rs, vs]
            b = la_ref[rs, ks]
            sh = 1
            while sh < T:
                b = b + jnp.where(row >= sh, pltpu.roll(b, sh, 0), 0.0)
                sh *= 2
            S = s0_ref[s, h]
            o = _dot((q * jnp.exp(b)).astype(BF16), S.astype(BF16))
            for j in range(T):
                e = jnp.exp(jnp.where(row >= j, b - b[j:j + 1], NEG))
                a_col = jnp.sum(q * k[j:j + 1] * e, -1, keepdims=True)
                o = o + a_col * v[j:j + 1]
            o_ref[rs, vs] = o
            b_last = b[T - 1:T]
            ke = k * jnp.exp(b_last - b)
            kt = jnp.concatenate([ke, jnp.broadcast_to(jnp.exp(b_last), (SUBLANES, GLA_HK)), k_fill], 0).T
            v_pad = jnp.concatenate([v, v_fill], 0)
            s_ref[s, h] = S * kt[:, T:T + 1] + _dot(kt.astype(BF16), v_pad.astype(BF16))
        return carry

    lax.fori_loop(0, SAMPLE_SEQS, one_seq, 0)


def _gla_sample(gq, gk, gv, la, s0, row0, t_s):
    n_seq = s0.shape[0]
    sb = SAMPLE_SEQS
    rb = sb * t_s
    b0 = row0 // rb
    rows = lambda w: pl.BlockSpec((rb, w), lambda i: (b0 + i, 0))
    st = pl.BlockSpec((sb, GLA_HEADS, GLA_HK, GLA_HV), lambda i: (i, 0, 0, 0))
    return pl.pallas_call(
        functools.partial(_gla_sample_kernel, t_s=t_s),
        grid=(n_seq // sb,),
        in_specs=[rows(GLA_DK), rows(GLA_DK), rows(GLA_DV), rows(GLA_DK), st],
        out_specs=[pl.BlockSpec((rb, GLA_DV), lambda i: (i, 0)), st],
        out_shape=[jax.ShapeDtypeStruct((n_seq * t_s, GLA_DV), F32), jax.ShapeDtypeStruct(s0.shape, F32)],
        compiler_params=pltpu.CompilerParams(dimension_semantics=("parallel",), vmem_limit_bytes=VMEM_LIMIT),
        name="gla_sample",
    )(gq, gk, gv, la, s0)


def _route(lt, valid):
    tm = lt.shape[1]
    el = lt[0:N_EXPERTS]
    gl = lt[N_EXPERTS:N_EXPERTS + N_GROUPS]
    g_max = jnp.max(gl, 0, keepdims=True)
    g_row = lax.broadcasted_iota(jnp.int32, (N_GROUPS, tm), 0)
    g_idx = jnp.min(jnp.where(gl == g_max, g_row, N_GROUPS), 0, keepdims=True)
    p_max = 1.0 / jnp.sum(jnp.exp(gl - g_max), 0, keepdims=True)
    e_row = lax.broadcasted_iota(jnp.int32, (N_EXPERTS, tm), 0)
    m1 = jnp.where(e_row // EXP_PER_GROUP == g_idx, el, -jnp.inf)
    v1 = jnp.max(m1, 0, keepdims=True)
    i1 = jnp.min(jnp.where(m1 == v1, e_row, N_EXPERTS), 0, keepdims=True)
    m2 = jnp.where(e_row == i1, -jnp.inf, m1)
    v2 = jnp.max(m2, 0, keepdims=True)
    i2 = jnp.min(jnp.where(m2 == v2, e_row, N_EXPERTS), 0, keepdims=True)
    e2 = jnp.exp(v2 - v1)
    w1 = p_max / (1.0 + e2)
    w2 = p_max * e2 / (1.0 + e2)
    o_row = lax.broadcasted_iota(jnp.int32, (SUBLANES, tm), 0)
    ids = jnp.where(o_row == 0, i1, jnp.where(o_row == 1, i2, -1))
    return jnp.where(valid, ids, -1), jnp.where(o_row == 0, w1, jnp.where(o_row == 1, w2, 0.0))


def _store_row_tiles(ref, x):
    t, d = x.shape
    n = d // LANES
    for s in range(n):
        ref[pl.ds(s, t, stride=n), :] = x[:, s * LANES:(s + 1) * LANES]


def _load_row_tiles(ref, t, n=ROW_CHUNKS):
    return jnp.concatenate([ref[pl.ds(s, t, stride=n), :] for s in range(n)], axis=1)


def _merge_kernel(h_ref, yap_ref, yas_ref, ogp_ref, ogs_ref, gr_ref, gate_ref, valid_ref, ng_ref,
                  wa_ref, wg_ref, wo_ref, g1_ref, b1_ref, wrh_ref, wrl_ref, br_ref, u_ref,
                  h1t_ref, ids_ref, wts_ref, cnt_ref, run_ref, *, prompt_tiles):
    @pl.when(pl.program_id(0) == 0)
    def _():
        run_ref[...] = jnp.zeros_like(run_ref)

    h = h_ref[...]
    is_prompt = pl.program_id(0) < prompt_tiles
    og = jnp.where(is_prompt, ogp_ref[...], ogs_ref[...])
    ya = jnp.where(is_prompt, yap_ref[...], yas_ref[...])
    parts = []
    for hh in range(GLA_HEADS):
        o = og[:, hh * GLA_HV:(hh + 1) * GLA_HV]
        parts.append(o * lax.rsqrt(jnp.mean(o * o, -1, keepdims=True) + EPS))
    y_gla = jnp.concatenate(parts, 1) * ng_ref[...] * gr_ref[...].astype(F32)
    a = _dot(ya.astype(BF16), wa_ref[...])
    b = _dot(y_gla.astype(BF16), wg_ref[...])
    hm = gate_ref[:, :D_MODEL].astype(F32) * a + gate_ref[:, D_MODEL:].astype(F32) * b
    mix = _dot(hm.astype(BF16), wo_ref[...])
    h1 = _ln(DN_ALPHA * h + mix, g1_ref[...], b1_ref[...])
    _store_row_tiles(h1t_ref, h1)
    h_hi = h1.astype(BF16)
    h_lo = (h1 - h_hi.astype(F32)).astype(BF16)
    both = _dot(h_hi, jnp.concatenate([wrh_ref[...], wrl_ref[...]], 1))
    logits = both[:, :LANES] + both[:, LANES:] + _dot(h_lo, wrh_ref[...]) + br_ref[...]
    ids, wts_ref[...] = _route(logits.T, valid_ref[...] > 0.0)
    tm = ids.shape[1]
    e_row = lax.broadcasted_iota(jnp.int32, (N_EXPERTS, tm), 0)
    run = run_ref[:, 0:1]
    onehots = [(e_row == ids[kk:kk + 1]).astype(F32) for kk in range(2)]
    befores = _dot(jnp.concatenate(onehots, 0).astype(BF16), u_ref[...])
    ranks = []
    for kk in range(2):
        onehot = onehots[kk]
        before = befores[kk * N_EXPERTS:(kk + 1) * N_EXPERTS]
        ranks.append(jnp.sum(onehot * (run + before), 0, keepdims=True).astype(jnp.int32))
        run = run + jnp.sum(onehot, 1, keepdims=True)
    run_ref[...] = jnp.broadcast_to(run, run_ref.shape)
    o_row = lax.broadcasted_iota(jnp.int32, (SUBLANES, tm), 0)
    ids_ref[...] = jnp.where(o_row == 2, ranks[0], jnp.where(o_row == 3, ranks[1], ids))
    cnt_ref[...] = run_ref[...].astype(jnp.int32)


def _merge(h, ya_p, ya_s, og_p, og_s, gr, gate, valid, ng, wa, wg, wo, g1, b1, wrh, wrl, br):
    n = h.shape[0]
    tm = ROW_TILE
    u = jnp.asarray(np.triu(np.ones((tm, tm), np.float32), 1), BF16)
    pt = ya_p.shape[0] // tm
    st = ya_s.shape[0] // tm
    row = lambda w: pl.BlockSpec((tm, w), lambda i: (i, 0))
    row_p = lambda w: pl.BlockSpec((tm, w), lambda i: (jnp.minimum(i, pt - 1), 0))
    row_s = lambda w: pl.BlockSpec((tm, w), lambda i: (jnp.clip(i - pt, 0, st - 1), 0))
    lane = lambda r: pl.BlockSpec((r, tm), lambda i: (0, i))
    full = lambda a: pl.BlockSpec(a.shape, lambda i: (0,) * a.ndim)
    return pl.pallas_call(
        functools.partial(_merge_kernel, prompt_tiles=pt),
        grid=(n // tm,),
        in_specs=[row(D_MODEL), row_p(Q_W), row_s(Q_W), row_p(GLA_DV), row_s(GLA_DV), row(GLA_DV), row(2 * D_MODEL),
                  lane(1), full(ng), full(wa), full(wg), full(wo), full(g1), full(b1),
                  full(wrh), full(wrl), full(br), full(u)],
        out_specs=[pl.BlockSpec((tm * ROW_CHUNKS, LANES), lambda i: (i, 0)), lane(SUBLANES), lane(SUBLANES),
                   pl.BlockSpec((N_EXPERTS, LANES), lambda i: (0, 0))],
        out_shape=[jax.ShapeDtypeStruct((n * ROW_CHUNKS, LANES), F32),
                   jax.ShapeDtypeStruct((SUBLANES, n), jnp.int32), jax.ShapeDtypeStruct((SUBLANES, n), F32),
                   jax.ShapeDtypeStruct((N_EXPERTS, LANES), jnp.int32)],
        scratch_shapes=[pltpu.VMEM((N_EXPERTS, LANES), F32)],
        compiler_params=pltpu.CompilerParams(dimension_semantics=("arbitrary",), vmem_limit_bytes=VMEM_LIMIT),
        name="merge",
    )(h, ya_p, ya_s, og_p, og_s, gr, gate, valid, ng, wa, wg, wo, g1, b1, wrh, wrl, br, u)


def _wait_row_tiles(src_hbm, dst, sem, n):
    pltpu.make_async_copy(src_hbm.at[pl.ds(0, n * ROW_CHUNKS), :], dst, sem).wait()


def _tiles(ref, first, n=1):
    return ref.at[pl.ds(pl.multiple_of(first * ROW_CHUNKS, ROW_CHUNKS), n * ROW_CHUNKS), :]


def _dispatch_kernel(dest_ref, start_ref, cnt_ref, nt_ref, h_hbm, xs_hbm, buf, zbuf, sem_in, sem_out, sem_z,
                     *, ranges, n_rows, max_tiles):
    CH = DISPATCH_CHUNK
    T = FFN_TILE
    c = pl.program_id(0)
    n_chunks = pl.num_programs(0)

    def load(row0, slot):
        return pltpu.make_async_copy(_tiles(h_hbm, row0, CH), buf.at[slot], sem_in.at[slot])

    def scatter(row0, slot):
        def body(r, carry):
            for kk in range(2):
                d = dest_ref[kk * n_rows + row0 + r]
                pltpu.make_async_copy(_tiles(buf.at[slot], r), _tiles(xs_hbm, d), sem_out.at[slot]).start(priority=kk)
            return carry
        lax.fori_loop(0, CH, body, 0, unroll=8)

    def drain(slot):
        for _ in range(2):
            pltpu.make_async_copy(buf.at[slot], _tiles(xs_hbm, 0, CH), sem_out.at[slot]).wait()

    groups, per_group, first, stride, extra, extra_first = ranges

    def row0(j):
        in_group = first + (j // per_group) * stride + (j % per_group) * CH
        return jnp.where(j < groups * per_group, in_group, extra_first + (j - groups * per_group) * CH)

    @pl.when(c == 0)
    def _():
        for j in range(2):
            load(row0(j), j).start()

    slot = c % 3
    load(row0(c), slot).wait()
    scatter(row0(c), slot)

    @pl.when(c > 0)
    def _():
        drain((c + 2) % 3)

    @pl.when(c + 2 < n_chunks)
    def _():
        load(row0(c + 2), (c + 2) % 3).start()

    @pl.when(c == n_chunks - 1)
    def _():
        drain(slot)
        _zero_unowned_slots(start_ref, cnt_ref, nt_ref, xs_hbm, zbuf, sem_z, max_tiles)


def _zero_unowned_slots(start_ref, cnt_ref, nt_ref, xs_hbm, zbuf, sem_z, max_tiles):
    T = FFN_TILE
    zbuf[...] = jnp.zeros_like(zbuf)

    def tail_copies(e, wait):
        cnt = cnt_ref[e]
        n = (T - (cnt & (T - 1))) & (T - 1)
        first = start_ref[e] + cnt
        for bit in reversed(range(T.bit_length() - 1)):
            size = 1 << bit

            @pl.when((n & size) != 0)
            def _():
                cp = pltpu.make_async_copy(_tiles(zbuf, 0, size),
                                           _tiles(xs_hbm, first + ((n >> (bit + 1)) << (bit + 1)), size), sem_z)
                cp.wait() if wait else cp.start()

    def unused_tile(t, wait):
        cp = pltpu.make_async_copy(zbuf, _tiles(xs_hbm, t * T, T), sem_z)
        cp.wait() if wait else cp.start()

    for wait in (False, True):
        def per_expert(e, carry, wait=wait):
            tail_copies(e, wait)
            return carry

        def per_tile(t, carry, wait=wait):
            unused_tile(t, wait)
            return carry
        lax.fori_loop(0, N_EXPERTS, per_expert, 0)
        lax.fori_loop(nt_ref[0], max_tiles, per_tile, 0)


def _dispatch(dest, start, counts, n_tiles, h1t, ranges, max_tiles):
    T = FFN_TILE
    n_rows = h1t.shape[0] // ROW_CHUNKS
    return pl.pallas_call(
        functools.partial(_dispatch_kernel, ranges=ranges, n_rows=n_rows, max_tiles=max_tiles),
        grid_spec=pltpu.PrefetchScalarGridSpec(
            num_scalar_prefetch=4,
            grid=(ranges[0] * ranges[1] + ranges[4],),
            in_specs=[pl.BlockSpec(memory_space=pl.ANY)],
            out_specs=pl.BlockSpec(memory_space=pl.ANY),
            scratch_shapes=[pltpu.VMEM((3, DISPATCH_CHUNK * ROW_CHUNKS, LANES), F32),
                            pltpu.VMEM((T * ROW_CHUNKS, LANES), F32),
                            pltpu.SemaphoreType.DMA((3,)), pltpu.SemaphoreType.DMA((3,)), pltpu.SemaphoreType.DMA]),
        out_shape=jax.ShapeDtypeStruct((max_tiles * T * ROW_CHUNKS, LANES), F32),
        compiler_params=pltpu.CompilerParams(dimension_semantics=("arbitrary",)),
        name="dispatch",
    )(dest, start, counts, n_tiles, h1t)


def _ffn_kernel(te_ref, nxt_ref, nt_ref, x_ref, wg_hbm, wu_hbm, wd_hbm, out_ref, stage_g, stage_u, stage_d, sem,
                wgb, wub, wdb):
    i = pl.program_id(0)
    nt = nt_ref[0]
    T = FFN_TILE

    def stage(e):
        return [pltpu.make_async_copy(src.at[e], dst, sem.at[n])
                for n, (src, dst) in enumerate(((wg_hbm, stage_g), (wu_hbm, stage_u), (wd_hbm, stage_d)))]

    @pl.when(i == 0)
    def _():
        for cp in stage(te_ref[0]):
            cp.start()

    @pl.when(i < nt)
    def _():
        @pl.when((i == 0) | (te_ref[i] != te_ref[jnp.maximum(i - 1, 0)]))
        def _():
            for cp in stage(te_ref[i]):
                cp.wait()
            wgb[...] = stage_g[...].astype(BF16)
            wub[...] = stage_u[...].astype(BF16)
            wdb[...] = stage_d[...].astype(BF16)

            @pl.when(nxt_ref[i] >= 0)
            def _():
                for cp in stage(nxt_ref[i]):
                    cp.start()

        x = _load_row_tiles(x_ref, T).astype(BF16)
        g = _dot(x, wgb[...])
        u = _dot(x, wub[...])
        _store_row_tiles(out_ref, _dot((g * _sigmoid(g) * u).astype(BF16), wdb[...]))

    @pl.when(i >= nt)
    def _():
        out_ref[...] = jnp.zeros_like(out_ref)


def _ffn(tile_expert, next_expert, n_tiles, xs, w_g, w_u, w_d):
    T = FFN_TILE
    max_tiles = tile_expert.shape[0]
    hbm = pl.BlockSpec(memory_space=pl.ANY)
    tile = lambda imap: pl.BlockSpec((T * ROW_CHUNKS, LANES), imap)
    return pl.pallas_call(
        _ffn_kernel,
        grid_spec=pltpu.PrefetchScalarGridSpec(
            num_scalar_prefetch=3,
            grid=(max_tiles,),
            in_specs=[tile(lambda i, te, nxt, nt: (jnp.minimum(i, nt[0] - 1), 0)), hbm, hbm, hbm],
            out_specs=tile(lambda i, te, nxt, nt: (i, 0)),
            scratch_shapes=[pltpu.VMEM(w_g.shape[1:], F32), pltpu.VMEM(w_u.shape[1:], F32),
                            pltpu.VMEM(w_d.shape[1:], F32), pltpu.SemaphoreType.DMA((3,)),
                            pltpu.VMEM(w_g.shape[1:], BF16), pltpu.VMEM(w_u.shape[1:], BF16),
                            pltpu.VMEM(w_d.shape[1:], BF16)]),
        out_shape=jax.ShapeDtypeStruct(xs.shape, F32),
        compiler_params=pltpu.CompilerParams(dimension_semantics=("arbitrary",)),
        name="ffn",
    )(tile_expert, next_expert, n_tiles, xs, w_g, w_u, w_d)


def _combine_kernel(dest_ref, h_ref, w_ref, g_ref, b_ref, ys_hbm, y_ref, buf, sem, *, blk, n_rows):
    i = pl.program_id(0)
    n = pl.num_programs(0)
    T = CMB_TILE

    def gather(step, slot):
        base = blk(step) * T

        def body(r, carry):
            for kk in range(2):
                pltpu.make_async_copy(_tiles(ys_hbm, dest_ref[kk * n_rows + base + r]), _tiles(buf.at[slot, kk], r),
                                      sem.at[slot]).start(priority=kk)
            return carry
        lax.fori_loop(0, T, body, 0, unroll=8)

    @pl.when(i == 0)
    def _():
        gather(0, 0)

    @pl.when(i + 1 < n)
    def _():
        gather(i + 1, (i + 1) % 2)

    slot = i % 2
    for kk in range(2):
        _wait_row_tiles(ys_hbm, buf.at[slot, kk], sem.at[slot], T)
    w = w_ref[...]
    ff = w[:, 0:1] * _load_row_tiles(buf.at[slot, 0], T) + w[:, 1:2] * _load_row_tiles(buf.at[slot, 1], T)
    y_ref[...] = _ln(DN_ALPHA * _load_row_tiles(h_ref, T) + ff, g_ref[...], b_ref[...])


def _combine(dest, h1t, wts_t, ys, g2, b2, n_out, first_block, blocks_per_batch, skip_blocks):
    T = CMB_TILE
    n_rows = h1t.shape[0] // ROW_CHUNKS
    if skip_blocks:
        blk = lambda i: first_block + i + (i // blocks_per_batch + 1) * skip_blocks
    else:
        blk = lambda i: first_block + i
    full = lambda a: pl.BlockSpec(a.shape, lambda i, d: (0,) * a.ndim)
    return pl.pallas_call(
        functools.partial(_combine_kernel, blk=blk, n_rows=n_rows),
        grid_spec=pltpu.PrefetchScalarGridSpec(
            num_scalar_prefetch=1,
            grid=(n_out // T,),
            in_specs=[pl.BlockSpec((T * ROW_CHUNKS, LANES), lambda i, d: (blk(i), 0)),
                      pl.BlockSpec((T, 2), lambda i, d: (blk(i), 0)),
                      full(g2), full(b2), pl.BlockSpec(memory_space=pl.ANY)],
            out_specs=pl.BlockSpec((T, D_MODEL), lambda i, d: (i, 0)),
            scratch_shapes=[pltpu.VMEM((2, 2, T * ROW_CHUNKS, LANES), F32), pltpu.SemaphoreType.DMA((2,))]),
        out_shape=jax.ShapeDtypeStruct((n_out, D_MODEL), F32),
        compiler_params=pltpu.CompilerParams(dimension_semantics=("arbitrary",)),
        name="combine",
    )(dest, h1t, wts_t, g2, b2, ys)


def _dispatch_plan(routing, counts, max_tiles):
    T = FFN_TILE
    tiles_e = (counts + T - 1) // T
    tile_end = jnp.cumsum(tiles_e)
    n_tiles = tile_end[-1]
    start = (tile_end - tiles_e) * T
    ids, rank = routing[0:2], routing[2:4]
    onehot = (ids[..., None] == jnp.arange(N_EXPERTS, dtype=jnp.int32)).astype(jnp.int32)
    dest = (jnp.sum(onehot * start, axis=-1) + jnp.where(ids >= 0, rank, 0)).reshape(-1).astype(jnp.int32)
    experts = jnp.arange(N_EXPERTS, dtype=jnp.int32)
    tiles = jnp.arange(max_tiles, dtype=jnp.int32)
    te = jnp.sum((tiles[:, None] >= tile_end[None, :]).astype(jnp.int32), axis=1)
    te_last = jnp.max(jnp.where(counts > 0, experts, 0))
    te = jnp.where(tiles < n_tiles, te, te_last).astype(jnp.int32)
    later = jnp.where((counts > 0)[None, :] & (experts[None, :] > te[:, None]), experts[None, :], N_EXPERTS)
    nxt = jnp.min(later, axis=1)
    nxt = jnp.where(nxt < N_EXPERTS, nxt, -1).astype(jnp.int32)
    return te, nxt, n_tiles.reshape(1).astype(jnp.int32), start.astype(jnp.int32), dest


def kernel(x_prompt, x_sample, state_swa_k, state_swa_v, state_gla, meta_tokens, ln_emb_g, ln_emb_b, w_in, b_gate, attn_sink, w_alpha2, b_alpha, gla_norm_g, w_attn_br, w_gla_br, w_out, ln1_g, ln1_b, w_router_group, b_router_group, w_router_expert, b_router_expert, w_exp_gate, w_exp_up, w_exp_down, ln2_g, ln2_b):
    B, seq, _ = x_prompt.shape
    n_seq, t_s, _ = x_sample.shape
    depth = w_in.shape[0]
    assert depth == 1 and seq % ATT_BLOCK == 0 and t_s == 8 and SKIP_ROWS == ATT_BLOCK == WINDOW
    lp = SKIP_ROWS + seq
    NP, NS = B * lp, n_seq * t_s
    NR = NP + NS
    assert NP % ROW_TILE == 0 and NS % ROW_TILE == 0 and n_seq % SAMPLE_SEQS == 0
    assert seq % DISPATCH_CHUNK == 0 and NS % DISPATCH_CHUNK == 0
    l = 0
    row2 = lambda a: a.reshape(1, -1)

    head = jnp.concatenate([jnp.zeros((FRONT_PAD, D_MODEL), F32), meta_tokens], axis=0)
    pos = np.arange(NR)
    moe_valid = jnp.asarray(~((pos < NP) & (pos % lp < SKIP_ROWS)), F32).reshape(1, NR)

    wi = w_in[l]
    c_ga = sum((Q_W, KV_W, KV_W, GLA_DK, GLA_DK, GLA_DV))
    assert c_ga == W_IN_SPLIT
    w_pieces = (wi[:, :c_ga].astype(BF16), wi[:, c_ga + GLA_RANK:].astype(BF16),
                jnp.pad(wi[:, c_ga:c_ga + GLA_RANK], ((0, 0), (0, LANES - GLA_RANK))).astype(BF16))
    wa2_bf = jnp.concatenate([w_alpha2[l], jnp.zeros((LANES - GLA_RANK, GLA_DK), F32)], axis=0).astype(BF16)

    h, q, k, v, gq, gk, gv, la, gr, gate = _inproj(x_prompt, x_sample.reshape(NS, D_MODEL), head, row2(ln_emb_g),
                                                   row2(ln_emb_b), w_pieces, wa2_bf, row2(b_alpha[l]),
                                                   row2(b_gate[l]))

    sink = attn_sink[l]
    ya_p, k_p, v_p = _swa_prompt(sink, q, k, v, B, lp)
    buf_k = state_swa_k[l].reshape(n_seq, WINDOW, KV_W)
    buf_v = state_swa_v[l].reshape(n_seq, WINDOW, KV_W)
    ya_s, nk_s, nv_s = _swa_sample(sink, q, k, v, buf_k, buf_v, NP, t_s)

    og_p, s_p = _gla_prompt(gq, gk, gv, la, B, lp)
    og_s, s_s = _gla_sample(gq, gk, gv, la, state_gla[l], NP, t_s)

    wr = jnp.concatenate([w_router_expert[l], w_router_group[l],
                          jnp.zeros((D_MODEL, LANES - N_EXPERTS - N_GROUPS), F32)], axis=1)
    br = jnp.concatenate([b_router_expert[l], b_router_group[l],
                          jnp.zeros((LANES - N_EXPERTS - N_GROUPS,), F32)]).reshape(1, LANES)
    wr_hi = wr.astype(BF16)
    h1t, routing, wts, counts = _merge(h, ya_p, ya_s, og_p, og_s, gr, gate, moe_valid,
                                       row2(gla_norm_g[l]),
                                       w_attn_br[l].astype(BF16), w_gla_br[l].astype(BF16), w_out[l].astype(BF16),
                                       row2(ln1_g[l]), row2(ln1_b[l]), wr_hi, (wr - wr_hi.astype(F32)).astype(BF16), br)

    n_tok = B * seq + NS
    max_tiles = (2 * n_tok) // FFN_TILE + N_EXPERTS
    counts = counts[:, 0]
    te, nxt, n_tiles, start, dest = _dispatch_plan(routing[0:4], counts, max_tiles)
    routed = (B, seq // DISPATCH_CHUNK, SKIP_ROWS, lp, NS // DISPATCH_CHUNK, NP)
    xs = _dispatch(dest, start, counts, n_tiles, h1t, routed, max_tiles)
    ys = _ffn(te, nxt, n_tiles, xs, w_exp_gate[l], w_exp_up[l], w_exp_down[l])

    wts_t = wts[0:2].T
    g2, b2 = row2(ln2_g[l]), row2(ln2_b[l])
    skip_blocks = SKIP_ROWS // CMB_TILE
    y_p = _combine(dest, h1t, wts_t, ys, g2, b2, B * seq, 0, seq // CMB_TILE, skip_blocks)
    y_s = _combine(dest, h1t, wts_t, ys, g2, b2, NS, NP // CMB_TILE, 1, 0)

    kv_shape = (1, B, WINDOW, N_KV, HEAD_DIM)
    return (y_p.reshape(B, seq, D_MODEL), y_s.reshape(n_seq, t_s, D_MODEL),
            k_p.reshape(kv_shape), v_p.reshape(kv_shape), s_p[None],
            nk_s.reshape(1, n_seq, WINDOW, N_KV, HEAD_DIM), nv_s.reshape(1, n_seq, WINDOW, N_KV, HEAD_DIM),
            s_s[None])
```

```python
import functools

import numpy as np
import jax
import jax.numpy as jnp
from jax import lax
from jax.experimental import pallas as pl
from jax.experimental.pallas import tpu as pltpu

F32 = jnp.float32
BF16 = jnp.bfloat16

D_MODEL = 1024
N_META = 16
HEAD_DIM = 64
N_HEADS = 8
N_KV = 2
Q_PER_KV = 4
WINDOW = 128
ATT_BLOCK = 128
GLA_HEADS = 4
GLA_HK = 128
GLA_HV = 256
GLA_DK = GLA_HEADS * GLA_HK
GLA_DV = GLA_HEADS * GLA_HV
GLA_RANK = 16
GLA_TAU = 16.0
GLA_CHUNK = 64
N_GROUPS = 4
EXP_PER_GROUP = 8
N_EXPERTS = 32
D_EXPERT = 256
DN_ALPHA = 2.0 ** 0.25
EPS = 1e-5
NEG = -1e30

FRONT_PAD = (-N_META) % ATT_BLOCK
SKIP_ROWS = FRONT_PAD + N_META

Q_W, KV_W = N_HEADS * HEAD_DIM, N_KV * HEAD_DIM
SEG = {}
_o = 0
for _n, _w in (("q", Q_W), ("k", KV_W), ("v", KV_W), ("gq", GLA_DK), ("gk", GLA_DK), ("gv", GLA_DV)):
    SEG[_n] = (0, _o, _o + _w)
    _o += _w
W_IN_SPLIT = _o
SEG["gr"] = (1, 0, GLA_DV)
SEG["gate"] = (1, GLA_DV, GLA_DV + 2 * D_MODEL)
SEG["ga"] = (2, 0, 128)

ROW_TILE = 512
FFN_TILE = 256
CMB_TILE = 128
DISPATCH_CHUNK = 256
GLA_INTRA_CHUNKS = 6
SWA_BLOCKS = 1
INPROJ_TILE = 512
SAMPLE_SEQS = 8
LANES = 128
ROW_CHUNKS = D_MODEL // LANES
SUBLANES = 8
VMEM_LIMIT = 56 * 1024 * 1024


def _ln(x, g, b):
    mu = jnp.mean(x, -1, keepdims=True)
    xc = x - mu
    var = jnp.mean(xc * xc, -1, keepdims=True)
    return xc * lax.rsqrt(var + EPS) * g + b


def _sigmoid(x):
    return 0.5 * jnp.tanh(0.5 * x) + 0.5


def _dot(a, b):
    return jnp.dot(a, b, preferred_element_type=F32)


def _dot_nt(a, b):
    return lax.dot_general(a, b, (((1,), (1,)), ((), ())), preferred_element_type=F32)


def _inproj_kernel(*refs, blocks, batch_blocks, prompt_blocks):
    xp_refs, xs_refs = refs[:blocks], refs[blocks:2 * blocks]
    (head_ref, g_ref, b_ref, w0_ref, w1_ref, w2_ref, wa2_ref, ba_ref, bg_ref,
     h_ref, q_ref, k_ref, v_ref, gq_ref, gk_ref, gv_ref, la_ref, gr_ref, gate_ref) = refs[2 * blocks:]
    w_refs = (w0_ref, w1_ref, w2_ref)
    row = lax.broadcasted_iota(jnp.int32, (ATT_BLOCK, 1), 0)
    xs, keeps = [], []
    for s in range(blocks):
        p = pl.program_id(0) * blocks + s
        is_sample = p >= prompt_blocks
        is_head = jnp.logical_and(jnp.logical_not(is_sample), p % batch_blocks == 0)
        xs.append(jnp.where(is_sample, xs_refs[s][...], jnp.where(is_head, head_ref[...], xp_refs[s][...])))
        keeps.append(jnp.where(jnp.logical_and(is_head, row < FRONT_PAD), 0.0, 1.0))
    h = _ln(jnp.concatenate(xs, 0), g_ref[...], b_ref[...])
    h_ref[...] = h
    hb = h.astype(BF16)
    keep = jnp.concatenate(keeps, 0)

    def seg(name):
        piece, a, b = SEG[name]
        return _dot(hb, w_refs[piece][:, a:b])

    qkv = _dot(hb, w0_ref[:, SEG["q"][1]:SEG["v"][2]])
    q_ref[...] = qkv[:, :Q_W]
    k_ref[...] = qkv[:, Q_W:Q_W + KV_W]
    v_ref[...] = qkv[:, Q_W + KV_W:]
    gq_ref[...] = seg("gq") * (GLA_HK ** -0.5)
    gk_ref[...] = seg("gk") * keep
    gv_ref[...] = seg("gv") * keep
    gr = seg("gr")
    gr_ref[...] = (gr * _sigmoid(gr)).astype(BF16)
    gate_ref[...] = _sigmoid(seg("gate") + bg_ref[...]).astype(BF16)
    z = _dot(seg("ga").astype(BF16), wa2_ref[...]) + ba_ref[...]
    la = (jnp.minimum(z, 0.0) - jnp.log(1.0 + jnp.exp(-jnp.abs(z)))) * (1.0 / GLA_TAU)
    la_ref[...] = la * keep


def _inproj(x_prompt, x_sample, head, ln_g, ln_b, w_pieces, wa2_bf, b_alpha, b_gate):
    B, seq, _ = x_prompt.shape
    blk = ATT_BLOCK
    seq_blocks = seq // blk
    batch_blocks = seq_blocks + 1
    prompt_blocks = B * batch_blocks
    sample_blocks = x_sample.shape[0] // blk
    n = (prompt_blocks + sample_blocks) * blk
    tm = INPROJ_TILE
    blocks = tm // blk
    xp = x_prompt.reshape(B * seq, D_MODEL)

    def prompt_block(s):
        def imap(i):
            p = jnp.minimum(i * blocks + s, prompt_blocks - 1)
            return (p // batch_blocks * seq_blocks + jnp.maximum(p % batch_blocks - 1, 0), 0)
        return pl.BlockSpec((blk, D_MODEL), imap)

    def sample_block(s):
        return pl.BlockSpec((blk, D_MODEL), lambda i: (jnp.clip(i * blocks + s - prompt_blocks, 0, sample_blocks - 1), 0))

    widths = [D_MODEL, Q_W, KV_W, KV_W, GLA_DK, GLA_DK, GLA_DV, GLA_DK, GLA_DV, 2 * D_MODEL]
    row = lambda w: pl.BlockSpec((tm, w), lambda i: (i, 0))
    const = lambda a: pl.BlockSpec(a.shape, lambda i: (0,) * a.ndim, pipeline_mode=pl.Buffered(1))
    return pl.pallas_call(
        functools.partial(_inproj_kernel, blocks=blocks, batch_blocks=batch_blocks, prompt_blocks=prompt_blocks),
        grid=(n // tm,),
        in_specs=[prompt_block(s) for s in range(blocks)] + [sample_block(s) for s in range(blocks)]
                 + [const(a) for a in (head, ln_g, ln_b, *w_pieces, wa2_bf, b_alpha, b_gate)],
        out_specs=[row(w) for w in widths],
        out_shape=[jax.ShapeDtypeStruct((n, w), BF16 if i >= len(widths) - 2 else F32) for i, w in enumerate(widths)],
        compiler_params=pltpu.CompilerParams(dimension_semantics=("parallel",), vmem_limit_bytes=VMEM_LIMIT),
        name="inproj",
    )(*([xp] * blocks), *([x_sample] * blocks), head, ln_g, ln_b, *w_pieces, wa2_bf, b_alpha, b_gate)


def _softmax_pv(s, sink, vv):
    m = jnp.maximum(jnp.max(s, -1, keepdims=True), sink)
    p = jnp.exp(s - m)
    l = jnp.sum(p, -1, keepdims=True) + jnp.exp(sink - m)
    return _dot(p.astype(BF16), vv) / l


def _swa_prompt_kernel(sink_ref, bias_ref, q_ref, kp_ref, kc_ref, vp_ref, vc_ref, o_ref, kl_ref, vl_ref):
    nq = q_ref.shape[0] // ATT_BLOCK
    q_all = q_ref[...] * (HEAD_DIM ** -0.5)
    kb = jnp.concatenate([kp_ref[...], kc_ref[...]], 0)
    vb = jnp.concatenate([vp_ref[...], vc_ref[...]], 0)
    low = lax.broadcasted_iota(jnp.int32, (1, 2 * HEAD_DIM), 1) < HEAD_DIM
    k_low = jnp.where(low, kb, 0.0)
    k_high = jnp.where(low, 0.0, kb)
    keys = {(0, 0): k_low.astype(BF16), (0, 1): pltpu.roll(k_low, HEAD_DIM, 1).astype(BF16),
            (1, 0): pltpu.roll(k_high, HEAD_DIM, 1).astype(BF16), (1, 1): k_high.astype(BF16)}
    ones_col = (lax.broadcasted_iota(jnp.int32, (1, 2 * HEAD_DIM), 1) == HEAD_DIM).astype(F32)
    values = [jnp.where(low, vb, ones_col).astype(BF16),
              jnp.where(low, pltpu.roll(vb, HEAD_DIM, 1), ones_col).astype(BF16)]
    for blk in range(nq):
        rows = slice(blk * ATT_BLOCK, (blk + 1) * ATT_BLOCK)
        band = slice(blk * ATT_BLOCK, (blk + 2) * ATT_BLOCK)
        variant = jnp.minimum(pl.program_id(1) * nq + blk, 2)
        outs = {}
        for kv in range(N_KV):
            for half in range(2):
                heads = [kv * Q_PER_KV + half, kv * Q_PER_KV + half + 2]
                qms = [jnp.where(low if half == 0 else jnp.logical_not(low),
                                 q_all[rows, (h // 2) * 2 * HEAD_DIM:(h // 2 + 1) * 2 * HEAD_DIM], 0.0).astype(BF16)
                       for h in heads]
                s2 = _dot_nt(jnp.concatenate(qms, 0), keys[(kv, half)][band])
                for n, h in enumerate(heads):
                    s = s2[n * ATT_BLOCK:(n + 1) * ATT_BLOCK] + bias_ref[variant, h]
                    m = jnp.maximum(jnp.max(s, -1, keepdims=True), sink_ref[h])
                    pv = _dot(jnp.exp(s - m).astype(BF16), values[kv][band])
                    outs[h] = pv / (pv[:, HEAD_DIM:HEAD_DIM + 1] + jnp.exp(sink_ref[h] - m))
        for pair in range(N_HEADS // 2):
            o_ref[rows, pair * 2 * HEAD_DIM:(pair + 1) * 2 * HEAD_DIM] = jnp.where(
                low, outs[2 * pair], pltpu.roll(outs[2 * pair + 1], HEAD_DIM, 1))

    @pl.when(pl.program_id(1) == pl.num_programs(1) - 1)
    def _():
        kl_ref[0] = kc_ref[(nq - 1) * ATT_BLOCK:nq * ATT_BLOCK, :]
        vl_ref[0] = vc_ref[(nq - 1) * ATT_BLOCK:nq * ATT_BLOCK, :]


def _swa_bias_table():
    r = np.arange(ATT_BLOCK)[:, None]
    c = np.arange(2 * ATT_BLOCK)[None, :]
    dist = r - c + ATT_BLOCK
    slopes = 2.0 ** -(np.arange(N_HEADS) + 1.0)
    table = np.empty((3, N_HEADS, ATT_BLOCK, 2 * ATT_BLOCK), np.float32)
    for j in range(3):
        seen = (dist >= 0) & (dist < WINDOW) & ((j - 1) * ATT_BLOCK + c - FRONT_PAD >= 0)
        table[j] = np.where(seen[None], -slopes[:, None, None] * dist[None], NEG)
    return table


def _swa_prompt(sink, q, k, v, batch, lp):
    nq = SWA_BLOCKS
    nb = lp // ATT_BLOCK
    assert nb >= 3 and nb % nq == 0
    steps = nb // nq
    n = batch * lp
    bias = jnp.asarray(_swa_bias_table())
    cur = lambda w: pl.BlockSpec((nq * ATT_BLOCK, w), lambda b, j: (b * steps + j, 0))
    prev = lambda w: pl.BlockSpec((ATT_BLOCK, w), lambda b, j: (b * nb + jnp.maximum(j * nq - 1, 0), 0))
    last = pl.BlockSpec((1, ATT_BLOCK, KV_W), lambda b, j: (b, 0, 0))
    return pl.pallas_call(
        _swa_prompt_kernel,
        grid=(batch, steps),
        in_specs=[pl.BlockSpec(memory_space=pltpu.SMEM), pl.BlockSpec(bias.shape, lambda b, j: (0, 0, 0, 0)),
                  cur(Q_W), prev(KV_W), cur(KV_W), prev(KV_W), cur(KV_W)],
        out_specs=[cur(Q_W), last, last],
        out_shape=[jax.ShapeDtypeStruct((n, Q_W), F32), jax.ShapeDtypeStruct((batch, ATT_BLOCK, KV_W), F32),
                   jax.ShapeDtypeStruct((batch, ATT_BLOCK, KV_W), F32)],
        compiler_params=pltpu.CompilerParams(dimension_semantics=("parallel", "arbitrary")),
        name="swa_prompt",
    )(sink, bias, q, k, k, v, v)


def _swa_sample_kernel(sink_ref, q_ref, k_ref, v_ref, bk_ref, bv_ref, o_ref, nk_ref, nv_ref, *, t_s):
    nbuf = WINDOW
    span = 2 * WINDOW
    rows = Q_PER_KV * t_s
    r = lax.broadcasted_iota(jnp.int32, (rows, span), 0)
    c = lax.broadcasted_iota(jnp.int32, (rows, span), 1)
    t = r % t_s
    dist = t + nbuf - c
    mask = (dist >= 0) & (dist < WINDOW) & (c < nbuf + t_s)
    distf = dist.astype(F32)
    g_col = lax.broadcasted_iota(jnp.int32, (rows, 1), 0) // t_s
    fill = jnp.zeros((span - nbuf - t_s, KV_W), F32)

    def one_seq(s, carry):
        rs = pl.ds(pl.multiple_of(s * t_s, t_s), t_s)
        q = q_ref[rs, :]
        k_new = k_ref[rs, :]
        v_new = v_ref[rs, :]
        bk = bk_ref[s]
        bv = bv_ref[s]
        k_all = jnp.concatenate([bk, k_new, fill], 0)
        v_all = jnp.concatenate([bv, v_new, fill], 0)
        for kv in range(N_KV):
            qg = jnp.concatenate(
                [q[:, (kv * Q_PER_KV + g) * HEAD_DIM:(kv * Q_PER_KV + g + 1) * HEAD_DIM] for g in range(Q_PER_KV)], 0)
            kk = k_all[:, kv * HEAD_DIM:(kv + 1) * HEAD_DIM].astype(BF16)
            vv = v_all[:, kv * HEAD_DIM:(kv + 1) * HEAD_DIM].astype(BF16)
            slope = jnp.zeros((rows, 1), F32)
            sink = jnp.zeros((rows, 1), F32)
            for g in range(Q_PER_KV):
                h = kv * Q_PER_KV + g
                slope = jnp.where(g_col == g, 2.0 ** -(h + 1), slope)
                sink = jnp.where(g_col == g, sink_ref[h], sink)
            sc = _dot_nt(qg.astype(BF16), kk) * (HEAD_DIM ** -0.5) - slope * distf
            sc = jnp.where(mask, sc, NEG)
            o = _softmax_pv(sc, sink, vv)
            for g in range(Q_PER_KV):
                h = kv * Q_PER_KV + g
                o_ref[rs, h * HEAD_DIM:(h + 1) * HEAD_DIM] = o[g * t_s:(g + 1) * t_s]
        nk_ref[s, 0:nbuf - t_s, :] = bk[t_s:, :]
        nk_ref[s, nbuf - t_s:nbuf, :] = k_new
        nv_ref[s, 0:nbuf - t_s, :] = bv[t_s:, :]
        nv_ref[s, nbuf - t_s:nbuf, :] = v_new
        return carry

    lax.fori_loop(0, SAMPLE_SEQS, one_seq, 0, unroll=True)


def _swa_sample(sink, q, k, v, buf_k, buf_v, row0, t_s):
    n_seq = buf_k.shape[0]
    sb = SAMPLE_SEQS
    rb = sb * t_s
    b0 = row0 // rb
    rows = lambda w: pl.BlockSpec((rb, w), lambda i: (b0 + i, 0))
    bufs = pl.BlockSpec((sb, WINDOW, KV_W), lambda i: (i, 0, 0))
    return pl.pallas_call(
        functools.partial(_swa_sample_kernel, t_s=t_s),
        grid=(n_seq // sb,),
        in_specs=[pl.BlockSpec(memory_space=pltpu.SMEM), rows(Q_W), rows(KV_W), rows(KV_W), bufs, bufs],
        out_specs=[pl.BlockSpec((rb, Q_W), lambda i: (i, 0)), bufs, bufs],
        out_shape=[jax.ShapeDtypeStruct((n_seq * t_s, Q_W), F32),
                   jax.ShapeDtypeStruct(buf_k.shape, F32), jax.ShapeDtypeStruct(buf_v.shape, F32)],
        compiler_params=pltpu.CompilerParams(dimension_semantics=("parallel",)),
        name="swa_sample",
    )(sink, q, k, v, buf_k, buf_v)


def _gla_tables(chunk):
    t = np.arange(chunk)[:, None]
    u = np.arange(chunk)[None, :]
    masks = []
    w = chunk // 2
    while w >= 1:
        masks.append((t // (2 * w) == u // (2 * w)) & ((t // w) % 2 == 1) & ((u // w) % 2 == 0))
        w //= 2
    return (u <= t).astype(np.float32), np.stack(masks, 0).astype(np.float32)


def _level_exponents(b, la, w):
    C = b.shape[0]
    row = lax.broadcasted_iota(jnp.int32, b.shape, 0)
    if w >= 4:
        pieces = [jnp.broadcast_to(b[p + w - 1:p + w], (2 * w, b.shape[1])) for p in range(0, C, 2 * w)]
        ref = pieces[0] if len(pieces) == 1 else jnp.concatenate(pieces, 0)
        return jnp.where((row & w) != 0, b - ref, ref - b)
    if w == 2:
        m = row & 3
        nxt = pltpu.roll(la, C - 1, 0)
        prv = pltpu.roll(la, 1, 0)
        return jnp.where(m == 2, la, jnp.where(m == 3, la + prv, jnp.where(m == 0, nxt, 0.0)))
    return jnp.where((row & 1) != 0, la, 0.0)


def _split3(x):
    hi = x.astype(BF16)
    r1 = x - hi.astype(F32)
    mid = r1.astype(BF16)
    lo = (r1 - mid.astype(F32)).astype(BF16)
    return hi, mid, lo


def _gla_intra_kernel(g_ref, m_ref, q_ref, k_ref, v_ref, la_ref, o_ref, qe_ref, ke_ref, vt_ref, d_ref):
    C = GLA_CHUNK
    n_lvl = m_ref.shape[0]
    G = g_ref[...]
    eye = (lax.broadcasted_iota(jnp.int32, (C, C), 0) == lax.broadcasted_iota(jnp.int32, (C, C), 1)).astype(F32)
    for j in range(q_ref.shape[0] // C):
        rs = slice(j * C, (j + 1) * C)
        la = la_ref[rs, :]
        hi, mid, lo = _split3(la)
        b = _dot(G, hi) + _dot(G, mid) + _dot(G, lo)
        b_last = b[C - 1:C]
        q_all = q_ref[rs, :]
        k_all = k_ref[rs, :]
        qe_ref[0, rs, :] = (q_all * jnp.exp(b)).astype(BF16)
        ke_ref[0, rs, :] = (k_all * jnp.exp(b_last - b)).astype(BF16)
        d_ref[0, j] = jnp.broadcast_to(jnp.exp(b_last), (SUBLANES, GLA_DK))
        q_lvl, k_lvl = [], []
        for l in range(n_lvl):
            El = jnp.exp(_level_exponents(b, la, C >> (l + 1)))
            q_lvl.append((q_all * El).astype(BF16))
            k_lvl.append((k_all * El).astype(BF16))
        for h in range(GLA_HEADS):
            ks = slice(h * GLA_HK, (h + 1) * GLA_HK)
            vs = slice(h * GLA_HV, (h + 1) * GLA_HV)
            v = v_ref[rs, vs]
            att = eye * jnp.sum(q_all[:, ks] * k_all[:, ks], -1, keepdims=True)
            for l in range(n_lvl):
                att = att + m_ref[l] * _dot_nt(q_lvl[l][:, ks], k_lvl[l][:, ks])
            o_ref[0, rs, vs] = _dot(att.astype(BF16), v.astype(BF16))
            vt_ref[0, j, vs, :] = v.T.astype(BF16)


def _gla_inter_kernel(o_ref, qe_ref, ke_ref, vt_ref, d_ref, og_ref, s_ref, st_ref):
    c = pl.program_id(0)
    batch = o_ref.shape[0]

    @pl.when(c == 0)
    def _():
        st_ref[...] = jnp.zeros_like(st_ref)

    for b in range(batch):
        for h in range(GLA_HEADS):
            ks = slice(h * GLA_HK, (h + 1) * GLA_HK)
            vs = slice(h * GLA_HV, (h + 1) * GLA_HV)
            st = st_ref[b * GLA_HEADS + h]
            og_ref[b, :, vs] = o_ref[b, :, vs] + _dot_nt(qe_ref[b, :, ks], st.astype(BF16))
            st_ref[b * GLA_HEADS + h] = st * d_ref[b, 0, 0:1, ks] + _dot(vt_ref[b, 0, vs, :], ke_ref[b, :, ks])

    @pl.when(c == pl.num_programs(0) - 1)
    def _():
        for b in range(batch):
            for h in range(GLA_HEADS):
                s_ref[b, h] = st_ref[b * GLA_HEADS + h].T


def _gla_prompt(gq, gk, gv, la, batch, lp):
    C = GLA_CHUNK
    nc = lp // C
    G, M = _gla_tables(C)
    G = jnp.asarray(G, BF16)
    M = jnp.asarray(M, F32)
    cps = GLA_INTRA_CHUNKS
    assert nc % cps == 0
    rows = lambda w: pl.BlockSpec((cps * C, w), lambda b, c: (b * (nc // cps) + c, 0))
    rows3 = lambda w: pl.BlockSpec((1, cps * C, w), lambda b, c: (b, c, 0))
    full = lambda a: pl.BlockSpec(a.shape, lambda b, c: (0,) * a.ndim)
    o_intra, qe, ke, vt, d = pl.pallas_call(
        _gla_intra_kernel,
        grid=(batch, nc // cps),
        in_specs=[full(G), full(M), rows(GLA_DK), rows(GLA_DK), rows(GLA_DV), rows(GLA_DK)],
        out_specs=[rows3(GLA_DV), rows3(GLA_DK), rows3(GLA_DK),
                   pl.BlockSpec((1, cps, GLA_DV, C), lambda b, c: (b, c, 0, 0)),
                   pl.BlockSpec((1, cps, SUBLANES, GLA_DK), lambda b, c: (b, c, 0, 0))],
        out_shape=[jax.ShapeDtypeStruct((batch, lp, GLA_DV), F32),
                   jax.ShapeDtypeStruct((batch, lp, GLA_DK), BF16), jax.ShapeDtypeStruct((batch, lp, GLA_DK), BF16),
                   jax.ShapeDtypeStruct((batch, nc, GLA_DV, C), BF16),
                   jax.ShapeDtypeStruct((batch, nc, SUBLANES, GLA_DK), F32)],
        compiler_params=pltpu.CompilerParams(dimension_semantics=("parallel", "parallel")),
        name="gla_intra",
    )(G, M, gq, gk, gv, la)
    chunk = lambda w: pl.BlockSpec((batch, C, w), lambda c: (0, c, 0))
    og, s_fin = pl.pallas_call(
        _gla_inter_kernel,
        grid=(nc,),
        in_specs=[chunk(GLA_DV), chunk(GLA_DK), chunk(GLA_DK),
                  pl.BlockSpec((batch, 1, GLA_DV, C), lambda c: (0, c, 0, 0)),
                  pl.BlockSpec((batch, 1, SUBLANES, GLA_DK), lambda c: (0, c, 0, 0))],
        out_specs=[chunk(GLA_DV), pl.BlockSpec((batch, GLA_HEADS, GLA_HK, GLA_HV), lambda c: (0, 0, 0, 0))],
        out_shape=[jax.ShapeDtypeStruct((batch, lp, GLA_DV), F32),
                   jax.ShapeDtypeStruct((batch, GLA_HEADS, GLA_HK, GLA_HV), F32)],
        scratch_shapes=[pltpu.VMEM((batch * GLA_HEADS, GLA_HV, GLA_HK), F32)],
        compiler_params=pltpu.CompilerParams(dimension_semantics=("arbitrary",)),
        name="gla_inter",
    )(o_intra, qe, ke, vt, d)
    return og.reshape(batch * lp, GLA_DV), s_fin


def _gla_sample_kernel(q_ref, k_ref, v_ref, la_ref, s0_ref, o_ref, s_ref, *, t_s):
    T = t_s
    row = lax.broadcasted_iota(jnp.int32, (T, GLA_HK), 0)
    k_fill = jnp.zeros((GLA_HK - T - SUBLANES, GLA_HK), F32)
    v_fill = jnp.zeros((GLA_HK - T, GLA_HV), F32)

    def one_seq(s, carry):
        rs = pl.ds(pl.multiple_of(s * T, T), T)
        for h in range(GLA_HEADS):
            ks = slice(h * GLA_HK, (h + 1) * GLA_HK)
            vs = slice(h * GLA_HV, (h + 1) * GLA_HV)
            q = q_ref[rs, ks]
            k = k_ref[rs, ks]
            v = v_ref[rs, vs]
            b = la_ref[rs, ks]
            sh = 1
            while sh < T:
                b = b + jnp.where(row >= sh, pltpu.roll(b, sh, 0), 0.0)
                sh *= 2
            S = s0_ref[s, h]
            o = _dot((q * jnp.exp(b)).astype(BF16), S.astype(BF16))
            for j in range(T):
                e = jnp.exp(jnp.where(row >= j, b - b[j:j + 1], NEG))
                a_col = jnp.sum(q * k[j:j + 1] * e, -1, keepdims=True)
                o = o + a_col * v[j:j + 1]
            o_ref[rs, vs] = o
            b_last = b[T - 1:T]
            ke = k * jnp.exp(b_last - b)
            kt = jnp.concatenate([ke, jnp.broadcast_to(jnp.exp(b_last), (SUBLANES, GLA_HK)), k_fill], 0).T
            v_pad = jnp.concatenate([v, v_fill], 0)
            s_ref[s, h] = S * kt[:, T:T + 1] + _dot(kt.astype(BF16), v_pad.astype(BF16))
        return carry

    lax.fori_loop(0, SAMPLE_SEQS, one_seq, 0)


def _gla_sample(gq, gk, gv, la, s0, row0, t_s):
    n_seq = s0.shape[0]
    sb = SAMPLE_SEQS
    rb = sb * t_s
    b0 = row0 // rb
    rows = lambda w: pl.BlockSpec((rb, w), lambda i: (b0 + i, 0))
    st = pl.BlockSpec((sb, GLA_HEADS, GLA_HK, GLA_HV), lambda i: (i, 0, 0, 0))
    return pl.pallas_call(
        functools.partial(_gla_sample_kernel, t_s=t_s),
        grid=(n_seq // sb,),
        in_specs=[rows(GLA_DK), rows(GLA_DK), rows(GLA_DV), rows(GLA_DK), st],
        out_specs=[pl.BlockSpec((rb, GLA_DV), lambda i: (i, 0)), st],
        out_shape=[jax.ShapeDtypeStruct((n_seq * t_s, GLA_DV), F32), jax.ShapeDtypeStruct(s0.shape, F32)],
        compiler_params=pltpu.CompilerParams(dimension_semantics=("parallel",), vmem_limit_bytes=VMEM_LIMIT),
        name="gla_sample",
    )(gq, gk, gv, la, s0)


def _route(lt, valid):
    tm = lt.shape[1]
    el = lt[0:N_EXPERTS]
    gl = lt[N_EXPERTS:N_EXPERTS + N_GROUPS]
    g_max = jnp.max(gl, 0, keepdims=True)
    g_row = lax.broadcasted_iota(jnp.int32, (N_GROUPS, tm), 0)
    g_idx = jnp.min(jnp.where(gl == g_max, g_row, N_GROUPS), 0, keepdims=True)
    p_max = 1.0 / jnp.sum(jnp.exp(gl - g_max), 0, keepdims=True)
    e_row = lax.broadcasted_iota(jnp.int32, (N_EXPERTS, tm), 0)
    m1 = jnp.where(e_row // EXP_PER_GROUP == g_idx, el, -jnp.inf)
    v1 = jnp.max(m1, 0, keepdims=True)
    i1 = jnp.min(jnp.where(m1 == v1, e_row, N_EXPERTS), 0, keepdims=True)
    m2 = jnp.where(e_row == i1, -jnp.inf, m1)
    v2 = jnp.max(m2, 0, keepdims=True)
    i2 = jnp.min(jnp.where(m2 == v2, e_row, N_EXPERTS), 0, keepdims=True)
    e2 = jnp.exp(v2 - v1)
    w1 = p_max / (1.0 + e2)
    w2 = p_max * e2 / (1.0 + e2)
    o_row = lax.broadcasted_iota(jnp.int32, (SUBLANES, tm), 0)
    ids = jnp.where(o_row == 0, i1, jnp.where(o_row == 1, i2, -1))
    return jnp.where(valid, ids, -1), jnp.where(o_row == 0, w1, jnp.where(o_row == 1, w2, 0.0))


def _store_row_tiles(ref, x):
    t, d = x.shape
    n = d // LANES
    for s in range(n):
        ref[pl.ds(s, t, stride=n), :] = x[:, s * LANES:(s + 1) * LANES]


def _load_row_tiles(ref, t, n=ROW_CHUNKS):
    return jnp.concatenate([ref[pl.ds(s, t, stride=n), :] for s in range(n)], axis=1)


def _merge_kernel(h_ref, yap_ref, yas_ref, ogp_ref, ogs_ref, gr_ref, gate_ref, valid_ref, ng_ref,
                  wa_ref, wg_ref, wo_ref, g1_ref, b1_ref, wrh_ref, wrl_ref, br_ref, u_ref,
                  h1t_ref, ids_ref, wts_ref, cnt_ref, run_ref, *, prompt_tiles):
    @pl.when(pl.program_id(0) == 0)
    def _():
        run_ref[...] = jnp.zeros_like(run_ref)

    h = h_ref[...]
    is_prompt = pl.program_id(0) < prompt_tiles
    og = jnp.where(is_prompt, ogp_ref[...], ogs_ref[...])
    ya = jnp.where(is_prompt, yap_ref[...], yas_ref[...])
    parts = []
    for hh in range(GLA_HEADS):
        o = og[:, hh * GLA_HV:(hh + 1) * GLA_HV]
        parts.append(o * lax.rsqrt(jnp.mean(o * o, -1, keepdims=True) + EPS))
    y_gla = jnp.concatenate(parts, 1) * ng_ref[...] * gr_ref[...].astype(F32)
    a = _dot(ya.astype(BF16), wa_ref[...])
    b = _dot(y_gla.astype(BF16), wg_ref[...])
    hm = gate_ref[:, :D_MODEL].astype(F32) * a + gate_ref[:, D_MODEL:].astype(F32) * b
    mix = _dot(hm.astype(BF16), wo_ref[...])
    h1 = _ln(DN_ALPHA * h + mix, g1_ref[...], b1_ref[...])
    _store_row_tiles(h1t_ref, h1)
    h_hi = h1.astype(BF16)
    h_lo = (h1 - h_hi.astype(F32)).astype(BF16)
    both = _dot(h_hi, jnp.concatenate([wrh_ref[...], wrl_ref[...]], 1))
    logits = both[:, :LANES] + both[:, LANES:] + _dot(h_lo, wrh_ref[...]) + br_ref[...]
    ids, wts_ref[...] = _route(logits.T, valid_ref[...] > 0.0)
    tm = ids.shape[1]
    e_row = lax.broadcasted_iota(jnp.int32, (N_EXPERTS, tm), 0)
    run = run_ref[:, 0:1]
    onehots = [(e_row == ids[kk:kk + 1]).astype(F32) for kk in range(2)]
    befores = _dot(jnp.concatenate(onehots, 0).astype(BF16), u_ref[...])
    ranks = []
    for kk in range(2):
        onehot = onehots[kk]
        before = befores[kk * N_EXPERTS:(kk + 1) * N_EXPERTS]
        ranks.append(jnp.sum(onehot * (run + before), 0, keepdims=True).astype(jnp.int32))
        run = run + jnp.sum(onehot, 1, keepdims=True)
    run_ref[...] = jnp.broadcast_to(run, run_ref.shape)
    o_row = lax.broadcasted_iota(jnp.int32, (SUBLANES, tm), 0)
    ids_ref[...] = jnp.where(o_row == 2, ranks[0], jnp.where(o_row == 3, ranks[1], ids))
    cnt_ref[...] = run_ref[...].astype(jnp.int32)


def _merge(h, ya_p, ya_s, og_p, og_s, gr, gate, valid, ng, wa, wg, wo, g1, b1, wrh, wrl, br):
    n = h.shape[0]
    tm = ROW_TILE
    u = jnp.asarray(np.triu(np.ones((tm, tm), np.float32), 1), BF16)
    pt = ya_p.shape[0] // tm
    st = ya_s.shape[0] // tm
    row = lambda w: pl.BlockSpec((tm, w), lambda i: (i, 0))
    row_p = lambda w: pl.BlockSpec((tm, w), lambda i: (jnp.minimum(i, pt - 1), 0))
    row_s = lambda w: pl.BlockSpec((tm, w), lambda i: (jnp.clip(i - pt, 0, st - 1), 0))
    lane = lambda r: pl.BlockSpec((r, tm), lambda i: (0, i))
    full = lambda a: pl.BlockSpec(a.shape, lambda i: (0,) * a.ndim)
    return pl.pallas_call(
        functools.partial(_merge_kernel, prompt_tiles=pt),
        grid=(n // tm,),
        in_specs=[row(D_MODEL), row_p(Q_W), row_s(Q_W), row_p(GLA_DV), row_s(GLA_DV), row(GLA_DV), row(2 * D_MODEL),
                  lane(1), full(ng), full(wa), full(wg), full(wo), full(g1), full(b1),
                  full(wrh), full(wrl), full(br), full(u)],
        out_specs=[pl.BlockSpec((tm * ROW_CHUNKS, LANES), lambda i: (i, 0)), lane(SUBLANES), lane(SUBLANES),
                   pl.BlockSpec((N_EXPERTS, LANES), lambda i: (0, 0))],
        out_shape=[jax.ShapeDtypeStruct((n * ROW_CHUNKS, LANES), F32),
                   jax.ShapeDtypeStruct((SUBLANES, n), jnp.int32), jax.ShapeDtypeStruct((SUBLANES, n), F32),
                   jax.ShapeDtypeStruct((N_EXPERTS, LANES), jnp.int32)],
        scratch_shapes=[pltpu.VMEM((N_EXPERTS, LANES), F32)],
        compiler_params=pltpu.CompilerParams(dimension_semantics=("arbitrary",), vmem_limit_bytes=VMEM_LIMIT),
        name="merge",
    )(h, ya_p, ya_s, og_p, og_s, gr, gate, valid, ng, wa, wg, wo, g1, b1, wrh, wrl, br, u)


def _wait_row_tiles(src_hbm, dst, sem, n):
    pltpu.make_async_copy(src_hbm.at[pl.ds(0, n * ROW_CHUNKS), :], dst, sem).wait()


def _tiles(ref, first, n=1):
    return ref.at[pl.ds(pl.multiple_of(first * ROW_CHUNKS, ROW_CHUNKS), n * ROW_CHUNKS), :]


def _dispatch_kernel(dest_ref, start_ref, cnt_ref, nt_ref, h_hbm, xs_hbm, buf, zbuf, sem_in, sem_out, sem_z,
                     *, ranges, n_rows, max_tiles):
    CH = DISPATCH_CHUNK
    T = FFN_TILE
    c = pl.program_id(0)
    n_chunks = pl.num_programs(0)

    def load(row0, slot):
        return pltpu.make_async_copy(_tiles(h_hbm, row0, CH), buf.at[slot], sem_in.at[slot])

    def scatter(row0, slot):
        def body(r, carry):
            for kk in range(2):
                d = dest_ref[kk * n_rows + row0 + r]
                pltpu.make_async_copy(_tiles(buf.at[slot], r), _tiles(xs_hbm, d), sem_out.at[slot]).start(priority=kk)
            return carry
        lax.fori_loop(0, CH, body, 0, unroll=8)

    def drain(slot):
        for _ in range(2):
            pltpu.make_async_copy(buf.at[slot], _tiles(xs_hbm, 0, CH), sem_out.at[slot]).wait()

    groups, per_group, first, stride, extra, extra_first = ranges

    def row0(j):
        in_group = first + (j // per_group) * stride + (j % per_group) * CH
        return jnp.where(j < groups * per_group, in_group, extra_first + (j - groups * per_group) * CH)

    @pl.when(c == 0)
    def _():
        for j in range(2):
            load(row0(j), j).start()

    slot = c % 3
    load(row0(c), slot).wait()
    scatter(row0(c), slot)

    @pl.when(c > 0)
    def _():
        drain((c + 2) % 3)

    @pl.when(c + 2 < n_chunks)
    def _():
        load(row0(c + 2), (c + 2) % 3).start()

    @pl.when(c == n_chunks - 1)
    def _():
        drain(slot)
        _zero_unowned_slots(start_ref, cnt_ref, nt_ref, xs_hbm, zbuf, sem_z, max_tiles)


def _zero_unowned_slots(start_ref, cnt_ref, nt_ref, xs_hbm, zbuf, sem_z, max_tiles):
    T = FFN_TILE
    zbuf[...] = jnp.zeros_like(zbuf)

    def tail_copies(e, wait):
        cnt = cnt_ref[e]
        n = (T - (cnt & (T - 1))) & (T - 1)
        first = start_ref[e] + cnt
        for bit in reversed(range(T.bit_length() - 1)):
            size = 1 << bit

            @pl.when((n & size) != 0)
            def _():
                cp = pltpu.make_async_copy(_tiles(zbuf, 0, size),
                                           _tiles(xs_hbm, first + ((n >> (bit + 1)) << (bit + 1)), size), sem_z)
                cp.wait() if wait else cp.start()

    def unused_tile(t, wait):
        cp = pltpu.make_async_copy(zbuf, _tiles(xs_hbm, t * T, T), sem_z)
        cp.wait() if wait else cp.start()

    for wait in (False, True):
        def per_expert(e, carry, wait=wait):
            tail_copies(e, wait)
            return carry

        def per_tile(t, carry, wait=wait):
            unused_tile(t, wait)
            return carry
        lax.fori_loop(0, N_EXPERTS, per_expert, 0)
        lax.fori_loop(nt_ref[0], max_tiles, per_tile, 0)


def _dispatch(dest, start, counts, n_tiles, h1t, ranges, max_tiles):
    T = FFN_TILE
    n_rows = h1t.shape[0] // ROW_CHUNKS
    return pl.pallas_call(
        functools.partial(_dispatch_kernel, ranges=ranges, n_rows=n_rows, max_tiles=max_tiles),
        grid_spec=pltpu.PrefetchScalarGridSpec(
            num_scalar_prefetch=4,
            grid=(ranges[0] * ranges[1] + ranges[4],),
            in_specs=[pl.BlockSpec(memory_space=pl.ANY)],
            out_specs=pl.BlockSpec(memory_space=pl.ANY),
            scratch_shapes=[pltpu.VMEM((3, DISPATCH_CHUNK * ROW_CHUNKS, LANES), F32),
                            pltpu.VMEM((T * ROW_CHUNKS, LANES), F32),
                            pltpu.SemaphoreType.DMA((3,)), pltpu.SemaphoreType.DMA((3,)), pltpu.SemaphoreType.DMA]),
        out_shape=jax.ShapeDtypeStruct((max_tiles * T * ROW_CHUNKS, LANES), F32),
        compiler_params=pltpu.CompilerParams(dimension_semantics=("arbitrary",)),
        name="dispatch",
    )(dest, start, counts, n_tiles, h1t)


def _ffn_kernel(te_ref, nxt_ref, nt_ref, x_ref, wg_hbm, wu_hbm, wd_hbm, out_ref, stage_g, stage_u, stage_d, sem,
                wgb, wub, wdb):
    i = pl.program_id(0)
    nt = nt_ref[0]
    T = FFN_TILE

    def stage(e):
        return [pltpu.make_async_copy(src.at[e], dst, sem.at[n])
                for n, (src, dst) in enumerate(((wg_hbm, stage_g), (wu_hbm, stage_u), (wd_hbm, stage_d)))]

    @pl.when(i == 0)
    def _():
        for cp in stage(te_ref[0]):
            cp.start()

    @pl.when(i < nt)
    def _():
        @pl.when((i == 0) | (te_ref[i] != te_ref[jnp.maximum(i - 1, 0)]))
        def _():
            for cp in stage(te_ref[i]):
                cp.wait()
            wgb[...] = stage_g[...].astype(BF16)
            wub[...] = stage_u[...].astype(BF16)
            wdb[...] = stage_d[...].astype(BF16)

            @pl.when(nxt_ref[i] >= 0)
            def _():
                for cp in stage(nxt_ref[i]):
                    cp.start()

        x = _load_row_tiles(x_ref, T).astype(BF16)
        g = _dot(x, wgb[...])
        u = _dot(x, wub[...])
        _store_row_tiles(out_ref, _dot((g * _sigmoid(g) * u).astype(BF16), wdb[...]))

    @pl.when(i >= nt)
    def _():
        out_ref[...] = jnp.zeros_like(out_ref)


def _ffn(tile_expert, next_expert, n_tiles, xs, w_g, w_u, w_d):
    T = FFN_TILE
    max_tiles = tile_expert.shape[0]
    hbm = pl.BlockSpec(memory_space=pl.ANY)
    tile = lambda imap: pl.BlockSpec((T * ROW_CHUNKS, LANES), imap)
    return pl.pallas_call(
        _ffn_kernel,
        grid_spec=pltpu.PrefetchScalarGridSpec(
            num_scalar_prefetch=3,
            grid=(max_tiles,),
            in_specs=[tile(lambda i, te, nxt, nt: (jnp.minimum(i, nt[0] - 1), 0)), hbm, hbm, hbm],
            out_specs=tile(lambda i, te, nxt, nt: (i, 0)),
            scratch_shapes=[pltpu.VMEM(w_g.shape[1:], F32), pltpu.VMEM(w_u.shape[1:], F32),
                            pltpu.VMEM(w_d.shape[1:], F32), pltpu.SemaphoreType.DMA((3,)),
                            pltpu.VMEM(w_g.shape[1:], BF16), pltpu.VMEM(w_u.shape[1:], BF16),
                            pltpu.VMEM(w_d.shape[1:], BF16)]),
        out_shape=jax.ShapeDtypeStruct(xs.shape, F32),
        compiler_params=pltpu.CompilerParams(dimension_semantics=("arbitrary",)),
        name="ffn",
    )(tile_expert, next_expert, n_tiles, xs, w_g, w_u, w_d)


def _combine_kernel(dest_ref, h_ref, w_ref, g_ref, b_ref, ys_hbm, y_ref, buf, sem, *, blk, n_rows):
    i = pl.program_id(0)
    n = pl.num_programs(0)
    T = CMB_TILE

    def gather(step, slot, unroll):
        base = blk(step) * T

        def body(r, carry):
            for kk in range(2):
                pltpu.make_async_copy(_tiles(ys_hbm, dest_ref[kk * n_rows + base + r]), _tiles(buf.at[slot, kk], r),
                                      sem.at[slot]).start(priority=kk)
            return carry
        lax.fori_loop(0, T, body, 0, unroll=unroll)

    def drain(slot):
        for kk in range(2):
            _wait_row_tiles(ys_hbm, buf.at[slot, kk], sem.at[slot], T)

    @pl.when(i == 0)
    def _():
        gather(0, 0, 8)
        gather(1, 1, 8)

    slot = i % 3
    drain(slot)
    gather(jnp.minimum(i + 2, n - 1), (i + 2) % 3, True)
    w = w_ref[...]
    ff = w[:, 0:1] * _load_row_tiles(buf.at[slot, 0], T) + w[:, 1:2] * _load_row_tiles(buf.at[slot, 1], T)
    y_ref[...] = _ln(DN_ALPHA * _load_row_tiles(h_ref, T) + ff, g_ref[...], b_ref[...])

    @pl.when(i == n - 1)
    def _():
        drain((i + 1) % 3)
        drain((i + 2) % 3)


def _combine(dest, h1t, wts_t, ys, g2, b2, n_out, first_block, blocks_per_batch, skip_blocks):
    T = CMB_TILE
    n_rows = h1t.shape[0] // ROW_CHUNKS
    if skip_blocks:
        blk = lambda i: first_block + i + (i // blocks_per_batch + 1) * skip_blocks
    else:
        blk = lambda i: first_block + i
    full = lambda a: pl.BlockSpec(a.shape, lambda i, d: (0,) * a.ndim)
    return pl.pallas_call(
        functools.partial(_combine_kernel, blk=blk, n_rows=n_rows),
        grid_spec=pltpu.PrefetchScalarGridSpec(
            num_scalar_prefetch=1,
            grid=(n_out // T,),
            in_specs=[pl.BlockSpec((T * ROW_CHUNKS, LANES), lambda i, d: (blk(i), 0)),
                      pl.BlockSpec((T, 2), lambda i, d: (blk(i), 0)),
                      full(g2), full(b2), pl.BlockSpec(memory_space=pl.ANY)],
            out_specs=pl.BlockSpec((T, D_MODEL), lambda i, d: (i, 0)),
            scratch_shapes=[pltpu.VMEM((3, 2, T * ROW_CHUNKS, LANES), F32), pltpu.SemaphoreType.DMA((3,))]),
        out_shape=jax.ShapeDtypeStruct((n_out, D_MODEL), F32),
        compiler_params=pltpu.CompilerParams(dimension_semantics=("arbitrary",)),
        name="combine",
    )(dest, h1t, wts_t, g2, b2, ys)


def _dispatch_plan(routing, counts, max_tiles):
    T = FFN_TILE
    tiles_e = (counts + T - 1) // T
    tile_end = jnp.cumsum(tiles_e)
    n_tiles = tile_end[-1]
    start = (tile_end - tiles_e) * T
    ids, rank = routing[0:2], routing[2:4]
    onehot = (ids[..., None] == jnp.arange(N_EXPERTS, dtype=jnp.int32)).astype(jnp.int32)
    dest = (jnp.sum(onehot * start, axis=-1) + jnp.where(ids >= 0, rank, 0)).reshape(-1).astype(jnp.int32)
    experts = jnp.arange(N_EXPERTS, dtype=jnp.int32)
    tiles = jnp.arange(max_tiles, dtype=jnp.int32)
    te = jnp.sum((tiles[:, None] >= tile_end[None, :]).astype(jnp.int32), axis=1)
    te_last = jnp.max(jnp.where(counts > 0, experts, 0))
    te = jnp.where(tiles < n_tiles, te, te_last).astype(jnp.int32)
    later = jnp.where((counts > 0)[None, :] & (experts[None, :] > te[:, None]), experts[None, :], N_EXPERTS)
    nxt = jnp.min(later, axis=1)
    nxt = jnp.where(nxt < N_EXPERTS, nxt, -1).astype(jnp.int32)
    return te, nxt, n_tiles.reshape(1).astype(jnp.int32), start.astype(jnp.int32), dest


def kernel(x_prompt, x_sample, state_swa_k, state_swa_v, state_gla, meta_tokens, ln_emb_g, ln_emb_b, w_in, b_gate, attn_sink, w_alpha2, b_alpha, gla_norm_g, w_attn_br, w_gla_br, w_out, ln1_g, ln1_b, w_router_group, b_router_group, w_router_expert, b_router_expert, w_exp_gate, w_exp_up, w_exp_down, ln2_g, ln2_b):
    B, seq, _ = x_prompt.shape
    n_seq, t_s, _ = x_sample.shape
    depth = w_in.shape[0]
    assert depth == 1 and seq % ATT_BLOCK == 0 and t_s == 8 and SKIP_ROWS == ATT_BLOCK == WINDOW
    lp = SKIP_ROWS + seq
    NP, NS = B * lp, n_seq * t_s
    NR = NP + NS
    assert NP % ROW_TILE == 0 and NS % ROW_TILE == 0 and n_seq % SAMPLE_SEQS == 0
    assert seq % DISPATCH_CHUNK == 0 and NS % DISPATCH_CHUNK == 0
    l = 0
    row2 = lambda a: a.reshape(1, -1)

    head = jnp.concatenate([jnp.zeros((FRONT_PAD, D_MODEL), F32), meta_tokens], axis=0)
    pos = np.arange(NR)
    moe_valid = jnp.asarray(~((pos < NP) & (pos % lp < SKIP_ROWS)), F32).reshape(1, NR)

    wi = w_in[l]
    c_ga = sum((Q_W, KV_W, KV_W, GLA_DK, GLA_DK, GLA_DV))
    assert c_ga == W_IN_SPLIT
    w_pieces = (wi[:, :c_ga].astype(BF16), wi[:, c_ga + GLA_RANK:].astype(BF16),
                jnp.pad(wi[:, c_ga:c_ga + GLA_RANK], ((0, 0), (0, LANES - GLA_RANK))).astype(BF16))
    wa2_bf = jnp.concatenate([w_alpha2[l], jnp.zeros((LANES - GLA_RANK, GLA_DK), F32)], axis=0).astype(BF16)

    h, q, k, v, gq, gk, gv, la, gr, gate = _inproj(x_prompt, x_sample.reshape(NS, D_MODEL), head, row2(ln_emb_g),
                                                   row2(ln_emb_b), w_pieces, wa2_bf, row2(b_alpha[l]),
                                                   row2(b_gate[l]))

    sink = attn_sink[l]
    ya_p, k_p, v_p = _swa_prompt(sink, q, k, v, B, lp)
    buf_k = state_swa_k[l].reshape(n_seq, WINDOW, KV_W)
    buf_v = state_swa_v[l].reshape(n_seq, WINDOW, KV_W)
    ya_s, nk_s, nv_s = _swa_sample(sink, q, k, v, buf_k, buf_v, NP, t_s)

    og_p, s_p = _gla_prompt(gq, gk, gv, la, B, lp)
    og_s, s_s = _gla_sample(gq, gk, gv, la, state_gla[l], NP, t_s)

    wr = jnp.concatenate([w_router_expert[l], w_router_group[l],
                          jnp.zeros((D_MODEL, LANES - N_EXPERTS - N_GROUPS), F32)], axis=1)
    br = jnp.concatenate([b_router_expert[l], b_router_group[l],
                          jnp.zeros((LANES - N_EXPERTS - N_GROUPS,), F32)]).reshape(1, LANES)
    wr_hi = wr.astype(BF16)
    h1t, routing, wts, counts = _merge(h, ya_p, ya_s, og_p, og_s, gr, gate, moe_valid,
                                       row2(gla_norm_g[l]),
                                       w_attn_br[l].astype(BF16), w_gla_br[l].astype(BF16), w_out[l].astype(BF16),
                                       row2(ln1_g[l]), row2(ln1_b[l]), wr_hi, (wr - wr_hi.astype(F32)).astype(BF16), br)

    n_tok = B * seq + NS
    max_tiles = (2 * n_tok) // FFN_TILE + N_EXPERTS
    counts = counts[:, 0]
    te, nxt, n_tiles, start, dest = _dispatch_plan(routing[0:4], counts, max_tiles)
    routed = (B, seq // DISPATCH_CHUNK, SKIP_ROWS, lp, NS // DISPATCH_CHUNK, NP)
    xs = _dispatch(dest, start, counts, n_tiles, h1t, routed, max_tiles)
    ys = _ffn(te, nxt, n_tiles, xs, w_exp_gate[l], w_exp_up[l], w_exp_down[l])

    wts_t = wts[0:2].T
    g2, b2 = row2(ln2_g[l]), row2(ln2_b[l])
    skip_blocks = SKIP_ROWS // CMB_TILE
    y_p = _combine(dest, h1t, wts_t, ys, g2, b2, B * seq, 0, seq // CMB_TILE, skip_blocks)
    y_s = _combine(dest, h1t, wts_t, ys, g2, b2, NS, NP // CMB_TILE, 1, 0)

    kv_shape = (1, B, WINDOW, N_KV, HEAD_DIM)
    return (y_p.reshape(B, seq, D_MODEL), y_s.reshape(n_seq, t_s, D_MODEL),
            k_p.reshape(kv_shape), v_p.reshape(kv_shape), s_p[None],
            nk_s.reshape(1, n_seq, WINDOW, N_KV, HEAD_DIM), nv_s.reshape(1, n_seq, WINDOW, N_KV, HEAD_DIM),
            s_s[None])
```
